```python
import jax, jax.numpy as jnp
from jax import lax
import numpy as np

D_MODEL = 2048
BATCH = 8
SEQ = 2048
DEPTH = 4

D_FF = 5632
SSM_WIDTH = 1024
SSM_GROUP = 16
SSM_GROUPS = SSM_WIDTH // SSM_GROUP
SSM_STATE = 64
DT_MIN = 1e-3
DT_MAX = 1e-1
GDN_HEADS = 8
GDN_HEAD_DIM = 128
GDN_WIDTH = GDN_HEADS * GDN_HEAD_DIM
CONV_K = 4
CHUNK = 64
IN_SIZES = (SSM_WIDTH, GDN_WIDTH, GDN_WIDTH, GDN_WIDTH, GDN_WIDTH, GDN_HEADS, GDN_HEADS, D_MODEL, D_MODEL)
IN_COLS = sum(IN_SIZES)
LN_EPS = 1e-5
RMS_EPS = 1e-6
L2_EPS = 1e-6

kernel_name = 'hybrid_s5_gdn_macaron_deepnorm'


def layer_norm(x, g, b):
    xf = x.astype(jnp.float32)
    mu = jnp.mean(xf, axis=-1, keepdims=True)
    var = jnp.mean(jnp.square(xf - mu), axis=-1, keepdims=True)
    y = (xf - mu) * lax.rsqrt(var + LN_EPS) * g.astype(jnp.float32) + b.astype(jnp.float32)
    return y.astype(x.dtype)


def swiglu_ffn(x, w_gu, w_down):
    gate, up = jnp.split(x @ w_gu, 2, axis=-1)
    return (jax.nn.silu(gate) * up) @ w_down


def cmul(ar, ai, br, bi):
    return ar * br - ai * bi, ar * bi + ai * br


def s5_branch(u, a_re, a_im, log_dt, b_re, b_im, c_re, c_im, d_skip, glu_w, glu_b):
    f32 = jnp.float32
    bsz, seq, _ = u.shape
    ug = u.astype(f32).reshape(bsz, seq, SSM_GROUPS, SSM_GROUP)
    dt = jnp.exp(log_dt.astype(f32))[:, None]
    lr, li = a_re.astype(f32), a_im.astype(f32)
    mag = jnp.exp(lr * dt)
    lbar_r, lbar_i = mag * jnp.cos(li * dt), mag * jnp.sin(li * dt)
    den = lr * lr + li * li
    zr, zi = cmul(lbar_r - 1.0, lbar_i, lr / den, -li / den)
    bbar_r, bbar_i = cmul(zr[:, :, None], zi[:, :, None], b_re.astype(f32), b_im.astype(f32))
    bu_r = jnp.einsum('blgh,gph->blgp', ug, bbar_r)
    bu_i = jnp.einsum('blgh,gph->blgp', ug, bbar_i)
    a_r = jnp.broadcast_to(lbar_r, (1, seq) + lbar_r.shape)
    a_i = jnp.broadcast_to(lbar_i, (1, seq) + lbar_i.shape)

    def combine(e_early, e_late):
        a1r, a1i, b1r, b1i = e_early
        a2r, a2i, b2r, b2i = e_late
        ar, ai = cmul(a2r, a2i, a1r, a1i)
        br, bi = cmul(a2r, a2i, b1r, b1i)
        return (ar, ai, br + b2r, bi + b2i)

    _, _, s_r, s_i = lax.associative_scan(combine, (a_r, a_i, bu_r, bu_i), axis=1)
    y = (jnp.einsum('ghp,blgp->blgh', c_re.astype(f32), s_r)
         - jnp.einsum('ghp,blgp->blgh', c_im.astype(f32), s_i)
         + d_skip.astype(f32) * ug)
    y = y.reshape(bsz, seq, SSM_WIDTH).astype(u.dtype)
    y = jax.nn.gelu(y)
    return y * jax.nn.sigmoid(y @ glu_w + glu_b)


def causal_dwconv(x, w):
    return lax.conv_general_dilated(
        x, w[:, None, :], window_strides=(1,), padding=[(CONV_K - 1, 0)],
        dimension_numbers=('NWC', 'WIO', 'NWC'), feature_group_count=x.shape[-1])


def l2norm(t):
    return t * lax.rsqrt(jnp.sum(t * t, axis=-1, keepdims=True) + L2_EPS)


def gated_deltanet_branch(q, k, v, z, beta_logit, a_in, conv_w, a_log, dt_bias, norm_w):
    f32 = jnp.float32
    bsz, seq, _ = q.shape
    n_chunks = seq // CHUNK
    qkv = jax.nn.silu(causal_dwconv(jnp.concatenate([q, k, v], axis=-1), conv_w)).astype(f32)
    q, k, v = jnp.split(qkv, 3, axis=-1)

    def heads(t):
        return t.reshape(bsz, n_chunks, CHUNK, GDN_HEADS, GDN_HEAD_DIM).transpose(0, 3, 1, 2, 4)

    def head_scalars(t):
        return t.reshape(bsz, n_chunks, CHUNK, GDN_HEADS).transpose(0, 3, 1, 2)

    q = l2norm(heads(q)) * (GDN_HEAD_DIM ** -0.5)
    k = l2norm(heads(k))
    v = heads(v)
    beta = head_scalars(jax.nn.sigmoid(beta_logit.astype(f32)))
    g = -jnp.exp(a_log.astype(f32)) * jax.nn.softplus(a_in.astype(f32) + dt_bias.astype(f32))
    gcum = jnp.cumsum(head_scalars(g), axis=-1)
    idx = jnp.arange(CHUNK)
    causal = idx[:, None] >= idx[None, :]
    strict = idx[:, None] > idx[None, :]
    decay = jnp.exp(jnp.where(causal, gcum[..., :, None] - gcum[..., None, :], -jnp.inf))
    k_beta = k * beta[..., None]
    lower = jnp.where(strict, jnp.einsum('bhncd,bhnsd->bhncs', k_beta, k) * decay, 0.0)
    rhs = jnp.concatenate([v * beta[..., None], k_beta * jnp.exp(gcum)[..., None]], axis=-1)
    sol = lax.linalg.triangular_solve(lower + jnp.eye(CHUNK, dtype=f32), rhs,
                                      left_side=True, lower=True, unit_diagonal=True)
    u_val, w_key = jnp.split(sol, 2, axis=-1)
    attn_intra = jnp.einsum('bhncd,bhnsd->bhncs', q, k) * decay
    q_dec = q * jnp.exp(gcum)[..., None]
    k_dec = k * jnp.exp(gcum[..., -1:] - gcum)[..., None]
    g_last = jnp.exp(gcum[..., -1])

    def chunk_step(state, xs):
        u_c, w_c, a_c, qd_c, kd_c, gl_c = xs
        v_new = u_c - jnp.einsum('bhcd,bhde->bhce', w_c, state)
        out = (jnp.einsum('bhcd,bhde->bhce', qd_c, state)
               + jnp.einsum('bhcs,bhse->bhce', a_c, v_new))
        state = state * gl_c[..., None, None] + jnp.einsum('bhcd,bhce->bhde', kd_c, v_new)
        return state, out

    xs = tuple(jnp.moveaxis(t, 2, 0) for t in (u_val, w_key, attn_intra, q_dec, k_dec, g_last))
    state0 = jnp.zeros((bsz, GDN_HEADS, GDN_HEAD_DIM, GDN_HEAD_DIM), f32)
    _, o = lax.scan(chunk_step, state0, xs)
    o = o.transpose(1, 0, 3, 2, 4).reshape(bsz, seq, GDN_HEADS, GDN_HEAD_DIM)
    o = o * lax.rsqrt(jnp.mean(o * o, axis=-1, keepdims=True) + RMS_EPS) * norm_w.astype(f32)
    o = o * jax.nn.silu(z.astype(f32).reshape(bsz, seq, GDN_HEADS, GDN_HEAD_DIM))
    return o.reshape(bsz, seq, GDN_WIDTH).astype(q.dtype if q.dtype != f32 else z.dtype)


def hybrid_mixer(h, w_in, conv_w, ssm_a_re, ssm_a_im, ssm_log_dt, ssm_b_re, ssm_b_im, ssm_c_re,
                 ssm_c_im, ssm_d, glu_w, glu_b, gdn_a_log, gdn_dt_bias, gdn_norm_w,
                 w_br_ssm, w_br_gdn, w_out):
    offsets = np.cumsum(IN_SIZES)[:-1].tolist()
    u, q, k, v, z, beta_logit, a_in, gate_ssm, gate_gdn = jnp.split(h @ w_in, offsets, axis=-1)
    y_ssm = s5_branch(u, ssm_a_re, ssm_a_im, ssm_log_dt, ssm_b_re, ssm_b_im,
                      ssm_c_re, ssm_c_im, ssm_d, glu_w, glu_b)
    y_gdn = gated_deltanet_branch(q, k, v, z, beta_logit, a_in, conv_w,
                                  gdn_a_log, gdn_dt_bias, gdn_norm_w)
    merged = (jax.nn.sigmoid(gate_ssm) * (y_ssm @ w_br_ssm)
              + jax.nn.sigmoid(gate_gdn) * (y_gdn @ w_br_gdn))
    return merged @ w_out


def _fwd_setup_inputs(seed: int = 0) -> dict:
    key = jax.random.key(seed)
    ks = jax.random.split(key, 32)
    f32 = jnp.float32
    L = DEPTH
    dn_beta = (8.0 * DEPTH) ** -0.25

    def nrm(k, shape, scale):
        return scale * jax.random.normal(k, shape, f32)

    lo, hi = float(np.log(DT_MIN)), float(np.log(DT_MAX))
    gdn_dt = jnp.exp(jax.random.uniform(ks[20], (L, GDN_HEADS), f32, lo, hi))
    return {
        'x': nrm(ks[0], (BATCH, SEQ, D_MODEL), 1.0),
        'ffn1_w_gu': nrm(ks[1], (L, D_MODEL, 2 * D_FF), D_MODEL ** -0.5),
        'ffn1_w_down': nrm(ks[2], (L, D_FF, D_MODEL), dn_beta * D_FF ** -0.5),
        'ln1_g': 1.0 + nrm(ks[3], (L, D_MODEL), 0.02),
        'ln1_b': nrm(ks[4], (L, D_MODEL), 0.02),
        'w_in': nrm(ks[5], (L, D_MODEL, IN_COLS), D_MODEL ** -0.5),
        'conv_w': nrm(ks[6], (L, CONV_K, 3 * GDN_WIDTH), CONV_K ** -0.5),
        'ssm_a_re': -0.5 + nrm(ks[7], (L, SSM_GROUPS, SSM_STATE), 0.02),
        'ssm_a_im': jnp.pi * jnp.arange(SSM_STATE, dtype=f32) + nrm(ks[8], (L, SSM_GROUPS, SSM_STATE), 0.02),
        'ssm_log_dt': jax.random.uniform(ks[9], (L, SSM_GROUPS), f32, lo, hi),
        'ssm_b_re': nrm(ks[10], (L, SSM_GROUPS, SSM_STATE, SSM_GROUP), (2 * SSM_GROUP) ** -0.5),
        'ssm_b_im': nrm(ks[11], (L, SSM_GROUPS, SSM_STATE, SSM_GROUP), (2 * SSM_GROUP) ** -0.5),
        'ssm_c_re': nrm(ks[12], (L, SSM_GROUPS, SSM_GROUP, SSM_STATE), (2 * SSM_STATE) ** -0.5),
        'ssm_c_im': nrm(ks[13], (L, SSM_GROUPS, SSM_GROUP, SSM_STATE), (2 * SSM_STATE) ** -0.5),
        'ssm_d': nrm(ks[14], (L, SSM_GROUPS, SSM_GROUP), 1.0),
        'glu_w': nrm(ks[15], (L, SSM_WIDTH, SSM_WIDTH), SSM_WIDTH ** -0.5),
        'glu_b': nrm(ks[16], (L, SSM_WIDTH), 0.02),
        'gdn_a_log': jnp.log(jax.random.uniform(ks[17], (L, GDN_HEADS), f32, 1.0, 16.0)),
        'gdn_dt_bias': gdn_dt + jnp.log(-jnp.expm1(-gdn_dt)),
        'gdn_norm_w': 1.0 + nrm(ks[18], (L, GDN_HEAD_DIM), 0.02),
        'w_br_ssm': nrm(ks[19], (L, SSM_WIDTH, D_MODEL), SSM_WIDTH ** -0.5),
        'w_br_gdn': nrm(ks[21], (L, GDN_WIDTH, D_MODEL), GDN_WIDTH ** -0.5),
        'w_out': nrm(ks[22], (L, D_MODEL, D_MODEL), dn_beta * D_MODEL ** -0.5),
        'ln2_g': 1.0 + nrm(ks[23], (L, D_MODEL), 0.02),
        'ln2_b': nrm(ks[24], (L, D_MODEL), 0.02),
        'ffn2_w_gu': nrm(ks[25], (L, D_MODEL, 2 * D_FF), D_MODEL ** -0.5),
        'ffn2_w_down': nrm(ks[26], (L, D_FF, D_MODEL), dn_beta * D_FF ** -0.5),
        'ln3_g': 1.0 + nrm(ks[27], (L, D_MODEL), 0.02),
        'ln3_b': nrm(ks[28], (L, D_MODEL), 0.02),
    }


def _fwd_reference(x, ffn1_w_gu, ffn1_w_down, ln1_g, ln1_b, w_in, conv_w, ssm_a_re, ssm_a_im,
              ssm_log_dt, ssm_b_re, ssm_b_im, ssm_c_re, ssm_c_im, ssm_d, glu_w, glu_b,
              gdn_a_log, gdn_dt_bias, gdn_norm_w, w_br_ssm, w_br_gdn, w_out, ln2_g, ln2_b,
              ffn2_w_gu, ffn2_w_down, ln3_g, ln3_b):
    alpha = (2.0 * DEPTH) ** 0.25
    for l in range(DEPTH):
        x = layer_norm(alpha * x + 0.5 * swiglu_ffn(x, ffn1_w_gu[l], ffn1_w_down[l]), ln1_g[l], ln1_b[l])
        mix = hybrid_mixer(x, w_in[l], conv_w[l], ssm_a_re[l], ssm_a_im[l], ssm_log_dt[l],
                           ssm_b_re[l], ssm_b_im[l], ssm_c_re[l], ssm_c_im[l], ssm_d[l],
                           glu_w[l], glu_b[l], gdn_a_log[l], gdn_dt_bias[l], gdn_norm_w[l],
                           w_br_ssm[l], w_br_gdn[l], w_out[l])
        x = layer_norm(alpha * x + mix, ln2_g[l], ln2_b[l])
        x = layer_norm(alpha * x + 0.5 * swiglu_ffn(x, ffn2_w_gu[l], ffn2_w_down[l]), ln3_g[l], ln3_b[l])
    return x


import jax as _jax
import jax.numpy as _jnp

TWIN_FORMAT = 'train_step'
FWD_PARAMS = ['x', 'ffn1_w_gu', 'ffn1_w_down', 'ln1_g', 'ln1_b', 'w_in', 'conv_w', 'ssm_a_re', 'ssm_a_im', 'ssm_log_dt', 'ssm_b_re', 'ssm_b_im', 'ssm_c_re', 'ssm_c_im', 'ssm_d', 'glu_w', 'glu_b', 'gdn_a_log', 'gdn_dt_bias', 'gdn_norm_w', 'w_br_ssm', 'w_br_gdn', 'w_out', 'ln2_g', 'ln2_b', 'ffn2_w_gu', 'ffn2_w_down', 'ln3_g', 'ln3_b']
TWIN_WEIGHTS = ['ffn1_w_gu', 'ffn1_w_down', 'ln1_g', 'ln1_b', 'w_in', 'conv_w', 'ssm_a_re', 'ssm_a_im', 'ssm_log_dt', 'ssm_b_re', 'ssm_b_im', 'ssm_c_re', 'ssm_c_im', 'ssm_d', 'glu_w', 'glu_b', 'gdn_a_log', 'gdn_dt_bias', 'gdn_norm_w', 'w_br_ssm', 'w_br_gdn', 'w_out', 'ln2_g', 'ln2_b', 'ffn2_w_gu', 'ffn2_w_down', 'ln3_g', 'ln3_b']
TWIN_DIFF_INPUT = 'x'
TWIN_INPUTS = ['x', 'ffn1_w_gu', 'ffn1_w_down', 'ln1_g', 'ln1_b', 'w_in', 'conv_w', 'ssm_a_re', 'ssm_a_im', 'ssm_log_dt', 'ssm_b_re', 'ssm_b_im', 'ssm_c_re', 'ssm_c_im', 'ssm_d', 'glu_w', 'glu_b', 'gdn_a_log', 'gdn_dt_bias', 'gdn_norm_w', 'w_br_ssm', 'w_br_gdn', 'w_out', 'ln2_g', 'ln2_b', 'ffn2_w_gu', 'ffn2_w_down', 'ln3_g', 'ln3_b', 'loss_target', 'm_ffn1_w_gu', 'm_ffn1_w_down', 'm_ln1_g', 'm_ln1_b', 'm_w_in', 'm_conv_w', 'm_ssm_a_re', 'm_ssm_a_im', 'm_ssm_log_dt', 'm_ssm_b_re', 'm_ssm_b_im', 'm_ssm_c_re', 'm_ssm_c_im', 'm_ssm_d', 'm_glu_w', 'm_glu_b', 'm_gdn_a_log', 'm_gdn_dt_bias', 'm_gdn_norm_w', 'm_w_br_ssm', 'm_w_br_gdn', 'm_w_out', 'm_ln2_g', 'm_ln2_b', 'm_ffn2_w_gu', 'm_ffn2_w_down', 'm_ln3_g', 'm_ln3_b', 'v_ffn1_w_gu', 'v_ffn1_w_down', 'v_ln1_g', 'v_ln1_b', 'v_w_in', 'v_conv_w', 'v_ssm_a_re', 'v_ssm_a_im', 'v_ssm_log_dt', 'v_ssm_b_re', 'v_ssm_b_im', 'v_ssm_c_re', 'v_ssm_c_im', 'v_ssm_d', 'v_glu_w', 'v_glu_b', 'v_gdn_a_log', 'v_gdn_dt_bias', 'v_gdn_norm_w', 'v_w_br_ssm', 'v_w_br_gdn', 'v_w_out', 'v_ln2_g', 'v_ln2_b', 'v_ffn2_w_gu', 'v_ffn2_w_down', 'v_ln3_g', 'v_ln3_b']
TWIN_OUTPUTS = ['loss', 'grad_x', 'grad_ffn1_w_gu', 'grad_ffn1_w_down', 'grad_ln1_g', 'grad_ln1_b', 'grad_w_in', 'grad_conv_w', 'grad_ssm_a_re', 'grad_ssm_a_im', 'grad_ssm_log_dt', 'grad_ssm_b_re', 'grad_ssm_b_im', 'grad_ssm_c_re', 'grad_ssm_c_im', 'grad_ssm_d', 'grad_glu_w', 'grad_glu_b', 'grad_gdn_a_log', 'grad_gdn_dt_bias', 'grad_gdn_norm_w', 'grad_w_br_ssm', 'grad_w_br_gdn', 'grad_w_out', 'grad_ln2_g', 'grad_ln2_b', 'grad_ffn2_w_gu', 'grad_ffn2_w_down', 'grad_ln3_g', 'grad_ln3_b', 'delta_ffn1_w_gu', 'delta_ffn1_w_down', 'delta_ln1_g', 'delta_ln1_b', 'delta_w_in', 'delta_conv_w', 'delta_ssm_a_re', 'delta_ssm_a_im', 'delta_ssm_log_dt', 'delta_ssm_b_re', 'delta_ssm_b_im', 'delta_ssm_c_re', 'delta_ssm_c_im', 'delta_ssm_d', 'delta_glu_w', 'delta_glu_b', 'delta_gdn_a_log', 'delta_gdn_dt_bias', 'delta_gdn_norm_w', 'delta_w_br_ssm', 'delta_w_br_gdn', 'delta_w_out', 'delta_ln2_g', 'delta_ln2_b', 'delta_ffn2_w_gu', 'delta_ffn2_w_down', 'delta_ln3_g', 'delta_ln3_b', 'new_m_ffn1_w_gu', 'new_m_ffn1_w_down', 'new_m_ln1_g', 'new_m_ln1_b', 'new_m_w_in', 'new_m_conv_w', 'new_m_ssm_a_re', 'new_m_ssm_a_im', 'new_m_ssm_log_dt', 'new_m_ssm_b_re', 'new_m_ssm_b_im', 'new_m_ssm_c_re', 'new_m_ssm_c_im', 'new_m_ssm_d', 'new_m_glu_w', 'new_m_glu_b', 'new_m_gdn_a_log', 'new_m_gdn_dt_bias', 'new_m_gdn_norm_w', 'new_m_w_br_ssm', 'new_m_w_br_gdn', 'new_m_w_out', 'new_m_ln2_g', 'new_m_ln2_b', 'new_m_ffn2_w_gu', 'new_m_ffn2_w_down', 'new_m_ln3_g', 'new_m_ln3_b', 'new_v_ffn1_w_gu', 'new_v_ffn1_w_down', 'new_v_ln1_g', 'new_v_ln1_b', 'new_v_w_in', 'new_v_conv_w', 'new_v_ssm_a_re', 'new_v_ssm_a_im', 'new_v_ssm_log_dt', 'new_v_ssm_b_re', 'new_v_ssm_b_im', 'new_v_ssm_c_re', 'new_v_ssm_c_im', 'new_v_ssm_d', 'new_v_glu_w', 'new_v_glu_b', 'new_v_gdn_a_log', 'new_v_gdn_dt_bias', 'new_v_gdn_norm_w', 'new_v_w_br_ssm', 'new_v_w_br_gdn', 'new_v_w_out', 'new_v_ln2_g', 'new_v_ln2_b', 'new_v_ffn2_w_gu', 'new_v_ffn2_w_down', 'new_v_ln3_g', 'new_v_ln3_b']
TWIN_LEAF_KINDS = {'loss': 'loss', 'grad_x': 'grad_x', 'grad_ffn1_w_gu': 'grad_w', 'grad_ffn1_w_down': 'grad_w', 'grad_ln1_g': 'grad_w', 'grad_ln1_b': 'grad_w', 'grad_w_in': 'grad_w', 'grad_conv_w': 'grad_w', 'grad_ssm_a_re': 'grad_w', 'grad_ssm_a_im': 'grad_w', 'grad_ssm_log_dt': 'grad_w', 'grad_ssm_b_re': 'grad_w', 'grad_ssm_b_im': 'grad_w', 'grad_ssm_c_re': 'grad_w', 'grad_ssm_c_im': 'grad_w', 'grad_ssm_d': 'grad_w', 'grad_glu_w': 'grad_w', 'grad_glu_b': 'grad_w', 'grad_gdn_a_log': 'grad_w', 'grad_gdn_dt_bias': 'grad_w', 'grad_gdn_norm_w': 'grad_w', 'grad_w_br_ssm': 'grad_w', 'grad_w_br_gdn': 'grad_w', 'grad_w_out': 'grad_w', 'grad_ln2_g': 'grad_w', 'grad_ln2_b': 'grad_w', 'grad_ffn2_w_gu': 'grad_w', 'grad_ffn2_w_down': 'grad_w', 'grad_ln3_g': 'grad_w', 'grad_ln3_b': 'grad_w', 'delta_ffn1_w_gu': 'delta_w', 'delta_ffn1_w_down': 'delta_w', 'delta_ln1_g': 'delta_w', 'delta_ln1_b': 'delta_w', 'delta_w_in': 'delta_w', 'delta_conv_w': 'delta_w', 'delta_ssm_a_re': 'delta_w', 'delta_ssm_a_im': 'delta_w', 'delta_ssm_log_dt': 'delta_w', 'delta_ssm_b_re': 'delta_w', 'delta_ssm_b_im': 'delta_w', 'delta_ssm_c_re': 'delta_w', 'delta_ssm_c_im': 'delta_w', 'delta_ssm_d': 'delta_w', 'delta_glu_w': 'delta_w', 'delta_glu_b': 'delta_w', 'delta_gdn_a_log': 'delta_w', 'delta_gdn_dt_bias': 'delta_w', 'delta_gdn_norm_w': 'delta_w', 'delta_w_br_ssm': 'delta_w', 'delta_w_br_gdn': 'delta_w', 'delta_w_out': 'delta_w', 'delta_ln2_g': 'delta_w', 'delta_ln2_b': 'delta_w', 'delta_ffn2_w_gu': 'delta_w', 'delta_ffn2_w_down': 'delta_w', 'delta_ln3_g': 'delta_w', 'delta_ln3_b': 'delta_w', 'new_m_ffn1_w_gu': 'new_m', 'new_m_ffn1_w_down': 'new_m', 'new_m_ln1_g': 'new_m', 'new_m_ln1_b': 'new_m', 'new_m_w_in': 'new_m', 'new_m_conv_w': 'new_m', 'new_m_ssm_a_re': 'new_m', 'new_m_ssm_a_im': 'new_m', 'new_m_ssm_log_dt': 'new_m', 'new_m_ssm_b_re': 'new_m', 'new_m_ssm_b_im': 'new_m', 'new_m_ssm_c_re': 'new_m', 'new_m_ssm_c_im': 'new_m', 'new_m_ssm_d': 'new_m', 'new_m_glu_w': 'new_m', 'new_m_glu_b': 'new_m', 'new_m_gdn_a_log': 'new_m', 'new_m_gdn_dt_bias': 'new_m', 'new_m_gdn_norm_w': 'new_m', 'new_m_w_br_ssm': 'new_m', 'new_m_w_br_gdn': 'new_m', 'new_m_w_out': 'new_m', 'new_m_ln2_g': 'new_m', 'new_m_ln2_b': 'new_m', 'new_m_ffn2_w_gu': 'new_m', 'new_m_ffn2_w_down': 'new_m', 'new_m_ln3_g': 'new_m', 'new_m_ln3_b': 'new_m', 'new_v_ffn1_w_gu': 'new_v', 'new_v_ffn1_w_down': 'new_v', 'new_v_ln1_g': 'new_v', 'new_v_ln1_b': 'new_v', 'new_v_w_in': 'new_v', 'new_v_conv_w': 'new_v', 'new_v_ssm_a_re': 'new_v', 'new_v_ssm_a_im': 'new_v', 'new_v_ssm_log_dt': 'new_v', 'new_v_ssm_b_re': 'new_v', 'new_v_ssm_b_im': 'new_v', 'new_v_ssm_c_re': 'new_v', 'new_v_ssm_c_im': 'new_v', 'new_v_ssm_d': 'new_v', 'new_v_glu_w': 'new_v', 'new_v_glu_b': 'new_v', 'new_v_gdn_a_log': 'new_v', 'new_v_gdn_dt_bias': 'new_v', 'new_v_gdn_norm_w': 'new_v', 'new_v_w_br_ssm': 'new_v', 'new_v_w_br_gdn': 'new_v', 'new_v_w_out': 'new_v', 'new_v_ln2_g': 'new_v', 'new_v_ln2_b': 'new_v', 'new_v_ffn2_w_gu': 'new_v', 'new_v_ffn2_w_down': 'new_v', 'new_v_ln3_g': 'new_v', 'new_v_ln3_b': 'new_v'}


def _forward(args):
    return _fwd_reference(*[args[k] for k in FWD_PARAMS])


def _output_shape():
    out = _jax.eval_shape(lambda: _forward(_fwd_setup_inputs(0)))
    return out.shape, out.dtype

N_MICROBATCH = 1
ADAM_LR = 0.001
ADAM_B1 = 0.9
ADAM_B2 = 0.999
ADAM_EPS = 1e-08
ADAM_WD = 0.01
ADAM_STEP = 10
PER_EXAMPLE_BATCH_AXIS = {'x': 0, 'loss_target': 0}
SHARED_INPUTS = []
_WEIGHT_DTYPES = {'ffn1_w_gu': _jnp.float32, 'ffn1_w_down': _jnp.float32, 'ln1_g': _jnp.float32, 'ln1_b': _jnp.float32, 'w_in': _jnp.float32, 'conv_w': _jnp.float32, 'ssm_a_re': _jnp.float32, 'ssm_a_im': _jnp.float32, 'ssm_log_dt': _jnp.float32, 'ssm_b_re': _jnp.float32, 'ssm_b_im': _jnp.float32, 'ssm_c_re': _jnp.float32, 'ssm_c_im': _jnp.float32, 'ssm_d': _jnp.float32, 'glu_w': _jnp.float32, 'glu_b': _jnp.float32, 'gdn_a_log': _jnp.float32, 'gdn_dt_bias': _jnp.float32, 'gdn_norm_w': _jnp.float32, 'w_br_ssm': _jnp.float32, 'w_br_gdn': _jnp.float32, 'w_out': _jnp.float32, 'ln2_g': _jnp.float32, 'ln2_b': _jnp.float32, 'ffn2_w_gu': _jnp.float32, 'ffn2_w_down': _jnp.float32, 'ln3_g': _jnp.float32, 'ln3_b': _jnp.float32}
MOMENT_SCALE = {'ffn1_w_gu': 2.932074e-03, 'ffn1_w_down': 1.137600e-02, 'ln1_g': 2.824031e-01, 'ln1_b': 1.437236e-01, 'w_in': 4.445894e-03, 'conv_w': 5.567317e-03, 'ssm_a_re': 2.265713e-04, 'ssm_a_im': 2.279933e-04, 'ssm_log_dt': 1.890262e-01, 'ssm_b_re': 1.477342e-04, 'ssm_b_im': 1.468912e-04, 'ssm_c_re': 2.971540e-04, 'ssm_c_im': 2.962034e-04, 'ssm_d': 5.839035e-03, 'glu_w': 1.398035e-03, 'glu_b': 2.540311e-03, 'gdn_a_log': 3.384079e-02, 'gdn_dt_bias': 3.283104e-02, 'gdn_norm_w': 2.011660e-02, 'w_br_ssm': 3.845554e-03, 'w_br_gdn': 5.281968e-03, 'w_out': 1.551879e-02, 'ln2_g': 2.854105e-01, 'ln2_b': 1.440014e-01, 'ffn2_w_gu': 2.903420e-03, 'ffn2_w_down': 1.126983e-02, 'ln3_g': 4.039306e+00, 'ln3_b': 3.412032e-01}


def _to_microbatches(a, axis):
    t = _jnp.moveaxis(a, axis, 0)
    t = t.reshape((N_MICROBATCH, t.shape[0] // N_MICROBATCH) + t.shape[1:])
    return _jnp.moveaxis(t, 1, axis + 1)


def setup_inputs(seed: int = 0) -> dict:
    inp = _fwd_setup_inputs(seed)
    key = _jax.random.fold_in(_jax.random.key(seed), 7919)
    shape, _ = _output_shape()
    out = dict(inp)
    out["loss_target"] = _jax.random.normal(_jax.random.fold_in(key, 0), shape, _jnp.float32)
    for i, name in enumerate(TWIN_WEIGHTS):
        w = inp[name].astype(_jnp.float32)
        if MOMENT_SCALE is None:
            s = _jnp.sqrt(_jnp.mean(_jnp.square(w)) + 1e-30)
        else:
            s = MOMENT_SCALE[name]
        km, kv = _jax.random.split(_jax.random.fold_in(key, i + 1))
        out[name] = w
        out["m_" + name] = s * _jax.random.normal(km, w.shape, _jnp.float32)
        out["v_" + name] = (s * s) * _jax.random.uniform(kv, w.shape, _jnp.float32, 0.5, 1.5)
    if N_MICROBATCH > 1:
        for name, axis in PER_EXAMPLE_BATCH_AXIS.items():
            out[name] = _to_microbatches(out[name], axis)
    return {'x': out['x'], 'ffn1_w_gu': out['ffn1_w_gu'], 'ffn1_w_down': out['ffn1_w_down'], 'ln1_g': out['ln1_g'], 'ln1_b': out['ln1_b'], 'w_in': out['w_in'], 'conv_w': out['conv_w'], 'ssm_a_re': out['ssm_a_re'], 'ssm_a_im': out['ssm_a_im'], 'ssm_log_dt': out['ssm_log_dt'], 'ssm_b_re': out['ssm_b_re'], 'ssm_b_im': out['ssm_b_im'], 'ssm_c_re': out['ssm_c_re'], 'ssm_c_im': out['ssm_c_im'], 'ssm_d': out['ssm_d'], 'glu_w': out['glu_w'], 'glu_b': out['glu_b'], 'gdn_a_log': out['gdn_a_log'], 'gdn_dt_bias': out['gdn_dt_bias'], 'gdn_norm_w': out['gdn_norm_w'], 'w_br_ssm': out['w_br_ssm'], 'w_br_gdn': out['w_br_gdn'], 'w_out': out['w_out'], 'ln2_g': out['ln2_g'], 'ln2_b': out['ln2_b'], 'ffn2_w_gu': out['ffn2_w_gu'], 'ffn2_w_down': out['ffn2_w_down'], 'ln3_g': out['ln3_g'], 'ln3_b': out['ln3_b'], 'loss_target': out['loss_target'], 'm_ffn1_w_gu': out['m_ffn1_w_gu'], 'm_ffn1_w_down': out['m_ffn1_w_down'], 'm_ln1_g': out['m_ln1_g'], 'm_ln1_b': out['m_ln1_b'], 'm_w_in': out['m_w_in'], 'm_conv_w': out['m_conv_w'], 'm_ssm_a_re': out['m_ssm_a_re'], 'm_ssm_a_im': out['m_ssm_a_im'], 'm_ssm_log_dt': out['m_ssm_log_dt'], 'm_ssm_b_re': out['m_ssm_b_re'], 'm_ssm_b_im': out['m_ssm_b_im'], 'm_ssm_c_re': out['m_ssm_c_re'], 'm_ssm_c_im': out['m_ssm_c_im'], 'm_ssm_d': out['m_ssm_d'], 'm_glu_w': out['m_glu_w'], 'm_glu_b': out['m_glu_b'], 'm_gdn_a_log': out['m_gdn_a_log'], 'm_gdn_dt_bias': out['m_gdn_dt_bias'], 'm_gdn_norm_w': out['m_gdn_norm_w'], 'm_w_br_ssm': out['m_w_br_ssm'], 'm_w_br_gdn': out['m_w_br_gdn'], 'm_w_out': out['m_w_out'], 'm_ln2_g': out['m_ln2_g'], 'm_ln2_b': out['m_ln2_b'], 'm_ffn2_w_gu': out['m_ffn2_w_gu'], 'm_ffn2_w_down': out['m_ffn2_w_down'], 'm_ln3_g': out['m_ln3_g'], 'm_ln3_b': out['m_ln3_b'], 'v_ffn1_w_gu': out['v_ffn1_w_gu'], 'v_ffn1_w_down': out['v_ffn1_w_down'], 'v_ln1_g': out['v_ln1_g'], 'v_ln1_b': out['v_ln1_b'], 'v_w_in': out['v_w_in'], 'v_conv_w': out['v_conv_w'], 'v_ssm_a_re': out['v_ssm_a_re'], 'v_ssm_a_im': out['v_ssm_a_im'], 'v_ssm_log_dt': out['v_ssm_log_dt'], 'v_ssm_b_re': out['v_ssm_b_re'], 'v_ssm_b_im': out['v_ssm_b_im'], 'v_ssm_c_re': out['v_ssm_c_re'], 'v_ssm_c_im': out['v_ssm_c_im'], 'v_ssm_d': out['v_ssm_d'], 'v_glu_w': out['v_glu_w'], 'v_glu_b': out['v_glu_b'], 'v_gdn_a_log': out['v_gdn_a_log'], 'v_gdn_dt_bias': out['v_gdn_dt_bias'], 'v_gdn_norm_w': out['v_gdn_norm_w'], 'v_w_br_ssm': out['v_w_br_ssm'], 'v_w_br_gdn': out['v_w_br_gdn'], 'v_w_out': out['v_w_out'], 'v_ln2_g': out['v_ln2_g'], 'v_ln2_b': out['v_ln2_b'], 'v_ffn2_w_gu': out['v_ffn2_w_gu'], 'v_ffn2_w_down': out['v_ffn2_w_down'], 'v_ln3_g': out['v_ln3_g'], 'v_ln3_b': out['v_ln3_b']}


def _loss(weights, diff, rest, loss_target):
    with _jax.named_scope("forward"):
        args = {**rest, TWIN_DIFF_INPUT: diff, **{k: w.astype(_WEIGHT_DTYPES[k]) for k, w in weights.items()}}
        y = _forward(args)
    with _jax.named_scope("loss_head"):
        err = _jnp.square(y.astype(_jnp.float32) - loss_target)
        return 0.5 * _jnp.sum(_jnp.mean(err, axis=-1)) if err.ndim else 0.5 * err


def _adamw(w, g, m, v):
    m = ADAM_B1 * m + (1.0 - ADAM_B1) * g
    v = ADAM_B2 * v + (1.0 - ADAM_B2) * _jnp.square(g)
    m_hat = m / (1.0 - ADAM_B1 ** ADAM_STEP)
    v_hat = v / (1.0 - ADAM_B2 ** ADAM_STEP)
    delta = -ADAM_LR * (m_hat / (_jnp.sqrt(v_hat) + ADAM_EPS) + ADAM_WD * w)
    return delta, m, v


def reference(x, ffn1_w_gu, ffn1_w_down, ln1_g, ln1_b, w_in, conv_w, ssm_a_re, ssm_a_im, ssm_log_dt, ssm_b_re, ssm_b_im, ssm_c_re, ssm_c_im, ssm_d, glu_w, glu_b, gdn_a_log, gdn_dt_bias, gdn_norm_w, w_br_ssm, w_br_gdn, w_out, ln2_g, ln2_b, ffn2_w_gu, ffn2_w_down, ln3_g, ln3_b, loss_target, m_ffn1_w_gu, m_ffn1_w_down, m_ln1_g, m_ln1_b, m_w_in, m_conv_w, m_ssm_a_re, m_ssm_a_im, m_ssm_log_dt, m_ssm_b_re, m_ssm_b_im, m_ssm_c_re, m_ssm_c_im, m_ssm_d, m_glu_w, m_glu_b, m_gdn_a_log, m_gdn_dt_bias, m_gdn_norm_w, m_w_br_ssm, m_w_br_gdn, m_w_out, m_ln2_g, m_ln2_b, m_ffn2_w_gu, m_ffn2_w_down, m_ln3_g, m_ln3_b, v_ffn1_w_gu, v_ffn1_w_down, v_ln1_g, v_ln1_b, v_w_in, v_conv_w, v_ssm_a_re, v_ssm_a_im, v_ssm_log_dt, v_ssm_b_re, v_ssm_b_im, v_ssm_c_re, v_ssm_c_im, v_ssm_d, v_glu_w, v_glu_b, v_gdn_a_log, v_gdn_dt_bias, v_gdn_norm_w, v_w_br_ssm, v_w_br_gdn, v_w_out, v_ln2_g, v_ln2_b, v_ffn2_w_gu, v_ffn2_w_down, v_ln3_g, v_ln3_b):
    given = dict(x=x, ffn1_w_gu=ffn1_w_gu, ffn1_w_down=ffn1_w_down, ln1_g=ln1_g, ln1_b=ln1_b, w_in=w_in, conv_w=conv_w, ssm_a_re=ssm_a_re, ssm_a_im=ssm_a_im, ssm_log_dt=ssm_log_dt, ssm_b_re=ssm_b_re, ssm_b_im=ssm_b_im, ssm_c_re=ssm_c_re, ssm_c_im=ssm_c_im, ssm_d=ssm_d, glu_w=glu_w, glu_b=glu_b, gdn_a_log=gdn_a_log, gdn_dt_bias=gdn_dt_bias, gdn_norm_w=gdn_norm_w, w_br_ssm=w_br_ssm, w_br_gdn=w_br_gdn, w_out=w_out, ln2_g=ln2_g, ln2_b=ln2_b, ffn2_w_gu=ffn2_w_gu, ffn2_w_down=ffn2_w_down, ln3_g=ln3_g, ln3_b=ln3_b, loss_target=loss_target, m_ffn1_w_gu=m_ffn1_w_gu, m_ffn1_w_down=m_ffn1_w_down, m_ln1_g=m_ln1_g, m_ln1_b=m_ln1_b, m_w_in=m_w_in, m_conv_w=m_conv_w, m_ssm_a_re=m_ssm_a_re, m_ssm_a_im=m_ssm_a_im, m_ssm_log_dt=m_ssm_log_dt, m_ssm_b_re=m_ssm_b_re, m_ssm_b_im=m_ssm_b_im, m_ssm_c_re=m_ssm_c_re, m_ssm_c_im=m_ssm_c_im, m_ssm_d=m_ssm_d, m_glu_w=m_glu_w, m_glu_b=m_glu_b, m_gdn_a_log=m_gdn_a_log, m_gdn_dt_bias=m_gdn_dt_bias, m_gdn_norm_w=m_gdn_norm_w, m_w_br_ssm=m_w_br_ssm, m_w_br_gdn=m_w_br_gdn, m_w_out=m_w_out, m_ln2_g=m_ln2_g, m_ln2_b=m_ln2_b, m_ffn2_w_gu=m_ffn2_w_gu, m_ffn2_w_down=m_ffn2_w_down, m_ln3_g=m_ln3_g, m_ln3_b=m_ln3_b, v_ffn1_w_gu=v_ffn1_w_gu, v_ffn1_w_down=v_ffn1_w_down, v_ln1_g=v_ln1_g, v_ln1_b=v_ln1_b, v_w_in=v_w_in, v_conv_w=v_conv_w, v_ssm_a_re=v_ssm_a_re, v_ssm_a_im=v_ssm_a_im, v_ssm_log_dt=v_ssm_log_dt, v_ssm_b_re=v_ssm_b_re, v_ssm_b_im=v_ssm_b_im, v_ssm_c_re=v_ssm_c_re, v_ssm_c_im=v_ssm_c_im, v_ssm_d=v_ssm_d, v_glu_w=v_glu_w, v_glu_b=v_glu_b, v_gdn_a_log=v_gdn_a_log, v_gdn_dt_bias=v_gdn_dt_bias, v_gdn_norm_w=v_gdn_norm_w, v_w_br_ssm=v_w_br_ssm, v_w_br_gdn=v_w_br_gdn, v_w_out=v_w_out, v_ln2_g=v_ln2_g, v_ln2_b=v_ln2_b, v_ffn2_w_gu=v_ffn2_w_gu, v_ffn2_w_down=v_ffn2_w_down, v_ln3_g=v_ln3_g, v_ln3_b=v_ln3_b)
    weights = {n: given[n] for n in TWIN_WEIGHTS}
    shared = {n: given[n] for n in SHARED_INPUTS}
    per_example = {n: given[n] for n in ['x']}
    grad_fn = _jax.value_and_grad(_loss, argnums=(0, 1))

    def one_microbatch(ex, loss_target):
        ex = dict(ex)
        diff = ex.pop(TWIN_DIFF_INPUT)
        return grad_fn(weights, diff, {**shared, **ex}, loss_target)

    if N_MICROBATCH == 1:
        loss, (grad_w, grad_x) = one_microbatch(per_example, given["loss_target"])
    else:
        def body(carry, xs):
            loss_sum, grad_sum = carry
            l_k, (gw_k, gx_k) = one_microbatch(xs[0], xs[1])
            with _jax.named_scope("update"):
                return (loss_sum + l_k, _jax.tree.map(_jnp.add, grad_sum, gw_k)), gx_k

        init = (_jnp.zeros((), _jnp.float32), _jax.tree.map(_jnp.zeros_like, weights))
        (loss, grad_w), grad_x = _jax.lax.scan(body, init, (per_example, given["loss_target"]))
    with _jax.named_scope("update"):
        delta_w, new_m, new_v = {}, {}, {}
        for n in TWIN_WEIGHTS:
            delta_w[n], new_m[n], new_v[n] = _adamw(weights[n], grad_w[n], given["m_" + n], given["v_" + n])
    return (loss, grad_x, *[grad_w[n] for n in TWIN_WEIGHTS], *[delta_w[n] for n in TWIN_WEIGHTS],
            *[new_m[n] for n in TWIN_WEIGHTS], *[new_v[n] for n in TWIN_WEIGHTS])
```

```python
import functools
import math

import jax
import jax.numpy as jnp
from jax import lax
from jax.experimental import pallas as pl
from jax.experimental.pallas import tpu as pltpu

f32 = jnp.float32
_MXU = jnp.bfloat16
_GDT = jnp.bfloat16
_HP = lax.Precision.HIGHEST
_VMEM_LIMIT = 56 * 1024 * 1024
_MESH_T = pl.DeviceIdType.MESH

LN_EPS = 1e-5
RMS_EPS = 1e-6
L2_EPS = 1e-6
CHUNK = 64
ADAM_LR = 0.001
ADAM_B1 = 0.9
ADAM_B2 = 0.999
ADAM_EPS = 1e-08
ADAM_WD = 0.01
ADAM_STEP = 10

_NN = (((1,), (0,)), ((), ()))
_NT = (((1,), (1,)), ((), ()))
_TN = (((0,), (0,)), ((), ()))

SDS = jax.ShapeDtypeStruct


def _cp(sem):
    return pltpu.CompilerParams(dimension_semantics=sem, vmem_limit_bytes=_VMEM_LIMIT)


def _tile(n, pref):
    if n <= pref:
        return n
    t = (pref // 128) * 128
    while t >= 128:
        if n % t == 0:
            return t
        t -= 128
    return n


def _rtile(n, pref):
    if n <= pref:
        return n
    t = (pref // 16) * 16
    while t >= 16:
        if n % t == 0:
            return t
        t -= 16
    return n


def _mm(name, a, b, dims, grid, a_spec, b_spec, acc_shape, extras, extra_specs, out_shape, out_specs, epilogue):
    nk = grid[2]
    ne = len(extras)
    no = len(out_shape)

    def body(*refs):
        a_ref, b_ref = refs[0], refs[1]
        ex = refs[2:2 + ne]
        outs = refs[2 + ne:2 + ne + no]
        acc = refs[-1]
        k = pl.program_id(2)
        part = lax.dot_general(a_ref[...].astype(_MXU), b_ref[...].astype(_MXU), dims, preferred_element_type=f32)

        @pl.when(k == 0)
        def _():
            acc[...] = part

        @pl.when(k > 0)
        def _():
            acc[...] += part

        @pl.when(k == nk - 1)
        def _():
            epilogue(acc[...], ex, outs)

    return pl.pallas_call(
        body, grid=grid, in_specs=[a_spec, b_spec, *extra_specs], out_specs=list(out_specs), out_shape=list(out_shape),
        scratch_shapes=[pltpu.VMEM(acc_shape, f32)], compiler_params=_cp(("parallel", "parallel", "arbitrary")), name=name,
    )(a, b, *extras)


def _store_epi(acc, ex, outs):
    for o in outs:
        o[...] = acc.astype(o.dtype)


def _ln_epilogue(alpha, c):
    def epi(acc, ex, outs):
        x_ref, g_ref, b_ref = ex
        y_ref, yb_ref, xh_ref, r_ref = outs
        z = alpha * x_ref[...] + c * acc
        mu = jnp.mean(z, axis=-1, keepdims=True)
        zc = z - mu
        var = jnp.mean(zc * zc, axis=-1, keepdims=True)
        r = lax.rsqrt(var + LN_EPS)
        xh = zc * r
        y = xh * g_ref[...] + b_ref[...]
        y_ref[...] = y
        yb_ref[...] = y.astype(yb_ref.dtype)
        xh_ref[...] = xh
        r_ref[...] = r
    return epi


def _mm_ln(name, a, w, x, g, b, alpha, c):
    T, K = a.shape
    D = w.shape[1]
    tm, tk = _rtile(T, 512), _tile(K, 512)
    row = pl.BlockSpec((tm, D), lambda i, j, k: (i, 0))
    vec = pl.BlockSpec((1, D), lambda i, j, k: (0, 0))
    return _mm(
        name, a, w, _NN, (T // tm, 1, K // tk),
        pl.BlockSpec((tm, tk), lambda i, j, k: (i, k)), pl.BlockSpec((tk, D), lambda i, j, k: (k, 0)), (tm, D),
        [x, g, b], [row, vec, vec],
        [SDS((T, D), f32), SDS((T, D), _MXU), SDS((T, D), f32), SDS((T, 1), f32)],
        [row, row, row, pl.BlockSpec((tm, 1), lambda i, j, k: (i, 0))],
        _ln_epilogue(alpha, c),
    )


def _ln_bwd(name, dy, xhat, rstd, g, c):
    T, D = dy.shape
    tm = _rtile(T, 256)

    def body(dy_ref, xh_ref, r_ref, g_ref, dz_ref, df_ref, dg_ref, db_ref):
        i = pl.program_id(0)
        dyv = dy_ref[...]
        xh = xh_ref[...]
        dxh = dyv * g_ref[...]
        m1 = jnp.mean(dxh, axis=-1, keepdims=True)
        m2 = jnp.mean(dxh * xh, axis=-1, keepdims=True)
        dz = r_ref[...] * (dxh - m1 - xh * m2)
        dz_ref[...] = dz
        df_ref[...] = (c * dz).astype(df_ref.dtype)
        pg = jnp.sum(dyv * xh, axis=0, keepdims=True)
        pb = jnp.sum(dyv, axis=0, keepdims=True)

        @pl.when(i == 0)
        def _():
            dg_ref[...] = pg
            db_ref[...] = pb

        @pl.when(i > 0)
        def _():
            dg_ref[...] += pg
            db_ref[...] += pb

    row = pl.BlockSpec((tm, D), lambda i: (i, 0))
    vec = pl.BlockSpec((1, D), lambda i: (0, 0))
    return pl.pallas_call(
        body, grid=(T // tm,), in_specs=[row, row, pl.BlockSpec((tm, 1), lambda i: (i, 0)), vec],
        out_specs=[row, row, vec, vec],
        out_shape=[SDS((T, D), f32), SDS((T, D), _MXU), SDS((1, D), f32), SDS((1, D), f32)],
        compiler_params=_cp(("arbitrary",)), name=name,
    )(dy, xhat, rstd, g)


def _swiglu(g, u):
    return jax.nn.silu(g) * u


def _ffn_up(name, xb, wgu):
    T, D = xb.shape
    FS = wgu.shape[2]
    F = 4 * FS
    tm = _rtile(T, 256)

    def body(x_ref, wg_ref, wu_ref, g_ref, u_ref, h_ref):
        xv = x_ref[...]
        g = jnp.dot(xv, wg_ref[...], preferred_element_type=f32)
        u = jnp.dot(xv, wu_ref[...], preferred_element_type=f32)
        g_ref[...] = g
        u_ref[...] = u
        h_ref[...] = _swiglu(g, u).astype(h_ref.dtype)

    out = pl.BlockSpec((tm, FS), lambda j, i: (i, j))
    return pl.pallas_call(
        body, grid=(4, T // tm),
        in_specs=[pl.BlockSpec((tm, D), lambda j, i: (i, 0)),
                  pl.BlockSpec((None, D, FS), lambda j, i: (j, 0, 0)),
                  pl.BlockSpec((None, D, FS), lambda j, i: (j + 4, 0, 0))],
        out_specs=[out, out, out],
        out_shape=[SDS((T, F), f32), SDS((T, F), f32), SDS((T, F), _MXU)],
        compiler_params=_cp(("parallel", "arbitrary")), name=name,
    )(xb, wgu, wgu)


def _ffn_bwd(pfx, dy, sv, wgu, wd, g_ln, alpha):
    T, D = dy.shape
    FS = wgu.shape[2]
    F = 4 * FS
    dz, dfb, dg, db = _ln_bwd(pfx + "_lnb", dy, sv["xhat"], sv["rstd"], g_ln, 0.5)

    tm, tn, tk = _rtile(T, 512), _tile(F, 512), _tile(D, 512)

    def epi(acc, ex, outs):
        g_ref, u_ref = ex
        _, vjp = jax.vjp(_swiglu, g_ref[...], u_ref[...])
        dgate, dup = vjp(acc)
        outs[0][0] = dgate.astype(outs[0].dtype)
        outs[0][1] = dup.astype(outs[0].dtype)

    gu = pl.BlockSpec((tm, tn), lambda i, j, k: (i, j))
    (dgu,) = _mm(
        pfx + "_dh", dfb, wd, _NT, (T // tm, F // tn, D // tk),
        pl.BlockSpec((tm, tk), lambda i, j, k: (i, k)), pl.BlockSpec((tn, tk), lambda i, j, k: (j, k)), (tm, tn),
        [sv["gate"], sv["up"]], [gu, gu],
        [SDS((2, T, F), _MXU)], [pl.BlockSpec((2, tm, tn), lambda i, j, k: (0, i, j))], epi,
    )

    tm2, tk2 = _tile(F, 512), _rtile(T, 512)
    (dwd,) = _mm(
        pfx + "_dwd", sv["h"], dfb, _TN, (F // tm2, 1, T // tk2),
        pl.BlockSpec((tk2, tm2), lambda i, j, k: (k, i)), pl.BlockSpec((tk2, D), lambda i, j, k: (k, 0)), (tm2, D),
        [], [], [SDS((F, D), _GDT)], [pl.BlockSpec((tm2, D), lambda i, j, k: (i, 0))], _store_epi,
    )

    tn3 = _tile(D, 1024)

    def epi3(acc, ex, outs):
        outs[0][...] = alpha * ex[0][...] + acc

    (dx,) = _mm(
        pfx + "_dx", dgu, wgu, _NT, (T // tm, D // tn3, 8),
        pl.BlockSpec((None, tm, FS), lambda i, j, k: (k // 4, i, k % 4)),
        pl.BlockSpec((None, tn3, FS), lambda i, j, k: (k, j, 0)), (tm, tn3),
        [dz], [pl.BlockSpec((tm, tn3), lambda i, j, k: (i, j))],
        [SDS((T, D), f32)], [pl.BlockSpec((tm, tn3), lambda i, j, k: (i, j))], epi3,
    )

    tm4, tk4 = _tile(D, 512), _rtile(T, 512)
    (dwgu,) = _mm(
        pfx + "_dwgu", sv["xb"], dgu, _TN, (D // tm4, 8, T // tk4),
        pl.BlockSpec((tk4, tm4), lambda i, j, k: (k, i)),
        pl.BlockSpec((None, tk4, FS), lambda i, j, k: (j // 4, k, j % 4)), (tm4, FS),
        [], [], [SDS((8, D, FS), _GDT)], [pl.BlockSpec((None, tm4, FS), lambda i, j, k: (j, i, 0))], _store_epi,
    )
    return dx, dwgu, dwd, dg, db


def _zoh(a_re, a_im, log_dt, b_re_t, b_im_t):
    dt = jnp.exp(log_dt)
    mag = jnp.exp(a_re * dt)
    lr_, li_ = mag * jnp.cos(a_im * dt), mag * jnp.sin(a_im * dt)
    den = a_re * a_re + a_im * a_im
    pr, pi = lr_ - 1.0, li_
    qr, qi = a_re / den, -a_im / den
    zr, zi = pr * qr - pi * qi, pr * qi + pi * qr
    bbr = zr[None] * b_re_t - zi[None] * b_im_t
    bbi = zr[None] * b_im_t + zi[None] * b_re_t
    return lr_, li_, bbr, bbi


def _zoh_fwd(name, a_re, a_im, log_dt, b_re_t, b_im_t):
    G, P = a_re.shape
    H = b_re_t.shape[0]

    def body(ar, ai, ld, br, bi, o1, o2, o3, o4):
        r = _zoh(ar[...], ai[...], ld[...], br[...], bi[...])
        o1[...], o2[...], o3[...], o4[...] = r

    return pl.pallas_call(
        body, out_shape=[SDS((G, P), f32), SDS((G, P), f32), SDS((H, G, P), f32), SDS((H, G, P), f32)], name=name,
    )(a_re, a_im, log_dt, b_re_t, b_im_t)


def _zoh_bwd(name, a_re, a_im, log_dt, b_re_t, b_im_t, dlr, dli, dbbr, dbbi):
    G, P = a_re.shape
    H = b_re_t.shape[0]

    def body(ar, ai, ld, br, bi, g1, g2, g3, g4, o1, o2, o3, o4, o5):
        _, vjp = jax.vjp(_zoh, ar[...], ai[...], ld[...], br[...], bi[...])
        r = vjp((g1[...], g2[...], g3[...], g4[...]))
        o1[...], o2[...], o3[...], o4[...], o5[...] = r

    return pl.pallas_call(
        body, out_shape=[SDS((G, P), f32), SDS((G, P), f32), SDS((G, 1), f32), SDS((H, G, P), f32), SDS((H, G, P), f32)],
        name=name,
    )(a_re, a_im, log_dt, b_re_t, b_im_t, dlr, dli, dbbr, dbbi)


def _blockdiag(m):
    G, A, B = m.shape
    eye = jnp.eye(8, dtype=bool)
    m4 = m.reshape(G // 8, 8, A, B)
    out = jnp.where(eye[None, :, None, :, None], m4[:, :, :, None, :], jnp.zeros((), m.dtype))
    return out.reshape(G // 8, 8 * A, 8 * B)


def _blockdiag_extract(mb, A, B):
    J = mb.shape[0]
    m5 = mb.reshape(J, 8, A, 8, B)
    d = jnp.stack([m5[:, i, :, i, :] for i in range(8)], axis=1)
    return d.reshape(J * 8, A, B)


def _bd2(name, a, a_col0, b1, b2, out_dtype=f32):
    T = a.shape[0]
    J, KA, NB = b1.shape
    tm = _rtile(T, 512)

    def body(a_ref, b1_ref, b2_ref, o1, o2):
        av = a_ref[...].astype(_MXU)
        o1[...] = jnp.dot(av, b1_ref[...].astype(_MXU), preferred_element_type=f32).astype(o1.dtype)
        o2[...] = jnp.dot(av, b2_ref[...].astype(_MXU), preferred_element_type=f32).astype(o2.dtype)

    bs = pl.BlockSpec((None, KA, NB), lambda i, j: (j, 0, 0))
    os_ = pl.BlockSpec((tm, NB), lambda i, j: (i, j))
    return pl.pallas_call(
        body, grid=(T // tm, J), in_specs=[pl.BlockSpec((tm, KA), lambda i, j: (i, j + a_col0)), bs, bs],
        out_specs=[os_, os_], out_shape=[SDS((T, J * NB), out_dtype)] * 2,
        compiler_params=_cp(("parallel", "parallel")), name=name,
    )(a, b1, b2)


def _bd_sum(name, a1, a2, b1, b2, extras, extra_specs_fn, out_shape, epilogue):
    T = a1.shape[0]
    J, KA, NB = b1.shape
    tm = _rtile(T, 512)
    ne = len(extras)

    def body(*refs):
        a1_ref, a2_ref, b1_ref, b2_ref = refs[:4]
        ex = refs[4:4 + ne]
        outs = refs[4 + ne:]
        acc = jnp.dot(a1_ref[...].astype(_MXU), b1_ref[...].astype(_MXU), preferred_element_type=f32)
        acc = acc + jnp.dot(a2_ref[...].astype(_MXU), b2_ref[...].astype(_MXU), preferred_element_type=f32)
        epilogue(acc, ex, outs)

    as_ = pl.BlockSpec((tm, KA), lambda i, j: (i, j))
    bs = pl.BlockSpec((None, KA, NB), lambda i, j: (j, 0, 0))
    os_ = pl.BlockSpec((tm, NB), lambda i, j: (i, j))
    return pl.pallas_call(
        body, grid=(T // tm, J), in_specs=[as_, as_, bs, bs, *extra_specs_fn(tm, NB)],
        out_specs=[os_] * len(out_shape), out_shape=list(out_shape),
        compiler_params=_cp(("parallel", "parallel")), name=name,
    )(a1, a2, b1, b2, *extras)


def _bdT2(name, a1, a2, a_col0, b1, b2, b_col0, KA, NB, J):
    T = a1.shape[0]
    tk = _rtile(T, 512)

    def body(a1_ref, a2_ref, b1_ref, b2_ref, o1, o2):
        k = pl.program_id(1)
        p1 = lax.dot_general(a1_ref[...].astype(_MXU), b1_ref[...].astype(_MXU), _TN, preferred_element_type=f32)
        p2 = lax.dot_general(a2_ref[...].astype(_MXU), b2_ref[...].astype(_MXU), _TN, preferred_element_type=f32)

        @pl.when(k == 0)
        def _():
            o1[...] = p1
            o2[...] = p2

        @pl.when(k > 0)
        def _():
            o1[...] += p1
            o2[...] += p2

    as_ = pl.BlockSpec((tk, KA), lambda j, k: (k, j + a_col0))
    bs = pl.BlockSpec((tk, NB), lambda j, k: (k, j + b_col0))
    os_ = pl.BlockSpec((None, KA, NB), lambda j, k: (j, 0, 0))
    return pl.pallas_call(
        body, grid=(J, T // tk), in_specs=[as_, as_, bs, bs], out_specs=[os_, os_],
        out_shape=[SDS((J, KA, NB), f32)] * 2, compiler_params=_cp(("parallel", "arbitrary")), name=name,
    )(a1, a2, b1, b2)


def _s5_scan(name, bur, bui, lr_, li_):
    T, N = bur.shape
    cb = _tile(N, 512)

    def body(br_ref, bi_ref, lr_ref, li_ref, sr_ref, si_ref):
        lr_v, li_v = lr_ref[...], li_ref[...]

        def step(t, carry):
            sr, si = carry
            nr = lr_v * sr - li_v * si + br_ref[pl.ds(t, 1), :]
            ni = lr_v * si + li_v * sr + bi_ref[pl.ds(t, 1), :]
            sr_ref[pl.ds(t, 1), :] = nr
            si_ref[pl.ds(t, 1), :] = ni
            return nr, ni

        z = jnp.zeros((1, cb), f32)
        lax.fori_loop(0, T, step, (z, z))

    col = pl.BlockSpec((T, cb), lambda j: (0, j))
    vec = pl.BlockSpec((1, cb), lambda j: (0, j))
    return pl.pallas_call(
        body, grid=(N // cb,), in_specs=[col, col, vec, vec], out_specs=[col, col],
        out_shape=[SDS((T, N), f32)] * 2, compiler_params=_cp(("parallel",)), name=name,
    )(bur, bui, lr_, li_)


def _s5_scan_bwd(name, dsr, dsi, sr, si, lr_, li_):
    T, N = dsr.shape
    cb = _tile(N, 256)

    def body(dr_ref, di_ref, sr_ref, si_ref, lr_ref, li_ref, ar_ref, ai_ref, glr_ref, gli_ref):
        lr_v, li_v = lr_ref[...], li_ref[...]

        def step(n, carry):
            ar, ai, glr, gli = carry
            t = T - 1 - n
            nr = dr_ref[pl.ds(t, 1), :] + lr_v * ar + li_v * ai
            ni = di_ref[pl.ds(t, 1), :] + lr_v * ai - li_v * ar
            ar_ref[pl.ds(t, 1), :] = nr
            ai_ref[pl.ds(t, 1), :] = ni
            pr = sr_ref[pl.ds(t - 1, 1), :]
            pi = si_ref[pl.ds(t - 1, 1), :]
            glr = glr + nr * pr + ni * pi
            gli = gli + ni * pr - nr * pi
            return nr, ni, glr, gli

        z = jnp.zeros((1, cb), f32)
        ar, ai, glr, gli = lax.fori_loop(0, T - 1, step, (z, z, z, z))
        ar_ref[pl.ds(0, 1), :] = dr_ref[pl.ds(0, 1), :] + lr_v * ar + li_v * ai
        ai_ref[pl.ds(0, 1), :] = di_ref[pl.ds(0, 1), :] + lr_v * ai - li_v * ar
        glr_ref[...] = glr
        gli_ref[...] = gli

    col = pl.BlockSpec((T, cb), lambda j: (0, j))
    vec = pl.BlockSpec((1, cb), lambda j: (0, j))
    return pl.pallas_call(
        body, grid=(N // cb,), in_specs=[col, col, col, col, vec, vec], out_specs=[col, col, vec, vec],
        out_shape=[SDS((T, N), f32), SDS((T, N), f32), SDS((1, N), f32), SDS((1, N), f32)],
        compiler_params=_cp(("parallel",)), name=name,
    )(dsr, dsi, sr, si, lr_, li_)


def _conv_fwd(name, p, col0, w, GW3):
    T = p.shape[0]
    K = w.shape[0]
    cb = 128
    c0 = col0 // cb

    def body(x_ref, w_ref, o_ref, pad_ref):
        pad_ref[pl.ds(0, 8), :] = jnp.zeros((8, cb), f32)
        pad_ref[pl.ds(8, T), :] = x_ref[...]
        wv = w_ref[...]
        acc = jnp.zeros((T, cb), f32)
        for j in range(K):
            acc = acc + wv[j:j + 1, :] * pad_ref[pl.ds(8 - (K - 1) + j, T), :]
        o_ref[...] = jax.nn.silu(acc)

    return pl.pallas_call(
        body, grid=(GW3 // cb,),
        in_specs=[pl.BlockSpec((T, cb), lambda j: (0, j + c0)), pl.BlockSpec((K, cb), lambda j: (0, j))],
        out_specs=pl.BlockSpec((T, cb), lambda j: (0, j)), out_shape=SDS((T, GW3), f32),
        scratch_shapes=[pltpu.VMEM((T + 8, cb), f32)], compiler_params=_cp(("parallel",)), name=name,
    )(p, w)


def _conv_bwd(name, p, col0, w, dout3):
    T = p.shape[0]
    K = w.shape[0]
    GW = dout3.shape[2]
    GW3 = 3 * GW
    cb = 128
    c0 = col0 // cb
    nb = GW // cb

    def body(x_ref, w_ref, d_ref, dx_ref, dw_ref, pad_ref, dpad_ref):
        pad_ref[pl.ds(0, 8), :] = jnp.zeros((8, cb), f32)
        pad_ref[pl.ds(8, T), :] = x_ref[...]
        wv = w_ref[...]
        pre = jnp.zeros((T, cb), f32)
        for j in range(K):
            pre = pre + wv[j:j + 1, :] * pad_ref[pl.ds(8 - (K - 1) + j, T), :]
        _, vjp = jax.vjp(jax.nn.silu, pre)
        (dpre,) = vjp(d_ref[...])
        dpad_ref[pl.ds(0, T), :] = dpre
        dpad_ref[pl.ds(T, 8), :] = jnp.zeros((8, cb), f32)
        dx = jnp.zeros((T, cb), f32)
        rows = []
        for j in range(K):
            dx = dx + wv[j:j + 1, :] * dpad_ref[pl.ds((K - 1) - j, T), :]
            rows.append(jnp.sum(dpre * pad_ref[pl.ds(8 - (K - 1) + j, T), :], axis=0, keepdims=True))
        dx_ref[...] = dx.astype(dx_ref.dtype)
        for j in range(K):
            dw_ref[pl.ds(j, 1), :] = rows[j]

    return pl.pallas_call(
        body, grid=(GW3 // cb,),
        in_specs=[pl.BlockSpec((T, cb), lambda j: (0, j + c0)), pl.BlockSpec((K, cb), lambda j: (0, j)),
                  pl.BlockSpec((None, T, cb), lambda j: (j // nb, 0, j % nb))],
        out_specs=[pl.BlockSpec((T, cb), lambda j: (0, j)), pl.BlockSpec((K, cb), lambda j: (0, j))],
        out_shape=[SDS((T, GW3), _MXU), SDS((K, GW3), f32)],
        scratch_shapes=[pltpu.VMEM((T + 8, cb), f32), pltpu.VMEM((T + 8, cb), f32)],
        compiler_params=_cp(("parallel",)), name=name,
    )(p, w, dout3)


def _hdot(a, b, dims=_NN):
    return lax.dot_general(a, b, dims, precision=_HP, preferred_element_type=f32)


def _gdn_chunk(S, q, k, v, z, bl, ain, alog, dtb, nw):
    C, d = q.shape
    ri = lax.broadcasted_iota(jnp.int32, (C, C), 0)
    ci = lax.broadcasted_iota(jnp.int32, (C, C), 1)
    causal = ri >= ci
    strict = ri > ci
    tri = causal.astype(f32)
    qn = q * lax.rsqrt(jnp.sum(q * q, axis=-1, keepdims=True) + L2_EPS) * (d ** -0.5)
    kn = k * lax.rsqrt(jnp.sum(k * k, axis=-1, keepdims=True) + L2_EPS)
    beta = jax.nn.sigmoid(bl)
    g = -jnp.exp(alog) * jax.nn.softplus(ain + dtb)
    gb = jnp.broadcast_to(g, (C, C))
    gc_col = _hdot(tri, gb)
    gc_row = _hdot(jnp.ones((C, C), f32), jnp.where(ri <= ci, gb, 0.0))
    diff = jnp.where(causal, gc_col - gc_row, 0.0)
    decay = jnp.where(causal, jnp.exp(diff), 0.0)
    gcum = gc_col[:, 0:1]
    glast = gc_col[C - 1:C, 0:1]
    egc = jnp.exp(gcum)
    kb = kn * beta
    lower = jnp.where(strict, _hdot(kb, kn, _NT) * decay, 0.0)
    x = jnp.concatenate([v * beta, kb * egc], axis=-1)
    m = -lower
    for it in range(6):
        x = x + _hdot(m, x)
        if it < 5:
            m = _hdot(m, m)
    u_val, w_key = x[:, :d], x[:, d:]
    attn = _hdot(qn, kn, _NT) * decay
    q_dec = qn * egc
    k_dec = kn * jnp.exp(glast - gcum)
    v_new = u_val - _hdot(w_key, S)
    out = _hdot(q_dec, S) + _hdot(attn, v_new)
    s_new = S * jnp.exp(glast) + _hdot(k_dec, v_new, _TN)
    o = out * lax.rsqrt(jnp.mean(out * out, axis=-1, keepdims=True) + RMS_EPS) * nw
    o = o * jax.nn.silu(z)
    return s_new, o


def _gdn_fwd(name, qkv, p, zcol0, blt, aint, alog, dtb, nw, NH, HD):
    T = qkv.shape[0]
    N = T // CHUNK
    GW = NH * HD
    zc0 = zcol0 // HD

    def body(q_ref, k_ref, v_ref, z_ref, bl_ref, ain_ref, al_ref, dtb_ref, nw_ref, o_ref, ssave_ref, s_scr):
        n = pl.program_id(1)

        @pl.when(n == 0)
        def _():
            s_scr[...] = jnp.zeros_like(s_scr)

        s_in = s_scr[...]
        ssave_ref[...] = s_in
        s_new, o = _gdn_chunk(s_in, q_ref[...], k_ref[...], v_ref[...], z_ref[...], bl_ref[...], ain_ref[...],
                              al_ref[...], dtb_ref[...], nw_ref[...])
        s_scr[...] = s_new
        o_ref[...] = o.astype(o_ref.dtype)

    ch = lambda off: pl.BlockSpec((CHUNK, HD), lambda h, n: (n, h + off))
    sc = pl.BlockSpec((None, CHUNK, 1), lambda h, n: (h, n, 0))
    hs = pl.BlockSpec((None, 1, 1), lambda h, n: (h, 0, 0))
    return pl.pallas_call(
        body, grid=(NH, N),
        in_specs=[ch(0), ch(NH), ch(2 * NH), ch(zc0), sc, sc, hs, hs, pl.BlockSpec((1, HD), lambda h, n: (0, 0))],
        out_specs=[pl.BlockSpec((CHUNK, HD), lambda h, n: (n, h)),
                   pl.BlockSpec((None, None, HD, HD), lambda h, n: (h, n, 0, 0))],
        out_shape=[SDS((T, GW), _MXU), SDS((NH, N, HD, HD), f32)],
        scratch_shapes=[pltpu.VMEM((HD, HD), f32)], compiler_params=_cp(("parallel", "arbitrary")), name=name,
    )(qkv, qkv, qkv, p, blt, aint, alog, dtb, nw)


def _gdn_bwd(name, qkv, p, zcol0, blt, aint, alog, dtb, nw, ssave, do, NH, HD):
    T = qkv.shape[0]
    N = T // CHUNK
    GW = NH * HD
    zc0 = zcol0 // HD

    def body(q_ref, k_ref, v_ref, z_ref, bl_ref, ain_ref, al_ref, dtb_ref, nw_ref, ss_ref, do_ref,
             dqkv_ref, dz_ref, dbl_ref, dain_ref, dal_ref, ddtb_ref, dnw_ref, ds_scr):
        h = pl.program_id(0)
        n = pl.program_id(1)

        @pl.when(n == 0)
        def _():
            ds_scr[...] = jnp.zeros_like(ds_scr)

        _, vjp = jax.vjp(_gdn_chunk, ss_ref[...], q_ref[...], k_ref[...], v_ref[...], z_ref[...], bl_ref[...],
                         ain_ref[...], al_ref[...], dtb_ref[...], nw_ref[...])
        ds, dq, dk, dv, dz, dbl, dain, dal, ddtb, dnw = vjp((ds_scr[...], do_ref[...].astype(f32)))
        ds_scr[...] = ds
        dqkv_ref[0] = dq
        dqkv_ref[1] = dk
        dqkv_ref[2] = dv
        dz_ref[...] = dz.astype(dz_ref.dtype)
        dbl_ref[...] = dbl
        dain_ref[...] = dain

        @pl.when(n == 0)
        def _():
            dal_ref[...] = dal
            ddtb_ref[...] = ddtb

        @pl.when(n > 0)
        def _():
            dal_ref[...] += dal
            ddtb_ref[...] += ddtb

        @pl.when((n == 0) & (h == 0))
        def _():
            dnw_ref[...] = dnw

        @pl.when((n > 0) | (h > 0))
        def _():
            dnw_ref[...] += dnw

    R = N - 1
    ch = lambda off: pl.BlockSpec((CHUNK, HD), lambda h, n: (R - n, h + off))
    sc = pl.BlockSpec((None, CHUNK, 1), lambda h, n: (h, R - n, 0))
    hs = pl.BlockSpec((None, 1, 1), lambda h, n: (h, 0, 0))
    nws = pl.BlockSpec((1, HD), lambda h, n: (0, 0))
    return pl.pallas_call(
        body, grid=(NH, N),
        in_specs=[ch(0), ch(NH), ch(2 * NH), ch(zc0), sc, sc, hs, hs, nws,
                  pl.BlockSpec((None, None, HD, HD), lambda h, n: (h, R - n, 0, 0)),
                  pl.BlockSpec((CHUNK, HD), lambda h, n: (R - n, h))],
        out_specs=[pl.BlockSpec((3, CHUNK, HD), lambda h, n: (0, R - n, h)),
                   pl.BlockSpec((CHUNK, HD), lambda h, n: (R - n, h)), sc, sc, hs, hs, nws],
        out_shape=[SDS((3, T, GW), f32), SDS((T, GW), _MXU), SDS((NH, T, 1), f32), SDS((NH, T, 1), f32),
                   SDS((NH, 1, 1), f32), SDS((NH, 1, 1), f32), SDS((1, HD), f32)],
        scratch_shapes=[pltpu.VMEM((HD, HD), f32)], compiler_params=_cp(("arbitrary", "arbitrary")), name=name,
    )(qkv, qkv, qkv, p, blt, aint, alog, dtb, nw, ssave, do)


def _loss_head(name, y, tgt):
    T, D = y.shape
    tm = _rtile(T, 256)

    def body(y_ref, t_ref, dy_ref, l_ref):
        i = pl.program_id(0)
        err = y_ref[...] - t_ref[...]
        dy_ref[...] = err * (1.0 / D)
        part = 0.5 * jnp.sum(jnp.sum(err * err, axis=-1, keepdims=True) * (1.0 / D), axis=0, keepdims=True)

        @pl.when(i == 0)
        def _():
            l_ref[...] = part

        @pl.when(i > 0)
        def _():
            l_ref[...] += part

    row = pl.BlockSpec((tm, D), lambda i: (i, 0))
    return pl.pallas_call(
        body, grid=(T // tm,), in_specs=[row, row], out_specs=[row, pl.BlockSpec((1, 1), lambda i: (0, 0))],
        out_shape=[SDS((T, D), f32), SDS((1, 1), f32)], compiler_params=_cp(("arbitrary",)), name=name,
    )(y, tgt)


def _adam_math(w, g, m, v):
    m = ADAM_B1 * m + (1.0 - ADAM_B1) * g
    v = ADAM_B2 * v + (1.0 - ADAM_B2) * jnp.square(g)
    m_hat = m / (1.0 - ADAM_B1 ** ADAM_STEP)
    v_hat = v / (1.0 - ADAM_B2 ** ADAM_STEP)
    delta = -ADAM_LR * (m_hat / (jnp.sqrt(v_hat) + ADAM_EPS) + ADAM_WD * w)
    return delta, m, v


def _add2(name, a, b, out_dtype):
    N, R, C = a.shape
    tr = _rtile(R, max(16, (1 << 19) // max(C, 1) // 16 * 16))

    def body(a_ref, b_ref, o_ref):
        o_ref[...] = (a_ref[...].astype(f32) + b_ref[...].astype(f32)).astype(o_ref.dtype)

    blk = pl.BlockSpec((None, tr, C), lambda n, i: (n, i, 0))
    return pl.pallas_call(
        body, grid=(N, R // tr), in_specs=[blk, blk], out_specs=blk, out_shape=SDS((N, R, C), out_dtype),
        compiler_params=_cp(("parallel", "parallel")), name=name,
    )(a, b)


def _adamw_big(name, ga, gb, w, m, v, l):
    R, C = ga.shape
    tr = _rtile(R, max(16, (1 << 18) // max(C, 1) // 16 * 16))

    def body(ga_ref, gb_ref, w_ref, m_ref, v_ref, g_ref, d_ref, nm_ref, nv_ref):
        g = ga_ref[...].astype(f32) + gb_ref[...].astype(f32)
        d, nm, nv = _adam_math(w_ref[...], g, m_ref[...], v_ref[...])
        g_ref[...] = g
        d_ref[...] = d
        nm_ref[...] = nm
        nv_ref[...] = nv

    blk = pl.BlockSpec((tr, C), lambda i: (i, 0))
    lblk = pl.BlockSpec((None, tr, C), lambda i: (l, i, 0))
    return pl.pallas_call(
        body, grid=(R // tr,), in_specs=[blk, blk, lblk, lblk, lblk], out_specs=[blk] * 4,
        out_shape=[SDS((R, C), f32)] * 4, compiler_params=_cp(("parallel",)), name=name,
    )(ga, gb, w, m, v)


def _adamw_small(name, gall, w, m, v):
    _, R, C = gall.shape
    tr = _rtile(R, 512)

    def body(ga_ref, w_ref, m_ref, v_ref, g_ref, d_ref, nm_ref, nv_ref):
        g = ga_ref[0]
        for s in range(1, 8):
            g = g + ga_ref[s]
        d, nm, nv = _adam_math(w_ref[...], g, m_ref[...], v_ref[...])
        g_ref[...] = g
        d_ref[...] = d
        nm_ref[...] = nm
        nv_ref[...] = nv

    blk = pl.BlockSpec((tr, C), lambda i: (i, 0))
    return pl.pallas_call(
        body, grid=(R // tr,), in_specs=[pl.BlockSpec((8, tr, C), lambda i: (0, i, 0)), blk, blk, blk], out_specs=[blk] * 4,
        out_shape=[SDS((R, C), f32)] * 4, compiler_params=_cp(("parallel",)), name=name,
    )(gall, w, m, v)


def _peer(axis):
    x, y, c = lax.axis_index("x"), lax.axis_index("y"), lax.axis_index("c")
    me = {"x": x, "y": y, "c": c}[axis]
    peer = {"x": (1 - x, y, c), "y": (x, 1 - y, c), "c": (x, y, 1 - c)}[axis]
    return me, peer


def _gather_stage(name, axis, tensors):
    n = len(tensors)
    hbm = pl.BlockSpec(memory_space=pltpu.HBM)

    def body(*refs):
        ins, outs = refs[:n], refs[n:2 * n]
        send_sems, recv_sems, loc_sems = refs[2 * n:]
        me, peer = _peer(axis)
        rem, loc = [], []
        for t in range(n):
            dst = outs[t].at[:, me]
            loc.append(pltpu.make_async_copy(ins[t], dst, loc_sems.at[t]))
            rem.append(pltpu.make_async_remote_copy(src_ref=ins[t], dst_ref=dst, send_sem=send_sems.at[t],
                                                    recv_sem=recv_sems.at[t], device_id=peer, device_id_type=_MESH_T))
        for cp in loc + rem:
            cp.start()
        for cp in loc + rem:
            cp.wait()

    return pl.pallas_call(
        body, in_specs=[hbm] * n, out_specs=[hbm] * n,
        out_shape=[SDS((t.shape[0], 2) + tuple(t.shape[1:]), t.dtype) for t in tensors],
        scratch_shapes=[pltpu.SemaphoreType.DMA((n,)), pltpu.SemaphoreType.DMA((n,)), pltpu.SemaphoreType.DMA((n,))],
        name=name,
    )(*tensors)


def _scatter_stage(name, axis, tensors):
    n = len(tensors)
    hbm = pl.BlockSpec(memory_space=pltpu.HBM)

    def body(*refs):
        ins, keeps, recvs = refs[:n], refs[n:2 * n], refs[2 * n:3 * n]
        send_sems, recv_sems, loc_sems = refs[3 * n:]
        me, peer = _peer(axis)
        rem, loc = [], []
        for t in range(n):
            loc.append(pltpu.make_async_copy(ins[t].at[:, me], keeps[t], loc_sems.at[t]))
            rem.append(pltpu.make_async_remote_copy(src_ref=ins[t].at[:, 1 - me], dst_ref=recvs[t], send_sem=send_sems.at[t],
                                                    recv_sem=recv_sems.at[t], device_id=peer, device_id_type=_MESH_T))
        for cp in loc + rem:
            cp.start()
        for cp in loc + rem:
            cp.wait()

    half = [SDS((t.shape[0],) + tuple(t.shape[2:]), t.dtype) for t in tensors]
    res = pl.pallas_call(
        body, in_specs=[hbm] * n, out_specs=[hbm] * (2 * n), out_shape=half + half,
        scratch_shapes=[pltpu.SemaphoreType.DMA((n,)), pltpu.SemaphoreType.DMA((n,)), pltpu.SemaphoreType.DMA((n,))],
        name=name,
    )(*tensors)
    return res[:n], res[n:]


def _all_gather(pfx, tensors):
    shapes = [t.shape for t in tensors]
    cur = [t[None] for t in tensors]
    cur = _gather_stage(pfx + "_y", "y", cur)
    cur = _gather_stage(pfx + "_x", "x", cur)
    cur = [t.reshape((4,) + tuple(s)) for t, s in zip(cur, shapes)]
    cur = _gather_stage(pfx + "_c", "c", cur)
    return [t.reshape((8,) + tuple(s)) for t, s in zip(cur, shapes)]


def _reduce_scatter(pfx, tensors):
    rcs = [t.shape[1:] for t in tensors]
    cur = [t.reshape((4, 2) + tuple(rc)) for t, rc in zip(tensors, rcs)]
    keeps, recvs = _scatter_stage(pfx + "_c", "c", cur)
    cur = [_add2(f"{pfx}_addc{i}", k, r, k.dtype) for i, (k, r) in enumerate(zip(keeps, recvs))]
    cur = [t.reshape((1, 2, 2) + tuple(rc)) for t, rc in zip(cur, rcs)]
    keeps, recvs = _scatter_stage(pfx + "_x", "x", cur)
    cur = [_add2(f"{pfx}_addx{i}", k[0], r[0], k.dtype) for i, (k, r) in enumerate(zip(keeps, recvs))]
    cur = [t.reshape((1, 2) + tuple(rc)) for t, rc in zip(cur, rcs)]
    keeps, recvs = _scatter_stage(pfx + "_y", "y", cur)
    return [(k[0], r[0]) for k, r in zip(keeps, recvs)]


def _mm_nn(name, a, w, out_dtype, tn_pref=1024):
    T, K = a.shape
    N = w.shape[1]
    tm, tn, tk = _rtile(T, 512), _tile(N, tn_pref), _tile(K, 512)
    return _mm(
        name, a, w, _NN, (T // tm, N // tn, K // tk),
        pl.BlockSpec((tm, tk), lambda i, j, k: (i, k)), pl.BlockSpec((tk, tn), lambda i, j, k: (k, j)), (tm, tn),
        [], [], [SDS((T, N), out_dtype)], [pl.BlockSpec((tm, tn), lambda i, j, k: (i, j))], _store_epi,
    )[0]


def _mm_tn(name, a, b, out_dtype):
    T, M = a.shape
    N = b.shape[1]
    tm, tn, tk = _tile(M, 512), _tile(N, 2048), _rtile(T, 512)
    return _mm(
        name, a, b, _TN, (M // tm, N // tn, T // tk),
        pl.BlockSpec((tk, tm), lambda i, j, k: (k, i)), pl.BlockSpec((tk, tn), lambda i, j, k: (k, j)), (tm, tn),
        [], [], [SDS((M, N), out_dtype)], [pl.BlockSpec((tm, tn), lambda i, j, k: (i, j))], _store_epi,
    )[0]


def _mm_tn_slots(name, a, b, out_dtype):
    T, M = a.shape
    NS = b.shape[1] // 8
    tm, tk = _tile(M, 512), _rtile(T, 512)
    return _mm(
        name, a, b, _TN, (M // tm, 8, T // tk),
        pl.BlockSpec((tk, tm), lambda i, j, k: (k, i)), pl.BlockSpec((tk, NS), lambda i, j, k: (k, j)), (tm, NS),
        [], [], [SDS((8, M, NS), out_dtype)], [pl.BlockSpec((None, tm, NS), lambda i, j, k: (j, i, 0))], _store_epi,
    )[0]


def _mm_nt_slots(name, a, w8, out_dtype):
    T = a.shape[0]
    _, M, NS = w8.shape
    tm, tn = _rtile(T, 512), _tile(M, 1024)
    return _mm(
        name, a, w8, _NT, (T // tm, M // tn, 8),
        pl.BlockSpec((tm, NS), lambda i, j, k: (i, k)), pl.BlockSpec((None, tn, NS), lambda i, j, k: (k, j, 0)), (tm, tn),
        [], [], [SDS((T, M), out_dtype)], [pl.BlockSpec((tm, tn), lambda i, j, k: (i, j))], _store_epi,
    )[0]


def _colsum_kernel(name, fn, ins, in_cols, outs_elem, n_sum, C):
    T = ins[0].shape[0]
    tm = _rtile(T, 256)
    ne = len(outs_elem)

    def body(*refs):
        i = pl.program_id(0)
        iv = [r[...] for r in refs[:len(ins)]]
        res = fn(*iv)
        for o, r in zip(refs[len(ins):len(ins) + ne], res[:ne]):
            o[...] = r.astype(o.dtype)
        sums = [jnp.sum(r, axis=0, keepdims=True) for r in res[ne:]]

        @pl.when(i == 0)
        def _():
            for o, s in zip(refs[len(ins) + ne:], sums):
                o[...] = s

        @pl.when(i > 0)
        def _():
            for o, s in zip(refs[len(ins) + ne:], sums):
                o[...] += s

    in_specs = []
    for arr, off in zip(ins, in_cols):
        if off is None:
            in_specs.append(pl.BlockSpec((1, C), lambda i: (0, 0)))
        else:
            in_specs.append(pl.BlockSpec((tm, C), lambda i, off=off: (i, off)))
    row = pl.BlockSpec((tm, C), lambda i: (i, 0))
    vec = pl.BlockSpec((1, C), lambda i: (0, 0))
    return pl.pallas_call(
        body, grid=(T // tm,), in_specs=in_specs, out_specs=[row] * ne + [vec] * n_sum,
        out_shape=[SDS((T, C), dt) for dt in outs_elem] + [SDS((1, C), f32)] * n_sum,
        compiler_params=_cp(("arbitrary",)), name=name,
    )(*ins)


def _merge(gs, gg, a_s, a_g):
    return jax.nn.sigmoid(gs) * a_s + jax.nn.sigmoid(gg) * a_g


def _glu(yg, lp):
    return yg * jax.nn.sigmoid(lp)


_BIG = ("ffn1_w_gu", "ffn1_w_down", "w_in", "conv_w", "glu_w", "w_br_ssm", "w_br_gdn", "w_out", "ffn2_w_gu", "ffn2_w_down")
_SMALL = ("ln1_g", "ln1_b", "ssm_a_re", "ssm_a_im", "ssm_log_dt", "ssm_b_re", "ssm_b_im", "ssm_c_re", "ssm_c_im", "ssm_d",
          "glu_b", "gdn_a_log", "gdn_dt_bias", "gdn_norm_w", "ln2_g", "ln2_b", "ln3_g", "ln3_b")
_ORDER = ("ffn1_w_gu", "ffn1_w_down", "ln1_g", "ln1_b", "w_in", "conv_w", "ssm_a_re", "ssm_a_im", "ssm_log_dt", "ssm_b_re",
          "ssm_b_im", "ssm_c_re", "ssm_c_im", "ssm_d", "glu_w", "glu_b", "gdn_a_log", "gdn_dt_bias", "gdn_norm_w", "w_br_ssm",
          "w_br_gdn", "w_out", "ln2_g", "ln2_b", "ffn2_w_gu", "ffn2_w_down", "ln3_g", "ln3_b")


def _step(x, tgt, W, M, V):
    T, D = x.shape[1], x.shape[2]
    L = W["ffn1_w_gu"].shape[0]
    G, P = W["ssm_a_re"].shape[1:]
    H = W["ssm_b_re"].shape[3]
    SW = G * H
    NH = W["gdn_a_log"].shape[1]
    HD = W["gdn_norm_w"].shape[1]
    GW = NH * HD
    KC = W["conv_w"].shape[1]
    DS = D // 8
    alpha = (2.0 * L) ** 0.25
    o_b = SW + 4 * GW
    o_gs = o_b + 2 * NH
    IN = o_gs + 2 * D
    NM = IN - 2 * NH
    m_qkv, m_z, m_gs, m_gg = SW, SW + 3 * GW, SW + 4 * GW, SW + 4 * GW + D
    J = G // 8

    x0 = x[0]
    tg = tgt[0]

    def vec(name, l):
        return W[name][l:l + 1]

    saves, weights = [], []
    xc, xcb = x0, x0.astype(_MXU)
    for l in range(L):
        gathered = _all_gather("ag", [
            W["ffn1_w_gu"][l].astype(_MXU), W["ffn1_w_down"][l].astype(_MXU), W["w_in"][l].astype(_MXU), W["conv_w"][l],
            W["glu_w"][l].astype(_MXU), W["w_br_ssm"][l].astype(_MXU), W["w_br_gdn"][l].astype(_MXU),
            W["w_out"][l].astype(_MXU), W["ffn2_w_gu"][l].astype(_MXU), W["ffn2_w_down"][l].astype(_MXU)])
        wgu1, wd1, win8, cw8, wglu, wbs, wbg, wo, wgu2, wd2 = gathered
        wd1 = wd1.reshape(-1, D)
        wd2 = wd2.reshape(-1, D)
        wglu = wglu.reshape(SW, SW)
        wo = wo.reshape(D, D)
        win = jnp.transpose(win8, (1, 0, 2)).reshape(D, IN)
        wmain = jnp.concatenate([win[:, :o_b], win[:, o_gs:]], axis=1)
        wba = jnp.pad(win[:, o_b:o_gs], ((0, 0), (0, 128 - 2 * NH)))
        cw = jnp.transpose(cw8, (1, 0, 2)).reshape(KC, 3 * GW)
        wl = dict(wgu1=wgu1, wd1=wd1, wmain=wmain, wba=wba, cw=cw, wglu=wglu, wbs=wbs, wbg=wbg, wo=wo, wgu2=wgu2, wd2=wd2)
        weights.append(wl)
        sv = {}

        gate, up, hh = _ffn_up("ffn_up", xcb, wgu1)
        x1, x1b, xh1, r1 = _mm_ln("ffn_down_ln", hh, wd1, xc, vec("ln1_g", l), vec("ln1_b", l), alpha, 0.5)
        sv["f1"] = dict(xb=xcb, gate=gate, up=up, h=hh, xhat=xh1, rstd=r1)

        p = _mm_nn("mix_in", x1b, wmain, f32)
        pba = _mm_nn("mix_in_ba", x1b, wba, f32)
        b_re_t = jnp.transpose(W["ssm_b_re"][l], (2, 0, 1))
        b_im_t = jnp.transpose(W["ssm_b_im"][l], (2, 0, 1))
        zoh_in = (W["ssm_a_re"][l], W["ssm_a_im"][l], W["ssm_log_dt"][l][:, None], b_re_t, b_im_t)
        lbr, lbi, bbr_t, bbi_t = _zoh_fwd("zoh", *zoh_in)
        bblk_r = _blockdiag(jnp.transpose(bbr_t, (1, 0, 2)))
        bblk_i = _blockdiag(jnp.transpose(bbi_t, (1, 0, 2)))
        cblkT_r = _blockdiag(W["ssm_c_re"][l])
        cblkT_in = _blockdiag(-W["ssm_c_im"][l])
        lbr_f, lbi_f = lbr.reshape(1, G * P), lbi.reshape(1, G * P)
        bur, bui = _bd2("s5_bu", p, 0, bblk_r, bblk_i)
        sr, si = _s5_scan("s5_scan", bur, bui, lbr_f, lbi_f)
        dflat = W["ssm_d"][l].reshape(1, SW)

        def out_epi(acc, ex, outs):
            y_raw = acc + ex[1][...] * ex[0][...]
            yg = jax.nn.gelu(y_raw)
            outs[0][...] = y_raw
            outs[1][...] = yg
            outs[2][...] = yg.astype(outs[2].dtype)

        y_raw, yg, ygb = _bd_sum(
            "s5_out", sr, si, jnp.transpose(cblkT_r, (0, 2, 1)), jnp.transpose(cblkT_in, (0, 2, 1)), [p, dflat],
            lambda tm, nb: [pl.BlockSpec((tm, nb), lambda i, j: (i, j)), pl.BlockSpec((1, nb), lambda i, j: (0, j))],
            [SDS((T, SW), f32), SDS((T, SW), f32), SDS((T, SW), _MXU)], out_epi)

        tmg, tng, tkg = _rtile(T, 512), _tile(SW, 512), _tile(SW, 512)

        def glu_epi(acc, ex, outs):
            lp = acc + ex[1][...]
            outs[0][...] = lp
            outs[1][...] = _glu(ex[0][...], lp).astype(outs[1].dtype)

        lp, ysb = _mm(
            "s5_glu", ygb, wglu, _NN, (T // tmg, SW // tng, SW // tkg),
            pl.BlockSpec((tmg, tkg), lambda i, j, k: (i, k)), pl.BlockSpec((tkg, tng), lambda i, j, k: (k, j)), (tmg, tng),
            [yg, vec("glu_b", l)], [pl.BlockSpec((tmg, tng), lambda i, j, k: (i, j)), pl.BlockSpec((1, tng), lambda i, j, k: (0, j))],
            [SDS((T, SW), f32), SDS((T, SW), _MXU)], [pl.BlockSpec((tmg, tng), lambda i, j, k: (i, j))] * 2, glu_epi)

        qkv = _conv_fwd("gdn_conv", p, m_qkv, cw, 3 * GW)
        blt = jnp.transpose(pba[:, :NH])[:, :, None]
        aint = jnp.transpose(pba[:, NH:2 * NH])[:, :, None]
        alog = W["gdn_a_log"][l].reshape(NH, 1, 1)
        dtb = W["gdn_dt_bias"][l].reshape(NH, 1, 1)
        nw = vec("gdn_norm_w", l)
        og, ssave = _gdn_fwd("gdn", qkv, p, m_z, blt, aint, alog, dtb, nw, NH, HD)

        a_s = _mm(
            "br_ssm", ysb, wbs, _NN, (T // tmg, 8, SW // tkg),
            pl.BlockSpec((tmg, tkg), lambda i, j, k: (i, k)), pl.BlockSpec((None, tkg, DS), lambda i, j, k: (j, k, 0)), (tmg, DS),
            [], [], [SDS((T, D), f32)], [pl.BlockSpec((tmg, DS), lambda i, j, k: (i, j))], _store_epi)[0]
        tkd = _tile(GW, 512)
        gsb, ggb = m_gs // DS, m_gg // DS

        def merge_epi(acc, ex, outs):
            outs[0][...] = acc
            outs[1][...] = _merge(ex[1][...], ex[2][...], ex[0][...], acc).astype(outs[1].dtype)

        tile_ij = pl.BlockSpec((tmg, DS), lambda i, j, k: (i, j))
        a_g, merged = _mm(
            "br_gdn_merge", og, wbg, _NN, (T // tmg, 8, GW // tkd),
            pl.BlockSpec((tmg, tkd), lambda i, j, k: (i, k)), pl.BlockSpec((None, tkd, DS), lambda i, j, k: (j, k, 0)), (tmg, DS),
            [a_s, p, p], [tile_ij, pl.BlockSpec((tmg, DS), lambda i, j, k: (i, j + gsb)),
                          pl.BlockSpec((tmg, DS), lambda i, j, k: (i, j + ggb))],
            [SDS((T, D), f32), SDS((T, D), _MXU)], [tile_ij, tile_ij], merge_epi)
        x2, x2b, xh2, r2 = _mm_ln("mix_out_ln", merged, wo, x1, vec("ln2_g", l), vec("ln2_b", l), alpha, 1.0)
        sv["mx"] = dict(x1b=x1b, p=p, zoh_in=zoh_in, lbr_f=lbr_f, lbi_f=lbi_f, bblk_r=bblk_r, bblk_i=bblk_i, cblkT_r=cblkT_r,
                        cblkT_in=cblkT_in, sr=sr, si=si, dflat=dflat, y_raw=y_raw, yg=yg, ygb=ygb, lp=lp, ysb=ysb, qkv=qkv,
                        blt=blt, aint=aint, alog=alog, dtb=dtb, nw=nw, og=og, ssave=ssave, a_s=a_s, a_g=a_g, merged=merged,
                        xhat=xh2, rstd=r2)

        gate2, up2, hh2 = _ffn_up("ffn_up", x2b, wgu2)
        x3, x3b, xh3, r3 = _mm_ln("ffn_down_ln", hh2, wd2, x2, vec("ln3_g", l), vec("ln3_b", l), alpha, 0.5)
        sv["f2"] = dict(xb=x2b, gate=gate2, up=up2, h=hh2, xhat=xh3, rstd=r3)
        saves.append(sv)
        xc, xcb = x3, x3b

    dy, loss_part = _loss_head("loss_head", xc, tg)
    loss = lax.psum(loss_part[0, 0], ("x", "y", "c"))

    big_out = {n: [None] * L for n in _BIG}
    small_g = {n: [None] * L for n in _SMALL}
    for l in reversed(range(L)):
        sv, wl = saves[l], weights[l]
        mx = sv["mx"]
        p = mx["p"]
        dx2, dwgu2, dwd2, dg3, db3 = _ffn_bwd("ffn_b", dy, sv["f2"], wl["wgu2"], wl["wd2"], vec("ln3_g", l), alpha)
        small_g["ln3_g"][l], small_g["ln3_b"][l] = dg3[0], db3[0]

        dz2, dmixb, dg2, db2 = _ln_bwd("mix_lnb", dx2, mx["xhat"], mx["rstd"], vec("ln2_g", l), 1.0)
        small_g["ln2_g"][l], small_g["ln2_b"][l] = dg2[0], db2[0]
        tmg, tkd = _rtile(T, 512), _tile(D, 512)
        gsb, ggb = m_gs // DS, m_gg // DS

        def dmerge_epi(acc, ex, outs):
            _, vjp = jax.vjp(_merge, ex[0][...], ex[1][...], ex[2][...], ex[3][...])
            dgs, dgg, das, dag = vjp(acc)
            outs[0][...] = das.astype(outs[0].dtype)
            outs[1][...] = dag.astype(outs[1].dtype)
            outs[2][...] = dgs.astype(outs[2].dtype)
            outs[3][...] = dgg.astype(outs[3].dtype)

        tile_ij = pl.BlockSpec((tmg, DS), lambda i, j, k: (i, j))
        das, dag, dgs, dgg = _mm(
            "mix_dmerge", dmixb, wl["wo"], _NT, (T // tmg, 8, D // tkd),
            pl.BlockSpec((tmg, tkd), lambda i, j, k: (i, k)), pl.BlockSpec((DS, tkd), lambda i, j, k: (j, k)), (tmg, DS),
            [p, p, mx["a_s"], mx["a_g"]],
            [pl.BlockSpec((tmg, DS), lambda i, j, k: (i, j + gsb)), pl.BlockSpec((tmg, DS), lambda i, j, k: (i, j + ggb)),
             tile_ij, tile_ij],
            [SDS((T, D), _MXU)] * 4, [tile_ij] * 4, dmerge_epi)
        dwo = _mm_tn("mix_dwo", mx["merged"], dmixb, _GDT)
        dys = _mm_nt_slots("br_ssm_dx", das, wl["wbs"], f32)
        dog = _mm_nt_slots("br_gdn_dx", dag, wl["wbg"], f32)
        dwbs = _mm_tn_slots("br_ssm_dw", mx["ysb"], das, _GDT)
        dwbg = _mm_tn_slots("br_gdn_dw", mx["og"], dag, _GDT)

        def glu_b_fn(dys_t, yg_t, lp_t):
            _, vjp = jax.vjp(_glu, yg_t, lp_t)
            dyg1, dlp = vjp(dys_t)
            return dyg1, dlp, dlp

        dyg1, dlpb, dglub = _colsum_kernel("s5_glu_b", glu_b_fn, [dys, mx["yg"], mx["lp"]], [0, 0, 0], [f32, _MXU], 1, SW)
        small_g["glu_b"][l] = dglub[0]
        dwglu = _mm_tn("s5_dwglu", mx["ygb"], dlpb, _GDT)
        tng, tkg = _tile(SW, 512), _tile(SW, 512)

        def dyraw_epi(acc, ex, outs):
            _, vjp = jax.vjp(jax.nn.gelu, ex[1][...])
            (d,) = vjp(ex[0][...] + acc)
            outs[0][...] = d

        t_ij = pl.BlockSpec((tmg, tng), lambda i, j, k: (i, j))
        (dyraw,) = _mm(
            "s5_dyraw", dlpb, wl["wglu"], _NT, (T // tmg, SW // tng, SW // tkg),
            pl.BlockSpec((tmg, tkg), lambda i, j, k: (i, k)), pl.BlockSpec((tng, tkg), lambda i, j, k: (j, k)), (tmg, tng),
            [dyg1, mx["y_raw"]], [t_ij, t_ij], [SDS((T, SW), f32)], [t_ij], dyraw_epi)

        def dd_fn(dyr, u_t, d_t):
            return d_t * dyr, dyr * u_t

        dud, dd = _colsum_kernel("s5_dd", dd_fn, [dyraw, p, mx["dflat"]], [0, 0, None], [f32], 1, SW)
        small_g["ssm_d"][l] = dd.reshape(G, H)
        dsr, dsi = _bd2("s5_ds", dyraw, 0, mx["cblkT_r"], mx["cblkT_in"])
        dcb_r, dcb_i = _bdT2("s5_dc", mx["sr"], mx["si"], 0, dyraw, dyraw, 0, 8 * P, 8 * H, J)
        small_g["ssm_c_re"][l] = _blockdiag_extract(jnp.transpose(dcb_r, (0, 2, 1)), H, P)
        small_g["ssm_c_im"][l] = -_blockdiag_extract(jnp.transpose(dcb_i, (0, 2, 1)), H, P)
        ar, ai, dlr, dli = _s5_scan_bwd("s5_scan_b", dsr, dsi, mx["sr"], mx["si"], mx["lbr_f"], mx["lbi_f"])

        def du_epi(acc, ex, outs):
            outs[0][...] = (acc + ex[0][...]).astype(outs[0].dtype)

        (du,) = _bd_sum(
            "s5_du", ar, ai, jnp.transpose(mx["bblk_r"], (0, 2, 1)), jnp.transpose(mx["bblk_i"], (0, 2, 1)), [dud],
            lambda tm, nb: [pl.BlockSpec((tm, nb), lambda i, j: (i, j))], [SDS((T, SW), _MXU)], du_epi)
        dbb_r, dbb_i = _bdT2("s5_db", p, p, 0, ar, ai, 0, 8 * H, 8 * P, J)
        dbbr_t = jnp.transpose(_blockdiag_extract(dbb_r, H, P), (1, 0, 2))
        dbbi_t = jnp.transpose(_blockdiag_extract(dbb_i, H, P), (1, 0, 2))
        da_re, da_im, dlog_dt, dbre_t, dbim_t = _zoh_bwd("zoh_b", *mx["zoh_in"], dlr.reshape(G, P), dli.reshape(G, P),
                                                         dbbr_t, dbbi_t)
        small_g["ssm_a_re"][l], small_g["ssm_a_im"][l], small_g["ssm_log_dt"][l] = da_re, da_im, dlog_dt[:, 0]
        small_g["ssm_b_re"][l] = jnp.transpose(dbre_t, (1, 2, 0))
        small_g["ssm_b_im"][l] = jnp.transpose(dbim_t, (1, 2, 0))

        dqkv3, dzb, dbl, dain, dal, ddtb, dnw = _gdn_bwd("gdn_b", mx["qkv"], p, m_z, mx["blt"], mx["aint"], mx["alog"],
                                                         mx["dtb"], mx["nw"], mx["ssave"], dog, NH, HD)
        small_g["gdn_a_log"][l], small_g["gdn_dt_bias"][l], small_g["gdn_norm_w"][l] = dal[:, 0, 0], ddtb[:, 0, 0], dnw[0]
        dqkv_pre, dcw = _conv_bwd("gdn_conv_b", p, m_qkv, wl["cw"], dqkv3)

        dpm = jnp.concatenate([du, dqkv_pre, dzb, dgs, dgg], axis=1)
        dpba = jnp.concatenate([jnp.transpose(dbl[:, :, 0]), jnp.transpose(dain[:, :, 0]),
                                jnp.zeros((T, 128 - 2 * NH), f32)], axis=1).astype(_MXU)
        tnd, tkm = _tile(D, 1024), _tile(NM, 512)
        t_ba = _mm(
            "mix_dx_ba", dpba, wl["wba"], _NT, (T // tmg, D // tnd, 1),
            pl.BlockSpec((tmg, 128), lambda i, j, k: (i, 0)), pl.BlockSpec((tnd, 128), lambda i, j, k: (j, 0)), (tmg, tnd),
            [], [], [SDS((T, D), f32)], [pl.BlockSpec((tmg, tnd), lambda i, j, k: (i, j))], _store_epi)[0]

        def dx1_epi(acc, ex, outs):
            outs[0][...] = alpha * ex[0][...] + ex[1][...] + acc

        t_d = pl.BlockSpec((tmg, tnd), lambda i, j, k: (i, j))
        (dx1,) = _mm(
            "mix_dx", dpm, wl["wmain"], _NT, (T // tmg, D // tnd, NM // tkm),
            pl.BlockSpec((tmg, tkm), lambda i, j, k: (i, k)), pl.BlockSpec((tnd, tkm), lambda i, j, k: (j, k)), (tmg, tnd),
            [dz2, t_ba], [t_d, t_d], [SDS((T, D), f32)], [t_d], dx1_epi)
        tnm = _tile(NM, 1024)
        dwmain = _mm(
            "mix_dw", mx["x1b"], dpm, _TN, (D // tkd, NM // tnm, T // tmg),
            pl.BlockSpec((tmg, tkd), lambda i, j, k: (k, i)), pl.BlockSpec((tmg, tnm), lambda i, j, k: (k, j)), (tkd, tnm),
            [], [], [SDS((D, NM), _GDT)], [pl.BlockSpec((tkd, tnm), lambda i, j, k: (i, j))], _store_epi)[0]
        dwba = _mm_tn("mix_dw_ba", mx["x1b"], dpba, _GDT)
        dwin = jnp.concatenate([dwmain[:, :o_b], dwba[:, :2 * NH], dwmain[:, o_b:]], axis=1)
        dwin8 = jnp.transpose(dwin.reshape(D, 8, IN // 8), (1, 0, 2))
        dcw8 = jnp.transpose(dcw.reshape(KC, 8, 3 * GW // 8), (1, 0, 2))

        dx0, dwgu1, dwd1, dg1, db1 = _ffn_bwd("ffn_b", dx1, sv["f1"], wl["wgu1"], wl["wd1"], vec("ln1_g", l), alpha)
        small_g["ln1_g"][l], small_g["ln1_b"][l] = dg1[0], db1[0]
        dy = dx0

        parts = [dwgu1, dwd1.reshape(8, -1, D), dwin8, dcw8, dwglu.reshape(8, SW // 8, SW), dwbs, dwbg,
                 dwo.reshape(8, DS, D), dwgu2, dwd2.reshape(8, -1, D)]
        pairs = _reduce_scatter("rs", parts)
        for n, (ka, kb) in zip(_BIG, pairs):
            big_out[n][l] = _adamw_big("adamw_" + n, ka, kb, W[n], M[n], V[n], l)

    def pack(arrs):
        flat = jnp.concatenate([a.reshape(-1) for a in arrs])
        n = flat.shape[0]
        rows = -(-n // (128 * 16)) * 16
        return jnp.pad(flat, (0, rows * 128 - n)).reshape(rows, 128)

    gs_full = [jnp.stack(small_g[n]).reshape(W[n].shape) for n in _SMALL]
    gpack = pack(gs_full)
    (gall,) = _all_gather("ag_small", [gpack])
    sg, sd, sm, sv_ = _adamw_small("adamw_small", gall, pack([W[n] for n in _SMALL]), pack([M[n] for n in _SMALL]),
                                   pack([V[n] for n in _SMALL]))

    def unpack(packed):
        flat = packed.reshape(-1)
        out, off = {}, 0
        for n in _SMALL:
            sz = math.prod(W[n].shape)
            out[n] = flat[off:off + sz].reshape(W[n].shape)
            off += sz
        return out

    res = [unpack(a) for a in (sg, sd, sm, sv_)]
    for n in _BIG:
        for i in range(4):
            res[i][n] = jnp.stack([big_out[n][l][i] for l in range(L)])
    outs = [loss, dy[None]]
    for i in range(4):
        outs += [res[i][n] for n in _ORDER]
    return tuple(outs)


def kernel(x, ffn1_w_gu, ffn1_w_down, ln1_g, ln1_b, w_in, conv_w, ssm_a_re, ssm_a_im, ssm_log_dt, ssm_b_re, ssm_b_im, ssm_c_re, ssm_c_im, ssm_d, glu_w, glu_b, gdn_a_log, gdn_dt_bias, gdn_norm_w, w_br_ssm, w_br_gdn, w_out, ln2_g, ln2_b, ffn2_w_gu, ffn2_w_down, ln3_g, ln3_b, loss_target, m_ffn1_w_gu, m_ffn1_w_down, m_ln1_g, m_ln1_b, m_w_in, m_conv_w, m_ssm_a_re, m_ssm_a_im, m_ssm_log_dt, m_ssm_b_re, m_ssm_b_im, m_ssm_c_re, m_ssm_c_im, m_ssm_d, m_glu_w, m_glu_b, m_gdn_a_log, m_gdn_dt_bias, m_gdn_norm_w, m_w_br_ssm, m_w_br_gdn, m_w_out, m_ln2_g, m_ln2_b, m_ffn2_w_gu, m_ffn2_w_down, m_ln3_g, m_ln3_b, v_ffn1_w_gu, v_ffn1_w_down, v_ln1_g, v_ln1_b, v_w_in, v_conv_w, v_ssm_a_re, v_ssm_a_im, v_ssm_log_dt, v_ssm_b_re, v_ssm_b_im, v_ssm_c_re, v_ssm_c_im, v_ssm_d, v_glu_w, v_glu_b, v_gdn_a_log, v_gdn_dt_bias, v_gdn_norm_w, v_w_br_ssm, v_w_br_gdn, v_w_out, v_ln2_g, v_ln2_b, v_ffn2_w_gu, v_ffn2_w_down, v_ln3_g, v_ln3_b):
    given = dict(locals())
    W = {n: given[n] for n in _ORDER}
    M = {n: given["m_" + n] for n in _ORDER}
    V = {n: given["v_" + n] for n in _ORDER}
    return _step(x, loss_target, W, M, V)
```

```python
import functools
import math

import jax
import jax.numpy as jnp
from jax import lax
from jax.experimental import pallas as pl
from jax.experimental.pallas import tpu as pltpu

f32 = jnp.float32
_MXU = jnp.bfloat16
_GDT = jnp.bfloat16
_HP = lax.Precision.HIGHEST
_VMEM_LIMIT = 56 * 1024 * 1024
_MESH_T = pl.DeviceIdType.MESH

LN_EPS = 1e-5
RMS_EPS = 1e-6
L2_EPS = 1e-6
CHUNK = 64
ADAM_LR = 0.001
ADAM_B1 = 0.9
ADAM_B2 = 0.999
ADAM_EPS = 1e-08
ADAM_WD = 0.01
ADAM_STEP = 10

_NN = (((1,), (0,)), ((), ()))
_NT = (((1,), (1,)), ((), ()))
_TN = (((0,), (0,)), ((), ()))

SDS = jax.ShapeDtypeStruct


def _cp(sem):
    return pltpu.CompilerParams(dimension_semantics=sem, vmem_limit_bytes=_VMEM_LIMIT)


def _tile(n, pref):
    if n <= pref:
        return n
    t = (pref // 128) * 128
    while t >= 128:
        if n % t == 0:
            return t
        t -= 128
    return n


def _rtile(n, pref):
    if n <= pref:
        return n
    t = (pref // 16) * 16
    while t >= 16:
        if n % t == 0:
            return t
        t -= 16
    return n


def _mm(name, a, b, dims, grid, a_spec, b_spec, acc_shape, extras, extra_specs, out_shape, out_specs, epilogue):
    nk = grid[2]
    ne = len(extras)
    no = len(out_shape)

    def body(*refs):
        a_ref, b_ref = refs[0], refs[1]
        ex = refs[2:2 + ne]
        outs = refs[2 + ne:2 + ne + no]
        acc = refs[-1]
        k = pl.program_id(2)
        part = lax.dot_general(a_ref[...].astype(_MXU), b_ref[...].astype(_MXU), dims, preferred_element_type=f32)

        @pl.when(k == 0)
        def _():
            acc[...] = part

        @pl.when(k > 0)
        def _():
            acc[...] += part

        @pl.when(k == nk - 1)
        def _():
            epilogue(acc[...], ex, outs)

    return pl.pallas_call(
        body, grid=grid, in_specs=[a_spec, b_spec, *extra_specs], out_specs=list(out_specs), out_shape=list(out_shape),
        scratch_shapes=[pltpu.VMEM(acc_shape, f32)], compiler_params=_cp(("parallel", "parallel", "arbitrary")), name=name,
    )(a, b, *extras)


def _store_epi(acc, ex, outs):
    for o in outs:
        o[...] = acc.astype(o.dtype)


def _ln_epilogue(alpha, c):
    def epi(acc, ex, outs):
        x_ref, g_ref, b_ref = ex
        y_ref, yb_ref, xh_ref, r_ref = outs
        z = alpha * x_ref[...] + c * acc
        mu = jnp.mean(z, axis=-1, keepdims=True)
        zc = z - mu
        var = jnp.mean(zc * zc, axis=-1, keepdims=True)
        r = lax.rsqrt(var + LN_EPS)
        xh = zc * r
        y = xh * g_ref[...] + b_ref[...]
        y_ref[...] = y
        yb_ref[...] = y.astype(yb_ref.dtype)
        xh_ref[...] = xh
        r_ref[...] = r
    return epi


def _mm_ln(name, a, w, x, g, b, alpha, c):
    T, K = a.shape
    D = w.shape[1]
    tm, tk = _rtile(T, 512), _tile(K, 512)
    row = pl.BlockSpec((tm, D), lambda i, j, k: (i, 0))
    vec = pl.BlockSpec((1, D), lambda i, j, k: (0, 0))
    return _mm(
        name, a, w, _NN, (T // tm, 1, K // tk),
        pl.BlockSpec((tm, tk), lambda i, j, k: (i, k)), pl.BlockSpec((tk, D), lambda i, j, k: (k, 0)), (tm, D),
        [x, g, b], [row, vec, vec],
        [SDS((T, D), f32), SDS((T, D), _MXU), SDS((T, D), f32), SDS((T, 1), f32)],
        [row, row, row, pl.BlockSpec((tm, 1), lambda i, j, k: (i, 0))],
        _ln_epilogue(alpha, c),
    )


def _ln_bwd(name, dy, xhat, rstd, g, c):
    T, D = dy.shape
    tm = _rtile(T, 256)

    def body(dy_ref, xh_ref, r_ref, g_ref, dz_ref, df_ref, dg_ref, db_ref):
        i = pl.program_id(0)
        dyv = dy_ref[...]
        xh = xh_ref[...]
        dxh = dyv * g_ref[...]
        m1 = jnp.mean(dxh, axis=-1, keepdims=True)
        m2 = jnp.mean(dxh * xh, axis=-1, keepdims=True)
        dz = r_ref[...] * (dxh - m1 - xh * m2)
        dz_ref[...] = dz
        df_ref[...] = (c * dz).astype(df_ref.dtype)
        pg = jnp.sum(dyv * xh, axis=0, keepdims=True)
        pb = jnp.sum(dyv, axis=0, keepdims=True)

        @pl.when(i == 0)
        def _():
            dg_ref[...] = pg
            db_ref[...] = pb

        @pl.when(i > 0)
        def _():
            dg_ref[...] += pg
            db_ref[...] += pb

    row = pl.BlockSpec((tm, D), lambda i: (i, 0))
    vec = pl.BlockSpec((1, D), lambda i: (0, 0))
    return pl.pallas_call(
        body, grid=(T // tm,), in_specs=[row, row, pl.BlockSpec((tm, 1), lambda i: (i, 0)), vec],
        out_specs=[row, row, vec, vec],
        out_shape=[SDS((T, D), f32), SDS((T, D), _MXU), SDS((1, D), f32), SDS((1, D), f32)],
        compiler_params=_cp(("arbitrary",)), name=name,
    )(dy, xhat, rstd, g)


def _swiglu(g, u):
    return jax.nn.silu(g) * u


def _ffn_up(name, xb, wgu):
    T, D = xb.shape
    FS = wgu.shape[2]
    F = 4 * FS
    tm = _rtile(T, 256)

    def body(x_ref, wg_ref, wu_ref, g_ref, u_ref, h_ref):
        xv = x_ref[...]
        g = jnp.dot(xv, wg_ref[...], preferred_element_type=f32)
        u = jnp.dot(xv, wu_ref[...], preferred_element_type=f32)
        g_ref[...] = g
        u_ref[...] = u
        h_ref[...] = _swiglu(g, u).astype(h_ref.dtype)

    out = pl.BlockSpec((tm, FS), lambda j, i: (i, j))
    return pl.pallas_call(
        body, grid=(4, T // tm),
        in_specs=[pl.BlockSpec((tm, D), lambda j, i: (i, 0)),
                  pl.BlockSpec((None, D, FS), lambda j, i: (j, 0, 0)),
                  pl.BlockSpec((None, D, FS), lambda j, i: (j + 4, 0, 0))],
        out_specs=[out, out, out],
        out_shape=[SDS((T, F), f32), SDS((T, F), f32), SDS((T, F), _MXU)],
        compiler_params=_cp(("parallel", "arbitrary")), name=name,
    )(xb, wgu, wgu)


def _ffn_bwd(pfx, dy, sv, wgu, wd, g_ln, alpha):
    T, D = dy.shape
    FS = wgu.shape[2]
    F = 4 * FS
    dz, dfb, dg, db = _ln_bwd(pfx + "_lnb", dy, sv["xhat"], sv["rstd"], g_ln, 0.5)

    tm, tn, tk = _rtile(T, 512), _tile(F, 512), _tile(D, 2048)

    def epi(acc, ex, outs):
        g_ref, u_ref = ex
        _, vjp = jax.vjp(_swiglu, g_ref[...], u_ref[...])
        dgate, dup = vjp(acc)
        outs[0][0] = dgate.astype(outs[0].dtype)
        outs[0][1] = dup.astype(outs[0].dtype)

    gu = pl.BlockSpec((tm, tn), lambda i, j, k: (i, j))
    (dgu,) = _mm(
        pfx + "_dh", dfb, wd, _NT, (T // tm, F // tn, D // tk),
        pl.BlockSpec((tm, tk), lambda i, j, k: (i, k)), pl.BlockSpec((tn, tk), lambda i, j, k: (j, k)), (tm, tn),
        [sv["gate"], sv["up"]], [gu, gu],
        [SDS((2, T, F), _MXU)], [pl.BlockSpec((2, tm, tn), lambda i, j, k: (0, i, j))], epi,
    )

    tm2, tk2 = _tile(F, 512), _rtile(T, 512)
    (dwd,) = _mm(
        pfx + "_dwd", sv["h"], dfb, _TN, (F // tm2, 1, T // tk2),
        pl.BlockSpec((tk2, tm2), lambda i, j, k: (k, i)), pl.BlockSpec((tk2, D), lambda i, j, k: (k, 0)), (tm2, D),
        [], [], [SDS((F, D), _GDT)], [pl.BlockSpec((tm2, D), lambda i, j, k: (i, 0))], _store_epi,
    )

    tn3 = _tile(D, 1024)

    def epi3(acc, ex, outs):
        outs[0][...] = alpha * ex[0][...] + acc

    (dx,) = _mm(
        pfx + "_dx", dgu, wgu, _NT, (T // tm, D // tn3, 8),
        pl.BlockSpec((None, tm, FS), lambda i, j, k: (k // 4, i, k % 4)),
        pl.BlockSpec((None, tn3, FS), lambda i, j, k: (k, j, 0)), (tm, tn3),
        [dz], [pl.BlockSpec((tm, tn3), lambda i, j, k: (i, j))],
        [SDS((T, D), f32)], [pl.BlockSpec((tm, tn3), lambda i, j, k: (i, j))], epi3,
    )

    tm4, tk4 = _rtile(D, 512), _tile(T, 1024)
    (dwgu,) = _mm(
        pfx + "_dwgu", jnp.transpose(sv["xb"]), dgu, _NN, (D // tm4, 8, T // tk4),
        pl.BlockSpec((tm4, tk4), lambda i, j, k: (i, k)),
        pl.BlockSpec((None, tk4, FS), lambda i, j, k: (j // 4, k, j % 4)), (tm4, FS),
        [], [], [SDS((8, D, FS), _GDT)], [pl.BlockSpec((None, tm4, FS), lambda i, j, k: (j, i, 0))], _store_epi,
    )
    return dx, dwgu, dwd, dg, db


def _zoh(a_re, a_im, log_dt, b_re_t, b_im_t):
    dt = jnp.exp(log_dt)
    mag = jnp.exp(a_re * dt)
    lr_, li_ = mag * jnp.cos(a_im * dt), mag * jnp.sin(a_im * dt)
    den = a_re * a_re + a_im * a_im
    pr, pi = lr_ - 1.0, li_
    qr, qi = a_re / den, -a_im / den
    zr, zi = pr * qr - pi * qi, pr * qi + pi * qr
    bbr = zr[None] * b_re_t - zi[None] * b_im_t
    bbi = zr[None] * b_im_t + zi[None] * b_re_t
    return lr_, li_, bbr, bbi


def _zoh_fwd(name, a_re, a_im, log_dt, b_re_t, b_im_t):
    G, P = a_re.shape
    H = b_re_t.shape[0]

    def body(ar, ai, ld, br, bi, o1, o2, o3, o4):
        r = _zoh(ar[...], ai[...], ld[...], br[...], bi[...])
        o1[...], o2[...], o3[...], o4[...] = r

    return pl.pallas_call(
        body, out_shape=[SDS((G, P), f32), SDS((G, P), f32), SDS((H, G, P), f32), SDS((H, G, P), f32)], name=name,
    )(a_re, a_im, log_dt, b_re_t, b_im_t)


def _zoh_bwd(name, a_re, a_im, log_dt, b_re_t, b_im_t, dlr, dli, dbbr, dbbi):
    G, P = a_re.shape
    H = b_re_t.shape[0]

    def body(ar, ai, ld, br, bi, g1, g2, g3, g4, o1, o2, o3, o4, o5):
        _, vjp = jax.vjp(_zoh, ar[...], ai[...], ld[...], br[...], bi[...])
        r = vjp((g1[...], g2[...], g3[...], g4[...]))
        o1[...], o2[...], o3[...], o4[...], o5[...] = r

    return pl.pallas_call(
        body, out_shape=[SDS((G, P), f32), SDS((G, P), f32), SDS((G, 1), f32), SDS((H, G, P), f32), SDS((H, G, P), f32)],
        name=name,
    )(a_re, a_im, log_dt, b_re_t, b_im_t, dlr, dli, dbbr, dbbi)


def _blockdiag(m):
    G, A, B = m.shape
    eye = jnp.eye(8, dtype=bool)
    m4 = m.reshape(G // 8, 8, A, B)
    out = jnp.where(eye[None, :, None, :, None], m4[:, :, :, None, :], jnp.zeros((), m.dtype))
    return out.reshape(G // 8, 8 * A, 8 * B)


def _blockdiag_extract(mb, A, B):
    J = mb.shape[0]
    m5 = mb.reshape(J, 8, A, 8, B)
    d = jnp.stack([m5[:, i, :, i, :] for i in range(8)], axis=1)
    return d.reshape(J * 8, A, B)


def _bd2(name, a, a_col0, b1, b2, out_dtype=f32):
    T = a.shape[0]
    J, KA, NB = b1.shape
    tm = _rtile(T, 512)

    def body(a_ref, b1_ref, b2_ref, o1, o2):
        av = a_ref[...].astype(_MXU)
        o1[...] = jnp.dot(av, b1_ref[...].astype(_MXU), preferred_element_type=f32).astype(o1.dtype)
        o2[...] = jnp.dot(av, b2_ref[...].astype(_MXU), preferred_element_type=f32).astype(o2.dtype)

    bs = pl.BlockSpec((None, KA, NB), lambda i, j: (j, 0, 0))
    os_ = pl.BlockSpec((tm, NB), lambda i, j: (i, j))
    return pl.pallas_call(
        body, grid=(T // tm, J), in_specs=[pl.BlockSpec((tm, KA), lambda i, j: (i, j + a_col0)), bs, bs],
        out_specs=[os_, os_], out_shape=[SDS((T, J * NB), out_dtype)] * 2,
        compiler_params=_cp(("parallel", "parallel")), name=name,
    )(a, b1, b2)


def _bd_sum(name, a1, a2, b1, b2, extras, extra_specs_fn, out_shape, epilogue):
    T = a1.shape[0]
    J, KA, NB = b1.shape
    tm = _rtile(T, 512)
    ne = len(extras)

    def body(*refs):
        a1_ref, a2_ref, b1_ref, b2_ref = refs[:4]
        ex = refs[4:4 + ne]
        outs = refs[4 + ne:]
        acc = jnp.dot(a1_ref[...].astype(_MXU), b1_ref[...].astype(_MXU), preferred_element_type=f32)
        acc = acc + jnp.dot(a2_ref[...].astype(_MXU), b2_ref[...].astype(_MXU), preferred_element_type=f32)
        epilogue(acc, ex, outs)

    as_ = pl.BlockSpec((tm, KA), lambda i, j: (i, j))
    bs = pl.BlockSpec((None, KA, NB), lambda i, j: (j, 0, 0))
    os_ = pl.BlockSpec((tm, NB), lambda i, j: (i, j))
    return pl.pallas_call(
        body, grid=(T // tm, J), in_specs=[as_, as_, bs, bs, *extra_specs_fn(tm, NB)],
        out_specs=[os_] * len(out_shape), out_shape=list(out_shape),
        compiler_params=_cp(("parallel", "parallel")), name=name,
    )(a1, a2, b1, b2, *extras)


def _bdT2(name, a1, a2, a_col0, b1, b2, b_col0, KA, NB, J):
    T = a1.shape[0]
    tk = _rtile(T, 512)

    def body(a1_ref, a2_ref, b1_ref, b2_ref, o1, o2):
        k = pl.program_id(1)
        p1 = lax.dot_general(a1_ref[...].astype(_MXU), b1_ref[...].astype(_MXU), _TN, preferred_element_type=f32)
        p2 = lax.dot_general(a2_ref[...].astype(_MXU), b2_ref[...].astype(_MXU), _TN, preferred_element_type=f32)

        @pl.when(k == 0)
        def _():
            o1[...] = p1
            o2[...] = p2

        @pl.when(k > 0)
        def _():
            o1[...] += p1
            o2[...] += p2

    as_ = pl.BlockSpec((tk, KA), lambda j, k: (k, j + a_col0))
    bs = pl.BlockSpec((tk, NB), lambda j, k: (k, j + b_col0))
    os_ = pl.BlockSpec((None, KA, NB), lambda j, k: (j, 0, 0))
    return pl.pallas_call(
        body, grid=(J, T // tk), in_specs=[as_, as_, bs, bs], out_specs=[os_, os_],
        out_shape=[SDS((J, KA, NB), f32)] * 2, compiler_params=_cp(("parallel", "arbitrary")), name=name,
    )(a1, a2, b1, b2)


def _s5_scan(name, bur, bui, lr_, li_):
    T, N = bur.shape
    cb = _tile(N, 512)

    def body(br_ref, bi_ref, lr_ref, li_ref, sr_ref, si_ref):
        lr_v, li_v = lr_ref[...], li_ref[...]

        def step(t, carry):
            sr, si = carry
            nr = lr_v * sr - li_v * si + br_ref[pl.ds(t, 1), :]
            ni = lr_v * si + li_v * sr + bi_ref[pl.ds(t, 1), :]
            sr_ref[pl.ds(t, 1), :] = nr
            si_ref[pl.ds(t, 1), :] = ni
            return nr, ni

        z = jnp.zeros((1, cb), f32)
        lax.fori_loop(0, T, step, (z, z))

    col = pl.BlockSpec((T, cb), lambda j: (0, j))
    vec = pl.BlockSpec((1, cb), lambda j: (0, j))
    return pl.pallas_call(
        body, grid=(N // cb,), in_specs=[col, col, vec, vec], out_specs=[col, col],
        out_shape=[SDS((T, N), f32)] * 2, compiler_params=_cp(("parallel",)), name=name,
    )(bur, bui, lr_, li_)


def _s5_scan_bwd(name, dsr, dsi, sr, si, lr_, li_):
    T, N = dsr.shape
    cb = _tile(N, 256)

    def body(dr_ref, di_ref, sr_ref, si_ref, lr_ref, li_ref, ar_ref, ai_ref, glr_ref, gli_ref):
        lr_v, li_v = lr_ref[...], li_ref[...]

        def step(n, carry):
            ar, ai, glr, gli = carry
            t = T - 1 - n
            nr = dr_ref[pl.ds(t, 1), :] + lr_v * ar + li_v * ai
            ni = di_ref[pl.ds(t, 1), :] + lr_v * ai - li_v * ar
            ar_ref[pl.ds(t, 1), :] = nr
            ai_ref[pl.ds(t, 1), :] = ni
            pr = sr_ref[pl.ds(t - 1, 1), :]
            pi = si_ref[pl.ds(t - 1, 1), :]
            glr = glr + nr * pr + ni * pi
            gli = gli + ni * pr - nr * pi
            return nr, ni, glr, gli

        z = jnp.zeros((1, cb), f32)
        ar, ai, glr, gli = lax.fori_loop(0, T - 1, step, (z, z, z, z))
        ar_ref[pl.ds(0, 1), :] = dr_ref[pl.ds(0, 1), :] + lr_v * ar + li_v * ai
        ai_ref[pl.ds(0, 1), :] = di_ref[pl.ds(0, 1), :] + lr_v * ai - li_v * ar
        glr_ref[...] = glr
        gli_ref[...] = gli

    col = pl.BlockSpec((T, cb), lambda j: (0, j))
    vec = pl.BlockSpec((1, cb), lambda j: (0, j))
    return pl.pallas_call(
        body, grid=(N // cb,), in_specs=[col, col, col, col, vec, vec], out_specs=[col, col, vec, vec],
        out_shape=[SDS((T, N), f32), SDS((T, N), f32), SDS((1, N), f32), SDS((1, N), f32)],
        compiler_params=_cp(("parallel",)), name=name,
    )(dsr, dsi, sr, si, lr_, li_)


def _conv_fwd(name, p, col0, w, GW3):
    T = p.shape[0]
    K = w.shape[0]
    cb = 128
    c0 = col0 // cb

    def body(x_ref, w_ref, o_ref, pad_ref):
        pad_ref[pl.ds(0, 8), :] = jnp.zeros((8, cb), f32)
        pad_ref[pl.ds(8, T), :] = x_ref[...]
        wv = w_ref[...]
        acc = jnp.zeros((T, cb), f32)
        for j in range(K):
            acc = acc + wv[j:j + 1, :] * pad_ref[pl.ds(8 - (K - 1) + j, T), :]
        o_ref[...] = jax.nn.silu(acc)

    return pl.pallas_call(
        body, grid=(GW3 // cb,),
        in_specs=[pl.BlockSpec((T, cb), lambda j: (0, j + c0)), pl.BlockSpec((K, cb), lambda j: (0, j))],
        out_specs=pl.BlockSpec((T, cb), lambda j: (0, j)), out_shape=SDS((T, GW3), f32),
        scratch_shapes=[pltpu.VMEM((T + 8, cb), f32)], compiler_params=_cp(("parallel",)), name=name,
    )(p, w)


def _conv_bwd(name, p, col0, w, dout3):
    T = p.shape[0]
    K = w.shape[0]
    GW = dout3.shape[2]
    GW3 = 3 * GW
    cb = 128
    c0 = col0 // cb
    nb = GW // cb

    def body(x_ref, w_ref, d_ref, dx_ref, dw_ref, pad_ref, dpad_ref):
        pad_ref[pl.ds(0, 8), :] = jnp.zeros((8, cb), f32)
        pad_ref[pl.ds(8, T), :] = x_ref[...]
        wv = w_ref[...]
        pre = jnp.zeros((T, cb), f32)
        for j in range(K):
            pre = pre + wv[j:j + 1, :] * pad_ref[pl.ds(8 - (K - 1) + j, T), :]
        _, vjp = jax.vjp(jax.nn.silu, pre)
        (dpre,) = vjp(d_ref[...])
        dpad_ref[pl.ds(0, T), :] = dpre
        dpad_ref[pl.ds(T, 8), :] = jnp.zeros((8, cb), f32)
        dx = jnp.zeros((T, cb), f32)
        rows = []
        for j in range(K):
            dx = dx + wv[j:j + 1, :] * dpad_ref[pl.ds((K - 1) - j, T), :]
            rows.append(jnp.sum(dpre * pad_ref[pl.ds(8 - (K - 1) + j, T), :], axis=0, keepdims=True))
        dx_ref[...] = dx.astype(dx_ref.dtype)
        for j in range(K):
            dw_ref[pl.ds(j, 1), :] = rows[j]

    return pl.pallas_call(
        body, grid=(GW3 // cb,),
        in_specs=[pl.BlockSpec((T, cb), lambda j: (0, j + c0)), pl.BlockSpec((K, cb), lambda j: (0, j)),
                  pl.BlockSpec((None, T, cb), lambda j: (j // nb, 0, j % nb))],
        out_specs=[pl.BlockSpec((T, cb), lambda j: (0, j)), pl.BlockSpec((K, cb), lambda j: (0, j))],
        out_shape=[SDS((T, GW3), _MXU), SDS((K, GW3), f32)],
        scratch_shapes=[pltpu.VMEM((T + 8, cb), f32), pltpu.VMEM((T + 8, cb), f32)],
        compiler_params=_cp(("parallel",)), name=name,
    )(p, w, dout3)


def _hdot(a, b, dims=_NN):
    return lax.dot_general(a, b, dims, precision=_HP, preferred_element_type=f32)


def _ldot(a, b, dims=_NN):
    return lax.dot_general(a.astype(_MXU), b.astype(_MXU), dims, preferred_element_type=f32)


def _gdn_chunk(S, q, k, v, z, bl, ain, alog, dtb, nw):
    C, d = q.shape
    ri = lax.broadcasted_iota(jnp.int32, (C, C), 0)
    ci = lax.broadcasted_iota(jnp.int32, (C, C), 1)
    causal = ri >= ci
    strict = ri > ci
    tri = causal.astype(f32)
    qn = q * lax.rsqrt(jnp.sum(q * q, axis=-1, keepdims=True) + L2_EPS) * (d ** -0.5)
    kn = k * lax.rsqrt(jnp.sum(k * k, axis=-1, keepdims=True) + L2_EPS)
    beta = jax.nn.sigmoid(bl)
    g = -jnp.exp(alog) * jax.nn.softplus(ain + dtb)
    gb = jnp.broadcast_to(g, (C, C))
    gc_col = _hdot(tri, gb)
    gc_row = _hdot(jnp.ones((C, C), f32), jnp.where(ri <= ci, gb, 0.0))
    diff = jnp.where(causal, gc_col - gc_row, 0.0)
    decay = jnp.where(causal, jnp.exp(diff), 0.0)
    gcum = gc_col[:, 0:1]
    glast = gc_col[C - 1:C, 0:1]
    egc = jnp.exp(gcum)
    kb = kn * beta
    lower = jnp.where(strict, _ldot(kb, kn, _NT) * decay, 0.0)
    x = jnp.concatenate([v * beta, kb * egc], axis=-1)
    m = -lower
    for it in range(6):
        x = x + _hdot(m, x)
        if it < 5:
            m = _hdot(m, m)
    u_val, w_key = x[:, :d], x[:, d:]
    attn = _ldot(qn, kn, _NT) * decay
    q_dec = qn * egc
    k_dec = kn * jnp.exp(glast - gcum)
    v_new = u_val - _ldot(w_key, S)
    out = _ldot(q_dec, S) + _ldot(attn, v_new)
    s_new = S * jnp.exp(glast) + _ldot(k_dec, v_new, _TN)
    o = out * lax.rsqrt(jnp.mean(out * out, axis=-1, keepdims=True) + RMS_EPS) * nw
    o = o * jax.nn.silu(z)
    return s_new, o


def _heads_per_step(NH, HD, zcol0):
    for hb in (4, 2):
        if NH % hb == 0 and zcol0 % (hb * HD) == 0:
            return hb
    return 1


def _gdn_fwd(name, qkv, p, zcol0, blt, aint, alog, dtb, nw, NH, HD):
    T = qkv.shape[0]
    N = T // CHUNK
    GW = NH * HD
    HB = _heads_per_step(NH, HD, zcol0)
    W = HB * HD
    zc0 = zcol0 // W
    nb = GW // W

    def body(q_ref, k_ref, v_ref, z_ref, bl_ref, ain_ref, al_ref, dtb_ref, nw_ref, o_ref, ssave_ref, s_scr):
        n = pl.program_id(1)

        @pl.when(n == 0)
        def _():
            s_scr[...] = jnp.zeros_like(s_scr)

        for hh in range(HB):
            cs = slice(hh * HD, (hh + 1) * HD)
            s_in = s_scr[hh]
            ssave_ref[hh] = s_in
            s_new, o = _gdn_chunk(s_in, q_ref[:, cs], k_ref[:, cs], v_ref[:, cs], z_ref[:, cs], bl_ref[hh], ain_ref[hh],
                                  al_ref[hh], dtb_ref[hh], nw_ref[...])
            s_scr[hh] = s_new
            o_ref[:, cs] = o.astype(o_ref.dtype)

    ch = lambda off: pl.BlockSpec((CHUNK, W), lambda h, n: (n, h + off))
    sc = pl.BlockSpec((HB, CHUNK, 1), lambda h, n: (h, n, 0))
    hs = pl.BlockSpec((HB, 1, 1), lambda h, n: (h, 0, 0))
    return pl.pallas_call(
        body, grid=(NH // HB, N),
        in_specs=[ch(0), ch(nb), ch(2 * nb), ch(zc0), sc, sc, hs, hs, pl.BlockSpec((1, HD), lambda h, n: (0, 0))],
        out_specs=[pl.BlockSpec((CHUNK, W), lambda h, n: (n, h)),
                   pl.BlockSpec((HB, None, HD, HD), lambda h, n: (h, n, 0, 0))],
        out_shape=[SDS((T, GW), _MXU), SDS((NH, N, HD, HD), f32)],
        scratch_shapes=[pltpu.VMEM((HB, HD, HD), f32)], compiler_params=_cp(("parallel", "arbitrary")), name=name,
    )(qkv, qkv, qkv, p, blt, aint, alog, dtb, nw)


def _gdn_bwd(name, qkv, p, zcol0, blt, aint, alog, dtb, nw, ssave, do, NH, HD):
    T = qkv.shape[0]
    N = T // CHUNK
    GW = NH * HD
    HB = _heads_per_step(NH, HD, zcol0)
    W = HB * HD
    zc0 = zcol0 // W
    nb = GW // W

    def body(q_ref, k_ref, v_ref, z_ref, bl_ref, ain_ref, al_ref, dtb_ref, nw_ref, ss_ref, do_ref,
             dqkv_ref, dz_ref, dbl_ref, dain_ref, dal_ref, ddtb_ref, dnw_ref, ds_scr):
        h = pl.program_id(0)
        n = pl.program_id(1)

        @pl.when(n == 0)
        def _():
            ds_scr[...] = jnp.zeros_like(ds_scr)

        dnw_sum = jnp.zeros((1, HD), f32)
        for hh in range(HB):
            cs = slice(hh * HD, (hh + 1) * HD)
            _, vjp = jax.vjp(_gdn_chunk, ss_ref[hh], q_ref[:, cs], k_ref[:, cs], v_ref[:, cs], z_ref[:, cs], bl_ref[hh],
                             ain_ref[hh], al_ref[hh], dtb_ref[hh], nw_ref[...])
            ds, dq, dk, dv, dz, dbl, dain, dal, ddtb, dnw = vjp((ds_scr[hh], do_ref[:, cs].astype(f32)))
            ds_scr[hh] = ds
            dqkv_ref[0, :, cs] = dq
            dqkv_ref[1, :, cs] = dk
            dqkv_ref[2, :, cs] = dv
            dz_ref[:, cs] = dz.astype(dz_ref.dtype)
            dbl_ref[hh] = dbl
            dain_ref[hh] = dain
            dnw_sum = dnw_sum + dnw

            @pl.when(n == 0)
            def _(hh=hh, dal=dal, ddtb=ddtb):
                dal_ref[hh] = dal
                ddtb_ref[hh] = ddtb

            @pl.when(n > 0)
            def _(hh=hh, dal=dal, ddtb=ddtb):
                dal_ref[hh] += dal
                ddtb_ref[hh] += ddtb

        @pl.when((n == 0) & (h == 0))
        def _():
            dnw_ref[...] = dnw_sum

        @pl.when((n > 0) | (h > 0))
        def _():
            dnw_ref[...] += dnw_sum

    R = N - 1
    ch = lambda off: pl.BlockSpec((CHUNK, W), lambda h, n: (R - n, h + off))
    sc = pl.BlockSpec((HB, CHUNK, 1), lambda h, n: (h, R - n, 0))
    hs = pl.BlockSpec((HB, 1, 1), lambda h, n: (h, 0, 0))
    nws = pl.BlockSpec((1, HD), lambda h, n: (0, 0))
    return pl.pallas_call(
        body, grid=(NH // HB, N),
        in_specs=[ch(0), ch(nb), ch(2 * nb), ch(zc0), sc, sc, hs, hs, nws,
                  pl.BlockSpec((HB, None, HD, HD), lambda h, n: (h, R - n, 0, 0)),
                  pl.BlockSpec((CHUNK, W), lambda h, n: (R - n, h))],
        out_specs=[pl.BlockSpec((3, CHUNK, W), lambda h, n: (0, R - n, h)),
                   pl.BlockSpec((CHUNK, W), lambda h, n: (R - n, h)), sc, sc, hs, hs, nws],
        out_shape=[SDS((3, T, GW), f32), SDS((T, GW), _MXU), SDS((NH, T, 1), f32), SDS((NH, T, 1), f32),
                   SDS((NH, 1, 1), f32), SDS((NH, 1, 1), f32), SDS((1, HD), f32)],
        scratch_shapes=[pltpu.VMEM((HB, HD, HD), f32)], compiler_params=_cp(("arbitrary", "arbitrary")), name=name,
    )(qkv, qkv, qkv, p, blt, aint, alog, dtb, nw, ssave, do)


def _loss_head(name, y, tgt):
    T, D = y.shape
    tm = _rtile(T, 256)

    def body(y_ref, t_ref, dy_ref, l_ref):
        i = pl.program_id(0)
        err = y_ref[...] - t_ref[...]
        dy_ref[...] = err * (1.0 / D)
        part = 0.5 * jnp.sum(jnp.sum(err * err, axis=-1, keepdims=True) * (1.0 / D), axis=0, keepdims=True)

        @pl.when(i == 0)
        def _():
            l_ref[...] = part

        @pl.when(i > 0)
        def _():
            l_ref[...] += part

    row = pl.BlockSpec((tm, D), lambda i: (i, 0))
    return pl.pallas_call(
        body, grid=(T // tm,), in_specs=[row, row], out_specs=[row, pl.BlockSpec((1, 1), lambda i: (0, 0))],
        out_shape=[SDS((T, D), f32), SDS((1, 1), f32)], compiler_params=_cp(("arbitrary",)), name=name,
    )(y, tgt)


def _adam_math(w, g, m, v):
    m = ADAM_B1 * m + (1.0 - ADAM_B1) * g
    v = ADAM_B2 * v + (1.0 - ADAM_B2) * jnp.square(g)
    m_hat = m / (1.0 - ADAM_B1 ** ADAM_STEP)
    v_hat = v / (1.0 - ADAM_B2 ** ADAM_STEP)
    delta = -ADAM_LR * (m_hat / (jnp.sqrt(v_hat) + ADAM_EPS) + ADAM_WD * w)
    return delta, m, v


def _add_mine(name, full, recv, me, out_dtype):
    N, _, R, C = full.shape
    tr = _rtile(R, max(16, (1 << 19) // max(C, 1) // 16 * 16))

    def body(me_ref, a_ref, b_ref, o_ref):
        o_ref[...] = (a_ref[...].astype(f32) + b_ref[...].astype(f32)).astype(o_ref.dtype)

    blk = pl.BlockSpec((None, tr, C), lambda n, i, me_ref: (n, i, 0))
    return pl.pallas_call(
        body,
        grid_spec=pltpu.PrefetchScalarGridSpec(
            num_scalar_prefetch=1, grid=(N, R // tr),
            in_specs=[pl.BlockSpec((None, None, tr, C), lambda n, i, me_ref: (n, me_ref[0], i, 0)), blk], out_specs=blk),
        out_shape=SDS((N, R, C), out_dtype), compiler_params=_cp(("parallel", "parallel")), name=name,
    )(me, full, recv)


def _adamw_big(name, full, recv, me, w, m, v, l):
    _, R, C = full.shape
    tr = _rtile(R, max(16, (1 << 18) // max(C, 1) // 16 * 16))

    def body(me_ref, ga_ref, gb_ref, w_ref, m_ref, v_ref, g_ref, d_ref, nm_ref, nv_ref):
        g = ga_ref[...].astype(f32) + gb_ref[...].astype(f32)
        d, nm, nv = _adam_math(w_ref[...], g, m_ref[...], v_ref[...])
        g_ref[...] = g
        d_ref[...] = d
        nm_ref[...] = nm
        nv_ref[...] = nv

    blk = pl.BlockSpec((tr, C), lambda i, me_ref: (i, 0))
    lblk = pl.BlockSpec((None, tr, C), lambda i, me_ref: (l, i, 0))
    return pl.pallas_call(
        body,
        grid_spec=pltpu.PrefetchScalarGridSpec(
            num_scalar_prefetch=1, grid=(R // tr,),
            in_specs=[pl.BlockSpec((None, tr, C), lambda i, me_ref: (me_ref[0], i, 0)), blk, lblk, lblk, lblk],
            out_specs=[blk] * 4),
        out_shape=[SDS((R, C), f32)] * 4, compiler_params=_cp(("parallel",)), name=name,
    )(me, full, recv, w, m, v)


def _adamw_small(name, gall, w, m, v):
    _, R, C = gall.shape
    tr = _rtile(R, 512)

    def body(ga_ref, w_ref, m_ref, v_ref, g_ref, d_ref, nm_ref, nv_ref):
        g = ga_ref[0]
        for s in range(1, 8):
            g = g + ga_ref[s]
        d, nm, nv = _adam_math(w_ref[...], g, m_ref[...], v_ref[...])
        g_ref[...] = g
        d_ref[...] = d
        nm_ref[...] = nm
        nv_ref[...] = nv

    blk = pl.BlockSpec((tr, C), lambda i: (i, 0))
    return pl.pallas_call(
        body, grid=(R // tr,), in_specs=[pl.BlockSpec((8, tr, C), lambda i: (0, i, 0)), blk, blk, blk], out_specs=[blk] * 4,
        out_shape=[SDS((R, C), f32)] * 4, compiler_params=_cp(("parallel",)), name=name,
    )(gall, w, m, v)


def _peer(axis):
    x, y, c = lax.axis_index("x"), lax.axis_index("y"), lax.axis_index("c")
    me = {"x": x, "y": y, "c": c}[axis]
    peer = {"x": (1 - x, y, c), "y": (x, 1 - y, c), "c": (x, y, 1 - c)}[axis]
    return me, peer


def _held(ref, axis):
    x, y, c = lax.axis_index("x"), lax.axis_index("y"), lax.axis_index("c")
    if axis == "y":
        return ref.at[x, y, c]
    if axis == "x":
        return ref.at[x, :, c]
    return ref.at[:, :, c]


def _gather_stage(name, axis, bufs):
    n = len(bufs)
    hbm = pl.BlockSpec(memory_space=pltpu.HBM)

    def body(*refs):
        outs = refs[n:2 * n]
        send_sems, recv_sems = refs[2 * n:]
        _, peer = _peer(axis)
        cps = []
        for t in range(n):
            blk = _held(outs[t], axis)
            cps.append(pltpu.make_async_remote_copy(src_ref=blk, dst_ref=blk, send_sem=send_sems.at[t],
                                                    recv_sem=recv_sems.at[t], device_id=peer, device_id_type=_MESH_T))
        for cp in cps:
            cp.start()
        for cp in cps:
            cp.wait()

    return pl.pallas_call(
        body, in_specs=[hbm] * n, out_specs=[hbm] * n, out_shape=[SDS(b.shape, b.dtype) for b in bufs],
        input_output_aliases={t: t for t in range(n)},
        scratch_shapes=[pltpu.SemaphoreType.DMA((n,)), pltpu.SemaphoreType.DMA((n,))], name=name,
    )(*bufs)


def _scatter_stage(name, axis, tensors, nsplit):
    n = len(tensors)
    hbm = pl.BlockSpec(memory_space=pltpu.HBM)

    def body(*refs):
        ins, recvs = refs[:n], refs[n:2 * n]
        send_sems, recv_sems = refs[2 * n:]
        me, peer = _peer(axis)
        cps = []
        for t in range(n):
            for k in range(nsplit):
                src = ins[t].at[:, 1 - me] if nsplit == 1 else ins[t].at[k, 1 - me]
                dst = recvs[t] if nsplit == 1 else recvs[t].at[k]
                cps.append(pltpu.make_async_remote_copy(
                    src_ref=src, dst_ref=dst, send_sem=send_sems.at[t * nsplit + k], recv_sem=recv_sems.at[t * nsplit + k],
                    device_id=peer, device_id_type=_MESH_T))
        for cp in cps:
            cp.start()
        for cp in cps:
            cp.wait()

    return pl.pallas_call(
        body, in_specs=[hbm] * n, out_specs=[hbm] * n,
        out_shape=[SDS((t.shape[0],) + tuple(t.shape[2:]), t.dtype) for t in tensors],
        scratch_shapes=[pltpu.SemaphoreType.DMA((n * nsplit,)), pltpu.SemaphoreType.DMA((n * nsplit,))], name=name,
    )(*tensors)


def _coords():
    return tuple(lax.axis_index(a).astype(jnp.int32).reshape(1) for a in ("x", "y", "c"))


def _all_gather(pfx, tensors):
    x, y, c = (lax.axis_index(a) for a in ("x", "y", "c"))
    bufs = []
    for t in tensors:
        zero = (0,) * t.ndim
        bufs.append(lax.dynamic_update_slice(jnp.zeros((2, 2, 2) + tuple(t.shape), t.dtype), t[None, None, None],
                                             (x, y, c) + zero))
    for axis in ("y", "x", "c"):
        bufs = _gather_stage(f"{pfx}_{axis}", axis, bufs)
    return [b.reshape((8,) + tuple(t.shape)) for b, t in zip(bufs, tensors)]


def _reduce_scatter(pfx, tensors):
    xs, ys, cs = _coords()
    rcs = [tuple(t.shape[1:]) for t in tensors]
    cur = [t.reshape((4, 2) + rc) for t, rc in zip(tensors, rcs)]
    recvs = _scatter_stage(pfx + "_c", "c", cur, 4)
    cur = [_add_mine(f"{pfx}_addc{i}", f, r, cs, f.dtype) for i, (f, r) in enumerate(zip(cur, recvs))]
    cur = [t.reshape((1, 2, 2 * rc[0], rc[1])) for t, rc in zip(cur, rcs)]
    recvs = _scatter_stage(pfx + "_x", "x", cur, 1)
    cur = [_add_mine(f"{pfx}_addx{i}", f, r, xs, f.dtype) for i, (f, r) in enumerate(zip(cur, recvs))]
    cur = [t.reshape((2,) + rc) for t, rc in zip(cur, rcs)]
    recvs = _scatter_stage(pfx + "_y", "y", [t[None] for t in cur], 1)
    return [(f, r[0]) for f, r in zip(cur, recvs)]


def _mm_nn(name, a, w, out_dtype, tn_pref=1024):
    T, K = a.shape
    N = w.shape[1]
    tm, tn, tk = _rtile(T, 512), _tile(N, tn_pref), _tile(K, 2048)
    return _mm(
        name, a, w, _NN, (T // tm, N // tn, K // tk),
        pl.BlockSpec((tm, tk), lambda i, j, k: (i, k)), pl.BlockSpec((tk, tn), lambda i, j, k: (k, j)), (tm, tn),
        [], [], [SDS((T, N), out_dtype)], [pl.BlockSpec((tm, tn), lambda i, j, k: (i, j))], _store_epi,
    )[0]


def _mm_tn(name, a, b, out_dtype):
    T, M = a.shape
    N = b.shape[1]
    tm, tn, tk = _tile(M, 512), _tile(N, 2048), _rtile(T, 512)
    return _mm(
        name, a, b, _TN, (M // tm, N // tn, T // tk),
        pl.BlockSpec((tk, tm), lambda i, j, k: (k, i)), pl.BlockSpec((tk, tn), lambda i, j, k: (k, j)), (tm, tn),
        [], [], [SDS((M, N), out_dtype)], [pl.BlockSpec((tm, tn), lambda i, j, k: (i, j))], _store_epi,
    )[0]


def _mm_tn_slots(name, a, b, out_dtype):
    T, M = a.shape
    NS = b.shape[1] // 8
    tm, tk = _tile(M, 512), _rtile(T, 512)
    return _mm(
        name, a, b, _TN, (M // tm, 8, T // tk),
        pl.BlockSpec((tk, tm), lambda i, j, k: (k, i)), pl.BlockSpec((tk, NS), lambda i, j, k: (k, j)), (tm, NS),
        [], [], [SDS((8, M, NS), out_dtype)], [pl.BlockSpec((None, tm, NS), lambda i, j, k: (j, i, 0))], _store_epi,
    )[0]


def _mm_nt_slots(name, a, w8, out_dtype):
    T = a.shape[0]
    _, M, NS = w8.shape
    tm, tn = _rtile(T, 512), _tile(M, 1024)
    return _mm(
        name, a, w8, _NT, (T // tm, M // tn, 8),
        pl.BlockSpec((tm, NS), lambda i, j, k: (i, k)), pl.BlockSpec((None, tn, NS), lambda i, j, k: (k, j, 0)), (tm, tn),
        [], [], [SDS((T, M), out_dtype)], [pl.BlockSpec((tm, tn), lambda i, j, k: (i, j))], _store_epi,
    )[0]


def _colsum_kernel(name, fn, ins, in_cols, outs_elem, n_sum, C):
    T = ins[0].shape[0]
    tm = _rtile(T, 256)
    ne = len(outs_elem)

    def body(*refs):
        i = pl.program_id(0)
        iv = [r[...] for r in refs[:len(ins)]]
        res = fn(*iv)
        for o, r in zip(refs[len(ins):len(ins) + ne], res[:ne]):
            o[...] = r.astype(o.dtype)
        sums = [jnp.sum(r, axis=0, keepdims=True) for r in res[ne:]]

        @pl.when(i == 0)
        def _():
            for o, s in zip(refs[len(ins) + ne:], sums):
                o[...] = s

        @pl.when(i > 0)
        def _():
            for o, s in zip(refs[len(ins) + ne:], sums):
                o[...] += s

    in_specs = []
    for arr, off in zip(ins, in_cols):
        if off is None:
            in_specs.append(pl.BlockSpec((1, C), lambda i: (0, 0)))
        else:
            in_specs.append(pl.BlockSpec((tm, C), lambda i, off=off: (i, off)))
    row = pl.BlockSpec((tm, C), lambda i: (i, 0))
    vec = pl.BlockSpec((1, C), lambda i: (0, 0))
    return pl.pallas_call(
        body, grid=(T // tm,), in_specs=in_specs, out_specs=[row] * ne + [vec] * n_sum,
        out_shape=[SDS((T, C), dt) for dt in outs_elem] + [SDS((1, C), f32)] * n_sum,
        compiler_params=_cp(("arbitrary",)), name=name,
    )(*ins)


def _merge(gs, gg, a_s, a_g):
    return jax.nn.sigmoid(gs) * a_s + jax.nn.sigmoid(gg) * a_g


def _glu(yg, lp):
    return yg * jax.nn.sigmoid(lp)


_BIG = ("ffn1_w_gu", "ffn1_w_down", "w_in", "conv_w", "glu_w", "w_br_ssm", "w_br_gdn", "w_out", "ffn2_w_gu", "ffn2_w_down")
_SMALL = ("ln1_g", "ln1_b", "ssm_a_re", "ssm_a_im", "ssm_log_dt", "ssm_b_re", "ssm_b_im", "ssm_c_re", "ssm_c_im", "ssm_d",
          "glu_b", "gdn_a_log", "gdn_dt_bias", "gdn_norm_w", "ln2_g", "ln2_b", "ln3_g", "ln3_b")
_ORDER = ("ffn1_w_gu", "ffn1_w_down", "ln1_g", "ln1_b", "w_in", "conv_w", "ssm_a_re", "ssm_a_im", "ssm_log_dt", "ssm_b_re",
          "ssm_b_im", "ssm_c_re", "ssm_c_im", "ssm_d", "glu_w", "glu_b", "gdn_a_log", "gdn_dt_bias", "gdn_norm_w", "w_br_ssm",
          "w_br_gdn", "w_out", "ln2_g", "ln2_b", "ffn2_w_gu", "ffn2_w_down", "ln3_g", "ln3_b")


def _step(x, tgt, W, M, V):
    T, D = x.shape[1], x.shape[2]
    L = W["ffn1_w_gu"].shape[0]
    G, P = W["ssm_a_re"].shape[1:]
    H = W["ssm_b_re"].shape[3]
    SW = G * H
    NH = W["gdn_a_log"].shape[1]
    HD = W["gdn_norm_w"].shape[1]
    GW = NH * HD
    KC = W["conv_w"].shape[1]
    DS = D // 8
    alpha = (2.0 * L) ** 0.25
    o_b = SW + 4 * GW
    o_gs = o_b + 2 * NH
    IN = o_gs + 2 * D
    NM = IN - 2 * NH
    m_qkv, m_z, m_gs, m_gg = SW, SW + 3 * GW, SW + 4 * GW, SW + 4 * GW + D
    J = G // 8

    x0 = x[0]
    tg = tgt[0]

    def vec(name, l):
        return W[name][l:l + 1]

    saves, weights = [], []
    xc, xcb = x0, x0.astype(_MXU)
    for l in range(L):
        gathered = _all_gather("ag", [
            W["ffn1_w_gu"][l].astype(_MXU), W["ffn1_w_down"][l].astype(_MXU), W["w_in"][l].astype(_MXU), W["conv_w"][l],
            W["glu_w"][l].astype(_MXU), W["w_br_ssm"][l].astype(_MXU), W["w_br_gdn"][l].astype(_MXU),
            W["w_out"][l].astype(_MXU), W["ffn2_w_gu"][l].astype(_MXU), W["ffn2_w_down"][l].astype(_MXU)])
        wgu1, wd1, win8, cw8, wglu, wbs, wbg, wo, wgu2, wd2 = gathered
        wd1 = wd1.reshape(-1, D)
        wd2 = wd2.reshape(-1, D)
        wglu = wglu.reshape(SW, SW)
        wo = wo.reshape(D, D)
        win = jnp.transpose(win8, (1, 0, 2)).reshape(D, IN)
        wmain = jnp.concatenate([win[:, :o_b], win[:, o_gs:]], axis=1)
        wba = jnp.pad(win[:, o_b:o_gs], ((0, 0), (0, 128 - 2 * NH)))
        cw = jnp.transpose(cw8, (1, 0, 2)).reshape(KC, 3 * GW)
        wl = dict(wgu1=wgu1, wd1=wd1, wmain=wmain, wba=wba, cw=cw, wglu=wglu, wbs=wbs, wbg=wbg, wo=wo, wgu2=wgu2, wd2=wd2)
        weights.append(wl)
        sv = {}

        gate, up, hh = _ffn_up("ffn_up", xcb, wgu1)
        x1, x1b, xh1, r1 = _mm_ln("ffn_down_ln", hh, wd1, xc, vec("ln1_g", l), vec("ln1_b", l), alpha, 0.5)
        sv["f1"] = dict(xb=xcb, gate=gate, up=up, h=hh, xhat=xh1, rstd=r1)

        p = _mm_nn("mix_in", x1b, wmain, f32)
        pba = _mm_nn("mix_in_ba", x1b, wba, f32)
        b_re_t = jnp.transpose(W["ssm_b_re"][l], (2, 0, 1))
        b_im_t = jnp.transpose(W["ssm_b_im"][l], (2, 0, 1))
        zoh_in = (W["ssm_a_re"][l], W["ssm_a_im"][l], W["ssm_log_dt"][l][:, None], b_re_t, b_im_t)
        lbr, lbi, bbr_t, bbi_t = _zoh_fwd("zoh", *zoh_in)
        bblk_r = _blockdiag(jnp.transpose(bbr_t, (1, 0, 2)))
        bblk_i = _blockdiag(jnp.transpose(bbi_t, (1, 0, 2)))
        cblkT_r = _blockdiag(W["ssm_c_re"][l])
        cblkT_in = _blockdiag(-W["ssm_c_im"][l])
        lbr_f, lbi_f = lbr.reshape(1, G * P), lbi.reshape(1, G * P)
        bur, bui = _bd2("s5_bu", p, 0, bblk_r, bblk_i)
        sr, si = _s5_scan("s5_scan", bur, bui, lbr_f, lbi_f)
        dflat = W["ssm_d"][l].reshape(1, SW)

        def out_epi(acc, ex, outs):
            y_raw = acc + ex[1][...] * ex[0][...]
            yg = jax.nn.gelu(y_raw)
            outs[0][...] = y_raw
            outs[1][...] = yg
            outs[2][...] = yg.astype(outs[2].dtype)

        y_raw, yg, ygb = _bd_sum(
            "s5_out", sr, si, jnp.transpose(cblkT_r, (0, 2, 1)), jnp.transpose(cblkT_in, (0, 2, 1)), [p, dflat],
            lambda tm, nb: [pl.BlockSpec((tm, nb), lambda i, j: (i, j)), pl.BlockSpec((1, nb), lambda i, j: (0, j))],
            [SDS((T, SW), f32), SDS((T, SW), f32), SDS((T, SW), _MXU)], out_epi)

        tmg, tng, tkg = _rtile(T, 512), _tile(SW, 512), _tile(SW, 512)

        def glu_epi(acc, ex, outs):
            lp = acc + ex[1][...]
            outs[0][...] = lp
            outs[1][...] = _glu(ex[0][...], lp).astype(outs[1].dtype)

        lp, ysb = _mm(
            "s5_glu", ygb, wglu, _NN, (T // tmg, SW // tng, SW // tkg),
            pl.BlockSpec((tmg, tkg), lambda i, j, k: (i, k)), pl.BlockSpec((tkg, tng), lambda i, j, k: (k, j)), (tmg, tng),
            [yg, vec("glu_b", l)], [pl.BlockSpec((tmg, tng), lambda i, j, k: (i, j)), pl.BlockSpec((1, tng), lambda i, j, k: (0, j))],
            [SDS((T, SW), f32), SDS((T, SW), _MXU)], [pl.BlockSpec((tmg, tng), lambda i, j, k: (i, j))] * 2, glu_epi)

        qkv = _conv_fwd("gdn_conv", p, m_qkv, cw, 3 * GW)
        blt = jnp.transpose(pba[:, :NH])[:, :, None]
        aint = jnp.transpose(pba[:, NH:2 * NH])[:, :, None]
        alog = W["gdn_a_log"][l].reshape(NH, 1, 1)
        dtb = W["gdn_dt_bias"][l].reshape(NH, 1, 1)
        nw = vec("gdn_norm_w", l)
        og, ssave = _gdn_fwd("gdn", qkv, p, m_z, blt, aint, alog, dtb, nw, NH, HD)

        a_s = _mm(
            "br_ssm", ysb, wbs, _NN, (T // tmg, 8, SW // tkg),
            pl.BlockSpec((tmg, tkg), lambda i, j, k: (i, k)), pl.BlockSpec((None, tkg, DS), lambda i, j, k: (j, k, 0)), (tmg, DS),
            [], [], [SDS((T, D), f32)], [pl.BlockSpec((tmg, DS), lambda i, j, k: (i, j))], _store_epi)[0]
        tkd = _tile(GW, 512)
        gsb, ggb = m_gs // DS, m_gg // DS

        def merge_epi(acc, ex, outs):
            outs[0][...] = acc
            outs[1][...] = _merge(ex[1][...], ex[2][...], ex[0][...], acc).astype(outs[1].dtype)

        tile_ij = pl.BlockSpec((tmg, DS), lambda i, j, k: (i, j))
        a_g, merged = _mm(
            "br_gdn_merge", og, wbg, _NN, (T // tmg, 8, GW // tkd),
            pl.BlockSpec((tmg, tkd), lambda i, j, k: (i, k)), pl.BlockSpec((None, tkd, DS), lambda i, j, k: (j, k, 0)), (tmg, DS),
            [a_s, p, p], [tile_ij, pl.BlockSpec((tmg, DS), lambda i, j, k: (i, j + gsb)),
                          pl.BlockSpec((tmg, DS), lambda i, j, k: (i, j + ggb))],
            [SDS((T, D), f32), SDS((T, D), _MXU)], [tile_ij, tile_ij], merge_epi)
        x2, x2b, xh2, r2 = _mm_ln("mix_out_ln", merged, wo, x1, vec("ln2_g", l), vec("ln2_b", l), alpha, 1.0)
        sv["mx"] = dict(x1b=x1b, p=p, zoh_in=zoh_in, lbr_f=lbr_f, lbi_f=lbi_f, bblk_r=bblk_r, bblk_i=bblk_i, cblkT_r=cblkT_r,
                        cblkT_in=cblkT_in, sr=sr, si=si, dflat=dflat, y_raw=y_raw, yg=yg, ygb=ygb, lp=lp, ysb=ysb, qkv=qkv,
                        blt=blt, aint=aint, alog=alog, dtb=dtb, nw=nw, og=og, ssave=ssave, a_s=a_s, a_g=a_g, merged=merged,
                        xhat=xh2, rstd=r2)

        gate2, up2, hh2 = _ffn_up("ffn_up", x2b, wgu2)
        x3, x3b, xh3, r3 = _mm_ln("ffn_down_ln", hh2, wd2, x2, vec("ln3_g", l), vec("ln3_b", l), alpha, 0.5)
        sv["f2"] = dict(xb=x2b, gate=gate2, up=up2, h=hh2, xhat=xh3, rstd=r3)
        saves.append(sv)
        xc, xcb = x3, x3b

    dy, loss_part = _loss_head("loss_head", xc, tg)
    loss = lax.psum(loss_part[0, 0], ("x", "y", "c"))

    big_out = {n: [None] * L for n in _BIG}
    small_g = {n: [None] * L for n in _SMALL}
    for l in reversed(range(L)):
        sv, wl = saves[l], weights[l]
        mx = sv["mx"]
        p = mx["p"]
        dx2, dwgu2, dwd2, dg3, db3 = _ffn_bwd("ffn_b", dy, sv["f2"], wl["wgu2"], wl["wd2"], vec("ln3_g", l), alpha)
        small_g["ln3_g"][l], small_g["ln3_b"][l] = dg3[0], db3[0]

        dz2, dmixb, dg2, db2 = _ln_bwd("mix_lnb", dx2, mx["xhat"], mx["rstd"], vec("ln2_g", l), 1.0)
        small_g["ln2_g"][l], small_g["ln2_b"][l] = dg2[0], db2[0]
        tmg, tkd = _rtile(T, 512), _tile(D, 512)
        gsb, ggb = m_gs // DS, m_gg // DS

        def dmerge_epi(acc, ex, outs):
            _, vjp = jax.vjp(_merge, ex[0][...], ex[1][...], ex[2][...], ex[3][...])
            dgs, dgg, das, dag = vjp(acc)
            outs[0][...] = das.astype(outs[0].dtype)
            outs[1][...] = dag.astype(outs[1].dtype)
            outs[2][...] = dgs.astype(outs[2].dtype)
            outs[3][...] = dgg.astype(outs[3].dtype)

        tile_ij = pl.BlockSpec((tmg, DS), lambda i, j, k: (i, j))
        das, dag, dgs, dgg = _mm(
            "mix_dmerge", dmixb, wl["wo"], _NT, (T // tmg, 8, D // tkd),
            pl.BlockSpec((tmg, tkd), lambda i, j, k: (i, k)), pl.BlockSpec((DS, tkd), lambda i, j, k: (j, k)), (tmg, DS),
            [p, p, mx["a_s"], mx["a_g"]],
            [pl.BlockSpec((tmg, DS), lambda i, j, k: (i, j + gsb)), pl.BlockSpec((tmg, DS), lambda i, j, k: (i, j + ggb)),
             tile_ij, tile_ij],
            [SDS((T, D), _MXU)] * 4, [tile_ij] * 4, dmerge_epi)
        dwo = _mm_tn("mix_dwo", mx["merged"], dmixb, _GDT)
        dys = _mm_nt_slots("br_ssm_dx", das, wl["wbs"], f32)
        dog = _mm_nt_slots("br_gdn_dx", dag, wl["wbg"], f32)
        dwbs = _mm_tn_slots("br_ssm_dw", mx["ysb"], das, _GDT)
        dwbg = _mm_tn_slots("br_gdn_dw", mx["og"], dag, _GDT)

        def glu_b_fn(dys_t, yg_t, lp_t):
            _, vjp = jax.vjp(_glu, yg_t, lp_t)
            dyg1, dlp = vjp(dys_t)
            return dyg1, dlp, dlp

        dyg1, dlpb, dglub = _colsum_kernel("s5_glu_b", glu_b_fn, [dys, mx["yg"], mx["lp"]], [0, 0, 0], [f32, _MXU], 1, SW)
        small_g["glu_b"][l] = dglub[0]
        dwglu = _mm_tn("s5_dwglu", mx["ygb"], dlpb, _GDT)
        tng, tkg = _tile(SW, 512), _tile(SW, 512)

        def dyraw_epi(acc, ex, outs):
            _, vjp = jax.vjp(jax.nn.gelu, ex[1][...])
            (d,) = vjp(ex[0][...] + acc)
            outs[0][...] = d

        t_ij = pl.BlockSpec((tmg, tng), lambda i, j, k: (i, j))
        (dyraw,) = _mm(
            "s5_dyraw", dlpb, wl["wglu"], _NT, (T // tmg, SW // tng, SW // tkg),
            pl.BlockSpec((tmg, tkg), lambda i, j, k: (i, k)), pl.BlockSpec((tng, tkg), lambda i, j, k: (j, k)), (tmg, tng),
            [dyg1, mx["y_raw"]], [t_ij, t_ij], [SDS((T, SW), f32)], [t_ij], dyraw_epi)

        def dd_fn(dyr, u_t, d_t):
            return d_t * dyr, dyr * u_t

        dud, dd = _colsum_kernel("s5_dd", dd_fn, [dyraw, p, mx["dflat"]], [0, 0, None], [f32], 1, SW)
        small_g["ssm_d"][l] = dd.reshape(G, H)
        dsr, dsi = _bd2("s5_ds", dyraw, 0, mx["cblkT_r"], mx["cblkT_in"])
        dcb_r, dcb_i = _bdT2("s5_dc", mx["sr"], mx["si"], 0, dyraw, dyraw, 0, 8 * P, 8 * H, J)
        small_g["ssm_c_re"][l] = _blockdiag_extract(jnp.transpose(dcb_r, (0, 2, 1)), H, P)
        small_g["ssm_c_im"][l] = -_blockdiag_extract(jnp.transpose(dcb_i, (0, 2, 1)), H, P)
        ar, ai, dlr, dli = _s5_scan_bwd("s5_scan_b", dsr, dsi, mx["sr"], mx["si"], mx["lbr_f"], mx["lbi_f"])

        def du_epi(acc, ex, outs):
            outs[0][...] = (acc + ex[0][...]).astype(outs[0].dtype)

        (du,) = _bd_sum(
            "s5_du", ar, ai, jnp.transpose(mx["bblk_r"], (0, 2, 1)), jnp.transpose(mx["bblk_i"], (0, 2, 1)), [dud],
            lambda tm, nb: [pl.BlockSpec((tm, nb), lambda i, j: (i, j))], [SDS((T, SW), _MXU)], du_epi)
        dbb_r, dbb_i = _bdT2("s5_db", p, p, 0, ar, ai, 0, 8 * H, 8 * P, J)
        dbbr_t = jnp.transpose(_blockdiag_extract(dbb_r, H, P), (1, 0, 2))
        dbbi_t = jnp.transpose(_blockdiag_extract(dbb_i, H, P), (1, 0, 2))
        da_re, da_im, dlog_dt, dbre_t, dbim_t = _zoh_bwd("zoh_b", *mx["zoh_in"], dlr.reshape(G, P), dli.reshape(G, P),
                                                         dbbr_t, dbbi_t)
        small_g["ssm_a_re"][l], small_g["ssm_a_im"][l], small_g["ssm_log_dt"][l] = da_re, da_im, dlog_dt[:, 0]
        small_g["ssm_b_re"][l] = jnp.transpose(dbre_t, (1, 2, 0))
        small_g["ssm_b_im"][l] = jnp.transpose(dbim_t, (1, 2, 0))

        dqkv3, dzb, dbl, dain, dal, ddtb, dnw = _gdn_bwd("gdn_b", mx["qkv"], p, m_z, mx["blt"], mx["aint"], mx["alog"],
                                                         mx["dtb"], mx["nw"], mx["ssave"], dog, NH, HD)
        small_g["gdn_a_log"][l], small_g["gdn_dt_bias"][l], small_g["gdn_norm_w"][l] = dal[:, 0, 0], ddtb[:, 0, 0], dnw[0]
        dqkv_pre, dcw = _conv_bwd("gdn_conv_b", p, m_qkv, wl["cw"], dqkv3)

        dpm = jnp.concatenate([du, dqkv_pre, dzb, dgs, dgg], axis=1)
        dpba = jnp.concatenate([jnp.transpose(dbl[:, :, 0]), jnp.transpose(dain[:, :, 0]),
                                jnp.zeros((T, 128 - 2 * NH), f32)], axis=1).astype(_MXU)
        tnd, tkm = _tile(D, 1024), _tile(NM, 512)
        t_ba = _mm(
            "mix_dx_ba", dpba, wl["wba"], _NT, (T // tmg, D // tnd, 1),
            pl.BlockSpec((tmg, 128), lambda i, j, k: (i, 0)), pl.BlockSpec((tnd, 128), lambda i, j, k: (j, 0)), (tmg, tnd),
            [], [], [SDS((T, D), f32)], [pl.BlockSpec((tmg, tnd), lambda i, j, k: (i, j))], _store_epi)[0]

        def dx1_epi(acc, ex, outs):
            outs[0][...] = alpha * ex[0][...] + ex[1][...] + acc

        t_d = pl.BlockSpec((tmg, tnd), lambda i, j, k: (i, j))
        (dx1,) = _mm(
            "mix_dx", dpm, wl["wmain"], _NT, (T // tmg, D // tnd, NM // tkm),
            pl.BlockSpec((tmg, tkm), lambda i, j, k: (i, k)), pl.BlockSpec((tnd, tkm), lambda i, j, k: (j, k)), (tmg, tnd),
            [dz2, t_ba], [t_d, t_d], [SDS((T, D), f32)], [t_d], dx1_epi)
        tnm = _tile(NM, 1024)
        tkt = _tile(T, 1024)
        dwmain = _mm(
            "mix_dw", jnp.transpose(mx["x1b"]), dpm, _NN, (D // tkd, NM // tnm, T // tkt),
            pl.BlockSpec((tkd, tkt), lambda i, j, k: (i, k)), pl.BlockSpec((tkt, tnm), lambda i, j, k: (k, j)), (tkd, tnm),
            [], [], [SDS((D, NM), _GDT)], [pl.BlockSpec((tkd, tnm), lambda i, j, k: (i, j))], _store_epi)[0]
        dwba = _mm_tn("mix_dw_ba", mx["x1b"], dpba, _GDT)
        dwin = jnp.concatenate([dwmain[:, :o_b], dwba[:, :2 * NH], dwmain[:, o_b:]], axis=1)
        dwin8 = jnp.transpose(dwin.reshape(D, 8, IN // 8), (1, 0, 2))
        dcw8 = jnp.transpose(dcw.reshape(KC, 8, 3 * GW // 8), (1, 0, 2))

        dx0, dwgu1, dwd1, dg1, db1 = _ffn_bwd("ffn_b", dx1, sv["f1"], wl["wgu1"], wl["wd1"], vec("ln1_g", l), alpha)
        small_g["ln1_g"][l], small_g["ln1_b"][l] = dg1[0], db1[0]
        dy = dx0

        parts = [dwgu1, dwd1.reshape(8, -1, D), dwin8, dcw8, dwglu.reshape(8, SW // 8, SW), dwbs, dwbg,
                 dwo.reshape(8, DS, D), dwgu2, dwd2.reshape(8, -1, D)]
        pairs = _reduce_scatter("rs", parts)
        my_y = _coords()[1]
        for n, (full, recv) in zip(_BIG, pairs):
            big_out[n][l] = _adamw_big("adamw_" + n, full, recv, my_y, W[n], M[n], V[n], l)

    def pack(arrs):
        flat = jnp.concatenate([a.reshape(-1) for a in arrs])
        n = flat.shape[0]
        rows = -(-n // (128 * 16)) * 16
        return jnp.pad(flat, (0, rows * 128 - n)).reshape(rows, 128)

    gs_full = [jnp.stack(small_g[n]).reshape(W[n].shape) for n in _SMALL]
    gpack = pack(gs_full)
    (gall,) = _all_gather("ag_small", [gpack])
    sg, sd, sm, sv_ = _adamw_small("adamw_small", gall, pack([W[n] for n in _SMALL]), pack([M[n] for n in _SMALL]),
                                   pack([V[n] for n in _SMALL]))

    def unpack(packed):
        flat = packed.reshape(-1)
        out, off = {}, 0
        for n in _SMALL:
            sz = math.prod(W[n].shape)
            out[n] = flat[off:off + sz].reshape(W[n].shape)
            off += sz
        return out

    res = [unpack(a) for a in (sg, sd, sm, sv_)]
    for n in _BIG:
        for i in range(4):
            res[i][n] = jnp.stack([big_out[n][l][i] for l in range(L)])
    outs = [loss, dy[None]]
    for i in range(4):
        outs += [res[i][n] for n in _ORDER]
    return tuple(outs)


def kernel(x, ffn1_w_gu, ffn1_w_down, ln1_g, ln1_b, w_in, conv_w, ssm_a_re, ssm_a_im, ssm_log_dt, ssm_b_re, ssm_b_im, ssm_c_re, ssm_c_im, ssm_d, glu_w, glu_b, gdn_a_log, gdn_dt_bias, gdn_norm_w, w_br_ssm, w_br_gdn, w_out, ln2_g, ln2_b, ffn2_w_gu, ffn2_w_down, ln3_g, ln3_b, loss_target, m_ffn1_w_gu, m_ffn1_w_down, m_ln1_g, m_ln1_b, m_w_in, m_conv_w, m_ssm_a_re, m_ssm_a_im, m_ssm_log_dt, m_ssm_b_re, m_ssm_b_im, m_ssm_c_re, m_ssm_c_im, m_ssm_d, m_glu_w, m_glu_b, m_gdn_a_log, m_gdn_dt_bias, m_gdn_norm_w, m_w_br_ssm, m_w_br_gdn, m_w_out, m_ln2_g, m_ln2_b, m_ffn2_w_gu, m_ffn2_w_down, m_ln3_g, m_ln3_b, v_ffn1_w_gu, v_ffn1_w_down, v_ln1_g, v_ln1_b, v_w_in, v_conv_w, v_ssm_a_re, v_ssm_a_im, v_ssm_log_dt, v_ssm_b_re, v_ssm_b_im, v_ssm_c_re, v_ssm_c_im, v_ssm_d, v_glu_w, v_glu_b, v_gdn_a_log, v_gdn_dt_bias, v_gdn_norm_w, v_w_br_ssm, v_w_br_gdn, v_w_out, v_ln2_g, v_ln2_b, v_ffn2_w_gu, v_ffn2_w_down, v_ln3_g, v_ln3_b):
    given = dict(locals())
    W = {n: given[n] for n in _ORDER}
    M = {n: given["m_" + n] for n in _ORDER}
    V = {n: given["v_" + n] for n in _ORDER}
    return _step(x, loss_target, W, M, V)
```

```python
import functools
import math

import jax
import jax.numpy as jnp
from jax import lax
from jax.experimental import pallas as pl
from jax.experimental.pallas import tpu as pltpu

f32 = jnp.float32
_MXU = jnp.bfloat16
_GDT = jnp.bfloat16
_HP = lax.Precision.HIGHEST
_VMEM_LIMIT = 56 * 1024 * 1024
_MESH_T = pl.DeviceIdType.MESH

LN_EPS = 1e-5
RMS_EPS = 1e-6
L2_EPS = 1e-6
CHUNK = 64
ADAM_LR = 0.001
ADAM_B1 = 0.9
ADAM_B2 = 0.999
ADAM_EPS = 1e-08
ADAM_WD = 0.01
ADAM_STEP = 10

_NN = (((1,), (0,)), ((), ()))
_NT = (((1,), (1,)), ((), ()))
_TN = (((0,), (0,)), ((), ()))

SDS = jax.ShapeDtypeStruct


def _cp(sem):
    return pltpu.CompilerParams(dimension_semantics=sem, vmem_limit_bytes=_VMEM_LIMIT)


def _tile(n, pref):
    if n <= pref:
        return n
    t = (pref // 128) * 128
    while t >= 128:
        if n % t == 0:
            return t
        t -= 128
    return n


def _rtile(n, pref):
    if n <= pref:
        return n
    t = (pref // 16) * 16
    while t >= 16:
        if n % t == 0:
            return t
        t -= 16
    return n


def _mm(name, a, b, dims, grid, a_spec, b_spec, acc_shape, extras, extra_specs, out_shape, out_specs, epilogue):
    nk = grid[2]
    ne = len(extras)
    no = len(out_shape)

    def body(*refs):
        a_ref, b_ref = refs[0], refs[1]
        ex = refs[2:2 + ne]
        outs = refs[2 + ne:2 + ne + no]
        acc = refs[-1]
        k = pl.program_id(2)
        part = lax.dot_general(a_ref[...].astype(_MXU), b_ref[...].astype(_MXU), dims, preferred_element_type=f32)

        @pl.when(k == 0)
        def _():
            acc[...] = part

        @pl.when(k > 0)
        def _():
            acc[...] += part

        @pl.when(k == nk - 1)
        def _():
            epilogue(acc[...], ex, outs)

    return pl.pallas_call(
        body, grid=grid, in_specs=[a_spec, b_spec, *extra_specs], out_specs=list(out_specs), out_shape=list(out_shape),
        scratch_shapes=[pltpu.VMEM(acc_shape, f32)], compiler_params=_cp(("parallel", "parallel", "arbitrary")), name=name,
    )(a, b, *extras)


def _store_epi(acc, ex, outs):
    for o in outs:
        o[...] = acc.astype(o.dtype)


def _ln_epilogue(alpha, c):
    def epi(acc, ex, outs):
        x_ref, g_ref, b_ref = ex
        y_ref, yb_ref, xh_ref, r_ref = outs
        z = alpha * x_ref[...] + c * acc
        mu = jnp.mean(z, axis=-1, keepdims=True)
        zc = z - mu
        var = jnp.mean(zc * zc, axis=-1, keepdims=True)
        r = lax.rsqrt(var + LN_EPS)
        xh = zc * r
        y = xh * g_ref[...] + b_ref[...]
        y_ref[...] = y
        yb_ref[...] = y.astype(yb_ref.dtype)
        xh_ref[...] = xh
        r_ref[...] = r
    return epi


def _mm_ln(name, a, w, x, g, b, alpha, c):
    T, K = a.shape
    D = w.shape[1]
    tm, tk = _rtile(T, 512), _tile(K, 512)
    row = pl.BlockSpec((tm, D), lambda i, j, k: (i, 0))
    vec = pl.BlockSpec((1, D), lambda i, j, k: (0, 0))
    return _mm(
        name, a, w, _NN, (T // tm, 1, K // tk),
        pl.BlockSpec((tm, tk), lambda i, j, k: (i, k)), pl.BlockSpec((tk, D), lambda i, j, k: (k, 0)), (tm, D),
        [x, g, b], [row, vec, vec],
        [SDS((T, D), f32), SDS((T, D), _MXU), SDS((T, D), f32), SDS((T, 1), f32)],
        [row, row, row, pl.BlockSpec((tm, 1), lambda i, j, k: (i, 0))],
        _ln_epilogue(alpha, c),
    )


def _ln_bwd(name, dy, xhat, rstd, g, c):
    T, D = dy.shape
    tm = _rtile(T, 256)

    def body(dy_ref, xh_ref, r_ref, g_ref, dz_ref, df_ref, dg_ref, db_ref):
        i = pl.program_id(0)
        dyv = dy_ref[...]
        xh = xh_ref[...]
        dxh = dyv * g_ref[...]
        m1 = jnp.mean(dxh, axis=-1, keepdims=True)
        m2 = jnp.mean(dxh * xh, axis=-1, keepdims=True)
        dz = r_ref[...] * (dxh - m1 - xh * m2)
        dz_ref[...] = dz
        df_ref[...] = (c * dz).astype(df_ref.dtype)
        pg = jnp.sum(dyv * xh, axis=0, keepdims=True)
        pb = jnp.sum(dyv, axis=0, keepdims=True)

        @pl.when(i == 0)
        def _():
            dg_ref[...] = pg
            db_ref[...] = pb

        @pl.when(i > 0)
        def _():
            dg_ref[...] += pg
            db_ref[...] += pb

    row = pl.BlockSpec((tm, D), lambda i: (i, 0))
    vec = pl.BlockSpec((1, D), lambda i: (0, 0))
    return pl.pallas_call(
        body, grid=(T // tm,), in_specs=[row, row, pl.BlockSpec((tm, 1), lambda i: (i, 0)), vec],
        out_specs=[row, row, vec, vec],
        out_shape=[SDS((T, D), f32), SDS((T, D), _MXU), SDS((1, D), f32), SDS((1, D), f32)],
        compiler_params=_cp(("arbitrary",)), name=name,
    )(dy, xhat, rstd, g)


def _swiglu(g, u):
    return jax.nn.silu(g) * u


def _ffn_up(name, xb, wgu):
    T, D = xb.shape
    FS = wgu.shape[2]
    F = 4 * FS
    tm = _rtile(T, 256)

    def body(x_ref, wg_ref, wu_ref, g_ref, u_ref, h_ref):
        xv = x_ref[...]
        g = jnp.dot(xv, wg_ref[...], preferred_element_type=f32)
        u = jnp.dot(xv, wu_ref[...], preferred_element_type=f32)
        g_ref[...] = g
        u_ref[...] = u
        h_ref[...] = _swiglu(g, u).astype(h_ref.dtype)

    out = pl.BlockSpec((tm, FS), lambda j, i: (i, j))
    return pl.pallas_call(
        body, grid=(4, T // tm),
        in_specs=[pl.BlockSpec((tm, D), lambda j, i: (i, 0)),
                  pl.BlockSpec((None, D, FS), lambda j, i: (j, 0, 0)),
                  pl.BlockSpec((None, D, FS), lambda j, i: (j + 4, 0, 0))],
        out_specs=[out, out, out],
        out_shape=[SDS((T, F), f32), SDS((T, F), f32), SDS((T, F), _MXU)],
        compiler_params=_cp(("parallel", "arbitrary")), name=name,
    )(xb, wgu, wgu)


def _ffn_bwd(pfx, dy, sv, wgu, wd, g_ln, alpha):
    T, D = dy.shape
    FS = wgu.shape[2]
    F = 4 * FS
    dz, dfb, dg, db = _ln_bwd(pfx + "_lnb", dy, sv["xhat"], sv["rstd"], g_ln, 0.5)

    tm, tn, tk = _rtile(T, 512), _tile(F, 512), _tile(D, 2048)

    def epi(acc, ex, outs):
        g_ref, u_ref = ex
        _, vjp = jax.vjp(_swiglu, g_ref[...], u_ref[...])
        dgate, dup = vjp(acc)
        outs[0][0] = dgate.astype(outs[0].dtype)
        outs[0][1] = dup.astype(outs[0].dtype)

    gu = pl.BlockSpec((tm, tn), lambda i, j, k: (i, j))
    (dgu,) = _mm(
        pfx + "_dh", dfb, wd, _NT, (T // tm, F // tn, D // tk),
        pl.BlockSpec((tm, tk), lambda i, j, k: (i, k)), pl.BlockSpec((tn, tk), lambda i, j, k: (j, k)), (tm, tn),
        [sv["gate"], sv["up"]], [gu, gu],
        [SDS((2, T, F), _MXU)], [pl.BlockSpec((2, tm, tn), lambda i, j, k: (0, i, j))], epi,
    )

    tm2, tk2 = _tile(F, 512), _rtile(T, 512)
    (dwd,) = _mm(
        pfx + "_dwd", sv["h"], dfb, _TN, (F // tm2, 1, T // tk2),
        pl.BlockSpec((tk2, tm2), lambda i, j, k: (k, i)), pl.BlockSpec((tk2, D), lambda i, j, k: (k, 0)), (tm2, D),
        [], [], [SDS((F, D), _GDT)], [pl.BlockSpec((tm2, D), lambda i, j, k: (i, 0))], _store_epi,
    )

    tn3 = _tile(D, 1024)

    def epi3(acc, ex, outs):
        outs[0][...] = alpha * ex[0][...] + acc

    (dx,) = _mm(
        pfx + "_dx", dgu, wgu, _NT, (T // tm, D // tn3, 8),
        pl.BlockSpec((None, tm, FS), lambda i, j, k: (k // 4, i, k % 4)),
        pl.BlockSpec((None, tn3, FS), lambda i, j, k: (k, j, 0)), (tm, tn3),
        [dz], [pl.BlockSpec((tm, tn3), lambda i, j, k: (i, j))],
        [SDS((T, D), f32)], [pl.BlockSpec((tm, tn3), lambda i, j, k: (i, j))], epi3,
    )

    tm4, tk4 = _rtile(D, 512), _tile(T, 1024)
    (dwgu,) = _mm(
        pfx + "_dwgu", jnp.transpose(sv["xb"]), dgu, _NN, (D // tm4, 8, T // tk4),
        pl.BlockSpec((tm4, tk4), lambda i, j, k: (i, k)),
        pl.BlockSpec((None, tk4, FS), lambda i, j, k: (j // 4, k, j % 4)), (tm4, FS),
        [], [], [SDS((8, D, FS), _GDT)], [pl.BlockSpec((None, tm4, FS), lambda i, j, k: (j, i, 0))], _store_epi,
    )
    return dx, dwgu, dwd, dg, db


def _zoh(a_re, a_im, log_dt, b_re_t, b_im_t):
    dt = jnp.exp(log_dt)
    mag = jnp.exp(a_re * dt)
    lr_, li_ = mag * jnp.cos(a_im * dt), mag * jnp.sin(a_im * dt)
    den = a_re * a_re + a_im * a_im
    pr, pi = lr_ - 1.0, li_
    qr, qi = a_re / den, -a_im / den
    zr, zi = pr * qr - pi * qi, pr * qi + pi * qr
    bbr = zr[None] * b_re_t - zi[None] * b_im_t
    bbi = zr[None] * b_im_t + zi[None] * b_re_t
    return lr_, li_, bbr, bbi


def _zoh_fwd(name, a_re, a_im, log_dt, b_re_t, b_im_t):
    G, P = a_re.shape
    H = b_re_t.shape[0]

    def body(ar, ai, ld, br, bi, o1, o2, o3, o4):
        r = _zoh(ar[...], ai[...], ld[...], br[...], bi[...])
        o1[...], o2[...], o3[...], o4[...] = r

    return pl.pallas_call(
        body, out_shape=[SDS((G, P), f32), SDS((G, P), f32), SDS((H, G, P), f32), SDS((H, G, P), f32)], name=name,
    )(a_re, a_im, log_dt, b_re_t, b_im_t)


def _zoh_bwd(name, a_re, a_im, log_dt, b_re_t, b_im_t, dlr, dli, dbbr, dbbi):
    G, P = a_re.shape
    H = b_re_t.shape[0]

    def body(ar, ai, ld, br, bi, g1, g2, g3, g4, o1, o2, o3, o4, o5):
        _, vjp = jax.vjp(_zoh, ar[...], ai[...], ld[...], br[...], bi[...])
        r = vjp((g1[...], g2[...], g3[...], g4[...]))
        o1[...], o2[...], o3[...], o4[...], o5[...] = r

    return pl.pallas_call(
        body, out_shape=[SDS((G, P), f32), SDS((G, P), f32), SDS((G, 1), f32), SDS((H, G, P), f32), SDS((H, G, P), f32)],
        name=name,
    )(a_re, a_im, log_dt, b_re_t, b_im_t, dlr, dli, dbbr, dbbi)


def _blockdiag(m):
    G, A, B = m.shape
    eye = jnp.eye(8, dtype=bool)
    m4 = m.reshape(G // 8, 8, A, B)
    out = jnp.where(eye[None, :, None, :, None], m4[:, :, :, None, :], jnp.zeros((), m.dtype))
    return out.reshape(G // 8, 8 * A, 8 * B)


def _blockdiag_extract(mb, A, B):
    J = mb.shape[0]
    m5 = mb.reshape(J, 8, A, 8, B)
    d = jnp.stack([m5[:, i, :, i, :] for i in range(8)], axis=1)
    return d.reshape(J * 8, A, B)


def _bd2(name, a, a_col0, b1, b2, out_dtype=f32):
    T = a.shape[0]
    J, KA, NB = b1.shape
    tm = _rtile(T, 512)

    def body(a_ref, b1_ref, b2_ref, o1, o2):
        av = a_ref[...].astype(_MXU)
        o1[...] = jnp.dot(av, b1_ref[...].astype(_MXU), preferred_element_type=f32).astype(o1.dtype)
        o2[...] = jnp.dot(av, b2_ref[...].astype(_MXU), preferred_element_type=f32).astype(o2.dtype)

    bs = pl.BlockSpec((None, KA, NB), lambda i, j: (j, 0, 0))
    os_ = pl.BlockSpec((tm, NB), lambda i, j: (i, j))
    return pl.pallas_call(
        body, grid=(T // tm, J), in_specs=[pl.BlockSpec((tm, KA), lambda i, j: (i, j + a_col0)), bs, bs],
        out_specs=[os_, os_], out_shape=[SDS((T, J * NB), out_dtype)] * 2,
        compiler_params=_cp(("parallel", "parallel")), name=name,
    )(a, b1, b2)


def _bd_sum(name, a1, a2, b1, b2, extras, extra_specs_fn, out_shape, epilogue):
    T = a1.shape[0]
    J, KA, NB = b1.shape
    tm = _rtile(T, 512)
    ne = len(extras)

    def body(*refs):
        a1_ref, a2_ref, b1_ref, b2_ref = refs[:4]
        ex = refs[4:4 + ne]
        outs = refs[4 + ne:]
        acc = jnp.dot(a1_ref[...].astype(_MXU), b1_ref[...].astype(_MXU), preferred_element_type=f32)
        acc = acc + jnp.dot(a2_ref[...].astype(_MXU), b2_ref[...].astype(_MXU), preferred_element_type=f32)
        epilogue(acc, ex, outs)

    as_ = pl.BlockSpec((tm, KA), lambda i, j: (i, j))
    bs = pl.BlockSpec((None, KA, NB), lambda i, j: (j, 0, 0))
    os_ = pl.BlockSpec((tm, NB), lambda i, j: (i, j))
    return pl.pallas_call(
        body, grid=(T // tm, J), in_specs=[as_, as_, bs, bs, *extra_specs_fn(tm, NB)],
        out_specs=[os_] * len(out_shape), out_shape=list(out_shape),
        compiler_params=_cp(("parallel", "parallel")), name=name,
    )(a1, a2, b1, b2, *extras)


def _bdT2(name, a1, a2, a_col0, b1, b2, b_col0, KA, NB, J):
    T = a1.shape[0]
    tk = _rtile(T, 512)

    def body(a1_ref, a2_ref, b1_ref, b2_ref, o1, o2):
        k = pl.program_id(1)
        p1 = lax.dot_general(a1_ref[...].astype(_MXU), b1_ref[...].astype(_MXU), _TN, preferred_element_type=f32)
        p2 = lax.dot_general(a2_ref[...].astype(_MXU), b2_ref[...].astype(_MXU), _TN, preferred_element_type=f32)

        @pl.when(k == 0)
        def _():
            o1[...] = p1
            o2[...] = p2

        @pl.when(k > 0)
        def _():
            o1[...] += p1
            o2[...] += p2

    as_ = pl.BlockSpec((tk, KA), lambda j, k: (k, j + a_col0))
    bs = pl.BlockSpec((tk, NB), lambda j, k: (k, j + b_col0))
    os_ = pl.BlockSpec((None, KA, NB), lambda j, k: (j, 0, 0))
    return pl.pallas_call(
        body, grid=(J, T // tk), in_specs=[as_, as_, bs, bs], out_specs=[os_, os_],
        out_shape=[SDS((J, KA, NB), f32)] * 2, compiler_params=_cp(("parallel", "arbitrary")), name=name,
    )(a1, a2, b1, b2)


_RB = 8


def _cmul(ar, ai, br, bi):
    return ar * br - ai * bi, ar * bi + ai * br


def _lam_powers(lr_v, li_v, cb):
    pw = {1: (lr_v, li_v)}
    for k in range(2, _RB + 1):
        pw[k] = _cmul(*pw[k - 1], lr_v, li_v)
    return pw


def _row_powers(pw, row, cb, reverse):
    outr = jnp.zeros((_RB, cb), f32)
    outi = jnp.zeros((_RB, cb), f32)
    for r in range(_RB):
        k = _RB - r if reverse else r + 1
        outr = jnp.where(row == r, pw[k][0], outr)
        outi = jnp.where(row == r, pw[k][1], outi)
    return outr, outi


def _tile_scan(xr, xi, pw, row, reverse):
    for k in (1, 2, 4):
        if reverse:
            keep = row < _RB - k
            shr, shi = pltpu.roll(xr, _RB - k, 0), pltpu.roll(xi, _RB - k, 0)
        else:
            keep = row >= k
            shr, shi = pltpu.roll(xr, k, 0), pltpu.roll(xi, k, 0)
        shr, shi = jnp.where(keep, shr, 0.0), jnp.where(keep, shi, 0.0)
        mr, mi = pw[k]
        xr, xi = xr + (mr * shr - mi * shi), xi + (mr * shi + mi * shr)
    return xr, xi


def _s5_scan(name, bur, bui, lr_, li_):
    T, N = bur.shape
    cb = _tile(N, 512)

    def body(br_ref, bi_ref, lr_ref, li_ref, sr_ref, si_ref):
        pw = _lam_powers(lr_ref[...], li_ref[...], cb)
        row = lax.broadcasted_iota(jnp.int32, (_RB, cb), 0)
        cr, ci = _row_powers(pw, row, cb, False)

        def step(n, carry):
            pr, pi = carry
            t0 = pl.multiple_of(n * _RB, _RB)
            xr, xi = _tile_scan(br_ref[pl.ds(t0, _RB), :], bi_ref[pl.ds(t0, _RB), :], pw, row, False)
            xr, xi = xr + (cr * pr - ci * pi), xi + (cr * pi + ci * pr)
            sr_ref[pl.ds(t0, _RB), :] = xr
            si_ref[pl.ds(t0, _RB), :] = xi
            return xr[_RB - 1:_RB, :], xi[_RB - 1:_RB, :]

        z = jnp.zeros((1, cb), f32)
        lax.fori_loop(0, T // _RB, step, (z, z))

    col = pl.BlockSpec((T, cb), lambda j: (0, j))
    vec = pl.BlockSpec((1, cb), lambda j: (0, j))
    return pl.pallas_call(
        body, grid=(N // cb,), in_specs=[col, col, vec, vec], out_specs=[col, col],
        out_shape=[SDS((T, N), f32)] * 2, compiler_params=_cp(("parallel",)), name=name,
    )(bur, bui, lr_, li_)


def _s5_scan_bwd(name, dsr, dsi, sr, si, lr_, li_):
    T, N = dsr.shape
    cb = _tile(N, 256)

    def body(dr_ref, di_ref, sr_ref, si_ref, lr_ref, li_ref, ar_ref, ai_ref, glr_ref, gli_ref):
        pw = _lam_powers(lr_ref[...], -li_ref[...], cb)
        row = lax.broadcasted_iota(jnp.int32, (_RB, cb), 0)
        cr, ci = _row_powers(pw, row, cb, True)
        NT = T // _RB

        def tile(t0, nxt, prev_last):
            xr, xi = _tile_scan(dr_ref[pl.ds(t0, _RB), :], di_ref[pl.ds(t0, _RB), :], pw, row, True)
            xr, xi = xr + (cr * nxt[0] - ci * nxt[1]), xi + (cr * nxt[1] + ci * nxt[0])
            ar_ref[pl.ds(t0, _RB), :] = xr
            ai_ref[pl.ds(t0, _RB), :] = xi
            pr = jnp.where(row == 0, prev_last[0], pltpu.roll(sr_ref[pl.ds(t0, _RB), :], 1, 0))
            pi = jnp.where(row == 0, prev_last[1], pltpu.roll(si_ref[pl.ds(t0, _RB), :], 1, 0))
            return xr, xi, xr * pr + xi * pi, xi * pr - xr * pi

        def step(n, carry):
            nr, ni, glr, gli = carry
            t0 = pl.multiple_of((NT - 1 - n) * _RB, _RB)
            tp = pl.multiple_of((NT - 2 - n) * _RB, _RB)
            prev_last = (sr_ref[pl.ds(tp, _RB), :][_RB - 1:_RB, :], si_ref[pl.ds(tp, _RB), :][_RB - 1:_RB, :])
            xr, xi, gr, gi = tile(t0, (nr, ni), prev_last)
            return xr[0:1, :], xi[0:1, :], glr + gr, gli + gi

        z1 = jnp.zeros((1, cb), f32)
        z8 = jnp.zeros((_RB, cb), f32)
        nr, ni, glr, gli = lax.fori_loop(0, NT - 1, step, (z1, z1, z8, z8))
        _, _, gr, gi = tile(0, (nr, ni), (z1, z1))
        glr_ref[...] = jnp.sum(glr + gr, axis=0, keepdims=True)
        gli_ref[...] = jnp.sum(gli + gi, axis=0, keepdims=True)

    col = pl.BlockSpec((T, cb), lambda j: (0, j))
    vec = pl.BlockSpec((1, cb), lambda j: (0, j))
    return pl.pallas_call(
        body, grid=(N // cb,), in_specs=[col, col, col, col, vec, vec], out_specs=[col, col, vec, vec],
        out_shape=[SDS((T, N), f32), SDS((T, N), f32), SDS((1, N), f32), SDS((1, N), f32)],
        compiler_params=_cp(("parallel",)), name=name,
    )(dsr, dsi, sr, si, lr_, li_)


def _conv_fwd(name, p, col0, w, GW3):
    T = p.shape[0]
    K = w.shape[0]
    cb = 128
    c0 = col0 // cb

    def body(x_ref, w_ref, o_ref, pad_ref):
        pad_ref[pl.ds(0, 8), :] = jnp.zeros((8, cb), f32)
        pad_ref[pl.ds(8, T), :] = x_ref[...]
        wv = w_ref[...]
        acc = jnp.zeros((T, cb), f32)
        for j in range(K):
            acc = acc + wv[j:j + 1, :] * pad_ref[pl.ds(8 - (K - 1) + j, T), :]
        o_ref[...] = jax.nn.silu(acc)

    return pl.pallas_call(
        body, grid=(GW3 // cb,),
        in_specs=[pl.BlockSpec((T, cb), lambda j: (0, j + c0)), pl.BlockSpec((K, cb), lambda j: (0, j))],
        out_specs=pl.BlockSpec((T, cb), lambda j: (0, j)), out_shape=SDS((T, GW3), f32),
        scratch_shapes=[pltpu.VMEM((T + 8, cb), f32)], compiler_params=_cp(("parallel",)), name=name,
    )(p, w)


def _conv_bwd(name, p, col0, w, dout3):
    T = p.shape[0]
    K = w.shape[0]
    GW = dout3.shape[2]
    GW3 = 3 * GW
    cb = 128
    c0 = col0 // cb
    nb = GW // cb

    def body(x_ref, w_ref, d_ref, dx_ref, dw_ref, pad_ref, dpad_ref):
        pad_ref[pl.ds(0, 8), :] = jnp.zeros((8, cb), f32)
        pad_ref[pl.ds(8, T), :] = x_ref[...]
        wv = w_ref[...]
        pre = jnp.zeros((T, cb), f32)
        for j in range(K):
            pre = pre + wv[j:j + 1, :] * pad_ref[pl.ds(8 - (K - 1) + j, T), :]
        _, vjp = jax.vjp(jax.nn.silu, pre)
        (dpre,) = vjp(d_ref[...])
        dpad_ref[pl.ds(0, T), :] = dpre
        dpad_ref[pl.ds(T, 8), :] = jnp.zeros((8, cb), f32)
        dx = jnp.zeros((T, cb), f32)
        rows = []
        for j in range(K):
            dx = dx + wv[j:j + 1, :] * dpad_ref[pl.ds((K - 1) - j, T), :]
            rows.append(jnp.sum(dpre * pad_ref[pl.ds(8 - (K - 1) + j, T), :], axis=0, keepdims=True))
        dx_ref[...] = dx.astype(dx_ref.dtype)
        for j in range(K):
            dw_ref[pl.ds(j, 1), :] = rows[j]

    return pl.pallas_call(
        body, grid=(GW3 // cb,),
        in_specs=[pl.BlockSpec((T, cb), lambda j: (0, j + c0)), pl.BlockSpec((K, cb), lambda j: (0, j)),
                  pl.BlockSpec((None, T, cb), lambda j: (j // nb, 0, j % nb))],
        out_specs=[pl.BlockSpec((T, cb), lambda j: (0, j)), pl.BlockSpec((K, cb), lambda j: (0, j))],
        out_shape=[SDS((T, GW3), _MXU), SDS((K, GW3), f32)],
        scratch_shapes=[pltpu.VMEM((T + 8, cb), f32), pltpu.VMEM((T + 8, cb), f32)],
        compiler_params=_cp(("parallel",)), name=name,
    )(p, w, dout3)


def _hdot(a, b, dims=_NN):
    return lax.dot_general(a, b, dims, precision=_HP, preferred_element_type=f32)


def _split(a):
    hi = a.astype(jnp.bfloat16)
    lo = (a - hi.astype(f32)).astype(jnp.bfloat16)
    return hi, lo


def _dot3_raw(a, b):
    ah, al = _split(a)
    bh, bl = _split(b)
    d = functools.partial(jnp.dot, preferred_element_type=f32)
    return d(ah, bh) + (d(al, bh) + d(ah, bl))


@jax.custom_vjp
def _dot3(a, b):
    return _dot3_raw(a, b)


def _dot3_fwd(a, b):
    return _dot3_raw(a, b), (a, b)


def _dot3_bwd(res, g):
    a, b = res
    return _dot3_raw(g, b.T), _dot3_raw(a.T, g)


_dot3.defvjp(_dot3_fwd, _dot3_bwd)


def _ldot(a, b, dims=_NN):
    return lax.dot_general(a.astype(_MXU), b.astype(_MXU), dims, preferred_element_type=f32)


def _sdot(a, b):
    return _ldot(a, b)


def _gdn_chunk(S, q, k, v, z, bl, ain, alog, dtb, nw):
    C, d = q.shape
    ri = lax.broadcasted_iota(jnp.int32, (C, C), 0)
    ci = lax.broadcasted_iota(jnp.int32, (C, C), 1)
    causal = ri >= ci
    strict = ri > ci
    tri = causal.astype(f32)
    qn = q * lax.rsqrt(jnp.sum(q * q, axis=-1, keepdims=True) + L2_EPS) * (d ** -0.5)
    kn = k * lax.rsqrt(jnp.sum(k * k, axis=-1, keepdims=True) + L2_EPS)
    beta = jax.nn.sigmoid(bl)
    g = -jnp.exp(alog) * jax.nn.softplus(ain + dtb)
    gb = jnp.broadcast_to(g, (C, C))
    gc_col = _dot3(tri, gb)
    gc_row = _dot3(jnp.ones((C, C), f32), jnp.where(ri <= ci, gb, 0.0))
    diff = jnp.where(causal, gc_col - gc_row, 0.0)
    decay = jnp.where(causal, jnp.exp(diff), 0.0)
    gcum = gc_col[:, 0:1]
    glast = gc_col[C - 1:C, 0:1]
    egc = jnp.exp(gcum)
    kb = kn * beta
    lower = jnp.where(strict, _ldot(kb, kn, _NT) * decay, 0.0)
    x = jnp.concatenate([v * beta, kb * egc], axis=-1)
    m = -lower
    for it in range(6):
        x = x + _sdot(m, x)
        if it < 5:
            m = _sdot(m, m)
    u_val, w_key = x[:, :d], x[:, d:]
    attn = _ldot(qn, kn, _NT) * decay
    q_dec = qn * egc
    k_dec = kn * jnp.exp(glast - gcum)
    v_new = u_val - _ldot(w_key, S)
    out = _ldot(q_dec, S) + _ldot(attn, v_new)
    s_new = S * jnp.exp(glast) + _ldot(k_dec, v_new, _TN)
    o = out * lax.rsqrt(jnp.mean(out * out, axis=-1, keepdims=True) + RMS_EPS) * nw
    o = o * jax.nn.silu(z)
    return s_new, o


def _heads_per_step(NH, HD, zcol0):
    for hb in (4, 2):
        if NH % hb == 0 and zcol0 % (hb * HD) == 0:
            return hb
    return 1


def _gdn_fwd(name, qkv, p, zcol0, blt, aint, alog, dtb, nw, NH, HD):
    T = qkv.shape[0]
    N = T // CHUNK
    GW = NH * HD
    HB = _heads_per_step(NH, HD, zcol0)
    W = HB * HD
    zc0 = zcol0 // W
    nb = GW // W

    def body(q_ref, k_ref, v_ref, z_ref, bl_ref, ain_ref, al_ref, dtb_ref, nw_ref, o_ref, ssave_ref, s_scr):
        n = pl.program_id(1)

        @pl.when(n == 0)
        def _():
            s_scr[...] = jnp.zeros_like(s_scr)

        for hh in range(HB):
            cs = slice(hh * HD, (hh + 1) * HD)
            s_in = s_scr[hh]
            ssave_ref[hh] = s_in
            s_new, o = _gdn_chunk(s_in, q_ref[:, cs], k_ref[:, cs], v_ref[:, cs], z_ref[:, cs], bl_ref[hh], ain_ref[hh],
                                  al_ref[hh], dtb_ref[hh], nw_ref[...])
            s_scr[hh] = s_new
            o_ref[:, cs] = o.astype(o_ref.dtype)

    ch = lambda off: pl.BlockSpec((CHUNK, W), lambda h, n: (n, h + off))
    sc = pl.BlockSpec((HB, CHUNK, 1), lambda h, n: (h, n, 0))
    hs = pl.BlockSpec((HB, 1, 1), lambda h, n: (h, 0, 0))
    return pl.pallas_call(
        body, grid=(NH // HB, N),
        in_specs=[ch(0), ch(nb), ch(2 * nb), ch(zc0), sc, sc, hs, hs, pl.BlockSpec((1, HD), lambda h, n: (0, 0))],
        out_specs=[pl.BlockSpec((CHUNK, W), lambda h, n: (n, h)),
                   pl.BlockSpec((HB, None, HD, HD), lambda h, n: (h, n, 0, 0))],
        out_shape=[SDS((T, GW), _MXU), SDS((NH, N, HD, HD), f32)],
        scratch_shapes=[pltpu.VMEM((HB, HD, HD), f32)], compiler_params=_cp(("parallel", "arbitrary")), name=name,
    )(qkv, qkv, qkv, p, blt, aint, alog, dtb, nw)


def _gdn_bwd(name, qkv, p, zcol0, blt, aint, alog, dtb, nw, ssave, do, NH, HD):
    T = qkv.shape[0]
    N = T // CHUNK
    GW = NH * HD
    HB = _heads_per_step(NH, HD, zcol0)
    W = HB * HD
    zc0 = zcol0 // W
    nb = GW // W

    def body(q_ref, k_ref, v_ref, z_ref, bl_ref, ain_ref, al_ref, dtb_ref, nw_ref, ss_ref, do_ref,
             dqkv_ref, dz_ref, dbl_ref, dain_ref, dal_ref, ddtb_ref, dnw_ref, ds_scr):
        h = pl.program_id(0)
        n = pl.program_id(1)

        @pl.when(n == 0)
        def _():
            ds_scr[...] = jnp.zeros_like(ds_scr)

        dnw_sum = jnp.zeros((1, HD), f32)
        for hh in range(HB):
            cs = slice(hh * HD, (hh + 1) * HD)
            _, vjp = jax.vjp(_gdn_chunk, ss_ref[hh], q_ref[:, cs], k_ref[:, cs], v_ref[:, cs], z_ref[:, cs], bl_ref[hh],
                             ain_ref[hh], al_ref[hh], dtb_ref[hh], nw_ref[...])
            ds, dq, dk, dv, dz, dbl, dain, dal, ddtb, dnw = vjp((ds_scr[hh], do_ref[:, cs].astype(f32)))
            ds_scr[hh] = ds
            dqkv_ref[0, :, cs] = dq
            dqkv_ref[1, :, cs] = dk
            dqkv_ref[2, :, cs] = dv
            dz_ref[:, cs] = dz.astype(dz_ref.dtype)
            dbl_ref[hh] = dbl
            dain_ref[hh] = dain
            dnw_sum = dnw_sum + dnw

            @pl.when(n == 0)
            def _(hh=hh, dal=dal, ddtb=ddtb):
                dal_ref[hh] = dal
                ddtb_ref[hh] = ddtb

            @pl.when(n > 0)
            def _(hh=hh, dal=dal, ddtb=ddtb):
                dal_ref[hh] += dal
                ddtb_ref[hh] += ddtb

        @pl.when((n == 0) & (h == 0))
        def _():
            dnw_ref[...] = dnw_sum

        @pl.when((n > 0) | (h > 0))
        def _():
            dnw_ref[...] += dnw_sum

    R = N - 1
    ch = lambda off: pl.BlockSpec((CHUNK, W), lambda h, n: (R - n, h + off))
    sc = pl.BlockSpec((HB, CHUNK, 1), lambda h, n: (h, R - n, 0))
    hs = pl.BlockSpec((HB, 1, 1), lambda h, n: (h, 0, 0))
    nws = pl.BlockSpec((1, HD), lambda h, n: (0, 0))
    return pl.pallas_call(
        body, grid=(NH // HB, N),
        in_specs=[ch(0), ch(nb), ch(2 * nb), ch(zc0), sc, sc, hs, hs, nws,
                  pl.BlockSpec((HB, None, HD, HD), lambda h, n: (h, R - n, 0, 0)),
                  pl.BlockSpec((CHUNK, W), lambda h, n: (R - n, h))],
        out_specs=[pl.BlockSpec((3, CHUNK, W), lambda h, n: (0, R - n, h)),
                   pl.BlockSpec((CHUNK, W), lambda h, n: (R - n, h)), sc, sc, hs, hs, nws],
        out_shape=[SDS((3, T, GW), f32), SDS((T, GW), _MXU), SDS((NH, T, 1), f32), SDS((NH, T, 1), f32),
                   SDS((NH, 1, 1), f32), SDS((NH, 1, 1), f32), SDS((1, HD), f32)],
        scratch_shapes=[pltpu.VMEM((HB, HD, HD), f32)], compiler_params=_cp(("arbitrary", "arbitrary")), name=name,
    )(qkv, qkv, qkv, p, blt, aint, alog, dtb, nw, ssave, do)


def _loss_head(name, y, tgt):
    T, D = y.shape
    tm = _rtile(T, 256)

    def body(y_ref, t_ref, dy_ref, l_ref):
        i = pl.program_id(0)
        err = y_ref[...] - t_ref[...]
        dy_ref[...] = err * (1.0 / D)
        part = 0.5 * jnp.sum(jnp.sum(err * err, axis=-1, keepdims=True) * (1.0 / D), axis=0, keepdims=True)

        @pl.when(i == 0)
        def _():
            l_ref[...] = part

        @pl.when(i > 0)
        def _():
            l_ref[...] += part

    row = pl.BlockSpec((tm, D), lambda i: (i, 0))
    return pl.pallas_call(
        body, grid=(T // tm,), in_specs=[row, row], out_specs=[row, pl.BlockSpec((1, 1), lambda i: (0, 0))],
        out_shape=[SDS((T, D), f32), SDS((1, 1), f32)], compiler_params=_cp(("arbitrary",)), name=name,
    )(y, tgt)


def _adam_math(w, g, m, v):
    m = ADAM_B1 * m + (1.0 - ADAM_B1) * g
    v = ADAM_B2 * v + (1.0 - ADAM_B2) * jnp.square(g)
    m_hat = m / (1.0 - ADAM_B1 ** ADAM_STEP)
    v_hat = v / (1.0 - ADAM_B2 ** ADAM_STEP)
    delta = -ADAM_LR * (m_hat / (jnp.sqrt(v_hat) + ADAM_EPS) + ADAM_WD * w)
    return delta, m, v


def _add_mine(name, full, recv, me, out_dtype):
    N, _, R, C = full.shape
    tr = _rtile(R, max(16, (1 << 19) // max(C, 1) // 16 * 16))

    def body(me_ref, a_ref, b_ref, o_ref):
        o_ref[...] = (a_ref[...].astype(f32) + b_ref[...].astype(f32)).astype(o_ref.dtype)

    blk = pl.BlockSpec((None, tr, C), lambda n, i, me_ref: (n, i, 0))
    return pl.pallas_call(
        body,
        grid_spec=pltpu.PrefetchScalarGridSpec(
            num_scalar_prefetch=1, grid=(N, R // tr),
            in_specs=[pl.BlockSpec((None, None, tr, C), lambda n, i, me_ref: (n, me_ref[0], i, 0)), blk], out_specs=blk),
        out_shape=SDS((N, R, C), out_dtype), compiler_params=_cp(("parallel", "parallel")), name=name,
    )(me, full, recv)


def _adamw_big(name, full, recv, me, w, m, v, l):
    _, R, C = full.shape
    tr = _rtile(R, max(16, (1 << 18) // max(C, 1) // 16 * 16))

    def body(me_ref, ga_ref, gb_ref, w_ref, m_ref, v_ref, g_ref, d_ref, nm_ref, nv_ref):
        g = ga_ref[...].astype(f32) + gb_ref[...].astype(f32)
        d, nm, nv = _adam_math(w_ref[...], g, m_ref[...], v_ref[...])
        g_ref[...] = g
        d_ref[...] = d
        nm_ref[...] = nm
        nv_ref[...] = nv

    blk = pl.BlockSpec((tr, C), lambda i, me_ref: (i, 0))
    lblk = pl.BlockSpec((None, tr, C), lambda i, me_ref: (l, i, 0))
    return pl.pallas_call(
        body,
        grid_spec=pltpu.PrefetchScalarGridSpec(
            num_scalar_prefetch=1, grid=(R // tr,),
            in_specs=[pl.BlockSpec((None, tr, C), lambda i, me_ref: (me_ref[0], i, 0)), blk, lblk, lblk, lblk],
            out_specs=[blk] * 4),
        out_shape=[SDS((R, C), f32)] * 4, compiler_params=_cp(("parallel",)), name=name,
    )(me, full, recv, w, m, v)


def _adamw_small(name, gall, w, m, v):
    _, R, C = gall.shape
    tr = _rtile(R, 512)

    def body(ga_ref, w_ref, m_ref, v_ref, g_ref, d_ref, nm_ref, nv_ref):
        g = ga_ref[0]
        for s in range(1, 8):
            g = g + ga_ref[s]
        d, nm, nv = _adam_math(w_ref[...], g, m_ref[...], v_ref[...])
        g_ref[...] = g
        d_ref[...] = d
        nm_ref[...] = nm
        nv_ref[...] = nv

    blk = pl.BlockSpec((tr, C), lambda i: (i, 0))
    return pl.pallas_call(
        body, grid=(R // tr,), in_specs=[pl.BlockSpec((8, tr, C), lambda i: (0, i, 0)), blk, blk, blk], out_specs=[blk] * 4,
        out_shape=[SDS((R, C), f32)] * 4, compiler_params=_cp(("parallel",)), name=name,
    )(gall, w, m, v)


def _peer(axis):
    x, y, c = lax.axis_index("x"), lax.axis_index("y"), lax.axis_index("c")
    me = {"x": x, "y": y, "c": c}[axis]
    peer = {"x": (1 - x, y, c), "y": (x, 1 - y, c), "c": (x, y, 1 - c)}[axis]
    return me, peer


def _held(ref, done):
    idx = tuple(slice(None) if a in done else lax.axis_index(a) for a in ("x", "y", "c"))
    return ref.at[idx]


def _gather_stage(name, bufs, axes, dones):
    n = len(bufs)
    hbm = pl.BlockSpec(memory_space=pltpu.HBM)

    def body(*refs):
        outs = refs[n:2 * n]
        send_sems, recv_sems = refs[2 * n:]
        cps = []
        for t in range(n):
            _, peer = _peer(axes[t])
            blk = _held(outs[t], dones[t])
            cps.append(pltpu.make_async_remote_copy(src_ref=blk, dst_ref=blk, send_sem=send_sems.at[t],
                                                    recv_sem=recv_sems.at[t], device_id=peer, device_id_type=_MESH_T))
        for cp in cps:
            cp.start()
        for cp in cps:
            cp.wait()

    return pl.pallas_call(
        body, in_specs=[hbm] * n, out_specs=[hbm] * n, out_shape=[SDS(b.shape, b.dtype) for b in bufs],
        input_output_aliases={t: t for t in range(n)},
        scratch_shapes=[pltpu.SemaphoreType.DMA((n,)), pltpu.SemaphoreType.DMA((n,))], name=name,
    )(*bufs)


def _scatter_stage(name, tensors, axes):
    n = len(tensors)
    hbm = pl.BlockSpec(memory_space=pltpu.HBM)

    def body(*refs):
        ins, recvs = refs[:n], refs[n:2 * n]
        send_sems, recv_sems = refs[2 * n:]
        cps = []
        for t in range(n):
            me, peer = _peer(axes[t])
            cps.append(pltpu.make_async_remote_copy(
                src_ref=ins[t].at[:, 1 - me], dst_ref=recvs[t], send_sem=send_sems.at[t], recv_sem=recv_sems.at[t],
                device_id=peer, device_id_type=_MESH_T))
        for cp in cps:
            cp.start()
        for cp in cps:
            cp.wait()

    return pl.pallas_call(
        body, in_specs=[hbm] * n, out_specs=[hbm] * n,
        out_shape=[SDS((t.shape[0],) + tuple(t.shape[2:]), t.dtype) for t in tensors],
        scratch_shapes=[pltpu.SemaphoreType.DMA((n,)), pltpu.SemaphoreType.DMA((n,))], name=name,
    )(*tensors)


def _coord(axis):
    return lax.axis_index(axis).astype(jnp.int32).reshape(1)


def _all_gather(pfx, tensors, paths):
    x, y, c = (lax.axis_index(a) for a in ("x", "y", "c"))
    bufs = []
    for t in tensors:
        zero = (0,) * t.ndim
        bufs.append(lax.dynamic_update_slice(lax.empty((2, 2, 2) + tuple(t.shape), t.dtype), t[None, None, None],
                                             (x, y, c) + zero))
    orders = [tuple(p) + ("c",) for p in paths]
    for ph in range(3):
        bufs = _gather_stage(f"{pfx}_{ph}", bufs, [o[ph] for o in orders], [o[:ph] for o in orders])
    return [b.reshape((8,) + tuple(t.shape)) for b, t in zip(bufs, tensors)]


def _reduce_scatter(pfx, tensors, paths):
    rcs = [tuple(t.shape[1:]) for t in tensors]
    orders = [("c",) + tuple(p) for p in paths]
    left = [["x", "y", "c"] for _ in tensors]
    cur = list(tensors)
    for ph in range(3):
        views = []
        for i, (t, rc) in enumerate(zip(cur, rcs)):
            pos = left[i].index(orders[i][ph])
            nb, na = 2 ** pos, 2 ** (len(left[i]) - pos - 1)
            views.append(t.reshape((nb, 2, na * rc[0], rc[1])))
        recvs = _scatter_stage(f"{pfx}_{ph}", views, [o[ph] for o in orders])
        if ph == 2:
            return [(v[0], r[0], o[2]) for v, r, o in zip(views, recvs, orders)]
        cur = [_add_mine(f"{pfx}_add{ph}_{i}", v, r, _coord(o[ph]), v.dtype)
               for i, (v, r, o) in enumerate(zip(views, recvs, orders))]
        for i, o in enumerate(orders):
            left[i].remove(o[ph])


def _mm_nn(name, a, w, out_dtype, tn_pref=1024):
    T, K = a.shape
    N = w.shape[1]
    tm, tn, tk = _rtile(T, 512), _tile(N, tn_pref), _tile(K, 2048)
    return _mm(
        name, a, w, _NN, (T // tm, N // tn, K // tk),
        pl.BlockSpec((tm, tk), lambda i, j, k: (i, k)), pl.BlockSpec((tk, tn), lambda i, j, k: (k, j)), (tm, tn),
        [], [], [SDS((T, N), out_dtype)], [pl.BlockSpec((tm, tn), lambda i, j, k: (i, j))], _store_epi,
    )[0]


def _mm_tn(name, a, b, out_dtype):
    T, M = a.shape
    N = b.shape[1]
    tm, tn, tk = _tile(M, 512), _tile(N, 2048), _rtile(T, 512)
    return _mm(
        name, a, b, _TN, (M // tm, N // tn, T // tk),
        pl.BlockSpec((tk, tm), lambda i, j, k: (k, i)), pl.BlockSpec((tk, tn), lambda i, j, k: (k, j)), (tm, tn),
        [], [], [SDS((M, N), out_dtype)], [pl.BlockSpec((tm, tn), lambda i, j, k: (i, j))], _store_epi,
    )[0]


def _mm_tn_slots(name, a, b, out_dtype):
    T, M = a.shape
    NS = b.shape[1] // 8
    tm, tk = _tile(M, 512), _rtile(T, 512)
    return _mm(
        name, a, b, _TN, (M // tm, 8, T // tk),
        pl.BlockSpec((tk, tm), lambda i, j, k: (k, i)), pl.BlockSpec((tk, NS), lambda i, j, k: (k, j)), (tm, NS),
        [], [], [SDS((8, M, NS), out_dtype)], [pl.BlockSpec((None, tm, NS), lambda i, j, k: (j, i, 0))], _store_epi,
    )[0]


def _mm_nt_slots(name, a, w8, out_dtype):
    T = a.shape[0]
    _, M, NS = w8.shape
    tm, tn = _rtile(T, 512), _tile(M, 1024)
    return _mm(
        name, a, w8, _NT, (T // tm, M // tn, 8),
        pl.BlockSpec((tm, NS), lambda i, j, k: (i, k)), pl.BlockSpec((None, tn, NS), lambda i, j, k: (k, j, 0)), (tm, tn),
        [], [], [SDS((T, M), out_dtype)], [pl.BlockSpec((tm, tn), lambda i, j, k: (i, j))], _store_epi,
    )[0]


def _colsum_kernel(name, fn, ins, in_cols, outs_elem, n_sum, C):
    T = ins[0].shape[0]
    tm = _rtile(T, 256)
    ne = len(outs_elem)

    def body(*refs):
        i = pl.program_id(0)
        iv = [r[...] for r in refs[:len(ins)]]
        res = fn(*iv)
        for o, r in zip(refs[len(ins):len(ins) + ne], res[:ne]):
            o[...] = r.astype(o.dtype)
        sums = [jnp.sum(r, axis=0, keepdims=True) for r in res[ne:]]

        @pl.when(i == 0)
        def _():
            for o, s in zip(refs[len(ins) + ne:], sums):
                o[...] = s

        @pl.when(i > 0)
        def _():
            for o, s in zip(refs[len(ins) + ne:], sums):
                o[...] += s

    in_specs = []
    for arr, off in zip(ins, in_cols):
        if off is None:
            in_specs.append(pl.BlockSpec((1, C), lambda i: (0, 0)))
        else:
            in_specs.append(pl.BlockSpec((tm, C), lambda i, off=off: (i, off)))
    row = pl.BlockSpec((tm, C), lambda i: (i, 0))
    vec = pl.BlockSpec((1, C), lambda i: (0, 0))
    return pl.pallas_call(
        body, grid=(T // tm,), in_specs=in_specs, out_specs=[row] * ne + [vec] * n_sum,
        out_shape=[SDS((T, C), dt) for dt in outs_elem] + [SDS((1, C), f32)] * n_sum,
        compiler_params=_cp(("arbitrary",)), name=name,
    )(*ins)


def _merge(gs, gg, a_s, a_g):
    return jax.nn.sigmoid(gs) * a_s + jax.nn.sigmoid(gg) * a_g


def _glu(yg, lp):
    return yg * jax.nn.sigmoid(lp)


_BIG = ("ffn1_w_gu", "ffn1_w_down", "w_in", "conv_w", "glu_w", "w_br_ssm", "w_br_gdn", "w_out", "ffn2_w_gu", "ffn2_w_down")
_PATHS = ("yx", "yx", "xy", "xy", "yx", "yx", "yx", "yx", "xy", "xy")
_SMALL = ("ln1_g", "ln1_b", "ssm_a_re", "ssm_a_im", "ssm_log_dt", "ssm_b_re", "ssm_b_im", "ssm_c_re", "ssm_c_im", "ssm_d",
          "glu_b", "gdn_a_log", "gdn_dt_bias", "gdn_norm_w", "ln2_g", "ln2_b", "ln3_g", "ln3_b")
_ORDER = ("ffn1_w_gu", "ffn1_w_down", "ln1_g", "ln1_b", "w_in", "conv_w", "ssm_a_re", "ssm_a_im", "ssm_log_dt", "ssm_b_re",
          "ssm_b_im", "ssm_c_re", "ssm_c_im", "ssm_d", "glu_w", "glu_b", "gdn_a_log", "gdn_dt_bias", "gdn_norm_w", "w_br_ssm",
          "w_br_gdn", "w_out", "ln2_g", "ln2_b", "ffn2_w_gu", "ffn2_w_down", "ln3_g", "ln3_b")


def _step(x, tgt, W, M, V):
    T, D = x.shape[1], x.shape[2]
    L = W["ffn1_w_gu"].shape[0]
    G, P = W["ssm_a_re"].shape[1:]
    H = W["ssm_b_re"].shape[3]
    SW = G * H
    NH = W["gdn_a_log"].shape[1]
    HD = W["gdn_norm_w"].shape[1]
    GW = NH * HD
    KC = W["conv_w"].shape[1]
    DS = D // 8
    alpha = (2.0 * L) ** 0.25
    o_b = SW + 4 * GW
    o_gs = o_b + 2 * NH
    IN = o_gs + 2 * D
    NM = IN - 2 * NH
    m_qkv, m_z, m_gs, m_gg = SW, SW + 3 * GW, SW + 4 * GW, SW + 4 * GW + D
    J = G // 8

    x0 = x[0]
    tg = tgt[0]

    def vec(name, l):
        return W[name][l:l + 1]

    saves, weights = [], []
    xc, xcb = x0, x0.astype(_MXU)
    for l in range(L):
        gathered = _all_gather("ag", [
            W["ffn1_w_gu"][l].astype(_MXU), W["ffn1_w_down"][l].astype(_MXU), W["w_in"][l].astype(_MXU), W["conv_w"][l],
            W["glu_w"][l].astype(_MXU), W["w_br_ssm"][l].astype(_MXU), W["w_br_gdn"][l].astype(_MXU),
            W["w_out"][l].astype(_MXU), W["ffn2_w_gu"][l].astype(_MXU), W["ffn2_w_down"][l].astype(_MXU)], _PATHS)
        wgu1, wd1, win8, cw8, wglu, wbs, wbg, wo, wgu2, wd2 = gathered
        wd1 = wd1.reshape(-1, D)
        wd2 = wd2.reshape(-1, D)
        wglu = wglu.reshape(SW, SW)
        wo = wo.reshape(D, D)
        win = jnp.transpose(win8, (1, 0, 2)).reshape(D, IN)
        wmain = jnp.concatenate([win[:, :o_b], win[:, o_gs:]], axis=1)
        wba = jnp.pad(win[:, o_b:o_gs], ((0, 0), (0, 128 - 2 * NH)))
        cw = jnp.transpose(cw8, (1, 0, 2)).reshape(KC, 3 * GW)
        wl = dict(wgu1=wgu1, wd1=wd1, wmain=wmain, wba=wba, cw=cw, wglu=wglu, wbs=wbs, wbg=wbg, wo=wo, wgu2=wgu2, wd2=wd2)
        weights.append(wl)
        sv = {}

        gate, up, hh = _ffn_up("ffn_up", xcb, wgu1)
        x1, x1b, xh1, r1 = _mm_ln("ffn_down_ln", hh, wd1, xc, vec("ln1_g", l), vec("ln1_b", l), alpha, 0.5)
        sv["f1"] = dict(xb=xcb, gate=gate, up=up, h=hh, xhat=xh1, rstd=r1)

        p = _mm_nn("mix_in", x1b, wmain, f32)
        pba = _mm_nn("mix_in_ba", x1b, wba, f32)
        b_re_t = jnp.transpose(W["ssm_b_re"][l], (2, 0, 1))
        b_im_t = jnp.transpose(W["ssm_b_im"][l], (2, 0, 1))
        zoh_in = (W["ssm_a_re"][l], W["ssm_a_im"][l], W["ssm_log_dt"][l][:, None], b_re_t, b_im_t)
        lbr, lbi, bbr_t, bbi_t = _zoh_fwd("zoh", *zoh_in)
        bblk_r = _blockdiag(jnp.transpose(bbr_t, (1, 0, 2)))
        bblk_i = _blockdiag(jnp.transpose(bbi_t, (1, 0, 2)))
        cblkT_r = _blockdiag(W["ssm_c_re"][l])
        cblkT_in = _blockdiag(-W["ssm_c_im"][l])
        lbr_f, lbi_f = lbr.reshape(1, G * P), lbi.reshape(1, G * P)
        bur, bui = _bd2("s5_bu", p, 0, bblk_r, bblk_i)
        sr, si = _s5_scan("s5_scan", bur, bui, lbr_f, lbi_f)
        dflat = W["ssm_d"][l].reshape(1, SW)

        def out_epi(acc, ex, outs):
            y_raw = acc + ex[1][...] * ex[0][...]
            yg = jax.nn.gelu(y_raw)
            outs[0][...] = y_raw
            outs[1][...] = yg
            outs[2][...] = yg.astype(outs[2].dtype)

        y_raw, yg, ygb = _bd_sum(
            "s5_out", sr, si, jnp.transpose(cblkT_r, (0, 2, 1)), jnp.transpose(cblkT_in, (0, 2, 1)), [p, dflat],
            lambda tm, nb: [pl.BlockSpec((tm, nb), lambda i, j: (i, j)), pl.BlockSpec((1, nb), lambda i, j: (0, j))],
            [SDS((T, SW), f32), SDS((T, SW), f32), SDS((T, SW), _MXU)], out_epi)

        tmg, tng, tkg = _rtile(T, 512), _tile(SW, 512), _tile(SW, 512)

        def glu_epi(acc, ex, outs):
            lp = acc + ex[1][...]
            outs[0][...] = lp
            outs[1][...] = _glu(ex[0][...], lp).astype(outs[1].dtype)

        lp, ysb = _mm(
            "s5_glu", ygb, wglu, _NN, (T // tmg, SW // tng, SW // tkg),
            pl.BlockSpec((tmg, tkg), lambda i, j, k: (i, k)), pl.BlockSpec((tkg, tng), lambda i, j, k: (k, j)), (tmg, tng),
            [yg, vec("glu_b", l)], [pl.BlockSpec((tmg, tng), lambda i, j, k: (i, j)), pl.BlockSpec((1, tng), lambda i, j, k: (0, j))],
            [SDS((T, SW), f32), SDS((T, SW), _MXU)], [pl.BlockSpec((tmg, tng), lambda i, j, k: (i, j))] * 2, glu_epi)

        qkv = _conv_fwd("gdn_conv", p, m_qkv, cw, 3 * GW)
        blt = jnp.transpose(pba[:, :NH])[:, :, None]
        aint = jnp.transpose(pba[:, NH:2 * NH])[:, :, None]
        alog = W["gdn_a_log"][l].reshape(NH, 1, 1)
        dtb = W["gdn_dt_bias"][l].reshape(NH, 1, 1)
        nw = vec("gdn_norm_w", l)
        og, ssave = _gdn_fwd("gdn", qkv, p, m_z, blt, aint, alog, dtb, nw, NH, HD)

        a_s = _mm(
            "br_ssm", ysb, wbs, _NN, (T // tmg, 8, SW // tkg),
            pl.BlockSpec((tmg, tkg), lambda i, j, k: (i, k)), pl.BlockSpec((None, tkg, DS), lambda i, j, k: (j, k, 0)), (tmg, DS),
            [], [], [SDS((T, D), f32)], [pl.BlockSpec((tmg, DS), lambda i, j, k: (i, j))], _store_epi)[0]
        tkd = _tile(GW, 512)
        gsb, ggb = m_gs // DS, m_gg // DS

        def merge_epi(acc, ex, outs):
            outs[0][...] = acc
            outs[1][...] = _merge(ex[1][...], ex[2][...], ex[0][...], acc).astype(outs[1].dtype)

        tile_ij = pl.BlockSpec((tmg, DS), lambda i, j, k: (i, j))
        a_g, merged = _mm(
            "br_gdn_merge", og, wbg, _NN, (T // tmg, 8, GW // tkd),
            pl.BlockSpec((tmg, tkd), lambda i, j, k: (i, k)), pl.BlockSpec((None, tkd, DS), lambda i, j, k: (j, k, 0)), (tmg, DS),
            [a_s, p, p], [tile_ij, pl.BlockSpec((tmg, DS), lambda i, j, k: (i, j + gsb)),
                          pl.BlockSpec((tmg, DS), lambda i, j, k: (i, j + ggb))],
            [SDS((T, D), f32), SDS((T, D), _MXU)], [tile_ij, tile_ij], merge_epi)
        x2, x2b, xh2, r2 = _mm_ln("mix_out_ln", merged, wo, x1, vec("ln2_g", l), vec("ln2_b", l), alpha, 1.0)
        sv["mx"] = dict(x1b=x1b, p=p, zoh_in=zoh_in, lbr_f=lbr_f, lbi_f=lbi_f, bblk_r=bblk_r, bblk_i=bblk_i, cblkT_r=cblkT_r,
                        cblkT_in=cblkT_in, sr=sr, si=si, dflat=dflat, y_raw=y_raw, yg=yg, ygb=ygb, lp=lp, ysb=ysb, qkv=qkv,
                        blt=blt, aint=aint, alog=alog, dtb=dtb, nw=nw, og=og, ssave=ssave, a_s=a_s, a_g=a_g, merged=merged,
                        xhat=xh2, rstd=r2)

        gate2, up2, hh2 = _ffn_up("ffn_up", x2b, wgu2)
        x3, x3b, xh3, r3 = _mm_ln("ffn_down_ln", hh2, wd2, x2, vec("ln3_g", l), vec("ln3_b", l), alpha, 0.5)
        sv["f2"] = dict(xb=x2b, gate=gate2, up=up2, h=hh2, xhat=xh3, rstd=r3)
        saves.append(sv)
        xc, xcb = x3, x3b

    dy, loss_part = _loss_head("loss_head", xc, tg)
    loss = lax.psum(loss_part[0, 0], ("x", "y", "c"))

    big_out = {n: [None] * L for n in _BIG}
    small_g = {n: [None] * L for n in _SMALL}
    for l in reversed(range(L)):
        sv, wl = saves[l], weights[l]
        mx = sv["mx"]
        p = mx["p"]
        dx2, dwgu2, dwd2, dg3, db3 = _ffn_bwd("ffn_b", dy, sv["f2"], wl["wgu2"], wl["wd2"], vec("ln3_g", l), alpha)
        small_g["ln3_g"][l], small_g["ln3_b"][l] = dg3[0], db3[0]

        dz2, dmixb, dg2, db2 = _ln_bwd("mix_lnb", dx2, mx["xhat"], mx["rstd"], vec("ln2_g", l), 1.0)
        small_g["ln2_g"][l], small_g["ln2_b"][l] = dg2[0], db2[0]
        tmg, tkd = _rtile(T, 512), _tile(D, 512)
        gsb, ggb = m_gs // DS, m_gg // DS

        def dmerge_epi(acc, ex, outs):
            _, vjp = jax.vjp(_merge, ex[0][...], ex[1][...], ex[2][...], ex[3][...])
            dgs, dgg, das, dag = vjp(acc)
            outs[0][...] = das.astype(outs[0].dtype)
            outs[1][...] = dag.astype(outs[1].dtype)
            outs[2][...] = dgs.astype(outs[2].dtype)
            outs[3][...] = dgg.astype(outs[3].dtype)

        tile_ij = pl.BlockSpec((tmg, DS), lambda i, j, k: (i, j))
        das, dag, dgs, dgg = _mm(
            "mix_dmerge", dmixb, wl["wo"], _NT, (T // tmg, 8, D // tkd),
            pl.BlockSpec((tmg, tkd), lambda i, j, k: (i, k)), pl.BlockSpec((DS, tkd), lambda i, j, k: (j, k)), (tmg, DS),
            [p, p, mx["a_s"], mx["a_g"]],
            [pl.BlockSpec((tmg, DS), lambda i, j, k: (i, j + gsb)), pl.BlockSpec((tmg, DS), lambda i, j, k: (i, j + ggb)),
             tile_ij, tile_ij],
            [SDS((T, D), _MXU)] * 4, [tile_ij] * 4, dmerge_epi)
        dwo = _mm_tn("mix_dwo", mx["merged"], dmixb, _GDT)
        dys = _mm_nt_slots("br_ssm_dx", das, wl["wbs"], f32)
        dog = _mm_nt_slots("br_gdn_dx", dag, wl["wbg"], f32)
        dwbs = _mm_tn_slots("br_ssm_dw", mx["ysb"], das, _GDT)
        dwbg = _mm_tn_slots("br_gdn_dw", mx["og"], dag, _GDT)

        def glu_b_fn(dys_t, yg_t, lp_t):
            _, vjp = jax.vjp(_glu, yg_t, lp_t)
            dyg1, dlp = vjp(dys_t)
            return dyg1, dlp, dlp

        dyg1, dlpb, dglub = _colsum_kernel("s5_glu_b", glu_b_fn, [dys, mx["yg"], mx["lp"]], [0, 0, 0], [f32, _MXU], 1, SW)
        small_g["glu_b"][l] = dglub[0]
        dwglu = _mm_tn("s5_dwglu", mx["ygb"], dlpb, _GDT)
        tng, tkg = _tile(SW, 512), _tile(SW, 512)

        def dyraw_epi(acc, ex, outs):
            _, vjp = jax.vjp(jax.nn.gelu, ex[1][...])
            (d,) = vjp(ex[0][...] + acc)
            outs[0][...] = d

        t_ij = pl.BlockSpec((tmg, tng), lambda i, j, k: (i, j))
        (dyraw,) = _mm(
            "s5_dyraw", dlpb, wl["wglu"], _NT, (T // tmg, SW // tng, SW // tkg),
            pl.BlockSpec((tmg, tkg), lambda i, j, k: (i, k)), pl.BlockSpec((tng, tkg), lambda i, j, k: (j, k)), (tmg, tng),
            [dyg1, mx["y_raw"]], [t_ij, t_ij], [SDS((T, SW), f32)], [t_ij], dyraw_epi)

        def dd_fn(dyr, u_t, d_t):
            return d_t * dyr, dyr * u_t

        dud, dd = _colsum_kernel("s5_dd", dd_fn, [dyraw, p, mx["dflat"]], [0, 0, None], [f32], 1, SW)
        small_g["ssm_d"][l] = dd.reshape(G, H)
        dsr, dsi = _bd2("s5_ds", dyraw, 0, mx["cblkT_r"], mx["cblkT_in"])
        dcb_r, dcb_i = _bdT2("s5_dc", mx["sr"], mx["si"], 0, dyraw, dyraw, 0, 8 * P, 8 * H, J)
        small_g["ssm_c_re"][l] = _blockdiag_extract(jnp.transpose(dcb_r, (0, 2, 1)), H, P)
        small_g["ssm_c_im"][l] = -_blockdiag_extract(jnp.transpose(dcb_i, (0, 2, 1)), H, P)
        ar, ai, dlr, dli = _s5_scan_bwd("s5_scan_b", dsr, dsi, mx["sr"], mx["si"], mx["lbr_f"], mx["lbi_f"])

        def du_epi(acc, ex, outs):
            outs[0][...] = (acc + ex[0][...]).astype(outs[0].dtype)

        (du,) = _bd_sum(
            "s5_du", ar, ai, jnp.transpose(mx["bblk_r"], (0, 2, 1)), jnp.transpose(mx["bblk_i"], (0, 2, 1)), [dud],
            lambda tm, nb: [pl.BlockSpec((tm, nb), lambda i, j: (i, j))], [SDS((T, SW), _MXU)], du_epi)
        dbb_r, dbb_i = _bdT2("s5_db", p, p, 0, ar, ai, 0, 8 * H, 8 * P, J)
        dbbr_t = jnp.transpose(_blockdiag_extract(dbb_r, H, P), (1, 0, 2))
        dbbi_t = jnp.transpose(_blockdiag_extract(dbb_i, H, P), (1, 0, 2))
        da_re, da_im, dlog_dt, dbre_t, dbim_t = _zoh_bwd("zoh_b", *mx["zoh_in"], dlr.reshape(G, P), dli.reshape(G, P),
                                                         dbbr_t, dbbi_t)
        small_g["ssm_a_re"][l], small_g["ssm_a_im"][l], small_g["ssm_log_dt"][l] = da_re, da_im, dlog_dt[:, 0]
        small_g["ssm_b_re"][l] = jnp.transpose(dbre_t, (1, 2, 0))
        small_g["ssm_b_im"][l] = jnp.transpose(dbim_t, (1, 2, 0))

        dqkv3, dzb, dbl, dain, dal, ddtb, dnw = _gdn_bwd("gdn_b", mx["qkv"], p, m_z, mx["blt"], mx["aint"], mx["alog"],
                                                         mx["dtb"], mx["nw"], mx["ssave"], dog, NH, HD)
        small_g["gdn_a_log"][l], small_g["gdn_dt_bias"][l], small_g["gdn_norm_w"][l] = dal[:, 0, 0], ddtb[:, 0, 0], dnw[0]
        dqkv_pre, dcw = _conv_bwd("gdn_conv_b", p, m_qkv, wl["cw"], dqkv3)

        dpm = jnp.concatenate([du, dqkv_pre, dzb, dgs, dgg], axis=1)
        dpba = jnp.concatenate([jnp.transpose(dbl[:, :, 0]), jnp.transpose(dain[:, :, 0]),
                                jnp.zeros((T, 128 - 2 * NH), f32)], axis=1).astype(_MXU)
        tnd, tkm = _tile(D, 1024), _tile(NM, 512)
        t_ba = _mm(
            "mix_dx_ba", dpba, wl["wba"], _NT, (T // tmg, D // tnd, 1),
            pl.BlockSpec((tmg, 128), lambda i, j, k: (i, 0)), pl.BlockSpec((tnd, 128), lambda i, j, k: (j, 0)), (tmg, tnd),
            [], [], [SDS((T, D), f32)], [pl.BlockSpec((tmg, tnd), lambda i, j, k: (i, j))], _store_epi)[0]

        def dx1_epi(acc, ex, outs):
            outs[0][...] = alpha * ex[0][...] + ex[1][...] + acc

        t_d = pl.BlockSpec((tmg, tnd), lambda i, j, k: (i, j))
        (dx1,) = _mm(
            "mix_dx", dpm, wl["wmain"], _NT, (T // tmg, D // tnd, NM // tkm),
            pl.BlockSpec((tmg, tkm), lambda i, j, k: (i, k)), pl.BlockSpec((tnd, tkm), lambda i, j, k: (j, k)), (tmg, tnd),
            [dz2, t_ba], [t_d, t_d], [SDS((T, D), f32)], [t_d], dx1_epi)
        tnm = _tile(NM, 1024)
        tkt = _tile(T, 1024)
        dwmain = _mm(
            "mix_dw", jnp.transpose(mx["x1b"]), dpm, _NN, (D // tkd, NM // tnm, T // tkt),
            pl.BlockSpec((tkd, tkt), lambda i, j, k: (i, k)), pl.BlockSpec((tkt, tnm), lambda i, j, k: (k, j)), (tkd, tnm),
            [], [], [SDS((D, NM), _GDT)], [pl.BlockSpec((tkd, tnm), lambda i, j, k: (i, j))], _store_epi)[0]
        dwba = _mm_tn("mix_dw_ba", mx["x1b"], dpba, _GDT)
        dwin = jnp.concatenate([dwmain[:, :o_b], dwba[:, :2 * NH], dwmain[:, o_b:]], axis=1)
        dwin8 = jnp.transpose(dwin.reshape(D, 8, IN // 8), (1, 0, 2))
        dcw8 = jnp.transpose(dcw.reshape(KC, 8, 3 * GW // 8), (1, 0, 2))

        dx0, dwgu1, dwd1, dg1, db1 = _ffn_bwd("ffn_b", dx1, sv["f1"], wl["wgu1"], wl["wd1"], vec("ln1_g", l), alpha)
        small_g["ln1_g"][l], small_g["ln1_b"][l] = dg1[0], db1[0]
        dy = dx0

        parts = [dwgu1, dwd1.reshape(8, -1, D), dwin8, dcw8, dwglu.reshape(8, SW // 8, SW), dwbs, dwbg,
                 dwo.reshape(8, DS, D), dwgu2, dwd2.reshape(8, -1, D)]
        for n, (full, recv, last) in zip(_BIG, _reduce_scatter("rs", parts, _PATHS)):
            big_out[n][l] = _adamw_big("adamw_" + n, full, recv, _coord(last), W[n], M[n], V[n], l)

    def pack(arrs):
        flat = jnp.concatenate([a.reshape(-1) for a in arrs])
        n = flat.shape[0]
        rows = -(-n // (128 * 16)) * 16
        return jnp.pad(flat, (0, rows * 128 - n)).reshape(rows, 128)

    gs_full = [jnp.stack(small_g[n]).reshape(W[n].shape) for n in _SMALL]
    gpack = pack(gs_full)
    (gall,) = _all_gather("ag_small", [gpack], ["yx"])
    sg, sd, sm, sv_ = _adamw_small("adamw_small", gall, pack([W[n] for n in _SMALL]), pack([M[n] for n in _SMALL]),
                                   pack([V[n] for n in _SMALL]))

    def unpack(packed):
        flat = packed.reshape(-1)
        out, off = {}, 0
        for n in _SMALL:
            sz = math.prod(W[n].shape)
            out[n] = flat[off:off + sz].reshape(W[n].shape)
            off += sz
        return out

    res = [unpack(a) for a in (sg, sd, sm, sv_)]
    for n in _BIG:
        for i in range(4):
            res[i][n] = jnp.stack([big_out[n][l][i] for l in range(L)])
    outs = [loss, dy[None]]
    for i in range(4):
        outs += [res[i][n] for n in _ORDER]
    return tuple(outs)


def kernel(x, ffn1_w_gu, ffn1_w_down, ln1_g, ln1_b, w_in, conv_w, ssm_a_re, ssm_a_im, ssm_log_dt, ssm_b_re, ssm_b_im, ssm_c_re, ssm_c_im, ssm_d, glu_w, glu_b, gdn_a_log, gdn_dt_bias, gdn_norm_w, w_br_ssm, w_br_gdn, w_out, ln2_g, ln2_b, ffn2_w_gu, ffn2_w_down, ln3_g, ln3_b, loss_target, m_ffn1_w_gu, m_ffn1_w_down, m_ln1_g, m_ln1_b, m_w_in, m_conv_w, m_ssm_a_re, m_ssm_a_im, m_ssm_log_dt, m_ssm_b_re, m_ssm_b_im, m_ssm_c_re, m_ssm_c_im, m_ssm_d, m_glu_w, m_glu_b, m_gdn_a_log, m_gdn_dt_bias, m_gdn_norm_w, m_w_br_ssm, m_w_br_gdn, m_w_out, m_ln2_g, m_ln2_b, m_ffn2_w_gu, m_ffn2_w_down, m_ln3_g, m_ln3_b, v_ffn1_w_gu, v_ffn1_w_down, v_ln1_g, v_ln1_b, v_w_in, v_conv_w, v_ssm_a_re, v_ssm_a_im, v_ssm_log_dt, v_ssm_b_re, v_ssm_b_im, v_ssm_c_re, v_ssm_c_im, v_ssm_d, v_glu_w, v_glu_b, v_gdn_a_log, v_gdn_dt_bias, v_gdn_norm_w, v_w_br_ssm, v_w_br_gdn, v_w_out, v_ln2_g, v_ln2_b, v_ffn2_w_gu, v_ffn2_w_down, v_ln3_g, v_ln3_b):
    given = dict(locals())
    W = {n: given[n] for n in _ORDER}
    M = {n: given["m_" + n] for n in _ORDER}
    V = {n: given["v_" + n] for n in _ORDER}
    return _step(x, loss_target, W, M, V)
```

```python
import functools
import math

import jax
import jax.numpy as jnp
from jax import lax
from jax.experimental import pallas as pl
from jax.experimental.pallas import tpu as pltpu

f32 = jnp.float32
_MXU = jnp.bfloat16
_GDT = jnp.bfloat16
_HP = lax.Precision.HIGHEST
_VMEM_LIMIT = 56 * 1024 * 1024
_MESH_T = pl.DeviceIdType.MESH

LN_EPS = 1e-5
RMS_EPS = 1e-6
L2_EPS = 1e-6
CHUNK = 64
ADAM_LR = 0.001
ADAM_B1 = 0.9
ADAM_B2 = 0.999
ADAM_EPS = 1e-08
ADAM_WD = 0.01
ADAM_STEP = 10

_NN = (((1,), (0,)), ((), ()))
_NT = (((1,), (1,)), ((), ()))
_TN = (((0,), (0,)), ((), ()))

SDS = jax.ShapeDtypeStruct


def _cp(sem):
    return pltpu.CompilerParams(dimension_semantics=sem, vmem_limit_bytes=_VMEM_LIMIT)


def _tile(n, pref):
    if n <= pref:
        return n
    t = (pref // 128) * 128
    while t >= 128:
        if n % t == 0:
            return t
        t -= 128
    return n


def _rtile(n, pref):
    if n <= pref:
        return n
    t = (pref // 16) * 16
    while t >= 16:
        if n % t == 0:
            return t
        t -= 16
    return n


def _mm(name, a, b, dims, grid, a_spec, b_spec, acc_shape, extras, extra_specs, out_shape, out_specs, epilogue):
    nk = grid[2]
    ne = len(extras)
    no = len(out_shape)

    def body(*refs):
        a_ref, b_ref = refs[0], refs[1]
        ex = refs[2:2 + ne]
        outs = refs[2 + ne:2 + ne + no]
        acc = refs[-1]
        k = pl.program_id(2)
        part = lax.dot_general(a_ref[...].astype(_MXU), b_ref[...].astype(_MXU), dims, preferred_element_type=f32)

        @pl.when(k == 0)
        def _():
            acc[...] = part

        @pl.when(k > 0)
        def _():
            acc[...] += part

        @pl.when(k == nk - 1)
        def _():
            epilogue(acc[...], ex, outs)

    return pl.pallas_call(
        body, grid=grid, in_specs=[a_spec, b_spec, *extra_specs], out_specs=list(out_specs), out_shape=list(out_shape),
        scratch_shapes=[pltpu.VMEM(acc_shape, f32)], compiler_params=_cp(("parallel", "parallel", "arbitrary")), name=name,
    )(a, b, *extras)


def _store_epi(acc, ex, outs):
    for o in outs:
        o[...] = acc.astype(o.dtype)


def _ln_epilogue(alpha, c):
    def epi(acc, ex, outs):
        x_ref, g_ref, b_ref = ex
        y_ref, yb_ref, xh_ref, r_ref = outs
        z = alpha * x_ref[...] + c * acc
        mu = jnp.mean(z, axis=-1, keepdims=True)
        zc = z - mu
        var = jnp.mean(zc * zc, axis=-1, keepdims=True)
        r = lax.rsqrt(var + LN_EPS)
        xh = zc * r
        y = xh * g_ref[...] + b_ref[...]
        y_ref[...] = y
        yb_ref[...] = y.astype(yb_ref.dtype)
        xh_ref[...] = xh
        r_ref[...] = r
    return epi


def _mm_ln(name, a, w, x, g, b, alpha, c):
    T, K = a.shape
    D = w.shape[1]
    tm, tk = _rtile(T, 512), _tile(K, 512)
    row = pl.BlockSpec((tm, D), lambda i, j, k: (i, 0))
    vec = pl.BlockSpec((1, D), lambda i, j, k: (0, 0))
    return _mm(
        name, a, w, _NN, (T // tm, 1, K // tk),
        pl.BlockSpec((tm, tk), lambda i, j, k: (i, k)), pl.BlockSpec((tk, D), lambda i, j, k: (k, 0)), (tm, D),
        [x, g, b], [row, vec, vec],
        [SDS((T, D), f32), SDS((T, D), _MXU), SDS((T, D), f32), SDS((T, 1), f32)],
        [row, row, row, pl.BlockSpec((tm, 1), lambda i, j, k: (i, 0))],
        _ln_epilogue(alpha, c),
    )


def _ln_bwd(name, dy, xhat, rstd, g, c):
    T, D = dy.shape
    tm = _rtile(T, 256)

    def body(dy_ref, xh_ref, r_ref, g_ref, dz_ref, df_ref, dg_ref, db_ref):
        i = pl.program_id(0)
        dyv = dy_ref[...]
        xh = xh_ref[...]
        dxh = dyv * g_ref[...]
        m1 = jnp.mean(dxh, axis=-1, keepdims=True)
        m2 = jnp.mean(dxh * xh, axis=-1, keepdims=True)
        dz = r_ref[...] * (dxh - m1 - xh * m2)
        dz_ref[...] = dz
        df_ref[...] = (c * dz).astype(df_ref.dtype)
        pg = jnp.sum(dyv * xh, axis=0, keepdims=True)
        pb = jnp.sum(dyv, axis=0, keepdims=True)

        @pl.when(i == 0)
        def _():
            dg_ref[...] = pg
            db_ref[...] = pb

        @pl.when(i > 0)
        def _():
            dg_ref[...] += pg
            db_ref[...] += pb

    row = pl.BlockSpec((tm, D), lambda i: (i, 0))
    vec = pl.BlockSpec((1, D), lambda i: (0, 0))
    return pl.pallas_call(
        body, grid=(T // tm,), in_specs=[row, row, pl.BlockSpec((tm, 1), lambda i: (i, 0)), vec],
        out_specs=[row, row, vec, vec],
        out_shape=[SDS((T, D), f32), SDS((T, D), _MXU), SDS((1, D), f32), SDS((1, D), f32)],
        compiler_params=_cp(("arbitrary",)), name=name,
    )(dy, xhat, rstd, g)


def _swiglu(g, u):
    return jax.nn.silu(g) * u


def _ffn_up(name, xb, wgu):
    T, D = xb.shape
    FS = wgu.shape[2]
    F = 4 * FS
    tm = _rtile(T, 256)

    def body(x_ref, wg_ref, wu_ref, g_ref, u_ref, h_ref):
        xv = x_ref[...]
        g = jnp.dot(xv, wg_ref[...], preferred_element_type=f32)
        u = jnp.dot(xv, wu_ref[...], preferred_element_type=f32)
        g_ref[...] = g
        u_ref[...] = u
        h_ref[...] = _swiglu(g, u).astype(h_ref.dtype)

    out = pl.BlockSpec((tm, FS), lambda j, i: (i, j))
    return pl.pallas_call(
        body, grid=(4, T // tm),
        in_specs=[pl.BlockSpec((tm, D), lambda j, i: (i, 0)),
                  pl.BlockSpec((None, D, FS), lambda j, i: (j, 0, 0)),
                  pl.BlockSpec((None, D, FS), lambda j, i: (j + 4, 0, 0))],
        out_specs=[out, out, out],
        out_shape=[SDS((T, F), f32), SDS((T, F), f32), SDS((T, F), _MXU)],
        compiler_params=_cp(("parallel", "arbitrary")), name=name,
    )(xb, wgu, wgu)


def _ffn_bwd(pfx, dy, sv, wgu, wd, g_ln, alpha):
    T, D = dy.shape
    FS = wgu.shape[2]
    F = 4 * FS
    dz, dfb, dg, db = _ln_bwd(pfx + "_lnb", dy, sv["xhat"], sv["rstd"], g_ln, 0.5)

    tm, tn, tk = _rtile(T, 512), _tile(F, 512), _tile(D, 2048)

    def epi(acc, ex, outs):
        g_ref, u_ref = ex
        _, vjp = jax.vjp(_swiglu, g_ref[...], u_ref[...])
        dgate, dup = vjp(acc)
        outs[0][0] = dgate.astype(outs[0].dtype)
        outs[0][1] = dup.astype(outs[0].dtype)

    gu = pl.BlockSpec((tm, tn), lambda i, j, k: (i, j))
    (dgu,) = _mm(
        pfx + "_dh", dfb, wd, _NT, (T // tm, F // tn, D // tk),
        pl.BlockSpec((tm, tk), lambda i, j, k: (i, k)), pl.BlockSpec((tn, tk), lambda i, j, k: (j, k)), (tm, tn),
        [sv["gate"], sv["up"]], [gu, gu],
        [SDS((2, T, F), _MXU)], [pl.BlockSpec((2, tm, tn), lambda i, j, k: (0, i, j))], epi,
    )

    tm2, tk2 = _tile(F, 512), _rtile(T, 512)
    (dwd,) = _mm(
        pfx + "_dwd", sv["h"], dfb, _TN, (F // tm2, 1, T // tk2),
        pl.BlockSpec((tk2, tm2), lambda i, j, k: (k, i)), pl.BlockSpec((tk2, D), lambda i, j, k: (k, 0)), (tm2, D),
        [], [], [SDS((F, D), _GDT)], [pl.BlockSpec((tm2, D), lambda i, j, k: (i, 0))], _store_epi,
    )

    tn3 = _tile(D, 1024)

    def epi3(acc, ex, outs):
        outs[0][...] = alpha * ex[0][...] + acc

    (dx,) = _mm(
        pfx + "_dx", dgu, wgu, _NT, (T // tm, D // tn3, 8),
        pl.BlockSpec((None, tm, FS), lambda i, j, k: (k // 4, i, k % 4)),
        pl.BlockSpec((None, tn3, FS), lambda i, j, k: (k, j, 0)), (tm, tn3),
        [dz], [pl.BlockSpec((tm, tn3), lambda i, j, k: (i, j))],
        [SDS((T, D), f32)], [pl.BlockSpec((tm, tn3), lambda i, j, k: (i, j))], epi3,
    )

    tm4, tk4 = _rtile(D, 512), _tile(T, 1024)
    (dwgu,) = _mm(
        pfx + "_dwgu", jnp.transpose(sv["xb"]), dgu, _NN, (D // tm4, 8, T // tk4),
        pl.BlockSpec((tm4, tk4), lambda i, j, k: (i, k)),
        pl.BlockSpec((None, tk4, FS), lambda i, j, k: (j // 4, k, j % 4)), (tm4, FS),
        [], [], [SDS((8, D, FS), _GDT)], [pl.BlockSpec((None, tm4, FS), lambda i, j, k: (j, i, 0))], _store_epi,
    )
    return dx, dwgu, dwd, dg, db


def _zoh(a_re, a_im, log_dt, b_re_t, b_im_t):
    dt = jnp.exp(log_dt)
    mag = jnp.exp(a_re * dt)
    lr_, li_ = mag * jnp.cos(a_im * dt), mag * jnp.sin(a_im * dt)
    den = a_re * a_re + a_im * a_im
    pr, pi = lr_ - 1.0, li_
    qr, qi = a_re / den, -a_im / den
    zr, zi = pr * qr - pi * qi, pr * qi + pi * qr
    bbr = zr[None] * b_re_t - zi[None] * b_im_t
    bbi = zr[None] * b_im_t + zi[None] * b_re_t
    return lr_, li_, bbr, bbi


def _zoh_fwd(name, a_re, a_im, log_dt, b_re_t, b_im_t):
    G, P = a_re.shape
    H = b_re_t.shape[0]

    def body(ar, ai, ld, br, bi, o1, o2, o3, o4):
        r = _zoh(ar[...], ai[...], ld[...], br[...], bi[...])
        o1[...], o2[...], o3[...], o4[...] = r

    return pl.pallas_call(
        body, out_shape=[SDS((G, P), f32), SDS((G, P), f32), SDS((H, G, P), f32), SDS((H, G, P), f32)], name=name,
    )(a_re, a_im, log_dt, b_re_t, b_im_t)


def _zoh_bwd(name, a_re, a_im, log_dt, b_re_t, b_im_t, dlr, dli, dbbr, dbbi):
    G, P = a_re.shape
    H = b_re_t.shape[0]

    def body(ar, ai, ld, br, bi, g1, g2, g3, g4, o1, o2, o3, o4, o5):
        _, vjp = jax.vjp(_zoh, ar[...], ai[...], ld[...], br[...], bi[...])
        r = vjp((g1[...], g2[...], g3[...], g4[...]))
        o1[...], o2[...], o3[...], o4[...], o5[...] = r

    return pl.pallas_call(
        body, out_shape=[SDS((G, P), f32), SDS((G, P), f32), SDS((G, 1), f32), SDS((H, G, P), f32), SDS((H, G, P), f32)],
        name=name,
    )(a_re, a_im, log_dt, b_re_t, b_im_t, dlr, dli, dbbr, dbbi)


def _blockdiag(m):
    G, A, B = m.shape
    eye = jnp.eye(8, dtype=bool)
    m4 = m.reshape(G // 8, 8, A, B)
    out = jnp.where(eye[None, :, None, :, None], m4[:, :, :, None, :], jnp.zeros((), m.dtype))
    return out.reshape(G // 8, 8 * A, 8 * B)


def _blockdiag_extract(mb, A, B):
    J = mb.shape[0]
    m5 = mb.reshape(J, 8, A, 8, B)
    d = jnp.stack([m5[:, i, :, i, :] for i in range(8)], axis=1)
    return d.reshape(J * 8, A, B)


def _bd2(name, a, a_col0, b1, b2, out_dtype=f32):
    T = a.shape[0]
    J, KA, NB = b1.shape
    tm = _rtile(T, 512)

    def body(a_ref, b1_ref, b2_ref, o1, o2):
        av = a_ref[...].astype(_MXU)
        o1[...] = jnp.dot(av, b1_ref[...].astype(_MXU), preferred_element_type=f32).astype(o1.dtype)
        o2[...] = jnp.dot(av, b2_ref[...].astype(_MXU), preferred_element_type=f32).astype(o2.dtype)

    bs = pl.BlockSpec((None, KA, NB), lambda i, j: (j, 0, 0))
    os_ = pl.BlockSpec((tm, NB), lambda i, j: (i, j))
    return pl.pallas_call(
        body, grid=(T // tm, J), in_specs=[pl.BlockSpec((tm, KA), lambda i, j: (i, j + a_col0)), bs, bs],
        out_specs=[os_, os_], out_shape=[SDS((T, J * NB), out_dtype)] * 2,
        compiler_params=_cp(("parallel", "parallel")), name=name,
    )(a, b1, b2)


def _bd_sum(name, a1, a2, b1, b2, extras, extra_specs_fn, out_shape, epilogue):
    T = a1.shape[0]
    J, KA, NB = b1.shape
    tm = _rtile(T, 512)
    ne = len(extras)

    def body(*refs):
        a1_ref, a2_ref, b1_ref, b2_ref = refs[:4]
        ex = refs[4:4 + ne]
        outs = refs[4 + ne:]
        acc = jnp.dot(a1_ref[...].astype(_MXU), b1_ref[...].astype(_MXU), preferred_element_type=f32)
        acc = acc + jnp.dot(a2_ref[...].astype(_MXU), b2_ref[...].astype(_MXU), preferred_element_type=f32)
        epilogue(acc, ex, outs)

    as_ = pl.BlockSpec((tm, KA), lambda i, j: (i, j))
    bs = pl.BlockSpec((None, KA, NB), lambda i, j: (j, 0, 0))
    os_ = pl.BlockSpec((tm, NB), lambda i, j: (i, j))
    return pl.pallas_call(
        body, grid=(T // tm, J), in_specs=[as_, as_, bs, bs, *extra_specs_fn(tm, NB)],
        out_specs=[os_] * len(out_shape), out_shape=list(out_shape),
        compiler_params=_cp(("parallel", "parallel")), name=name,
    )(a1, a2, b1, b2, *extras)


def _bdT2(name, a1, a2, a_col0, b1, b2, b_col0, KA, NB, J):
    T = a1.shape[0]
    tk = _rtile(T, 512)

    def body(a1_ref, a2_ref, b1_ref, b2_ref, o1, o2):
        k = pl.program_id(1)
        p1 = lax.dot_general(a1_ref[...].astype(_MXU), b1_ref[...].astype(_MXU), _TN, preferred_element_type=f32)
        p2 = lax.dot_general(a2_ref[...].astype(_MXU), b2_ref[...].astype(_MXU), _TN, preferred_element_type=f32)

        @pl.when(k == 0)
        def _():
            o1[...] = p1
            o2[...] = p2

        @pl.when(k > 0)
        def _():
            o1[...] += p1
            o2[...] += p2

    as_ = pl.BlockSpec((tk, KA), lambda j, k: (k, j + a_col0))
    bs = pl.BlockSpec((tk, NB), lambda j, k: (k, j + b_col0))
    os_ = pl.BlockSpec((None, KA, NB), lambda j, k: (j, 0, 0))
    return pl.pallas_call(
        body, grid=(J, T // tk), in_specs=[as_, as_, bs, bs], out_specs=[os_, os_],
        out_shape=[SDS((J, KA, NB), f32)] * 2, compiler_params=_cp(("parallel", "arbitrary")), name=name,
    )(a1, a2, b1, b2)


_RB = 8


def _cmul(ar, ai, br, bi):
    return ar * br - ai * bi, ar * bi + ai * br


def _lam_powers(lr_v, li_v, cb):
    pw = {1: (lr_v, li_v)}
    for k in range(2, _RB + 1):
        pw[k] = _cmul(*pw[k - 1], lr_v, li_v)
    return pw


def _row_powers(pw, row, cb, reverse):
    outr = jnp.zeros((_RB, cb), f32)
    outi = jnp.zeros((_RB, cb), f32)
    for r in range(_RB):
        k = _RB - r if reverse else r + 1
        outr = jnp.where(row == r, pw[k][0], outr)
        outi = jnp.where(row == r, pw[k][1], outi)
    return outr, outi


def _tile_scan(xr, xi, pw, row, reverse):
    for k in (1, 2, 4):
        if reverse:
            keep = row < _RB - k
            shr, shi = pltpu.roll(xr, _RB - k, 0), pltpu.roll(xi, _RB - k, 0)
        else:
            keep = row >= k
            shr, shi = pltpu.roll(xr, k, 0), pltpu.roll(xi, k, 0)
        shr, shi = jnp.where(keep, shr, 0.0), jnp.where(keep, shi, 0.0)
        mr, mi = pw[k]
        xr, xi = xr + (mr * shr - mi * shi), xi + (mr * shi + mi * shr)
    return xr, xi


def _s5_scan(name, bur, bui, lr_, li_):
    T, N = bur.shape
    cb = _tile(N, 512)

    def body(br_ref, bi_ref, lr_ref, li_ref, sr_ref, si_ref):
        pw = _lam_powers(lr_ref[...], li_ref[...], cb)
        row = lax.broadcasted_iota(jnp.int32, (_RB, cb), 0)
        cr, ci = _row_powers(pw, row, cb, False)

        def step(n, carry):
            pr, pi = carry
            t0 = pl.multiple_of(n * _RB, _RB)
            xr, xi = _tile_scan(br_ref[pl.ds(t0, _RB), :], bi_ref[pl.ds(t0, _RB), :], pw, row, False)
            xr, xi = xr + (cr * pr - ci * pi), xi + (cr * pi + ci * pr)
            sr_ref[pl.ds(t0, _RB), :] = xr
            si_ref[pl.ds(t0, _RB), :] = xi
            return xr[_RB - 1:_RB, :], xi[_RB - 1:_RB, :]

        z = jnp.zeros((1, cb), f32)
        lax.fori_loop(0, T // _RB, step, (z, z))

    col = pl.BlockSpec((T, cb), lambda j: (0, j))
    vec = pl.BlockSpec((1, cb), lambda j: (0, j))
    return pl.pallas_call(
        body, grid=(N // cb,), in_specs=[col, col, vec, vec], out_specs=[col, col],
        out_shape=[SDS((T, N), f32)] * 2, compiler_params=_cp(("parallel",)), name=name,
    )(bur, bui, lr_, li_)


def _s5_scan_bwd(name, dsr, dsi, sr, si, lr_, li_):
    T, N = dsr.shape
    cb = _tile(N, 256)

    def body(dr_ref, di_ref, sr_ref, si_ref, lr_ref, li_ref, ar_ref, ai_ref, glr_ref, gli_ref):
        pw = _lam_powers(lr_ref[...], -li_ref[...], cb)
        row = lax.broadcasted_iota(jnp.int32, (_RB, cb), 0)
        cr, ci = _row_powers(pw, row, cb, True)
        NT = T // _RB

        def tile(t0, nxt, prev_last):
            xr, xi = _tile_scan(dr_ref[pl.ds(t0, _RB), :], di_ref[pl.ds(t0, _RB), :], pw, row, True)
            xr, xi = xr + (cr * nxt[0] - ci * nxt[1]), xi + (cr * nxt[1] + ci * nxt[0])
            ar_ref[pl.ds(t0, _RB), :] = xr
            ai_ref[pl.ds(t0, _RB), :] = xi
            pr = jnp.where(row == 0, prev_last[0], pltpu.roll(sr_ref[pl.ds(t0, _RB), :], 1, 0))
            pi = jnp.where(row == 0, prev_last[1], pltpu.roll(si_ref[pl.ds(t0, _RB), :], 1, 0))
            return xr, xi, xr * pr + xi * pi, xi * pr - xr * pi

        def step(n, carry):
            nr, ni, glr, gli = carry
            t0 = pl.multiple_of((NT - 1 - n) * _RB, _RB)
            tp = pl.multiple_of((NT - 2 - n) * _RB, _RB)
            prev_last = (sr_ref[pl.ds(tp, _RB), :][_RB - 1:_RB, :], si_ref[pl.ds(tp, _RB), :][_RB - 1:_RB, :])
            xr, xi, gr, gi = tile(t0, (nr, ni), prev_last)
            return xr[0:1, :], xi[0:1, :], glr + gr, gli + gi

        z1 = jnp.zeros((1, cb), f32)
        z8 = jnp.zeros((_RB, cb), f32)
        nr, ni, glr, gli = lax.fori_loop(0, NT - 1, step, (z1, z1, z8, z8))
        _, _, gr, gi = tile(0, (nr, ni), (z1, z1))
        glr_ref[...] = jnp.sum(glr + gr, axis=0, keepdims=True)
        gli_ref[...] = jnp.sum(gli + gi, axis=0, keepdims=True)

    col = pl.BlockSpec((T, cb), lambda j: (0, j))
    vec = pl.BlockSpec((1, cb), lambda j: (0, j))
    return pl.pallas_call(
        body, grid=(N // cb,), in_specs=[col, col, col, col, vec, vec], out_specs=[col, col, vec, vec],
        out_shape=[SDS((T, N), f32), SDS((T, N), f32), SDS((1, N), f32), SDS((1, N), f32)],
        compiler_params=_cp(("parallel",)), name=name,
    )(dsr, dsi, sr, si, lr_, li_)


def _conv_fwd(name, p, col0, w, GW3):
    T = p.shape[0]
    K = w.shape[0]
    cb = 128
    c0 = col0 // cb

    def body(x_ref, w_ref, o_ref, pad_ref):
        pad_ref[pl.ds(0, 8), :] = jnp.zeros((8, cb), f32)
        pad_ref[pl.ds(8, T), :] = x_ref[...]
        wv = w_ref[...]
        acc = jnp.zeros((T, cb), f32)
        for j in range(K):
            acc = acc + wv[j:j + 1, :] * pad_ref[pl.ds(8 - (K - 1) + j, T), :]
        o_ref[...] = jax.nn.silu(acc)

    return pl.pallas_call(
        body, grid=(GW3 // cb,),
        in_specs=[pl.BlockSpec((T, cb), lambda j: (0, j + c0)), pl.BlockSpec((K, cb), lambda j: (0, j))],
        out_specs=pl.BlockSpec((T, cb), lambda j: (0, j)), out_shape=SDS((T, GW3), f32),
        scratch_shapes=[pltpu.VMEM((T + 8, cb), f32)], compiler_params=_cp(("parallel",)), name=name,
    )(p, w)


def _conv_bwd(name, p, col0, w, dout3):
    T = p.shape[0]
    K = w.shape[0]
    GW = dout3.shape[2]
    GW3 = 3 * GW
    cb = 128
    c0 = col0 // cb
    nb = GW // cb

    def body(x_ref, w_ref, d_ref, dx_ref, dw_ref, pad_ref, dpad_ref):
        pad_ref[pl.ds(0, 8), :] = jnp.zeros((8, cb), f32)
        pad_ref[pl.ds(8, T), :] = x_ref[...]
        wv = w_ref[...]
        pre = jnp.zeros((T, cb), f32)
        for j in range(K):
            pre = pre + wv[j:j + 1, :] * pad_ref[pl.ds(8 - (K - 1) + j, T), :]
        _, vjp = jax.vjp(jax.nn.silu, pre)
        (dpre,) = vjp(d_ref[...])
        dpad_ref[pl.ds(0, T), :] = dpre
        dpad_ref[pl.ds(T, 8), :] = jnp.zeros((8, cb), f32)
        dx = jnp.zeros((T, cb), f32)
        rows = []
        for j in range(K):
            dx = dx + wv[j:j + 1, :] * dpad_ref[pl.ds((K - 1) - j, T), :]
            rows.append(jnp.sum(dpre * pad_ref[pl.ds(8 - (K - 1) + j, T), :], axis=0, keepdims=True))
        dx_ref[...] = dx.astype(dx_ref.dtype)
        for j in range(K):
            dw_ref[pl.ds(j, 1), :] = rows[j]

    return pl.pallas_call(
        body, grid=(GW3 // cb,),
        in_specs=[pl.BlockSpec((T, cb), lambda j: (0, j + c0)), pl.BlockSpec((K, cb), lambda j: (0, j)),
                  pl.BlockSpec((None, T, cb), lambda j: (j // nb, 0, j % nb))],
        out_specs=[pl.BlockSpec((T, cb), lambda j: (0, j)), pl.BlockSpec((K, cb), lambda j: (0, j))],
        out_shape=[SDS((T, GW3), _MXU), SDS((K, GW3), f32)],
        scratch_shapes=[pltpu.VMEM((T + 8, cb), f32), pltpu.VMEM((T + 8, cb), f32)],
        compiler_params=_cp(("parallel",)), name=name,
    )(p, w, dout3)


def _hdot(a, b, dims=_NN):
    return lax.dot_general(a, b, dims, precision=_HP, preferred_element_type=f32)


def _split(a):
    hi = a.astype(jnp.bfloat16)
    lo = (a - hi.astype(f32)).astype(jnp.bfloat16)
    return hi, lo


def _dot3_raw(a, b):
    ah, al = _split(a)
    bh, bl = _split(b)
    d = functools.partial(jnp.dot, preferred_element_type=f32)
    return d(ah, bh) + (d(al, bh) + d(ah, bl))


@jax.custom_vjp
def _dot3(a, b):
    return _dot3_raw(a, b)


def _dot3_fwd(a, b):
    return _dot3_raw(a, b), (a, b)


def _dot3_bwd(res, g):
    a, b = res
    return _dot3_raw(g, b.T), _dot3_raw(a.T, g)


_dot3.defvjp(_dot3_fwd, _dot3_bwd)


def _ldot(a, b, dims=_NN):
    return lax.dot_general(a.astype(_MXU), b.astype(_MXU), dims, preferred_element_type=f32)


def _sdot(a, b):
    return _ldot(a, b)


def _gdn_chunk(S, q, k, v, z, bl, ain, alog, dtb, nw):
    C, d = q.shape
    ri = lax.broadcasted_iota(jnp.int32, (C, C), 0)
    ci = lax.broadcasted_iota(jnp.int32, (C, C), 1)
    causal = ri >= ci
    strict = ri > ci
    tri = causal.astype(f32)
    qn = q * lax.rsqrt(jnp.sum(q * q, axis=-1, keepdims=True) + L2_EPS) * (d ** -0.5)
    kn = k * lax.rsqrt(jnp.sum(k * k, axis=-1, keepdims=True) + L2_EPS)
    beta = jax.nn.sigmoid(bl)
    g = -jnp.exp(alog) * jax.nn.softplus(ain + dtb)
    gb = jnp.broadcast_to(g, (C, C))
    gc_col = _dot3(tri, gb)
    gc_row = _dot3(jnp.ones((C, C), f32), jnp.where(ri <= ci, gb, 0.0))
    diff = jnp.where(causal, gc_col - gc_row, 0.0)
    decay = jnp.where(causal, jnp.exp(diff), 0.0)
    gcum = gc_col[:, 0:1]
    glast = gc_col[C - 1:C, 0:1]
    egc = jnp.exp(gcum)
    kb = kn * beta
    lower = jnp.where(strict, _ldot(kb, kn, _NT) * decay, 0.0)
    x = jnp.concatenate([v * beta, kb * egc], axis=-1)
    m = -lower
    for it in range(6):
        x = x + _sdot(m, x)
        if it < 5:
            m = _sdot(m, m)
    u_val, w_key = x[:, :d], x[:, d:]
    attn = _ldot(qn, kn, _NT) * decay
    q_dec = qn * egc
    k_dec = kn * jnp.exp(glast - gcum)
    v_new = u_val - _ldot(w_key, S)
    out = _ldot(q_dec, S) + _ldot(attn, v_new)
    s_new = S * jnp.exp(glast) + _ldot(k_dec, v_new, _TN)
    o = out * lax.rsqrt(jnp.mean(out * out, axis=-1, keepdims=True) + RMS_EPS) * nw
    o = o * jax.nn.silu(z)
    return s_new, o


def _heads_per_step(NH, HD, zcol0):
    for hb in (4, 2):
        if NH % hb == 0 and zcol0 % (hb * HD) == 0:
            return hb
    return 1


def _gdn_fwd(name, qkv, p, zcol0, blt, aint, alog, dtb, nw, NH, HD):
    T = qkv.shape[0]
    N = T // CHUNK
    GW = NH * HD
    HB = _heads_per_step(NH, HD, zcol0)
    W = HB * HD
    zc0 = zcol0 // W
    nb = GW // W

    def body(q_ref, k_ref, v_ref, z_ref, bl_ref, ain_ref, al_ref, dtb_ref, nw_ref, o_ref, ssave_ref, s_scr):
        n = pl.program_id(1)

        @pl.when(n == 0)
        def _():
            s_scr[...] = jnp.zeros_like(s_scr)

        for hh in range(HB):
            cs = slice(hh * HD, (hh + 1) * HD)
            s_in = s_scr[hh]
            ssave_ref[hh] = s_in
            s_new, o = _gdn_chunk(s_in, q_ref[:, cs], k_ref[:, cs], v_ref[:, cs], z_ref[:, cs], bl_ref[hh], ain_ref[hh],
                                  al_ref[hh], dtb_ref[hh], nw_ref[...])
            s_scr[hh] = s_new
            o_ref[:, cs] = o.astype(o_ref.dtype)

    ch = lambda off: pl.BlockSpec((CHUNK, W), lambda h, n: (n, h + off))
    sc = pl.BlockSpec((HB, CHUNK, 1), lambda h, n: (h, n, 0))
    hs = pl.BlockSpec((HB, 1, 1), lambda h, n: (h, 0, 0))
    return pl.pallas_call(
        body, grid=(NH // HB, N),
        in_specs=[ch(0), ch(nb), ch(2 * nb), ch(zc0), sc, sc, hs, hs, pl.BlockSpec((1, HD), lambda h, n: (0, 0))],
        out_specs=[pl.BlockSpec((CHUNK, W), lambda h, n: (n, h)),
                   pl.BlockSpec((HB, None, HD, HD), lambda h, n: (h, n, 0, 0))],
        out_shape=[SDS((T, GW), _MXU), SDS((NH, N, HD, HD), f32)],
        scratch_shapes=[pltpu.VMEM((HB, HD, HD), f32)], compiler_params=_cp(("parallel", "arbitrary")), name=name,
    )(qkv, qkv, qkv, p, blt, aint, alog, dtb, nw)


def _gdn_bwd(name, qkv, p, zcol0, blt, aint, alog, dtb, nw, ssave, do, NH, HD):
    T = qkv.shape[0]
    N = T // CHUNK
    GW = NH * HD
    HB = _heads_per_step(NH, HD, zcol0)
    W = HB * HD
    zc0 = zcol0 // W
    nb = GW // W

    def body(q_ref, k_ref, v_ref, z_ref, bl_ref, ain_ref, al_ref, dtb_ref, nw_ref, ss_ref, do_ref,
             dqkv_ref, dz_ref, dbl_ref, dain_ref, dal_ref, ddtb_ref, dnw_ref, ds_scr):
        h = pl.program_id(0)
        n = pl.program_id(1)

        @pl.when(n == 0)
        def _():
            ds_scr[...] = jnp.zeros_like(ds_scr)

        dnw_sum = jnp.zeros((1, HD), f32)
        for hh in range(HB):
            cs = slice(hh * HD, (hh + 1) * HD)
            _, vjp = jax.vjp(_gdn_chunk, ss_ref[hh], q_ref[:, cs], k_ref[:, cs], v_ref[:, cs], z_ref[:, cs], bl_ref[hh],
                             ain_ref[hh], al_ref[hh], dtb_ref[hh], nw_ref[...])
            ds, dq, dk, dv, dz, dbl, dain, dal, ddtb, dnw = vjp((ds_scr[hh], do_ref[:, cs].astype(f32)))
            ds_scr[hh] = ds
            dqkv_ref[0, :, cs] = dq
            dqkv_ref[1, :, cs] = dk
            dqkv_ref[2, :, cs] = dv
            dz_ref[:, cs] = dz.astype(dz_ref.dtype)
            dbl_ref[hh] = dbl
            dain_ref[hh] = dain
            dnw_sum = dnw_sum + dnw

            @pl.when(n == 0)
            def _(hh=hh, dal=dal, ddtb=ddtb):
                dal_ref[hh] = dal
                ddtb_ref[hh] = ddtb

            @pl.when(n > 0)
            def _(hh=hh, dal=dal, ddtb=ddtb):
                dal_ref[hh] += dal
                ddtb_ref[hh] += ddtb

        @pl.when((n == 0) & (h == 0))
        def _():
            dnw_ref[...] = dnw_sum

        @pl.when((n > 0) | (h > 0))
        def _():
            dnw_ref[...] += dnw_sum

    R = N - 1
    ch = lambda off: pl.BlockSpec((CHUNK, W), lambda h, n: (R - n, h + off))
    sc = pl.BlockSpec((HB, CHUNK, 1), lambda h, n: (h, R - n, 0))
    hs = pl.BlockSpec((HB, 1, 1), lambda h, n: (h, 0, 0))
    nws = pl.BlockSpec((1, HD), lambda h, n: (0, 0))
    return pl.pallas_call(
        body, grid=(NH // HB, N),
        in_specs=[ch(0), ch(nb), ch(2 * nb), ch(zc0), sc, sc, hs, hs, nws,
                  pl.BlockSpec((HB, None, HD, HD), lambda h, n: (h, R - n, 0, 0)),
                  pl.BlockSpec((CHUNK, W), lambda h, n: (R - n, h))],
        out_specs=[pl.BlockSpec((3, CHUNK, W), lambda h, n: (0, R - n, h)),
                   pl.BlockSpec((CHUNK, W), lambda h, n: (R - n, h)), sc, sc, hs, hs, nws],
        out_shape=[SDS((3, T, GW), f32), SDS((T, GW), _MXU), SDS((NH, T, 1), f32), SDS((NH, T, 1), f32),
                   SDS((NH, 1, 1), f32), SDS((NH, 1, 1), f32), SDS((1, HD), f32)],
        scratch_shapes=[pltpu.VMEM((HB, HD, HD), f32)], compiler_params=_cp(("arbitrary", "arbitrary")), name=name,
    )(qkv, qkv, qkv, p, blt, aint, alog, dtb, nw, ssave, do)


def _loss_head(name, y, tgt):
    T, D = y.shape
    tm = _rtile(T, 256)

    def body(y_ref, t_ref, dy_ref, l_ref):
        i = pl.program_id(0)
        err = y_ref[...] - t_ref[...]
        dy_ref[...] = err * (1.0 / D)
        part = 0.5 * jnp.sum(jnp.sum(err * err, axis=-1, keepdims=True) * (1.0 / D), axis=0, keepdims=True)

        @pl.when(i == 0)
        def _():
            l_ref[...] = part

        @pl.when(i > 0)
        def _():
            l_ref[...] += part

    row = pl.BlockSpec((tm, D), lambda i: (i, 0))
    return pl.pallas_call(
        body, grid=(T // tm,), in_specs=[row, row], out_specs=[row, pl.BlockSpec((1, 1), lambda i: (0, 0))],
        out_shape=[SDS((T, D), f32), SDS((1, 1), f32)], compiler_params=_cp(("arbitrary",)), name=name,
    )(y, tgt)


def _adam_math(w, g, m, v):
    m = ADAM_B1 * m + (1.0 - ADAM_B1) * g
    v = ADAM_B2 * v + (1.0 - ADAM_B2) * jnp.square(g)
    m_hat = m / (1.0 - ADAM_B1 ** ADAM_STEP)
    v_hat = v / (1.0 - ADAM_B2 ** ADAM_STEP)
    delta = -ADAM_LR * (m_hat / (jnp.sqrt(v_hat) + ADAM_EPS) + ADAM_WD * w)
    return delta, m, v


def _add_mine(name, full, recv, me, out_dtype):
    N, _, R, C = full.shape
    tr = _rtile(R, max(16, (1 << 19) // max(C, 1) // 16 * 16))

    def body(me_ref, a_ref, b_ref, o_ref):
        o_ref[...] = (a_ref[...].astype(f32) + b_ref[...].astype(f32)).astype(o_ref.dtype)

    blk = pl.BlockSpec((None, tr, C), lambda n, i, me_ref: (n, i, 0))
    return pl.pallas_call(
        body,
        grid_spec=pltpu.PrefetchScalarGridSpec(
            num_scalar_prefetch=1, grid=(N, R // tr),
            in_specs=[pl.BlockSpec((None, None, tr, C), lambda n, i, me_ref: (n, me_ref[0], i, 0)), blk], out_specs=blk),
        out_shape=SDS((N, R, C), out_dtype), compiler_params=_cp(("parallel", "parallel")), name=name,
    )(me, full, recv)


def _adamw_big(name, full, recv, me, w, m, v, l):
    _, R, C = full.shape
    tr = _rtile(R, max(16, (1 << 18) // max(C, 1) // 16 * 16))

    def body(me_ref, ga_ref, gb_ref, w_ref, m_ref, v_ref, g_ref, d_ref, nm_ref, nv_ref):
        g = ga_ref[...].astype(f32) + gb_ref[...].astype(f32)
        d, nm, nv = _adam_math(w_ref[...], g, m_ref[...], v_ref[...])
        g_ref[...] = g
        d_ref[...] = d
        nm_ref[...] = nm
        nv_ref[...] = nv

    blk = pl.BlockSpec((tr, C), lambda i, me_ref: (i, 0))
    lblk = pl.BlockSpec((None, tr, C), lambda i, me_ref: (l, i, 0))
    return pl.pallas_call(
        body,
        grid_spec=pltpu.PrefetchScalarGridSpec(
            num_scalar_prefetch=1, grid=(R // tr,),
            in_specs=[pl.BlockSpec((None, tr, C), lambda i, me_ref: (me_ref[0], i, 0)), blk, lblk, lblk, lblk],
            out_specs=[blk] * 4),
        out_shape=[SDS((R, C), f32)] * 4, compiler_params=_cp(("parallel",)), name=name,
    )(me, full, recv, w, m, v)


def _adamw_small(name, gall, w, m, v):
    _, R, C = gall.shape
    tr = _rtile(R, 512)

    def body(ga_ref, w_ref, m_ref, v_ref, g_ref, d_ref, nm_ref, nv_ref):
        g = ga_ref[0]
        for s in range(1, 8):
            g = g + ga_ref[s]
        d, nm, nv = _adam_math(w_ref[...], g, m_ref[...], v_ref[...])
        g_ref[...] = g
        d_ref[...] = d
        nm_ref[...] = nm
        nv_ref[...] = nv

    blk = pl.BlockSpec((tr, C), lambda i: (i, 0))
    return pl.pallas_call(
        body, grid=(R // tr,), in_specs=[pl.BlockSpec((8, tr, C), lambda i: (0, i, 0)), blk, blk, blk], out_specs=[blk] * 4,
        out_shape=[SDS((R, C), f32)] * 4, compiler_params=_cp(("parallel",)), name=name,
    )(gall, w, m, v)


def _peer(axis):
    x, y, c = lax.axis_index("x"), lax.axis_index("y"), lax.axis_index("c")
    me = {"x": x, "y": y, "c": c}[axis]
    peer = {"x": (1 - x, y, c), "y": (x, 1 - y, c), "c": (x, y, 1 - c)}[axis]
    return me, peer


def _held(ref, done):
    idx = tuple(slice(None) if a in done else lax.axis_index(a) for a in ("x", "y", "c"))
    return ref.at[idx]


def _gather_stage(name, bufs, axes, dones):
    n = len(bufs)
    hbm = pl.BlockSpec(memory_space=pltpu.HBM)

    def body(*refs):
        outs = refs[n:2 * n]
        send_sems, recv_sems = refs[2 * n:]
        cps = []
        for t in range(n):
            _, peer = _peer(axes[t])
            blk = _held(outs[t], dones[t])
            cps.append(pltpu.make_async_remote_copy(src_ref=blk, dst_ref=blk, send_sem=send_sems.at[t],
                                                    recv_sem=recv_sems.at[t], device_id=peer, device_id_type=_MESH_T))
        for cp in cps:
            cp.start()
        for cp in cps:
            cp.wait()

    return pl.pallas_call(
        body, in_specs=[hbm] * n, out_specs=[hbm] * n, out_shape=[SDS(b.shape, b.dtype) for b in bufs],
        input_output_aliases={t: t for t in range(n)},
        scratch_shapes=[pltpu.SemaphoreType.DMA((n,)), pltpu.SemaphoreType.DMA((n,))], name=name,
    )(*bufs)


def _scatter_stage(name, tensors, axes):
    n = len(tensors)
    hbm = pl.BlockSpec(memory_space=pltpu.HBM)

    def body(*refs):
        ins, recvs = refs[:n], refs[n:2 * n]
        send_sems, recv_sems = refs[2 * n:]
        cps = []
        for t in range(n):
            me, peer = _peer(axes[t])
            cps.append(pltpu.make_async_remote_copy(
                src_ref=ins[t].at[:, 1 - me], dst_ref=recvs[t], send_sem=send_sems.at[t], recv_sem=recv_sems.at[t],
                device_id=peer, device_id_type=_MESH_T))
        for cp in cps:
            cp.start()
        for cp in cps:
            cp.wait()

    return pl.pallas_call(
        body, in_specs=[hbm] * n, out_specs=[hbm] * n,
        out_shape=[SDS((t.shape[0],) + tuple(t.shape[2:]), t.dtype) for t in tensors],
        scratch_shapes=[pltpu.SemaphoreType.DMA((n,)), pltpu.SemaphoreType.DMA((n,))], name=name,
    )(*tensors)


_HBM = pl.BlockSpec(memory_space=pltpu.HBM)
_SEM = pl.BlockSpec(memory_space=pltpu.SEMAPHORE)
_EFFECT = pltpu.SideEffectType.DATAFLOW_SIDE_EFFECTING


def _split_start(name, arrays, n_copies, make_copies):
    na = len(arrays)

    def body(*refs):
        ins = refs[:na]
        send_sems, recv_sems = refs[na], refs[na + 1]
        token = refs[2 * na + 2]
        for cp in make_copies(ins, send_sems, recv_sems):
            cp.start()
        token[...] = jnp.zeros_like(token)

    res = pl.pallas_call(
        body, name=name,
        out_shape=(pltpu.SemaphoreType.DMA((n_copies,)), pltpu.SemaphoreType.DMA((n_copies,)),
                   *[pltpu.HBM(a.shape, a.dtype) for a in arrays], SDS((8, 128), f32)),
        in_specs=[_HBM] * na, out_specs=(_SEM, _SEM, *[_HBM] * na, pl.BlockSpec(memory_space=pltpu.VMEM)),
        input_output_aliases={i: 2 + i for i in range(na)},
        compiler_params=pltpu.CompilerParams(has_side_effects=_EFFECT),
    )(*[pltpu.with_memory_space_constraint(a, pltpu.HBM) for a in arrays])
    return res[0], res[1], list(res[2:2 + na]), res[2 + na]


def _split_wait(name, arrays, send_sems, recv_sems, after, make_copies):
    na = len(arrays)

    def body(*refs):
        ins = refs[:na]
        for cp in make_copies(ins, refs[na], refs[na + 1]):
            cp.wait_send()
            cp.wait_recv()

    res = pl.pallas_call(
        body, name=name, out_shape=tuple(pltpu.HBM(a.shape, a.dtype) for a in arrays),
        in_specs=[_HBM] * na + [_SEM, _SEM, pl.BlockSpec(memory_space=pl.ANY)], out_specs=tuple([_HBM] * na),
        input_output_aliases={i: i for i in range(na)},
        compiler_params=pltpu.CompilerParams(has_side_effects=_EFFECT),
    )(*arrays, send_sems, recv_sems, after)
    return list(res)


def _gather_copies(axes, dones):
    def make(refs, send_sems, recv_sems):
        cps = []
        for t in range(len(axes)):
            _, peer = _peer(axes[t])
            blk = _held(refs[t], dones[t])
            cps.append(pltpu.make_async_remote_copy(src_ref=blk, dst_ref=blk, send_sem=send_sems.at[t],
                                                    recv_sem=recv_sems.at[t], device_id=peer, device_id_type=_MESH_T))
        return cps
    return make


def _scatter_copies(axes):
    n = len(axes)

    def make(refs, send_sems, recv_sems):
        cps = []
        for t in range(n):
            me, peer = _peer(axes[t])
            cps.append(pltpu.make_async_remote_copy(
                src_ref=refs[t].at[:, 1 - me], dst_ref=refs[n + t], send_sem=send_sems.at[t], recv_sem=recv_sems.at[t],
                device_id=peer, device_id_type=_MESH_T))
        return cps
    return make


class _AsyncGather:
    def __init__(self, pfx, tensors, paths):
        x, y, c = (lax.axis_index(a) for a in ("x", "y", "c"))
        self.pfx, self.shapes = pfx, [tuple(t.shape) for t in tensors]
        self.bufs = [lax.dynamic_update_slice(lax.empty((2, 2, 2) + tuple(t.shape), t.dtype), t[None, None, None],
                                              (x, y, c) + (0,) * t.ndim) for t in tensors]
        self.orders = [tuple(p) + ("c",) for p in paths]
        self.ph = 0

    def _make(self):
        return _gather_copies([o[self.ph] for o in self.orders], [o[:self.ph] for o in self.orders])

    def start(self):
        self.ss, self.rs, self.bufs, tok = _split_start(f"{self.pfx}_start{self.ph}", self.bufs, len(self.bufs), self._make())
        return tok

    def wait(self, after):
        self.bufs = _split_wait(f"{self.pfx}_wait{self.ph}", self.bufs, self.ss, self.rs, after, self._make())
        self.ph += 1

    def result(self):
        return [b.reshape((8,) + s) for b, s in zip(self.bufs, self.shapes)]


class _AsyncReduceScatter:
    def __init__(self, pfx, tensors, paths):
        self.pfx = pfx
        self.rcs = [tuple(t.shape[1:]) for t in tensors]
        self.orders = [("c",) + tuple(p) for p in paths]
        self.left = [["x", "y", "c"] for _ in tensors]
        self.cur = list(tensors)
        self.ph = 0

    def start(self):
        n = len(self.cur)
        views = []
        for i, (t, rc) in enumerate(zip(self.cur, self.rcs)):
            pos = self.left[i].index(self.orders[i][self.ph])
            nb, na = 2 ** pos, 2 ** (len(self.left[i]) - pos - 1)
            views.append(t.reshape((nb, 2, na * rc[0], rc[1])))
        lands = [lax.empty((v.shape[0],) + tuple(v.shape[2:]), v.dtype) for v in views]
        self.make = _scatter_copies([o[self.ph] for o in self.orders])
        self.ss, self.rs, arrs, tok = _split_start(f"{self.pfx}_start{self.ph}", views + lands, n, self.make)
        self.arrs = arrs
        return tok

    def wait(self, after):
        n = len(self.cur)
        arrs = _split_wait(f"{self.pfx}_wait{self.ph}", self.arrs, self.ss, self.rs, after, self.make)
        views, recvs = arrs[:n], arrs[n:]
        ph = self.ph
        if ph == 2:
            self.out = [(v[0], r[0], o[2]) for v, r, o in zip(views, recvs, self.orders)]
        else:
            self.cur = [_add_mine(f"{self.pfx}_add{ph}_{i}", v, r, _coord(o[ph]), v.dtype)
                        for i, (v, r, o) in enumerate(zip(views, recvs, self.orders))]
            for i, o in enumerate(self.orders):
                self.left[i].remove(o[ph])
        self.ph += 1


def _coord(axis):
    return lax.axis_index(axis).astype(jnp.int32).reshape(1)


def _all_gather(pfx, tensors, paths):
    x, y, c = (lax.axis_index(a) for a in ("x", "y", "c"))
    bufs = []
    for t in tensors:
        zero = (0,) * t.ndim
        bufs.append(lax.dynamic_update_slice(lax.empty((2, 2, 2) + tuple(t.shape), t.dtype), t[None, None, None],
                                             (x, y, c) + zero))
    orders = [tuple(p) + ("c",) for p in paths]
    for ph in range(3):
        bufs = _gather_stage(f"{pfx}_{ph}", bufs, [o[ph] for o in orders], [o[:ph] for o in orders])
    return [b.reshape((8,) + tuple(t.shape)) for b, t in zip(bufs, tensors)]


def _reduce_scatter(pfx, tensors, paths):
    rcs = [tuple(t.shape[1:]) for t in tensors]
    orders = [("c",) + tuple(p) for p in paths]
    left = [["x", "y", "c"] for _ in tensors]
    cur = list(tensors)
    for ph in range(3):
        views = []
        for i, (t, rc) in enumerate(zip(cur, rcs)):
            pos = left[i].index(orders[i][ph])
            nb, na = 2 ** pos, 2 ** (len(left[i]) - pos - 1)
            views.append(t.reshape((nb, 2, na * rc[0], rc[1])))
        recvs = _scatter_stage(f"{pfx}_{ph}", views, [o[ph] for o in orders])
        if ph == 2:
            return [(v[0], r[0], o[2]) for v, r, o in zip(views, recvs, orders)]
        cur = [_add_mine(f"{pfx}_add{ph}_{i}", v, r, _coord(o[ph]), v.dtype)
               for i, (v, r, o) in enumerate(zip(views, recvs, orders))]
        for i, o in enumerate(orders):
            left[i].remove(o[ph])


def _mm_nn(name, a, w, out_dtype, tn_pref=1024):
    T, K = a.shape
    N = w.shape[1]
    tm, tn, tk = _rtile(T, 512), _tile(N, tn_pref), _tile(K, 2048)
    return _mm(
        name, a, w, _NN, (T // tm, N // tn, K // tk),
        pl.BlockSpec((tm, tk), lambda i, j, k: (i, k)), pl.BlockSpec((tk, tn), lambda i, j, k: (k, j)), (tm, tn),
        [], [], [SDS((T, N), out_dtype)], [pl.BlockSpec((tm, tn), lambda i, j, k: (i, j))], _store_epi,
    )[0]


def _mm_tn(name, a, b, out_dtype):
    T, M = a.shape
    N = b.shape[1]
    tm, tn, tk = _tile(M, 512), _tile(N, 2048), _rtile(T, 512)
    return _mm(
        name, a, b, _TN, (M // tm, N // tn, T // tk),
        pl.BlockSpec((tk, tm), lambda i, j, k: (k, i)), pl.BlockSpec((tk, tn), lambda i, j, k: (k, j)), (tm, tn),
        [], [], [SDS((M, N), out_dtype)], [pl.BlockSpec((tm, tn), lambda i, j, k: (i, j))], _store_epi,
    )[0]


def _mm_tn_slots(name, a, b, out_dtype):
    T, M = a.shape
    NS = b.shape[1] // 8
    tm, tk = _tile(M, 512), _rtile(T, 512)
    return _mm(
        name, a, b, _TN, (M // tm, 8, T // tk),
        pl.BlockSpec((tk, tm), lambda i, j, k: (k, i)), pl.BlockSpec((tk, NS), lambda i, j, k: (k, j)), (tm, NS),
        [], [], [SDS((8, M, NS), out_dtype)], [pl.BlockSpec((None, tm, NS), lambda i, j, k: (j, i, 0))], _store_epi,
    )[0]


def _mm_nt_slots(name, a, w8, out_dtype):
    T = a.shape[0]
    _, M, NS = w8.shape
    tm, tn = _rtile(T, 512), _tile(M, 1024)
    return _mm(
        name, a, w8, _NT, (T // tm, M // tn, 8),
        pl.BlockSpec((tm, NS), lambda i, j, k: (i, k)), pl.BlockSpec((None, tn, NS), lambda i, j, k: (k, j, 0)), (tm, tn),
        [], [], [SDS((T, M), out_dtype)], [pl.BlockSpec((tm, tn), lambda i, j, k: (i, j))], _store_epi,
    )[0]


def _colsum_kernel(name, fn, ins, in_cols, outs_elem, n_sum, C):
    T = ins[0].shape[0]
    tm = _rtile(T, 256)
    ne = len(outs_elem)

    def body(*refs):
        i = pl.program_id(0)
        iv = [r[...] for r in refs[:len(ins)]]
        res = fn(*iv)
        for o, r in zip(refs[len(ins):len(ins) + ne], res[:ne]):
            o[...] = r.astype(o.dtype)
        sums = [jnp.sum(r, axis=0, keepdims=True) for r in res[ne:]]

        @pl.when(i == 0)
        def _():
            for o, s in zip(refs[len(ins) + ne:], sums):
                o[...] = s

        @pl.when(i > 0)
        def _():
            for o, s in zip(refs[len(ins) + ne:], sums):
                o[...] += s

    in_specs = []
    for arr, off in zip(ins, in_cols):
        if off is None:
            in_specs.append(pl.BlockSpec((1, C), lambda i: (0, 0)))
        else:
            in_specs.append(pl.BlockSpec((tm, C), lambda i, off=off: (i, off)))
    row = pl.BlockSpec((tm, C), lambda i: (i, 0))
    vec = pl.BlockSpec((1, C), lambda i: (0, 0))
    return pl.pallas_call(
        body, grid=(T // tm,), in_specs=in_specs, out_specs=[row] * ne + [vec] * n_sum,
        out_shape=[SDS((T, C), dt) for dt in outs_elem] + [SDS((1, C), f32)] * n_sum,
        compiler_params=_cp(("arbitrary",)), name=name,
    )(*ins)


def _merge(gs, gg, a_s, a_g):
    return jax.nn.sigmoid(gs) * a_s + jax.nn.sigmoid(gg) * a_g


def _glu(yg, lp):
    return yg * jax.nn.sigmoid(lp)


_BIG = ("ffn1_w_gu", "ffn1_w_down", "w_in", "conv_w", "glu_w", "w_br_ssm", "w_br_gdn", "w_out", "ffn2_w_gu", "ffn2_w_down")
_PATHS = ("yx", "yx", "xy", "xy", "yx", "yx", "yx", "yx", "xy", "xy")
_SMALL = ("ln1_g", "ln1_b", "ssm_a_re", "ssm_a_im", "ssm_log_dt", "ssm_b_re", "ssm_b_im", "ssm_c_re", "ssm_c_im", "ssm_d",
          "glu_b", "gdn_a_log", "gdn_dt_bias", "gdn_norm_w", "ln2_g", "ln2_b", "ln3_g", "ln3_b")
_ORDER = ("ffn1_w_gu", "ffn1_w_down", "ln1_g", "ln1_b", "w_in", "conv_w", "ssm_a_re", "ssm_a_im", "ssm_log_dt", "ssm_b_re",
          "ssm_b_im", "ssm_c_re", "ssm_c_im", "ssm_d", "glu_w", "glu_b", "gdn_a_log", "gdn_dt_bias", "gdn_norm_w", "w_br_ssm",
          "w_br_gdn", "w_out", "ln2_g", "ln2_b", "ffn2_w_gu", "ffn2_w_down", "ln3_g", "ln3_b")


def _step(x, tgt, W, M, V):
    T, D = x.shape[1], x.shape[2]
    L = W["ffn1_w_gu"].shape[0]
    G, P = W["ssm_a_re"].shape[1:]
    H = W["ssm_b_re"].shape[3]
    SW = G * H
    NH = W["gdn_a_log"].shape[1]
    HD = W["gdn_norm_w"].shape[1]
    GW = NH * HD
    KC = W["conv_w"].shape[1]
    DS = D // 8
    alpha = (2.0 * L) ** 0.25
    o_b = SW + 4 * GW
    o_gs = o_b + 2 * NH
    IN = o_gs + 2 * D
    NM = IN - 2 * NH
    m_qkv, m_z, m_gs, m_gg = SW, SW + 3 * GW, SW + 4 * GW, SW + 4 * GW + D
    J = G // 8

    x0 = x[0]
    tg = tgt[0]

    def vec(name, l):
        return W[name][l:l + 1]

    saves, weights = [], []
    xc, xcb = x0, x0.astype(_MXU)
    def shards(l):
        return [W["ffn1_w_gu"][l].astype(_MXU), W["ffn1_w_down"][l].astype(_MXU), W["w_in"][l].astype(_MXU), W["conv_w"][l],
                W["glu_w"][l].astype(_MXU), W["w_br_ssm"][l].astype(_MXU), W["w_br_gdn"][l].astype(_MXU),
                W["w_out"][l].astype(_MXU), W["ffn2_w_gu"][l].astype(_MXU), W["ffn2_w_down"][l].astype(_MXU)]

    def dep(a, tok):
        return a if tok is None else a + tok[0:1, 0:1].astype(a.dtype)

    gathered = _all_gather("ag", shards(0), _PATHS)
    for l in range(L):
        nxt = _AsyncGather("agp", shards(l + 1), _PATHS) if l + 1 < L else None
        tok = nxt.start() if nxt else None
        wgu1, wd1, win8, cw8, wglu, wbs, wbg, wo, wgu2, wd2 = gathered
        wd1 = wd1.reshape(-1, D)
        wd2 = wd2.reshape(-1, D)
        wglu = wglu.reshape(SW, SW)
        wo = wo.reshape(D, D)
        win = jnp.transpose(win8, (1, 0, 2)).reshape(D, IN)
        wmain = jnp.concatenate([win[:, :o_b], win[:, o_gs:]], axis=1)
        wba = jnp.pad(win[:, o_b:o_gs], ((0, 0), (0, 128 - 2 * NH)))
        cw = jnp.transpose(cw8, (1, 0, 2)).reshape(KC, 3 * GW)
        wl = dict(wgu1=wgu1, wd1=wd1, wmain=wmain, wba=wba, cw=cw, wglu=wglu, wbs=wbs, wbg=wbg, wo=wo, wgu2=wgu2, wd2=wd2)
        weights.append(wl)
        sv = {}

        gate, up, hh = _ffn_up("ffn_up", xcb, wgu1)
        x1, x1b, xh1, r1 = _mm_ln("ffn_down_ln", hh, wd1, xc, dep(vec("ln1_g", l), tok), vec("ln1_b", l), alpha, 0.5)
        sv["f1"] = dict(xb=xcb, gate=gate, up=up, h=hh, xhat=xh1, rstd=r1)

        p = _mm_nn("mix_in", x1b, wmain, f32)
        pba = _mm_nn("mix_in_ba", x1b, wba, f32)
        b_re_t = jnp.transpose(W["ssm_b_re"][l], (2, 0, 1))
        b_im_t = jnp.transpose(W["ssm_b_im"][l], (2, 0, 1))
        zoh_in = (W["ssm_a_re"][l], W["ssm_a_im"][l], W["ssm_log_dt"][l][:, None], b_re_t, b_im_t)
        lbr, lbi, bbr_t, bbi_t = _zoh_fwd("zoh", *zoh_in)
        bblk_r = _blockdiag(jnp.transpose(bbr_t, (1, 0, 2)))
        bblk_i = _blockdiag(jnp.transpose(bbi_t, (1, 0, 2)))
        cblkT_r = _blockdiag(W["ssm_c_re"][l])
        cblkT_in = _blockdiag(-W["ssm_c_im"][l])
        lbr_f, lbi_f = lbr.reshape(1, G * P), lbi.reshape(1, G * P)
        bur, bui = _bd2("s5_bu", p, 0, bblk_r, bblk_i)
        if nxt:
            nxt.wait(bur)
            tok = nxt.start()
        sr, si = _s5_scan("s5_scan", bur, bui, dep(lbr_f, tok), lbi_f)
        dflat = W["ssm_d"][l].reshape(1, SW)

        def out_epi(acc, ex, outs):
            y_raw = acc + ex[1][...] * ex[0][...]
            yg = jax.nn.gelu(y_raw)
            outs[0][...] = y_raw
            outs[1][...] = yg
            outs[2][...] = yg.astype(outs[2].dtype)

        y_raw, yg, ygb = _bd_sum(
            "s5_out", sr, si, jnp.transpose(cblkT_r, (0, 2, 1)), jnp.transpose(cblkT_in, (0, 2, 1)), [p, dflat],
            lambda tm, nb: [pl.BlockSpec((tm, nb), lambda i, j: (i, j)), pl.BlockSpec((1, nb), lambda i, j: (0, j))],
            [SDS((T, SW), f32), SDS((T, SW), f32), SDS((T, SW), _MXU)], out_epi)

        tmg, tng, tkg = _rtile(T, 512), _tile(SW, 512), _tile(SW, 512)

        def glu_epi(acc, ex, outs):
            lp = acc + ex[1][...]
            outs[0][...] = lp
            outs[1][...] = _glu(ex[0][...], lp).astype(outs[1].dtype)

        lp, ysb = _mm(
            "s5_glu", ygb, wglu, _NN, (T // tmg, SW // tng, SW // tkg),
            pl.BlockSpec((tmg, tkg), lambda i, j, k: (i, k)), pl.BlockSpec((tkg, tng), lambda i, j, k: (k, j)), (tmg, tng),
            [yg, vec("glu_b", l)], [pl.BlockSpec((tmg, tng), lambda i, j, k: (i, j)), pl.BlockSpec((1, tng), lambda i, j, k: (0, j))],
            [SDS((T, SW), f32), SDS((T, SW), _MXU)], [pl.BlockSpec((tmg, tng), lambda i, j, k: (i, j))] * 2, glu_epi)

        qkv = _conv_fwd("gdn_conv", p, m_qkv, cw, 3 * GW)
        blt = jnp.transpose(pba[:, :NH])[:, :, None]
        aint = jnp.transpose(pba[:, NH:2 * NH])[:, :, None]
        alog = W["gdn_a_log"][l].reshape(NH, 1, 1)
        dtb = W["gdn_dt_bias"][l].reshape(NH, 1, 1)
        nw = vec("gdn_norm_w", l)
        og, ssave = _gdn_fwd("gdn", qkv, p, m_z, blt, aint, alog, dtb, nw, NH, HD)

        a_s = _mm(
            "br_ssm", ysb, wbs, _NN, (T // tmg, 8, SW // tkg),
            pl.BlockSpec((tmg, tkg), lambda i, j, k: (i, k)), pl.BlockSpec((None, tkg, DS), lambda i, j, k: (j, k, 0)), (tmg, DS),
            [], [], [SDS((T, D), f32)], [pl.BlockSpec((tmg, DS), lambda i, j, k: (i, j))], _store_epi)[0]
        tkd = _tile(GW, 512)
        gsb, ggb = m_gs // DS, m_gg // DS

        def merge_epi(acc, ex, outs):
            outs[0][...] = acc
            outs[1][...] = _merge(ex[1][...], ex[2][...], ex[0][...], acc).astype(outs[1].dtype)

        tile_ij = pl.BlockSpec((tmg, DS), lambda i, j, k: (i, j))
        a_g, merged = _mm(
            "br_gdn_merge", og, wbg, _NN, (T // tmg, 8, GW // tkd),
            pl.BlockSpec((tmg, tkd), lambda i, j, k: (i, k)), pl.BlockSpec((None, tkd, DS), lambda i, j, k: (j, k, 0)), (tmg, DS),
            [a_s, p, p], [tile_ij, pl.BlockSpec((tmg, DS), lambda i, j, k: (i, j + gsb)),
                          pl.BlockSpec((tmg, DS), lambda i, j, k: (i, j + ggb))],
            [SDS((T, D), f32), SDS((T, D), _MXU)], [tile_ij, tile_ij], merge_epi)
        if nxt:
            nxt.wait(merged)
            tok = nxt.start()
        x2, x2b, xh2, r2 = _mm_ln("mix_out_ln", merged, wo, x1, dep(vec("ln2_g", l), tok), vec("ln2_b", l), alpha, 1.0)
        sv["mx"] = dict(x1b=x1b, p=p, zoh_in=zoh_in, lbr_f=lbr_f, lbi_f=lbi_f, bblk_r=bblk_r, bblk_i=bblk_i, cblkT_r=cblkT_r,
                        cblkT_in=cblkT_in, sr=sr, si=si, dflat=dflat, y_raw=y_raw, yg=yg, ygb=ygb, lp=lp, ysb=ysb, qkv=qkv,
                        blt=blt, aint=aint, alog=alog, dtb=dtb, nw=nw, og=og, ssave=ssave, a_s=a_s, a_g=a_g, merged=merged,
                        xhat=xh2, rstd=r2)

        gate2, up2, hh2 = _ffn_up("ffn_up", x2b, wgu2)
        x3, x3b, xh3, r3 = _mm_ln("ffn_down_ln", hh2, wd2, x2, vec("ln3_g", l), vec("ln3_b", l), alpha, 0.5)
        sv["f2"] = dict(xb=x2b, gate=gate2, up=up2, h=hh2, xhat=xh3, rstd=r3)
        saves.append(sv)
        xc, xcb = x3, x3b
        if nxt:
            nxt.wait(x3)
            gathered = nxt.result()

    dy, loss_part = _loss_head("loss_head", xc, tg)
    loss = lax.psum(loss_part[0, 0], ("x", "y", "c"))

    big_out = {n: [None] * L for n in _BIG}
    small_g = {n: [None] * L for n in _SMALL}
    pend = None
    for l in reversed(range(L)):
        sv, wl = saves[l], weights[l]
        mx = sv["mx"]
        p = mx["p"]
        tok = pend.start() if pend else None
        dx2, dwgu2, dwd2, dg3, db3 = _ffn_bwd("ffn_b", dy, sv["f2"], wl["wgu2"], wl["wd2"], dep(vec("ln3_g", l), tok), alpha)
        small_g["ln3_g"][l], small_g["ln3_b"][l] = dg3[0], db3[0]

        if pend:
            pend.wait(dx2)
            tok = pend.start()
        dz2, dmixb, dg2, db2 = _ln_bwd("mix_lnb", dx2, mx["xhat"], mx["rstd"], dep(vec("ln2_g", l), tok), 1.0)
        small_g["ln2_g"][l], small_g["ln2_b"][l] = dg2[0], db2[0]
        tmg, tkd = _rtile(T, 512), _tile(D, 512)
        gsb, ggb = m_gs // DS, m_gg // DS

        def dmerge_epi(acc, ex, outs):
            _, vjp = jax.vjp(_merge, ex[0][...], ex[1][...], ex[2][...], ex[3][...])
            dgs, dgg, das, dag = vjp(acc)
            outs[0][...] = das.astype(outs[0].dtype)
            outs[1][...] = dag.astype(outs[1].dtype)
            outs[2][...] = dgs.astype(outs[2].dtype)
            outs[3][...] = dgg.astype(outs[3].dtype)

        tile_ij = pl.BlockSpec((tmg, DS), lambda i, j, k: (i, j))
        das, dag, dgs, dgg = _mm(
            "mix_dmerge", dmixb, wl["wo"], _NT, (T // tmg, 8, D // tkd),
            pl.BlockSpec((tmg, tkd), lambda i, j, k: (i, k)), pl.BlockSpec((DS, tkd), lambda i, j, k: (j, k)), (tmg, DS),
            [p, p, mx["a_s"], mx["a_g"]],
            [pl.BlockSpec((tmg, DS), lambda i, j, k: (i, j + gsb)), pl.BlockSpec((tmg, DS), lambda i, j, k: (i, j + ggb)),
             tile_ij, tile_ij],
            [SDS((T, D), _MXU)] * 4, [tile_ij] * 4, dmerge_epi)
        dwo = _mm_tn("mix_dwo", mx["merged"], dmixb, _GDT)
        dys = _mm_nt_slots("br_ssm_dx", das, wl["wbs"], f32)
        dog = _mm_nt_slots("br_gdn_dx", dag, wl["wbg"], f32)
        dwbs = _mm_tn_slots("br_ssm_dw", mx["ysb"], das, _GDT)
        dwbg = _mm_tn_slots("br_gdn_dw", mx["og"], dag, _GDT)

        def glu_b_fn(dys_t, yg_t, lp_t):
            _, vjp = jax.vjp(_glu, yg_t, lp_t)
            dyg1, dlp = vjp(dys_t)
            return dyg1, dlp, dlp

        dyg1, dlpb, dglub = _colsum_kernel("s5_glu_b", glu_b_fn, [dys, mx["yg"], mx["lp"]], [0, 0, 0], [f32, _MXU], 1, SW)
        small_g["glu_b"][l] = dglub[0]
        dwglu = _mm_tn("s5_dwglu", mx["ygb"], dlpb, _GDT)
        tng, tkg = _tile(SW, 512), _tile(SW, 512)

        def dyraw_epi(acc, ex, outs):
            _, vjp = jax.vjp(jax.nn.gelu, ex[1][...])
            (d,) = vjp(ex[0][...] + acc)
            outs[0][...] = d

        t_ij = pl.BlockSpec((tmg, tng), lambda i, j, k: (i, j))
        (dyraw,) = _mm(
            "s5_dyraw", dlpb, wl["wglu"], _NT, (T // tmg, SW // tng, SW // tkg),
            pl.BlockSpec((tmg, tkg), lambda i, j, k: (i, k)), pl.BlockSpec((tng, tkg), lambda i, j, k: (j, k)), (tmg, tng),
            [dyg1, mx["y_raw"]], [t_ij, t_ij], [SDS((T, SW), f32)], [t_ij], dyraw_epi)

        def dd_fn(dyr, u_t, d_t):
            return d_t * dyr, dyr * u_t

        dud, dd = _colsum_kernel("s5_dd", dd_fn, [dyraw, p, mx["dflat"]], [0, 0, None], [f32], 1, SW)
        small_g["ssm_d"][l] = dd.reshape(G, H)
        dsr, dsi = _bd2("s5_ds", dyraw, 0, mx["cblkT_r"], mx["cblkT_in"])
        dcb_r, dcb_i = _bdT2("s5_dc", mx["sr"], mx["si"], 0, dyraw, dyraw, 0, 8 * P, 8 * H, J)
        small_g["ssm_c_re"][l] = _blockdiag_extract(jnp.transpose(dcb_r, (0, 2, 1)), H, P)
        small_g["ssm_c_im"][l] = -_blockdiag_extract(jnp.transpose(dcb_i, (0, 2, 1)), H, P)
        ar, ai, dlr, dli = _s5_scan_bwd("s5_scan_b", dsr, dsi, mx["sr"], mx["si"], mx["lbr_f"], mx["lbi_f"])

        def du_epi(acc, ex, outs):
            outs[0][...] = (acc + ex[0][...]).astype(outs[0].dtype)

        (du,) = _bd_sum(
            "s5_du", ar, ai, jnp.transpose(mx["bblk_r"], (0, 2, 1)), jnp.transpose(mx["bblk_i"], (0, 2, 1)), [dud],
            lambda tm, nb: [pl.BlockSpec((tm, nb), lambda i, j: (i, j))], [SDS((T, SW), _MXU)], du_epi)
        dbb_r, dbb_i = _bdT2("s5_db", p, p, 0, ar, ai, 0, 8 * H, 8 * P, J)
        dbbr_t = jnp.transpose(_blockdiag_extract(dbb_r, H, P), (1, 0, 2))
        dbbi_t = jnp.transpose(_blockdiag_extract(dbb_i, H, P), (1, 0, 2))
        da_re, da_im, dlog_dt, dbre_t, dbim_t = _zoh_bwd("zoh_b", *mx["zoh_in"], dlr.reshape(G, P), dli.reshape(G, P),
                                                         dbbr_t, dbbi_t)
        small_g["ssm_a_re"][l], small_g["ssm_a_im"][l], small_g["ssm_log_dt"][l] = da_re, da_im, dlog_dt[:, 0]
        small_g["ssm_b_re"][l] = jnp.transpose(dbre_t, (1, 2, 0))
        small_g["ssm_b_im"][l] = jnp.transpose(dbim_t, (1, 2, 0))

        if pend:
            pend.wait(du)
            tok = pend.start()
        dqkv3, dzb, dbl, dain, dal, ddtb, dnw = _gdn_bwd("gdn_b", mx["qkv"], p, m_z, mx["blt"], mx["aint"], mx["alog"],
                                                         mx["dtb"], dep(mx["nw"], tok), mx["ssave"], dog, NH, HD)
        small_g["gdn_a_log"][l], small_g["gdn_dt_bias"][l], small_g["gdn_norm_w"][l] = dal[:, 0, 0], ddtb[:, 0, 0], dnw[0]
        dqkv_pre, dcw = _conv_bwd("gdn_conv_b", p, m_qkv, wl["cw"], dqkv3)

        dpm = jnp.concatenate([du, dqkv_pre, dzb, dgs, dgg], axis=1)
        dpba = jnp.concatenate([jnp.transpose(dbl[:, :, 0]), jnp.transpose(dain[:, :, 0]),
                                jnp.zeros((T, 128 - 2 * NH), f32)], axis=1).astype(_MXU)
        tnd, tkm = _tile(D, 1024), _tile(NM, 512)
        t_ba = _mm(
            "mix_dx_ba", dpba, wl["wba"], _NT, (T // tmg, D // tnd, 1),
            pl.BlockSpec((tmg, 128), lambda i, j, k: (i, 0)), pl.BlockSpec((tnd, 128), lambda i, j, k: (j, 0)), (tmg, tnd),
            [], [], [SDS((T, D), f32)], [pl.BlockSpec((tmg, tnd), lambda i, j, k: (i, j))], _store_epi)[0]

        def dx1_epi(acc, ex, outs):
            outs[0][...] = alpha * ex[0][...] + ex[1][...] + acc

        t_d = pl.BlockSpec((tmg, tnd), lambda i, j, k: (i, j))
        (dx1,) = _mm(
            "mix_dx", dpm, wl["wmain"], _NT, (T // tmg, D // tnd, NM // tkm),
            pl.BlockSpec((tmg, tkm), lambda i, j, k: (i, k)), pl.BlockSpec((tnd, tkm), lambda i, j, k: (j, k)), (tmg, tnd),
            [dz2, t_ba], [t_d, t_d], [SDS((T, D), f32)], [t_d], dx1_epi)
        tnm = _tile(NM, 1024)
        tkt = _tile(T, 1024)
        dwmain = _mm(
            "mix_dw", jnp.transpose(mx["x1b"]), dpm, _NN, (D // tkd, NM // tnm, T // tkt),
            pl.BlockSpec((tkd, tkt), lambda i, j, k: (i, k)), pl.BlockSpec((tkt, tnm), lambda i, j, k: (k, j)), (tkd, tnm),
            [], [], [SDS((D, NM), _GDT)], [pl.BlockSpec((tkd, tnm), lambda i, j, k: (i, j))], _store_epi)[0]
        dwba = _mm_tn("mix_dw_ba", mx["x1b"], dpba, _GDT)
        dwin = jnp.concatenate([dwmain[:, :o_b], dwba[:, :2 * NH], dwmain[:, o_b:]], axis=1)
        dwin8 = jnp.transpose(dwin.reshape(D, 8, IN // 8), (1, 0, 2))
        dcw8 = jnp.transpose(dcw.reshape(KC, 8, 3 * GW // 8), (1, 0, 2))

        dx0, dwgu1, dwd1, dg1, db1 = _ffn_bwd("ffn_b", dx1, sv["f1"], wl["wgu1"], wl["wd1"], vec("ln1_g", l), alpha)
        small_g["ln1_g"][l], small_g["ln1_b"][l] = dg1[0], db1[0]
        dy = dx0

        parts = [dwgu1, dwd1.reshape(8, -1, D), dwin8, dcw8, dwglu.reshape(8, SW // 8, SW), dwbs, dwbg,
                 dwo.reshape(8, DS, D), dwgu2, dwd2.reshape(8, -1, D)]
        if pend:
            pend.wait(dx0)
            for n, (full, recv, last) in zip(_BIG, pend.out):
                big_out[n][l + 1] = _adamw_big("adamw_" + n, full, recv, _coord(last), W[n], M[n], V[n], l + 1)
        if l > 0:
            pend = _AsyncReduceScatter("rsp", parts, _PATHS)
        else:
            for n, (full, recv, last) in zip(_BIG, _reduce_scatter("rs", parts, _PATHS)):
                big_out[n][l] = _adamw_big("adamw_" + n, full, recv, _coord(last), W[n], M[n], V[n], l)

    def pack(arrs):
        flat = jnp.concatenate([a.reshape(-1) for a in arrs])
        n = flat.shape[0]
        rows = -(-n // (128 * 16)) * 16
        return jnp.pad(flat, (0, rows * 128 - n)).reshape(rows, 128)

    gs_full = [jnp.stack(small_g[n]).reshape(W[n].shape) for n in _SMALL]
    gpack = pack(gs_full)
    (gall,) = _all_gather("ag_small", [gpack], ["yx"])
    sg, sd, sm, sv_ = _adamw_small("adamw_small", gall, pack([W[n] for n in _SMALL]), pack([M[n] for n in _SMALL]),
                                   pack([V[n] for n in _SMALL]))

    def unpack(packed):
        flat = packed.reshape(-1)
        out, off = {}, 0
        for n in _SMALL:
            sz = math.prod(W[n].shape)
            out[n] = flat[off:off + sz].reshape(W[n].shape)
            off += sz
        return out

    res = [unpack(a) for a in (sg, sd, sm, sv_)]
    for n in _BIG:
        for i in range(4):
            res[i][n] = jnp.stack([big_out[n][l][i] for l in range(L)])
    outs = [loss, dy[None]]
    for i in range(4):
        outs += [res[i][n] for n in _ORDER]
    return tuple(outs)


def kernel(x, ffn1_w_gu, ffn1_w_down, ln1_g, ln1_b, w_in, conv_w, ssm_a_re, ssm_a_im, ssm_log_dt, ssm_b_re, ssm_b_im, ssm_c_re, ssm_c_im, ssm_d, glu_w, glu_b, gdn_a_log, gdn_dt_bias, gdn_norm_w, w_br_ssm, w_br_gdn, w_out, ln2_g, ln2_b, ffn2_w_gu, ffn2_w_down, ln3_g, ln3_b, loss_target, m_ffn1_w_gu, m_ffn1_w_down, m_ln1_g, m_ln1_b, m_w_in, m_conv_w, m_ssm_a_re, m_ssm_a_im, m_ssm_log_dt, m_ssm_b_re, m_ssm_b_im, m_ssm_c_re, m_ssm_c_im, m_ssm_d, m_glu_w, m_glu_b, m_gdn_a_log, m_gdn_dt_bias, m_gdn_norm_w, m_w_br_ssm, m_w_br_gdn, m_w_out, m_ln2_g, m_ln2_b, m_ffn2_w_gu, m_ffn2_w_down, m_ln3_g, m_ln3_b, v_ffn1_w_gu, v_ffn1_w_down, v_ln1_g, v_ln1_b, v_w_in, v_conv_w, v_ssm_a_re, v_ssm_a_im, v_ssm_log_dt, v_ssm_b_re, v_ssm_b_im, v_ssm_c_re, v_ssm_c_im, v_ssm_d, v_glu_w, v_glu_b, v_gdn_a_log, v_gdn_dt_bias, v_gdn_norm_w, v_w_br_ssm, v_w_br_gdn, v_w_out, v_ln2_g, v_ln2_b, v_ffn2_w_gu, v_ffn2_w_down, v_ln3_g, v_ln3_b):
    given = dict(locals())
    W = {n: given[n] for n in _ORDER}
    M = {n: given["m_" + n] for n in _ORDER}
    V = {n: given["v_" + n] for n in _ORDER}
    return _step(x, loss_target, W, M, V)
```

```python
import functools
import math

import jax
import jax.numpy as jnp
from jax import lax
from jax.experimental import pallas as pl
from jax.experimental.pallas import tpu as pltpu

f32 = jnp.float32
_MXU = jnp.bfloat16
_GDT = jnp.bfloat16
_HP = lax.Precision.HIGHEST
_VMEM_LIMIT = 56 * 1024 * 1024
_MESH_T = pl.DeviceIdType.MESH

LN_EPS = 1e-5
RMS_EPS = 1e-6
L2_EPS = 1e-6
CHUNK = 64
ADAM_LR = 0.001
ADAM_B1 = 0.9
ADAM_B2 = 0.999
ADAM_EPS = 1e-08
ADAM_WD = 0.01
ADAM_STEP = 10

_NN = (((1,), (0,)), ((), ()))
_NT = (((1,), (1,)), ((), ()))
_TN = (((0,), (0,)), ((), ()))

SDS = jax.ShapeDtypeStruct


def _cp(sem):
    return pltpu.CompilerParams(dimension_semantics=sem, vmem_limit_bytes=_VMEM_LIMIT)


def _tile(n, pref):
    if n <= pref:
        return n
    t = (pref // 128) * 128
    while t >= 128:
        if n % t == 0:
            return t
        t -= 128
    return n


def _rtile(n, pref):
    if n <= pref:
        return n
    t = (pref // 16) * 16
    while t >= 16:
        if n % t == 0:
            return t
        t -= 16
    return n


def _mm(name, a, b, dims, grid, a_spec, b_spec, acc_shape, extras, extra_specs, out_shape, out_specs, epilogue):
    nk = grid[2]
    ne = len(extras)
    no = len(out_shape)

    def body(*refs):
        a_ref, b_ref = refs[0], refs[1]
        ex = refs[2:2 + ne]
        outs = refs[2 + ne:2 + ne + no]
        acc = refs[-1]
        k = pl.program_id(2)
        part = lax.dot_general(a_ref[...].astype(_MXU), b_ref[...].astype(_MXU), dims, preferred_element_type=f32)

        @pl.when(k == 0)
        def _():
            acc[...] = part

        @pl.when(k > 0)
        def _():
            acc[...] += part

        @pl.when(k == nk - 1)
        def _():
            epilogue(acc[...], ex, outs)

    return pl.pallas_call(
        body, grid=grid, in_specs=[a_spec, b_spec, *extra_specs], out_specs=list(out_specs), out_shape=list(out_shape),
        scratch_shapes=[pltpu.VMEM(acc_shape, f32)], compiler_params=_cp(("parallel", "parallel", "arbitrary")), name=name,
    )(a, b, *extras)


def _store_epi(acc, ex, outs):
    for o in outs:
        o[...] = acc.astype(o.dtype)


def _ln_epilogue(alpha, c):
    def epi(acc, ex, outs):
        x_ref, g_ref, b_ref = ex
        y_ref, yb_ref, xh_ref, r_ref = outs
        z = alpha * x_ref[...] + c * acc
        mu = jnp.mean(z, axis=-1, keepdims=True)
        zc = z - mu
        var = jnp.mean(zc * zc, axis=-1, keepdims=True)
        r = lax.rsqrt(var + LN_EPS)
        xh = zc * r
        y = xh * g_ref[...] + b_ref[...]
        y_ref[...] = y
        yb_ref[...] = y.astype(yb_ref.dtype)
        xh_ref[...] = xh
        r_ref[...] = r
    return epi


def _mm_ln(name, a, w, x, g, b, alpha, c):
    T, K = a.shape
    D = w.shape[1]
    tm, tk = _rtile(T, 512), _tile(K, 512)
    row = pl.BlockSpec((tm, D), lambda i, j, k: (i, 0))
    vec = pl.BlockSpec((1, D), lambda i, j, k: (0, 0))
    return _mm(
        name, a, w, _NN, (T // tm, 1, K // tk),
        pl.BlockSpec((tm, tk), lambda i, j, k: (i, k)), pl.BlockSpec((tk, D), lambda i, j, k: (k, 0)), (tm, D),
        [x, g, b], [row, vec, vec],
        [SDS((T, D), f32), SDS((T, D), _MXU), SDS((T, D), f32), SDS((T, 1), f32)],
        [row, row, row, pl.BlockSpec((tm, 1), lambda i, j, k: (i, 0))],
        _ln_epilogue(alpha, c),
    )


def _ln_bwd(name, dy, xhat, rstd, g, c):
    T, D = dy.shape
    tm = _rtile(T, 256)

    def body(dy_ref, xh_ref, r_ref, g_ref, dz_ref, df_ref, dg_ref, db_ref):
        i = pl.program_id(0)
        dyv = dy_ref[...]
        xh = xh_ref[...]
        dxh = dyv * g_ref[...]
        m1 = jnp.mean(dxh, axis=-1, keepdims=True)
        m2 = jnp.mean(dxh * xh, axis=-1, keepdims=True)
        dz = r_ref[...] * (dxh - m1 - xh * m2)
        dz_ref[...] = dz
        df_ref[...] = (c * dz).astype(df_ref.dtype)
        pg = jnp.sum(dyv * xh, axis=0, keepdims=True)
        pb = jnp.sum(dyv, axis=0, keepdims=True)

        @pl.when(i == 0)
        def _():
            dg_ref[...] = pg
            db_ref[...] = pb

        @pl.when(i > 0)
        def _():
            dg_ref[...] += pg
            db_ref[...] += pb

    row = pl.BlockSpec((tm, D), lambda i: (i, 0))
    vec = pl.BlockSpec((1, D), lambda i: (0, 0))
    return pl.pallas_call(
        body, grid=(T // tm,), in_specs=[row, row, pl.BlockSpec((tm, 1), lambda i: (i, 0)), vec],
        out_specs=[row, row, vec, vec],
        out_shape=[SDS((T, D), f32), SDS((T, D), _MXU), SDS((1, D), f32), SDS((1, D), f32)],
        compiler_params=_cp(("arbitrary",)), name=name,
    )(dy, xhat, rstd, g)


def _swiglu(g, u):
    return jax.nn.silu(g) * u


def _ffn_up(name, xb, wgu):
    T, D = xb.shape
    FS = wgu.shape[2]
    F = 4 * FS
    tm = _rtile(T, 256)

    def body(x_ref, wg_ref, wu_ref, g_ref, u_ref, h_ref):
        xv = x_ref[...]
        g = jnp.dot(xv, wg_ref[...], preferred_element_type=f32)
        u = jnp.dot(xv, wu_ref[...], preferred_element_type=f32)
        g_ref[...] = g
        u_ref[...] = u
        h_ref[...] = _swiglu(g, u).astype(h_ref.dtype)

    out = pl.BlockSpec((tm, FS), lambda j, i: (i, j))
    return pl.pallas_call(
        body, grid=(4, T // tm),
        in_specs=[pl.BlockSpec((tm, D), lambda j, i: (i, 0)),
                  pl.BlockSpec((None, D, FS), lambda j, i: (j, 0, 0)),
                  pl.BlockSpec((None, D, FS), lambda j, i: (j + 4, 0, 0))],
        out_specs=[out, out, out],
        out_shape=[SDS((T, F), f32), SDS((T, F), f32), SDS((T, F), _MXU)],
        compiler_params=_cp(("parallel", "arbitrary")), name=name,
    )(xb, wgu, wgu)


def _ffn_bwd(pfx, dy, sv, wgu, wd, g_ln, alpha):
    T, D = dy.shape
    FS = wgu.shape[2]
    F = 4 * FS
    dz, dfb, dg, db = _ln_bwd(pfx + "_lnb", dy, sv["xhat"], sv["rstd"], g_ln, 0.5)

    tm, tn, tk = _rtile(T, 512), _tile(F, 512), _tile(D, 2048)

    def epi(acc, ex, outs):
        g_ref, u_ref = ex
        _, vjp = jax.vjp(_swiglu, g_ref[...], u_ref[...])
        dgate, dup = vjp(acc)
        outs[0][0] = dgate.astype(outs[0].dtype)
        outs[0][1] = dup.astype(outs[0].dtype)

    gu = pl.BlockSpec((tm, tn), lambda i, j, k: (i, j))
    (dgu,) = _mm(
        pfx + "_dh", dfb, wd, _NT, (T // tm, F // tn, D // tk),
        pl.BlockSpec((tm, tk), lambda i, j, k: (i, k)), pl.BlockSpec((tn, tk), lambda i, j, k: (j, k)), (tm, tn),
        [sv["gate"], sv["up"]], [gu, gu],
        [SDS((2, T, F), _MXU)], [pl.BlockSpec((2, tm, tn), lambda i, j, k: (0, i, j))], epi,
    )

    tm2, tk2 = _tile(F, 512), _rtile(T, 512)
    (dwd,) = _mm(
        pfx + "_dwd", sv["h"], dfb, _TN, (F // tm2, 1, T // tk2),
        pl.BlockSpec((tk2, tm2), lambda i, j, k: (k, i)), pl.BlockSpec((tk2, D), lambda i, j, k: (k, 0)), (tm2, D),
        [], [], [SDS((F, D), _GDT)], [pl.BlockSpec((tm2, D), lambda i, j, k: (i, 0))], _store_epi,
    )

    tn3 = _tile(D, 1024)

    def epi3(acc, ex, outs):
        outs[0][...] = alpha * ex[0][...] + acc

    (dx,) = _mm(
        pfx + "_dx", dgu, wgu, _NT, (T // tm, D // tn3, 8),
        pl.BlockSpec((None, tm, FS), lambda i, j, k: (k // 4, i, k % 4)),
        pl.BlockSpec((None, tn3, FS), lambda i, j, k: (k, j, 0)), (tm, tn3),
        [dz], [pl.BlockSpec((tm, tn3), lambda i, j, k: (i, j))],
        [SDS((T, D), f32)], [pl.BlockSpec((tm, tn3), lambda i, j, k: (i, j))], epi3,
    )

    tm4, tk4 = _rtile(D, 512), _tile(T, 1024)
    (dwgu,) = _mm(
        pfx + "_dwgu", jnp.transpose(sv["xb"]), dgu, _NN, (D // tm4, 8, T // tk4),
        pl.BlockSpec((tm4, tk4), lambda i, j, k: (i, k)),
        pl.BlockSpec((None, tk4, FS), lambda i, j, k: (j // 4, k, j % 4)), (tm4, FS),
        [], [], [SDS((8, D, FS), _GDT)], [pl.BlockSpec((None, tm4, FS), lambda i, j, k: (j, i, 0))], _store_epi,
    )
    return dx, dwgu, dwd, dg, db


def _zoh(a_re, a_im, log_dt, b_re_t, b_im_t):
    dt = jnp.exp(log_dt)
    mag = jnp.exp(a_re * dt)
    lr_, li_ = mag * jnp.cos(a_im * dt), mag * jnp.sin(a_im * dt)
    den = a_re * a_re + a_im * a_im
    pr, pi = lr_ - 1.0, li_
    qr, qi = a_re / den, -a_im / den
    zr, zi = pr * qr - pi * qi, pr * qi + pi * qr
    bbr = zr[None] * b_re_t - zi[None] * b_im_t
    bbi = zr[None] * b_im_t + zi[None] * b_re_t
    return lr_, li_, bbr, bbi


def _zoh_fwd(name, a_re, a_im, log_dt, b_re_t, b_im_t):
    G, P = a_re.shape
    H = b_re_t.shape[0]

    def body(ar, ai, ld, br, bi, o1, o2, o3, o4):
        r = _zoh(ar[...], ai[...], ld[...], br[...], bi[...])
        o1[...], o2[...], o3[...], o4[...] = r

    return pl.pallas_call(
        body, out_shape=[SDS((G, P), f32), SDS((G, P), f32), SDS((H, G, P), f32), SDS((H, G, P), f32)], name=name,
    )(a_re, a_im, log_dt, b_re_t, b_im_t)


def _zoh_bwd(name, a_re, a_im, log_dt, b_re_t, b_im_t, dlr, dli, dbbr, dbbi):
    G, P = a_re.shape
    H = b_re_t.shape[0]

    def body(ar, ai, ld, br, bi, g1, g2, g3, g4, o1, o2, o3, o4, o5):
        _, vjp = jax.vjp(_zoh, ar[...], ai[...], ld[...], br[...], bi[...])
        r = vjp((g1[...], g2[...], g3[...], g4[...]))
        o1[...], o2[...], o3[...], o4[...], o5[...] = r

    return pl.pallas_call(
        body, out_shape=[SDS((G, P), f32), SDS((G, P), f32), SDS((G, 1), f32), SDS((H, G, P), f32), SDS((H, G, P), f32)],
        name=name,
    )(a_re, a_im, log_dt, b_re_t, b_im_t, dlr, dli, dbbr, dbbi)


def _blockdiag(m):
    G, A, B = m.shape
    eye = jnp.eye(8, dtype=bool)
    m4 = m.reshape(G // 8, 8, A, B)
    out = jnp.where(eye[None, :, None, :, None], m4[:, :, :, None, :], jnp.zeros((), m.dtype))
    return out.reshape(G // 8, 8 * A, 8 * B)


def _blockdiag_extract(mb, A, B):
    J = mb.shape[0]
    m5 = mb.reshape(J, 8, A, 8, B)
    d = jnp.stack([m5[:, i, :, i, :] for i in range(8)], axis=1)
    return d.reshape(J * 8, A, B)


def _bd2(name, a, a_col0, b1, b2, out_dtype=f32):
    T = a.shape[0]
    J, KA, NB = b1.shape
    tm = _rtile(T, 512)

    def body(a_ref, b1_ref, b2_ref, o1, o2):
        av = a_ref[...].astype(_MXU)
        o1[...] = jnp.dot(av, b1_ref[...].astype(_MXU), preferred_element_type=f32).astype(o1.dtype)
        o2[...] = jnp.dot(av, b2_ref[...].astype(_MXU), preferred_element_type=f32).astype(o2.dtype)

    bs = pl.BlockSpec((None, KA, NB), lambda i, j: (j, 0, 0))
    os_ = pl.BlockSpec((tm, NB), lambda i, j: (i, j))
    return pl.pallas_call(
        body, grid=(T // tm, J), in_specs=[pl.BlockSpec((tm, KA), lambda i, j: (i, j + a_col0)), bs, bs],
        out_specs=[os_, os_], out_shape=[SDS((T, J * NB), out_dtype)] * 2,
        compiler_params=_cp(("parallel", "parallel")), name=name,
    )(a, b1, b2)


def _bd_sum(name, a1, a2, b1, b2, extras, extra_specs_fn, out_shape, epilogue):
    T = a1.shape[0]
    J, KA, NB = b1.shape
    tm = _rtile(T, 512)
    ne = len(extras)

    def body(*refs):
        a1_ref, a2_ref, b1_ref, b2_ref = refs[:4]
        ex = refs[4:4 + ne]
        outs = refs[4 + ne:]
        acc = jnp.dot(a1_ref[...].astype(_MXU), b1_ref[...].astype(_MXU), preferred_element_type=f32)
        acc = acc + jnp.dot(a2_ref[...].astype(_MXU), b2_ref[...].astype(_MXU), preferred_element_type=f32)
        epilogue(acc, ex, outs)

    as_ = pl.BlockSpec((tm, KA), lambda i, j: (i, j))
    bs = pl.BlockSpec((None, KA, NB), lambda i, j: (j, 0, 0))
    os_ = pl.BlockSpec((tm, NB), lambda i, j: (i, j))
    return pl.pallas_call(
        body, grid=(T // tm, J), in_specs=[as_, as_, bs, bs, *extra_specs_fn(tm, NB)],
        out_specs=[os_] * len(out_shape), out_shape=list(out_shape),
        compiler_params=_cp(("parallel", "parallel")), name=name,
    )(a1, a2, b1, b2, *extras)


def _bdT2(name, a1, a2, a_col0, b1, b2, b_col0, KA, NB, J):
    T = a1.shape[0]
    tk = _rtile(T, 512)

    def body(a1_ref, a2_ref, b1_ref, b2_ref, o1, o2):
        k = pl.program_id(1)
        p1 = lax.dot_general(a1_ref[...].astype(_MXU), b1_ref[...].astype(_MXU), _TN, preferred_element_type=f32)
        p2 = lax.dot_general(a2_ref[...].astype(_MXU), b2_ref[...].astype(_MXU), _TN, preferred_element_type=f32)

        @pl.when(k == 0)
        def _():
            o1[...] = p1
            o2[...] = p2

        @pl.when(k > 0)
        def _():
            o1[...] += p1
            o2[...] += p2

    as_ = pl.BlockSpec((tk, KA), lambda j, k: (k, j + a_col0))
    bs = pl.BlockSpec((tk, NB), lambda j, k: (k, j + b_col0))
    os_ = pl.BlockSpec((None, KA, NB), lambda j, k: (j, 0, 0))
    return pl.pallas_call(
        body, grid=(J, T // tk), in_specs=[as_, as_, bs, bs], out_specs=[os_, os_],
        out_shape=[SDS((J, KA, NB), f32)] * 2, compiler_params=_cp(("parallel", "arbitrary")), name=name,
    )(a1, a2, b1, b2)


_RB = 8


def _cmul(ar, ai, br, bi):
    return ar * br - ai * bi, ar * bi + ai * br


def _lam_powers(lr_v, li_v, cb):
    pw = {1: (lr_v, li_v)}
    for k in range(2, _RB + 1):
        pw[k] = _cmul(*pw[k - 1], lr_v, li_v)
    return pw


def _row_powers(pw, row, cb, reverse):
    outr = jnp.zeros((_RB, cb), f32)
    outi = jnp.zeros((_RB, cb), f32)
    for r in range(_RB):
        k = _RB - r if reverse else r + 1
        outr = jnp.where(row == r, pw[k][0], outr)
        outi = jnp.where(row == r, pw[k][1], outi)
    return outr, outi


def _tile_scan(xr, xi, pw, row, reverse):
    for k in (1, 2, 4):
        if reverse:
            keep = row < _RB - k
            shr, shi = pltpu.roll(xr, _RB - k, 0), pltpu.roll(xi, _RB - k, 0)
        else:
            keep = row >= k
            shr, shi = pltpu.roll(xr, k, 0), pltpu.roll(xi, k, 0)
        shr, shi = jnp.where(keep, shr, 0.0), jnp.where(keep, shi, 0.0)
        mr, mi = pw[k]
        xr, xi = xr + (mr * shr - mi * shi), xi + (mr * shi + mi * shr)
    return xr, xi


def _s5_scan(name, bur, bui, lr_, li_):
    T, N = bur.shape
    cb = _tile(N, 512)

    def body(br_ref, bi_ref, lr_ref, li_ref, sr_ref, si_ref):
        pw = _lam_powers(lr_ref[...], li_ref[...], cb)
        row = lax.broadcasted_iota(jnp.int32, (_RB, cb), 0)
        cr, ci = _row_powers(pw, row, cb, False)

        def step(n, carry):
            pr, pi = carry
            t0 = pl.multiple_of(n * _RB, _RB)
            xr, xi = _tile_scan(br_ref[pl.ds(t0, _RB), :], bi_ref[pl.ds(t0, _RB), :], pw, row, False)
            xr, xi = xr + (cr * pr - ci * pi), xi + (cr * pi + ci * pr)
            sr_ref[pl.ds(t0, _RB), :] = xr
            si_ref[pl.ds(t0, _RB), :] = xi
            return xr[_RB - 1:_RB, :], xi[_RB - 1:_RB, :]

        z = jnp.zeros((1, cb), f32)
        lax.fori_loop(0, T // _RB, step, (z, z))

    col = pl.BlockSpec((T, cb), lambda j: (0, j))
    vec = pl.BlockSpec((1, cb), lambda j: (0, j))
    return pl.pallas_call(
        body, grid=(N // cb,), in_specs=[col, col, vec, vec], out_specs=[col, col],
        out_shape=[SDS((T, N), f32)] * 2, compiler_params=_cp(("parallel",)), name=name,
    )(bur, bui, lr_, li_)


def _s5_scan_bwd(name, dsr, dsi, sr, si, lr_, li_):
    T, N = dsr.shape
    cb = _tile(N, 256)

    def body(dr_ref, di_ref, sr_ref, si_ref, lr_ref, li_ref, ar_ref, ai_ref, glr_ref, gli_ref):
        pw = _lam_powers(lr_ref[...], -li_ref[...], cb)
        row = lax.broadcasted_iota(jnp.int32, (_RB, cb), 0)
        cr, ci = _row_powers(pw, row, cb, True)
        NT = T // _RB

        def tile(t0, nxt, prev_last):
            xr, xi = _tile_scan(dr_ref[pl.ds(t0, _RB), :], di_ref[pl.ds(t0, _RB), :], pw, row, True)
            xr, xi = xr + (cr * nxt[0] - ci * nxt[1]), xi + (cr * nxt[1] + ci * nxt[0])
            ar_ref[pl.ds(t0, _RB), :] = xr
            ai_ref[pl.ds(t0, _RB), :] = xi
            pr = jnp.where(row == 0, prev_last[0], pltpu.roll(sr_ref[pl.ds(t0, _RB), :], 1, 0))
            pi = jnp.where(row == 0, prev_last[1], pltpu.roll(si_ref[pl.ds(t0, _RB), :], 1, 0))
            return xr, xi, xr * pr + xi * pi, xi * pr - xr * pi

        def step(n, carry):
            nr, ni, glr, gli = carry
            t0 = pl.multiple_of((NT - 1 - n) * _RB, _RB)
            tp = pl.multiple_of((NT - 2 - n) * _RB, _RB)
            prev_last = (sr_ref[pl.ds(tp, _RB), :][_RB - 1:_RB, :], si_ref[pl.ds(tp, _RB), :][_RB - 1:_RB, :])
            xr, xi, gr, gi = tile(t0, (nr, ni), prev_last)
            return xr[0:1, :], xi[0:1, :], glr + gr, gli + gi

        z1 = jnp.zeros((1, cb), f32)
        z8 = jnp.zeros((_RB, cb), f32)
        nr, ni, glr, gli = lax.fori_loop(0, NT - 1, step, (z1, z1, z8, z8))
        _, _, gr, gi = tile(0, (nr, ni), (z1, z1))
        glr_ref[...] = jnp.sum(glr + gr, axis=0, keepdims=True)
        gli_ref[...] = jnp.sum(gli + gi, axis=0, keepdims=True)

    col = pl.BlockSpec((T, cb), lambda j: (0, j))
    vec = pl.BlockSpec((1, cb), lambda j: (0, j))
    return pl.pallas_call(
        body, grid=(N // cb,), in_specs=[col, col, col, col, vec, vec], out_specs=[col, col, vec, vec],
        out_shape=[SDS((T, N), f32), SDS((T, N), f32), SDS((1, N), f32), SDS((1, N), f32)],
        compiler_params=_cp(("parallel",)), name=name,
    )(dsr, dsi, sr, si, lr_, li_)


def _conv_fwd(name, p, col0, w, GW3):
    T = p.shape[0]
    K = w.shape[0]
    cb = 128
    c0 = col0 // cb

    def body(x_ref, w_ref, o_ref, pad_ref):
        pad_ref[pl.ds(0, 8), :] = jnp.zeros((8, cb), f32)
        pad_ref[pl.ds(8, T), :] = x_ref[...]
        wv = w_ref[...]
        acc = jnp.zeros((T, cb), f32)
        for j in range(K):
            acc = acc + wv[j:j + 1, :] * pad_ref[pl.ds(8 - (K - 1) + j, T), :]
        o_ref[...] = jax.nn.silu(acc)

    return pl.pallas_call(
        body, grid=(GW3 // cb,),
        in_specs=[pl.BlockSpec((T, cb), lambda j: (0, j + c0)), pl.BlockSpec((K, cb), lambda j: (0, j))],
        out_specs=pl.BlockSpec((T, cb), lambda j: (0, j)), out_shape=SDS((T, GW3), f32),
        scratch_shapes=[pltpu.VMEM((T + 8, cb), f32)], compiler_params=_cp(("parallel",)), name=name,
    )(p, w)


def _conv_bwd(name, p, col0, w, dout3):
    T = p.shape[0]
    K = w.shape[0]
    GW = dout3.shape[2]
    GW3 = 3 * GW
    cb = 128
    c0 = col0 // cb
    nb = GW // cb

    def body(x_ref, w_ref, d_ref, dx_ref, dw_ref, pad_ref, dpad_ref):
        pad_ref[pl.ds(0, 8), :] = jnp.zeros((8, cb), f32)
        pad_ref[pl.ds(8, T), :] = x_ref[...]
        wv = w_ref[...]
        pre = jnp.zeros((T, cb), f32)
        for j in range(K):
            pre = pre + wv[j:j + 1, :] * pad_ref[pl.ds(8 - (K - 1) + j, T), :]
        _, vjp = jax.vjp(jax.nn.silu, pre)
        (dpre,) = vjp(d_ref[...])
        dpad_ref[pl.ds(0, T), :] = dpre
        dpad_ref[pl.ds(T, 8), :] = jnp.zeros((8, cb), f32)
        dx = jnp.zeros((T, cb), f32)
        rows = []
        for j in range(K):
            dx = dx + wv[j:j + 1, :] * dpad_ref[pl.ds((K - 1) - j, T), :]
            rows.append(jnp.sum(dpre * pad_ref[pl.ds(8 - (K - 1) + j, T), :], axis=0, keepdims=True))
        dx_ref[...] = dx.astype(dx_ref.dtype)
        for j in range(K):
            dw_ref[pl.ds(j, 1), :] = rows[j]

    return pl.pallas_call(
        body, grid=(GW3 // cb,),
        in_specs=[pl.BlockSpec((T, cb), lambda j: (0, j + c0)), pl.BlockSpec((K, cb), lambda j: (0, j)),
                  pl.BlockSpec((None, T, cb), lambda j: (j // nb, 0, j % nb))],
        out_specs=[pl.BlockSpec((T, cb), lambda j: (0, j)), pl.BlockSpec((K, cb), lambda j: (0, j))],
        out_shape=[SDS((T, GW3), _MXU), SDS((K, GW3), f32)],
        scratch_shapes=[pltpu.VMEM((T + 8, cb), f32), pltpu.VMEM((T + 8, cb), f32)],
        compiler_params=_cp(("parallel",)), name=name,
    )(p, w, dout3)


def _hdot(a, b, dims=_NN):
    return lax.dot_general(a, b, dims, precision=_HP, preferred_element_type=f32)


def _split(a):
    hi = a.astype(jnp.bfloat16)
    lo = (a - hi.astype(f32)).astype(jnp.bfloat16)
    return hi, lo


_BNN = (((2,), (1,)), ((0,), (0,)))
_BNT = (((2,), (2,)), ((0,), (0,)))
_BTN = (((1,), (1,)), ((0,), (0,)))


def _dot3_raw(a, b, dims):
    ah, al = _split(a)
    bh, bl = _split(b)
    d = functools.partial(lax.dot_general, dimension_numbers=dims, preferred_element_type=f32)
    return d(ah, bh) + (d(al, bh) + d(ah, bl))


@jax.custom_vjp
def _dot3(a, b):
    return _dot3_raw(a, b, _BNN)


def _dot3_fwd(a, b):
    return _dot3_raw(a, b, _BNN), (a, b)


def _dot3_bwd(res, g):
    a, b = res
    return _dot3_raw(g, b, _BNT), _dot3_raw(a, g, _BTN)


_dot3.defvjp(_dot3_fwd, _dot3_bwd)


def _ldot(a, b, dims=_BNN):
    return lax.dot_general(a.astype(_MXU), b.astype(_MXU), dims, preferred_element_type=f32)


def _sdot(a, b):
    return _ldot(a, b)


def _gdn_chunk(S, q, k, v, z, bl, ain, alog, dtb, nw):
    H, C, d = q.shape
    ri = lax.broadcasted_iota(jnp.int32, (H, C, C), 1)
    ci = lax.broadcasted_iota(jnp.int32, (H, C, C), 2)
    causal = ri >= ci
    strict = ri > ci
    tri = causal.astype(f32)
    qn = q * lax.rsqrt(jnp.sum(q * q, axis=-1, keepdims=True) + L2_EPS) * (d ** -0.5)
    kn = k * lax.rsqrt(jnp.sum(k * k, axis=-1, keepdims=True) + L2_EPS)
    beta = jax.nn.sigmoid(bl)
    g = -jnp.exp(alog) * jax.nn.softplus(ain + dtb)
    gb = jnp.broadcast_to(g, (H, C, C))
    gc_col = _dot3(tri, gb)
    gc_row = _dot3(jnp.ones((H, C, C), f32), jnp.where(ri <= ci, gb, 0.0))
    diff = jnp.where(causal, gc_col - gc_row, 0.0)
    decay = jnp.where(causal, jnp.exp(diff), 0.0)
    gcum = gc_col[:, :, 0:1]
    glast = gc_col[:, C - 1:C, 0:1]
    egc = jnp.exp(gcum)
    kb = kn * beta
    lower = jnp.where(strict, _ldot(kb, kn, _BNT) * decay, 0.0)
    x = jnp.concatenate([v * beta, kb * egc], axis=-1)
    m = -lower
    for it in range(6):
        x = x + _sdot(m, x)
        if it < 5:
            m = _sdot(m, m)
    u_val, w_key = x[:, :, :d], x[:, :, d:]
    attn = _ldot(qn, kn, _BNT) * decay
    q_dec = qn * egc
    k_dec = kn * jnp.exp(glast - gcum)
    v_new = u_val - _ldot(w_key, S)
    out = _ldot(q_dec, S) + _ldot(attn, v_new)
    s_new = S * jnp.exp(glast) + _ldot(k_dec, v_new, _BTN)
    o = out * lax.rsqrt(jnp.mean(out * out, axis=-1, keepdims=True) + RMS_EPS) * nw
    o = o * jax.nn.silu(z)
    return s_new, o


def _heads_per_step(NH, HD, zcol0):
    for hb in (4, 2):
        if NH % hb == 0 and zcol0 % (hb * HD) == 0:
            return hb
    return 1


def _gdn_fwd(name, qkv, p, zcol0, blt, aint, alog, dtb, nw, NH, HD):
    T = qkv.shape[0]
    N = T // CHUNK
    GW = NH * HD
    HB = _heads_per_step(NH, HD, zcol0)
    W = HB * HD
    zc0 = zcol0 // W
    nb = GW // W

    def body(q_ref, k_ref, v_ref, z_ref, bl_ref, ain_ref, al_ref, dtb_ref, nw_ref, o_ref, ssave_ref, s_scr):
        n = pl.program_id(1)

        @pl.when(n == 0)
        def _():
            s_scr[...] = jnp.zeros_like(s_scr)

        heads = lambda r: jnp.stack([r[:, hh * HD:(hh + 1) * HD] for hh in range(HB)], axis=0)
        s_in = s_scr[...]
        ssave_ref[...] = s_in
        s_new, o = _gdn_chunk(s_in, heads(q_ref), heads(k_ref), heads(v_ref), heads(z_ref), bl_ref[...], ain_ref[...],
                              al_ref[...], dtb_ref[...], nw_ref[...])
        s_scr[...] = s_new
        for hh in range(HB):
            o_ref[:, hh * HD:(hh + 1) * HD] = o[hh].astype(o_ref.dtype)

    ch = lambda off: pl.BlockSpec((CHUNK, W), lambda h, n: (n, h + off))
    sc = pl.BlockSpec((HB, CHUNK, 1), lambda h, n: (h, n, 0))
    hs = pl.BlockSpec((HB, 1, 1), lambda h, n: (h, 0, 0))
    return pl.pallas_call(
        body, grid=(NH // HB, N),
        in_specs=[ch(0), ch(nb), ch(2 * nb), ch(zc0), sc, sc, hs, hs, pl.BlockSpec((1, HD), lambda h, n: (0, 0))],
        out_specs=[pl.BlockSpec((CHUNK, W), lambda h, n: (n, h)),
                   pl.BlockSpec((HB, None, HD, HD), lambda h, n: (h, n, 0, 0))],
        out_shape=[SDS((T, GW), _MXU), SDS((NH, N, HD, HD), f32)],
        scratch_shapes=[pltpu.VMEM((HB, HD, HD), f32)], compiler_params=_cp(("parallel", "arbitrary")), name=name,
    )(qkv, qkv, qkv, p, blt, aint, alog, dtb, nw)


def _gdn_bwd(name, qkv, p, zcol0, blt, aint, alog, dtb, nw, ssave, do, NH, HD):
    T = qkv.shape[0]
    N = T // CHUNK
    GW = NH * HD
    HB = _heads_per_step(NH, HD, zcol0)
    W = HB * HD
    zc0 = zcol0 // W
    nb = GW // W

    def body(q_ref, k_ref, v_ref, z_ref, bl_ref, ain_ref, al_ref, dtb_ref, nw_ref, ss_ref, do_ref,
             dqkv_ref, dz_ref, dbl_ref, dain_ref, dal_ref, ddtb_ref, dnw_ref, ds_scr):
        h = pl.program_id(0)
        n = pl.program_id(1)

        @pl.when(n == 0)
        def _():
            ds_scr[...] = jnp.zeros_like(ds_scr)

        heads = lambda r: jnp.stack([r[:, hh * HD:(hh + 1) * HD] for hh in range(HB)], axis=0)
        _, vjp = jax.vjp(_gdn_chunk, ss_ref[...], heads(q_ref), heads(k_ref), heads(v_ref), heads(z_ref), bl_ref[...],
                         ain_ref[...], al_ref[...], dtb_ref[...], nw_ref[...])
        ds, dq, dk, dv, dz, dbl, dain, dal, ddtb, dnw = vjp((ds_scr[...], heads(do_ref).astype(f32)))
        ds_scr[...] = ds
        for hh in range(HB):
            cs = slice(hh * HD, (hh + 1) * HD)
            dqkv_ref[0, :, cs] = dq[hh]
            dqkv_ref[1, :, cs] = dk[hh]
            dqkv_ref[2, :, cs] = dv[hh]
            dz_ref[:, cs] = dz[hh].astype(dz_ref.dtype)
        dbl_ref[...] = dbl
        dain_ref[...] = dain

        @pl.when(n == 0)
        def _():
            dal_ref[...] = dal
            ddtb_ref[...] = ddtb

        @pl.when(n > 0)
        def _():
            dal_ref[...] += dal
            ddtb_ref[...] += ddtb

        @pl.when((n == 0) & (h == 0))
        def _():
            dnw_ref[...] = dnw

        @pl.when((n > 0) | (h > 0))
        def _():
            dnw_ref[...] += dnw

    R = N - 1
    ch = lambda off: pl.BlockSpec((CHUNK, W), lambda h, n: (R - n, h + off))
    sc = pl.BlockSpec((HB, CHUNK, 1), lambda h, n: (h, R - n, 0))
    hs = pl.BlockSpec((HB, 1, 1), lambda h, n: (h, 0, 0))
    nws = pl.BlockSpec((1, HD), lambda h, n: (0, 0))
    return pl.pallas_call(
        body, grid=(NH // HB, N),
        in_specs=[ch(0), ch(nb), ch(2 * nb), ch(zc0), sc, sc, hs, hs, nws,
                  pl.BlockSpec((HB, None, HD, HD), lambda h, n: (h, R - n, 0, 0)),
                  pl.BlockSpec((CHUNK, W), lambda h, n: (R - n, h))],
        out_specs=[pl.BlockSpec((3, CHUNK, W), lambda h, n: (0, R - n, h)),
                   pl.BlockSpec((CHUNK, W), lambda h, n: (R - n, h)), sc, sc, hs, hs, nws],
        out_shape=[SDS((3, T, GW), f32), SDS((T, GW), _MXU), SDS((NH, T, 1), f32), SDS((NH, T, 1), f32),
                   SDS((NH, 1, 1), f32), SDS((NH, 1, 1), f32), SDS((1, HD), f32)],
        scratch_shapes=[pltpu.VMEM((HB, HD, HD), f32)], compiler_params=_cp(("arbitrary", "arbitrary")), name=name,
    )(qkv, qkv, qkv, p, blt, aint, alog, dtb, nw, ssave, do)


def _loss_head(name, y, tgt):
    T, D = y.shape
    tm = _rtile(T, 256)

    def body(y_ref, t_ref, dy_ref, l_ref):
        i = pl.program_id(0)
        err = y_ref[...] - t_ref[...]
        dy_ref[...] = err * (1.0 / D)
        part = 0.5 * jnp.sum(jnp.sum(err * err, axis=-1, keepdims=True) * (1.0 / D), axis=0, keepdims=True)

        @pl.when(i == 0)
        def _():
            l_ref[...] = part

        @pl.when(i > 0)
        def _():
            l_ref[...] += part

    row = pl.BlockSpec((tm, D), lambda i: (i, 0))
    return pl.pallas_call(
        body, grid=(T // tm,), in_specs=[row, row], out_specs=[row, pl.BlockSpec((1, 1), lambda i: (0, 0))],
        out_shape=[SDS((T, D), f32), SDS((1, 1), f32)], compiler_params=_cp(("arbitrary",)), name=name,
    )(y, tgt)


def _adam_math(w, g, m, v):
    m = ADAM_B1 * m + (1.0 - ADAM_B1) * g
    v = ADAM_B2 * v + (1.0 - ADAM_B2) * jnp.square(g)
    m_hat = m / (1.0 - ADAM_B1 ** ADAM_STEP)
    v_hat = v / (1.0 - ADAM_B2 ** ADAM_STEP)
    delta = -ADAM_LR * (m_hat / (jnp.sqrt(v_hat) + ADAM_EPS) + ADAM_WD * w)
    return delta, m, v


def _add_mine(name, full, recv, me, out_dtype):
    N, _, R, C = full.shape
    tr = _rtile(R, max(16, (1 << 19) // max(C, 1) // 16 * 16))

    def body(me_ref, a_ref, b_ref, o_ref):
        o_ref[...] = (a_ref[...].astype(f32) + b_ref[...].astype(f32)).astype(o_ref.dtype)

    blk = pl.BlockSpec((None, tr, C), lambda n, i, me_ref: (n, i, 0))
    return pl.pallas_call(
        body,
        grid_spec=pltpu.PrefetchScalarGridSpec(
            num_scalar_prefetch=1, grid=(N, R // tr),
            in_specs=[pl.BlockSpec((None, None, tr, C), lambda n, i, me_ref: (n, me_ref[0], i, 0)), blk], out_specs=blk),
        out_shape=SDS((N, R, C), out_dtype), compiler_params=_cp(("parallel", "parallel")), name=name,
    )(me, full, recv)


def _adamw_big(name, full, recv, me, w, m, v, l, accs):
    _, R, C = full.shape
    L = w.shape[0]
    tr = _rtile(R, max(16, (1 << 18) // max(C, 1) // 16 * 16))

    def body(me_ref, ga_ref, gb_ref, w_ref, m_ref, v_ref, a0, a1, a2, a3, g_ref, d_ref, nm_ref, nv_ref):
        g = ga_ref[...].astype(f32) + gb_ref[...].astype(f32)
        d, nm, nv = _adam_math(w_ref[...], g, m_ref[...], v_ref[...])
        g_ref[...] = g
        d_ref[...] = d
        nm_ref[...] = nm
        nv_ref[...] = nv

    blk = pl.BlockSpec((tr, C), lambda i, me_ref: (i, 0))
    lblk = pl.BlockSpec((None, tr, C), lambda i, me_ref: (l, i, 0))
    untouched = pl.BlockSpec(memory_space=pl.ANY)
    return pl.pallas_call(
        body,
        grid_spec=pltpu.PrefetchScalarGridSpec(
            num_scalar_prefetch=1, grid=(R // tr,),
            in_specs=[pl.BlockSpec((None, tr, C), lambda i, me_ref: (me_ref[0], i, 0)), blk, lblk, lblk, lblk] + [untouched] * 4,
            out_specs=[lblk] * 4),
        out_shape=[SDS((L, R, C), f32)] * 4, input_output_aliases={6: 0, 7: 1, 8: 2, 9: 3},
        compiler_params=_cp(("parallel",)), name=name,
    )(me, full, recv, w, m, v, *accs)


def _adamw_small(name, gall, w, m, v):
    _, R, C = gall.shape
    tr = _rtile(R, 512)

    def body(ga_ref, w_ref, m_ref, v_ref, g_ref, d_ref, nm_ref, nv_ref):
        g = ga_ref[0]
        for s in range(1, 8):
            g = g + ga_ref[s]
        d, nm, nv = _adam_math(w_ref[...], g, m_ref[...], v_ref[...])
        g_ref[...] = g
        d_ref[...] = d
        nm_ref[...] = nm
        nv_ref[...] = nv

    blk = pl.BlockSpec((tr, C), lambda i: (i, 0))
    return pl.pallas_call(
        body, grid=(R // tr,), in_specs=[pl.BlockSpec((8, tr, C), lambda i: (0, i, 0)), blk, blk, blk], out_specs=[blk] * 4,
        out_shape=[SDS((R, C), f32)] * 4, compiler_params=_cp(("parallel",)), name=name,
    )(gall, w, m, v)


def _peer(axis):
    x, y, c = lax.axis_index("x"), lax.axis_index("y"), lax.axis_index("c")
    me = {"x": x, "y": y, "c": c}[axis]
    peer = {"x": (1 - x, y, c), "y": (x, 1 - y, c), "c": (x, y, 1 - c)}[axis]
    return me, peer


def _held(ref, done):
    idx = tuple(slice(None) if a in done else lax.axis_index(a) for a in ("x", "y", "c"))
    return ref.at[idx]


def _gather_stage(name, bufs, axes, dones):
    n = len(bufs)
    hbm = pl.BlockSpec(memory_space=pltpu.HBM)

    def body(*refs):
        outs = refs[n:2 * n]
        send_sems, recv_sems = refs[2 * n:]
        cps = []
        for t in range(n):
            _, peer = _peer(axes[t])
            blk = _held(outs[t], dones[t])
            cps.append(pltpu.make_async_remote_copy(src_ref=blk, dst_ref=blk, send_sem=send_sems.at[t],
                                                    recv_sem=recv_sems.at[t], device_id=peer, device_id_type=_MESH_T))
        for cp in cps:
            cp.start()
        for cp in cps:
            cp.wait()

    return pl.pallas_call(
        body, in_specs=[hbm] * n, out_specs=[hbm] * n, out_shape=[SDS(b.shape, b.dtype) for b in bufs],
        input_output_aliases={t: t for t in range(n)},
        scratch_shapes=[pltpu.SemaphoreType.DMA((n,)), pltpu.SemaphoreType.DMA((n,))], name=name,
    )(*bufs)


def _scatter_stage(name, tensors, axes):
    n = len(tensors)
    hbm = pl.BlockSpec(memory_space=pltpu.HBM)

    def body(*refs):
        ins, recvs = refs[:n], refs[n:2 * n]
        send_sems, recv_sems = refs[2 * n:]
        cps = []
        for t in range(n):
            me, peer = _peer(axes[t])
            cps.append(pltpu.make_async_remote_copy(
                src_ref=ins[t].at[:, 1 - me], dst_ref=recvs[t], send_sem=send_sems.at[t], recv_sem=recv_sems.at[t],
                device_id=peer, device_id_type=_MESH_T))
        for cp in cps:
            cp.start()
        for cp in cps:
            cp.wait()

    return pl.pallas_call(
        body, in_specs=[hbm] * n, out_specs=[hbm] * n,
        out_shape=[SDS((t.shape[0],) + tuple(t.shape[2:]), t.dtype) for t in tensors],
        scratch_shapes=[pltpu.SemaphoreType.DMA((n,)), pltpu.SemaphoreType.DMA((n,))], name=name,
    )(*tensors)


_HBM = pl.BlockSpec(memory_space=pltpu.HBM)
_SEM = pl.BlockSpec(memory_space=pltpu.SEMAPHORE)
_EFFECT = pltpu.SideEffectType.DATAFLOW_SIDE_EFFECTING


def _split_start(name, arrays, n_copies, make_copies):
    na = len(arrays)

    def body(*refs):
        ins = refs[:na]
        send_sems, recv_sems = refs[na], refs[na + 1]
        token = refs[2 * na + 2]
        for cp in make_copies(ins, send_sems, recv_sems):
            cp.start()
        token[...] = jnp.zeros_like(token)

    res = pl.pallas_call(
        body, name=name,
        out_shape=(pltpu.SemaphoreType.DMA((n_copies,)), pltpu.SemaphoreType.DMA((n_copies,)),
                   *[pltpu.HBM(a.shape, a.dtype) for a in arrays], SDS((8, 128), f32)),
        in_specs=[_HBM] * na, out_specs=(_SEM, _SEM, *[_HBM] * na, pl.BlockSpec(memory_space=pltpu.VMEM)),
        input_output_aliases={i: 2 + i for i in range(na)},
        compiler_params=pltpu.CompilerParams(has_side_effects=_EFFECT),
    )(*[pltpu.with_memory_space_constraint(a, pltpu.HBM) for a in arrays])
    return res[0], res[1], list(res[2:2 + na]), res[2 + na]


def _split_wait(name, arrays, send_sems, recv_sems, after, make_copies):
    na = len(arrays)

    def body(*refs):
        ins = refs[:na]
        for cp in make_copies(ins, refs[na], refs[na + 1]):
            cp.wait_send()
            cp.wait_recv()

    res = pl.pallas_call(
        body, name=name, out_shape=tuple(pltpu.HBM(a.shape, a.dtype) for a in arrays),
        in_specs=[_HBM] * na + [_SEM, _SEM, pl.BlockSpec(memory_space=pl.ANY)], out_specs=tuple([_HBM] * na),
        input_output_aliases={i: i for i in range(na)},
        compiler_params=pltpu.CompilerParams(has_side_effects=_EFFECT),
    )(*arrays, send_sems, recv_sems, after)
    return list(res)


def _gather_copies(axes, dones):
    def make(refs, send_sems, recv_sems):
        cps = []
        for t in range(len(axes)):
            _, peer = _peer(axes[t])
            blk = _held(refs[t], dones[t])
            cps.append(pltpu.make_async_remote_copy(src_ref=blk, dst_ref=blk, send_sem=send_sems.at[t],
                                                    recv_sem=recv_sems.at[t], device_id=peer, device_id_type=_MESH_T))
        return cps
    return make


def _scatter_copies(axes):
    n = len(axes)

    def make(refs, send_sems, recv_sems):
        cps = []
        for t in range(n):
            me, peer = _peer(axes[t])
            cps.append(pltpu.make_async_remote_copy(
                src_ref=refs[t].at[:, 1 - me], dst_ref=refs[n + t], send_sem=send_sems.at[t], recv_sem=recv_sems.at[t],
                device_id=peer, device_id_type=_MESH_T))
        return cps
    return make


class _AsyncGather:
    def __init__(self, pfx, tensors, paths):
        x, y, c = (lax.axis_index(a) for a in ("x", "y", "c"))
        self.pfx, self.shapes = pfx, [tuple(t.shape) for t in tensors]
        self.bufs = [lax.dynamic_update_slice(lax.empty((2, 2, 2) + tuple(t.shape), t.dtype), t[None, None, None],
                                              (x, y, c) + (0,) * t.ndim) for t in tensors]
        self.orders = [tuple(p) + ("c",) for p in paths]
        self.ph = 0

    def _make(self):
        return _gather_copies([o[self.ph] for o in self.orders], [o[:self.ph] for o in self.orders])

    def start(self):
        self.ss, self.rs, self.bufs, tok = _split_start(f"{self.pfx}_start{self.ph}", self.bufs, len(self.bufs), self._make())
        return tok

    def wait(self, after):
        self.bufs = _split_wait(f"{self.pfx}_wait{self.ph}", self.bufs, self.ss, self.rs, after, self._make())
        self.ph += 1

    def result(self):
        return [b.reshape((8,) + s) for b, s in zip(self.bufs, self.shapes)]


class _AsyncReduceScatter:
    def __init__(self, pfx, tensors, paths):
        self.pfx = pfx
        self.rcs = [tuple(t.shape[1:]) for t in tensors]
        self.orders = [("c",) + tuple(p) for p in paths]
        self.left = [["x", "y", "c"] for _ in tensors]
        self.cur = list(tensors)
        self.ph = 0

    def start(self):
        n = len(self.cur)
        views = []
        for i, (t, rc) in enumerate(zip(self.cur, self.rcs)):
            pos = self.left[i].index(self.orders[i][self.ph])
            nb, na = 2 ** pos, 2 ** (len(self.left[i]) - pos - 1)
            views.append(t.reshape((nb, 2, na * rc[0], rc[1])))
        lands = [lax.empty((v.shape[0],) + tuple(v.shape[2:]), v.dtype) for v in views]
        self.make = _scatter_copies([o[self.ph] for o in self.orders])
        self.ss, self.rs, arrs, tok = _split_start(f"{self.pfx}_start{self.ph}", views + lands, n, self.make)
        self.arrs = arrs
        return tok

    def wait(self, after):
        n = len(self.cur)
        arrs = _split_wait(f"{self.pfx}_wait{self.ph}", self.arrs, self.ss, self.rs, after, self.make)
        views, recvs = arrs[:n], arrs[n:]
        ph = self.ph
        if ph == 2:
            self.out = [(v[0], r[0], o[2]) for v, r, o in zip(views, recvs, self.orders)]
        else:
            self.cur = [_add_mine(f"{self.pfx}_add{ph}_{i}", v, r, _coord(o[ph]), v.dtype)
                        for i, (v, r, o) in enumerate(zip(views, recvs, self.orders))]
            for i, o in enumerate(self.orders):
                self.left[i].remove(o[ph])
        self.ph += 1


def _coord(axis):
    return lax.axis_index(axis).astype(jnp.int32).reshape(1)


def _all_gather(pfx, tensors, paths):
    x, y, c = (lax.axis_index(a) for a in ("x", "y", "c"))
    bufs = []
    for t in tensors:
        zero = (0,) * t.ndim
        bufs.append(lax.dynamic_update_slice(lax.empty((2, 2, 2) + tuple(t.shape), t.dtype), t[None, None, None],
                                             (x, y, c) + zero))
    orders = [tuple(p) + ("c",) for p in paths]
    for ph in range(3):
        bufs = _gather_stage(f"{pfx}_{ph}", bufs, [o[ph] for o in orders], [o[:ph] for o in orders])
    return [b.reshape((8,) + tuple(t.shape)) for b, t in zip(bufs, tensors)]


def _reduce_scatter(pfx, tensors, paths):
    rcs = [tuple(t.shape[1:]) for t in tensors]
    orders = [("c",) + tuple(p) for p in paths]
    left = [["x", "y", "c"] for _ in tensors]
    cur = list(tensors)
    for ph in range(3):
        views = []
        for i, (t, rc) in enumerate(zip(cur, rcs)):
            pos = left[i].index(orders[i][ph])
            nb, na = 2 ** pos, 2 ** (len(left[i]) - pos - 1)
            views.append(t.reshape((nb, 2, na * rc[0], rc[1])))
        recvs = _scatter_stage(f"{pfx}_{ph}", views, [o[ph] for o in orders])
        if ph == 2:
            return [(v[0], r[0], o[2]) for v, r, o in zip(views, recvs, orders)]
        cur = [_add_mine(f"{pfx}_add{ph}_{i}", v, r, _coord(o[ph]), v.dtype)
               for i, (v, r, o) in enumerate(zip(views, recvs, orders))]
        for i, o in enumerate(orders):
            left[i].remove(o[ph])


def _mm_nn(name, a, w, out_dtype, tn_pref=1024):
    T, K = a.shape
    N = w.shape[1]
    tm, tn, tk = _rtile(T, 512), _tile(N, tn_pref), _tile(K, 2048)
    return _mm(
        name, a, w, _NN, (T // tm, N // tn, K // tk),
        pl.BlockSpec((tm, tk), lambda i, j, k: (i, k)), pl.BlockSpec((tk, tn), lambda i, j, k: (k, j)), (tm, tn),
        [], [], [SDS((T, N), out_dtype)], [pl.BlockSpec((tm, tn), lambda i, j, k: (i, j))], _store_epi,
    )[0]


def _mm_tn(name, a, b, out_dtype):
    T, M = a.shape
    N = b.shape[1]
    tm, tn, tk = _tile(M, 512), _tile(N, 2048), _rtile(T, 512)
    return _mm(
        name, a, b, _TN, (M // tm, N // tn, T // tk),
        pl.BlockSpec((tk, tm), lambda i, j, k: (k, i)), pl.BlockSpec((tk, tn), lambda i, j, k: (k, j)), (tm, tn),
        [], [], [SDS((M, N), out_dtype)], [pl.BlockSpec((tm, tn), lambda i, j, k: (i, j))], _store_epi,
    )[0]


def _mm_tn_slots(name, a, b, out_dtype):
    T, M = a.shape
    NS = b.shape[1] // 8
    tm, tk = _tile(M, 512), _rtile(T, 512)
    return _mm(
        name, a, b, _TN, (M // tm, 8, T // tk),
        pl.BlockSpec((tk, tm), lambda i, j, k: (k, i)), pl.BlockSpec((tk, NS), lambda i, j, k: (k, j)), (tm, NS),
        [], [], [SDS((8, M, NS), out_dtype)], [pl.BlockSpec((None, tm, NS), lambda i, j, k: (j, i, 0))], _store_epi,
    )[0]


def _mm_nt_slots(name, a, w8, out_dtype):
    T = a.shape[0]
    _, M, NS = w8.shape
    tm, tn = _rtile(T, 512), _tile(M, 1024)
    return _mm(
        name, a, w8, _NT, (T // tm, M // tn, 8),
        pl.BlockSpec((tm, NS), lambda i, j, k: (i, k)), pl.BlockSpec((None, tn, NS), lambda i, j, k: (k, j, 0)), (tm, tn),
        [], [], [SDS((T, M), out_dtype)], [pl.BlockSpec((tm, tn), lambda i, j, k: (i, j))], _store_epi,
    )[0]


def _colsum_kernel(name, fn, ins, in_cols, outs_elem, n_sum, C):
    T = ins[0].shape[0]
    tm = _rtile(T, 256)
    ne = len(outs_elem)

    def body(*refs):
        i = pl.program_id(0)
        iv = [r[...] for r in refs[:len(ins)]]
        res = fn(*iv)
        for o, r in zip(refs[len(ins):len(ins) + ne], res[:ne]):
            o[...] = r.astype(o.dtype)
        sums = [jnp.sum(r, axis=0, keepdims=True) for r in res[ne:]]

        @pl.when(i == 0)
        def _():
            for o, s in zip(refs[len(ins) + ne:], sums):
                o[...] = s

        @pl.when(i > 0)
        def _():
            for o, s in zip(refs[len(ins) + ne:], sums):
                o[...] += s

    in_specs = []
    for arr, off in zip(ins, in_cols):
        if off is None:
            in_specs.append(pl.BlockSpec((1, C), lambda i: (0, 0)))
        else:
            in_specs.append(pl.BlockSpec((tm, C), lambda i, off=off: (i, off)))
    row = pl.BlockSpec((tm, C), lambda i: (i, 0))
    vec = pl.BlockSpec((1, C), lambda i: (0, 0))
    return pl.pallas_call(
        body, grid=(T // tm,), in_specs=in_specs, out_specs=[row] * ne + [vec] * n_sum,
        out_shape=[SDS((T, C), dt) for dt in outs_elem] + [SDS((1, C), f32)] * n_sum,
        compiler_params=_cp(("arbitrary",)), name=name,
    )(*ins)


def _merge(gs, gg, a_s, a_g):
    return jax.nn.sigmoid(gs) * a_s + jax.nn.sigmoid(gg) * a_g


def _glu(yg, lp):
    return yg * jax.nn.sigmoid(lp)


_BIG = ("ffn1_w_gu", "ffn1_w_down", "w_in", "conv_w", "glu_w", "w_br_ssm", "w_br_gdn", "w_out", "ffn2_w_gu", "ffn2_w_down")
_PATHS = ("yx", "yx", "xy", "xy", "yx", "yx", "yx", "yx", "xy", "xy")
_SMALL = ("ln1_g", "ln1_b", "ssm_a_re", "ssm_a_im", "ssm_log_dt", "ssm_b_re", "ssm_b_im", "ssm_c_re", "ssm_c_im", "ssm_d",
          "glu_b", "gdn_a_log", "gdn_dt_bias", "gdn_norm_w", "ln2_g", "ln2_b", "ln3_g", "ln3_b")
_ORDER = ("ffn1_w_gu", "ffn1_w_down", "ln1_g", "ln1_b", "w_in", "conv_w", "ssm_a_re", "ssm_a_im", "ssm_log_dt", "ssm_b_re",
          "ssm_b_im", "ssm_c_re", "ssm_c_im", "ssm_d", "glu_w", "glu_b", "gdn_a_log", "gdn_dt_bias", "gdn_norm_w", "w_br_ssm",
          "w_br_gdn", "w_out", "ln2_g", "ln2_b", "ffn2_w_gu", "ffn2_w_down", "ln3_g", "ln3_b")


def _step(x, tgt, W, M, V):
    T, D = x.shape[1], x.shape[2]
    L = W["ffn1_w_gu"].shape[0]
    G, P = W["ssm_a_re"].shape[1:]
    H = W["ssm_b_re"].shape[3]
    SW = G * H
    NH = W["gdn_a_log"].shape[1]
    HD = W["gdn_norm_w"].shape[1]
    GW = NH * HD
    KC = W["conv_w"].shape[1]
    DS = D // 8
    alpha = (2.0 * L) ** 0.25
    o_b = SW + 4 * GW
    o_gs = o_b + 2 * NH
    IN = o_gs + 2 * D
    NM = IN - 2 * NH
    m_qkv, m_z, m_gs, m_gg = SW, SW + 3 * GW, SW + 4 * GW, SW + 4 * GW + D
    J = G // 8

    x0 = x[0]
    tg = tgt[0]

    def vec(name, l):
        return W[name][l:l + 1]

    saves, weights = [], []
    xc, xcb = x0, x0.astype(_MXU)
    def shards(l):
        return [W["ffn1_w_gu"][l].astype(_MXU), W["ffn1_w_down"][l].astype(_MXU), W["w_in"][l].astype(_MXU), W["conv_w"][l],
                W["glu_w"][l].astype(_MXU), W["w_br_ssm"][l].astype(_MXU), W["w_br_gdn"][l].astype(_MXU),
                W["w_out"][l].astype(_MXU), W["ffn2_w_gu"][l].astype(_MXU), W["ffn2_w_down"][l].astype(_MXU)]

    def dep(a, tok):
        return a if tok is None else a + tok[0:1, 0:1].astype(a.dtype)

    gathered = _all_gather("ag", shards(0), _PATHS)
    for l in range(L):
        nxt = _AsyncGather("agp", shards(l + 1), _PATHS) if l + 1 < L else None
        tok = nxt.start() if nxt else None
        wgu1, wd1, win8, cw8, wglu, wbs, wbg, wo, wgu2, wd2 = gathered
        wd1 = wd1.reshape(-1, D)
        wd2 = wd2.reshape(-1, D)
        wglu = wglu.reshape(SW, SW)
        wo = wo.reshape(D, D)
        win = jnp.transpose(win8, (1, 0, 2)).reshape(D, IN)
        wmain = jnp.concatenate([win[:, :o_b], win[:, o_gs:]], axis=1)
        wba = jnp.pad(win[:, o_b:o_gs], ((0, 0), (0, 128 - 2 * NH)))
        cw = jnp.transpose(cw8, (1, 0, 2)).reshape(KC, 3 * GW)
        wl = dict(wgu1=wgu1, wd1=wd1, wmain=wmain, wba=wba, cw=cw, wglu=wglu, wbs=wbs, wbg=wbg, wo=wo, wgu2=wgu2, wd2=wd2)
        weights.append(wl)
        sv = {}

        gate, up, hh = _ffn_up("ffn_up", xcb, wgu1)
        x1, x1b, xh1, r1 = _mm_ln("ffn_down_ln", hh, wd1, xc, dep(vec("ln1_g", l), tok), vec("ln1_b", l), alpha, 0.5)
        sv["f1"] = dict(xb=xcb, gate=gate, up=up, h=hh, xhat=xh1, rstd=r1)

        p = _mm_nn("mix_in", x1b, wmain, f32)
        pba = _mm_nn("mix_in_ba", x1b, wba, f32)
        b_re_t = jnp.transpose(W["ssm_b_re"][l], (2, 0, 1))
        b_im_t = jnp.transpose(W["ssm_b_im"][l], (2, 0, 1))
        zoh_in = (W["ssm_a_re"][l], W["ssm_a_im"][l], W["ssm_log_dt"][l][:, None], b_re_t, b_im_t)
        lbr, lbi, bbr_t, bbi_t = _zoh_fwd("zoh", *zoh_in)
        bblk_r = _blockdiag(jnp.transpose(bbr_t, (1, 0, 2)))
        bblk_i = _blockdiag(jnp.transpose(bbi_t, (1, 0, 2)))
        cblkT_r = _blockdiag(W["ssm_c_re"][l])
        cblkT_in = _blockdiag(-W["ssm_c_im"][l])
        lbr_f, lbi_f = lbr.reshape(1, G * P), lbi.reshape(1, G * P)
        bur, bui = _bd2("s5_bu", p, 0, bblk_r, bblk_i)
        if nxt:
            nxt.wait(bur)
            tok = nxt.start()
        sr, si = _s5_scan("s5_scan", bur, bui, dep(lbr_f, tok), lbi_f)
        dflat = W["ssm_d"][l].reshape(1, SW)

        def out_epi(acc, ex, outs):
            y_raw = acc + ex[1][...] * ex[0][...]
            yg = jax.nn.gelu(y_raw)
            outs[0][...] = y_raw
            outs[1][...] = yg
            outs[2][...] = yg.astype(outs[2].dtype)

        y_raw, yg, ygb = _bd_sum(
            "s5_out", sr, si, jnp.transpose(cblkT_r, (0, 2, 1)), jnp.transpose(cblkT_in, (0, 2, 1)), [p, dflat],
            lambda tm, nb: [pl.BlockSpec((tm, nb), lambda i, j: (i, j)), pl.BlockSpec((1, nb), lambda i, j: (0, j))],
            [SDS((T, SW), f32), SDS((T, SW), f32), SDS((T, SW), _MXU)], out_epi)

        tmg, tng, tkg = _rtile(T, 512), _tile(SW, 512), _tile(SW, 512)

        def glu_epi(acc, ex, outs):
            lp = acc + ex[1][...]
            outs[0][...] = lp
            outs[1][...] = _glu(ex[0][...], lp).astype(outs[1].dtype)

        lp, ysb = _mm(
            "s5_glu", ygb, wglu, _NN, (T // tmg, SW // tng, SW // tkg),
            pl.BlockSpec((tmg, tkg), lambda i, j, k: (i, k)), pl.BlockSpec((tkg, tng), lambda i, j, k: (k, j)), (tmg, tng),
            [yg, vec("glu_b", l)], [pl.BlockSpec((tmg, tng), lambda i, j, k: (i, j)), pl.BlockSpec((1, tng), lambda i, j, k: (0, j))],
            [SDS((T, SW), f32), SDS((T, SW), _MXU)], [pl.BlockSpec((tmg, tng), lambda i, j, k: (i, j))] * 2, glu_epi)

        qkv = _conv_fwd("gdn_conv", p, m_qkv, cw, 3 * GW)
        blt = jnp.transpose(pba[:, :NH])[:, :, None]
        aint = jnp.transpose(pba[:, NH:2 * NH])[:, :, None]
        alog = W["gdn_a_log"][l].reshape(NH, 1, 1)
        dtb = W["gdn_dt_bias"][l].reshape(NH, 1, 1)
        nw = vec("gdn_norm_w", l)
        og, ssave = _gdn_fwd("gdn", qkv, p, m_z, blt, aint, alog, dtb, nw, NH, HD)

        a_s = _mm(
            "br_ssm", ysb, wbs, _NN, (T // tmg, 8, SW // tkg),
            pl.BlockSpec((tmg, tkg), lambda i, j, k: (i, k)), pl.BlockSpec((None, tkg, DS), lambda i, j, k: (j, k, 0)), (tmg, DS),
            [], [], [SDS((T, D), f32)], [pl.BlockSpec((tmg, DS), lambda i, j, k: (i, j))], _store_epi)[0]
        tkd = _tile(GW, 512)
        gsb, ggb = m_gs // DS, m_gg // DS

        def merge_epi(acc, ex, outs):
            outs[0][...] = acc
            outs[1][...] = _merge(ex[1][...], ex[2][...], ex[0][...], acc).astype(outs[1].dtype)

        tile_ij = pl.BlockSpec((tmg, DS), lambda i, j, k: (i, j))
        a_g, merged = _mm(
            "br_gdn_merge", og, wbg, _NN, (T // tmg, 8, GW // tkd),
            pl.BlockSpec((tmg, tkd), lambda i, j, k: (i, k)), pl.BlockSpec((None, tkd, DS), lambda i, j, k: (j, k, 0)), (tmg, DS),
            [a_s, p, p], [tile_ij, pl.BlockSpec((tmg, DS), lambda i, j, k: (i, j + gsb)),
                          pl.BlockSpec((tmg, DS), lambda i, j, k: (i, j + ggb))],
            [SDS((T, D), f32), SDS((T, D), _MXU)], [tile_ij, tile_ij], merge_epi)
        if nxt:
            nxt.wait(merged)
            tok = nxt.start()
        x2, x2b, xh2, r2 = _mm_ln("mix_out_ln", merged, wo, x1, dep(vec("ln2_g", l), tok), vec("ln2_b", l), alpha, 1.0)
        sv["mx"] = dict(x1b=x1b, p=p, zoh_in=zoh_in, lbr_f=lbr_f, lbi_f=lbi_f, bblk_r=bblk_r, bblk_i=bblk_i, cblkT_r=cblkT_r,
                        cblkT_in=cblkT_in, sr=sr, si=si, dflat=dflat, y_raw=y_raw, yg=yg, ygb=ygb, lp=lp, ysb=ysb, qkv=qkv,
                        blt=blt, aint=aint, alog=alog, dtb=dtb, nw=nw, og=og, ssave=ssave, a_s=a_s, a_g=a_g, merged=merged,
                        xhat=xh2, rstd=r2)

        gate2, up2, hh2 = _ffn_up("ffn_up", x2b, wgu2)
        x3, x3b, xh3, r3 = _mm_ln("ffn_down_ln", hh2, wd2, x2, vec("ln3_g", l), vec("ln3_b", l), alpha, 0.5)
        sv["f2"] = dict(xb=x2b, gate=gate2, up=up2, h=hh2, xhat=xh3, rstd=r3)
        saves.append(sv)
        xc, xcb = x3, x3b
        if nxt:
            nxt.wait(x3)
            gathered = nxt.result()

    dy, loss_part = _loss_head("loss_head", xc, tg)
    loss = lax.psum(loss_part[0, 0], ("x", "y", "c"))

    big_out = {n: [lax.empty(W[n].shape, f32) for _ in range(4)] for n in _BIG}
    small_g = {n: [None] * L for n in _SMALL}
    pend = None
    for l in reversed(range(L)):
        sv, wl = saves[l], weights[l]
        mx = sv["mx"]
        p = mx["p"]
        tok = pend.start() if pend else None
        dx2, dwgu2, dwd2, dg3, db3 = _ffn_bwd("ffn_b", dy, sv["f2"], wl["wgu2"], wl["wd2"], dep(vec("ln3_g", l), tok), alpha)
        small_g["ln3_g"][l], small_g["ln3_b"][l] = dg3[0], db3[0]

        if pend:
            pend.wait(dx2)
            tok = pend.start()
        dz2, dmixb, dg2, db2 = _ln_bwd("mix_lnb", dx2, mx["xhat"], mx["rstd"], dep(vec("ln2_g", l), tok), 1.0)
        small_g["ln2_g"][l], small_g["ln2_b"][l] = dg2[0], db2[0]
        tmg, tkd = _rtile(T, 512), _tile(D, 512)
        gsb, ggb = m_gs // DS, m_gg // DS

        def dmerge_epi(acc, ex, outs):
            _, vjp = jax.vjp(_merge, ex[0][...], ex[1][...], ex[2][...], ex[3][...])
            dgs, dgg, das, dag = vjp(acc)
            outs[0][...] = das.astype(outs[0].dtype)
            outs[1][...] = dag.astype(outs[1].dtype)
            outs[2][...] = dgs.astype(outs[2].dtype)
            outs[3][...] = dgg.astype(outs[3].dtype)

        tile_ij = pl.BlockSpec((tmg, DS), lambda i, j, k: (i, j))
        das, dag, dgs, dgg = _mm(
            "mix_dmerge", dmixb, wl["wo"], _NT, (T // tmg, 8, D // tkd),
            pl.BlockSpec((tmg, tkd), lambda i, j, k: (i, k)), pl.BlockSpec((DS, tkd), lambda i, j, k: (j, k)), (tmg, DS),
            [p, p, mx["a_s"], mx["a_g"]],
            [pl.BlockSpec((tmg, DS), lambda i, j, k: (i, j + gsb)), pl.BlockSpec((tmg, DS), lambda i, j, k: (i, j + ggb)),
             tile_ij, tile_ij],
            [SDS((T, D), _MXU)] * 4, [tile_ij] * 4, dmerge_epi)
        dwo = _mm_tn("mix_dwo", mx["merged"], dmixb, _GDT)
        dys = _mm_nt_slots("br_ssm_dx", das, wl["wbs"], f32)
        dog = _mm_nt_slots("br_gdn_dx", dag, wl["wbg"], f32)
        dwbs = _mm_tn_slots("br_ssm_dw", mx["ysb"], das, _GDT)
        dwbg = _mm_tn_slots("br_gdn_dw", mx["og"], dag, _GDT)

        def glu_b_fn(dys_t, yg_t, lp_t):
            _, vjp = jax.vjp(_glu, yg_t, lp_t)
            dyg1, dlp = vjp(dys_t)
            return dyg1, dlp, dlp

        dyg1, dlpb, dglub = _colsum_kernel("s5_glu_b", glu_b_fn, [dys, mx["yg"], mx["lp"]], [0, 0, 0], [f32, _MXU], 1, SW)
        small_g["glu_b"][l] = dglub[0]
        dwglu = _mm_tn("s5_dwglu", mx["ygb"], dlpb, _GDT)
        tng, tkg = _tile(SW, 512), _tile(SW, 512)

        def dyraw_epi(acc, ex, outs):
            _, vjp = jax.vjp(jax.nn.gelu, ex[1][...])
            (d,) = vjp(ex[0][...] + acc)
            outs[0][...] = d

        t_ij = pl.BlockSpec((tmg, tng), lambda i, j, k: (i, j))
        (dyraw,) = _mm(
            "s5_dyraw", dlpb, wl["wglu"], _NT, (T // tmg, SW // tng, SW // tkg),
            pl.BlockSpec((tmg, tkg), lambda i, j, k: (i, k)), pl.BlockSpec((tng, tkg), lambda i, j, k: (j, k)), (tmg, tng),
            [dyg1, mx["y_raw"]], [t_ij, t_ij], [SDS((T, SW), f32)], [t_ij], dyraw_epi)

        def dd_fn(dyr, u_t, d_t):
            return d_t * dyr, dyr * u_t

        dud, dd = _colsum_kernel("s5_dd", dd_fn, [dyraw, p, mx["dflat"]], [0, 0, None], [f32], 1, SW)
        small_g["ssm_d"][l] = dd.reshape(G, H)
        dsr, dsi = _bd2("s5_ds", dyraw, 0, mx["cblkT_r"], mx["cblkT_in"])
        dcb_r, dcb_i = _bdT2("s5_dc", mx["sr"], mx["si"], 0, dyraw, dyraw, 0, 8 * P, 8 * H, J)
        small_g["ssm_c_re"][l] = _blockdiag_extract(jnp.transpose(dcb_r, (0, 2, 1)), H, P)
        small_g["ssm_c_im"][l] = -_blockdiag_extract(jnp.transpose(dcb_i, (0, 2, 1)), H, P)
        ar, ai, dlr, dli = _s5_scan_bwd("s5_scan_b", dsr, dsi, mx["sr"], mx["si"], mx["lbr_f"], mx["lbi_f"])

        def du_epi(acc, ex, outs):
            outs[0][...] = (acc + ex[0][...]).astype(outs[0].dtype)

        (du,) = _bd_sum(
            "s5_du", ar, ai, jnp.transpose(mx["bblk_r"], (0, 2, 1)), jnp.transpose(mx["bblk_i"], (0, 2, 1)), [dud],
            lambda tm, nb: [pl.BlockSpec((tm, nb), lambda i, j: (i, j))], [SDS((T, SW), _MXU)], du_epi)
        dbb_r, dbb_i = _bdT2("s5_db", p, p, 0, ar, ai, 0, 8 * H, 8 * P, J)
        dbbr_t = jnp.transpose(_blockdiag_extract(dbb_r, H, P), (1, 0, 2))
        dbbi_t = jnp.transpose(_blockdiag_extract(dbb_i, H, P), (1, 0, 2))
        da_re, da_im, dlog_dt, dbre_t, dbim_t = _zoh_bwd("zoh_b", *mx["zoh_in"], dlr.reshape(G, P), dli.reshape(G, P),
                                                         dbbr_t, dbbi_t)
        small_g["ssm_a_re"][l], small_g["ssm_a_im"][l], small_g["ssm_log_dt"][l] = da_re, da_im, dlog_dt[:, 0]
        small_g["ssm_b_re"][l] = jnp.transpose(dbre_t, (1, 2, 0))
        small_g["ssm_b_im"][l] = jnp.transpose(dbim_t, (1, 2, 0))

        if pend:
            pend.wait(du)
            tok = pend.start()
        dqkv3, dzb, dbl, dain, dal, ddtb, dnw = _gdn_bwd("gdn_b", mx["qkv"], p, m_z, mx["blt"], mx["aint"], mx["alog"],
                                                         mx["dtb"], dep(mx["nw"], tok), mx["ssave"], dog, NH, HD)
        small_g["gdn_a_log"][l], small_g["gdn_dt_bias"][l], small_g["gdn_norm_w"][l] = dal[:, 0, 0], ddtb[:, 0, 0], dnw[0]
        dqkv_pre, dcw = _conv_bwd("gdn_conv_b", p, m_qkv, wl["cw"], dqkv3)

        dpm = jnp.concatenate([du, dqkv_pre, dzb, dgs, dgg], axis=1)
        dpba = jnp.concatenate([jnp.transpose(dbl[:, :, 0]), jnp.transpose(dain[:, :, 0]),
                                jnp.zeros((T, 128 - 2 * NH), f32)], axis=1).astype(_MXU)
        tnd, tkm = _tile(D, 1024), _tile(NM, 512)
        t_ba = _mm(
            "mix_dx_ba", dpba, wl["wba"], _NT, (T // tmg, D // tnd, 1),
            pl.BlockSpec((tmg, 128), lambda i, j, k: (i, 0)), pl.BlockSpec((tnd, 128), lambda i, j, k: (j, 0)), (tmg, tnd),
            [], [], [SDS((T, D), f32)], [pl.BlockSpec((tmg, tnd), lambda i, j, k: (i, j))], _store_epi)[0]

        def dx1_epi(acc, ex, outs):
            outs[0][...] = alpha * ex[0][...] + ex[1][...] + acc

        t_d = pl.BlockSpec((tmg, tnd), lambda i, j, k: (i, j))
        (dx1,) = _mm(
            "mix_dx", dpm, wl["wmain"], _NT, (T // tmg, D // tnd, NM // tkm),
            pl.BlockSpec((tmg, tkm), lambda i, j, k: (i, k)), pl.BlockSpec((tnd, tkm), lambda i, j, k: (j, k)), (tmg, tnd),
            [dz2, t_ba], [t_d, t_d], [SDS((T, D), f32)], [t_d], dx1_epi)
        tnm = _tile(NM, 1024)
        tkt = _tile(T, 1024)
        dwmain = _mm(
            "mix_dw", jnp.transpose(mx["x1b"]), dpm, _NN, (D // tkd, NM // tnm, T // tkt),
            pl.BlockSpec((tkd, tkt), lambda i, j, k: (i, k)), pl.BlockSpec((tkt, tnm), lambda i, j, k: (k, j)), (tkd, tnm),
            [], [], [SDS((D, NM), _GDT)], [pl.BlockSpec((tkd, tnm), lambda i, j, k: (i, j))], _store_epi)[0]
        dwba = _mm_tn("mix_dw_ba", mx["x1b"], dpba, _GDT)
        dwin = jnp.concatenate([dwmain[:, :o_b], dwba[:, :2 * NH], dwmain[:, o_b:]], axis=1)
        dwin8 = jnp.transpose(dwin.reshape(D, 8, IN // 8), (1, 0, 2))
        dcw8 = jnp.transpose(dcw.reshape(KC, 8, 3 * GW // 8), (1, 0, 2))

        dx0, dwgu1, dwd1, dg1, db1 = _ffn_bwd("ffn_b", dx1, sv["f1"], wl["wgu1"], wl["wd1"], vec("ln1_g", l), alpha)
        small_g["ln1_g"][l], small_g["ln1_b"][l] = dg1[0], db1[0]
        dy = dx0

        parts = [dwgu1, dwd1.reshape(8, -1, D), dwin8, dcw8, dwglu.reshape(8, SW // 8, SW), dwbs, dwbg,
                 dwo.reshape(8, DS, D), dwgu2, dwd2.reshape(8, -1, D)]
        if pend:
            pend.wait(dx0)
            for n, (full, recv, last) in zip(_BIG, pend.out):
                big_out[n] = _adamw_big("adamw_" + n, full, recv, _coord(last), W[n], M[n], V[n], l + 1, big_out[n])
        if l > 0:
            pend = _AsyncReduceScatter("rsp", parts, _PATHS)
        else:
            for n, (full, recv, last) in zip(_BIG, _reduce_scatter("rs", parts, _PATHS)):
                big_out[n] = _adamw_big("adamw_" + n, full, recv, _coord(last), W[n], M[n], V[n], l, big_out[n])

    seg = 8 * 128

    def padded(n):
        return -(-n // seg) * seg

    def pack(arrs):
        flat = jnp.concatenate([jnp.pad(a.reshape(-1), (0, padded(a.size) - a.size)) for a in arrs])
        n = flat.shape[0]
        rows = -(-n // (128 * 512)) * 512
        return jnp.pad(flat, (0, rows * 128 - n)).reshape(rows, 128)

    gs_full = [jnp.stack(small_g[n]).reshape(W[n].shape) for n in _SMALL]
    gpack = pack(gs_full)
    (gall,) = _all_gather("ag_small", [gpack], ["yx"])
    sg, sd, sm, sv_ = _adamw_small("adamw_small", gall, pack([W[n] for n in _SMALL]), pack([M[n] for n in _SMALL]),
                                   pack([V[n] for n in _SMALL]))

    def unpack(packed):
        out, row = {}, 0
        for n in _SMALL:
            sz = math.prod(W[n].shape)
            rows = padded(sz) // 128
            out[n] = packed[row:row + rows].reshape(-1)[:sz].reshape(W[n].shape)
            row += rows
        return out

    res = [unpack(a) for a in (sg, sd, sm, sv_)]
    for n in _BIG:
        for i in range(4):
            res[i][n] = big_out[n][i]
    outs = [loss, dy[None]]
    for i in range(4):
        outs += [res[i][n] for n in _ORDER]
    return tuple(outs)


def kernel(x, ffn1_w_gu, ffn1_w_down, ln1_g, ln1_b, w_in, conv_w, ssm_a_re, ssm_a_im, ssm_log_dt, ssm_b_re, ssm_b_im, ssm_c_re, ssm_c_im, ssm_d, glu_w, glu_b, gdn_a_log, gdn_dt_bias, gdn_norm_w, w_br_ssm, w_br_gdn, w_out, ln2_g, ln2_b, ffn2_w_gu, ffn2_w_down, ln3_g, ln3_b, loss_target, m_ffn1_w_gu, m_ffn1_w_down, m_ln1_g, m_ln1_b, m_w_in, m_conv_w, m_ssm_a_re, m_ssm_a_im, m_ssm_log_dt, m_ssm_b_re, m_ssm_b_im, m_ssm_c_re, m_ssm_c_im, m_ssm_d, m_glu_w, m_glu_b, m_gdn_a_log, m_gdn_dt_bias, m_gdn_norm_w, m_w_br_ssm, m_w_br_gdn, m_w_out, m_ln2_g, m_ln2_b, m_ffn2_w_gu, m_ffn2_w_down, m_ln3_g, m_ln3_b, v_ffn1_w_gu, v_ffn1_w_down, v_ln1_g, v_ln1_b, v_w_in, v_conv_w, v_ssm_a_re, v_ssm_a_im, v_ssm_log_dt, v_ssm_b_re, v_ssm_b_im, v_ssm_c_re, v_ssm_c_im, v_ssm_d, v_glu_w, v_glu_b, v_gdn_a_log, v_gdn_dt_bias, v_gdn_norm_w, v_w_br_ssm, v_w_br_gdn, v_w_out, v_ln2_g, v_ln2_b, v_ffn2_w_gu, v_ffn2_w_down, v_ln3_g, v_ln3_b):
    given = dict(locals())
    W = {n: given[n] for n in _ORDER}
    M = {n: given["m_" + n] for n in _ORDER}
    V = {n: given["v_" + n] for n in _ORDER}
    return _step(x, loss_target, W, M, V)
```

```python
import functools
import math

import jax
import jax.numpy as jnp
from jax import lax
from jax.experimental import pallas as pl
from jax.experimental.pallas import tpu as pltpu

f32 = jnp.float32
_MXU = jnp.bfloat16
_GDT = jnp.bfloat16
_HP = lax.Precision.HIGHEST
_VMEM_LIMIT = 56 * 1024 * 1024
_MESH_T = pl.DeviceIdType.MESH

LN_EPS = 1e-5
RMS_EPS = 1e-6
L2_EPS = 1e-6
CHUNK = 64
ADAM_LR = 0.001
ADAM_B1 = 0.9
ADAM_B2 = 0.999
ADAM_EPS = 1e-08
ADAM_WD = 0.01
ADAM_STEP = 10

_NN = (((1,), (0,)), ((), ()))
_NT = (((1,), (1,)), ((), ()))
_TN = (((0,), (0,)), ((), ()))

SDS = jax.ShapeDtypeStruct


def _cp(sem):
    return pltpu.CompilerParams(dimension_semantics=sem, vmem_limit_bytes=_VMEM_LIMIT)


def _tile(n, pref):
    if n <= pref:
        return n
    t = (pref // 128) * 128
    while t >= 128:
        if n % t == 0:
            return t
        t -= 128
    return n


def _rtile(n, pref):
    if n <= pref:
        return n
    t = (pref // 16) * 16
    while t >= 16:
        if n % t == 0:
            return t
        t -= 16
    return n


def _mm(name, a, b, dims, grid, a_spec, b_spec, acc_shape, extras, extra_specs, out_shape, out_specs, epilogue):
    nk = grid[2]
    ne = len(extras)
    no = len(out_shape)

    def body(*refs):
        a_ref, b_ref = refs[0], refs[1]
        ex = refs[2:2 + ne]
        outs = refs[2 + ne:2 + ne + no]
        acc = refs[-1]
        k = pl.program_id(2)
        part = lax.dot_general(a_ref[...].astype(_MXU), b_ref[...].astype(_MXU), dims, preferred_element_type=f32)

        @pl.when(k == 0)
        def _():
            acc[...] = part

        @pl.when(k > 0)
        def _():
            acc[...] += part

        @pl.when(k == nk - 1)
        def _():
            epilogue(acc[...], ex, outs)

    return pl.pallas_call(
        body, grid=grid, in_specs=[a_spec, b_spec, *extra_specs], out_specs=list(out_specs), out_shape=list(out_shape),
        scratch_shapes=[pltpu.VMEM(acc_shape, f32)], compiler_params=_cp(("parallel", "parallel", "arbitrary")), name=name,
    )(a, b, *extras)


def _store_epi(acc, ex, outs):
    for o in outs:
        o[...] = acc.astype(o.dtype)


def _ln_epilogue(alpha, c):
    def epi(acc, ex, outs):
        x_ref, g_ref, b_ref = ex
        y_ref, yb_ref, xh_ref, r_ref = outs
        z = alpha * x_ref[...] + c * acc
        mu = jnp.mean(z, axis=-1, keepdims=True)
        zc = z - mu
        var = jnp.mean(zc * zc, axis=-1, keepdims=True)
        r = lax.rsqrt(var + LN_EPS)
        xh = zc * r
        y = xh * g_ref[...] + b_ref[...]
        y_ref[...] = y
        yb_ref[...] = y.astype(yb_ref.dtype)
        xh_ref[...] = xh
        r_ref[...] = r
    return epi


def _mm_ln(name, a, w, x, g, b, alpha, c):
    T, K = a.shape
    D = w.shape[1]
    tm, tk = _rtile(T, 512), _tile(K, 512)
    row = pl.BlockSpec((tm, D), lambda i, j, k: (i, 0))
    vec = pl.BlockSpec((1, D), lambda i, j, k: (0, 0))
    return _mm(
        name, a, w, _NN, (T // tm, 1, K // tk),
        pl.BlockSpec((tm, tk), lambda i, j, k: (i, k)), pl.BlockSpec((tk, D), lambda i, j, k: (k, 0)), (tm, D),
        [x, g, b], [row, vec, vec],
        [SDS((T, D), f32), SDS((T, D), _MXU), SDS((T, D), f32), SDS((T, 1), f32)],
        [row, row, row, pl.BlockSpec((tm, 1), lambda i, j, k: (i, 0))],
        _ln_epilogue(alpha, c),
    )


def _ln_bwd(name, dy, xhat, rstd, g, c):
    T, D = dy.shape
    tm = _rtile(T, 256)

    def body(dy_ref, xh_ref, r_ref, g_ref, dz_ref, df_ref, dg_ref, db_ref):
        i = pl.program_id(0)
        dyv = dy_ref[...]
        xh = xh_ref[...]
        dxh = dyv * g_ref[...]
        m1 = jnp.mean(dxh, axis=-1, keepdims=True)
        m2 = jnp.mean(dxh * xh, axis=-1, keepdims=True)
        dz = r_ref[...] * (dxh - m1 - xh * m2)
        dz_ref[...] = dz
        df_ref[...] = (c * dz).astype(df_ref.dtype)
        pg = jnp.sum(dyv * xh, axis=0, keepdims=True)
        pb = jnp.sum(dyv, axis=0, keepdims=True)

        @pl.when(i == 0)
        def _():
            dg_ref[...] = pg
            db_ref[...] = pb

        @pl.when(i > 0)
        def _():
            dg_ref[...] += pg
            db_ref[...] += pb

    row = pl.BlockSpec((tm, D), lambda i: (i, 0))
    vec = pl.BlockSpec((1, D), lambda i: (0, 0))
    return pl.pallas_call(
        body, grid=(T // tm,), in_specs=[row, row, pl.BlockSpec((tm, 1), lambda i: (i, 0)), vec],
        out_specs=[row, row, vec, vec],
        out_shape=[SDS((T, D), f32), SDS((T, D), _MXU), SDS((1, D), f32), SDS((1, D), f32)],
        compiler_params=_cp(("arbitrary",)), name=name,
    )(dy, xhat, rstd, g)


def _swiglu(g, u):
    return jax.nn.silu(g) * u


def _ffn_up(name, xb, wgu):
    T, D = xb.shape
    FS = wgu.shape[2]
    F = 4 * FS
    tm = _rtile(T, 256)

    def body(x_ref, wg_ref, wu_ref, g_ref, u_ref, h_ref):
        xv = x_ref[...]
        g = jnp.dot(xv, wg_ref[...], preferred_element_type=f32)
        u = jnp.dot(xv, wu_ref[...], preferred_element_type=f32)
        g_ref[...] = g.astype(g_ref.dtype)
        u_ref[...] = u.astype(u_ref.dtype)
        h_ref[...] = _swiglu(g, u).astype(h_ref.dtype)

    out = pl.BlockSpec((tm, FS), lambda j, i: (i, j))
    return pl.pallas_call(
        body, grid=(4, T // tm),
        in_specs=[pl.BlockSpec((tm, D), lambda j, i: (i, 0)),
                  pl.BlockSpec((None, D, FS), lambda j, i: (j, 0, 0)),
                  pl.BlockSpec((None, D, FS), lambda j, i: (j + 4, 0, 0))],
        out_specs=[out, out, out],
        out_shape=[SDS((T, F), _MXU), SDS((T, F), _MXU), SDS((T, F), _MXU)],
        compiler_params=_cp(("parallel", "arbitrary")), name=name,
    )(xb, wgu, wgu)


def _ffn_bwd(pfx, dy, sv, wgu, wd, g_ln, alpha):
    T, D = dy.shape
    FS = wgu.shape[2]
    F = 4 * FS
    dz, dfb, dg, db = _ln_bwd(pfx + "_lnb", dy, sv["xhat"], sv["rstd"], g_ln, 0.5)

    tm, tn, tk = _rtile(T, 1024), _tile(F, 512), _tile(D, 2048)

    def epi(acc, ex, outs):
        g_ref, u_ref = ex
        _, vjp = jax.vjp(_swiglu, g_ref[...].astype(f32), u_ref[...].astype(f32))
        dgate, dup = vjp(acc)
        outs[0][0] = dgate.astype(outs[0].dtype)
        outs[0][1] = dup.astype(outs[0].dtype)

    gu = pl.BlockSpec((tm, tn), lambda i, j, k: (i, j))
    (dgu,) = _mm(
        pfx + "_dh", dfb, wd, _NT, (T // tm, F // tn, D // tk),
        pl.BlockSpec((tm, tk), lambda i, j, k: (i, k)), pl.BlockSpec((tn, tk), lambda i, j, k: (j, k)), (tm, tn),
        [sv["gate"], sv["up"]], [gu, gu],
        [SDS((2, T, F), _MXU)], [pl.BlockSpec((2, tm, tn), lambda i, j, k: (0, i, j))], epi,
    )

    tm2, tk2 = _tile(F, 512), _rtile(T, 512)
    (dwd,) = _mm(
        pfx + "_dwd", sv["h"], dfb, _TN, (F // tm2, 1, T // tk2),
        pl.BlockSpec((tk2, tm2), lambda i, j, k: (k, i)), pl.BlockSpec((tk2, D), lambda i, j, k: (k, 0)), (tm2, D),
        [], [], [SDS((F, D), _GDT)], [pl.BlockSpec((tm2, D), lambda i, j, k: (i, 0))], _store_epi,
    )

    tn3 = _tile(D, 1024)

    def epi3(acc, ex, outs):
        outs[0][...] = alpha * ex[0][...] + acc

    (dx,) = _mm(
        pfx + "_dx", dgu, wgu, _NT, (T // tm, D // tn3, 8),
        pl.BlockSpec((None, tm, FS), lambda i, j, k: (k // 4, i, k % 4)),
        pl.BlockSpec((None, tn3, FS), lambda i, j, k: (k, j, 0)), (tm, tn3),
        [dz], [pl.BlockSpec((tm, tn3), lambda i, j, k: (i, j))],
        [SDS((T, D), f32)], [pl.BlockSpec((tm, tn3), lambda i, j, k: (i, j))], epi3,
    )

    tm4, tk4 = _rtile(D, 512), _tile(T, 1024)
    (dwgu,) = _mm(
        pfx + "_dwgu", jnp.transpose(sv["xb"]), dgu, _NN, (D // tm4, 8, T // tk4),
        pl.BlockSpec((tm4, tk4), lambda i, j, k: (i, k)),
        pl.BlockSpec((None, tk4, FS), lambda i, j, k: (j // 4, k, j % 4)), (tm4, FS),
        [], [], [SDS((8, D, FS), _GDT)], [pl.BlockSpec((None, tm4, FS), lambda i, j, k: (j, i, 0))], _store_epi,
    )
    return dx, dwgu, dwd, dg, db


def _zoh(a_re, a_im, log_dt, b_re_t, b_im_t):
    dt = jnp.exp(log_dt)
    mag = jnp.exp(a_re * dt)
    lr_, li_ = mag * jnp.cos(a_im * dt), mag * jnp.sin(a_im * dt)
    den = a_re * a_re + a_im * a_im
    pr, pi = lr_ - 1.0, li_
    qr, qi = a_re / den, -a_im / den
    zr, zi = pr * qr - pi * qi, pr * qi + pi * qr
    bbr = zr[None] * b_re_t - zi[None] * b_im_t
    bbi = zr[None] * b_im_t + zi[None] * b_re_t
    return lr_, li_, bbr, bbi


def _zoh_fwd(name, a_re, a_im, log_dt, b_re_t, b_im_t):
    G, P = a_re.shape
    H = b_re_t.shape[0]

    def body(ar, ai, ld, br, bi, o1, o2, o3, o4):
        r = _zoh(ar[...], ai[...], ld[...], br[...], bi[...])
        o1[...], o2[...], o3[...], o4[...] = r

    return pl.pallas_call(
        body, out_shape=[SDS((G, P), f32), SDS((G, P), f32), SDS((H, G, P), f32), SDS((H, G, P), f32)], name=name,
    )(a_re, a_im, log_dt, b_re_t, b_im_t)


def _zoh_bwd(name, a_re, a_im, log_dt, b_re_t, b_im_t, dlr, dli, dbbr, dbbi):
    G, P = a_re.shape
    H = b_re_t.shape[0]

    def body(ar, ai, ld, br, bi, g1, g2, g3, g4, o1, o2, o3, o4, o5):
        _, vjp = jax.vjp(_zoh, ar[...], ai[...], ld[...], br[...], bi[...])
        r = vjp((g1[...], g2[...], g3[...], g4[...]))
        o1[...], o2[...], o3[...], o4[...], o5[...] = r

    return pl.pallas_call(
        body, out_shape=[SDS((G, P), f32), SDS((G, P), f32), SDS((G, 1), f32), SDS((H, G, P), f32), SDS((H, G, P), f32)],
        name=name,
    )(a_re, a_im, log_dt, b_re_t, b_im_t, dlr, dli, dbbr, dbbi)


def _blockdiag(m):
    G, A, B = m.shape
    eye = jnp.eye(8, dtype=bool)
    m4 = m.reshape(G // 8, 8, A, B)
    out = jnp.where(eye[None, :, None, :, None], m4[:, :, :, None, :], jnp.zeros((), m.dtype))
    return out.reshape(G // 8, 8 * A, 8 * B)


def _blockdiag_extract(mb, A, B):
    J = mb.shape[0]
    m5 = mb.reshape(J, 8, A, 8, B)
    d = jnp.stack([m5[:, i, :, i, :] for i in range(8)], axis=1)
    return d.reshape(J * 8, A, B)


def _bd2(name, a, a_col0, b1, b2, out_dtype=f32):
    T = a.shape[0]
    J, KA, NB = b1.shape
    tm = _rtile(T, 512)

    def body(a_ref, b1_ref, b2_ref, o1, o2):
        av = a_ref[...].astype(_MXU)
        o1[...] = jnp.dot(av, b1_ref[...].astype(_MXU), preferred_element_type=f32).astype(o1.dtype)
        o2[...] = jnp.dot(av, b2_ref[...].astype(_MXU), preferred_element_type=f32).astype(o2.dtype)

    bs = pl.BlockSpec((None, KA, NB), lambda i, j: (j, 0, 0))
    os_ = pl.BlockSpec((tm, NB), lambda i, j: (i, j))
    return pl.pallas_call(
        body, grid=(T // tm, J), in_specs=[pl.BlockSpec((tm, KA), lambda i, j: (i, j + a_col0)), bs, bs],
        out_specs=[os_, os_], out_shape=[SDS((T, J * NB), out_dtype)] * 2,
        compiler_params=_cp(("parallel", "parallel")), name=name,
    )(a, b1, b2)


def _bd_sum(name, a1, a2, b1, b2, extras, extra_specs_fn, out_shape, epilogue):
    T = a1.shape[0]
    J, KA, NB = b1.shape
    tm = _rtile(T, 512)
    ne = len(extras)

    def body(*refs):
        a1_ref, a2_ref, b1_ref, b2_ref = refs[:4]
        ex = refs[4:4 + ne]
        outs = refs[4 + ne:]
        acc = jnp.dot(a1_ref[...].astype(_MXU), b1_ref[...].astype(_MXU), preferred_element_type=f32)
        acc = acc + jnp.dot(a2_ref[...].astype(_MXU), b2_ref[...].astype(_MXU), preferred_element_type=f32)
        epilogue(acc, ex, outs)

    as_ = pl.BlockSpec((tm, KA), lambda i, j: (i, j))
    bs = pl.BlockSpec((None, KA, NB), lambda i, j: (j, 0, 0))
    os_ = pl.BlockSpec((tm, NB), lambda i, j: (i, j))
    return pl.pallas_call(
        body, grid=(T // tm, J), in_specs=[as_, as_, bs, bs, *extra_specs_fn(tm, NB)],
        out_specs=[os_] * len(out_shape), out_shape=list(out_shape),
        compiler_params=_cp(("parallel", "parallel")), name=name,
    )(a1, a2, b1, b2, *extras)


def _bdT2(name, a1, a2, a_col0, b1, b2, b_col0, KA, NB, J):
    T = a1.shape[0]
    tk = _rtile(T, 512)

    def body(a1_ref, a2_ref, b1_ref, b2_ref, o1, o2):
        k = pl.program_id(1)
        p1 = lax.dot_general(a1_ref[...].astype(_MXU), b1_ref[...].astype(_MXU), _TN, preferred_element_type=f32)
        p2 = lax.dot_general(a2_ref[...].astype(_MXU), b2_ref[...].astype(_MXU), _TN, preferred_element_type=f32)

        @pl.when(k == 0)
        def _():
            o1[...] = p1
            o2[...] = p2

        @pl.when(k > 0)
        def _():
            o1[...] += p1
            o2[...] += p2

    as_ = pl.BlockSpec((tk, KA), lambda j, k: (k, j + a_col0))
    bs = pl.BlockSpec((tk, NB), lambda j, k: (k, j + b_col0))
    os_ = pl.BlockSpec((None, KA, NB), lambda j, k: (j, 0, 0))
    return pl.pallas_call(
        body, grid=(J, T // tk), in_specs=[as_, as_, bs, bs], out_specs=[os_, os_],
        out_shape=[SDS((J, KA, NB), f32)] * 2, compiler_params=_cp(("parallel", "arbitrary")), name=name,
    )(a1, a2, b1, b2)


_RB = 8


def _cmul(ar, ai, br, bi):
    return ar * br - ai * bi, ar * bi + ai * br


def _lam_powers(lr_v, li_v, cb):
    pw = {1: (lr_v, li_v)}
    for k in range(2, _RB + 1):
        pw[k] = _cmul(*pw[k - 1], lr_v, li_v)
    return pw


def _row_powers(pw, row, cb, reverse):
    outr = jnp.zeros((_RB, cb), f32)
    outi = jnp.zeros((_RB, cb), f32)
    for r in range(_RB):
        k = _RB - r if reverse else r + 1
        outr = jnp.where(row == r, pw[k][0], outr)
        outi = jnp.where(row == r, pw[k][1], outi)
    return outr, outi


def _tile_scan(xr, xi, pw, row, reverse):
    for k in (1, 2, 4):
        if reverse:
            keep = row < _RB - k
            shr, shi = pltpu.roll(xr, _RB - k, 0), pltpu.roll(xi, _RB - k, 0)
        else:
            keep = row >= k
            shr, shi = pltpu.roll(xr, k, 0), pltpu.roll(xi, k, 0)
        shr, shi = jnp.where(keep, shr, 0.0), jnp.where(keep, shi, 0.0)
        mr, mi = pw[k]
        xr, xi = xr + (mr * shr - mi * shi), xi + (mr * shi + mi * shr)
    return xr, xi


def _s5_scan(name, bur, bui, lr_, li_):
    T, N = bur.shape
    cb = _tile(N, 512)

    def body(br_ref, bi_ref, lr_ref, li_ref, sr_ref, si_ref):
        pw = _lam_powers(lr_ref[...], li_ref[...], cb)
        row = lax.broadcasted_iota(jnp.int32, (_RB, cb), 0)
        cr, ci = _row_powers(pw, row, cb, False)

        def step(n, carry):
            pr, pi = carry
            t0 = pl.multiple_of(n * _RB, _RB)
            xr, xi = _tile_scan(br_ref[pl.ds(t0, _RB), :], bi_ref[pl.ds(t0, _RB), :], pw, row, False)
            xr, xi = xr + (cr * pr - ci * pi), xi + (cr * pi + ci * pr)
            sr_ref[pl.ds(t0, _RB), :] = xr
            si_ref[pl.ds(t0, _RB), :] = xi
            return xr[_RB - 1:_RB, :], xi[_RB - 1:_RB, :]

        z = jnp.zeros((1, cb), f32)
        lax.fori_loop(0, T // _RB, step, (z, z))

    col = pl.BlockSpec((T, cb), lambda j: (0, j))
    vec = pl.BlockSpec((1, cb), lambda j: (0, j))
    return pl.pallas_call(
        body, grid=(N // cb,), in_specs=[col, col, vec, vec], out_specs=[col, col],
        out_shape=[SDS((T, N), f32)] * 2, compiler_params=_cp(("parallel",)), name=name,
    )(bur, bui, lr_, li_)


def _s5_scan_bwd(name, dsr, dsi, sr, si, lr_, li_):
    T, N = dsr.shape
    cb = _tile(N, 256)

    def body(dr_ref, di_ref, sr_ref, si_ref, lr_ref, li_ref, ar_ref, ai_ref, glr_ref, gli_ref):
        pw = _lam_powers(lr_ref[...], -li_ref[...], cb)
        row = lax.broadcasted_iota(jnp.int32, (_RB, cb), 0)
        cr, ci = _row_powers(pw, row, cb, True)
        NT = T // _RB

        def tile(t0, nxt, prev_last):
            xr, xi = _tile_scan(dr_ref[pl.ds(t0, _RB), :], di_ref[pl.ds(t0, _RB), :], pw, row, True)
            xr, xi = xr + (cr * nxt[0] - ci * nxt[1]), xi + (cr * nxt[1] + ci * nxt[0])
            ar_ref[pl.ds(t0, _RB), :] = xr
            ai_ref[pl.ds(t0, _RB), :] = xi
            pr = jnp.where(row == 0, prev_last[0], pltpu.roll(sr_ref[pl.ds(t0, _RB), :], 1, 0))
            pi = jnp.where(row == 0, prev_last[1], pltpu.roll(si_ref[pl.ds(t0, _RB), :], 1, 0))
            return xr, xi, xr * pr + xi * pi, xi * pr - xr * pi

        def step(n, carry):
            nr, ni, glr, gli = carry
            t0 = pl.multiple_of((NT - 1 - n) * _RB, _RB)
            tp = pl.multiple_of((NT - 2 - n) * _RB, _RB)
            prev_last = (sr_ref[pl.ds(tp, _RB), :][_RB - 1:_RB, :], si_ref[pl.ds(tp, _RB), :][_RB - 1:_RB, :])
            xr, xi, gr, gi = tile(t0, (nr, ni), prev_last)
            return xr[0:1, :], xi[0:1, :], glr + gr, gli + gi

        z1 = jnp.zeros((1, cb), f32)
        z8 = jnp.zeros((_RB, cb), f32)
        nr, ni, glr, gli = lax.fori_loop(0, NT - 1, step, (z1, z1, z8, z8))
        _, _, gr, gi = tile(0, (nr, ni), (z1, z1))
        glr_ref[...] = jnp.sum(glr + gr, axis=0, keepdims=True)
        gli_ref[...] = jnp.sum(gli + gi, axis=0, keepdims=True)

    col = pl.BlockSpec((T, cb), lambda j: (0, j))
    vec = pl.BlockSpec((1, cb), lambda j: (0, j))
    return pl.pallas_call(
        body, grid=(N // cb,), in_specs=[col, col, col, col, vec, vec], out_specs=[col, col, vec, vec],
        out_shape=[SDS((T, N), f32), SDS((T, N), f32), SDS((1, N), f32), SDS((1, N), f32)],
        compiler_params=_cp(("parallel",)), name=name,
    )(dsr, dsi, sr, si, lr_, li_)


def _conv_fwd(name, p, col0, w, GW3):
    T = p.shape[0]
    K = w.shape[0]
    cb = 128
    c0 = col0 // cb

    def body(x_ref, w_ref, o_ref, pad_ref):
        pad_ref[pl.ds(0, 8), :] = jnp.zeros((8, cb), f32)
        pad_ref[pl.ds(8, T), :] = x_ref[...]
        wv = w_ref[...]
        acc = jnp.zeros((T, cb), f32)
        for j in range(K):
            acc = acc + wv[j:j + 1, :] * pad_ref[pl.ds(8 - (K - 1) + j, T), :]
        o_ref[...] = jax.nn.silu(acc)

    return pl.pallas_call(
        body, grid=(GW3 // cb,),
        in_specs=[pl.BlockSpec((T, cb), lambda j: (0, j + c0)), pl.BlockSpec((K, cb), lambda j: (0, j))],
        out_specs=pl.BlockSpec((T, cb), lambda j: (0, j)), out_shape=SDS((T, GW3), f32),
        scratch_shapes=[pltpu.VMEM((T + 8, cb), f32)], compiler_params=_cp(("parallel",)), name=name,
    )(p, w)


def _conv_bwd(name, p, col0, w, dout3):
    T = p.shape[0]
    K = w.shape[0]
    GW = dout3.shape[2]
    GW3 = 3 * GW
    cb = 128
    c0 = col0 // cb
    nb = GW // cb

    def body(x_ref, w_ref, d_ref, dx_ref, dw_ref, pad_ref, dpad_ref):
        pad_ref[pl.ds(0, 8), :] = jnp.zeros((8, cb), f32)
        pad_ref[pl.ds(8, T), :] = x_ref[...]
        wv = w_ref[...]
        pre = jnp.zeros((T, cb), f32)
        for j in range(K):
            pre = pre + wv[j:j + 1, :] * pad_ref[pl.ds(8 - (K - 1) + j, T), :]
        _, vjp = jax.vjp(jax.nn.silu, pre)
        (dpre,) = vjp(d_ref[...])
        dpad_ref[pl.ds(0, T), :] = dpre
        dpad_ref[pl.ds(T, 8), :] = jnp.zeros((8, cb), f32)
        dx = jnp.zeros((T, cb), f32)
        rows = []
        for j in range(K):
            dx = dx + wv[j:j + 1, :] * dpad_ref[pl.ds((K - 1) - j, T), :]
            rows.append(jnp.sum(dpre * pad_ref[pl.ds(8 - (K - 1) + j, T), :], axis=0, keepdims=True))
        dx_ref[...] = dx.astype(dx_ref.dtype)
        for j in range(K):
            dw_ref[pl.ds(j, 1), :] = rows[j]

    return pl.pallas_call(
        body, grid=(GW3 // cb,),
        in_specs=[pl.BlockSpec((T, cb), lambda j: (0, j + c0)), pl.BlockSpec((K, cb), lambda j: (0, j)),
                  pl.BlockSpec((None, T, cb), lambda j: (j // nb, 0, j % nb))],
        out_specs=[pl.BlockSpec((T, cb), lambda j: (0, j)), pl.BlockSpec((K, cb), lambda j: (0, j))],
        out_shape=[SDS((T, GW3), _MXU), SDS((K, GW3), f32)],
        scratch_shapes=[pltpu.VMEM((T + 8, cb), f32), pltpu.VMEM((T + 8, cb), f32)],
        compiler_params=_cp(("parallel",)), name=name,
    )(p, w, dout3)


def _hdot(a, b, dims=_NN):
    return lax.dot_general(a, b, dims, precision=_HP, preferred_element_type=f32)


def _split(a):
    hi = a.astype(jnp.bfloat16)
    lo = (a - hi.astype(f32)).astype(jnp.bfloat16)
    return hi, lo


_BNN = (((2,), (1,)), ((0,), (0,)))
_BNT = (((2,), (2,)), ((0,), (0,)))
_BTN = (((1,), (1,)), ((0,), (0,)))


def _dot3_raw(a, b, dims):
    ah, al = _split(a)
    bh, bl = _split(b)
    d = functools.partial(lax.dot_general, dimension_numbers=dims, preferred_element_type=f32)
    return d(ah, bh) + (d(al, bh) + d(ah, bl))


@jax.custom_vjp
def _dot3(a, b):
    return _dot3_raw(a, b, _BNN)


def _dot3_fwd(a, b):
    return _dot3_raw(a, b, _BNN), (a, b)


def _dot3_bwd(res, g):
    a, b = res
    return _dot3_raw(g, b, _BNT), _dot3_raw(a, g, _BTN)


_dot3.defvjp(_dot3_fwd, _dot3_bwd)


def _ldot(a, b, dims=_BNN):
    return lax.dot_general(a.astype(_MXU), b.astype(_MXU), dims, preferred_element_type=f32)


def _sdot(a, b):
    return _ldot(a, b)


def _gdn_chunk(S, q, k, v, z, bl, ain, alog, dtb, nw):
    H, C, d = q.shape
    ri = lax.broadcasted_iota(jnp.int32, (H, C, C), 1)
    ci = lax.broadcasted_iota(jnp.int32, (H, C, C), 2)
    causal = ri >= ci
    strict = ri > ci
    tri = causal.astype(f32)
    qn = q * lax.rsqrt(jnp.sum(q * q, axis=-1, keepdims=True) + L2_EPS) * (d ** -0.5)
    kn = k * lax.rsqrt(jnp.sum(k * k, axis=-1, keepdims=True) + L2_EPS)
    beta = jax.nn.sigmoid(bl)
    g = -jnp.exp(alog) * jax.nn.softplus(ain + dtb)
    gb = jnp.broadcast_to(g, (H, C, C))
    gc_col = _dot3(tri, gb)
    gc_row = _dot3(jnp.ones((H, C, C), f32), jnp.where(ri <= ci, gb, 0.0))
    diff = jnp.where(causal, gc_col - gc_row, 0.0)
    decay = jnp.where(causal, jnp.exp(diff), 0.0)
    gcum = gc_col[:, :, 0:1]
    glast = gc_col[:, C - 1:C, 0:1]
    egc = jnp.exp(gcum)
    kb = kn * beta
    lower = jnp.where(strict, _ldot(kb, kn, _BNT) * decay, 0.0)
    x = jnp.concatenate([v * beta, kb * egc], axis=-1)
    m = -lower
    for it in range(6):
        x = x + _sdot(m, x)
        if it < 5:
            m = _sdot(m, m)
    u_val, w_key = x[:, :, :d], x[:, :, d:]
    attn = _ldot(qn, kn, _BNT) * decay
    q_dec = qn * egc
    k_dec = kn * jnp.exp(glast - gcum)
    v_new = u_val - _ldot(w_key, S)
    out = _ldot(q_dec, S) + _ldot(attn, v_new)
    s_new = S * jnp.exp(glast) + _ldot(k_dec, v_new, _BTN)
    o = out * lax.rsqrt(jnp.mean(out * out, axis=-1, keepdims=True) + RMS_EPS) * nw
    o = o * jax.nn.silu(z)
    return s_new, o


def _heads_per_step(NH, HD, zcol0):
    for hb in (4, 2):
        if NH % hb == 0 and zcol0 % (hb * HD) == 0:
            return hb
    return 1


def _gdn_fwd(name, qkv, p, zcol0, blt, aint, alog, dtb, nw, NH, HD):
    T = qkv.shape[0]
    N = T // CHUNK
    GW = NH * HD
    HB = _heads_per_step(NH, HD, zcol0)
    W = HB * HD
    zc0 = zcol0 // W
    nb = GW // W

    def body(q_ref, k_ref, v_ref, z_ref, bl_ref, ain_ref, al_ref, dtb_ref, nw_ref, o_ref, ssave_ref, s_scr):
        n = pl.program_id(1)

        @pl.when(n == 0)
        def _():
            s_scr[...] = jnp.zeros_like(s_scr)

        heads = lambda r: jnp.stack([r[:, hh * HD:(hh + 1) * HD] for hh in range(HB)], axis=0)
        s_in = s_scr[...]
        ssave_ref[...] = s_in
        s_new, o = _gdn_chunk(s_in, heads(q_ref), heads(k_ref), heads(v_ref), heads(z_ref), bl_ref[...], ain_ref[...],
                              al_ref[...], dtb_ref[...], nw_ref[...])
        s_scr[...] = s_new
        for hh in range(HB):
            o_ref[:, hh * HD:(hh + 1) * HD] = o[hh].astype(o_ref.dtype)

    ch = lambda off: pl.BlockSpec((CHUNK, W), lambda h, n: (n, h + off))
    sc = pl.BlockSpec((HB, CHUNK, 1), lambda h, n: (h, n, 0))
    hs = pl.BlockSpec((HB, 1, 1), lambda h, n: (h, 0, 0))
    return pl.pallas_call(
        body, grid=(NH // HB, N),
        in_specs=[ch(0), ch(nb), ch(2 * nb), ch(zc0), sc, sc, hs, hs, pl.BlockSpec((1, HD), lambda h, n: (0, 0))],
        out_specs=[pl.BlockSpec((CHUNK, W), lambda h, n: (n, h)),
                   pl.BlockSpec((HB, None, HD, HD), lambda h, n: (h, n, 0, 0))],
        out_shape=[SDS((T, GW), _MXU), SDS((NH, N, HD, HD), f32)],
        scratch_shapes=[pltpu.VMEM((HB, HD, HD), f32)], compiler_params=_cp(("parallel", "arbitrary")), name=name,
    )(qkv, qkv, qkv, p, blt, aint, alog, dtb, nw)


def _gdn_bwd(name, qkv, p, zcol0, blt, aint, alog, dtb, nw, ssave, do, NH, HD):
    T = qkv.shape[0]
    N = T // CHUNK
    GW = NH * HD
    HB = _heads_per_step(NH, HD, zcol0)
    W = HB * HD
    zc0 = zcol0 // W
    nb = GW // W

    def body(q_ref, k_ref, v_ref, z_ref, bl_ref, ain_ref, al_ref, dtb_ref, nw_ref, ss_ref, do_ref,
             dqkv_ref, dz_ref, dbl_ref, dain_ref, dal_ref, ddtb_ref, dnw_ref, ds_scr):
        h = pl.program_id(0)
        n = pl.program_id(1)

        @pl.when(n == 0)
        def _():
            ds_scr[...] = jnp.zeros_like(ds_scr)

        heads = lambda r: jnp.stack([r[:, hh * HD:(hh + 1) * HD] for hh in range(HB)], axis=0)
        _, vjp = jax.vjp(_gdn_chunk, ss_ref[...], heads(q_ref), heads(k_ref), heads(v_ref), heads(z_ref), bl_ref[...],
                         ain_ref[...], al_ref[...], dtb_ref[...], nw_ref[...])
        ds, dq, dk, dv, dz, dbl, dain, dal, ddtb, dnw = vjp((ds_scr[...], heads(do_ref).astype(f32)))
        ds_scr[...] = ds
        for hh in range(HB):
            cs = slice(hh * HD, (hh + 1) * HD)
            dqkv_ref[0, :, cs] = dq[hh]
            dqkv_ref[1, :, cs] = dk[hh]
            dqkv_ref[2, :, cs] = dv[hh]
            dz_ref[:, cs] = dz[hh].astype(dz_ref.dtype)
        dbl_ref[...] = dbl
        dain_ref[...] = dain

        @pl.when(n == 0)
        def _():
            dal_ref[...] = dal
            ddtb_ref[...] = ddtb

        @pl.when(n > 0)
        def _():
            dal_ref[...] += dal
            ddtb_ref[...] += ddtb

        @pl.when((n == 0) & (h == 0))
        def _():
            dnw_ref[...] = dnw

        @pl.when((n > 0) | (h > 0))
        def _():
            dnw_ref[...] += dnw

    R = N - 1
    ch = lambda off: pl.BlockSpec((CHUNK, W), lambda h, n: (R - n, h + off))
    sc = pl.BlockSpec((HB, CHUNK, 1), lambda h, n: (h, R - n, 0))
    hs = pl.BlockSpec((HB, 1, 1), lambda h, n: (h, 0, 0))
    nws = pl.BlockSpec((1, HD), lambda h, n: (0, 0))
    return pl.pallas_call(
        body, grid=(NH // HB, N),
        in_specs=[ch(0), ch(nb), ch(2 * nb), ch(zc0), sc, sc, hs, hs, nws,
                  pl.BlockSpec((HB, None, HD, HD), lambda h, n: (h, R - n, 0, 0)),
                  pl.BlockSpec((CHUNK, W), lambda h, n: (R - n, h))],
        out_specs=[pl.BlockSpec((3, CHUNK, W), lambda h, n: (0, R - n, h)),
                   pl.BlockSpec((CHUNK, W), lambda h, n: (R - n, h)), sc, sc, hs, hs, nws],
        out_shape=[SDS((3, T, GW), f32), SDS((T, GW), _MXU), SDS((NH, T, 1), f32), SDS((NH, T, 1), f32),
                   SDS((NH, 1, 1), f32), SDS((NH, 1, 1), f32), SDS((1, HD), f32)],
        scratch_shapes=[pltpu.VMEM((HB, HD, HD), f32)], compiler_params=_cp(("arbitrary", "arbitrary")), name=name,
    )(qkv, qkv, qkv, p, blt, aint, alog, dtb, nw, ssave, do)


def _loss_head(name, y, tgt):
    T, D = y.shape
    tm = _rtile(T, 256)

    def body(y_ref, t_ref, dy_ref, l_ref):
        i = pl.program_id(0)
        err = y_ref[...] - t_ref[...]
        dy_ref[...] = err * (1.0 / D)
        part = 0.5 * jnp.sum(jnp.sum(err * err, axis=-1, keepdims=True) * (1.0 / D), axis=0, keepdims=True)

        @pl.when(i == 0)
        def _():
            l_ref[...] = part

        @pl.when(i > 0)
        def _():
            l_ref[...] += part

    row = pl.BlockSpec((tm, D), lambda i: (i, 0))
    return pl.pallas_call(
        body, grid=(T // tm,), in_specs=[row, row], out_specs=[row, pl.BlockSpec((1, 1), lambda i: (0, 0))],
        out_shape=[SDS((T, D), f32), SDS((1, 1), f32)], compiler_params=_cp(("arbitrary",)), name=name,
    )(y, tgt)


def _adam_math(w, g, m, v):
    m = ADAM_B1 * m + (1.0 - ADAM_B1) * g
    v = ADAM_B2 * v + (1.0 - ADAM_B2) * jnp.square(g)
    m_hat = m / (1.0 - ADAM_B1 ** ADAM_STEP)
    v_hat = v / (1.0 - ADAM_B2 ** ADAM_STEP)
    delta = -ADAM_LR * (m_hat / (jnp.sqrt(v_hat) + ADAM_EPS) + ADAM_WD * w)
    return delta, m, v


def _add_mine(name, full, recv, me, out_dtype):
    N, _, R, C = full.shape
    tr = _rtile(R, max(16, (1 << 19) // max(C, 1) // 16 * 16))

    def body(me_ref, a_ref, b_ref, o_ref):
        o_ref[...] = (a_ref[...].astype(f32) + b_ref[...].astype(f32)).astype(o_ref.dtype)

    blk = pl.BlockSpec((None, tr, C), lambda n, i, me_ref: (n, i, 0))
    return pl.pallas_call(
        body,
        grid_spec=pltpu.PrefetchScalarGridSpec(
            num_scalar_prefetch=1, grid=(N, R // tr),
            in_specs=[pl.BlockSpec((None, None, tr, C), lambda n, i, me_ref: (n, me_ref[0], i, 0)), blk], out_specs=blk),
        out_shape=SDS((N, R, C), out_dtype), compiler_params=_cp(("parallel", "parallel")), name=name,
    )(me, full, recv)


def _adamw_big(name, full, recv, me, w, m, v, l, accs):
    _, R, C = full.shape
    L = w.shape[0]
    tr = _rtile(R, max(16, (1 << 18) // max(C, 1) // 16 * 16))

    def body(me_ref, ga_ref, gb_ref, w_ref, m_ref, v_ref, a0, a1, a2, a3, g_ref, d_ref, nm_ref, nv_ref):
        g = ga_ref[...].astype(f32) + gb_ref[...].astype(f32)
        d, nm, nv = _adam_math(w_ref[...], g, m_ref[...], v_ref[...])
        g_ref[...] = g
        d_ref[...] = d
        nm_ref[...] = nm
        nv_ref[...] = nv

    blk = pl.BlockSpec((tr, C), lambda i, me_ref: (i, 0))
    lblk = pl.BlockSpec((None, tr, C), lambda i, me_ref: (l, i, 0))
    untouched = pl.BlockSpec(memory_space=pl.ANY)
    return pl.pallas_call(
        body,
        grid_spec=pltpu.PrefetchScalarGridSpec(
            num_scalar_prefetch=1, grid=(R // tr,),
            in_specs=[pl.BlockSpec((None, tr, C), lambda i, me_ref: (me_ref[0], i, 0)), blk, lblk, lblk, lblk] + [untouched] * 4,
            out_specs=[lblk] * 4),
        out_shape=[SDS((L, R, C), f32)] * 4, input_output_aliases={6: 0, 7: 1, 8: 2, 9: 3},
        compiler_params=_cp(("parallel",)), name=name,
    )(me, full, recv, w, m, v, *accs)


def _adamw_small(name, gall, w, m, v):
    _, R, C = gall.shape
    tr = _rtile(R, 512)

    def body(ga_ref, w_ref, m_ref, v_ref, g_ref, d_ref, nm_ref, nv_ref):
        g = ga_ref[0]
        for s in range(1, 8):
            g = g + ga_ref[s]
        d, nm, nv = _adam_math(w_ref[...], g, m_ref[...], v_ref[...])
        g_ref[...] = g
        d_ref[...] = d
        nm_ref[...] = nm
        nv_ref[...] = nv

    blk = pl.BlockSpec((tr, C), lambda i: (i, 0))
    return pl.pallas_call(
        body, grid=(R // tr,), in_specs=[pl.BlockSpec((8, tr, C), lambda i: (0, i, 0)), blk, blk, blk], out_specs=[blk] * 4,
        out_shape=[SDS((R, C), f32)] * 4, compiler_params=_cp(("parallel",)), name=name,
    )(gall, w, m, v)


def _peer(axis):
    x, y, c = lax.axis_index("x"), lax.axis_index("y"), lax.axis_index("c")
    me = {"x": x, "y": y, "c": c}[axis]
    peer = {"x": (1 - x, y, c), "y": (x, 1 - y, c), "c": (x, y, 1 - c)}[axis]
    return me, peer


def _held(ref, done):
    idx = tuple(slice(None) if a in done else lax.axis_index(a) for a in ("x", "y", "c"))
    return ref.at[idx]


def _gather_stage(name, bufs, axes, dones):
    n = len(bufs)
    hbm = pl.BlockSpec(memory_space=pltpu.HBM)

    def body(*refs):
        outs = refs[n:2 * n]
        send_sems, recv_sems = refs[2 * n:]
        cps = []
        for t in range(n):
            _, peer = _peer(axes[t])
            blk = _held(outs[t], dones[t])
            cps.append(pltpu.make_async_remote_copy(src_ref=blk, dst_ref=blk, send_sem=send_sems.at[t],
                                                    recv_sem=recv_sems.at[t], device_id=peer, device_id_type=_MESH_T))
        for cp in cps:
            cp.start()
        for cp in cps:
            cp.wait()

    return pl.pallas_call(
        body, in_specs=[hbm] * n, out_specs=[hbm] * n, out_shape=[SDS(b.shape, b.dtype) for b in bufs],
        input_output_aliases={t: t for t in range(n)},
        scratch_shapes=[pltpu.SemaphoreType.DMA((n,)), pltpu.SemaphoreType.DMA((n,))], name=name,
    )(*bufs)


def _scatter_stage(name, tensors, axes):
    n = len(tensors)
    hbm = pl.BlockSpec(memory_space=pltpu.HBM)

    def body(*refs):
        ins, recvs = refs[:n], refs[n:2 * n]
        send_sems, recv_sems = refs[2 * n:]
        cps = []
        for t in range(n):
            me, peer = _peer(axes[t])
            cps.append(pltpu.make_async_remote_copy(
                src_ref=ins[t].at[:, 1 - me], dst_ref=recvs[t], send_sem=send_sems.at[t], recv_sem=recv_sems.at[t],
                device_id=peer, device_id_type=_MESH_T))
        for cp in cps:
            cp.start()
        for cp in cps:
            cp.wait()

    return pl.pallas_call(
        body, in_specs=[hbm] * n, out_specs=[hbm] * n,
        out_shape=[SDS((t.shape[0],) + tuple(t.shape[2:]), t.dtype) for t in tensors],
        scratch_shapes=[pltpu.SemaphoreType.DMA((n,)), pltpu.SemaphoreType.DMA((n,))], name=name,
    )(*tensors)


_HBM = pl.BlockSpec(memory_space=pltpu.HBM)
_SEM = pl.BlockSpec(memory_space=pltpu.SEMAPHORE)
_EFFECT = pltpu.SideEffectType.DATAFLOW_SIDE_EFFECTING


def _split_start(name, arrays, n_copies, make_copies):
    na = len(arrays)

    def body(*refs):
        ins = refs[:na]
        send_sems, recv_sems = refs[na], refs[na + 1]
        token = refs[2 * na + 2]
        for cp in make_copies(ins, send_sems, recv_sems):
            cp.start()
        token[...] = jnp.zeros_like(token)

    res = pl.pallas_call(
        body, name=name,
        out_shape=(pltpu.SemaphoreType.DMA((n_copies,)), pltpu.SemaphoreType.DMA((n_copies,)),
                   *[pltpu.HBM(a.shape, a.dtype) for a in arrays], SDS((8, 128), f32)),
        in_specs=[_HBM] * na, out_specs=(_SEM, _SEM, *[_HBM] * na, pl.BlockSpec(memory_space=pltpu.VMEM)),
        input_output_aliases={i: 2 + i for i in range(na)},
        compiler_params=pltpu.CompilerParams(has_side_effects=_EFFECT),
    )(*[pltpu.with_memory_space_constraint(a, pltpu.HBM) for a in arrays])
    return res[0], res[1], list(res[2:2 + na]), res[2 + na]


def _split_wait(name, arrays, send_sems, recv_sems, after, make_copies):
    na = len(arrays)

    def body(*refs):
        ins = refs[:na]
        for cp in make_copies(ins, refs[na], refs[na + 1]):
            cp.wait_send()
            cp.wait_recv()

    res = pl.pallas_call(
        body, name=name, out_shape=tuple(pltpu.HBM(a.shape, a.dtype) for a in arrays),
        in_specs=[_HBM] * na + [_SEM, _SEM, pl.BlockSpec(memory_space=pl.ANY)], out_specs=tuple([_HBM] * na),
        input_output_aliases={i: i for i in range(na)},
        compiler_params=pltpu.CompilerParams(has_side_effects=_EFFECT),
    )(*arrays, send_sems, recv_sems, after)
    return list(res)


def _gather_copies(axes, dones):
    def make(refs, send_sems, recv_sems):
        cps = []
        for t in range(len(axes)):
            _, peer = _peer(axes[t])
            blk = _held(refs[t], dones[t])
            cps.append(pltpu.make_async_remote_copy(src_ref=blk, dst_ref=blk, send_sem=send_sems.at[t],
                                                    recv_sem=recv_sems.at[t], device_id=peer, device_id_type=_MESH_T))
        return cps
    return make


def _scatter_copies(axes):
    n = len(axes)

    def make(refs, send_sems, recv_sems):
        cps = []
        for t in range(n):
            me, peer = _peer(axes[t])
            cps.append(pltpu.make_async_remote_copy(
                src_ref=refs[t].at[:, 1 - me], dst_ref=refs[n + t], send_sem=send_sems.at[t], recv_sem=recv_sems.at[t],
                device_id=peer, device_id_type=_MESH_T))
        return cps
    return make


class _AsyncGather:
    def __init__(self, pfx, tensors, paths):
        x, y, c = (lax.axis_index(a) for a in ("x", "y", "c"))
        self.pfx, self.shapes = pfx, [tuple(t.shape) for t in tensors]
        self.bufs = [lax.dynamic_update_slice(lax.empty((2, 2, 2) + tuple(t.shape), t.dtype), t[None, None, None],
                                              (x, y, c) + (0,) * t.ndim) for t in tensors]
        self.orders = [tuple(p) + ("c",) for p in paths]
        self.ph = 0

    def _make(self):
        return _gather_copies([o[self.ph] for o in self.orders], [o[:self.ph] for o in self.orders])

    def start(self):
        self.ss, self.rs, self.bufs, tok = _split_start(f"{self.pfx}_start{self.ph}", self.bufs, len(self.bufs), self._make())
        return tok

    def wait(self, after):
        self.bufs = _split_wait(f"{self.pfx}_wait{self.ph}", self.bufs, self.ss, self.rs, after, self._make())
        self.ph += 1

    def result(self):
        return [b.reshape((8,) + s) for b, s in zip(self.bufs, self.shapes)]


class _AsyncReduceScatter:
    def __init__(self, pfx, tensors, paths):
        self.pfx = pfx
        self.rcs = [tuple(t.shape[1:]) for t in tensors]
        self.orders = [("c",) + tuple(p) for p in paths]
        self.left = [["x", "y", "c"] for _ in tensors]
        self.cur = list(tensors)
        self.ph = 0

    def start(self):
        n = len(self.cur)
        views = []
        for i, (t, rc) in enumerate(zip(self.cur, self.rcs)):
            pos = self.left[i].index(self.orders[i][self.ph])
            nb, na = 2 ** pos, 2 ** (len(self.left[i]) - pos - 1)
            views.append(t.reshape((nb, 2, na * rc[0], rc[1])))
        lands = [lax.empty((v.shape[0],) + tuple(v.shape[2:]), v.dtype) for v in views]
        self.make = _scatter_copies([o[self.ph] for o in self.orders])
        self.ss, self.rs, arrs, tok = _split_start(f"{self.pfx}_start{self.ph}", views + lands, n, self.make)
        self.arrs = arrs
        return tok

    def wait(self, after):
        n = len(self.cur)
        arrs = _split_wait(f"{self.pfx}_wait{self.ph}", self.arrs, self.ss, self.rs, after, self.make)
        views, recvs = arrs[:n], arrs[n:]
        ph = self.ph
        if ph == 2:
            self.out = [(v[0], r[0], o[2]) for v, r, o in zip(views, recvs, self.orders)]
        else:
            self.cur = [_add_mine(f"{self.pfx}_add{ph}_{i}", v, r, _coord(o[ph]), v.dtype)
                        for i, (v, r, o) in enumerate(zip(views, recvs, self.orders))]
            for i, o in enumerate(self.orders):
                self.left[i].remove(o[ph])
        self.ph += 1


def _coord(axis):
    return lax.axis_index(axis).astype(jnp.int32).reshape(1)


def _all_gather(pfx, tensors, paths):
    x, y, c = (lax.axis_index(a) for a in ("x", "y", "c"))
    bufs = []
    for t in tensors:
        zero = (0,) * t.ndim
        bufs.append(lax.dynamic_update_slice(lax.empty((2, 2, 2) + tuple(t.shape), t.dtype), t[None, None, None],
                                             (x, y, c) + zero))
    orders = [tuple(p) + ("c",) for p in paths]
    for ph in range(3):
        bufs = _gather_stage(f"{pfx}_{ph}", bufs, [o[ph] for o in orders], [o[:ph] for o in orders])
    return [b.reshape((8,) + tuple(t.shape)) for b, t in zip(bufs, tensors)]


def _reduce_scatter(pfx, tensors, paths):
    rcs = [tuple(t.shape[1:]) for t in tensors]
    orders = [("c",) + tuple(p) for p in paths]
    left = [["x", "y", "c"] for _ in tensors]
    cur = list(tensors)
    for ph in range(3):
        views = []
        for i, (t, rc) in enumerate(zip(cur, rcs)):
            pos = left[i].index(orders[i][ph])
            nb, na = 2 ** pos, 2 ** (len(left[i]) - pos - 1)
            views.append(t.reshape((nb, 2, na * rc[0], rc[1])))
        recvs = _scatter_stage(f"{pfx}_{ph}", views, [o[ph] for o in orders])
        if ph == 2:
            return [(v[0], r[0], o[2]) for v, r, o in zip(views, recvs, orders)]
        cur = [_add_mine(f"{pfx}_add{ph}_{i}", v, r, _coord(o[ph]), v.dtype)
               for i, (v, r, o) in enumerate(zip(views, recvs, orders))]
        for i, o in enumerate(orders):
            left[i].remove(o[ph])


def _mm_nn(name, a, w, out_dtype, tn_pref=1024):
    T, K = a.shape
    N = w.shape[1]
    tm, tn, tk = _rtile(T, 512), _tile(N, tn_pref), _tile(K, 2048)
    return _mm(
        name, a, w, _NN, (T // tm, N // tn, K // tk),
        pl.BlockSpec((tm, tk), lambda i, j, k: (i, k)), pl.BlockSpec((tk, tn), lambda i, j, k: (k, j)), (tm, tn),
        [], [], [SDS((T, N), out_dtype)], [pl.BlockSpec((tm, tn), lambda i, j, k: (i, j))], _store_epi,
    )[0]


def _mm_tn(name, a, b, out_dtype):
    T, M = a.shape
    N = b.shape[1]
    tm, tn, tk = _tile(M, 512), _tile(N, 2048), _rtile(T, 512)
    return _mm(
        name, a, b, _TN, (M // tm, N // tn, T // tk),
        pl.BlockSpec((tk, tm), lambda i, j, k: (k, i)), pl.BlockSpec((tk, tn), lambda i, j, k: (k, j)), (tm, tn),
        [], [], [SDS((M, N), out_dtype)], [pl.BlockSpec((tm, tn), lambda i, j, k: (i, j))], _store_epi,
    )[0]


def _mm_tn_slots(name, a, b, out_dtype):
    T, M = a.shape
    NS = b.shape[1] // 8
    tm, tk = _tile(M, 512), _rtile(T, 512)
    return _mm(
        name, a, b, _TN, (M // tm, 8, T // tk),
        pl.BlockSpec((tk, tm), lambda i, j, k: (k, i)), pl.BlockSpec((tk, NS), lambda i, j, k: (k, j)), (tm, NS),
        [], [], [SDS((8, M, NS), out_dtype)], [pl.BlockSpec((None, tm, NS), lambda i, j, k: (j, i, 0))], _store_epi,
    )[0]


def _mm_nt_slots(name, a, w8, out_dtype):
    T = a.shape[0]
    _, M, NS = w8.shape
    tm, tn = _rtile(T, 512), _tile(M, 1024)
    return _mm(
        name, a, w8, _NT, (T // tm, M // tn, 8),
        pl.BlockSpec((tm, NS), lambda i, j, k: (i, k)), pl.BlockSpec((None, tn, NS), lambda i, j, k: (k, j, 0)), (tm, tn),
        [], [], [SDS((T, M), out_dtype)], [pl.BlockSpec((tm, tn), lambda i, j, k: (i, j))], _store_epi,
    )[0]


def _colsum_kernel(name, fn, ins, in_cols, outs_elem, n_sum, C):
    T = ins[0].shape[0]
    tm = _rtile(T, 256)
    ne = len(outs_elem)

    def body(*refs):
        i = pl.program_id(0)
        iv = [r[...] for r in refs[:len(ins)]]
        res = fn(*iv)
        for o, r in zip(refs[len(ins):len(ins) + ne], res[:ne]):
            o[...] = r.astype(o.dtype)
        sums = [jnp.sum(r, axis=0, keepdims=True) for r in res[ne:]]

        @pl.when(i == 0)
        def _():
            for o, s in zip(refs[len(ins) + ne:], sums):
                o[...] = s

        @pl.when(i > 0)
        def _():
            for o, s in zip(refs[len(ins) + ne:], sums):
                o[...] += s

    in_specs = []
    for arr, off in zip(ins, in_cols):
        if off is None:
            in_specs.append(pl.BlockSpec((1, C), lambda i: (0, 0)))
        else:
            in_specs.append(pl.BlockSpec((tm, C), lambda i, off=off: (i, off)))
    row = pl.BlockSpec((tm, C), lambda i: (i, 0))
    vec = pl.BlockSpec((1, C), lambda i: (0, 0))
    return pl.pallas_call(
        body, grid=(T // tm,), in_specs=in_specs, out_specs=[row] * ne + [vec] * n_sum,
        out_shape=[SDS((T, C), dt) for dt in outs_elem] + [SDS((1, C), f32)] * n_sum,
        compiler_params=_cp(("arbitrary",)), name=name,
    )(*ins)


def _merge(gs, gg, a_s, a_g):
    return jax.nn.sigmoid(gs) * a_s + jax.nn.sigmoid(gg) * a_g


def _glu(yg, lp):
    return yg * jax.nn.sigmoid(lp)


_BIG = ("ffn1_w_gu", "ffn1_w_down", "w_in", "conv_w", "glu_w", "w_br_ssm", "w_br_gdn", "w_out", "ffn2_w_gu", "ffn2_w_down")
_PATHS = ("yx", "yx", "xy", "xy", "yx", "yx", "yx", "yx", "xy", "xy")
_SMALL = ("ln1_g", "ln1_b", "ssm_a_re", "ssm_a_im", "ssm_log_dt", "ssm_b_re", "ssm_b_im", "ssm_c_re", "ssm_c_im", "ssm_d",
          "glu_b", "gdn_a_log", "gdn_dt_bias", "gdn_norm_w", "ln2_g", "ln2_b", "ln3_g", "ln3_b")
_ORDER = ("ffn1_w_gu", "ffn1_w_down", "ln1_g", "ln1_b", "w_in", "conv_w", "ssm_a_re", "ssm_a_im", "ssm_log_dt", "ssm_b_re",
          "ssm_b_im", "ssm_c_re", "ssm_c_im", "ssm_d", "glu_w", "glu_b", "gdn_a_log", "gdn_dt_bias", "gdn_norm_w", "w_br_ssm",
          "w_br_gdn", "w_out", "ln2_g", "ln2_b", "ffn2_w_gu", "ffn2_w_down", "ln3_g", "ln3_b")


def _step(x, tgt, W, M, V):
    T, D = x.shape[1], x.shape[2]
    L = W["ffn1_w_gu"].shape[0]
    G, P = W["ssm_a_re"].shape[1:]
    H = W["ssm_b_re"].shape[3]
    SW = G * H
    NH = W["gdn_a_log"].shape[1]
    HD = W["gdn_norm_w"].shape[1]
    GW = NH * HD
    KC = W["conv_w"].shape[1]
    DS = D // 8
    alpha = (2.0 * L) ** 0.25
    o_b = SW + 4 * GW
    o_gs = o_b + 2 * NH
    IN = o_gs + 2 * D
    NM = IN - 2 * NH
    m_qkv, m_z, m_gs, m_gg = SW, SW + 3 * GW, SW + 4 * GW, SW + 4 * GW + D
    J = G // 8

    x0 = x[0]
    tg = tgt[0]

    def vec(name, l):
        return W[name][l:l + 1]

    saves, weights = [], []
    xc, xcb = x0, x0.astype(_MXU)
    def shards(l):
        return [W["ffn1_w_gu"][l].astype(_MXU), W["ffn1_w_down"][l].astype(_MXU), W["w_in"][l].astype(_MXU), W["conv_w"][l],
                W["glu_w"][l].astype(_MXU), W["w_br_ssm"][l].astype(_MXU), W["w_br_gdn"][l].astype(_MXU),
                W["w_out"][l].astype(_MXU), W["ffn2_w_gu"][l].astype(_MXU), W["ffn2_w_down"][l].astype(_MXU)]

    def dep(a, tok):
        return a if tok is None else a + tok[0:1, 0:1].astype(a.dtype)

    gathered = _all_gather("ag", shards(0), _PATHS)
    ahead = {}
    for l in range(L):
        toks = []
        for k in ([1, 2] if l == 0 else [l + 2]):
            if k < L:
                ahead[k] = _AsyncGather("agp", shards(k), _PATHS)
                toks.append(ahead[k].start())
        nxt = ahead.get(l + 1)
        if nxt and l >= 1:
            nxt.wait(xc)
            toks.append(nxt.start())
        tok = functools.reduce(lambda a, b: a + b, toks) if toks else None
        wgu1, wd1, win8, cw8, wglu, wbs, wbg, wo, wgu2, wd2 = gathered
        wd1 = wd1.reshape(-1, D)
        wd2 = wd2.reshape(-1, D)
        wglu = wglu.reshape(SW, SW)
        wo = wo.reshape(D, D)
        win = jnp.transpose(win8, (1, 0, 2)).reshape(D, IN)
        wmain = jnp.concatenate([win[:, :o_b], win[:, o_gs:]], axis=1)
        wba = jnp.pad(win[:, o_b:o_gs], ((0, 0), (0, 128 - 2 * NH)))
        cw = jnp.transpose(cw8, (1, 0, 2)).reshape(KC, 3 * GW)
        wl = dict(wgu1=wgu1, wd1=wd1, wmain=wmain, wba=wba, cw=cw, wglu=wglu, wbs=wbs, wbg=wbg, wo=wo, wgu2=wgu2, wd2=wd2)
        weights.append(wl)
        sv = {}

        gate, up, hh = _ffn_up("ffn_up", xcb, wgu1)
        x1, x1b, xh1, r1 = _mm_ln("ffn_down_ln", hh, wd1, xc, dep(vec("ln1_g", l), tok), vec("ln1_b", l), alpha, 0.5)
        sv["f1"] = dict(xb=xcb, gate=gate, up=up, h=hh, xhat=xh1, rstd=r1)

        p = _mm_nn("mix_in", x1b, wmain, f32)
        pba = _mm_nn("mix_in_ba", x1b, wba, f32)
        b_re_t = jnp.transpose(W["ssm_b_re"][l], (2, 0, 1))
        b_im_t = jnp.transpose(W["ssm_b_im"][l], (2, 0, 1))
        zoh_in = (W["ssm_a_re"][l], W["ssm_a_im"][l], W["ssm_log_dt"][l][:, None], b_re_t, b_im_t)
        lbr, lbi, bbr_t, bbi_t = _zoh_fwd("zoh", *zoh_in)
        bblk_r = _blockdiag(jnp.transpose(bbr_t, (1, 0, 2)))
        bblk_i = _blockdiag(jnp.transpose(bbi_t, (1, 0, 2)))
        cblkT_r = _blockdiag(W["ssm_c_re"][l])
        cblkT_in = _blockdiag(-W["ssm_c_im"][l])
        lbr_f, lbi_f = lbr.reshape(1, G * P), lbi.reshape(1, G * P)
        bur, bui = _bd2("s5_bu", p, 0, bblk_r, bblk_i)
        if nxt and l == 0:
            nxt.wait(bur)
            tok = nxt.start()
        sr, si = _s5_scan("s5_scan", bur, bui, dep(lbr_f, tok), lbi_f)
        dflat = W["ssm_d"][l].reshape(1, SW)

        def out_epi(acc, ex, outs):
            y_raw = acc + ex[1][...] * ex[0][...]
            yg = jax.nn.gelu(y_raw)
            outs[0][...] = y_raw
            outs[1][...] = yg
            outs[2][...] = yg.astype(outs[2].dtype)

        y_raw, yg, ygb = _bd_sum(
            "s5_out", sr, si, jnp.transpose(cblkT_r, (0, 2, 1)), jnp.transpose(cblkT_in, (0, 2, 1)), [p, dflat],
            lambda tm, nb: [pl.BlockSpec((tm, nb), lambda i, j: (i, j)), pl.BlockSpec((1, nb), lambda i, j: (0, j))],
            [SDS((T, SW), f32), SDS((T, SW), f32), SDS((T, SW), _MXU)], out_epi)

        tmg, tng, tkg = _rtile(T, 512), _tile(SW, 512), _tile(SW, 512)

        def glu_epi(acc, ex, outs):
            lp = acc + ex[1][...]
            outs[0][...] = lp
            outs[1][...] = _glu(ex[0][...], lp).astype(outs[1].dtype)

        lp, ysb = _mm(
            "s5_glu", ygb, wglu, _NN, (T // tmg, SW // tng, SW // tkg),
            pl.BlockSpec((tmg, tkg), lambda i, j, k: (i, k)), pl.BlockSpec((tkg, tng), lambda i, j, k: (k, j)), (tmg, tng),
            [yg, vec("glu_b", l)], [pl.BlockSpec((tmg, tng), lambda i, j, k: (i, j)), pl.BlockSpec((1, tng), lambda i, j, k: (0, j))],
            [SDS((T, SW), f32), SDS((T, SW), _MXU)], [pl.BlockSpec((tmg, tng), lambda i, j, k: (i, j))] * 2, glu_epi)

        qkv = _conv_fwd("gdn_conv", p, m_qkv, cw, 3 * GW)
        blt = jnp.transpose(pba[:, :NH])[:, :, None]
        aint = jnp.transpose(pba[:, NH:2 * NH])[:, :, None]
        alog = W["gdn_a_log"][l].reshape(NH, 1, 1)
        dtb = W["gdn_dt_bias"][l].reshape(NH, 1, 1)
        nw = vec("gdn_norm_w", l)
        og, ssave = _gdn_fwd("gdn", qkv, p, m_z, blt, aint, alog, dtb, nw, NH, HD)

        a_s = _mm(
            "br_ssm", ysb, wbs, _NN, (T // tmg, 8, SW // tkg),
            pl.BlockSpec((tmg, tkg), lambda i, j, k: (i, k)), pl.BlockSpec((None, tkg, DS), lambda i, j, k: (j, k, 0)), (tmg, DS),
            [], [], [SDS((T, D), f32)], [pl.BlockSpec((tmg, DS), lambda i, j, k: (i, j))], _store_epi)[0]
        tkd = _tile(GW, 512)
        gsb, ggb = m_gs // DS, m_gg // DS

        def merge_epi(acc, ex, outs):
            outs[0][...] = acc
            outs[1][...] = _merge(ex[1][...], ex[2][...], ex[0][...], acc).astype(outs[1].dtype)

        tile_ij = pl.BlockSpec((tmg, DS), lambda i, j, k: (i, j))
        a_g, merged = _mm(
            "br_gdn_merge", og, wbg, _NN, (T // tmg, 8, GW // tkd),
            pl.BlockSpec((tmg, tkd), lambda i, j, k: (i, k)), pl.BlockSpec((None, tkd, DS), lambda i, j, k: (j, k, 0)), (tmg, DS),
            [a_s, p, p], [tile_ij, pl.BlockSpec((tmg, DS), lambda i, j, k: (i, j + gsb)),
                          pl.BlockSpec((tmg, DS), lambda i, j, k: (i, j + ggb))],
            [SDS((T, D), f32), SDS((T, D), _MXU)], [tile_ij, tile_ij], merge_epi)
        if nxt:
            nxt.wait(merged)
            tok = nxt.start()
        x2, x2b, xh2, r2 = _mm_ln("mix_out_ln", merged, wo, x1, dep(vec("ln2_g", l), tok), vec("ln2_b", l), alpha, 1.0)
        sv["mx"] = dict(x1b=x1b, p=p, zoh_in=zoh_in, lbr_f=lbr_f, lbi_f=lbi_f, bblk_r=bblk_r, bblk_i=bblk_i, cblkT_r=cblkT_r,
                        cblkT_in=cblkT_in, sr=sr, si=si, dflat=dflat, y_raw=y_raw, yg=yg, ygb=ygb, lp=lp, ysb=ysb, qkv=qkv,
                        blt=blt, aint=aint, alog=alog, dtb=dtb, nw=nw, og=og, ssave=ssave, a_s=a_s, a_g=a_g, merged=merged,
                        xhat=xh2, rstd=r2)

        gate2, up2, hh2 = _ffn_up("ffn_up", x2b, wgu2)
        x3, x3b, xh3, r3 = _mm_ln("ffn_down_ln", hh2, wd2, x2, vec("ln3_g", l), vec("ln3_b", l), alpha, 0.5)
        sv["f2"] = dict(xb=x2b, gate=gate2, up=up2, h=hh2, xhat=xh3, rstd=r3)
        saves.append(sv)
        xc, xcb = x3, x3b
        if nxt:
            nxt.wait(x3)
            gathered = nxt.result()

    dy, loss_part = _loss_head("loss_head", xc, tg)
    loss = lax.psum(loss_part[0, 0], ("x", "y", "c"))

    big_out = {n: [lax.empty(W[n].shape, f32) for _ in range(4)] for n in _BIG}
    small_g = {n: [None] * L for n in _SMALL}
    pend = None
    for l in reversed(range(L)):
        sv, wl = saves[l], weights[l]
        mx = sv["mx"]
        p = mx["p"]
        tok = pend.start() if pend else None
        dx2, dwgu2, dwd2, dg3, db3 = _ffn_bwd("ffn_b", dy, sv["f2"], wl["wgu2"], wl["wd2"], dep(vec("ln3_g", l), tok), alpha)
        small_g["ln3_g"][l], small_g["ln3_b"][l] = dg3[0], db3[0]

        if pend:
            pend.wait(dx2)
            tok = pend.start()
        dz2, dmixb, dg2, db2 = _ln_bwd("mix_lnb", dx2, mx["xhat"], mx["rstd"], dep(vec("ln2_g", l), tok), 1.0)
        small_g["ln2_g"][l], small_g["ln2_b"][l] = dg2[0], db2[0]
        tmg, tkd = _rtile(T, 512), _tile(D, 512)
        gsb, ggb = m_gs // DS, m_gg // DS

        def dmerge_epi(acc, ex, outs):
            _, vjp = jax.vjp(_merge, ex[0][...], ex[1][...], ex[2][...], ex[3][...])
            dgs, dgg, das, dag = vjp(acc)
            outs[0][...] = das.astype(outs[0].dtype)
            outs[1][...] = dag.astype(outs[1].dtype)
            outs[2][...] = dgs.astype(outs[2].dtype)
            outs[3][...] = dgg.astype(outs[3].dtype)

        tile_ij = pl.BlockSpec((tmg, DS), lambda i, j, k: (i, j))
        das, dag, dgs, dgg = _mm(
            "mix_dmerge", dmixb, wl["wo"], _NT, (T // tmg, 8, D // tkd),
            pl.BlockSpec((tmg, tkd), lambda i, j, k: (i, k)), pl.BlockSpec((DS, tkd), lambda i, j, k: (j, k)), (tmg, DS),
            [p, p, mx["a_s"], mx["a_g"]],
            [pl.BlockSpec((tmg, DS), lambda i, j, k: (i, j + gsb)), pl.BlockSpec((tmg, DS), lambda i, j, k: (i, j + ggb)),
             tile_ij, tile_ij],
            [SDS((T, D), _MXU)] * 4, [tile_ij] * 4, dmerge_epi)
        dwo = _mm_tn("mix_dwo", mx["merged"], dmixb, _GDT)
        dys = _mm_nt_slots("br_ssm_dx", das, wl["wbs"], f32)
        dog = _mm_nt_slots("br_gdn_dx", dag, wl["wbg"], f32)
        dwbs = _mm_tn_slots("br_ssm_dw", mx["ysb"], das, _GDT)
        dwbg = _mm_tn_slots("br_gdn_dw", mx["og"], dag, _GDT)

        def glu_b_fn(dys_t, yg_t, lp_t):
            _, vjp = jax.vjp(_glu, yg_t, lp_t)
            dyg1, dlp = vjp(dys_t)
            return dyg1, dlp, dlp

        dyg1, dlpb, dglub = _colsum_kernel("s5_glu_b", glu_b_fn, [dys, mx["yg"], mx["lp"]], [0, 0, 0], [f32, _MXU], 1, SW)
        small_g["glu_b"][l] = dglub[0]
        dwglu = _mm_tn("s5_dwglu", mx["ygb"], dlpb, _GDT)
        tng, tkg = _tile(SW, 512), _tile(SW, 512)

        def dyraw_epi(acc, ex, outs):
            _, vjp = jax.vjp(jax.nn.gelu, ex[1][...])
            (d,) = vjp(ex[0][...] + acc)
            outs[0][...] = d

        t_ij = pl.BlockSpec((tmg, tng), lambda i, j, k: (i, j))
        (dyraw,) = _mm(
            "s5_dyraw", dlpb, wl["wglu"], _NT, (T // tmg, SW // tng, SW // tkg),
            pl.BlockSpec((tmg, tkg), lambda i, j, k: (i, k)), pl.BlockSpec((tng, tkg), lambda i, j, k: (j, k)), (tmg, tng),
            [dyg1, mx["y_raw"]], [t_ij, t_ij], [SDS((T, SW), f32)], [t_ij], dyraw_epi)

        def dd_fn(dyr, u_t, d_t):
            return d_t * dyr, dyr * u_t

        dud, dd = _colsum_kernel("s5_dd", dd_fn, [dyraw, p, mx["dflat"]], [0, 0, None], [f32], 1, SW)
        small_g["ssm_d"][l] = dd.reshape(G, H)
        dsr, dsi = _bd2("s5_ds", dyraw, 0, mx["cblkT_r"], mx["cblkT_in"])
        dcb_r, dcb_i = _bdT2("s5_dc", mx["sr"], mx["si"], 0, dyraw, dyraw, 0, 8 * P, 8 * H, J)
        small_g["ssm_c_re"][l] = _blockdiag_extract(jnp.transpose(dcb_r, (0, 2, 1)), H, P)
        small_g["ssm_c_im"][l] = -_blockdiag_extract(jnp.transpose(dcb_i, (0, 2, 1)), H, P)
        ar, ai, dlr, dli = _s5_scan_bwd("s5_scan_b", dsr, dsi, mx["sr"], mx["si"], mx["lbr_f"], mx["lbi_f"])

        def du_epi(acc, ex, outs):
            outs[0][...] = (acc + ex[0][...]).astype(outs[0].dtype)

        (du,) = _bd_sum(
            "s5_du", ar, ai, jnp.transpose(mx["bblk_r"], (0, 2, 1)), jnp.transpose(mx["bblk_i"], (0, 2, 1)), [dud],
            lambda tm, nb: [pl.BlockSpec((tm, nb), lambda i, j: (i, j))], [SDS((T, SW), _MXU)], du_epi)
        dbb_r, dbb_i = _bdT2("s5_db", p, p, 0, ar, ai, 0, 8 * H, 8 * P, J)
        dbbr_t = jnp.transpose(_blockdiag_extract(dbb_r, H, P), (1, 0, 2))
        dbbi_t = jnp.transpose(_blockdiag_extract(dbb_i, H, P), (1, 0, 2))
        da_re, da_im, dlog_dt, dbre_t, dbim_t = _zoh_bwd("zoh_b", *mx["zoh_in"], dlr.reshape(G, P), dli.reshape(G, P),
                                                         dbbr_t, dbbi_t)
        small_g["ssm_a_re"][l], small_g["ssm_a_im"][l], small_g["ssm_log_dt"][l] = da_re, da_im, dlog_dt[:, 0]
        small_g["ssm_b_re"][l] = jnp.transpose(dbre_t, (1, 2, 0))
        small_g["ssm_b_im"][l] = jnp.transpose(dbim_t, (1, 2, 0))

        dqkv3, dzb, dbl, dain, dal, ddtb, dnw = _gdn_bwd("gdn_b", mx["qkv"], p, m_z, mx["blt"], mx["aint"], mx["alog"],
                                                         mx["dtb"], mx["nw"], mx["ssave"], dog, NH, HD)
        small_g["gdn_a_log"][l], small_g["gdn_dt_bias"][l], small_g["gdn_norm_w"][l] = dal[:, 0, 0], ddtb[:, 0, 0], dnw[0]
        dqkv_pre, dcw = _conv_bwd("gdn_conv_b", p, m_qkv, wl["cw"], dqkv3)
        if pend:
            pend.wait(dqkv_pre)
            tok = pend.start()

        dpm = jnp.concatenate([du, dqkv_pre, dzb, dgs, dgg], axis=1)
        dpba = dep(jnp.concatenate([jnp.transpose(dbl[:, :, 0]), jnp.transpose(dain[:, :, 0]),
                                    jnp.zeros((T, 128 - 2 * NH), f32)], axis=1), tok).astype(_MXU)
        tnd, tkm = _tile(D, 1024), _tile(NM, 512)
        t_ba = _mm(
            "mix_dx_ba", dpba, wl["wba"], _NT, (T // tmg, D // tnd, 1),
            pl.BlockSpec((tmg, 128), lambda i, j, k: (i, 0)), pl.BlockSpec((tnd, 128), lambda i, j, k: (j, 0)), (tmg, tnd),
            [], [], [SDS((T, D), f32)], [pl.BlockSpec((tmg, tnd), lambda i, j, k: (i, j))], _store_epi)[0]

        def dx1_epi(acc, ex, outs):
            outs[0][...] = alpha * ex[0][...] + ex[1][...] + acc

        t_d = pl.BlockSpec((tmg, tnd), lambda i, j, k: (i, j))
        (dx1,) = _mm(
            "mix_dx", dpm, wl["wmain"], _NT, (T // tmg, D // tnd, NM // tkm),
            pl.BlockSpec((tmg, tkm), lambda i, j, k: (i, k)), pl.BlockSpec((tnd, tkm), lambda i, j, k: (j, k)), (tmg, tnd),
            [dz2, t_ba], [t_d, t_d], [SDS((T, D), f32)], [t_d], dx1_epi)
        tnm = _tile(NM, 1024)
        tkt = _tile(T, 1024)
        dwmain = _mm(
            "mix_dw", jnp.transpose(mx["x1b"]), dpm, _NN, (D // tkd, NM // tnm, T // tkt),
            pl.BlockSpec((tkd, tkt), lambda i, j, k: (i, k)), pl.BlockSpec((tkt, tnm), lambda i, j, k: (k, j)), (tkd, tnm),
            [], [], [SDS((D, NM), _GDT)], [pl.BlockSpec((tkd, tnm), lambda i, j, k: (i, j))], _store_epi)[0]
        dwba = _mm_tn("mix_dw_ba", mx["x1b"], dpba, _GDT)
        dwin = jnp.concatenate([dwmain[:, :o_b], dwba[:, :2 * NH], dwmain[:, o_b:]], axis=1)
        dwin8 = jnp.transpose(dwin.reshape(D, 8, IN // 8), (1, 0, 2))
        dcw8 = jnp.transpose(dcw.reshape(KC, 8, 3 * GW // 8), (1, 0, 2))

        dx0, dwgu1, dwd1, dg1, db1 = _ffn_bwd("ffn_b", dx1, sv["f1"], wl["wgu1"], wl["wd1"], vec("ln1_g", l), alpha)
        small_g["ln1_g"][l], small_g["ln1_b"][l] = dg1[0], db1[0]
        dy = dx0

        parts = [dwgu1, dwd1.reshape(8, -1, D), dwin8, dcw8, dwglu.reshape(8, SW // 8, SW), dwbs, dwbg,
                 dwo.reshape(8, DS, D), dwgu2, dwd2.reshape(8, -1, D)]
        if pend:
            pend.wait(dx0)
            for n, (full, recv, last) in zip(_BIG, pend.out):
                big_out[n] = _adamw_big("adamw_" + n, full, recv, _coord(last), W[n], M[n], V[n], l + 1, big_out[n])
        if l > 0:
            pend = _AsyncReduceScatter("rsp", parts, _PATHS)
        else:
            for n, (full, recv, last) in zip(_BIG, _reduce_scatter("rs", parts, _PATHS)):
                big_out[n] = _adamw_big("adamw_" + n, full, recv, _coord(last), W[n], M[n], V[n], l, big_out[n])

    seg = 8 * 128

    def padded(n):
        return -(-n // seg) * seg

    def pack(arrs):
        flat = jnp.concatenate([jnp.pad(a.reshape(-1), (0, padded(a.size) - a.size)) for a in arrs])
        n = flat.shape[0]
        rows = -(-n // (128 * 512)) * 512
        return jnp.pad(flat, (0, rows * 128 - n)).reshape(rows, 128)

    gs_full = [jnp.stack(small_g[n]).reshape(W[n].shape) for n in _SMALL]
    gpack = pack(gs_full)
    (gall,) = _all_gather("ag_small", [gpack], ["yx"])
    sg, sd, sm, sv_ = _adamw_small("adamw_small", gall, pack([W[n] for n in _SMALL]), pack([M[n] for n in _SMALL]),
                                   pack([V[n] for n in _SMALL]))

    def unpack(packed):
        out, row = {}, 0
        for n in _SMALL:
            sz = math.prod(W[n].shape)
            rows = padded(sz) // 128
            out[n] = packed[row:row + rows].reshape(-1)[:sz].reshape(W[n].shape)
            row += rows
        return out

    res = [unpack(a) for a in (sg, sd, sm, sv_)]
    for n in _BIG:
        for i in range(4):
            res[i][n] = big_out[n][i]
    outs = [loss, dy[None]]
    for i in range(4):
        outs += [res[i][n] for n in _ORDER]
    return tuple(outs)


def kernel(x, ffn1_w_gu, ffn1_w_down, ln1_g, ln1_b, w_in, conv_w, ssm_a_re, ssm_a_im, ssm_log_dt, ssm_b_re, ssm_b_im, ssm_c_re, ssm_c_im, ssm_d, glu_w, glu_b, gdn_a_log, gdn_dt_bias, gdn_norm_w, w_br_ssm, w_br_gdn, w_out, ln2_g, ln2_b, ffn2_w_gu, ffn2_w_down, ln3_g, ln3_b, loss_target, m_ffn1_w_gu, m_ffn1_w_down, m_ln1_g, m_ln1_b, m_w_in, m_conv_w, m_ssm_a_re, m_ssm_a_im, m_ssm_log_dt, m_ssm_b_re, m_ssm_b_im, m_ssm_c_re, m_ssm_c_im, m_ssm_d, m_glu_w, m_glu_b, m_gdn_a_log, m_gdn_dt_bias, m_gdn_norm_w, m_w_br_ssm, m_w_br_gdn, m_w_out, m_ln2_g, m_ln2_b, m_ffn2_w_gu, m_ffn2_w_down, m_ln3_g, m_ln3_b, v_ffn1_w_gu, v_ffn1_w_down, v_ln1_g, v_ln1_b, v_w_in, v_conv_w, v_ssm_a_re, v_ssm_a_im, v_ssm_log_dt, v_ssm_b_re, v_ssm_b_im, v_ssm_c_re, v_ssm_c_im, v_ssm_d, v_glu_w, v_glu_b, v_gdn_a_log, v_gdn_dt_bias, v_gdn_norm_w, v_w_br_ssm, v_w_br_gdn, v_w_out, v_ln2_g, v_ln2_b, v_ffn2_w_gu, v_ffn2_w_down, v_ln3_g, v_ln3_b):
    given = dict(locals())
    W = {n: given[n] for n in _ORDER}
    M = {n: given["m_" + n] for n in _ORDER}
    V = {n: given["v_" + n] for n in _ORDER}
    return _step(x, loss_target, W, M, V)
```

```python
import functools
import math

import jax
import jax.numpy as jnp
from jax import lax
from jax.experimental import pallas as pl
from jax.experimental.pallas import tpu as pltpu

f32 = jnp.float32
_MXU = jnp.bfloat16
_GDT = jnp.bfloat16
_HP = lax.Precision.HIGHEST
_VMEM_LIMIT = 56 * 1024 * 1024
_MESH_T = pl.DeviceIdType.MESH

LN_EPS = 1e-5
RMS_EPS = 1e-6
L2_EPS = 1e-6
CHUNK = 64
ADAM_LR = 0.001
ADAM_B1 = 0.9
ADAM_B2 = 0.999
ADAM_EPS = 1e-08
ADAM_WD = 0.01
ADAM_STEP = 10

_NN = (((1,), (0,)), ((), ()))
_NT = (((1,), (1,)), ((), ()))
_TN = (((0,), (0,)), ((), ()))

SDS = jax.ShapeDtypeStruct


def _cp(sem):
    return pltpu.CompilerParams(dimension_semantics=sem, vmem_limit_bytes=_VMEM_LIMIT)


def _tile(n, pref):
    if n <= pref:
        return n
    t = (pref // 128) * 128
    while t >= 128:
        if n % t == 0:
            return t
        t -= 128
    return n


def _rtile(n, pref):
    if n <= pref:
        return n
    t = (pref // 16) * 16
    while t >= 16:
        if n % t == 0:
            return t
        t -= 16
    return n


def _mm(name, a, b, dims, grid, a_spec, b_spec, acc_shape, extras, extra_specs, out_shape, out_specs, epilogue):
    nk = grid[2]
    ne = len(extras)
    no = len(out_shape)

    def body(*refs):
        a_ref, b_ref = refs[0], refs[1]
        ex = refs[2:2 + ne]
        outs = refs[2 + ne:2 + ne + no]
        acc = refs[-1]
        k = pl.program_id(2)
        part = lax.dot_general(a_ref[...].astype(_MXU), b_ref[...].astype(_MXU), dims, preferred_element_type=f32)

        @pl.when(k == 0)
        def _():
            acc[...] = part

        @pl.when(k > 0)
        def _():
            acc[...] += part

        @pl.when(k == nk - 1)
        def _():
            epilogue(acc[...], ex, outs)

    return pl.pallas_call(
        body, grid=grid, in_specs=[a_spec, b_spec, *extra_specs], out_specs=list(out_specs), out_shape=list(out_shape),
        scratch_shapes=[pltpu.VMEM(acc_shape, f32)], compiler_params=_cp(("parallel", "parallel", "arbitrary")), name=name,
    )(a, b, *extras)


def _store_epi(acc, ex, outs):
    for o in outs:
        o[...] = acc.astype(o.dtype)


def _ln_epilogue(alpha, c):
    def epi(acc, ex, outs):
        x_ref, g_ref, b_ref = ex
        y_ref, yb_ref, xh_ref, r_ref = outs
        z = alpha * x_ref[...] + c * acc
        mu = jnp.mean(z, axis=-1, keepdims=True)
        zc = z - mu
        var = jnp.mean(zc * zc, axis=-1, keepdims=True)
        r = lax.rsqrt(var + LN_EPS)
        xh = zc * r
        y = xh * g_ref[...] + b_ref[...]
        y_ref[...] = y
        yb_ref[...] = y.astype(yb_ref.dtype)
        xh_ref[...] = xh
        r_ref[...] = r
    return epi


def _mm_ln(name, a, w, x, g, b, alpha, c):
    T, K = a.shape
    D = w.shape[1]
    tm, tk = _rtile(T, 512), _tile(K, 512)
    row = pl.BlockSpec((tm, D), lambda i, j, k: (i, 0))
    vec = pl.BlockSpec((1, D), lambda i, j, k: (0, 0))
    return _mm(
        name, a, w, _NN, (T // tm, 1, K // tk),
        pl.BlockSpec((tm, tk), lambda i, j, k: (i, k)), pl.BlockSpec((tk, D), lambda i, j, k: (k, 0)), (tm, D),
        [x, g, b], [row, vec, vec],
        [SDS((T, D), f32), SDS((T, D), _MXU), SDS((T, D), f32), SDS((T, 1), f32)],
        [row, row, row, pl.BlockSpec((tm, 1), lambda i, j, k: (i, 0))],
        _ln_epilogue(alpha, c),
    )


def _ln_bwd(name, dy, xhat, rstd, g, c):
    T, D = dy.shape
    tm = _rtile(T, 256)

    def body(dy_ref, xh_ref, r_ref, g_ref, dz_ref, df_ref, dg_ref, db_ref):
        i = pl.program_id(0)
        dyv = dy_ref[...]
        xh = xh_ref[...]
        dxh = dyv * g_ref[...]
        m1 = jnp.mean(dxh, axis=-1, keepdims=True)
        m2 = jnp.mean(dxh * xh, axis=-1, keepdims=True)
        dz = r_ref[...] * (dxh - m1 - xh * m2)
        dz_ref[...] = dz
        df_ref[...] = (c * dz).astype(df_ref.dtype)
        pg = jnp.sum(dyv * xh, axis=0, keepdims=True)
        pb = jnp.sum(dyv, axis=0, keepdims=True)

        @pl.when(i == 0)
        def _():
            dg_ref[...] = pg
            db_ref[...] = pb

        @pl.when(i > 0)
        def _():
            dg_ref[...] += pg
            db_ref[...] += pb

    row = pl.BlockSpec((tm, D), lambda i: (i, 0))
    vec = pl.BlockSpec((1, D), lambda i: (0, 0))
    return pl.pallas_call(
        body, grid=(T // tm,), in_specs=[row, row, pl.BlockSpec((tm, 1), lambda i: (i, 0)), vec],
        out_specs=[row, row, vec, vec],
        out_shape=[SDS((T, D), f32), SDS((T, D), _MXU), SDS((1, D), f32), SDS((1, D), f32)],
        compiler_params=_cp(("arbitrary",)), name=name,
    )(dy, xhat, rstd, g)


def _swiglu(g, u):
    return jax.nn.silu(g) * u


def _ffn_up(name, xb, wgu):
    T, D = xb.shape
    FS = wgu.shape[2]
    F = 4 * FS
    tm = _rtile(T, 256)

    def body(x_ref, wg_ref, wu_ref, g_ref, u_ref, h_ref):
        xv = x_ref[...]
        g = jnp.dot(xv, wg_ref[...], preferred_element_type=f32)
        u = jnp.dot(xv, wu_ref[...], preferred_element_type=f32)
        g_ref[...] = g.astype(g_ref.dtype)
        u_ref[...] = u.astype(u_ref.dtype)
        h_ref[...] = _swiglu(g, u).astype(h_ref.dtype)

    out = pl.BlockSpec((tm, FS), lambda j, i: (i, j))
    return pl.pallas_call(
        body, grid=(4, T // tm),
        in_specs=[pl.BlockSpec((tm, D), lambda j, i: (i, 0)),
                  pl.BlockSpec((None, D, FS), lambda j, i: (j, 0, 0)),
                  pl.BlockSpec((None, D, FS), lambda j, i: (j + 4, 0, 0))],
        out_specs=[out, out, out],
        out_shape=[SDS((T, F), _MXU), SDS((T, F), _MXU), SDS((T, F), _MXU)],
        compiler_params=_cp(("parallel", "arbitrary")), name=name,
    )(xb, wgu, wgu)


def _ffn_bwd(pfx, dy, sv, wgu, wd, g_ln, alpha):
    T, D = dy.shape
    FS = wgu.shape[2]
    F = 4 * FS
    dz, dfb, dg, db = _ln_bwd(pfx + "_lnb", dy, sv["xhat"], sv["rstd"], g_ln, 0.5)

    tm, tn, tk = _rtile(T, 1024), _tile(F, 512), _tile(D, 2048)

    def epi(acc, ex, outs):
        g_ref, u_ref = ex
        _, vjp = jax.vjp(_swiglu, g_ref[...].astype(f32), u_ref[...].astype(f32))
        dgate, dup = vjp(acc)
        outs[0][0] = dgate.astype(outs[0].dtype)
        outs[0][1] = dup.astype(outs[0].dtype)

    gu = pl.BlockSpec((tm, tn), lambda i, j, k: (i, j))
    (dgu,) = _mm(
        pfx + "_dh", dfb, wd, _NT, (T // tm, F // tn, D // tk),
        pl.BlockSpec((tm, tk), lambda i, j, k: (i, k)), pl.BlockSpec((tn, tk), lambda i, j, k: (j, k)), (tm, tn),
        [sv["gate"], sv["up"]], [gu, gu],
        [SDS((2, T, F), _MXU)], [pl.BlockSpec((2, tm, tn), lambda i, j, k: (0, i, j))], epi,
    )

    tm2, tk2 = _tile(F, 512), _rtile(T, 512)
    (dwd,) = _mm(
        pfx + "_dwd", sv["h"], dfb, _TN, (F // tm2, 1, T // tk2),
        pl.BlockSpec((tk2, tm2), lambda i, j, k: (k, i)), pl.BlockSpec((tk2, D), lambda i, j, k: (k, 0)), (tm2, D),
        [], [], [SDS((F, D), _GDT)], [pl.BlockSpec((tm2, D), lambda i, j, k: (i, 0))], _store_epi,
    )

    tn3 = _tile(D, 1024)

    def epi3(acc, ex, outs):
        outs[0][...] = alpha * ex[0][...] + acc

    (dx,) = _mm(
        pfx + "_dx", dgu, wgu, _NT, (T // tm, D // tn3, 8),
        pl.BlockSpec((None, tm, FS), lambda i, j, k: (k // 4, i, k % 4)),
        pl.BlockSpec((None, tn3, FS), lambda i, j, k: (k, j, 0)), (tm, tn3),
        [dz], [pl.BlockSpec((tm, tn3), lambda i, j, k: (i, j))],
        [SDS((T, D), f32)], [pl.BlockSpec((tm, tn3), lambda i, j, k: (i, j))], epi3,
    )

    tm4, tk4 = _rtile(D, 512), _tile(T, 2048)
    (dwgu,) = _mm(
        pfx + "_dwgu", jnp.transpose(sv["xb"]), dgu, _NN, (D // tm4, 8, T // tk4),
        pl.BlockSpec((tm4, tk4), lambda i, j, k: (i, k)),
        pl.BlockSpec((None, tk4, FS), lambda i, j, k: (j // 4, k, j % 4)), (tm4, FS),
        [], [], [SDS((8, D, FS), _GDT)], [pl.BlockSpec((None, tm4, FS), lambda i, j, k: (j, i, 0))], _store_epi,
    )
    return dx, dwgu, dwd, dg, db


def _zoh(a_re, a_im, log_dt, b_re_t, b_im_t):
    dt = jnp.exp(log_dt)
    mag = jnp.exp(a_re * dt)
    lr_, li_ = mag * jnp.cos(a_im * dt), mag * jnp.sin(a_im * dt)
    den = a_re * a_re + a_im * a_im
    pr, pi = lr_ - 1.0, li_
    qr, qi = a_re / den, -a_im / den
    zr, zi = pr * qr - pi * qi, pr * qi + pi * qr
    bbr = zr[None] * b_re_t - zi[None] * b_im_t
    bbi = zr[None] * b_im_t + zi[None] * b_re_t
    return lr_, li_, bbr, bbi


def _zoh_fwd(name, a_re, a_im, log_dt, b_re_t, b_im_t):
    G, P = a_re.shape
    H = b_re_t.shape[0]

    def body(ar, ai, ld, br, bi, o1, o2, o3, o4):
        r = _zoh(ar[...], ai[...], ld[...], br[...], bi[...])
        o1[...], o2[...], o3[...], o4[...] = r

    return pl.pallas_call(
        body, out_shape=[SDS((G, P), f32), SDS((G, P), f32), SDS((H, G, P), f32), SDS((H, G, P), f32)], name=name,
    )(a_re, a_im, log_dt, b_re_t, b_im_t)


def _zoh_bwd(name, a_re, a_im, log_dt, b_re_t, b_im_t, dlr, dli, dbbr, dbbi):
    G, P = a_re.shape
    H = b_re_t.shape[0]

    def body(ar, ai, ld, br, bi, g1, g2, g3, g4, o1, o2, o3, o4, o5):
        _, vjp = jax.vjp(_zoh, ar[...], ai[...], ld[...], br[...], bi[...])
        r = vjp((g1[...], g2[...], g3[...], g4[...]))
        o1[...], o2[...], o3[...], o4[...], o5[...] = r

    return pl.pallas_call(
        body, out_shape=[SDS((G, P), f32), SDS((G, P), f32), SDS((G, 1), f32), SDS((H, G, P), f32), SDS((H, G, P), f32)],
        name=name,
    )(a_re, a_im, log_dt, b_re_t, b_im_t, dlr, dli, dbbr, dbbi)


def _blockdiag(m):
    G, A, B = m.shape
    eye = jnp.eye(8, dtype=bool)
    m4 = m.reshape(G // 8, 8, A, B)
    out = jnp.where(eye[None, :, None, :, None], m4[:, :, :, None, :], jnp.zeros((), m.dtype))
    return out.reshape(G // 8, 8 * A, 8 * B)


def _blockdiag_extract(mb, A, B):
    J = mb.shape[0]
    m5 = mb.reshape(J, 8, A, 8, B)
    d = jnp.stack([m5[:, i, :, i, :] for i in range(8)], axis=1)
    return d.reshape(J * 8, A, B)


def _bd2(name, a, a_col0, b1, b2, out_dtype=f32):
    T = a.shape[0]
    J, KA, NB = b1.shape
    tm = _rtile(T, 512)

    def body(a_ref, b1_ref, b2_ref, o1, o2):
        av = a_ref[...].astype(_MXU)
        o1[...] = jnp.dot(av, b1_ref[...].astype(_MXU), preferred_element_type=f32).astype(o1.dtype)
        o2[...] = jnp.dot(av, b2_ref[...].astype(_MXU), preferred_element_type=f32).astype(o2.dtype)

    bs = pl.BlockSpec((None, KA, NB), lambda i, j: (j, 0, 0))
    os_ = pl.BlockSpec((tm, NB), lambda i, j: (i, j))
    return pl.pallas_call(
        body, grid=(T // tm, J), in_specs=[pl.BlockSpec((tm, KA), lambda i, j: (i, j + a_col0)), bs, bs],
        out_specs=[os_, os_], out_shape=[SDS((T, J * NB), out_dtype)] * 2,
        compiler_params=_cp(("parallel", "parallel")), name=name,
    )(a, b1, b2)


def _bd_sum(name, a1, a2, b1, b2, extras, extra_specs_fn, out_shape, epilogue):
    T = a1.shape[0]
    J, KA, NB = b1.shape
    tm = _rtile(T, 512)
    ne = len(extras)

    def body(*refs):
        a1_ref, a2_ref, b1_ref, b2_ref = refs[:4]
        ex = refs[4:4 + ne]
        outs = refs[4 + ne:]
        acc = jnp.dot(a1_ref[...].astype(_MXU), b1_ref[...].astype(_MXU), preferred_element_type=f32)
        acc = acc + jnp.dot(a2_ref[...].astype(_MXU), b2_ref[...].astype(_MXU), preferred_element_type=f32)
        epilogue(acc, ex, outs)

    as_ = pl.BlockSpec((tm, KA), lambda i, j: (i, j))
    bs = pl.BlockSpec((None, KA, NB), lambda i, j: (j, 0, 0))
    os_ = pl.BlockSpec((tm, NB), lambda i, j: (i, j))
    return pl.pallas_call(
        body, grid=(T // tm, J), in_specs=[as_, as_, bs, bs, *extra_specs_fn(tm, NB)],
        out_specs=[os_] * len(out_shape), out_shape=list(out_shape),
        compiler_params=_cp(("parallel", "parallel")), name=name,
    )(a1, a2, b1, b2, *extras)


def _bdT2(name, a1, a2, a_col0, b1, b2, b_col0, KA, NB, J):
    T = a1.shape[0]
    tk = _rtile(T, 512)

    def body(a1_ref, a2_ref, b1_ref, b2_ref, o1, o2):
        k = pl.program_id(1)
        p1 = lax.dot_general(a1_ref[...].astype(_MXU), b1_ref[...].astype(_MXU), _TN, preferred_element_type=f32)
        p2 = lax.dot_general(a2_ref[...].astype(_MXU), b2_ref[...].astype(_MXU), _TN, preferred_element_type=f32)

        @pl.when(k == 0)
        def _():
            o1[...] = p1
            o2[...] = p2

        @pl.when(k > 0)
        def _():
            o1[...] += p1
            o2[...] += p2

    as_ = pl.BlockSpec((tk, KA), lambda j, k: (k, j + a_col0))
    bs = pl.BlockSpec((tk, NB), lambda j, k: (k, j + b_col0))
    os_ = pl.BlockSpec((None, KA, NB), lambda j, k: (j, 0, 0))
    return pl.pallas_call(
        body, grid=(J, T // tk), in_specs=[as_, as_, bs, bs], out_specs=[os_, os_],
        out_shape=[SDS((J, KA, NB), f32)] * 2, compiler_params=_cp(("parallel", "arbitrary")), name=name,
    )(a1, a2, b1, b2)


_RB = 8


def _cmul(ar, ai, br, bi):
    return ar * br - ai * bi, ar * bi + ai * br


def _lam_powers(lr_v, li_v, cb):
    pw = {1: (lr_v, li_v)}
    for k in range(2, _RB + 1):
        pw[k] = _cmul(*pw[k - 1], lr_v, li_v)
    return pw


def _row_powers(pw, row, cb, reverse):
    outr = jnp.zeros((_RB, cb), f32)
    outi = jnp.zeros((_RB, cb), f32)
    for r in range(_RB):
        k = _RB - r if reverse else r + 1
        outr = jnp.where(row == r, pw[k][0], outr)
        outi = jnp.where(row == r, pw[k][1], outi)
    return outr, outi


def _tile_scan(xr, xi, pw, row, reverse):
    for k in (1, 2, 4):
        if reverse:
            keep = row < _RB - k
            shr, shi = pltpu.roll(xr, _RB - k, 0), pltpu.roll(xi, _RB - k, 0)
        else:
            keep = row >= k
            shr, shi = pltpu.roll(xr, k, 0), pltpu.roll(xi, k, 0)
        shr, shi = jnp.where(keep, shr, 0.0), jnp.where(keep, shi, 0.0)
        mr, mi = pw[k]
        xr, xi = xr + (mr * shr - mi * shi), xi + (mr * shi + mi * shr)
    return xr, xi


def _s5_scan(name, bur, bui, lr_, li_):
    T, N = bur.shape
    cb = _tile(N, 512)

    def body(br_ref, bi_ref, lr_ref, li_ref, sr_ref, si_ref):
        pw = _lam_powers(lr_ref[...], li_ref[...], cb)
        row = lax.broadcasted_iota(jnp.int32, (_RB, cb), 0)
        cr, ci = _row_powers(pw, row, cb, False)

        def step(n, carry):
            pr, pi = carry
            t0 = pl.multiple_of(n * _RB, _RB)
            xr, xi = _tile_scan(br_ref[pl.ds(t0, _RB), :], bi_ref[pl.ds(t0, _RB), :], pw, row, False)
            xr, xi = xr + (cr * pr - ci * pi), xi + (cr * pi + ci * pr)
            sr_ref[pl.ds(t0, _RB), :] = xr
            si_ref[pl.ds(t0, _RB), :] = xi
            return xr[_RB - 1:_RB, :], xi[_RB - 1:_RB, :]

        z = jnp.zeros((1, cb), f32)
        lax.fori_loop(0, T // _RB, step, (z, z))

    col = pl.BlockSpec((T, cb), lambda j: (0, j))
    vec = pl.BlockSpec((1, cb), lambda j: (0, j))
    return pl.pallas_call(
        body, grid=(N // cb,), in_specs=[col, col, vec, vec], out_specs=[col, col],
        out_shape=[SDS((T, N), f32)] * 2, compiler_params=_cp(("parallel",)), name=name,
    )(bur, bui, lr_, li_)


def _s5_scan_bwd(name, dsr, dsi, sr, si, lr_, li_):
    T, N = dsr.shape
    cb = _tile(N, 256)

    def body(dr_ref, di_ref, sr_ref, si_ref, lr_ref, li_ref, ar_ref, ai_ref, glr_ref, gli_ref):
        pw = _lam_powers(lr_ref[...], -li_ref[...], cb)
        row = lax.broadcasted_iota(jnp.int32, (_RB, cb), 0)
        cr, ci = _row_powers(pw, row, cb, True)
        NT = T // _RB

        def tile(t0, nxt, prev_last):
            xr, xi = _tile_scan(dr_ref[pl.ds(t0, _RB), :], di_ref[pl.ds(t0, _RB), :], pw, row, True)
            xr, xi = xr + (cr * nxt[0] - ci * nxt[1]), xi + (cr * nxt[1] + ci * nxt[0])
            ar_ref[pl.ds(t0, _RB), :] = xr
            ai_ref[pl.ds(t0, _RB), :] = xi
            pr = jnp.where(row == 0, prev_last[0], pltpu.roll(sr_ref[pl.ds(t0, _RB), :], 1, 0))
            pi = jnp.where(row == 0, prev_last[1], pltpu.roll(si_ref[pl.ds(t0, _RB), :], 1, 0))
            return xr, xi, xr * pr + xi * pi, xi * pr - xr * pi

        def step(n, carry):
            nr, ni, glr, gli = carry
            t0 = pl.multiple_of((NT - 1 - n) * _RB, _RB)
            tp = pl.multiple_of((NT - 2 - n) * _RB, _RB)
            prev_last = (sr_ref[pl.ds(tp, _RB), :][_RB - 1:_RB, :], si_ref[pl.ds(tp, _RB), :][_RB - 1:_RB, :])
            xr, xi, gr, gi = tile(t0, (nr, ni), prev_last)
            return xr[0:1, :], xi[0:1, :], glr + gr, gli + gi

        z1 = jnp.zeros((1, cb), f32)
        z8 = jnp.zeros((_RB, cb), f32)
        nr, ni, glr, gli = lax.fori_loop(0, NT - 1, step, (z1, z1, z8, z8))
        _, _, gr, gi = tile(0, (nr, ni), (z1, z1))
        glr_ref[...] = jnp.sum(glr + gr, axis=0, keepdims=True)
        gli_ref[...] = jnp.sum(gli + gi, axis=0, keepdims=True)

    col = pl.BlockSpec((T, cb), lambda j: (0, j))
    vec = pl.BlockSpec((1, cb), lambda j: (0, j))
    return pl.pallas_call(
        body, grid=(N // cb,), in_specs=[col, col, col, col, vec, vec], out_specs=[col, col, vec, vec],
        out_shape=[SDS((T, N), f32), SDS((T, N), f32), SDS((1, N), f32), SDS((1, N), f32)],
        compiler_params=_cp(("parallel",)), name=name,
    )(dsr, dsi, sr, si, lr_, li_)


def _conv_fwd(name, p, col0, w, GW3):
    T = p.shape[0]
    K = w.shape[0]
    cb = 128
    c0 = col0 // cb

    def body(x_ref, w_ref, o_ref, pad_ref):
        pad_ref[pl.ds(0, 8), :] = jnp.zeros((8, cb), f32)
        pad_ref[pl.ds(8, T), :] = x_ref[...]
        wv = w_ref[...]
        acc = jnp.zeros((T, cb), f32)
        for j in range(K):
            acc = acc + wv[j:j + 1, :] * pad_ref[pl.ds(8 - (K - 1) + j, T), :]
        o_ref[...] = jax.nn.silu(acc)

    return pl.pallas_call(
        body, grid=(GW3 // cb,),
        in_specs=[pl.BlockSpec((T, cb), lambda j: (0, j + c0)), pl.BlockSpec((K, cb), lambda j: (0, j))],
        out_specs=pl.BlockSpec((T, cb), lambda j: (0, j)), out_shape=SDS((T, GW3), f32),
        scratch_shapes=[pltpu.VMEM((T + 8, cb), f32)], compiler_params=_cp(("parallel",)), name=name,
    )(p, w)


def _conv_bwd(name, p, col0, w, dout3):
    T = p.shape[0]
    K = w.shape[0]
    GW = dout3.shape[2]
    GW3 = 3 * GW
    cb = 128
    c0 = col0 // cb
    nb = GW // cb

    def body(x_ref, w_ref, d_ref, dx_ref, dw_ref, pad_ref, dpad_ref):
        pad_ref[pl.ds(0, 8), :] = jnp.zeros((8, cb), f32)
        pad_ref[pl.ds(8, T), :] = x_ref[...]
        wv = w_ref[...]
        pre = jnp.zeros((T, cb), f32)
        for j in range(K):
            pre = pre + wv[j:j + 1, :] * pad_ref[pl.ds(8 - (K - 1) + j, T), :]
        _, vjp = jax.vjp(jax.nn.silu, pre)
        (dpre,) = vjp(d_ref[...])
        dpad_ref[pl.ds(0, T), :] = dpre
        dpad_ref[pl.ds(T, 8), :] = jnp.zeros((8, cb), f32)
        dx = jnp.zeros((T, cb), f32)
        rows = []
        for j in range(K):
            dx = dx + wv[j:j + 1, :] * dpad_ref[pl.ds((K - 1) - j, T), :]
            rows.append(jnp.sum(dpre * pad_ref[pl.ds(8 - (K - 1) + j, T), :], axis=0, keepdims=True))
        dx_ref[...] = dx.astype(dx_ref.dtype)
        for j in range(K):
            dw_ref[pl.ds(j, 1), :] = rows[j]

    return pl.pallas_call(
        body, grid=(GW3 // cb,),
        in_specs=[pl.BlockSpec((T, cb), lambda j: (0, j + c0)), pl.BlockSpec((K, cb), lambda j: (0, j)),
                  pl.BlockSpec((None, T, cb), lambda j: (j // nb, 0, j % nb))],
        out_specs=[pl.BlockSpec((T, cb), lambda j: (0, j)), pl.BlockSpec((K, cb), lambda j: (0, j))],
        out_shape=[SDS((T, GW3), _MXU), SDS((K, GW3), f32)],
        scratch_shapes=[pltpu.VMEM((T + 8, cb), f32), pltpu.VMEM((T + 8, cb), f32)],
        compiler_params=_cp(("parallel",)), name=name,
    )(p, w, dout3)


def _hdot(a, b, dims=_NN):
    return lax.dot_general(a, b, dims, precision=_HP, preferred_element_type=f32)


def _split(a):
    hi = a.astype(jnp.bfloat16)
    lo = (a - hi.astype(f32)).astype(jnp.bfloat16)
    return hi, lo


_BNN = (((2,), (1,)), ((0,), (0,)))
_BNT = (((2,), (2,)), ((0,), (0,)))
_BTN = (((1,), (1,)), ((0,), (0,)))


def _dot3_raw(a, b, dims):
    ah, al = _split(a)
    bh, bl = _split(b)
    d = functools.partial(lax.dot_general, dimension_numbers=dims, preferred_element_type=f32)
    return d(ah, bh) + (d(al, bh) + d(ah, bl))


@jax.custom_vjp
def _dot3(a, b):
    return _dot3_raw(a, b, _BNN)


def _dot3_fwd(a, b):
    return _dot3_raw(a, b, _BNN), (a, b)


def _dot3_bwd(res, g):
    a, b = res
    return _dot3_raw(g, b, _BNT), _dot3_raw(a, g, _BTN)


_dot3.defvjp(_dot3_fwd, _dot3_bwd)


def _ldot(a, b, dims=_BNN):
    return lax.dot_general(a.astype(_MXU), b.astype(_MXU), dims, preferred_element_type=f32)


def _sdot(a, b):
    return _ldot(a, b)


def _gdn_chunk(S, q, k, v, z, bl, ain, alog, dtb, nw):
    H, C, d = q.shape
    ri = lax.broadcasted_iota(jnp.int32, (H, C, C), 1)
    ci = lax.broadcasted_iota(jnp.int32, (H, C, C), 2)
    causal = ri >= ci
    strict = ri > ci
    tri = causal.astype(f32)
    qn = q * lax.rsqrt(jnp.sum(q * q, axis=-1, keepdims=True) + L2_EPS) * (d ** -0.5)
    kn = k * lax.rsqrt(jnp.sum(k * k, axis=-1, keepdims=True) + L2_EPS)
    beta = jax.nn.sigmoid(bl)
    g = -jnp.exp(alog) * jax.nn.softplus(ain + dtb)
    gb = jnp.broadcast_to(g, (H, C, C))
    gc_col = _dot3(tri, gb)
    gc_row = _dot3(jnp.ones((H, C, C), f32), jnp.where(ri <= ci, gb, 0.0))
    diff = jnp.where(causal, gc_col - gc_row, 0.0)
    decay = jnp.where(causal, jnp.exp(diff), 0.0)
    gcum = gc_col[:, :, 0:1]
    glast = gc_col[:, C - 1:C, 0:1]
    egc = jnp.exp(gcum)
    kb = kn * beta
    lower = jnp.where(strict, _ldot(kb, kn, _BNT) * decay, 0.0)
    x = jnp.concatenate([v * beta, kb * egc], axis=-1)
    m = -lower
    for it in range(6):
        x = x + _sdot(m, x)
        if it < 5:
            m = _sdot(m, m)
    u_val, w_key = x[:, :, :d], x[:, :, d:]
    attn = _ldot(qn, kn, _BNT) * decay
    q_dec = qn * egc
    k_dec = kn * jnp.exp(glast - gcum)
    v_new = u_val - _ldot(w_key, S)
    out = _ldot(q_dec, S) + _ldot(attn, v_new)
    s_new = S * jnp.exp(glast) + _ldot(k_dec, v_new, _BTN)
    o = out * lax.rsqrt(jnp.mean(out * out, axis=-1, keepdims=True) + RMS_EPS) * nw
    o = o * jax.nn.silu(z)
    return s_new, o


def _heads_per_step(NH, HD, zcol0):
    for hb in (4, 2):
        if NH % hb == 0 and zcol0 % (hb * HD) == 0:
            return hb
    return 1


def _gdn_fwd(name, qkv, p, zcol0, blt, aint, alog, dtb, nw, NH, HD):
    T = qkv.shape[0]
    N = T // CHUNK
    GW = NH * HD
    HB = _heads_per_step(NH, HD, zcol0)
    W = HB * HD
    zc0 = zcol0 // W
    nb = GW // W

    def body(q_ref, k_ref, v_ref, z_ref, bl_ref, ain_ref, al_ref, dtb_ref, nw_ref, o_ref, ssave_ref, s_scr):
        n = pl.program_id(1)

        @pl.when(n == 0)
        def _():
            s_scr[...] = jnp.zeros_like(s_scr)

        heads = lambda r: jnp.stack([r[:, hh * HD:(hh + 1) * HD] for hh in range(HB)], axis=0)
        s_in = s_scr[...]
        ssave_ref[...] = s_in
        s_new, o = _gdn_chunk(s_in, heads(q_ref), heads(k_ref), heads(v_ref), heads(z_ref), bl_ref[...], ain_ref[...],
                              al_ref[...], dtb_ref[...], nw_ref[...])
        s_scr[...] = s_new
        for hh in range(HB):
            o_ref[:, hh * HD:(hh + 1) * HD] = o[hh].astype(o_ref.dtype)

    ch = lambda off: pl.BlockSpec((CHUNK, W), lambda h, n: (n, h + off))
    sc = pl.BlockSpec((HB, CHUNK, 1), lambda h, n: (h, n, 0))
    hs = pl.BlockSpec((HB, 1, 1), lambda h, n: (h, 0, 0))
    return pl.pallas_call(
        body, grid=(NH // HB, N),
        in_specs=[ch(0), ch(nb), ch(2 * nb), ch(zc0), sc, sc, hs, hs, pl.BlockSpec((1, HD), lambda h, n: (0, 0))],
        out_specs=[pl.BlockSpec((CHUNK, W), lambda h, n: (n, h)),
                   pl.BlockSpec((HB, None, HD, HD), lambda h, n: (h, n, 0, 0))],
        out_shape=[SDS((T, GW), _MXU), SDS((NH, N, HD, HD), f32)],
        scratch_shapes=[pltpu.VMEM((HB, HD, HD), f32)], compiler_params=_cp(("parallel", "arbitrary")), name=name,
    )(qkv, qkv, qkv, p, blt, aint, alog, dtb, nw)


def _gdn_bwd(name, qkv, p, zcol0, blt, aint, alog, dtb, nw, ssave, do, NH, HD):
    T = qkv.shape[0]
    N = T // CHUNK
    GW = NH * HD
    HB = _heads_per_step(NH, HD, zcol0)
    W = HB * HD
    zc0 = zcol0 // W
    nb = GW // W

    def body(q_ref, k_ref, v_ref, z_ref, bl_ref, ain_ref, al_ref, dtb_ref, nw_ref, ss_ref, do_ref,
             dqkv_ref, dz_ref, dbl_ref, dain_ref, dal_ref, ddtb_ref, dnw_ref, ds_scr):
        h = pl.program_id(0)
        n = pl.program_id(1)

        @pl.when(n == 0)
        def _():
            ds_scr[...] = jnp.zeros_like(ds_scr)

        heads = lambda r: jnp.stack([r[:, hh * HD:(hh + 1) * HD] for hh in range(HB)], axis=0)
        _, vjp = jax.vjp(_gdn_chunk, ss_ref[...], heads(q_ref), heads(k_ref), heads(v_ref), heads(z_ref), bl_ref[...],
                         ain_ref[...], al_ref[...], dtb_ref[...], nw_ref[...])
        ds, dq, dk, dv, dz, dbl, dain, dal, ddtb, dnw = vjp((ds_scr[...], heads(do_ref).astype(f32)))
        ds_scr[...] = ds
        for hh in range(HB):
            cs = slice(hh * HD, (hh + 1) * HD)
            dqkv_ref[0, :, cs] = dq[hh]
            dqkv_ref[1, :, cs] = dk[hh]
            dqkv_ref[2, :, cs] = dv[hh]
            dz_ref[:, cs] = dz[hh].astype(dz_ref.dtype)
        dbl_ref[...] = dbl
        dain_ref[...] = dain

        @pl.when(n == 0)
        def _():
            dal_ref[...] = dal
            ddtb_ref[...] = ddtb

        @pl.when(n > 0)
        def _():
            dal_ref[...] += dal
            ddtb_ref[...] += ddtb

        @pl.when((n == 0) & (h == 0))
        def _():
            dnw_ref[...] = dnw

        @pl.when((n > 0) | (h > 0))
        def _():
            dnw_ref[...] += dnw

    R = N - 1
    ch = lambda off: pl.BlockSpec((CHUNK, W), lambda h, n: (R - n, h + off))
    sc = pl.BlockSpec((HB, CHUNK, 1), lambda h, n: (h, R - n, 0))
    hs = pl.BlockSpec((HB, 1, 1), lambda h, n: (h, 0, 0))
    nws = pl.BlockSpec((1, HD), lambda h, n: (0, 0))
    return pl.pallas_call(
        body, grid=(NH // HB, N),
        in_specs=[ch(0), ch(nb), ch(2 * nb), ch(zc0), sc, sc, hs, hs, nws,
                  pl.BlockSpec((HB, None, HD, HD), lambda h, n: (h, R - n, 0, 0)),
                  pl.BlockSpec((CHUNK, W), lambda h, n: (R - n, h))],
        out_specs=[pl.BlockSpec((3, CHUNK, W), lambda h, n: (0, R - n, h)),
                   pl.BlockSpec((CHUNK, W), lambda h, n: (R - n, h)), sc, sc, hs, hs, nws],
        out_shape=[SDS((3, T, GW), f32), SDS((T, GW), _MXU), SDS((NH, T, 1), f32), SDS((NH, T, 1), f32),
                   SDS((NH, 1, 1), f32), SDS((NH, 1, 1), f32), SDS((1, HD), f32)],
        scratch_shapes=[pltpu.VMEM((HB, HD, HD), f32)], compiler_params=_cp(("arbitrary", "arbitrary")), name=name,
    )(qkv, qkv, qkv, p, blt, aint, alog, dtb, nw, ssave, do)


def _loss_head(name, y, tgt):
    T, D = y.shape
    tm = _rtile(T, 256)

    def body(y_ref, t_ref, dy_ref, l_ref):
        i = pl.program_id(0)
        err = y_ref[...] - t_ref[...]
        dy_ref[...] = err * (1.0 / D)
        part = 0.5 * jnp.sum(jnp.sum(err * err, axis=-1, keepdims=True) * (1.0 / D), axis=0, keepdims=True)

        @pl.when(i == 0)
        def _():
            l_ref[...] = part

        @pl.when(i > 0)
        def _():
            l_ref[...] += part

    row = pl.BlockSpec((tm, D), lambda i: (i, 0))
    return pl.pallas_call(
        body, grid=(T // tm,), in_specs=[row, row], out_specs=[row, pl.BlockSpec((1, 1), lambda i: (0, 0))],
        out_shape=[SDS((T, D), f32), SDS((1, 1), f32)], compiler_params=_cp(("arbitrary",)), name=name,
    )(y, tgt)


def _adam_math(w, g, m, v):
    m = ADAM_B1 * m + (1.0 - ADAM_B1) * g
    v = ADAM_B2 * v + (1.0 - ADAM_B2) * jnp.square(g)
    m_hat = m / (1.0 - ADAM_B1 ** ADAM_STEP)
    v_hat = v / (1.0 - ADAM_B2 ** ADAM_STEP)
    delta = -ADAM_LR * (m_hat / (jnp.sqrt(v_hat) + ADAM_EPS) + ADAM_WD * w)
    return delta, m, v


def _add_mine(name, full, recv, me, out_dtype):
    N, _, R, C = full.shape
    tr = _rtile(R, max(16, (1 << 19) // max(C, 1) // 16 * 16))

    def body(me_ref, a_ref, b_ref, o_ref):
        o_ref[...] = (a_ref[...].astype(f32) + b_ref[...].astype(f32)).astype(o_ref.dtype)

    blk = pl.BlockSpec((None, tr, C), lambda n, i, me_ref: (n, i, 0))
    return pl.pallas_call(
        body,
        grid_spec=pltpu.PrefetchScalarGridSpec(
            num_scalar_prefetch=1, grid=(N, R // tr),
            in_specs=[pl.BlockSpec((None, None, tr, C), lambda n, i, me_ref: (n, me_ref[0], i, 0)), blk], out_specs=blk),
        out_shape=SDS((N, R, C), out_dtype), compiler_params=_cp(("parallel", "parallel")), name=name,
    )(me, full, recv)


def _adamw_big(name, full, recv, me, w, m, v, l, accs):
    _, R, C = full.shape
    L = w.shape[0]
    tr = _rtile(R, max(16, (1 << 18) // max(C, 1) // 16 * 16))

    def body(me_ref, ga_ref, gb_ref, w_ref, m_ref, v_ref, a0, a1, a2, a3, g_ref, d_ref, nm_ref, nv_ref):
        g = ga_ref[...].astype(f32) + gb_ref[...].astype(f32)
        d, nm, nv = _adam_math(w_ref[...], g, m_ref[...], v_ref[...])
        g_ref[...] = g
        d_ref[...] = d
        nm_ref[...] = nm
        nv_ref[...] = nv

    blk = pl.BlockSpec((tr, C), lambda i, me_ref: (i, 0))
    lblk = pl.BlockSpec((None, tr, C), lambda i, me_ref: (l, i, 0))
    untouched = pl.BlockSpec(memory_space=pl.ANY)
    return pl.pallas_call(
        body,
        grid_spec=pltpu.PrefetchScalarGridSpec(
            num_scalar_prefetch=1, grid=(R // tr,),
            in_specs=[pl.BlockSpec((None, tr, C), lambda i, me_ref: (me_ref[0], i, 0)), blk, lblk, lblk, lblk] + [untouched] * 4,
            out_specs=[lblk] * 4),
        out_shape=[SDS((L, R, C), f32)] * 4, input_output_aliases={6: 0, 7: 1, 8: 2, 9: 3},
        compiler_params=_cp(("parallel",)), name=name,
    )(me, full, recv, w, m, v, *accs)


def _adamw_small(name, gall, w, m, v):
    _, R, C = gall.shape
    tr = _rtile(R, 512)

    def body(ga_ref, w_ref, m_ref, v_ref, g_ref, d_ref, nm_ref, nv_ref):
        g = ga_ref[0]
        for s in range(1, 8):
            g = g + ga_ref[s]
        d, nm, nv = _adam_math(w_ref[...], g, m_ref[...], v_ref[...])
        g_ref[...] = g
        d_ref[...] = d
        nm_ref[...] = nm
        nv_ref[...] = nv

    blk = pl.BlockSpec((tr, C), lambda i: (i, 0))
    return pl.pallas_call(
        body, grid=(R // tr,), in_specs=[pl.BlockSpec((8, tr, C), lambda i: (0, i, 0)), blk, blk, blk], out_specs=[blk] * 4,
        out_shape=[SDS((R, C), f32)] * 4, compiler_params=_cp(("parallel",)), name=name,
    )(gall, w, m, v)


def _peer(axis):
    x, y, c = lax.axis_index("x"), lax.axis_index("y"), lax.axis_index("c")
    me = {"x": x, "y": y, "c": c}[axis]
    peer = {"x": (1 - x, y, c), "y": (x, 1 - y, c), "c": (x, y, 1 - c)}[axis]
    return me, peer


def _held(ref, done):
    idx = tuple(slice(None) if a in done else lax.axis_index(a) for a in ("x", "y", "c"))
    return ref.at[idx]


def _gather_stage(name, bufs, axes, dones):
    n = len(bufs)
    hbm = pl.BlockSpec(memory_space=pltpu.HBM)

    def body(*refs):
        outs = refs[n:2 * n]
        send_sems, recv_sems = refs[2 * n:]
        cps = []
        for t in range(n):
            _, peer = _peer(axes[t])
            blk = _held(outs[t], dones[t])
            cps.append(pltpu.make_async_remote_copy(src_ref=blk, dst_ref=blk, send_sem=send_sems.at[t],
                                                    recv_sem=recv_sems.at[t], device_id=peer, device_id_type=_MESH_T))
        for cp in cps:
            cp.start()
        for cp in cps:
            cp.wait()

    return pl.pallas_call(
        body, in_specs=[hbm] * n, out_specs=[hbm] * n, out_shape=[SDS(b.shape, b.dtype) for b in bufs],
        input_output_aliases={t: t for t in range(n)},
        scratch_shapes=[pltpu.SemaphoreType.DMA((n,)), pltpu.SemaphoreType.DMA((n,))], name=name,
    )(*bufs)


_HBM = pl.BlockSpec(memory_space=pltpu.HBM)
_SEM = pl.BlockSpec(memory_space=pltpu.SEMAPHORE)
_EFFECT = pltpu.SideEffectType.DATAFLOW_SIDE_EFFECTING


def _split_start(name, arrays, n_copies, make_copies):
    na = len(arrays)

    def body(*refs):
        ins = refs[:na]
        send_sems, recv_sems = refs[na], refs[na + 1]
        token = refs[2 * na + 2]
        for cp in make_copies(ins, send_sems, recv_sems):
            cp.start()
        token[...] = jnp.zeros_like(token)

    res = pl.pallas_call(
        body, name=name,
        out_shape=(pltpu.SemaphoreType.DMA((n_copies,)), pltpu.SemaphoreType.DMA((n_copies,)),
                   *[pltpu.HBM(a.shape, a.dtype) for a in arrays], SDS((8, 128), f32)),
        in_specs=[_HBM] * na, out_specs=(_SEM, _SEM, *[_HBM] * na, pl.BlockSpec(memory_space=pltpu.VMEM)),
        input_output_aliases={i: 2 + i for i in range(na)},
        compiler_params=pltpu.CompilerParams(has_side_effects=_EFFECT),
    )(*[pltpu.with_memory_space_constraint(a, pltpu.HBM) for a in arrays])
    return res[0], res[1], list(res[2:2 + na]), res[2 + na]


def _split_wait(name, arrays, send_sems, recv_sems, after, make_copies):
    na = len(arrays)

    def body(*refs):
        ins = refs[:na]
        for cp in make_copies(ins, refs[na], refs[na + 1]):
            cp.wait_send()
            cp.wait_recv()

    res = pl.pallas_call(
        body, name=name, out_shape=tuple(pltpu.HBM(a.shape, a.dtype) for a in arrays),
        in_specs=[_HBM] * na + [_SEM, _SEM, pl.BlockSpec(memory_space=pl.ANY)], out_specs=tuple([_HBM] * na),
        input_output_aliases={i: i for i in range(na)},
        compiler_params=pltpu.CompilerParams(has_side_effects=_EFFECT),
    )(*arrays, send_sems, recv_sems, after)
    return list(res)


def _gather_copies(axes, dones):
    def make(refs, send_sems, recv_sems):
        cps = []
        for t in range(len(axes)):
            _, peer = _peer(axes[t])
            blk = _held(refs[t], dones[t])
            cps.append(pltpu.make_async_remote_copy(src_ref=blk, dst_ref=blk, send_sem=send_sems.at[t],
                                                    recv_sem=recv_sems.at[t], device_id=peer, device_id_type=_MESH_T))
        return cps
    return make


def _scatter_copies(axes):
    n = len(axes)

    def make(refs, send_sems, recv_sems):
        cps = []
        for t in range(n):
            me, peer = _peer(axes[t])
            cps.append(pltpu.make_async_remote_copy(
                src_ref=refs[t].at[:, 1 - me], dst_ref=refs[n + t], send_sem=send_sems.at[t], recv_sem=recv_sems.at[t],
                device_id=peer, device_id_type=_MESH_T))
        return cps
    return make


class _AsyncGather:
    def __init__(self, pfx, tensors, paths):
        x, y, c = (lax.axis_index(a) for a in ("x", "y", "c"))
        self.pfx, self.shapes = pfx, [tuple(t.shape) for t in tensors]
        self.bufs = [lax.dynamic_update_slice(lax.empty((2, 2, 2) + tuple(t.shape), t.dtype), t[None, None, None],
                                              (x, y, c) + (0,) * t.ndim) for t in tensors]
        self.orders = [tuple(p) + ("c",) for p in paths]
        self.ph = 0

    def _make(self):
        return _gather_copies([o[self.ph] for o in self.orders], [o[:self.ph] for o in self.orders])

    def start(self):
        self.ss, self.rs, self.bufs, tok = _split_start(f"{self.pfx}_start{self.ph}", self.bufs, len(self.bufs), self._make())
        return tok

    def wait(self, after):
        self.bufs = _split_wait(f"{self.pfx}_wait{self.ph}", self.bufs, self.ss, self.rs, after, self._make())
        self.ph += 1

    def result(self):
        return [b.reshape((8,) + s) for b, s in zip(self.bufs, self.shapes)]


class _AsyncReduceScatter:
    def __init__(self, pfx, tensors, paths):
        self.pfx = pfx
        self.rcs = [tuple(t.shape[1:]) for t in tensors]
        self.orders = [("c",) + tuple(p) for p in paths]
        self.left = [["x", "y", "c"] for _ in tensors]
        self.cur = list(tensors)
        self.ph = 0

    def start(self):
        n = len(self.cur)
        views = []
        for i, (t, rc) in enumerate(zip(self.cur, self.rcs)):
            pos = self.left[i].index(self.orders[i][self.ph])
            nb, na = 2 ** pos, 2 ** (len(self.left[i]) - pos - 1)
            views.append(t.reshape((nb, 2, na * rc[0], rc[1])))
        lands = [lax.empty((v.shape[0],) + tuple(v.shape[2:]), v.dtype) for v in views]
        self.make = _scatter_copies([o[self.ph] for o in self.orders])
        self.ss, self.rs, arrs, tok = _split_start(f"{self.pfx}_start{self.ph}", views + lands, n, self.make)
        self.arrs = arrs
        return tok

    def wait(self, after):
        n = len(self.cur)
        arrs = _split_wait(f"{self.pfx}_wait{self.ph}", self.arrs, self.ss, self.rs, after, self.make)
        views, recvs = arrs[:n], arrs[n:]
        ph = self.ph
        if ph == 2:
            self.out = [(v[0], r[0], o[2]) for v, r, o in zip(views, recvs, self.orders)]
        else:
            self.cur = [_add_mine(f"{self.pfx}_add{ph}_{i}", v, r, _coord(o[ph]), v.dtype)
                        for i, (v, r, o) in enumerate(zip(views, recvs, self.orders))]
            for i, o in enumerate(self.orders):
                self.left[i].remove(o[ph])
        self.ph += 1


def _coord(axis):
    return lax.axis_index(axis).astype(jnp.int32).reshape(1)


def _all_gather(pfx, tensors, paths):
    x, y, c = (lax.axis_index(a) for a in ("x", "y", "c"))
    bufs = []
    for t in tensors:
        zero = (0,) * t.ndim
        bufs.append(lax.dynamic_update_slice(lax.empty((2, 2, 2) + tuple(t.shape), t.dtype), t[None, None, None],
                                             (x, y, c) + zero))
    orders = [tuple(p) + ("c",) for p in paths]
    for ph in range(3):
        bufs = _gather_stage(f"{pfx}_{ph}", bufs, [o[ph] for o in orders], [o[:ph] for o in orders])
    return [b.reshape((8,) + tuple(t.shape)) for b, t in zip(bufs, tensors)]


def _mm_nn(name, a, w, out_dtype, tn_pref=1024):
    T, K = a.shape
    N = w.shape[1]
    tm, tn, tk = _rtile(T, 512), _tile(N, tn_pref), _tile(K, 2048)
    return _mm(
        name, a, w, _NN, (T // tm, N // tn, K // tk),
        pl.BlockSpec((tm, tk), lambda i, j, k: (i, k)), pl.BlockSpec((tk, tn), lambda i, j, k: (k, j)), (tm, tn),
        [], [], [SDS((T, N), out_dtype)], [pl.BlockSpec((tm, tn), lambda i, j, k: (i, j))], _store_epi,
    )[0]


def _mm_tn(name, a, b, out_dtype):
    T, M = a.shape
    N = b.shape[1]
    tm, tn, tk = _tile(M, 512), _tile(N, 2048), _rtile(T, 512)
    return _mm(
        name, a, b, _TN, (M // tm, N // tn, T // tk),
        pl.BlockSpec((tk, tm), lambda i, j, k: (k, i)), pl.BlockSpec((tk, tn), lambda i, j, k: (k, j)), (tm, tn),
        [], [], [SDS((M, N), out_dtype)], [pl.BlockSpec((tm, tn), lambda i, j, k: (i, j))], _store_epi,
    )[0]


def _mm_tn_slots(name, a, b, out_dtype):
    T, M = a.shape
    NS = b.shape[1] // 8
    tm, tk = _tile(M, 512), _rtile(T, 512)
    return _mm(
        name, a, b, _TN, (M // tm, 8, T // tk),
        pl.BlockSpec((tk, tm), lambda i, j, k: (k, i)), pl.BlockSpec((tk, NS), lambda i, j, k: (k, j)), (tm, NS),
        [], [], [SDS((8, M, NS), out_dtype)], [pl.BlockSpec((None, tm, NS), lambda i, j, k: (j, i, 0))], _store_epi,
    )[0]


def _mm_nt_slots(name, a, w8, out_dtype):
    T = a.shape[0]
    _, M, NS = w8.shape
    tm, tn = _rtile(T, 512), _tile(M, 1024)
    return _mm(
        name, a, w8, _NT, (T // tm, M // tn, 8),
        pl.BlockSpec((tm, NS), lambda i, j, k: (i, k)), pl.BlockSpec((None, tn, NS), lambda i, j, k: (k, j, 0)), (tm, tn),
        [], [], [SDS((T, M), out_dtype)], [pl.BlockSpec((tm, tn), lambda i, j, k: (i, j))], _store_epi,
    )[0]


def _colsum_kernel(name, fn, ins, in_cols, outs_elem, n_sum, C):
    T = ins[0].shape[0]
    tm = _rtile(T, 256)
    ne = len(outs_elem)

    def body(*refs):
        i = pl.program_id(0)
        iv = [r[...] for r in refs[:len(ins)]]
        res = fn(*iv)
        for o, r in zip(refs[len(ins):len(ins) + ne], res[:ne]):
            o[...] = r.astype(o.dtype)
        sums = [jnp.sum(r, axis=0, keepdims=True) for r in res[ne:]]

        @pl.when(i == 0)
        def _():
            for o, s in zip(refs[len(ins) + ne:], sums):
                o[...] = s

        @pl.when(i > 0)
        def _():
            for o, s in zip(refs[len(ins) + ne:], sums):
                o[...] += s

    in_specs = []
    for arr, off in zip(ins, in_cols):
        if off is None:
            in_specs.append(pl.BlockSpec((1, C), lambda i: (0, 0)))
        else:
            in_specs.append(pl.BlockSpec((tm, C), lambda i, off=off: (i, off)))
    row = pl.BlockSpec((tm, C), lambda i: (i, 0))
    vec = pl.BlockSpec((1, C), lambda i: (0, 0))
    return pl.pallas_call(
        body, grid=(T // tm,), in_specs=in_specs, out_specs=[row] * ne + [vec] * n_sum,
        out_shape=[SDS((T, C), dt) for dt in outs_elem] + [SDS((1, C), f32)] * n_sum,
        compiler_params=_cp(("arbitrary",)), name=name,
    )(*ins)


def _merge(gs, gg, a_s, a_g):
    return jax.nn.sigmoid(gs) * a_s + jax.nn.sigmoid(gg) * a_g


def _glu(yg, lp):
    return yg * jax.nn.sigmoid(lp)


_BIG = ("ffn1_w_gu", "ffn1_w_down", "w_in", "conv_w", "glu_w", "w_br_ssm", "w_br_gdn", "w_out", "ffn2_w_gu", "ffn2_w_down")
_PATHS = ("yx", "yx", "xy", "xy", "yx", "yx", "yx", "yx", "xy", "xy")
_SMALL = ("ln1_g", "ln1_b", "ssm_a_re", "ssm_a_im", "ssm_log_dt", "ssm_b_re", "ssm_b_im", "ssm_c_re", "ssm_c_im", "ssm_d",
          "glu_b", "gdn_a_log", "gdn_dt_bias", "gdn_norm_w", "ln2_g", "ln2_b", "ln3_g", "ln3_b")
_ORDER = ("ffn1_w_gu", "ffn1_w_down", "ln1_g", "ln1_b", "w_in", "conv_w", "ssm_a_re", "ssm_a_im", "ssm_log_dt", "ssm_b_re",
          "ssm_b_im", "ssm_c_re", "ssm_c_im", "ssm_d", "glu_w", "glu_b", "gdn_a_log", "gdn_dt_bias", "gdn_norm_w", "w_br_ssm",
          "w_br_gdn", "w_out", "ln2_g", "ln2_b", "ffn2_w_gu", "ffn2_w_down", "ln3_g", "ln3_b")


def _step(x, tgt, W, M, V):
    T, D = x.shape[1], x.shape[2]
    L = W["ffn1_w_gu"].shape[0]
    G, P = W["ssm_a_re"].shape[1:]
    H = W["ssm_b_re"].shape[3]
    SW = G * H
    NH = W["gdn_a_log"].shape[1]
    HD = W["gdn_norm_w"].shape[1]
    GW = NH * HD
    KC = W["conv_w"].shape[1]
    DS = D // 8
    alpha = (2.0 * L) ** 0.25
    o_b = SW + 4 * GW
    o_gs = o_b + 2 * NH
    IN = o_gs + 2 * D
    NM = IN - 2 * NH
    m_qkv, m_z, m_gs, m_gg = SW, SW + 3 * GW, SW + 4 * GW, SW + 4 * GW + D
    J = G // 8

    x0 = x[0]
    tg = tgt[0]

    def vec(name, l):
        return W[name][l:l + 1]

    saves, weights = [], []
    xc, xcb = x0, x0.astype(_MXU)
    def shards(l):
        return [W["ffn1_w_gu"][l].astype(_MXU), W["ffn1_w_down"][l].astype(_MXU), W["w_in"][l].astype(_MXU), W["conv_w"][l],
                W["glu_w"][l].astype(_MXU), W["w_br_ssm"][l].astype(_MXU), W["w_br_gdn"][l].astype(_MXU),
                W["w_out"][l].astype(_MXU), W["ffn2_w_gu"][l].astype(_MXU), W["ffn2_w_down"][l].astype(_MXU)]

    def dep(a, tok):
        return a if tok is None else a + tok[0:1, 0:1].astype(a.dtype)

    gathered = _all_gather("ag", shards(0), _PATHS)
    ahead = {}
    for l in range(L):
        toks = []
        for k in ([1, 2] if l == 0 else [l + 2]):
            if k < L:
                ahead[k] = _AsyncGather("agp", shards(k), _PATHS)
                toks.append(ahead[k].start())
        nxt = ahead.get(l + 1)
        if nxt and l >= 1:
            nxt.wait(xc)
            toks.append(nxt.start())
        tok = functools.reduce(lambda a, b: a + b, toks) if toks else None
        wgu1, wd1, win8, cw8, wglu, wbs, wbg, wo, wgu2, wd2 = gathered
        wd1 = wd1.reshape(-1, D)
        wd2 = wd2.reshape(-1, D)
        wglu = wglu.reshape(SW, SW)
        wo = wo.reshape(D, D)
        win = jnp.transpose(win8, (1, 0, 2)).reshape(D, IN)
        wmain = jnp.concatenate([win[:, :o_b], win[:, o_gs:]], axis=1)
        wba = jnp.pad(win[:, o_b:o_gs], ((0, 0), (0, 128 - 2 * NH)))
        cw = jnp.transpose(cw8, (1, 0, 2)).reshape(KC, 3 * GW)
        wl = dict(wgu1=wgu1, wd1=wd1, wmain=wmain, wba=wba, cw=cw, wglu=wglu, wbs=wbs, wbg=wbg, wo=wo, wgu2=wgu2, wd2=wd2)
        weights.append(wl)
        sv = {}

        gate, up, hh = _ffn_up("ffn_up", xcb, wgu1)
        x1, x1b, xh1, r1 = _mm_ln("ffn_down_ln", hh, wd1, xc, dep(vec("ln1_g", l), tok), vec("ln1_b", l), alpha, 0.5)
        sv["f1"] = dict(xb=xcb, gate=gate, up=up, h=hh, xhat=xh1, rstd=r1)

        p = _mm_nn("mix_in", x1b, wmain, f32)
        pba = _mm_nn("mix_in_ba", x1b, wba, f32)
        b_re_t = jnp.transpose(W["ssm_b_re"][l], (2, 0, 1))
        b_im_t = jnp.transpose(W["ssm_b_im"][l], (2, 0, 1))
        zoh_in = (W["ssm_a_re"][l], W["ssm_a_im"][l], W["ssm_log_dt"][l][:, None], b_re_t, b_im_t)
        lbr, lbi, bbr_t, bbi_t = _zoh_fwd("zoh", *zoh_in)
        bblk_r = _blockdiag(jnp.transpose(bbr_t, (1, 0, 2)))
        bblk_i = _blockdiag(jnp.transpose(bbi_t, (1, 0, 2)))
        cblkT_r = _blockdiag(W["ssm_c_re"][l])
        cblkT_in = _blockdiag(-W["ssm_c_im"][l])
        lbr_f, lbi_f = lbr.reshape(1, G * P), lbi.reshape(1, G * P)
        bur, bui = _bd2("s5_bu", p, 0, bblk_r, bblk_i)
        if nxt and l == 0:
            nxt.wait(bur)
            tok = nxt.start()
        sr, si = _s5_scan("s5_scan", bur, bui, dep(lbr_f, tok), lbi_f)
        dflat = W["ssm_d"][l].reshape(1, SW)

        def out_epi(acc, ex, outs):
            y_raw = acc + ex[1][...] * ex[0][...]
            yg = jax.nn.gelu(y_raw)
            outs[0][...] = y_raw
            outs[1][...] = yg
            outs[2][...] = yg.astype(outs[2].dtype)

        y_raw, yg, ygb = _bd_sum(
            "s5_out", sr, si, jnp.transpose(cblkT_r, (0, 2, 1)), jnp.transpose(cblkT_in, (0, 2, 1)), [p, dflat],
            lambda tm, nb: [pl.BlockSpec((tm, nb), lambda i, j: (i, j)), pl.BlockSpec((1, nb), lambda i, j: (0, j))],
            [SDS((T, SW), f32), SDS((T, SW), f32), SDS((T, SW), _MXU)], out_epi)

        tmg, tng, tkg = _rtile(T, 512), _tile(SW, 512), _tile(SW, 1024)

        def glu_epi(acc, ex, outs):
            lp = acc + ex[1][...]
            outs[0][...] = lp
            outs[1][...] = _glu(ex[0][...], lp).astype(outs[1].dtype)

        lp, ysb = _mm(
            "s5_glu", ygb, wglu, _NN, (T // tmg, SW // tng, SW // tkg),
            pl.BlockSpec((tmg, tkg), lambda i, j, k: (i, k)), pl.BlockSpec((tkg, tng), lambda i, j, k: (k, j)), (tmg, tng),
            [yg, vec("glu_b", l)], [pl.BlockSpec((tmg, tng), lambda i, j, k: (i, j)), pl.BlockSpec((1, tng), lambda i, j, k: (0, j))],
            [SDS((T, SW), f32), SDS((T, SW), _MXU)], [pl.BlockSpec((tmg, tng), lambda i, j, k: (i, j))] * 2, glu_epi)

        qkv = _conv_fwd("gdn_conv", p, m_qkv, cw, 3 * GW)
        blt = jnp.transpose(pba[:, :NH])[:, :, None]
        aint = jnp.transpose(pba[:, NH:2 * NH])[:, :, None]
        alog = W["gdn_a_log"][l].reshape(NH, 1, 1)
        dtb = W["gdn_dt_bias"][l].reshape(NH, 1, 1)
        nw = vec("gdn_norm_w", l)
        og, ssave = _gdn_fwd("gdn", qkv, p, m_z, blt, aint, alog, dtb, nw, NH, HD)

        a_s = _mm(
            "br_ssm", ysb, wbs, _NN, (T // tmg, 8, SW // tkg),
            pl.BlockSpec((tmg, tkg), lambda i, j, k: (i, k)), pl.BlockSpec((None, tkg, DS), lambda i, j, k: (j, k, 0)), (tmg, DS),
            [], [], [SDS((T, D), f32)], [pl.BlockSpec((tmg, DS), lambda i, j, k: (i, j))], _store_epi)[0]
        tkd = _tile(GW, 1024)
        gsb, ggb = m_gs // DS, m_gg // DS

        def merge_epi(acc, ex, outs):
            outs[0][...] = acc
            outs[1][...] = _merge(ex[1][...], ex[2][...], ex[0][...], acc).astype(outs[1].dtype)

        tile_ij = pl.BlockSpec((tmg, DS), lambda i, j, k: (i, j))
        a_g, merged = _mm(
            "br_gdn_merge", og, wbg, _NN, (T // tmg, 8, GW // tkd),
            pl.BlockSpec((tmg, tkd), lambda i, j, k: (i, k)), pl.BlockSpec((None, tkd, DS), lambda i, j, k: (j, k, 0)), (tmg, DS),
            [a_s, p, p], [tile_ij, pl.BlockSpec((tmg, DS), lambda i, j, k: (i, j + gsb)),
                          pl.BlockSpec((tmg, DS), lambda i, j, k: (i, j + ggb))],
            [SDS((T, D), f32), SDS((T, D), _MXU)], [tile_ij, tile_ij], merge_epi)
        if nxt:
            nxt.wait(merged)
            tok = nxt.start()
        x2, x2b, xh2, r2 = _mm_ln("mix_out_ln", merged, wo, x1, dep(vec("ln2_g", l), tok), vec("ln2_b", l), alpha, 1.0)
        sv["mx"] = dict(x1b=x1b, p=p, zoh_in=zoh_in, lbr_f=lbr_f, lbi_f=lbi_f, bblk_r=bblk_r, bblk_i=bblk_i, cblkT_r=cblkT_r,
                        cblkT_in=cblkT_in, sr=sr, si=si, dflat=dflat, y_raw=y_raw, yg=yg, ygb=ygb, lp=lp, ysb=ysb, qkv=qkv,
                        blt=blt, aint=aint, alog=alog, dtb=dtb, nw=nw, og=og, ssave=ssave, a_s=a_s, a_g=a_g, merged=merged,
                        xhat=xh2, rstd=r2)

        gate2, up2, hh2 = _ffn_up("ffn_up", x2b, wgu2)
        x3, x3b, xh3, r3 = _mm_ln("ffn_down_ln", hh2, wd2, x2, vec("ln3_g", l), vec("ln3_b", l), alpha, 0.5)
        sv["f2"] = dict(xb=x2b, gate=gate2, up=up2, h=hh2, xhat=xh3, rstd=r3)
        saves.append(sv)
        xc, xcb = x3, x3b
        if nxt:
            nxt.wait(x3)
            gathered = nxt.result()

    dy, loss_part = _loss_head("loss_head", xc, tg)
    loss = lax.psum(loss_part[0, 0], ("x", "y", "c"))

    big_out = {n: [lax.empty(W[n].shape, f32) for _ in range(4)] for n in _BIG}
    small_g = {n: [None] * L for n in _SMALL}
    pend = None
    for l in reversed(range(L)):
        sv, wl = saves[l], weights[l]
        mx = sv["mx"]
        p = mx["p"]
        tok = pend.start() if pend else None
        dx2, dwgu2, dwd2, dg3, db3 = _ffn_bwd("ffn_b", dy, sv["f2"], wl["wgu2"], wl["wd2"], dep(vec("ln3_g", l), tok), alpha)
        small_g["ln3_g"][l], small_g["ln3_b"][l] = dg3[0], db3[0]

        if pend:
            pend.wait(dx2)
            tok = pend.start()
        dz2, dmixb, dg2, db2 = _ln_bwd("mix_lnb", dx2, mx["xhat"], mx["rstd"], dep(vec("ln2_g", l), tok), 1.0)
        small_g["ln2_g"][l], small_g["ln2_b"][l] = dg2[0], db2[0]
        tmg, tkd = _rtile(T, 512), _tile(D, 512)
        tnq = 512 if (m_gs % 512 == 0 and D % 512 == 0) else DS
        tkq = _tile(D, 2048)
        gsb, ggb = m_gs // tnq, m_gg // tnq

        def dmerge_epi(acc, ex, outs):
            _, vjp = jax.vjp(_merge, ex[0][...], ex[1][...], ex[2][...], ex[3][...])
            dgs, dgg, das, dag = vjp(acc)
            outs[0][...] = das.astype(outs[0].dtype)
            outs[1][...] = dag.astype(outs[1].dtype)
            outs[2][...] = dgs.astype(outs[2].dtype)
            outs[3][...] = dgg.astype(outs[3].dtype)

        tile_ij = pl.BlockSpec((tmg, tnq), lambda i, j, k: (i, j))
        das, dag, dgs, dgg = _mm(
            "mix_dmerge", dmixb, wl["wo"], _NT, (T // tmg, D // tnq, D // tkq),
            pl.BlockSpec((tmg, tkq), lambda i, j, k: (i, k)), pl.BlockSpec((tnq, tkq), lambda i, j, k: (j, k)), (tmg, tnq),
            [p, p, mx["a_s"], mx["a_g"]],
            [pl.BlockSpec((tmg, tnq), lambda i, j, k: (i, j + gsb)), pl.BlockSpec((tmg, tnq), lambda i, j, k: (i, j + ggb)),
             tile_ij, tile_ij],
            [SDS((T, D), _MXU)] * 4, [tile_ij] * 4, dmerge_epi)
        dwo = _mm_tn("mix_dwo", mx["merged"], dmixb, _GDT)
        dys = _mm_nt_slots("br_ssm_dx", das, wl["wbs"], f32)
        dog = _mm_nt_slots("br_gdn_dx", dag, wl["wbg"], f32)
        dwbs = _mm_tn_slots("br_ssm_dw", mx["ysb"], das, _GDT)
        dwbg = _mm_tn_slots("br_gdn_dw", mx["og"], dag, _GDT)

        def glu_b_fn(dys_t, yg_t, lp_t):
            _, vjp = jax.vjp(_glu, yg_t, lp_t)
            dyg1, dlp = vjp(dys_t)
            return dyg1, dlp, dlp

        dyg1, dlpb, dglub = _colsum_kernel("s5_glu_b", glu_b_fn, [dys, mx["yg"], mx["lp"]], [0, 0, 0], [f32, _MXU], 1, SW)
        small_g["glu_b"][l] = dglub[0]
        dwglu = _mm_tn("s5_dwglu", mx["ygb"], dlpb, _GDT)
        tng, tkg = _tile(SW, 512), _tile(SW, 512)

        def dyraw_epi(acc, ex, outs):
            _, vjp = jax.vjp(jax.nn.gelu, ex[1][...])
            (d,) = vjp(ex[0][...] + acc)
            outs[0][...] = d

        t_ij = pl.BlockSpec((tmg, tng), lambda i, j, k: (i, j))
        (dyraw,) = _mm(
            "s5_dyraw", dlpb, wl["wglu"], _NT, (T // tmg, SW // tng, SW // tkg),
            pl.BlockSpec((tmg, tkg), lambda i, j, k: (i, k)), pl.BlockSpec((tng, tkg), lambda i, j, k: (j, k)), (tmg, tng),
            [dyg1, mx["y_raw"]], [t_ij, t_ij], [SDS((T, SW), f32)], [t_ij], dyraw_epi)

        def dd_fn(dyr, u_t, d_t):
            return d_t * dyr, dyr * u_t

        dud, dd = _colsum_kernel("s5_dd", dd_fn, [dyraw, p, mx["dflat"]], [0, 0, None], [f32], 1, SW)
        small_g["ssm_d"][l] = dd.reshape(G, H)
        dsr, dsi = _bd2("s5_ds", dyraw, 0, mx["cblkT_r"], mx["cblkT_in"])
        dcb_r, dcb_i = _bdT2("s5_dc", mx["sr"], mx["si"], 0, dyraw, dyraw, 0, 8 * P, 8 * H, J)
        small_g["ssm_c_re"][l] = _blockdiag_extract(jnp.transpose(dcb_r, (0, 2, 1)), H, P)
        small_g["ssm_c_im"][l] = -_blockdiag_extract(jnp.transpose(dcb_i, (0, 2, 1)), H, P)
        ar, ai, dlr, dli = _s5_scan_bwd("s5_scan_b", dsr, dsi, mx["sr"], mx["si"], mx["lbr_f"], mx["lbi_f"])

        def du_epi(acc, ex, outs):
            outs[0][...] = (acc + ex[0][...]).astype(outs[0].dtype)

        (du,) = _bd_sum(
            "s5_du", ar, ai, jnp.transpose(mx["bblk_r"], (0, 2, 1)), jnp.transpose(mx["bblk_i"], (0, 2, 1)), [dud],
            lambda tm, nb: [pl.BlockSpec((tm, nb), lambda i, j: (i, j))], [SDS((T, SW), _MXU)], du_epi)
        dbb_r, dbb_i = _bdT2("s5_db", p, p, 0, ar, ai, 0, 8 * H, 8 * P, J)
        dbbr_t = jnp.transpose(_blockdiag_extract(dbb_r, H, P), (1, 0, 2))
        dbbi_t = jnp.transpose(_blockdiag_extract(dbb_i, H, P), (1, 0, 2))
        da_re, da_im, dlog_dt, dbre_t, dbim_t = _zoh_bwd("zoh_b", *mx["zoh_in"], dlr.reshape(G, P), dli.reshape(G, P),
                                                         dbbr_t, dbbi_t)
        small_g["ssm_a_re"][l], small_g["ssm_a_im"][l], small_g["ssm_log_dt"][l] = da_re, da_im, dlog_dt[:, 0]
        small_g["ssm_b_re"][l] = jnp.transpose(dbre_t, (1, 2, 0))
        small_g["ssm_b_im"][l] = jnp.transpose(dbim_t, (1, 2, 0))

        dqkv3, dzb, dbl, dain, dal, ddtb, dnw = _gdn_bwd("gdn_b", mx["qkv"], p, m_z, mx["blt"], mx["aint"], mx["alog"],
                                                         mx["dtb"], mx["nw"], mx["ssave"], dog, NH, HD)
        small_g["gdn_a_log"][l], small_g["gdn_dt_bias"][l], small_g["gdn_norm_w"][l] = dal[:, 0, 0], ddtb[:, 0, 0], dnw[0]
        dqkv_pre, dcw = _conv_bwd("gdn_conv_b", p, m_qkv, wl["cw"], dqkv3)
        if pend:
            pend.wait(dqkv_pre)
            tok = pend.start()

        dpm = jnp.concatenate([du, dqkv_pre, dzb, dgs, dgg], axis=1)
        dpba = dep(jnp.concatenate([jnp.transpose(dbl[:, :, 0]), jnp.transpose(dain[:, :, 0]),
                                    jnp.zeros((T, 128 - 2 * NH), f32)], axis=1), tok).astype(_MXU)
        tnd, tkm = _tile(D, 1024), _tile(NM, 2304)
        t_ba = _mm(
            "mix_dx_ba", dpba, wl["wba"], _NT, (T // tmg, D // tnd, 1),
            pl.BlockSpec((tmg, 128), lambda i, j, k: (i, 0)), pl.BlockSpec((tnd, 128), lambda i, j, k: (j, 0)), (tmg, tnd),
            [], [], [SDS((T, D), f32)], [pl.BlockSpec((tmg, tnd), lambda i, j, k: (i, j))], _store_epi)[0]

        def dx1_epi(acc, ex, outs):
            outs[0][...] = alpha * ex[0][...] + ex[1][...] + acc

        t_d = pl.BlockSpec((tmg, tnd), lambda i, j, k: (i, j))
        (dx1,) = _mm(
            "mix_dx", dpm, wl["wmain"], _NT, (T // tmg, D // tnd, NM // tkm),
            pl.BlockSpec((tmg, tkm), lambda i, j, k: (i, k)), pl.BlockSpec((tnd, tkm), lambda i, j, k: (j, k)), (tmg, tnd),
            [dz2, t_ba], [t_d, t_d], [SDS((T, D), f32)], [t_d], dx1_epi)
        tnm = _tile(NM, 1024)
        tkt = _tile(T, 2048)
        dwmain = _mm(
            "mix_dw", jnp.transpose(mx["x1b"]), dpm, _NN, (D // tkd, NM // tnm, T // tkt),
            pl.BlockSpec((tkd, tkt), lambda i, j, k: (i, k)), pl.BlockSpec((tkt, tnm), lambda i, j, k: (k, j)), (tkd, tnm),
            [], [], [SDS((D, NM), _GDT)], [pl.BlockSpec((tkd, tnm), lambda i, j, k: (i, j))], _store_epi)[0]
        dwba = _mm_tn("mix_dw_ba", mx["x1b"], dpba, _GDT)
        dwin = jnp.concatenate([dwmain[:, :o_b], dwba[:, :2 * NH], dwmain[:, o_b:]], axis=1)
        dwin8 = jnp.transpose(dwin.reshape(D, 8, IN // 8), (1, 0, 2))
        dcw8 = jnp.transpose(dcw.reshape(KC, 8, 3 * GW // 8), (1, 0, 2))

        dx0, dwgu1, dwd1, dg1, db1 = _ffn_bwd("ffn_b", dx1, sv["f1"], wl["wgu1"], wl["wd1"], vec("ln1_g", l), alpha)
        small_g["ln1_g"][l], small_g["ln1_b"][l] = dg1[0], db1[0]
        dy = dx0

        parts = [dwgu1, dwd1.reshape(8, -1, D), dwin8, dcw8, dwglu.reshape(8, SW // 8, SW), dwbs, dwbg,
                 dwo.reshape(8, DS, D), dwgu2, dwd2.reshape(8, -1, D)]
        if pend:
            pend.wait(dx0)
            for n, (full, recv, last) in zip(_BIG, pend.out):
                big_out[n] = _adamw_big("adamw_" + n, full, recv, _coord(last), W[n], M[n], V[n], l + 1, big_out[n])
        pend = _AsyncReduceScatter("rsp", parts, _PATHS)

    seg = 8 * 128

    def padded(n):
        return -(-n // seg) * seg

    def pack(arrs):
        flat = jnp.concatenate([jnp.pad(a.reshape(-1), (0, padded(a.size) - a.size)) for a in arrs])
        n = flat.shape[0]
        rows = -(-n // (128 * 512)) * 512
        return jnp.pad(flat, (0, rows * 128 - n)).reshape(rows, 128)

    tok = pend.start()
    gs_full = [jnp.stack(small_g[n]).reshape(W[n].shape) for n in _SMALL]
    gpack = dep(pack(gs_full), tok)
    (gall,) = _all_gather("ag_small", [gpack], ["yx"])
    pend.wait(gall)
    tok = pend.start()
    sg, sd, sm, sv_ = _adamw_small("adamw_small", gall, dep(pack([W[n] for n in _SMALL]), tok),
                                   pack([M[n] for n in _SMALL]), pack([V[n] for n in _SMALL]))
    pend.wait(sg)
    pend.start()
    pend.wait(sv_)
    for n, (full, recv, last) in zip(_BIG, pend.out):
        big_out[n] = _adamw_big("adamw_" + n, full, recv, _coord(last), W[n], M[n], V[n], 0, big_out[n])

    def unpack(packed):
        out, row = {}, 0
        for n in _SMALL:
            sz = math.prod(W[n].shape)
            rows = padded(sz) // 128
            out[n] = packed[row:row + rows].reshape(-1)[:sz].reshape(W[n].shape)
            row += rows
        return out

    res = [unpack(a) for a in (sg, sd, sm, sv_)]
    for n in _BIG:
        for i in range(4):
            res[i][n] = big_out[n][i]
    outs = [loss, dy[None]]
    for i in range(4):
        outs += [res[i][n] for n in _ORDER]
    return tuple(outs)


def kernel(x, ffn1_w_gu, ffn1_w_down, ln1_g, ln1_b, w_in, conv_w, ssm_a_re, ssm_a_im, ssm_log_dt, ssm_b_re, ssm_b_im, ssm_c_re, ssm_c_im, ssm_d, glu_w, glu_b, gdn_a_log, gdn_dt_bias, gdn_norm_w, w_br_ssm, w_br_gdn, w_out, ln2_g, ln2_b, ffn2_w_gu, ffn2_w_down, ln3_g, ln3_b, loss_target, m_ffn1_w_gu, m_ffn1_w_down, m_ln1_g, m_ln1_b, m_w_in, m_conv_w, m_ssm_a_re, m_ssm_a_im, m_ssm_log_dt, m_ssm_b_re, m_ssm_b_im, m_ssm_c_re, m_ssm_c_im, m_ssm_d, m_glu_w, m_glu_b, m_gdn_a_log, m_gdn_dt_bias, m_gdn_norm_w, m_w_br_ssm, m_w_br_gdn, m_w_out, m_ln2_g, m_ln2_b, m_ffn2_w_gu, m_ffn2_w_down, m_ln3_g, m_ln3_b, v_ffn1_w_gu, v_ffn1_w_down, v_ln1_g, v_ln1_b, v_w_in, v_conv_w, v_ssm_a_re, v_ssm_a_im, v_ssm_log_dt, v_ssm_b_re, v_ssm_b_im, v_ssm_c_re, v_ssm_c_im, v_ssm_d, v_glu_w, v_glu_b, v_gdn_a_log, v_gdn_dt_bias, v_gdn_norm_w, v_w_br_ssm, v_w_br_gdn, v_w_out, v_ln2_g, v_ln2_b, v_ffn2_w_gu, v_ffn2_w_down, v_ln3_g, v_ln3_b):
    given = dict(locals())
    W = {n: given[n] for n in _ORDER}
    M = {n: given["m_" + n] for n in _ORDER}
    V = {n: given["v_" + n] for n in _ORDER}
    return _step(x, loss_target, W, M, V)
```

```python
import functools
import math

import jax
import jax.numpy as jnp
from jax import lax
from jax.experimental import pallas as pl
from jax.experimental.pallas import tpu as pltpu

f32 = jnp.float32
_MXU = jnp.bfloat16
_GDT = jnp.bfloat16
_HP = lax.Precision.HIGHEST
_VMEM_LIMIT = 56 * 1024 * 1024
_MESH_T = pl.DeviceIdType.MESH

LN_EPS = 1e-5
RMS_EPS = 1e-6
L2_EPS = 1e-6
CHUNK = 64
ADAM_LR = 0.001
ADAM_B1 = 0.9
ADAM_B2 = 0.999
ADAM_EPS = 1e-08
ADAM_WD = 0.01
ADAM_STEP = 10

_NN = (((1,), (0,)), ((), ()))
_NT = (((1,), (1,)), ((), ()))
_TN = (((0,), (0,)), ((), ()))

SDS = jax.ShapeDtypeStruct


def _cp(sem):
    return pltpu.CompilerParams(dimension_semantics=sem, vmem_limit_bytes=_VMEM_LIMIT)


def _tile(n, pref):
    if n <= pref:
        return n
    t = (pref // 128) * 128
    while t >= 128:
        if n % t == 0:
            return t
        t -= 128
    return n


def _rtile(n, pref):
    if n <= pref:
        return n
    t = (pref // 16) * 16
    while t >= 16:
        if n % t == 0:
            return t
        t -= 16
    return n


def _mm(name, a, b, dims, grid, a_spec, b_spec, acc_shape, extras, extra_specs, out_shape, out_specs, epilogue):
    nk = grid[2]
    ne = len(extras)
    no = len(out_shape)

    def body(*refs):
        a_ref, b_ref = refs[0], refs[1]
        ex = refs[2:2 + ne]
        outs = refs[2 + ne:2 + ne + no]
        acc = refs[-1]
        k = pl.program_id(2)
        part = lax.dot_general(a_ref[...].astype(_MXU), b_ref[...].astype(_MXU), dims, preferred_element_type=f32)

        @pl.when(k == 0)
        def _():
            acc[...] = part

        @pl.when(k > 0)
        def _():
            acc[...] += part

        @pl.when(k == nk - 1)
        def _():
            epilogue(acc[...], ex, outs)

    return pl.pallas_call(
        body, grid=grid, in_specs=[a_spec, b_spec, *extra_specs], out_specs=list(out_specs), out_shape=list(out_shape),
        scratch_shapes=[pltpu.VMEM(acc_shape, f32)], compiler_params=_cp(("parallel", "parallel", "arbitrary")), name=name,
    )(a, b, *extras)


def _store_epi(acc, ex, outs):
    for o in outs:
        o[...] = acc.astype(o.dtype)


def _ln_epilogue(alpha, c):
    def epi(acc, ex, outs):
        x_ref, g_ref, b_ref = ex
        y_ref, yb_ref, xh_ref, r_ref = outs
        z = alpha * x_ref[...] + c * acc
        mu = jnp.mean(z, axis=-1, keepdims=True)
        zc = z - mu
        var = jnp.mean(zc * zc, axis=-1, keepdims=True)
        r = lax.rsqrt(var + LN_EPS)
        xh = zc * r
        y = xh * g_ref[...] + b_ref[...]
        y_ref[...] = y
        yb_ref[...] = y.astype(yb_ref.dtype)
        xh_ref[...] = xh
        r_ref[...] = r
    return epi


def _mm_ln(name, a, w, x, g, b, alpha, c):
    T, K = a.shape
    D = w.shape[1]
    tm, tk = _rtile(T, 512), _tile(K, 512)
    row = pl.BlockSpec((tm, D), lambda i, j, k: (i, 0))
    vec = pl.BlockSpec((1, D), lambda i, j, k: (0, 0))
    return _mm(
        name, a, w, _NN, (T // tm, 1, K // tk),
        pl.BlockSpec((tm, tk), lambda i, j, k: (i, k)), pl.BlockSpec((tk, D), lambda i, j, k: (k, 0)), (tm, D),
        [x, g, b], [row, vec, vec],
        [SDS((T, D), f32), SDS((T, D), _MXU), SDS((T, D), f32), SDS((T, 1), f32)],
        [row, row, row, pl.BlockSpec((tm, 1), lambda i, j, k: (i, 0))],
        _ln_epilogue(alpha, c),
    )


def _ln_bwd(name, dy, xhat, rstd, g, c):
    T, D = dy.shape
    tm = _rtile(T, 256)

    def body(dy_ref, xh_ref, r_ref, g_ref, dz_ref, df_ref, dg_ref, db_ref):
        i = pl.program_id(0)
        dyv = dy_ref[...]
        xh = xh_ref[...]
        dxh = dyv * g_ref[...]
        m1 = jnp.mean(dxh, axis=-1, keepdims=True)
        m2 = jnp.mean(dxh * xh, axis=-1, keepdims=True)
        dz = r_ref[...] * (dxh - m1 - xh * m2)
        dz_ref[...] = dz
        df_ref[...] = (c * dz).astype(df_ref.dtype)
        pg = jnp.sum(dyv * xh, axis=0, keepdims=True)
        pb = jnp.sum(dyv, axis=0, keepdims=True)

        @pl.when(i == 0)
        def _():
            dg_ref[...] = pg
            db_ref[...] = pb

        @pl.when(i > 0)
        def _():
            dg_ref[...] += pg
            db_ref[...] += pb

    row = pl.BlockSpec((tm, D), lambda i: (i, 0))
    vec = pl.BlockSpec((1, D), lambda i: (0, 0))
    return pl.pallas_call(
        body, grid=(T // tm,), in_specs=[row, row, pl.BlockSpec((tm, 1), lambda i: (i, 0)), vec],
        out_specs=[row, row, vec, vec],
        out_shape=[SDS((T, D), f32), SDS((T, D), _MXU), SDS((1, D), f32), SDS((1, D), f32)],
        compiler_params=_cp(("arbitrary",)), name=name,
    )(dy, xhat, rstd, g)


def _swiglu(g, u):
    return jax.nn.silu(g) * u


def _ffn_up(name, xb, wgu):
    T, D = xb.shape
    FS = wgu.shape[2]
    F = 4 * FS
    tm = _rtile(T, 256)

    def body(x_ref, wg_ref, wu_ref, g_ref, u_ref, h_ref):
        xv = x_ref[...]
        g = jnp.dot(xv, wg_ref[...], preferred_element_type=f32)
        u = jnp.dot(xv, wu_ref[...], preferred_element_type=f32)
        g_ref[...] = g.astype(g_ref.dtype)
        u_ref[...] = u.astype(u_ref.dtype)
        h_ref[...] = _swiglu(g, u).astype(h_ref.dtype)

    out = pl.BlockSpec((tm, FS), lambda j, i: (i, j))
    return pl.pallas_call(
        body, grid=(4, T // tm),
        in_specs=[pl.BlockSpec((tm, D), lambda j, i: (i, 0)),
                  pl.BlockSpec((None, D, FS), lambda j, i: (j, 0, 0)),
                  pl.BlockSpec((None, D, FS), lambda j, i: (j + 4, 0, 0))],
        out_specs=[out, out, out],
        out_shape=[SDS((T, F), _MXU), SDS((T, F), _MXU), SDS((T, F), _MXU)],
        compiler_params=_cp(("parallel", "arbitrary")), name=name,
    )(xb, wgu, wgu)


def _ffn_bwd(pfx, dy, sv, wgu, wd, g_ln, alpha):
    T, D = dy.shape
    FS = wgu.shape[2]
    F = 4 * FS
    dz, dfb, dg, db = _ln_bwd(pfx + "_lnb", dy, sv["xhat"], sv["rstd"], g_ln, 0.5)

    tm, tn, tk = _rtile(T, 1024), _tile(F, 512), _tile(D, 2048)

    def epi(acc, ex, outs):
        g_ref, u_ref = ex
        _, vjp = jax.vjp(_swiglu, g_ref[...].astype(f32), u_ref[...].astype(f32))
        dgate, dup = vjp(acc)
        outs[0][0] = dgate.astype(outs[0].dtype)
        outs[0][1] = dup.astype(outs[0].dtype)

    gu = pl.BlockSpec((tm, tn), lambda i, j, k: (i, j))
    (dgu,) = _mm(
        pfx + "_dh", dfb, wd, _NT, (T // tm, F // tn, D // tk),
        pl.BlockSpec((tm, tk), lambda i, j, k: (i, k)), pl.BlockSpec((tn, tk), lambda i, j, k: (j, k)), (tm, tn),
        [sv["gate"], sv["up"]], [gu, gu],
        [SDS((2, T, F), _MXU)], [pl.BlockSpec((2, tm, tn), lambda i, j, k: (0, i, j))], epi,
    )

    tm2, tk2 = _tile(F, 512), _rtile(T, 512)
    (dwd,) = _mm(
        pfx + "_dwd", sv["h"], dfb, _TN, (F // tm2, 1, T // tk2),
        pl.BlockSpec((tk2, tm2), lambda i, j, k: (k, i)), pl.BlockSpec((tk2, D), lambda i, j, k: (k, 0)), (tm2, D),
        [], [], [SDS((F, D), _GDT)], [pl.BlockSpec((tm2, D), lambda i, j, k: (i, 0))], _store_epi,
    )

    tn3 = _tile(D, 1024)

    def epi3(acc, ex, outs):
        outs[0][...] = alpha * ex[0][...] + acc

    (dx,) = _mm(
        pfx + "_dx", dgu, wgu, _NT, (T // tm, D // tn3, 8),
        pl.BlockSpec((None, tm, FS), lambda i, j, k: (k // 4, i, k % 4)),
        pl.BlockSpec((None, tn3, FS), lambda i, j, k: (k, j, 0)), (tm, tn3),
        [dz], [pl.BlockSpec((tm, tn3), lambda i, j, k: (i, j))],
        [SDS((T, D), f32)], [pl.BlockSpec((tm, tn3), lambda i, j, k: (i, j))], epi3,
    )

    tm4, tk4 = _rtile(D, 512), _tile(T, 2048)
    (dwgu,) = _mm(
        pfx + "_dwgu", jnp.transpose(sv["xb"]), dgu, _NN, (D // tm4, 8, T // tk4),
        pl.BlockSpec((tm4, tk4), lambda i, j, k: (i, k)),
        pl.BlockSpec((None, tk4, FS), lambda i, j, k: (j // 4, k, j % 4)), (tm4, FS),
        [], [], [SDS((8, D, FS), _GDT)], [pl.BlockSpec((None, tm4, FS), lambda i, j, k: (j, i, 0))], _store_epi,
    )
    return dx, dwgu, dwd, dg, db


def _zoh(a_re, a_im, log_dt, b_re_t, b_im_t):
    dt = jnp.exp(log_dt)
    mag = jnp.exp(a_re * dt)
    lr_, li_ = mag * jnp.cos(a_im * dt), mag * jnp.sin(a_im * dt)
    den = a_re * a_re + a_im * a_im
    pr, pi = lr_ - 1.0, li_
    qr, qi = a_re / den, -a_im / den
    zr, zi = pr * qr - pi * qi, pr * qi + pi * qr
    bbr = zr[None] * b_re_t - zi[None] * b_im_t
    bbi = zr[None] * b_im_t + zi[None] * b_re_t
    return lr_, li_, bbr, bbi


def _zoh_fwd(name, a_re, a_im, log_dt, b_re_t, b_im_t):
    G, P = a_re.shape
    H = b_re_t.shape[0]

    def body(ar, ai, ld, br, bi, o1, o2, o3, o4):
        r = _zoh(ar[...], ai[...], ld[...], br[...], bi[...])
        o1[...], o2[...], o3[...], o4[...] = r

    return pl.pallas_call(
        body, out_shape=[SDS((G, P), f32), SDS((G, P), f32), SDS((H, G, P), f32), SDS((H, G, P), f32)], name=name,
    )(a_re, a_im, log_dt, b_re_t, b_im_t)


def _zoh_bwd(name, a_re, a_im, log_dt, b_re_t, b_im_t, dlr, dli, dbbr, dbbi):
    G, P = a_re.shape
    H = b_re_t.shape[0]

    def body(ar, ai, ld, br, bi, g1, g2, g3, g4, o1, o2, o3, o4, o5):
        _, vjp = jax.vjp(_zoh, ar[...], ai[...], ld[...], br[...], bi[...])
        r = vjp((g1[...], g2[...], g3[...], g4[...]))
        o1[...], o2[...], o3[...], o4[...], o5[...] = r

    return pl.pallas_call(
        body, out_shape=[SDS((G, P), f32), SDS((G, P), f32), SDS((G, 1), f32), SDS((H, G, P), f32), SDS((H, G, P), f32)],
        name=name,
    )(a_re, a_im, log_dt, b_re_t, b_im_t, dlr, dli, dbbr, dbbi)


def _blockdiag(m):
    G, A, B = m.shape
    eye = jnp.eye(8, dtype=bool)
    m4 = m.reshape(G // 8, 8, A, B)
    out = jnp.where(eye[None, :, None, :, None], m4[:, :, :, None, :], jnp.zeros((), m.dtype))
    return out.reshape(G // 8, 8 * A, 8 * B)


def _blockdiag_extract(mb, A, B):
    J = mb.shape[0]
    m5 = mb.reshape(J, 8, A, 8, B)
    d = jnp.stack([m5[:, i, :, i, :] for i in range(8)], axis=1)
    return d.reshape(J * 8, A, B)


def _bd2(name, a, a_col0, b1, b2, out_dtype=f32):
    T = a.shape[0]
    J, KA, NB = b1.shape
    tm = _rtile(T, 512)

    def body(a_ref, b1_ref, b2_ref, o1, o2):
        av = a_ref[...].astype(_MXU)
        o1[...] = jnp.dot(av, b1_ref[...].astype(_MXU), preferred_element_type=f32).astype(o1.dtype)
        o2[...] = jnp.dot(av, b2_ref[...].astype(_MXU), preferred_element_type=f32).astype(o2.dtype)

    bs = pl.BlockSpec((None, KA, NB), lambda i, j: (j, 0, 0))
    os_ = pl.BlockSpec((tm, NB), lambda i, j: (i, j))
    return pl.pallas_call(
        body, grid=(T // tm, J), in_specs=[pl.BlockSpec((tm, KA), lambda i, j: (i, j + a_col0)), bs, bs],
        out_specs=[os_, os_], out_shape=[SDS((T, J * NB), out_dtype)] * 2,
        compiler_params=_cp(("parallel", "parallel")), name=name,
    )(a, b1, b2)


def _bd_sum(name, a1, a2, b1, b2, extras, extra_specs_fn, out_shape, epilogue):
    T = a1.shape[0]
    J, KA, NB = b1.shape
    tm = _rtile(T, 512)
    ne = len(extras)

    def body(*refs):
        a1_ref, a2_ref, b1_ref, b2_ref = refs[:4]
        ex = refs[4:4 + ne]
        outs = refs[4 + ne:]
        acc = jnp.dot(a1_ref[...].astype(_MXU), b1_ref[...].astype(_MXU), preferred_element_type=f32)
        acc = acc + jnp.dot(a2_ref[...].astype(_MXU), b2_ref[...].astype(_MXU), preferred_element_type=f32)
        epilogue(acc, ex, outs)

    as_ = pl.BlockSpec((tm, KA), lambda i, j: (i, j))
    bs = pl.BlockSpec((None, KA, NB), lambda i, j: (j, 0, 0))
    os_ = pl.BlockSpec((tm, NB), lambda i, j: (i, j))
    return pl.pallas_call(
        body, grid=(T // tm, J), in_specs=[as_, as_, bs, bs, *extra_specs_fn(tm, NB)],
        out_specs=[os_] * len(out_shape), out_shape=list(out_shape),
        compiler_params=_cp(("parallel", "parallel")), name=name,
    )(a1, a2, b1, b2, *extras)


def _bdT2(name, a1, a2, a_col0, b1, b2, b_col0, KA, NB, J):
    T = a1.shape[0]
    tk = _rtile(T, 512)

    def body(a1_ref, a2_ref, b1_ref, b2_ref, o1, o2):
        k = pl.program_id(1)
        p1 = lax.dot_general(a1_ref[...].astype(_MXU), b1_ref[...].astype(_MXU), _TN, preferred_element_type=f32)
        p2 = lax.dot_general(a2_ref[...].astype(_MXU), b2_ref[...].astype(_MXU), _TN, preferred_element_type=f32)

        @pl.when(k == 0)
        def _():
            o1[...] = p1
            o2[...] = p2

        @pl.when(k > 0)
        def _():
            o1[...] += p1
            o2[...] += p2

    as_ = pl.BlockSpec((tk, KA), lambda j, k: (k, j + a_col0))
    bs = pl.BlockSpec((tk, NB), lambda j, k: (k, j + b_col0))
    os_ = pl.BlockSpec((None, KA, NB), lambda j, k: (j, 0, 0))
    return pl.pallas_call(
        body, grid=(J, T // tk), in_specs=[as_, as_, bs, bs], out_specs=[os_, os_],
        out_shape=[SDS((J, KA, NB), f32)] * 2, compiler_params=_cp(("parallel", "arbitrary")), name=name,
    )(a1, a2, b1, b2)


_RB = 8


def _cmul(ar, ai, br, bi):
    return ar * br - ai * bi, ar * bi + ai * br


def _lam_powers(lr_v, li_v, cb):
    pw = {1: (lr_v, li_v)}
    for k in range(2, _RB + 1):
        pw[k] = _cmul(*pw[k - 1], lr_v, li_v)
    return pw


def _row_powers(pw, row, cb, reverse):
    outr = jnp.zeros((_RB, cb), f32)
    outi = jnp.zeros((_RB, cb), f32)
    for r in range(_RB):
        k = _RB - r if reverse else r + 1
        outr = jnp.where(row == r, pw[k][0], outr)
        outi = jnp.where(row == r, pw[k][1], outi)
    return outr, outi


def _tile_scan(xr, xi, pw, row, reverse):
    for k in (1, 2, 4):
        if reverse:
            keep = row < _RB - k
            shr, shi = pltpu.roll(xr, _RB - k, 0), pltpu.roll(xi, _RB - k, 0)
        else:
            keep = row >= k
            shr, shi = pltpu.roll(xr, k, 0), pltpu.roll(xi, k, 0)
        shr, shi = jnp.where(keep, shr, 0.0), jnp.where(keep, shi, 0.0)
        mr, mi = pw[k]
        xr, xi = xr + (mr * shr - mi * shi), xi + (mr * shi + mi * shr)
    return xr, xi


def _s5_scan(name, bur, bui, lr_, li_):
    T, N = bur.shape
    cb = _tile(N, 512)

    def body(br_ref, bi_ref, lr_ref, li_ref, sr_ref, si_ref):
        pw = _lam_powers(lr_ref[...], li_ref[...], cb)
        row = lax.broadcasted_iota(jnp.int32, (_RB, cb), 0)
        cr, ci = _row_powers(pw, row, cb, False)

        def step(n, carry):
            pr, pi = carry
            t0 = pl.multiple_of(n * _RB, _RB)
            xr, xi = _tile_scan(br_ref[pl.ds(t0, _RB), :], bi_ref[pl.ds(t0, _RB), :], pw, row, False)
            xr, xi = xr + (cr * pr - ci * pi), xi + (cr * pi + ci * pr)
            sr_ref[pl.ds(t0, _RB), :] = xr
            si_ref[pl.ds(t0, _RB), :] = xi
            return xr[_RB - 1:_RB, :], xi[_RB - 1:_RB, :]

        z = jnp.zeros((1, cb), f32)
        lax.fori_loop(0, T // _RB, step, (z, z))

    col = pl.BlockSpec((T, cb), lambda j: (0, j))
    vec = pl.BlockSpec((1, cb), lambda j: (0, j))
    return pl.pallas_call(
        body, grid=(N // cb,), in_specs=[col, col, vec, vec], out_specs=[col, col],
        out_shape=[SDS((T, N), f32)] * 2, compiler_params=_cp(("parallel",)), name=name,
    )(bur, bui, lr_, li_)


def _s5_scan_bwd(name, dsr, dsi, sr, si, lr_, li_):
    T, N = dsr.shape
    cb = _tile(N, 256)

    def body(dr_ref, di_ref, sr_ref, si_ref, lr_ref, li_ref, ar_ref, ai_ref, glr_ref, gli_ref):
        pw = _lam_powers(lr_ref[...], -li_ref[...], cb)
        row = lax.broadcasted_iota(jnp.int32, (_RB, cb), 0)
        cr, ci = _row_powers(pw, row, cb, True)
        NT = T // _RB

        def tile(t0, nxt, prev_last):
            xr, xi = _tile_scan(dr_ref[pl.ds(t0, _RB), :], di_ref[pl.ds(t0, _RB), :], pw, row, True)
            xr, xi = xr + (cr * nxt[0] - ci * nxt[1]), xi + (cr * nxt[1] + ci * nxt[0])
            ar_ref[pl.ds(t0, _RB), :] = xr
            ai_ref[pl.ds(t0, _RB), :] = xi
            pr = jnp.where(row == 0, prev_last[0], pltpu.roll(sr_ref[pl.ds(t0, _RB), :], 1, 0))
            pi = jnp.where(row == 0, prev_last[1], pltpu.roll(si_ref[pl.ds(t0, _RB), :], 1, 0))
            return xr, xi, xr * pr + xi * pi, xi * pr - xr * pi

        def step(n, carry):
            nr, ni, glr, gli = carry
            t0 = pl.multiple_of((NT - 1 - n) * _RB, _RB)
            tp = pl.multiple_of((NT - 2 - n) * _RB, _RB)
            prev_last = (sr_ref[pl.ds(tp, _RB), :][_RB - 1:_RB, :], si_ref[pl.ds(tp, _RB), :][_RB - 1:_RB, :])
            xr, xi, gr, gi = tile(t0, (nr, ni), prev_last)
            return xr[0:1, :], xi[0:1, :], glr + gr, gli + gi

        z1 = jnp.zeros((1, cb), f32)
        z8 = jnp.zeros((_RB, cb), f32)
        nr, ni, glr, gli = lax.fori_loop(0, NT - 1, step, (z1, z1, z8, z8))
        _, _, gr, gi = tile(0, (nr, ni), (z1, z1))
        glr_ref[...] = jnp.sum(glr + gr, axis=0, keepdims=True)
        gli_ref[...] = jnp.sum(gli + gi, axis=0, keepdims=True)

    col = pl.BlockSpec((T, cb), lambda j: (0, j))
    vec = pl.BlockSpec((1, cb), lambda j: (0, j))
    return pl.pallas_call(
        body, grid=(N // cb,), in_specs=[col, col, col, col, vec, vec], out_specs=[col, col, vec, vec],
        out_shape=[SDS((T, N), f32), SDS((T, N), f32), SDS((1, N), f32), SDS((1, N), f32)],
        compiler_params=_cp(("parallel",)), name=name,
    )(dsr, dsi, sr, si, lr_, li_)


def _conv_fwd(name, p, col0, w, GW3):
    T = p.shape[0]
    K = w.shape[0]
    cb = 128
    c0 = col0 // cb

    def body(x_ref, w_ref, o_ref, pad_ref):
        pad_ref[pl.ds(0, 8), :] = jnp.zeros((8, cb), f32)
        pad_ref[pl.ds(8, T), :] = x_ref[...]
        wv = w_ref[...]
        acc = jnp.zeros((T, cb), f32)
        for j in range(K):
            acc = acc + wv[j:j + 1, :] * pad_ref[pl.ds(8 - (K - 1) + j, T), :]
        o_ref[...] = jax.nn.silu(acc)

    return pl.pallas_call(
        body, grid=(GW3 // cb,),
        in_specs=[pl.BlockSpec((T, cb), lambda j: (0, j + c0)), pl.BlockSpec((K, cb), lambda j: (0, j))],
        out_specs=pl.BlockSpec((T, cb), lambda j: (0, j)), out_shape=SDS((T, GW3), f32),
        scratch_shapes=[pltpu.VMEM((T + 8, cb), f32)], compiler_params=_cp(("parallel",)), name=name,
    )(p, w)


def _conv_bwd(name, p, col0, w, dout3):
    T = p.shape[0]
    K = w.shape[0]
    GW = dout3.shape[2]
    GW3 = 3 * GW
    cb = 128
    c0 = col0 // cb
    nb = GW // cb

    def body(x_ref, w_ref, d_ref, dx_ref, dw_ref, pad_ref, dpad_ref):
        pad_ref[pl.ds(0, 8), :] = jnp.zeros((8, cb), f32)
        pad_ref[pl.ds(8, T), :] = x_ref[...]
        wv = w_ref[...]
        pre = jnp.zeros((T, cb), f32)
        for j in range(K):
            pre = pre + wv[j:j + 1, :] * pad_ref[pl.ds(8 - (K - 1) + j, T), :]
        _, vjp = jax.vjp(jax.nn.silu, pre)
        (dpre,) = vjp(d_ref[...])
        dpad_ref[pl.ds(0, T), :] = dpre
        dpad_ref[pl.ds(T, 8), :] = jnp.zeros((8, cb), f32)
        dx = jnp.zeros((T, cb), f32)
        rows = []
        for j in range(K):
            dx = dx + wv[j:j + 1, :] * dpad_ref[pl.ds((K - 1) - j, T), :]
            rows.append(jnp.sum(dpre * pad_ref[pl.ds(8 - (K - 1) + j, T), :], axis=0, keepdims=True))
        dx_ref[...] = dx.astype(dx_ref.dtype)
        for j in range(K):
            dw_ref[pl.ds(j, 1), :] = rows[j]

    return pl.pallas_call(
        body, grid=(GW3 // cb,),
        in_specs=[pl.BlockSpec((T, cb), lambda j: (0, j + c0)), pl.BlockSpec((K, cb), lambda j: (0, j)),
                  pl.BlockSpec((None, T, cb), lambda j: (j // nb, 0, j % nb))],
        out_specs=[pl.BlockSpec((T, cb), lambda j: (0, j)), pl.BlockSpec((K, cb), lambda j: (0, j))],
        out_shape=[SDS((T, GW3), _MXU), SDS((K, GW3), f32)],
        scratch_shapes=[pltpu.VMEM((T + 8, cb), f32), pltpu.VMEM((T + 8, cb), f32)],
        compiler_params=_cp(("parallel",)), name=name,
    )(p, w, dout3)


def _hdot(a, b, dims=_NN):
    return lax.dot_general(a, b, dims, precision=_HP, preferred_element_type=f32)


def _split(a):
    hi = a.astype(jnp.bfloat16)
    lo = (a - hi.astype(f32)).astype(jnp.bfloat16)
    return hi, lo


_BNN = (((2,), (1,)), ((0,), (0,)))
_BNT = (((2,), (2,)), ((0,), (0,)))
_BTN = (((1,), (1,)), ((0,), (0,)))


def _dot3_raw(a, b, dims):
    ah, al = _split(a)
    bh, bl = _split(b)
    d = functools.partial(lax.dot_general, dimension_numbers=dims, preferred_element_type=f32)
    return d(ah, bh) + (d(al, bh) + d(ah, bl))


@jax.custom_vjp
def _dot3(a, b):
    return _dot3_raw(a, b, _BNN)


def _dot3_fwd(a, b):
    return _dot3_raw(a, b, _BNN), (a, b)


def _dot3_bwd(res, g):
    a, b = res
    return _dot3_raw(g, b, _BNT), _dot3_raw(a, g, _BTN)


_dot3.defvjp(_dot3_fwd, _dot3_bwd)


def _ldot(a, b, dims=_BNN):
    return lax.dot_general(a.astype(_MXU), b.astype(_MXU), dims, preferred_element_type=f32)


def _sdot(a, b):
    return _ldot(a, b)


def _gdn_chunk(S, q, k, v, z, bl, ain, alog, dtb, nw):
    H, C, d = q.shape
    ri = lax.broadcasted_iota(jnp.int32, (H, C, C), 1)
    ci = lax.broadcasted_iota(jnp.int32, (H, C, C), 2)
    causal = ri >= ci
    strict = ri > ci
    tri = causal.astype(f32)
    qn = q * lax.rsqrt(jnp.sum(q * q, axis=-1, keepdims=True) + L2_EPS) * (d ** -0.5)
    kn = k * lax.rsqrt(jnp.sum(k * k, axis=-1, keepdims=True) + L2_EPS)
    beta = jax.nn.sigmoid(bl)
    g = -jnp.exp(alog) * jax.nn.softplus(ain + dtb)
    gb = jnp.broadcast_to(g, (H, C, C))
    gc_col = _dot3(tri, gb)
    gc_row = _dot3(jnp.ones((H, C, C), f32), jnp.where(ri <= ci, gb, 0.0))
    diff = jnp.where(causal, gc_col - gc_row, 0.0)
    decay = jnp.where(causal, jnp.exp(diff), 0.0)
    gcum = gc_col[:, :, 0:1]
    glast = gc_col[:, C - 1:C, 0:1]
    egc = jnp.exp(gcum)
    kb = kn * beta
    lower = jnp.where(strict, _ldot(kb, kn, _BNT) * decay, 0.0)
    x = jnp.concatenate([v * beta, kb * egc], axis=-1)
    m = -lower
    for it in range(6):
        x = x + _sdot(m, x)
        if it < 5:
            m = _sdot(m, m)
    u_val, w_key = x[:, :, :d], x[:, :, d:]
    attn = _ldot(qn, kn, _BNT) * decay
    q_dec = qn * egc
    k_dec = kn * jnp.exp(glast - gcum)
    v_new = u_val - _ldot(w_key, S)
    out = _ldot(q_dec, S) + _ldot(attn, v_new)
    s_new = S * jnp.exp(glast) + _ldot(k_dec, v_new, _BTN)
    o = out * lax.rsqrt(jnp.mean(out * out, axis=-1, keepdims=True) + RMS_EPS) * nw
    o = o * jax.nn.silu(z)
    return s_new, o


def _heads_per_step(NH, HD, zcol0):
    for hb in (8, 4, 2):
        if NH % hb == 0 and zcol0 % (hb * HD) == 0:
            return hb
    return 1


def _gdn_fwd(name, qkv, p, zcol0, blt, aint, alog, dtb, nw, NH, HD):
    T = qkv.shape[0]
    N = T // CHUNK
    GW = NH * HD
    HB = _heads_per_step(NH, HD, zcol0)
    W = HB * HD
    zc0 = zcol0 // W
    nb = GW // W

    def body(q_ref, k_ref, v_ref, z_ref, bl_ref, ain_ref, al_ref, dtb_ref, nw_ref, o_ref, ssave_ref, s_scr):
        n = pl.program_id(1)

        @pl.when(n == 0)
        def _():
            s_scr[...] = jnp.zeros_like(s_scr)

        heads = lambda r: jnp.stack([r[:, hh * HD:(hh + 1) * HD] for hh in range(HB)], axis=0)
        s_in = s_scr[...]
        ssave_ref[...] = s_in
        s_new, o = _gdn_chunk(s_in, heads(q_ref), heads(k_ref), heads(v_ref), heads(z_ref), bl_ref[...], ain_ref[...],
                              al_ref[...], dtb_ref[...], nw_ref[...])
        s_scr[...] = s_new
        for hh in range(HB):
            o_ref[:, hh * HD:(hh + 1) * HD] = o[hh].astype(o_ref.dtype)

    ch = lambda off: pl.BlockSpec((CHUNK, W), lambda h, n: (n, h + off))
    sc = pl.BlockSpec((HB, CHUNK, 1), lambda h, n: (h, n, 0))
    hs = pl.BlockSpec((HB, 1, 1), lambda h, n: (h, 0, 0))
    return pl.pallas_call(
        body, grid=(NH // HB, N),
        in_specs=[ch(0), ch(nb), ch(2 * nb), ch(zc0), sc, sc, hs, hs, pl.BlockSpec((1, HD), lambda h, n: (0, 0))],
        out_specs=[pl.BlockSpec((CHUNK, W), lambda h, n: (n, h)),
                   pl.BlockSpec((HB, None, HD, HD), lambda h, n: (h, n, 0, 0))],
        out_shape=[SDS((T, GW), _MXU), SDS((NH, N, HD, HD), f32)],
        scratch_shapes=[pltpu.VMEM((HB, HD, HD), f32)], compiler_params=_cp(("parallel", "arbitrary")), name=name,
    )(qkv, qkv, qkv, p, blt, aint, alog, dtb, nw)


def _gdn_bwd(name, qkv, p, zcol0, blt, aint, alog, dtb, nw, ssave, do, NH, HD):
    T = qkv.shape[0]
    N = T // CHUNK
    GW = NH * HD
    HB = _heads_per_step(NH, HD, zcol0)
    W = HB * HD
    zc0 = zcol0 // W
    nb = GW // W

    def body(q_ref, k_ref, v_ref, z_ref, bl_ref, ain_ref, al_ref, dtb_ref, nw_ref, ss_ref, do_ref,
             dqkv_ref, dz_ref, dbl_ref, dain_ref, dal_ref, ddtb_ref, dnw_ref, ds_scr):
        h = pl.program_id(0)
        n = pl.program_id(1)

        @pl.when(n == 0)
        def _():
            ds_scr[...] = jnp.zeros_like(ds_scr)

        heads = lambda r: jnp.stack([r[:, hh * HD:(hh + 1) * HD] for hh in range(HB)], axis=0)
        _, vjp = jax.vjp(_gdn_chunk, ss_ref[...], heads(q_ref), heads(k_ref), heads(v_ref), heads(z_ref), bl_ref[...],
                         ain_ref[...], al_ref[...], dtb_ref[...], nw_ref[...])
        ds, dq, dk, dv, dz, dbl, dain, dal, ddtb, dnw = vjp((ds_scr[...], heads(do_ref).astype(f32)))
        ds_scr[...] = ds
        for hh in range(HB):
            cs = slice(hh * HD, (hh + 1) * HD)
            dqkv_ref[0, :, cs] = dq[hh]
            dqkv_ref[1, :, cs] = dk[hh]
            dqkv_ref[2, :, cs] = dv[hh]
            dz_ref[:, cs] = dz[hh].astype(dz_ref.dtype)
        dbl_ref[...] = dbl
        dain_ref[...] = dain

        @pl.when(n == 0)
        def _():
            dal_ref[...] = dal
            ddtb_ref[...] = ddtb

        @pl.when(n > 0)
        def _():
            dal_ref[...] += dal
            ddtb_ref[...] += ddtb

        @pl.when((n == 0) & (h == 0))
        def _():
            dnw_ref[...] = dnw

        @pl.when((n > 0) | (h > 0))
        def _():
            dnw_ref[...] += dnw

    R = N - 1
    ch = lambda off: pl.BlockSpec((CHUNK, W), lambda h, n: (R - n, h + off))
    sc = pl.BlockSpec((HB, CHUNK, 1), lambda h, n: (h, R - n, 0))
    hs = pl.BlockSpec((HB, 1, 1), lambda h, n: (h, 0, 0))
    nws = pl.BlockSpec((1, HD), lambda h, n: (0, 0))
    return pl.pallas_call(
        body, grid=(NH // HB, N),
        in_specs=[ch(0), ch(nb), ch(2 * nb), ch(zc0), sc, sc, hs, hs, nws,
                  pl.BlockSpec((HB, None, HD, HD), lambda h, n: (h, R - n, 0, 0)),
                  pl.BlockSpec((CHUNK, W), lambda h, n: (R - n, h))],
        out_specs=[pl.BlockSpec((3, CHUNK, W), lambda h, n: (0, R - n, h)),
                   pl.BlockSpec((CHUNK, W), lambda h, n: (R - n, h)), sc, sc, hs, hs, nws],
        out_shape=[SDS((3, T, GW), f32), SDS((T, GW), _MXU), SDS((NH, T, 1), f32), SDS((NH, T, 1), f32),
                   SDS((NH, 1, 1), f32), SDS((NH, 1, 1), f32), SDS((1, HD), f32)],
        scratch_shapes=[pltpu.VMEM((HB, HD, HD), f32)], compiler_params=_cp(("arbitrary", "arbitrary")), name=name,
    )(qkv, qkv, qkv, p, blt, aint, alog, dtb, nw, ssave, do)


def _loss_head(name, y, tgt):
    T, D = y.shape
    tm = _rtile(T, 256)

    def body(y_ref, t_ref, dy_ref, l_ref):
        i = pl.program_id(0)
        err = y_ref[...] - t_ref[...]
        dy_ref[...] = err * (1.0 / D)
        part = 0.5 * jnp.sum(jnp.sum(err * err, axis=-1, keepdims=True) * (1.0 / D), axis=0, keepdims=True)

        @pl.when(i == 0)
        def _():
            l_ref[...] = part

        @pl.when(i > 0)
        def _():
            l_ref[...] += part

    row = pl.BlockSpec((tm, D), lambda i: (i, 0))
    return pl.pallas_call(
        body, grid=(T // tm,), in_specs=[row, row], out_specs=[row, pl.BlockSpec((1, 1), lambda i: (0, 0))],
        out_shape=[SDS((T, D), f32), SDS((1, 1), f32)], compiler_params=_cp(("arbitrary",)), name=name,
    )(y, tgt)


def _adam_math(w, g, m, v):
    m = ADAM_B1 * m + (1.0 - ADAM_B1) * g
    v = ADAM_B2 * v + (1.0 - ADAM_B2) * jnp.square(g)
    m_hat = m / (1.0 - ADAM_B1 ** ADAM_STEP)
    v_hat = v / (1.0 - ADAM_B2 ** ADAM_STEP)
    delta = -ADAM_LR * (m_hat / (jnp.sqrt(v_hat) + ADAM_EPS) + ADAM_WD * w)
    return delta, m, v


def _add_mine(name, full, recv, me, out_dtype):
    N, _, R, C = full.shape
    tr = _rtile(R, max(16, (1 << 19) // max(C, 1) // 16 * 16))

    def body(me_ref, a_ref, b_ref, o_ref):
        o_ref[...] = (a_ref[...].astype(f32) + b_ref[...].astype(f32)).astype(o_ref.dtype)

    blk = pl.BlockSpec((None, tr, C), lambda n, i, me_ref: (n, i, 0))
    return pl.pallas_call(
        body,
        grid_spec=pltpu.PrefetchScalarGridSpec(
            num_scalar_prefetch=1, grid=(N, R // tr),
            in_specs=[pl.BlockSpec((None, None, tr, C), lambda n, i, me_ref: (n, me_ref[0], i, 0)), blk], out_specs=blk),
        out_shape=SDS((N, R, C), out_dtype), compiler_params=_cp(("parallel", "parallel")), name=name,
    )(me, full, recv)


def _adamw_big(name, full, recv, me, w, m, v, l, accs):
    _, R, C = full.shape
    L = w.shape[0]
    tr = _rtile(R, max(16, (1 << 18) // max(C, 1) // 16 * 16))

    def body(me_ref, ga_ref, gb_ref, w_ref, m_ref, v_ref, a0, a1, a2, a3, g_ref, d_ref, nm_ref, nv_ref):
        g = ga_ref[...].astype(f32) + gb_ref[...].astype(f32)
        d, nm, nv = _adam_math(w_ref[...], g, m_ref[...], v_ref[...])
        g_ref[...] = g
        d_ref[...] = d
        nm_ref[...] = nm
        nv_ref[...] = nv

    blk = pl.BlockSpec((tr, C), lambda i, me_ref: (i, 0))
    lblk = pl.BlockSpec((None, tr, C), lambda i, me_ref: (l, i, 0))
    untouched = pl.BlockSpec(memory_space=pl.ANY)
    return pl.pallas_call(
        body,
        grid_spec=pltpu.PrefetchScalarGridSpec(
            num_scalar_prefetch=1, grid=(R // tr,),
            in_specs=[pl.BlockSpec((None, tr, C), lambda i, me_ref: (me_ref[0], i, 0)), blk, lblk, lblk, lblk] + [untouched] * 4,
            out_specs=[lblk] * 4),
        out_shape=[SDS((L, R, C), f32)] * 4, input_output_aliases={6: 0, 7: 1, 8: 2, 9: 3},
        compiler_params=_cp(("parallel",)), name=name,
    )(me, full, recv, w, m, v, *accs)


def _adamw_small(name, gall, w, m, v):
    _, R, C = gall.shape
    tr = _rtile(R, 512)

    def body(ga_ref, w_ref, m_ref, v_ref, g_ref, d_ref, nm_ref, nv_ref):
        g = ga_ref[0]
        for s in range(1, 8):
            g = g + ga_ref[s]
        d, nm, nv = _adam_math(w_ref[...], g, m_ref[...], v_ref[...])
        g_ref[...] = g
        d_ref[...] = d
        nm_ref[...] = nm
        nv_ref[...] = nv

    blk = pl.BlockSpec((tr, C), lambda i: (i, 0))
    return pl.pallas_call(
        body, grid=(R // tr,), in_specs=[pl.BlockSpec((8, tr, C), lambda i: (0, i, 0)), blk, blk, blk], out_specs=[blk] * 4,
        out_shape=[SDS((R, C), f32)] * 4, compiler_params=_cp(("parallel",)), name=name,
    )(gall, w, m, v)


def _peer(axis):
    x, y, c = lax.axis_index("x"), lax.axis_index("y"), lax.axis_index("c")
    me = {"x": x, "y": y, "c": c}[axis]
    peer = {"x": (1 - x, y, c), "y": (x, 1 - y, c), "c": (x, y, 1 - c)}[axis]
    return me, peer


def _held(ref, done):
    idx = tuple(slice(None) if a in done else lax.axis_index(a) for a in ("x", "y", "c"))
    return ref.at[idx]


def _gather_stage(name, bufs, axes, dones):
    n = len(bufs)
    hbm = pl.BlockSpec(memory_space=pltpu.HBM)

    def body(*refs):
        outs = refs[n:2 * n]
        send_sems, recv_sems = refs[2 * n:]
        cps = []
        for t in range(n):
            _, peer = _peer(axes[t])
            blk = _held(outs[t], dones[t])
            cps.append(pltpu.make_async_remote_copy(src_ref=blk, dst_ref=blk, send_sem=send_sems.at[t],
                                                    recv_sem=recv_sems.at[t], device_id=peer, device_id_type=_MESH_T))
        for cp in cps:
            cp.start()
        for cp in cps:
            cp.wait()

    return pl.pallas_call(
        body, in_specs=[hbm] * n, out_specs=[hbm] * n, out_shape=[SDS(b.shape, b.dtype) for b in bufs],
        input_output_aliases={t: t for t in range(n)},
        scratch_shapes=[pltpu.SemaphoreType.DMA((n,)), pltpu.SemaphoreType.DMA((n,))], name=name,
    )(*bufs)


_HBM = pl.BlockSpec(memory_space=pltpu.HBM)
_SEM = pl.BlockSpec(memory_space=pltpu.SEMAPHORE)
_EFFECT = pltpu.SideEffectType.DATAFLOW_SIDE_EFFECTING


def _split_start(name, arrays, n_copies, make_copies):
    na = len(arrays)

    def body(*refs):
        ins = refs[:na]
        send_sems, recv_sems = refs[na], refs[na + 1]
        token = refs[2 * na + 2]
        for cp in make_copies(ins, send_sems, recv_sems):
            cp.start()
        token[...] = jnp.zeros_like(token)

    res = pl.pallas_call(
        body, name=name,
        out_shape=(pltpu.SemaphoreType.DMA((n_copies,)), pltpu.SemaphoreType.DMA((n_copies,)),
                   *[pltpu.HBM(a.shape, a.dtype) for a in arrays], SDS((8, 128), f32)),
        in_specs=[_HBM] * na, out_specs=(_SEM, _SEM, *[_HBM] * na, pl.BlockSpec(memory_space=pltpu.VMEM)),
        input_output_aliases={i: 2 + i for i in range(na)},
        compiler_params=pltpu.CompilerParams(has_side_effects=_EFFECT),
    )(*[pltpu.with_memory_space_constraint(a, pltpu.HBM) for a in arrays])
    return res[0], res[1], list(res[2:2 + na]), res[2 + na]


def _split_wait(name, arrays, send_sems, recv_sems, after, make_copies):
    na = len(arrays)

    def body(*refs):
        ins = refs[:na]
        for cp in make_copies(ins, refs[na], refs[na + 1]):
            cp.wait_send()
            cp.wait_recv()

    res = pl.pallas_call(
        body, name=name, out_shape=tuple(pltpu.HBM(a.shape, a.dtype) for a in arrays),
        in_specs=[_HBM] * na + [_SEM, _SEM, pl.BlockSpec(memory_space=pl.ANY)], out_specs=tuple([_HBM] * na),
        input_output_aliases={i: i for i in range(na)},
        compiler_params=pltpu.CompilerParams(has_side_effects=_EFFECT),
    )(*arrays, send_sems, recv_sems, after)
    return list(res)


def _gather_copies(axes, dones):
    def make(refs, send_sems, recv_sems):
        cps = []
        for t in range(len(axes)):
            _, peer = _peer(axes[t])
            blk = _held(refs[t], dones[t])
            cps.append(pltpu.make_async_remote_copy(src_ref=blk, dst_ref=blk, send_sem=send_sems.at[t],
                                                    recv_sem=recv_sems.at[t], device_id=peer, device_id_type=_MESH_T))
        return cps
    return make


def _scatter_copies(axes):
    n = len(axes)

    def make(refs, send_sems, recv_sems):
        cps = []
        for t in range(n):
            me, peer = _peer(axes[t])
            cps.append(pltpu.make_async_remote_copy(
                src_ref=refs[t].at[:, 1 - me], dst_ref=refs[n + t], send_sem=send_sems.at[t], recv_sem=recv_sems.at[t],
                device_id=peer, device_id_type=_MESH_T))
        return cps
    return make


class _AsyncGather:
    def __init__(self, pfx, tensors, paths):
        x, y, c = (lax.axis_index(a) for a in ("x", "y", "c"))
        self.pfx, self.shapes = pfx, [tuple(t.shape) for t in tensors]
        self.bufs = [lax.dynamic_update_slice(lax.empty((2, 2, 2) + tuple(t.shape), t.dtype), t[None, None, None],
                                              (x, y, c) + (0,) * t.ndim) for t in tensors]
        self.orders = [tuple(p) + ("c",) for p in paths]
        self.ph = 0

    def _make(self):
        return _gather_copies([o[self.ph] for o in self.orders], [o[:self.ph] for o in self.orders])

    def start(self):
        self.ss, self.rs, self.bufs, tok = _split_start(f"{self.pfx}_start{self.ph}", self.bufs, len(self.bufs), self._make())
        return tok

    def wait(self, after):
        self.bufs = _split_wait(f"{self.pfx}_wait{self.ph}", self.bufs, self.ss, self.rs, after, self._make())
        self.ph += 1

    def result(self):
        return [b.reshape((8,) + s) for b, s in zip(self.bufs, self.shapes)]


class _AsyncReduceScatter:
    def __init__(self, pfx, tensors, paths):
        self.pfx = pfx
        self.rcs = [tuple(t.shape[1:]) for t in tensors]
        self.orders = [("c",) + tuple(p) for p in paths]
        self.left = [["x", "y", "c"] for _ in tensors]
        self.cur = list(tensors)
        self.ph = 0

    def start(self):
        n = len(self.cur)
        views = []
        for i, (t, rc) in enumerate(zip(self.cur, self.rcs)):
            pos = self.left[i].index(self.orders[i][self.ph])
            nb, na = 2 ** pos, 2 ** (len(self.left[i]) - pos - 1)
            views.append(t.reshape((nb, 2, na * rc[0], rc[1])))
        lands = [lax.empty((v.shape[0],) + tuple(v.shape[2:]), v.dtype) for v in views]
        self.make = _scatter_copies([o[self.ph] for o in self.orders])
        self.ss, self.rs, arrs, tok = _split_start(f"{self.pfx}_start{self.ph}", views + lands, n, self.make)
        self.arrs = arrs
        return tok

    def wait(self, after):
        n = len(self.cur)
        arrs = _split_wait(f"{self.pfx}_wait{self.ph}", self.arrs, self.ss, self.rs, after, self.make)
        views, recvs = arrs[:n], arrs[n:]
        ph = self.ph
        if ph == 2:
            self.out = [(v[0], r[0], o[2]) for v, r, o in zip(views, recvs, self.orders)]
        else:
            self.cur = [_add_mine(f"{self.pfx}_add{ph}_{i}", v, r, _coord(o[ph]), v.dtype)
                        for i, (v, r, o) in enumerate(zip(views, recvs, self.orders))]
            for i, o in enumerate(self.orders):
                self.left[i].remove(o[ph])
        self.ph += 1


def _coord(axis):
    return lax.axis_index(axis).astype(jnp.int32).reshape(1)


def _all_gather(pfx, tensors, paths):
    x, y, c = (lax.axis_index(a) for a in ("x", "y", "c"))
    bufs = []
    for t in tensors:
        zero = (0,) * t.ndim
        bufs.append(lax.dynamic_update_slice(lax.empty((2, 2, 2) + tuple(t.shape), t.dtype), t[None, None, None],
                                             (x, y, c) + zero))
    orders = [tuple(p) + ("c",) for p in paths]
    for ph in range(3):
        bufs = _gather_stage(f"{pfx}_{ph}", bufs, [o[ph] for o in orders], [o[:ph] for o in orders])
    return [b.reshape((8,) + tuple(t.shape)) for b, t in zip(bufs, tensors)]


def _mm_nn(name, a, w, out_dtype, tn_pref=1024):
    T, K = a.shape
    N = w.shape[1]
    tm, tn, tk = _rtile(T, 512), _tile(N, tn_pref), _tile(K, 2048)
    return _mm(
        name, a, w, _NN, (T // tm, N // tn, K // tk),
        pl.BlockSpec((tm, tk), lambda i, j, k: (i, k)), pl.BlockSpec((tk, tn), lambda i, j, k: (k, j)), (tm, tn),
        [], [], [SDS((T, N), out_dtype)], [pl.BlockSpec((tm, tn), lambda i, j, k: (i, j))], _store_epi,
    )[0]


def _mm_tn(name, a, b, out_dtype):
    T, M = a.shape
    N = b.shape[1]
    tm, tn, tk = _tile(M, 512), _tile(N, 2048), _rtile(T, 512)
    return _mm(
        name, a, b, _TN, (M // tm, N // tn, T // tk),
        pl.BlockSpec((tk, tm), lambda i, j, k: (k, i)), pl.BlockSpec((tk, tn), lambda i, j, k: (k, j)), (tm, tn),
        [], [], [SDS((M, N), out_dtype)], [pl.BlockSpec((tm, tn), lambda i, j, k: (i, j))], _store_epi,
    )[0]


def _mm_tn_slots(name, a, b, out_dtype):
    T, M = a.shape
    NS = b.shape[1] // 8
    tm, tk = _tile(M, 512), _rtile(T, 512)
    return _mm(
        name, a, b, _TN, (M // tm, 8, T // tk),
        pl.BlockSpec((tk, tm), lambda i, j, k: (k, i)), pl.BlockSpec((tk, NS), lambda i, j, k: (k, j)), (tm, NS),
        [], [], [SDS((8, M, NS), out_dtype)], [pl.BlockSpec((None, tm, NS), lambda i, j, k: (j, i, 0))], _store_epi,
    )[0]


def _mm_nt_slots(name, a, w8, out_dtype):
    T = a.shape[0]
    _, M, NS = w8.shape
    tm, tn = _rtile(T, 512), _tile(M, 1024)
    return _mm(
        name, a, w8, _NT, (T // tm, M // tn, 8),
        pl.BlockSpec((tm, NS), lambda i, j, k: (i, k)), pl.BlockSpec((None, tn, NS), lambda i, j, k: (k, j, 0)), (tm, tn),
        [], [], [SDS((T, M), out_dtype)], [pl.BlockSpec((tm, tn), lambda i, j, k: (i, j))], _store_epi,
    )[0]


def _colsum_kernel(name, fn, ins, in_cols, outs_elem, n_sum, C):
    T = ins[0].shape[0]
    tm = _rtile(T, 256)
    ne = len(outs_elem)

    def body(*refs):
        i = pl.program_id(0)
        iv = [r[...] for r in refs[:len(ins)]]
        res = fn(*iv)
        for o, r in zip(refs[len(ins):len(ins) + ne], res[:ne]):
            o[...] = r.astype(o.dtype)
        sums = [jnp.sum(r, axis=0, keepdims=True) for r in res[ne:]]

        @pl.when(i == 0)
        def _():
            for o, s in zip(refs[len(ins) + ne:], sums):
                o[...] = s

        @pl.when(i > 0)
        def _():
            for o, s in zip(refs[len(ins) + ne:], sums):
                o[...] += s

    in_specs = []
    for arr, off in zip(ins, in_cols):
        if off is None:
            in_specs.append(pl.BlockSpec((1, C), lambda i: (0, 0)))
        else:
            in_specs.append(pl.BlockSpec((tm, C), lambda i, off=off: (i, off)))
    row = pl.BlockSpec((tm, C), lambda i: (i, 0))
    vec = pl.BlockSpec((1, C), lambda i: (0, 0))
    return pl.pallas_call(
        body, grid=(T // tm,), in_specs=in_specs, out_specs=[row] * ne + [vec] * n_sum,
        out_shape=[SDS((T, C), dt) for dt in outs_elem] + [SDS((1, C), f32)] * n_sum,
        compiler_params=_cp(("arbitrary",)), name=name,
    )(*ins)


def _merge(gs, gg, a_s, a_g):
    return jax.nn.sigmoid(gs) * a_s + jax.nn.sigmoid(gg) * a_g


def _glu(yg, lp):
    return yg * jax.nn.sigmoid(lp)


_BIG = ("ffn1_w_gu", "ffn1_w_down", "w_in", "conv_w", "glu_w", "w_br_ssm", "w_br_gdn", "w_out", "ffn2_w_gu", "ffn2_w_down")
_PATHS = ("yx", "yx", "xy", "xy", "yx", "yx", "yx", "yx", "xy", "xy")
_SMALL = ("ln1_g", "ln1_b", "ssm_a_re", "ssm_a_im", "ssm_log_dt", "ssm_b_re", "ssm_b_im", "ssm_c_re", "ssm_c_im", "ssm_d",
          "glu_b", "gdn_a_log", "gdn_dt_bias", "gdn_norm_w", "ln2_g", "ln2_b", "ln3_g", "ln3_b")
_ORDER = ("ffn1_w_gu", "ffn1_w_down", "ln1_g", "ln1_b", "w_in", "conv_w", "ssm_a_re", "ssm_a_im", "ssm_log_dt", "ssm_b_re",
          "ssm_b_im", "ssm_c_re", "ssm_c_im", "ssm_d", "glu_w", "glu_b", "gdn_a_log", "gdn_dt_bias", "gdn_norm_w", "w_br_ssm",
          "w_br_gdn", "w_out", "ln2_g", "ln2_b", "ffn2_w_gu", "ffn2_w_down", "ln3_g", "ln3_b")


def _step(x, tgt, W, M, V):
    T, D = x.shape[1], x.shape[2]
    L = W["ffn1_w_gu"].shape[0]
    G, P = W["ssm_a_re"].shape[1:]
    H = W["ssm_b_re"].shape[3]
    SW = G * H
    NH = W["gdn_a_log"].shape[1]
    HD = W["gdn_norm_w"].shape[1]
    GW = NH * HD
    KC = W["conv_w"].shape[1]
    DS = D // 8
    alpha = (2.0 * L) ** 0.25
    o_b = SW + 4 * GW
    o_gs = o_b + 2 * NH
    IN = o_gs + 2 * D
    NM = IN - 2 * NH
    m_qkv, m_z, m_gs, m_gg = SW, SW + 3 * GW, SW + 4 * GW, SW + 4 * GW + D
    J = G // 8

    x0 = x[0]
    tg = tgt[0]

    def vec(name, l):
        return W[name][l:l + 1]

    saves, weights = [], []
    xc, xcb = x0, x0.astype(_MXU)
    def shards(l):
        return [W["ffn1_w_gu"][l].astype(_MXU), W["ffn1_w_down"][l].astype(_MXU), W["w_in"][l].astype(_MXU), W["conv_w"][l],
                W["glu_w"][l].astype(_MXU), W["w_br_ssm"][l].astype(_MXU), W["w_br_gdn"][l].astype(_MXU),
                W["w_out"][l].astype(_MXU), W["ffn2_w_gu"][l].astype(_MXU), W["ffn2_w_down"][l].astype(_MXU)]

    def dep(a, tok):
        return a if tok is None else a + tok[0:1, 0:1].astype(a.dtype)

    gathered = _all_gather("ag", shards(0), _PATHS)
    ahead = {}
    for l in range(L):
        toks = []
        for k in ([1, 2] if l == 0 else [l + 2]):
            if k < L:
                ahead[k] = _AsyncGather("agp", shards(k), _PATHS)
                toks.append(ahead[k].start())
        nxt = ahead.get(l + 1)
        if nxt and l >= 1:
            nxt.wait(xc)
            toks.append(nxt.start())
        tok = functools.reduce(lambda a, b: a + b, toks) if toks else None
        wgu1, wd1, win8, cw8, wglu, wbs, wbg, wo, wgu2, wd2 = gathered
        wd1 = wd1.reshape(-1, D)
        wd2 = wd2.reshape(-1, D)
        wglu = wglu.reshape(SW, SW)
        wo = wo.reshape(D, D)
        win = jnp.transpose(win8, (1, 0, 2)).reshape(D, IN)
        wmain = jnp.concatenate([win[:, :o_b], win[:, o_gs:]], axis=1)
        wba = jnp.pad(win[:, o_b:o_gs], ((0, 0), (0, 128 - 2 * NH)))
        cw = jnp.transpose(cw8, (1, 0, 2)).reshape(KC, 3 * GW)
        wl = dict(wgu1=wgu1, wd1=wd1, wmain=wmain, wba=wba, cw=cw, wglu=wglu, wbs=wbs, wbg=wbg, wo=wo, wgu2=wgu2, wd2=wd2)
        weights.append(wl)
        sv = {}

        gate, up, hh = _ffn_up("ffn_up", xcb, wgu1)
        x1, x1b, xh1, r1 = _mm_ln("ffn_down_ln", hh, wd1, xc, dep(vec("ln1_g", l), tok), vec("ln1_b", l), alpha, 0.5)
        sv["f1"] = dict(xb=xcb, gate=gate, up=up, h=hh, xhat=xh1, rstd=r1)

        p = _mm_nn("mix_in", x1b, wmain, f32)
        pba = _mm_nn("mix_in_ba", x1b, wba, f32)
        b_re_t = jnp.transpose(W["ssm_b_re"][l], (2, 0, 1))
        b_im_t = jnp.transpose(W["ssm_b_im"][l], (2, 0, 1))
        zoh_in = (W["ssm_a_re"][l], W["ssm_a_im"][l], W["ssm_log_dt"][l][:, None], b_re_t, b_im_t)
        lbr, lbi, bbr_t, bbi_t = _zoh_fwd("zoh", *zoh_in)
        bblk_r = _blockdiag(jnp.transpose(bbr_t, (1, 0, 2)))
        bblk_i = _blockdiag(jnp.transpose(bbi_t, (1, 0, 2)))
        cblkT_r = _blockdiag(W["ssm_c_re"][l])
        cblkT_in = _blockdiag(-W["ssm_c_im"][l])
        lbr_f, lbi_f = lbr.reshape(1, G * P), lbi.reshape(1, G * P)
        bur, bui = _bd2("s5_bu", p, 0, bblk_r, bblk_i)
        if nxt and l == 0:
            nxt.wait(bur)
            tok = nxt.start()
        sr, si = _s5_scan("s5_scan", bur, bui, dep(lbr_f, tok), lbi_f)
        dflat = W["ssm_d"][l].reshape(1, SW)

        def out_epi(acc, ex, outs):
            y_raw = acc + ex[1][...] * ex[0][...]
            yg = jax.nn.gelu(y_raw)
            outs[0][...] = y_raw
            outs[1][...] = yg
            outs[2][...] = yg.astype(outs[2].dtype)

        y_raw, yg, ygb = _bd_sum(
            "s5_out", sr, si, jnp.transpose(cblkT_r, (0, 2, 1)), jnp.transpose(cblkT_in, (0, 2, 1)), [p, dflat],
            lambda tm, nb: [pl.BlockSpec((tm, nb), lambda i, j: (i, j)), pl.BlockSpec((1, nb), lambda i, j: (0, j))],
            [SDS((T, SW), f32), SDS((T, SW), f32), SDS((T, SW), _MXU)], out_epi)

        tmg, tng, tkg = _rtile(T, 512), _tile(SW, 512), _tile(SW, 1024)

        def glu_epi(acc, ex, outs):
            lp = acc + ex[1][...]
            outs[0][...] = lp
            outs[1][...] = _glu(ex[0][...], lp).astype(outs[1].dtype)

        lp, ysb = _mm(
            "s5_glu", ygb, wglu, _NN, (T // tmg, SW // tng, SW // tkg),
            pl.BlockSpec((tmg, tkg), lambda i, j, k: (i, k)), pl.BlockSpec((tkg, tng), lambda i, j, k: (k, j)), (tmg, tng),
            [yg, vec("glu_b", l)], [pl.BlockSpec((tmg, tng), lambda i, j, k: (i, j)), pl.BlockSpec((1, tng), lambda i, j, k: (0, j))],
            [SDS((T, SW), f32), SDS((T, SW), _MXU)], [pl.BlockSpec((tmg, tng), lambda i, j, k: (i, j))] * 2, glu_epi)

        qkv = _conv_fwd("gdn_conv", p, m_qkv, cw, 3 * GW)
        blt = jnp.transpose(pba[:, :NH])[:, :, None]
        aint = jnp.transpose(pba[:, NH:2 * NH])[:, :, None]
        alog = W["gdn_a_log"][l].reshape(NH, 1, 1)
        dtb = W["gdn_dt_bias"][l].reshape(NH, 1, 1)
        nw = vec("gdn_norm_w", l)
        og, ssave = _gdn_fwd("gdn", qkv, p, m_z, blt, aint, alog, dtb, nw, NH, HD)

        a_s = _mm(
            "br_ssm", ysb, wbs, _NN, (T // tmg, 8, SW // tkg),
            pl.BlockSpec((tmg, tkg), lambda i, j, k: (i, k)), pl.BlockSpec((None, tkg, DS), lambda i, j, k: (j, k, 0)), (tmg, DS),
            [], [], [SDS((T, D), f32)], [pl.BlockSpec((tmg, DS), lambda i, j, k: (i, j))], _store_epi)[0]
        tkd = _tile(GW, 1024)
        gsb, ggb = m_gs // DS, m_gg // DS

        def merge_epi(acc, ex, outs):
            outs[0][...] = acc
            outs[1][...] = _merge(ex[1][...], ex[2][...], ex[0][...], acc).astype(outs[1].dtype)

        tile_ij = pl.BlockSpec((tmg, DS), lambda i, j, k: (i, j))
        a_g, merged = _mm(
            "br_gdn_merge", og, wbg, _NN, (T // tmg, 8, GW // tkd),
            pl.BlockSpec((tmg, tkd), lambda i, j, k: (i, k)), pl.BlockSpec((None, tkd, DS), lambda i, j, k: (j, k, 0)), (tmg, DS),
            [a_s, p, p], [tile_ij, pl.BlockSpec((tmg, DS), lambda i, j, k: (i, j + gsb)),
                          pl.BlockSpec((tmg, DS), lambda i, j, k: (i, j + ggb))],
            [SDS((T, D), f32), SDS((T, D), _MXU)], [tile_ij, tile_ij], merge_epi)
        if nxt:
            nxt.wait(merged)
            tok = nxt.start()
        x2, x2b, xh2, r2 = _mm_ln("mix_out_ln", merged, wo, x1, dep(vec("ln2_g", l), tok), vec("ln2_b", l), alpha, 1.0)
        sv["mx"] = dict(x1b=x1b, p=p, zoh_in=zoh_in, lbr_f=lbr_f, lbi_f=lbi_f, bblk_r=bblk_r, bblk_i=bblk_i, cblkT_r=cblkT_r,
                        cblkT_in=cblkT_in, sr=sr, si=si, dflat=dflat, y_raw=y_raw, yg=yg, ygb=ygb, lp=lp, ysb=ysb, qkv=qkv,
                        blt=blt, aint=aint, alog=alog, dtb=dtb, nw=nw, og=og, ssave=ssave, a_s=a_s, a_g=a_g, merged=merged,
                        xhat=xh2, rstd=r2)

        gate2, up2, hh2 = _ffn_up("ffn_up", x2b, wgu2)
        x3, x3b, xh3, r3 = _mm_ln("ffn_down_ln", hh2, wd2, x2, vec("ln3_g", l), vec("ln3_b", l), alpha, 0.5)
        sv["f2"] = dict(xb=x2b, gate=gate2, up=up2, h=hh2, xhat=xh3, rstd=r3)
        saves.append(sv)
        xc, xcb = x3, x3b
        if nxt:
            nxt.wait(x3)
            gathered = nxt.result()

    dy, loss_part = _loss_head("loss_head", xc, tg)
    loss = lax.psum(loss_part[0, 0], ("x", "y", "c"))

    big_out = {n: [lax.empty(W[n].shape, f32) for _ in range(4)] for n in _BIG}
    small_g = {n: [None] * L for n in _SMALL}
    pend = None
    for l in reversed(range(L)):
        sv, wl = saves[l], weights[l]
        mx = sv["mx"]
        p = mx["p"]
        tok = pend.start() if pend else None
        dx2, dwgu2, dwd2, dg3, db3 = _ffn_bwd("ffn_b", dy, sv["f2"], wl["wgu2"], wl["wd2"], dep(vec("ln3_g", l), tok), alpha)
        small_g["ln3_g"][l], small_g["ln3_b"][l] = dg3[0], db3[0]

        if pend:
            pend.wait(dx2)
            tok = pend.start()
        dz2, dmixb, dg2, db2 = _ln_bwd("mix_lnb", dx2, mx["xhat"], mx["rstd"], dep(vec("ln2_g", l), tok), 1.0)
        small_g["ln2_g"][l], small_g["ln2_b"][l] = dg2[0], db2[0]
        tmg, tkd = _rtile(T, 512), _tile(D, 512)
        tnq = 512 if (m_gs % 512 == 0 and D % 512 == 0) else DS
        tkq = _tile(D, 2048)
        gsb, ggb = m_gs // tnq, m_gg // tnq

        def dmerge_epi(acc, ex, outs):
            _, vjp = jax.vjp(_merge, ex[0][...], ex[1][...], ex[2][...], ex[3][...])
            dgs, dgg, das, dag = vjp(acc)
            outs[0][...] = das.astype(outs[0].dtype)
            outs[1][...] = dag.astype(outs[1].dtype)
            outs[2][...] = dgs.astype(outs[2].dtype)
            outs[3][...] = dgg.astype(outs[3].dtype)

        tile_ij = pl.BlockSpec((tmg, tnq), lambda i, j, k: (i, j))
        das, dag, dgs, dgg = _mm(
            "mix_dmerge", dmixb, wl["wo"], _NT, (T // tmg, D // tnq, D // tkq),
            pl.BlockSpec((tmg, tkq), lambda i, j, k: (i, k)), pl.BlockSpec((tnq, tkq), lambda i, j, k: (j, k)), (tmg, tnq),
            [p, p, mx["a_s"], mx["a_g"]],
            [pl.BlockSpec((tmg, tnq), lambda i, j, k: (i, j + gsb)), pl.BlockSpec((tmg, tnq), lambda i, j, k: (i, j + ggb)),
             tile_ij, tile_ij],
            [SDS((T, D), _MXU)] * 4, [tile_ij] * 4, dmerge_epi)
        dwo = _mm_tn("mix_dwo", mx["merged"], dmixb, _GDT)
        dys = _mm_nt_slots("br_ssm_dx", das, wl["wbs"], f32)
        dog = _mm_nt_slots("br_gdn_dx", dag, wl["wbg"], f32)
        dwbs = _mm_tn_slots("br_ssm_dw", mx["ysb"], das, _GDT)
        dwbg = _mm_tn_slots("br_gdn_dw", mx["og"], dag, _GDT)

        def glu_b_fn(dys_t, yg_t, lp_t):
            _, vjp = jax.vjp(_glu, yg_t, lp_t)
            dyg1, dlp = vjp(dys_t)
            return dyg1, dlp, dlp

        dyg1, dlpb, dglub = _colsum_kernel("s5_glu_b", glu_b_fn, [dys, mx["yg"], mx["lp"]], [0, 0, 0], [f32, _MXU], 1, SW)
        small_g["glu_b"][l] = dglub[0]
        dwglu = _mm_tn("s5_dwglu", mx["ygb"], dlpb, _GDT)
        tng, tkg = _tile(SW, 512), _tile(SW, 512)

        def dyraw_epi(acc, ex, outs):
            _, vjp = jax.vjp(jax.nn.gelu, ex[1][...])
            (d,) = vjp(ex[0][...] + acc)
            outs[0][...] = d

        t_ij = pl.BlockSpec((tmg, tng), lambda i, j, k: (i, j))
        (dyraw,) = _mm(
            "s5_dyraw", dlpb, wl["wglu"], _NT, (T // tmg, SW // tng, SW // tkg),
            pl.BlockSpec((tmg, tkg), lambda i, j, k: (i, k)), pl.BlockSpec((tng, tkg), lambda i, j, k: (j, k)), (tmg, tng),
            [dyg1, mx["y_raw"]], [t_ij, t_ij], [SDS((T, SW), f32)], [t_ij], dyraw_epi)

        def dd_fn(dyr, u_t, d_t):
            return d_t * dyr, dyr * u_t

        dud, dd = _colsum_kernel("s5_dd", dd_fn, [dyraw, p, mx["dflat"]], [0, 0, None], [f32], 1, SW)
        small_g["ssm_d"][l] = dd.reshape(G, H)
        dsr, dsi = _bd2("s5_ds", dyraw, 0, mx["cblkT_r"], mx["cblkT_in"])
        dcb_r, dcb_i = _bdT2("s5_dc", mx["sr"], mx["si"], 0, dyraw, dyraw, 0, 8 * P, 8 * H, J)
        small_g["ssm_c_re"][l] = _blockdiag_extract(jnp.transpose(dcb_r, (0, 2, 1)), H, P)
        small_g["ssm_c_im"][l] = -_blockdiag_extract(jnp.transpose(dcb_i, (0, 2, 1)), H, P)
        ar, ai, dlr, dli = _s5_scan_bwd("s5_scan_b", dsr, dsi, mx["sr"], mx["si"], mx["lbr_f"], mx["lbi_f"])

        def du_epi(acc, ex, outs):
            outs[0][...] = (acc + ex[0][...]).astype(outs[0].dtype)

        (du,) = _bd_sum(
            "s5_du", ar, ai, jnp.transpose(mx["bblk_r"], (0, 2, 1)), jnp.transpose(mx["bblk_i"], (0, 2, 1)), [dud],
            lambda tm, nb: [pl.BlockSpec((tm, nb), lambda i, j: (i, j))], [SDS((T, SW), _MXU)], du_epi)
        dbb_r, dbb_i = _bdT2("s5_db", p, p, 0, ar, ai, 0, 8 * H, 8 * P, J)
        dbbr_t = jnp.transpose(_blockdiag_extract(dbb_r, H, P), (1, 0, 2))
        dbbi_t = jnp.transpose(_blockdiag_extract(dbb_i, H, P), (1, 0, 2))
        da_re, da_im, dlog_dt, dbre_t, dbim_t = _zoh_bwd("zoh_b", *mx["zoh_in"], dlr.reshape(G, P), dli.reshape(G, P),
                                                         dbbr_t, dbbi_t)
        small_g["ssm_a_re"][l], small_g["ssm_a_im"][l], small_g["ssm_log_dt"][l] = da_re, da_im, dlog_dt[:, 0]
        small_g["ssm_b_re"][l] = jnp.transpose(dbre_t, (1, 2, 0))
        small_g["ssm_b_im"][l] = jnp.transpose(dbim_t, (1, 2, 0))

        dqkv3, dzb, dbl, dain, dal, ddtb, dnw = _gdn_bwd("gdn_b", mx["qkv"], p, m_z, mx["blt"], mx["aint"], mx["alog"],
                                                         mx["dtb"], mx["nw"], mx["ssave"], dog, NH, HD)
        small_g["gdn_a_log"][l], small_g["gdn_dt_bias"][l], small_g["gdn_norm_w"][l] = dal[:, 0, 0], ddtb[:, 0, 0], dnw[0]
        dqkv_pre, dcw = _conv_bwd("gdn_conv_b", p, m_qkv, wl["cw"], dqkv3)
        if pend:
            pend.wait(dqkv_pre)
            tok = pend.start()

        dpm = jnp.concatenate([du, dqkv_pre, dzb, dgs, dgg], axis=1)
        dpba = dep(jnp.concatenate([jnp.transpose(dbl[:, :, 0]), jnp.transpose(dain[:, :, 0]),
                                    jnp.zeros((T, 128 - 2 * NH), f32)], axis=1), tok).astype(_MXU)
        tnd, tkm = _tile(D, 1024), _tile(NM, 2304)
        t_ba = _mm(
            "mix_dx_ba", dpba, wl["wba"], _NT, (T // tmg, D // tnd, 1),
            pl.BlockSpec((tmg, 128), lambda i, j, k: (i, 0)), pl.BlockSpec((tnd, 128), lambda i, j, k: (j, 0)), (tmg, tnd),
            [], [], [SDS((T, D), f32)], [pl.BlockSpec((tmg, tnd), lambda i, j, k: (i, j))], _store_epi)[0]

        def dx1_epi(acc, ex, outs):
            outs[0][...] = alpha * ex[0][...] + ex[1][...] + acc

        t_d = pl.BlockSpec((tmg, tnd), lambda i, j, k: (i, j))
        (dx1,) = _mm(
            "mix_dx", dpm, wl["wmain"], _NT, (T // tmg, D // tnd, NM // tkm),
            pl.BlockSpec((tmg, tkm), lambda i, j, k: (i, k)), pl.BlockSpec((tnd, tkm), lambda i, j, k: (j, k)), (tmg, tnd),
            [dz2, t_ba], [t_d, t_d], [SDS((T, D), f32)], [t_d], dx1_epi)
        tnm = _tile(NM, 1024)
        tkt = _tile(T, 2048)
        dwmain = _mm(
            "mix_dw", jnp.transpose(mx["x1b"]), dpm, _NN, (D // tkd, NM // tnm, T // tkt),
            pl.BlockSpec((tkd, tkt), lambda i, j, k: (i, k)), pl.BlockSpec((tkt, tnm), lambda i, j, k: (k, j)), (tkd, tnm),
            [], [], [SDS((D, NM), _GDT)], [pl.BlockSpec((tkd, tnm), lambda i, j, k: (i, j))], _store_epi)[0]
        dwba = _mm_tn("mix_dw_ba", mx["x1b"], dpba, _GDT)
        dwin = jnp.concatenate([dwmain[:, :o_b], dwba[:, :2 * NH], dwmain[:, o_b:]], axis=1)
        dwin8 = jnp.transpose(dwin.reshape(D, 8, IN // 8), (1, 0, 2))
        dcw8 = jnp.transpose(dcw.reshape(KC, 8, 3 * GW // 8), (1, 0, 2))

        dx0, dwgu1, dwd1, dg1, db1 = _ffn_bwd("ffn_b", dx1, sv["f1"], wl["wgu1"], wl["wd1"], vec("ln1_g", l), alpha)
        small_g["ln1_g"][l], small_g["ln1_b"][l] = dg1[0], db1[0]
        dy = dx0

        parts = [dwgu1, dwd1.reshape(8, -1, D), dwin8, dcw8, dwglu.reshape(8, SW // 8, SW), dwbs, dwbg,
                 dwo.reshape(8, DS, D), dwgu2, dwd2.reshape(8, -1, D)]
        late = None
        if pend:
            pend.wait(dx0)
            if l > 0:
                for n, (full, recv, last) in zip(_BIG, pend.out):
                    big_out[n] = _adamw_big("adamw_" + n, full, recv, _coord(last), W[n], M[n], V[n], l + 1, big_out[n])
            else:
                late = pend.out
        pend = _AsyncReduceScatter("rsp", parts, _PATHS)

    seg = 8 * 128

    def padded(n):
        return -(-n // seg) * seg

    def pack(arrs):
        flat = jnp.concatenate([jnp.pad(a.reshape(-1), (0, padded(a.size) - a.size)) for a in arrs])
        n = flat.shape[0]
        rows = -(-n // (128 * 512)) * 512
        return jnp.pad(flat, (0, rows * 128 - n)).reshape(rows, 128)

    tok = pend.start()
    after = dy
    if late:
        for i, (n, (full, recv, last)) in enumerate(zip(_BIG, late)):
            recv = dep(recv, tok) if i == 0 else recv
            big_out[n] = _adamw_big("adamw_" + n, full, recv, _coord(last), W[n], M[n], V[n], 1, big_out[n])
            after = big_out[n][0]
    pend.wait(after)
    tok = pend.start()
    gs_full = [jnp.stack(small_g[n]).reshape(W[n].shape) for n in _SMALL]
    gpack = dep(pack(gs_full), tok)
    (gall,) = _all_gather("ag_small", [gpack], ["yx"])
    sg, sd, sm, sv_ = _adamw_small("adamw_small", gall, dep(pack([W[n] for n in _SMALL]), tok),
                                   pack([M[n] for n in _SMALL]), pack([V[n] for n in _SMALL]))
    pend.wait(sg)
    pend.start()
    pend.wait(sv_)
    for n, (full, recv, last) in zip(_BIG, pend.out):
        big_out[n] = _adamw_big("adamw_" + n, full, recv, _coord(last), W[n], M[n], V[n], 0, big_out[n])

    def unpack(packed):
        out, row = {}, 0
        for n in _SMALL:
            sz = math.prod(W[n].shape)
            rows = padded(sz) // 128
            out[n] = packed[row:row + rows].reshape(-1)[:sz].reshape(W[n].shape)
            row += rows
        return out

    res = [unpack(a) for a in (sg, sd, sm, sv_)]
    for n in _BIG:
        for i in range(4):
            res[i][n] = big_out[n][i]
    outs = [loss, dy[None]]
    for i in range(4):
        outs += [res[i][n] for n in _ORDER]
    return tuple(outs)


def kernel(x, ffn1_w_gu, ffn1_w_down, ln1_g, ln1_b, w_in, conv_w, ssm_a_re, ssm_a_im, ssm_log_dt, ssm_b_re, ssm_b_im, ssm_c_re, ssm_c_im, ssm_d, glu_w, glu_b, gdn_a_log, gdn_dt_bias, gdn_norm_w, w_br_ssm, w_br_gdn, w_out, ln2_g, ln2_b, ffn2_w_gu, ffn2_w_down, ln3_g, ln3_b, loss_target, m_ffn1_w_gu, m_ffn1_w_down, m_ln1_g, m_ln1_b, m_w_in, m_conv_w, m_ssm_a_re, m_ssm_a_im, m_ssm_log_dt, m_ssm_b_re, m_ssm_b_im, m_ssm_c_re, m_ssm_c_im, m_ssm_d, m_glu_w, m_glu_b, m_gdn_a_log, m_gdn_dt_bias, m_gdn_norm_w, m_w_br_ssm, m_w_br_gdn, m_w_out, m_ln2_g, m_ln2_b, m_ffn2_w_gu, m_ffn2_w_down, m_ln3_g, m_ln3_b, v_ffn1_w_gu, v_ffn1_w_down, v_ln1_g, v_ln1_b, v_w_in, v_conv_w, v_ssm_a_re, v_ssm_a_im, v_ssm_log_dt, v_ssm_b_re, v_ssm_b_im, v_ssm_c_re, v_ssm_c_im, v_ssm_d, v_glu_w, v_glu_b, v_gdn_a_log, v_gdn_dt_bias, v_gdn_norm_w, v_w_br_ssm, v_w_br_gdn, v_w_out, v_ln2_g, v_ln2_b, v_ffn2_w_gu, v_ffn2_w_down, v_ln3_g, v_ln3_b):
    given = dict(locals())
    W = {n: given[n] for n in _ORDER}
    M = {n: given["m_" + n] for n in _ORDER}
    V = {n: given["v_" + n] for n in _ORDER}
    return _step(x, loss_target, W, M, V)
```

```python
import functools
import math

import jax
import jax.numpy as jnp
from jax import lax
from jax.experimental import pallas as pl
from jax.experimental.pallas import tpu as pltpu

f32 = jnp.float32
_MXU = jnp.bfloat16
_GDT = jnp.bfloat16
_HP = lax.Precision.HIGHEST
_VMEM_LIMIT = 56 * 1024 * 1024
_MESH_T = pl.DeviceIdType.MESH

LN_EPS = 1e-5
RMS_EPS = 1e-6
L2_EPS = 1e-6
CHUNK = 64
ADAM_LR = 0.001
ADAM_B1 = 0.9
ADAM_B2 = 0.999
ADAM_EPS = 1e-08
ADAM_WD = 0.01
ADAM_STEP = 10

_NN = (((1,), (0,)), ((), ()))
_NT = (((1,), (1,)), ((), ()))
_TN = (((0,), (0,)), ((), ()))

SDS = jax.ShapeDtypeStruct


def _cp(sem):
    return pltpu.CompilerParams(dimension_semantics=sem, vmem_limit_bytes=_VMEM_LIMIT)


def _tile(n, pref):
    if n <= pref:
        return n
    t = (pref // 128) * 128
    while t >= 128:
        if n % t == 0:
            return t
        t -= 128
    return n


def _rtile(n, pref):
    if n <= pref:
        return n
    t = (pref // 16) * 16
    while t >= 16:
        if n % t == 0:
            return t
        t -= 16
    return n


def _mm(name, a, b, dims, grid, a_spec, b_spec, acc_shape, extras, extra_specs, out_shape, out_specs, epilogue):
    nk = grid[2]
    ne = len(extras)
    no = len(out_shape)

    def body(*refs):
        a_ref, b_ref = refs[0], refs[1]
        ex = refs[2:2 + ne]
        outs = refs[2 + ne:2 + ne + no]
        acc = refs[-1]
        k = pl.program_id(2)
        part = lax.dot_general(a_ref[...].astype(_MXU), b_ref[...].astype(_MXU), dims, preferred_element_type=f32)

        @pl.when(k == 0)
        def _():
            acc[...] = part

        @pl.when(k > 0)
        def _():
            acc[...] += part

        @pl.when(k == nk - 1)
        def _():
            epilogue(acc[...], ex, outs)

    return pl.pallas_call(
        body, grid=grid, in_specs=[a_spec, b_spec, *extra_specs], out_specs=list(out_specs), out_shape=list(out_shape),
        scratch_shapes=[pltpu.VMEM(acc_shape, f32)], compiler_params=_cp(("parallel", "parallel", "arbitrary")), name=name,
    )(a, b, *extras)


def _store_epi(acc, ex, outs):
    for o in outs:
        o[...] = acc.astype(o.dtype)


def _ln_epilogue(alpha, c):
    def epi(acc, ex, outs):
        x_ref, g_ref, b_ref = ex
        y_ref, yb_ref, xh_ref, r_ref = outs
        z = alpha * x_ref[...] + c * acc
        mu = jnp.mean(z, axis=-1, keepdims=True)
        zc = z - mu
        var = jnp.mean(zc * zc, axis=-1, keepdims=True)
        r = lax.rsqrt(var + LN_EPS)
        xh = zc * r
        y = xh * g_ref[...] + b_ref[...]
        y_ref[...] = y
        yb_ref[...] = y.astype(yb_ref.dtype)
        xh_ref[...] = xh
        r_ref[...] = r
    return epi


def _mm_ln(name, a, w, x, g, b, alpha, c):
    T, K = a.shape
    D = w.shape[1]
    tm, tk = _rtile(T, 512), _tile(K, 512)
    row = pl.BlockSpec((tm, D), lambda i, j, k: (i, 0))
    vec = pl.BlockSpec((1, D), lambda i, j, k: (0, 0))
    return _mm(
        name, a, w, _NN, (T // tm, 1, K // tk),
        pl.BlockSpec((tm, tk), lambda i, j, k: (i, k)), pl.BlockSpec((tk, D), lambda i, j, k: (k, 0)), (tm, D),
        [x, g, b], [row, vec, vec],
        [SDS((T, D), f32), SDS((T, D), _MXU), SDS((T, D), f32), SDS((T, 1), f32)],
        [row, row, row, pl.BlockSpec((tm, 1), lambda i, j, k: (i, 0))],
        _ln_epilogue(alpha, c),
    )


def _ln_bwd(name, dy, xhat, rstd, g, c):
    T, D = dy.shape
    tm = _rtile(T, 256)

    def body(dy_ref, xh_ref, r_ref, g_ref, dz_ref, df_ref, dg_ref, db_ref):
        i = pl.program_id(0)
        dyv = dy_ref[...]
        xh = xh_ref[...]
        dxh = dyv * g_ref[...]
        m1 = jnp.mean(dxh, axis=-1, keepdims=True)
        m2 = jnp.mean(dxh * xh, axis=-1, keepdims=True)
        dz = r_ref[...] * (dxh - m1 - xh * m2)
        dz_ref[...] = dz
        df_ref[...] = (c * dz).astype(df_ref.dtype)
        pg = jnp.sum(dyv * xh, axis=0, keepdims=True)
        pb = jnp.sum(dyv, axis=0, keepdims=True)

        @pl.when(i == 0)
        def _():
            dg_ref[...] = pg
            db_ref[...] = pb

        @pl.when(i > 0)
        def _():
            dg_ref[...] += pg
            db_ref[...] += pb

    row = pl.BlockSpec((tm, D), lambda i: (i, 0))
    vec = pl.BlockSpec((1, D), lambda i: (0, 0))
    return pl.pallas_call(
        body, grid=(T // tm,), in_specs=[row, row, pl.BlockSpec((tm, 1), lambda i: (i, 0)), vec],
        out_specs=[row, row, vec, vec],
        out_shape=[SDS((T, D), f32), SDS((T, D), _MXU), SDS((1, D), f32), SDS((1, D), f32)],
        compiler_params=_cp(("arbitrary",)), name=name,
    )(dy, xhat, rstd, g)


def _swiglu(g, u):
    return jax.nn.silu(g) * u


def _ffn_up(name, xb, wgu):
    T, D = xb.shape
    FS = wgu.shape[2]
    F = 4 * FS
    tm = _rtile(T, 256)

    def body(x_ref, wg_ref, wu_ref, g_ref, u_ref, h_ref):
        xv = x_ref[...]
        g = jnp.dot(xv, wg_ref[...], preferred_element_type=f32)
        u = jnp.dot(xv, wu_ref[...], preferred_element_type=f32)
        g_ref[...] = g.astype(g_ref.dtype)
        u_ref[...] = u.astype(u_ref.dtype)
        h_ref[...] = _swiglu(g, u).astype(h_ref.dtype)

    out = pl.BlockSpec((tm, FS), lambda j, i: (i, j))
    return pl.pallas_call(
        body, grid=(4, T // tm),
        in_specs=[pl.BlockSpec((tm, D), lambda j, i: (i, 0)),
                  pl.BlockSpec((None, D, FS), lambda j, i: (j, 0, 0)),
                  pl.BlockSpec((None, D, FS), lambda j, i: (j + 4, 0, 0))],
        out_specs=[out, out, out],
        out_shape=[SDS((T, F), _MXU), SDS((T, F), _MXU), SDS((T, F), _MXU)],
        compiler_params=_cp(("parallel", "arbitrary")), name=name,
    )(xb, wgu, wgu)


def _ffn_bwd(pfx, dy, sv, wgu, wd, g_ln, alpha):
    T, D = dy.shape
    FS = wgu.shape[2]
    F = 4 * FS
    dz, dfb, dg, db = _ln_bwd(pfx + "_lnb", dy, sv["xhat"], sv["rstd"], g_ln, 0.5)

    tm, tn, tk = _rtile(T, 1024), _tile(F, 512), _tile(D, 2048)

    def epi(acc, ex, outs):
        g_ref, u_ref = ex
        _, vjp = jax.vjp(_swiglu, g_ref[...].astype(f32), u_ref[...].astype(f32))
        dgate, dup = vjp(acc)
        outs[0][0] = dgate.astype(outs[0].dtype)
        outs[0][1] = dup.astype(outs[0].dtype)

    gu = pl.BlockSpec((tm, tn), lambda i, j, k: (i, j))
    (dgu,) = _mm(
        pfx + "_dh", dfb, wd, _NT, (T // tm, F // tn, D // tk),
        pl.BlockSpec((tm, tk), lambda i, j, k: (i, k)), pl.BlockSpec((tn, tk), lambda i, j, k: (j, k)), (tm, tn),
        [sv["gate"], sv["up"]], [gu, gu],
        [SDS((2, T, F), _MXU)], [pl.BlockSpec((2, tm, tn), lambda i, j, k: (0, i, j))], epi,
    )

    tm2, tk2 = _tile(F, 512), _rtile(T, 512)
    (dwd,) = _mm(
        pfx + "_dwd", sv["h"], dfb, _TN, (F // tm2, 1, T // tk2),
        pl.BlockSpec((tk2, tm2), lambda i, j, k: (k, i)), pl.BlockSpec((tk2, D), lambda i, j, k: (k, 0)), (tm2, D),
        [], [], [SDS((F, D), _GDT)], [pl.BlockSpec((tm2, D), lambda i, j, k: (i, 0))], _store_epi,
    )

    tn3 = _tile(D, 1024)

    def epi3(acc, ex, outs):
        outs[0][...] = alpha * ex[0][...] + acc

    (dx,) = _mm(
        pfx + "_dx", dgu, wgu, _NT, (T // tm, D // tn3, 8),
        pl.BlockSpec((None, tm, FS), lambda i, j, k: (k // 4, i, k % 4)),
        pl.BlockSpec((None, tn3, FS), lambda i, j, k: (k, j, 0)), (tm, tn3),
        [dz], [pl.BlockSpec((tm, tn3), lambda i, j, k: (i, j))],
        [SDS((T, D), f32)], [pl.BlockSpec((tm, tn3), lambda i, j, k: (i, j))], epi3,
    )

    tm4, tk4 = _rtile(D, 512), _tile(T, 2048)
    (dwgu,) = _mm(
        pfx + "_dwgu", jnp.transpose(sv["xb"]), dgu, _NN, (D // tm4, 8, T // tk4),
        pl.BlockSpec((tm4, tk4), lambda i, j, k: (i, k)),
        pl.BlockSpec((None, tk4, FS), lambda i, j, k: (j // 4, k, j % 4)), (tm4, FS),
        [], [], [SDS((8, D, FS), _GDT)], [pl.BlockSpec((None, tm4, FS), lambda i, j, k: (j, i, 0))], _store_epi,
    )
    return dx, dwgu, dwd, dg, db


def _zoh(a_re, a_im, log_dt, b_re_t, b_im_t):
    dt = jnp.exp(log_dt)
    mag = jnp.exp(a_re * dt)
    lr_, li_ = mag * jnp.cos(a_im * dt), mag * jnp.sin(a_im * dt)
    den = a_re * a_re + a_im * a_im
    pr, pi = lr_ - 1.0, li_
    qr, qi = a_re / den, -a_im / den
    zr, zi = pr * qr - pi * qi, pr * qi + pi * qr
    bbr = zr[None] * b_re_t - zi[None] * b_im_t
    bbi = zr[None] * b_im_t + zi[None] * b_re_t
    return lr_, li_, bbr, bbi


def _zoh_fwd(name, a_re, a_im, log_dt, b_re_t, b_im_t):
    G, P = a_re.shape
    H = b_re_t.shape[0]

    def body(ar, ai, ld, br, bi, o1, o2, o3, o4):
        r = _zoh(ar[...], ai[...], ld[...], br[...], bi[...])
        o1[...], o2[...], o3[...], o4[...] = r

    return pl.pallas_call(
        body, out_shape=[SDS((G, P), f32), SDS((G, P), f32), SDS((H, G, P), f32), SDS((H, G, P), f32)], name=name,
    )(a_re, a_im, log_dt, b_re_t, b_im_t)


def _zoh_bwd(name, a_re, a_im, log_dt, b_re_t, b_im_t, dlr, dli, dbbr, dbbi):
    G, P = a_re.shape
    H = b_re_t.shape[0]

    def body(ar, ai, ld, br, bi, g1, g2, g3, g4, o1, o2, o3, o4, o5):
        _, vjp = jax.vjp(_zoh, ar[...], ai[...], ld[...], br[...], bi[...])
        r = vjp((g1[...], g2[...], g3[...], g4[...]))
        o1[...], o2[...], o3[...], o4[...], o5[...] = r

    return pl.pallas_call(
        body, out_shape=[SDS((G, P), f32), SDS((G, P), f32), SDS((G, 1), f32), SDS((H, G, P), f32), SDS((H, G, P), f32)],
        name=name,
    )(a_re, a_im, log_dt, b_re_t, b_im_t, dlr, dli, dbbr, dbbi)


def _blockdiag(m):
    G, A, B = m.shape
    eye = jnp.eye(8, dtype=bool)
    m4 = m.reshape(G // 8, 8, A, B)
    out = jnp.where(eye[None, :, None, :, None], m4[:, :, :, None, :], jnp.zeros((), m.dtype))
    return out.reshape(G // 8, 8 * A, 8 * B)


def _blockdiag_extract(mb, A, B):
    J = mb.shape[0]
    m5 = mb.reshape(J, 8, A, 8, B)
    d = jnp.stack([m5[:, i, :, i, :] for i in range(8)], axis=1)
    return d.reshape(J * 8, A, B)


def _bd2(name, a, a_col0, b1, b2, out_dtype=f32):
    T = a.shape[0]
    J, KA, NB = b1.shape
    tm = _rtile(T, 512)

    def body(a_ref, b1_ref, b2_ref, o1, o2):
        av = a_ref[...].astype(_MXU)
        o1[...] = jnp.dot(av, b1_ref[...].astype(_MXU), preferred_element_type=f32).astype(o1.dtype)
        o2[...] = jnp.dot(av, b2_ref[...].astype(_MXU), preferred_element_type=f32).astype(o2.dtype)

    bs = pl.BlockSpec((None, KA, NB), lambda i, j: (j, 0, 0))
    os_ = pl.BlockSpec((tm, NB), lambda i, j: (i, j))
    return pl.pallas_call(
        body, grid=(T // tm, J), in_specs=[pl.BlockSpec((tm, KA), lambda i, j: (i, j + a_col0)), bs, bs],
        out_specs=[os_, os_], out_shape=[SDS((T, J * NB), out_dtype)] * 2,
        compiler_params=_cp(("parallel", "parallel")), name=name,
    )(a, b1, b2)


def _bd_sum(name, a1, a2, b1, b2, extras, extra_specs_fn, out_shape, epilogue):
    T = a1.shape[0]
    J, KA, NB = b1.shape
    tm = _rtile(T, 512)
    ne = len(extras)

    def body(*refs):
        a1_ref, a2_ref, b1_ref, b2_ref = refs[:4]
        ex = refs[4:4 + ne]
        outs = refs[4 + ne:]
        acc = jnp.dot(a1_ref[...].astype(_MXU), b1_ref[...].astype(_MXU), preferred_element_type=f32)
        acc = acc + jnp.dot(a2_ref[...].astype(_MXU), b2_ref[...].astype(_MXU), preferred_element_type=f32)
        epilogue(acc, ex, outs)

    as_ = pl.BlockSpec((tm, KA), lambda i, j: (i, j))
    bs = pl.BlockSpec((None, KA, NB), lambda i, j: (j, 0, 0))
    os_ = pl.BlockSpec((tm, NB), lambda i, j: (i, j))
    return pl.pallas_call(
        body, grid=(T // tm, J), in_specs=[as_, as_, bs, bs, *extra_specs_fn(tm, NB)],
        out_specs=[os_] * len(out_shape), out_shape=list(out_shape),
        compiler_params=_cp(("parallel", "parallel")), name=name,
    )(a1, a2, b1, b2, *extras)


def _bdT2(name, a1, a2, a_col0, b1, b2, b_col0, KA, NB, J):
    T = a1.shape[0]
    tk = _rtile(T, 512)

    def body(a1_ref, a2_ref, b1_ref, b2_ref, o1, o2):
        k = pl.program_id(1)
        p1 = lax.dot_general(a1_ref[...].astype(_MXU), b1_ref[...].astype(_MXU), _TN, preferred_element_type=f32)
        p2 = lax.dot_general(a2_ref[...].astype(_MXU), b2_ref[...].astype(_MXU), _TN, preferred_element_type=f32)

        @pl.when(k == 0)
        def _():
            o1[...] = p1
            o2[...] = p2

        @pl.when(k > 0)
        def _():
            o1[...] += p1
            o2[...] += p2

    as_ = pl.BlockSpec((tk, KA), lambda j, k: (k, j + a_col0))
    bs = pl.BlockSpec((tk, NB), lambda j, k: (k, j + b_col0))
    os_ = pl.BlockSpec((None, KA, NB), lambda j, k: (j, 0, 0))
    return pl.pallas_call(
        body, grid=(J, T // tk), in_specs=[as_, as_, bs, bs], out_specs=[os_, os_],
        out_shape=[SDS((J, KA, NB), f32)] * 2, compiler_params=_cp(("parallel", "arbitrary")), name=name,
    )(a1, a2, b1, b2)


_RB = 8


def _cmul(ar, ai, br, bi):
    return ar * br - ai * bi, ar * bi + ai * br


def _lam_powers(lr_v, li_v, cb):
    pw = {1: (lr_v, li_v)}
    for k in range(2, _RB + 1):
        pw[k] = _cmul(*pw[k - 1], lr_v, li_v)
    return pw


def _row_powers(pw, row, cb, reverse):
    outr = jnp.zeros((_RB, cb), f32)
    outi = jnp.zeros((_RB, cb), f32)
    for r in range(_RB):
        k = _RB - r if reverse else r + 1
        outr = jnp.where(row == r, pw[k][0], outr)
        outi = jnp.where(row == r, pw[k][1], outi)
    return outr, outi


def _tile_scan(xr, xi, pw, row, reverse):
    for k in (1, 2, 4):
        if reverse:
            keep = row < _RB - k
            shr, shi = pltpu.roll(xr, _RB - k, 0), pltpu.roll(xi, _RB - k, 0)
        else:
            keep = row >= k
            shr, shi = pltpu.roll(xr, k, 0), pltpu.roll(xi, k, 0)
        shr, shi = jnp.where(keep, shr, 0.0), jnp.where(keep, shi, 0.0)
        mr, mi = pw[k]
        xr, xi = xr + (mr * shr - mi * shi), xi + (mr * shi + mi * shr)
    return xr, xi


def _s5_scan(name, bur, bui, lr_, li_):
    T, N = bur.shape
    cb = _tile(N, 512)

    def body(br_ref, bi_ref, lr_ref, li_ref, sr_ref, si_ref):
        pw = _lam_powers(lr_ref[...], li_ref[...], cb)
        row = lax.broadcasted_iota(jnp.int32, (_RB, cb), 0)
        cr, ci = _row_powers(pw, row, cb, False)

        def step(n, carry):
            pr, pi = carry
            t0 = pl.multiple_of(n * _RB, _RB)
            xr, xi = _tile_scan(br_ref[pl.ds(t0, _RB), :], bi_ref[pl.ds(t0, _RB), :], pw, row, False)
            xr, xi = xr + (cr * pr - ci * pi), xi + (cr * pi + ci * pr)
            sr_ref[pl.ds(t0, _RB), :] = xr
            si_ref[pl.ds(t0, _RB), :] = xi
            return xr[_RB - 1:_RB, :], xi[_RB - 1:_RB, :]

        z = jnp.zeros((1, cb), f32)
        lax.fori_loop(0, T // _RB, step, (z, z))

    col = pl.BlockSpec((T, cb), lambda j: (0, j))
    vec = pl.BlockSpec((1, cb), lambda j: (0, j))
    return pl.pallas_call(
        body, grid=(N // cb,), in_specs=[col, col, vec, vec], out_specs=[col, col],
        out_shape=[SDS((T, N), f32)] * 2, compiler_params=_cp(("parallel",)), name=name,
    )(bur, bui, lr_, li_)


def _s5_scan_bwd(name, dsr, dsi, sr, si, lr_, li_):
    T, N = dsr.shape
    cb = _tile(N, 256)

    def body(dr_ref, di_ref, sr_ref, si_ref, lr_ref, li_ref, ar_ref, ai_ref, glr_ref, gli_ref):
        pw = _lam_powers(lr_ref[...], -li_ref[...], cb)
        row = lax.broadcasted_iota(jnp.int32, (_RB, cb), 0)
        cr, ci = _row_powers(pw, row, cb, True)
        NT = T // _RB

        def tile(t0, nxt, prev_last):
            xr, xi = _tile_scan(dr_ref[pl.ds(t0, _RB), :], di_ref[pl.ds(t0, _RB), :], pw, row, True)
            xr, xi = xr + (cr * nxt[0] - ci * nxt[1]), xi + (cr * nxt[1] + ci * nxt[0])
            ar_ref[pl.ds(t0, _RB), :] = xr
            ai_ref[pl.ds(t0, _RB), :] = xi
            pr = jnp.where(row == 0, prev_last[0], pltpu.roll(sr_ref[pl.ds(t0, _RB), :], 1, 0))
            pi = jnp.where(row == 0, prev_last[1], pltpu.roll(si_ref[pl.ds(t0, _RB), :], 1, 0))
            return xr, xi, xr * pr + xi * pi, xi * pr - xr * pi

        def step(n, carry):
            nr, ni, glr, gli = carry
            t0 = pl.multiple_of((NT - 1 - n) * _RB, _RB)
            tp = pl.multiple_of((NT - 2 - n) * _RB, _RB)
            prev_last = (sr_ref[pl.ds(tp, _RB), :][_RB - 1:_RB, :], si_ref[pl.ds(tp, _RB), :][_RB - 1:_RB, :])
            xr, xi, gr, gi = tile(t0, (nr, ni), prev_last)
            return xr[0:1, :], xi[0:1, :], glr + gr, gli + gi

        z1 = jnp.zeros((1, cb), f32)
        z8 = jnp.zeros((_RB, cb), f32)
        nr, ni, glr, gli = lax.fori_loop(0, NT - 1, step, (z1, z1, z8, z8))
        _, _, gr, gi = tile(0, (nr, ni), (z1, z1))
        glr_ref[...] = jnp.sum(glr + gr, axis=0, keepdims=True)
        gli_ref[...] = jnp.sum(gli + gi, axis=0, keepdims=True)

    col = pl.BlockSpec((T, cb), lambda j: (0, j))
    vec = pl.BlockSpec((1, cb), lambda j: (0, j))
    return pl.pallas_call(
        body, grid=(N // cb,), in_specs=[col, col, col, col, vec, vec], out_specs=[col, col, vec, vec],
        out_shape=[SDS((T, N), f32), SDS((T, N), f32), SDS((1, N), f32), SDS((1, N), f32)],
        compiler_params=_cp(("parallel",)), name=name,
    )(dsr, dsi, sr, si, lr_, li_)


def _conv_fwd(name, p, col0, w, GW3):
    T = p.shape[0]
    K = w.shape[0]
    cb = 128
    c0 = col0 // cb

    def body(x_ref, w_ref, o_ref, pad_ref):
        pad_ref[pl.ds(0, 8), :] = jnp.zeros((8, cb), f32)
        pad_ref[pl.ds(8, T), :] = x_ref[...]
        wv = w_ref[...]
        acc = jnp.zeros((T, cb), f32)
        for j in range(K):
            acc = acc + wv[j:j + 1, :] * pad_ref[pl.ds(8 - (K - 1) + j, T), :]
        o_ref[...] = jax.nn.silu(acc)

    return pl.pallas_call(
        body, grid=(GW3 // cb,),
        in_specs=[pl.BlockSpec((T, cb), lambda j: (0, j + c0)), pl.BlockSpec((K, cb), lambda j: (0, j))],
        out_specs=pl.BlockSpec((T, cb), lambda j: (0, j)), out_shape=SDS((T, GW3), f32),
        scratch_shapes=[pltpu.VMEM((T + 8, cb), f32)], compiler_params=_cp(("parallel",)), name=name,
    )(p, w)


def _conv_bwd(name, p, col0, w, dout3):
    T = p.shape[0]
    K = w.shape[0]
    GW = dout3.shape[2]
    GW3 = 3 * GW
    cb = 128
    c0 = col0 // cb
    nb = GW // cb

    def body(x_ref, w_ref, d_ref, dx_ref, dw_ref, pad_ref, dpad_ref):
        pad_ref[pl.ds(0, 8), :] = jnp.zeros((8, cb), f32)
        pad_ref[pl.ds(8, T), :] = x_ref[...]
        wv = w_ref[...]
        pre = jnp.zeros((T, cb), f32)
        for j in range(K):
            pre = pre + wv[j:j + 1, :] * pad_ref[pl.ds(8 - (K - 1) + j, T), :]
        _, vjp = jax.vjp(jax.nn.silu, pre)
        (dpre,) = vjp(d_ref[...])
        dpad_ref[pl.ds(0, T), :] = dpre
        dpad_ref[pl.ds(T, 8), :] = jnp.zeros((8, cb), f32)
        dx = jnp.zeros((T, cb), f32)
        rows = []
        for j in range(K):
            dx = dx + wv[j:j + 1, :] * dpad_ref[pl.ds((K - 1) - j, T), :]
            rows.append(jnp.sum(dpre * pad_ref[pl.ds(8 - (K - 1) + j, T), :], axis=0, keepdims=True))
        dx_ref[...] = dx.astype(dx_ref.dtype)
        for j in range(K):
            dw_ref[pl.ds(j, 1), :] = rows[j]

    return pl.pallas_call(
        body, grid=(GW3 // cb,),
        in_specs=[pl.BlockSpec((T, cb), lambda j: (0, j + c0)), pl.BlockSpec((K, cb), lambda j: (0, j)),
                  pl.BlockSpec((None, T, cb), lambda j: (j // nb, 0, j % nb))],
        out_specs=[pl.BlockSpec((T, cb), lambda j: (0, j)), pl.BlockSpec((K, cb), lambda j: (0, j))],
        out_shape=[SDS((T, GW3), _MXU), SDS((K, GW3), f32)],
        scratch_shapes=[pltpu.VMEM((T + 8, cb), f32), pltpu.VMEM((T + 8, cb), f32)],
        compiler_params=_cp(("parallel",)), name=name,
    )(p, w, dout3)


def _hdot(a, b, dims=_NN):
    return lax.dot_general(a, b, dims, precision=_HP, preferred_element_type=f32)


def _split(a):
    hi = a.astype(jnp.bfloat16)
    lo = (a - hi.astype(f32)).astype(jnp.bfloat16)
    return hi, lo


_BNN = (((2,), (1,)), ((0,), (0,)))
_BNT = (((2,), (2,)), ((0,), (0,)))
_BTN = (((1,), (1,)), ((0,), (0,)))


def _dot3_raw(a, b, dims):
    ah, al = _split(a)
    bh, bl = _split(b)
    d = functools.partial(lax.dot_general, dimension_numbers=dims, preferred_element_type=f32)
    return d(ah, bh) + (d(al, bh) + d(ah, bl))


@jax.custom_vjp
def _dot3(a, b):
    return _dot3_raw(a, b, _BNN)


def _dot3_fwd(a, b):
    return _dot3_raw(a, b, _BNN), (a, b)


def _dot3_bwd(res, g):
    a, b = res
    return _dot3_raw(g, b, _BNT), _dot3_raw(a, g, _BTN)


_dot3.defvjp(_dot3_fwd, _dot3_bwd)


def _ldot(a, b, dims=_BNN):
    return lax.dot_general(a.astype(_MXU), b.astype(_MXU), dims, preferred_element_type=f32)


def _sdot(a, b):
    return _ldot(a, b)


def _gdn_chunk(S, q, k, v, z, bl, ain, alog, dtb, nw):
    H, C, d = q.shape
    ri = lax.broadcasted_iota(jnp.int32, (H, C, C), 1)
    ci = lax.broadcasted_iota(jnp.int32, (H, C, C), 2)
    causal = ri >= ci
    strict = ri > ci
    tri = causal.astype(f32)
    qn = q * lax.rsqrt(jnp.sum(q * q, axis=-1, keepdims=True) + L2_EPS) * (d ** -0.5)
    kn = k * lax.rsqrt(jnp.sum(k * k, axis=-1, keepdims=True) + L2_EPS)
    beta = jax.nn.sigmoid(bl)
    g = -jnp.exp(alog) * jax.nn.softplus(ain + dtb)
    gb = jnp.broadcast_to(g, (H, C, C))
    gc_col = _dot3(tri, gb)
    gc_row = _dot3(jnp.ones((H, C, C), f32), jnp.where(ri <= ci, gb, 0.0))
    diff = jnp.where(causal, gc_col - gc_row, 0.0)
    decay = jnp.where(causal, jnp.exp(diff), 0.0)
    gcum = gc_col[:, :, 0:1]
    glast = gc_col[:, C - 1:C, 0:1]
    egc = jnp.exp(gcum)
    kb = kn * beta
    lower = jnp.where(strict, _ldot(kb, kn, _BNT) * decay, 0.0)
    x = jnp.concatenate([v * beta, kb * egc], axis=-1)
    m = -lower
    for it in range(6):
        x = x + _sdot(m, x)
        if it < 5:
            m = _sdot(m, m)
    u_val, w_key = x[:, :, :d], x[:, :, d:]
    attn = _ldot(qn, kn, _BNT) * decay
    q_dec = qn * egc
    k_dec = kn * jnp.exp(glast - gcum)
    v_new = u_val - _ldot(w_key, S)
    out = _ldot(q_dec, S) + _ldot(attn, v_new)
    s_new = S * jnp.exp(glast) + _ldot(k_dec, v_new, _BTN)
    o = out * lax.rsqrt(jnp.mean(out * out, axis=-1, keepdims=True) + RMS_EPS) * nw
    o = o * jax.nn.silu(z)
    return s_new, o


def _heads_per_step(NH, HD, zcol0):
    for hb in (8, 4, 2):
        if NH % hb == 0 and zcol0 % (hb * HD) == 0:
            return hb
    return 1


def _gdn_fwd(name, qkv, p, zcol0, blt, aint, alog, dtb, nw, NH, HD):
    T = qkv.shape[0]
    N = T // CHUNK
    GW = NH * HD
    HB = _heads_per_step(NH, HD, zcol0)
    W = HB * HD
    zc0 = zcol0 // W
    nb = GW // W

    def body(q_ref, k_ref, v_ref, z_ref, bl_ref, ain_ref, al_ref, dtb_ref, nw_ref, o_ref, ssave_ref, s_scr):
        n = pl.program_id(1)

        @pl.when(n == 0)
        def _():
            s_scr[...] = jnp.zeros_like(s_scr)

        heads = lambda r: jnp.stack([r[:, hh * HD:(hh + 1) * HD] for hh in range(HB)], axis=0)
        s_in = s_scr[...]
        ssave_ref[...] = s_in
        s_new, o = _gdn_chunk(s_in, heads(q_ref), heads(k_ref), heads(v_ref), heads(z_ref), bl_ref[...], ain_ref[...],
                              al_ref[...], dtb_ref[...], nw_ref[...])
        s_scr[...] = s_new
        for hh in range(HB):
            o_ref[:, hh * HD:(hh + 1) * HD] = o[hh].astype(o_ref.dtype)

    ch = lambda off: pl.BlockSpec((CHUNK, W), lambda h, n: (n, h + off))
    sc = pl.BlockSpec((HB, CHUNK, 1), lambda h, n: (h, n, 0))
    hs = pl.BlockSpec((HB, 1, 1), lambda h, n: (h, 0, 0))
    return pl.pallas_call(
        body, grid=(NH // HB, N),
        in_specs=[ch(0), ch(nb), ch(2 * nb), ch(zc0), sc, sc, hs, hs, pl.BlockSpec((1, HD), lambda h, n: (0, 0))],
        out_specs=[pl.BlockSpec((CHUNK, W), lambda h, n: (n, h)),
                   pl.BlockSpec((HB, None, HD, HD), lambda h, n: (h, n, 0, 0))],
        out_shape=[SDS((T, GW), _MXU), SDS((NH, N, HD, HD), f32)],
        scratch_shapes=[pltpu.VMEM((HB, HD, HD), f32)], compiler_params=_cp(("parallel", "arbitrary")), name=name,
    )(qkv, qkv, qkv, p, blt, aint, alog, dtb, nw)


def _gdn_bwd(name, qkv, p, zcol0, blt, aint, alog, dtb, nw, ssave, do, NH, HD):
    T = qkv.shape[0]
    N = T // CHUNK
    GW = NH * HD
    HB = _heads_per_step(NH, HD, zcol0)
    W = HB * HD
    zc0 = zcol0 // W
    nb = GW // W

    def body(q_ref, k_ref, v_ref, z_ref, bl_ref, ain_ref, al_ref, dtb_ref, nw_ref, ss_ref, do_ref,
             dqkv_ref, dz_ref, dbl_ref, dain_ref, dal_ref, ddtb_ref, dnw_ref, ds_scr):
        h = pl.program_id(0)
        n = pl.program_id(1)

        @pl.when(n == 0)
        def _():
            ds_scr[...] = jnp.zeros_like(ds_scr)

        heads = lambda r: jnp.stack([r[:, hh * HD:(hh + 1) * HD] for hh in range(HB)], axis=0)
        _, vjp = jax.vjp(_gdn_chunk, ss_ref[...], heads(q_ref), heads(k_ref), heads(v_ref), heads(z_ref), bl_ref[...],
                         ain_ref[...], al_ref[...], dtb_ref[...], nw_ref[...])
        ds, dq, dk, dv, dz, dbl, dain, dal, ddtb, dnw = vjp((ds_scr[...], heads(do_ref).astype(f32)))
        ds_scr[...] = ds
        for hh in range(HB):
            cs = slice(hh * HD, (hh + 1) * HD)
            dqkv_ref[0, :, cs] = dq[hh]
            dqkv_ref[1, :, cs] = dk[hh]
            dqkv_ref[2, :, cs] = dv[hh]
            dz_ref[:, cs] = dz[hh].astype(dz_ref.dtype)
        dbl_ref[...] = dbl
        dain_ref[...] = dain

        @pl.when(n == 0)
        def _():
            dal_ref[...] = dal
            ddtb_ref[...] = ddtb

        @pl.when(n > 0)
        def _():
            dal_ref[...] += dal
            ddtb_ref[...] += ddtb

        @pl.when((n == 0) & (h == 0))
        def _():
            dnw_ref[...] = dnw

        @pl.when((n > 0) | (h > 0))
        def _():
            dnw_ref[...] += dnw

    R = N - 1
    ch = lambda off: pl.BlockSpec((CHUNK, W), lambda h, n: (R - n, h + off))
    sc = pl.BlockSpec((HB, CHUNK, 1), lambda h, n: (h, R - n, 0))
    hs = pl.BlockSpec((HB, 1, 1), lambda h, n: (h, 0, 0))
    nws = pl.BlockSpec((1, HD), lambda h, n: (0, 0))
    return pl.pallas_call(
        body, grid=(NH // HB, N),
        in_specs=[ch(0), ch(nb), ch(2 * nb), ch(zc0), sc, sc, hs, hs, nws,
                  pl.BlockSpec((HB, None, HD, HD), lambda h, n: (h, R - n, 0, 0)),
                  pl.BlockSpec((CHUNK, W), lambda h, n: (R - n, h))],
        out_specs=[pl.BlockSpec((3, CHUNK, W), lambda h, n: (0, R - n, h)),
                   pl.BlockSpec((CHUNK, W), lambda h, n: (R - n, h)), sc, sc, hs, hs, nws],
        out_shape=[SDS((3, T, GW), f32), SDS((T, GW), _MXU), SDS((NH, T, 1), f32), SDS((NH, T, 1), f32),
                   SDS((NH, 1, 1), f32), SDS((NH, 1, 1), f32), SDS((1, HD), f32)],
        scratch_shapes=[pltpu.VMEM((HB, HD, HD), f32)], compiler_params=_cp(("arbitrary", "arbitrary")), name=name,
    )(qkv, qkv, qkv, p, blt, aint, alog, dtb, nw, ssave, do)


def _loss_head(name, y, tgt):
    T, D = y.shape
    tm = _rtile(T, 256)

    def body(y_ref, t_ref, dy_ref, l_ref):
        i = pl.program_id(0)
        err = y_ref[...] - t_ref[...]
        dy_ref[...] = err * (1.0 / D)
        part = 0.5 * jnp.sum(jnp.sum(err * err, axis=-1, keepdims=True) * (1.0 / D), axis=0, keepdims=True)

        @pl.when(i == 0)
        def _():
            l_ref[...] = part

        @pl.when(i > 0)
        def _():
            l_ref[...] += part

    row = pl.BlockSpec((tm, D), lambda i: (i, 0))
    return pl.pallas_call(
        body, grid=(T // tm,), in_specs=[row, row], out_specs=[row, pl.BlockSpec((1, 1), lambda i: (0, 0))],
        out_shape=[SDS((T, D), f32), SDS((1, 1), f32)], compiler_params=_cp(("arbitrary",)), name=name,
    )(y, tgt)


def _adam_math(w, g, m, v):
    m = ADAM_B1 * m + (1.0 - ADAM_B1) * g
    v = ADAM_B2 * v + (1.0 - ADAM_B2) * jnp.square(g)
    m_hat = m / (1.0 - ADAM_B1 ** ADAM_STEP)
    v_hat = v / (1.0 - ADAM_B2 ** ADAM_STEP)
    delta = -ADAM_LR * (m_hat / (jnp.sqrt(v_hat) + ADAM_EPS) + ADAM_WD * w)
    return delta, m, v


def _add_mine(name, full, recv, me, out_dtype):
    N, _, R, C = full.shape
    tr = _rtile(R, max(16, (1 << 19) // max(C, 1) // 16 * 16))

    def body(me_ref, a_ref, b_ref, o_ref):
        o_ref[...] = (a_ref[...].astype(f32) + b_ref[...].astype(f32)).astype(o_ref.dtype)

    blk = pl.BlockSpec((None, tr, C), lambda n, i, me_ref: (n, i, 0))
    return pl.pallas_call(
        body,
        grid_spec=pltpu.PrefetchScalarGridSpec(
            num_scalar_prefetch=1, grid=(N, R // tr),
            in_specs=[pl.BlockSpec((None, None, tr, C), lambda n, i, me_ref: (n, me_ref[0], i, 0)), blk], out_specs=blk),
        out_shape=SDS((N, R, C), out_dtype), compiler_params=_cp(("parallel", "parallel")), name=name,
    )(me, full, recv)


def _adamw_big(name, full, recv, me, w, m, v, l, accs):
    _, R, C = full.shape
    L = w.shape[0]
    tr = _rtile(R, max(16, (1 << 18) // max(C, 1) // 16 * 16))

    def body(me_ref, ga_ref, gb_ref, w_ref, m_ref, v_ref, a0, a1, a2, a3, g_ref, d_ref, nm_ref, nv_ref):
        g = ga_ref[...].astype(f32) + gb_ref[...].astype(f32)
        d, nm, nv = _adam_math(w_ref[...], g, m_ref[...], v_ref[...])
        g_ref[...] = g
        d_ref[...] = d
        nm_ref[...] = nm
        nv_ref[...] = nv

    blk = pl.BlockSpec((tr, C), lambda i, me_ref: (i, 0))
    lblk = pl.BlockSpec((None, tr, C), lambda i, me_ref: (l, i, 0))
    untouched = pl.BlockSpec(memory_space=pl.ANY)
    return pl.pallas_call(
        body,
        grid_spec=pltpu.PrefetchScalarGridSpec(
            num_scalar_prefetch=1, grid=(R // tr,),
            in_specs=[pl.BlockSpec((None, tr, C), lambda i, me_ref: (me_ref[0], i, 0)), blk, lblk, lblk, lblk] + [untouched] * 4,
            out_specs=[lblk] * 4),
        out_shape=[SDS((L, R, C), f32)] * 4, input_output_aliases={6: 0, 7: 1, 8: 2, 9: 3},
        compiler_params=_cp(("parallel",)), name=name,
    )(me, full, recv, w, m, v, *accs)


def _adamw_small(name, gall, w, m, v):
    _, R, C = gall.shape
    tr = _rtile(R, 512)

    def body(ga_ref, w_ref, m_ref, v_ref, g_ref, d_ref, nm_ref, nv_ref):
        g = ga_ref[0]
        for s in range(1, 8):
            g = g + ga_ref[s]
        d, nm, nv = _adam_math(w_ref[...], g, m_ref[...], v_ref[...])
        g_ref[...] = g
        d_ref[...] = d
        nm_ref[...] = nm
        nv_ref[...] = nv

    blk = pl.BlockSpec((tr, C), lambda i: (i, 0))
    return pl.pallas_call(
        body, grid=(R // tr,), in_specs=[pl.BlockSpec((8, tr, C), lambda i: (0, i, 0)), blk, blk, blk], out_specs=[blk] * 4,
        out_shape=[SDS((R, C), f32)] * 4, compiler_params=_cp(("parallel",)), name=name,
    )(gall, w, m, v)


def _peer(axis):
    x, y, c = lax.axis_index("x"), lax.axis_index("y"), lax.axis_index("c")
    me = {"x": x, "y": y, "c": c}[axis]
    peer = {"x": (1 - x, y, c), "y": (x, 1 - y, c), "c": (x, y, 1 - c)}[axis]
    return me, peer


def _held(ref, done):
    idx = tuple(slice(None) if a in done else lax.axis_index(a) for a in ("x", "y", "c"))
    return ref.at[idx]


def _gather_stage(name, bufs, axes, dones):
    n = len(bufs)
    hbm = pl.BlockSpec(memory_space=pltpu.HBM)

    def body(*refs):
        outs = refs[n:2 * n]
        send_sems, recv_sems = refs[2 * n:]
        cps = []
        for t in range(n):
            _, peer = _peer(axes[t])
            blk = _held(outs[t], dones[t])
            cps.append(pltpu.make_async_remote_copy(src_ref=blk, dst_ref=blk, send_sem=send_sems.at[t],
                                                    recv_sem=recv_sems.at[t], device_id=peer, device_id_type=_MESH_T))
        for cp in cps:
            cp.start()
        for cp in cps:
            cp.wait()

    return pl.pallas_call(
        body, in_specs=[hbm] * n, out_specs=[hbm] * n, out_shape=[SDS(b.shape, b.dtype) for b in bufs],
        input_output_aliases={t: t for t in range(n)},
        scratch_shapes=[pltpu.SemaphoreType.DMA((n,)), pltpu.SemaphoreType.DMA((n,))], name=name,
    )(*bufs)


_HBM = pl.BlockSpec(memory_space=pltpu.HBM)
_SEM = pl.BlockSpec(memory_space=pltpu.SEMAPHORE)
_EFFECT = pltpu.SideEffectType.DATAFLOW_SIDE_EFFECTING


def _split_start(name, arrays, n_copies, make_copies):
    na = len(arrays)

    def body(*refs):
        ins = refs[:na]
        send_sems, recv_sems = refs[na], refs[na + 1]
        token = refs[2 * na + 2]
        for cp in make_copies(ins, send_sems, recv_sems):
            cp.start()
        token[...] = jnp.zeros_like(token)

    res = pl.pallas_call(
        body, name=name,
        out_shape=(pltpu.SemaphoreType.DMA((n_copies,)), pltpu.SemaphoreType.DMA((n_copies,)),
                   *[pltpu.HBM(a.shape, a.dtype) for a in arrays], SDS((8, 128), f32)),
        in_specs=[_HBM] * na, out_specs=(_SEM, _SEM, *[_HBM] * na, pl.BlockSpec(memory_space=pltpu.VMEM)),
        input_output_aliases={i: 2 + i for i in range(na)},
        compiler_params=pltpu.CompilerParams(has_side_effects=_EFFECT),
    )(*[pltpu.with_memory_space_constraint(a, pltpu.HBM) for a in arrays])
    return res[0], res[1], list(res[2:2 + na]), res[2 + na]


def _split_wait(name, arrays, send_sems, recv_sems, after, make_copies):
    na = len(arrays)
    afters = list(after) if isinstance(after, (list, tuple)) else [after]

    def body(*refs):
        ins = refs[:na]
        for cp in make_copies(ins, refs[na], refs[na + 1]):
            cp.wait_send()
            cp.wait_recv()

    res = pl.pallas_call(
        body, name=name, out_shape=tuple(pltpu.HBM(a.shape, a.dtype) for a in arrays),
        in_specs=[_HBM] * na + [_SEM, _SEM] + [pl.BlockSpec(memory_space=pl.ANY)] * len(afters),
        out_specs=tuple([_HBM] * na), input_output_aliases={i: i for i in range(na)},
        compiler_params=pltpu.CompilerParams(has_side_effects=_EFFECT),
    )(*arrays, send_sems, recv_sems, *afters)
    return list(res)


def _gather_copies(axes, dones):
    def make(refs, send_sems, recv_sems):
        cps = []
        for t in range(len(axes)):
            _, peer = _peer(axes[t])
            blk = _held(refs[t], dones[t])
            cps.append(pltpu.make_async_remote_copy(src_ref=blk, dst_ref=blk, send_sem=send_sems.at[t],
                                                    recv_sem=recv_sems.at[t], device_id=peer, device_id_type=_MESH_T))
        return cps
    return make


def _scatter_copies(axes):
    n = len(axes)

    def make(refs, send_sems, recv_sems):
        cps = []
        for t in range(n):
            me, peer = _peer(axes[t])
            cps.append(pltpu.make_async_remote_copy(
                src_ref=refs[t].at[:, 1 - me], dst_ref=refs[n + t], send_sem=send_sems.at[t], recv_sem=recv_sems.at[t],
                device_id=peer, device_id_type=_MESH_T))
        return cps
    return make


class _AsyncGather:
    def __init__(self, pfx, tensors, paths):
        x, y, c = (lax.axis_index(a) for a in ("x", "y", "c"))
        self.pfx, self.shapes = pfx, [tuple(t.shape) for t in tensors]
        self.bufs = [lax.dynamic_update_slice(lax.empty((2, 2, 2) + tuple(t.shape), t.dtype), t[None, None, None],
                                              (x, y, c) + (0,) * t.ndim) for t in tensors]
        self.orders = [tuple(p) + ("c",) for p in paths]
        self.ph = 0

    def _make(self):
        return _gather_copies([o[self.ph] for o in self.orders], [o[:self.ph] for o in self.orders])

    def start(self):
        self.ss, self.rs, self.bufs, tok = _split_start(f"{self.pfx}_start{self.ph}", self.bufs, len(self.bufs), self._make())
        return tok

    def wait(self, after):
        self.bufs = _split_wait(f"{self.pfx}_wait{self.ph}", self.bufs, self.ss, self.rs, after, self._make())
        self.ph += 1

    def result(self):
        return [b.reshape((8,) + s) for b, s in zip(self.bufs, self.shapes)]


class _AsyncReduceScatter:
    def __init__(self, pfx, tensors, paths):
        self.pfx = pfx
        self.rcs = [tuple(t.shape[1:]) for t in tensors]
        self.orders = [("c",) + tuple(p) for p in paths]
        self.left = [["x", "y", "c"] for _ in tensors]
        self.cur = list(tensors)
        self.ph = 0

    def start(self):
        n = len(self.cur)
        views = []
        for i, (t, rc) in enumerate(zip(self.cur, self.rcs)):
            pos = self.left[i].index(self.orders[i][self.ph])
            nb, na = 2 ** pos, 2 ** (len(self.left[i]) - pos - 1)
            views.append(t.reshape((nb, 2, na * rc[0], rc[1])))
        lands = [lax.empty((v.shape[0],) + tuple(v.shape[2:]), v.dtype) for v in views]
        self.make = _scatter_copies([o[self.ph] for o in self.orders])
        self.ss, self.rs, arrs, tok = _split_start(f"{self.pfx}_start{self.ph}", views + lands, n, self.make)
        self.arrs = arrs
        return tok

    def wait(self, after):
        n = len(self.cur)
        arrs = _split_wait(f"{self.pfx}_wait{self.ph}", self.arrs, self.ss, self.rs, after, self.make)
        views, recvs = arrs[:n], arrs[n:]
        ph = self.ph
        if ph == 2:
            self.out = [(v[0], r[0], o[2]) for v, r, o in zip(views, recvs, self.orders)]
        else:
            self.cur = [_add_mine(f"{self.pfx}_add{ph}_{i}", v, r, _coord(o[ph]), v.dtype)
                        for i, (v, r, o) in enumerate(zip(views, recvs, self.orders))]
            for i, o in enumerate(self.orders):
                self.left[i].remove(o[ph])
        self.ph += 1


def _coord(axis):
    return lax.axis_index(axis).astype(jnp.int32).reshape(1)


def _all_gather(pfx, tensors, paths):
    x, y, c = (lax.axis_index(a) for a in ("x", "y", "c"))
    bufs = []
    for t in tensors:
        zero = (0,) * t.ndim
        bufs.append(lax.dynamic_update_slice(lax.empty((2, 2, 2) + tuple(t.shape), t.dtype), t[None, None, None],
                                             (x, y, c) + zero))
    orders = [tuple(p) + ("c",) for p in paths]
    for ph in range(3):
        bufs = _gather_stage(f"{pfx}_{ph}", bufs, [o[ph] for o in orders], [o[:ph] for o in orders])
    return [b.reshape((8,) + tuple(t.shape)) for b, t in zip(bufs, tensors)]


def _mm_nn(name, a, w, out_dtype, tn_pref=1024):
    T, K = a.shape
    N = w.shape[1]
    tm, tn, tk = _rtile(T, 512), _tile(N, tn_pref), _tile(K, 2048)
    return _mm(
        name, a, w, _NN, (T // tm, N // tn, K // tk),
        pl.BlockSpec((tm, tk), lambda i, j, k: (i, k)), pl.BlockSpec((tk, tn), lambda i, j, k: (k, j)), (tm, tn),
        [], [], [SDS((T, N), out_dtype)], [pl.BlockSpec((tm, tn), lambda i, j, k: (i, j))], _store_epi,
    )[0]


def _mm_tn(name, a, b, out_dtype):
    T, M = a.shape
    N = b.shape[1]
    tm, tn, tk = _tile(M, 512), _tile(N, 2048), _rtile(T, 512)
    return _mm(
        name, a, b, _TN, (M // tm, N // tn, T // tk),
        pl.BlockSpec((tk, tm), lambda i, j, k: (k, i)), pl.BlockSpec((tk, tn), lambda i, j, k: (k, j)), (tm, tn),
        [], [], [SDS((M, N), out_dtype)], [pl.BlockSpec((tm, tn), lambda i, j, k: (i, j))], _store_epi,
    )[0]


def _mm_tn_slots(name, a, b, out_dtype):
    T, M = a.shape
    NS = b.shape[1] // 8
    tm, tk = _tile(M, 512), _rtile(T, 512)
    return _mm(
        name, a, b, _TN, (M // tm, 8, T // tk),
        pl.BlockSpec((tk, tm), lambda i, j, k: (k, i)), pl.BlockSpec((tk, NS), lambda i, j, k: (k, j)), (tm, NS),
        [], [], [SDS((8, M, NS), out_dtype)], [pl.BlockSpec((None, tm, NS), lambda i, j, k: (j, i, 0))], _store_epi,
    )[0]


def _mm_nt_slots(name, a, w8, out_dtype):
    T = a.shape[0]
    _, M, NS = w8.shape
    tm, tn = _rtile(T, 512), _tile(M, 1024)
    return _mm(
        name, a, w8, _NT, (T // tm, M // tn, 8),
        pl.BlockSpec((tm, NS), lambda i, j, k: (i, k)), pl.BlockSpec((None, tn, NS), lambda i, j, k: (k, j, 0)), (tm, tn),
        [], [], [SDS((T, M), out_dtype)], [pl.BlockSpec((tm, tn), lambda i, j, k: (i, j))], _store_epi,
    )[0]


def _colsum_kernel(name, fn, ins, in_cols, outs_elem, n_sum, C):
    T = ins[0].shape[0]
    tm = _rtile(T, 256)
    ne = len(outs_elem)

    def body(*refs):
        i = pl.program_id(0)
        iv = [r[...] for r in refs[:len(ins)]]
        res = fn(*iv)
        for o, r in zip(refs[len(ins):len(ins) + ne], res[:ne]):
            o[...] = r.astype(o.dtype)
        sums = [jnp.sum(r, axis=0, keepdims=True) for r in res[ne:]]

        @pl.when(i == 0)
        def _():
            for o, s in zip(refs[len(ins) + ne:], sums):
                o[...] = s

        @pl.when(i > 0)
        def _():
            for o, s in zip(refs[len(ins) + ne:], sums):
                o[...] += s

    in_specs = []
    for arr, off in zip(ins, in_cols):
        if off is None:
            in_specs.append(pl.BlockSpec((1, C), lambda i: (0, 0)))
        else:
            in_specs.append(pl.BlockSpec((tm, C), lambda i, off=off: (i, off)))
    row = pl.BlockSpec((tm, C), lambda i: (i, 0))
    vec = pl.BlockSpec((1, C), lambda i: (0, 0))
    return pl.pallas_call(
        body, grid=(T // tm,), in_specs=in_specs, out_specs=[row] * ne + [vec] * n_sum,
        out_shape=[SDS((T, C), dt) for dt in outs_elem] + [SDS((1, C), f32)] * n_sum,
        compiler_params=_cp(("arbitrary",)), name=name,
    )(*ins)


def _merge(gs, gg, a_s, a_g):
    return jax.nn.sigmoid(gs) * a_s + jax.nn.sigmoid(gg) * a_g


def _glu(yg, lp):
    return yg * jax.nn.sigmoid(lp)


_BIG = ("ffn1_w_gu", "ffn1_w_down", "w_in", "conv_w", "glu_w", "w_br_ssm", "w_br_gdn", "w_out", "ffn2_w_gu", "ffn2_w_down")
_PATHS = ("yx", "yx", "xy", "xy", "yx", "yx", "yx", "yx", "xy", "xy")
_SMALL = ("ln1_g", "ln1_b", "ssm_a_re", "ssm_a_im", "ssm_log_dt", "ssm_b_re", "ssm_b_im", "ssm_c_re", "ssm_c_im", "ssm_d",
          "glu_b", "gdn_a_log", "gdn_dt_bias", "gdn_norm_w", "ln2_g", "ln2_b", "ln3_g", "ln3_b")
_ORDER = ("ffn1_w_gu", "ffn1_w_down", "ln1_g", "ln1_b", "w_in", "conv_w", "ssm_a_re", "ssm_a_im", "ssm_log_dt", "ssm_b_re",
          "ssm_b_im", "ssm_c_re", "ssm_c_im", "ssm_d", "glu_w", "glu_b", "gdn_a_log", "gdn_dt_bias", "gdn_norm_w", "w_br_ssm",
          "w_br_gdn", "w_out", "ln2_g", "ln2_b", "ffn2_w_gu", "ffn2_w_down", "ln3_g", "ln3_b")


def _step(x, tgt, W, M, V):
    T, D = x.shape[1], x.shape[2]
    L = W["ffn1_w_gu"].shape[0]
    G, P = W["ssm_a_re"].shape[1:]
    H = W["ssm_b_re"].shape[3]
    SW = G * H
    NH = W["gdn_a_log"].shape[1]
    HD = W["gdn_norm_w"].shape[1]
    GW = NH * HD
    KC = W["conv_w"].shape[1]
    DS = D // 8
    alpha = (2.0 * L) ** 0.25
    o_b = SW + 4 * GW
    o_gs = o_b + 2 * NH
    IN = o_gs + 2 * D
    NM = IN - 2 * NH
    m_qkv, m_z, m_gs, m_gg = SW, SW + 3 * GW, SW + 4 * GW, SW + 4 * GW + D
    J = G // 8

    x0 = x[0]
    tg = tgt[0]

    def vec(name, l):
        return W[name][l:l + 1]

    saves, weights = [], []
    xc, xcb = x0, x0.astype(_MXU)
    def shards(l):
        return [W["ffn1_w_gu"][l].astype(_MXU), W["ffn1_w_down"][l].astype(_MXU), W["w_in"][l].astype(_MXU), W["conv_w"][l],
                W["glu_w"][l].astype(_MXU), W["w_br_ssm"][l].astype(_MXU), W["w_br_gdn"][l].astype(_MXU),
                W["w_out"][l].astype(_MXU), W["ffn2_w_gu"][l].astype(_MXU), W["ffn2_w_down"][l].astype(_MXU)]

    def dep(a, tok):
        return a if tok is None else a + tok[0:1, 0:1].astype(a.dtype)

    gathered = _all_gather("ag", shards(0), _PATHS)
    ahead = {}
    for l in range(L):
        toks = []
        for k in ([1, 2] if l == 0 else [l + 2]):
            if k < L:
                ahead[k] = _AsyncGather("agp", shards(k), _PATHS)
                toks.append(ahead[k].start())
        nxt = ahead.get(l + 1)
        if nxt and l >= 1:
            nxt.wait(xc)
            toks.append(nxt.start())
        tok = functools.reduce(lambda a, b: a + b, toks) if toks else None
        wgu1, wd1, win8, cw8, wglu, wbs, wbg, wo, wgu2, wd2 = gathered
        wd1 = wd1.reshape(-1, D)
        wd2 = wd2.reshape(-1, D)
        wglu = wglu.reshape(SW, SW)
        wo = wo.reshape(D, D)
        win = jnp.transpose(win8, (1, 0, 2)).reshape(D, IN)
        wmain = jnp.concatenate([win[:, :o_b], win[:, o_gs:]], axis=1)
        wba = jnp.pad(win[:, o_b:o_gs], ((0, 0), (0, 128 - 2 * NH)))
        cw = jnp.transpose(cw8, (1, 0, 2)).reshape(KC, 3 * GW)
        wl = dict(wgu1=wgu1, wd1=wd1, wmain=wmain, wba=wba, cw=cw, wglu=wglu, wbs=wbs, wbg=wbg, wo=wo, wgu2=wgu2, wd2=wd2)
        weights.append(wl)
        sv = {}

        gate, up, hh = _ffn_up("ffn_up", xcb, wgu1)
        x1, x1b, xh1, r1 = _mm_ln("ffn_down_ln", hh, wd1, xc, dep(vec("ln1_g", l), tok), vec("ln1_b", l), alpha, 0.5)
        sv["f1"] = dict(xb=xcb, gate=gate, up=up, h=hh, xhat=xh1, rstd=r1)

        p = _mm_nn("mix_in", x1b, wmain, f32)
        pba = _mm_nn("mix_in_ba", x1b, wba, f32)
        b_re_t = jnp.transpose(W["ssm_b_re"][l], (2, 0, 1))
        b_im_t = jnp.transpose(W["ssm_b_im"][l], (2, 0, 1))
        zoh_in = (W["ssm_a_re"][l], W["ssm_a_im"][l], W["ssm_log_dt"][l][:, None], b_re_t, b_im_t)
        lbr, lbi, bbr_t, bbi_t = _zoh_fwd("zoh", *zoh_in)
        bblk_r = _blockdiag(jnp.transpose(bbr_t, (1, 0, 2)))
        bblk_i = _blockdiag(jnp.transpose(bbi_t, (1, 0, 2)))
        cblkT_r = _blockdiag(W["ssm_c_re"][l])
        cblkT_in = _blockdiag(-W["ssm_c_im"][l])
        lbr_f, lbi_f = lbr.reshape(1, G * P), lbi.reshape(1, G * P)
        bur, bui = _bd2("s5_bu", p, 0, bblk_r, bblk_i)
        if nxt and l == 0:
            nxt.wait(bur)
            tok = nxt.start()
        sr, si = _s5_scan("s5_scan", bur, bui, dep(lbr_f, tok), lbi_f)
        dflat = W["ssm_d"][l].reshape(1, SW)

        def out_epi(acc, ex, outs):
            y_raw = acc + ex[1][...] * ex[0][...]
            yg = jax.nn.gelu(y_raw)
            outs[0][...] = y_raw
            outs[1][...] = yg
            outs[2][...] = yg.astype(outs[2].dtype)

        y_raw, yg, ygb = _bd_sum(
            "s5_out", sr, si, jnp.transpose(cblkT_r, (0, 2, 1)), jnp.transpose(cblkT_in, (0, 2, 1)), [p, dflat],
            lambda tm, nb: [pl.BlockSpec((tm, nb), lambda i, j: (i, j)), pl.BlockSpec((1, nb), lambda i, j: (0, j))],
            [SDS((T, SW), f32), SDS((T, SW), f32), SDS((T, SW), _MXU)], out_epi)

        tmg, tng, tkg = _rtile(T, 512), _tile(SW, 512), _tile(SW, 1024)

        def glu_epi(acc, ex, outs):
            lp = acc + ex[1][...]
            outs[0][...] = lp
            outs[1][...] = _glu(ex[0][...], lp).astype(outs[1].dtype)

        lp, ysb = _mm(
            "s5_glu", ygb, wglu, _NN, (T // tmg, SW // tng, SW // tkg),
            pl.BlockSpec((tmg, tkg), lambda i, j, k: (i, k)), pl.BlockSpec((tkg, tng), lambda i, j, k: (k, j)), (tmg, tng),
            [yg, vec("glu_b", l)], [pl.BlockSpec((tmg, tng), lambda i, j, k: (i, j)), pl.BlockSpec((1, tng), lambda i, j, k: (0, j))],
            [SDS((T, SW), f32), SDS((T, SW), _MXU)], [pl.BlockSpec((tmg, tng), lambda i, j, k: (i, j))] * 2, glu_epi)

        qkv = _conv_fwd("gdn_conv", p, m_qkv, cw, 3 * GW)
        blt = jnp.transpose(pba[:, :NH])[:, :, None]
        aint = jnp.transpose(pba[:, NH:2 * NH])[:, :, None]
        alog = W["gdn_a_log"][l].reshape(NH, 1, 1)
        dtb = W["gdn_dt_bias"][l].reshape(NH, 1, 1)
        nw = vec("gdn_norm_w", l)
        og, ssave = _gdn_fwd("gdn", qkv, p, m_z, blt, aint, alog, dtb, nw, NH, HD)

        a_s = _mm(
            "br_ssm", ysb, wbs, _NN, (T // tmg, 8, SW // tkg),
            pl.BlockSpec((tmg, tkg), lambda i, j, k: (i, k)), pl.BlockSpec((None, tkg, DS), lambda i, j, k: (j, k, 0)), (tmg, DS),
            [], [], [SDS((T, D), f32)], [pl.BlockSpec((tmg, DS), lambda i, j, k: (i, j))], _store_epi)[0]
        tkd = _tile(GW, 1024)
        gsb, ggb = m_gs // DS, m_gg // DS

        def merge_epi(acc, ex, outs):
            outs[0][...] = acc
            outs[1][...] = _merge(ex[1][...], ex[2][...], ex[0][...], acc).astype(outs[1].dtype)

        tile_ij = pl.BlockSpec((tmg, DS), lambda i, j, k: (i, j))
        a_g, merged = _mm(
            "br_gdn_merge", og, wbg, _NN, (T // tmg, 8, GW // tkd),
            pl.BlockSpec((tmg, tkd), lambda i, j, k: (i, k)), pl.BlockSpec((None, tkd, DS), lambda i, j, k: (j, k, 0)), (tmg, DS),
            [a_s, p, p], [tile_ij, pl.BlockSpec((tmg, DS), lambda i, j, k: (i, j + gsb)),
                          pl.BlockSpec((tmg, DS), lambda i, j, k: (i, j + ggb))],
            [SDS((T, D), f32), SDS((T, D), _MXU)], [tile_ij, tile_ij], merge_epi)
        if nxt:
            nxt.wait(merged)
            tok = nxt.start()
        x2, x2b, xh2, r2 = _mm_ln("mix_out_ln", merged, wo, x1, dep(vec("ln2_g", l), tok), vec("ln2_b", l), alpha, 1.0)
        sv["mx"] = dict(x1b=x1b, p=p, zoh_in=zoh_in, lbr_f=lbr_f, lbi_f=lbi_f, bblk_r=bblk_r, bblk_i=bblk_i, cblkT_r=cblkT_r,
                        cblkT_in=cblkT_in, sr=sr, si=si, dflat=dflat, y_raw=y_raw, yg=yg, ygb=ygb, lp=lp, ysb=ysb, qkv=qkv,
                        blt=blt, aint=aint, alog=alog, dtb=dtb, nw=nw, og=og, ssave=ssave, a_s=a_s, a_g=a_g, merged=merged,
                        xhat=xh2, rstd=r2)

        gate2, up2, hh2 = _ffn_up("ffn_up", x2b, wgu2)
        x3, x3b, xh3, r3 = _mm_ln("ffn_down_ln", hh2, wd2, x2, vec("ln3_g", l), vec("ln3_b", l), alpha, 0.5)
        sv["f2"] = dict(xb=x2b, gate=gate2, up=up2, h=hh2, xhat=xh3, rstd=r3)
        saves.append(sv)
        xc, xcb = x3, x3b
        if nxt:
            nxt.wait(x3)
            gathered = nxt.result()

    dy, loss_part = _loss_head("loss_head", xc, tg)
    loss = lax.psum(loss_part[0, 0], ("x", "y", "c"))

    big_out = {n: [lax.empty(W[n].shape, f32) for _ in range(4)] for n in _BIG}
    small_g = {n: [None] * L for n in _SMALL}
    pend = None
    for l in reversed(range(L)):
        sv, wl = saves[l], weights[l]
        mx = sv["mx"]
        p = mx["p"]
        tok = pend.start() if pend else None
        dx2, dwgu2, dwd2, dg3, db3 = _ffn_bwd("ffn_b", dy, sv["f2"], wl["wgu2"], wl["wd2"], dep(vec("ln3_g", l), tok), alpha)
        small_g["ln3_g"][l], small_g["ln3_b"][l] = dg3[0], db3[0]

        if pend:
            pend.wait(dx2)
            tok = pend.start()
        dz2, dmixb, dg2, db2 = _ln_bwd("mix_lnb", dx2, mx["xhat"], mx["rstd"], dep(vec("ln2_g", l), tok), 1.0)
        small_g["ln2_g"][l], small_g["ln2_b"][l] = dg2[0], db2[0]
        tmg, tkd = _rtile(T, 512), _tile(D, 512)
        tnq = 512 if (m_gs % 512 == 0 and D % 512 == 0) else DS
        tkq = _tile(D, 2048)
        gsb, ggb = m_gs // tnq, m_gg // tnq

        def dmerge_epi(acc, ex, outs):
            _, vjp = jax.vjp(_merge, ex[0][...], ex[1][...], ex[2][...], ex[3][...])
            dgs, dgg, das, dag = vjp(acc)
            outs[0][...] = das.astype(outs[0].dtype)
            outs[1][...] = dag.astype(outs[1].dtype)
            outs[2][...] = dgs.astype(outs[2].dtype)
            outs[3][...] = dgg.astype(outs[3].dtype)

        tile_ij = pl.BlockSpec((tmg, tnq), lambda i, j, k: (i, j))
        das, dag, dgs, dgg = _mm(
            "mix_dmerge", dmixb, wl["wo"], _NT, (T // tmg, D // tnq, D // tkq),
            pl.BlockSpec((tmg, tkq), lambda i, j, k: (i, k)), pl.BlockSpec((tnq, tkq), lambda i, j, k: (j, k)), (tmg, tnq),
            [p, p, mx["a_s"], mx["a_g"]],
            [pl.BlockSpec((tmg, tnq), lambda i, j, k: (i, j + gsb)), pl.BlockSpec((tmg, tnq), lambda i, j, k: (i, j + ggb)),
             tile_ij, tile_ij],
            [SDS((T, D), _MXU)] * 4, [tile_ij] * 4, dmerge_epi)
        dwo = _mm_tn("mix_dwo", mx["merged"], dmixb, _GDT)
        dys = _mm_nt_slots("br_ssm_dx", das, wl["wbs"], f32)
        dog = _mm_nt_slots("br_gdn_dx", dag, wl["wbg"], f32)
        dwbs = _mm_tn_slots("br_ssm_dw", mx["ysb"], das, _GDT)
        dwbg = _mm_tn_slots("br_gdn_dw", mx["og"], dag, _GDT)

        def glu_b_fn(dys_t, yg_t, lp_t):
            _, vjp = jax.vjp(_glu, yg_t, lp_t)
            dyg1, dlp = vjp(dys_t)
            return dyg1, dlp, dlp

        dyg1, dlpb, dglub = _colsum_kernel("s5_glu_b", glu_b_fn, [dys, mx["yg"], mx["lp"]], [0, 0, 0], [f32, _MXU], 1, SW)
        small_g["glu_b"][l] = dglub[0]
        dwglu = _mm_tn("s5_dwglu", mx["ygb"], dlpb, _GDT)
        tng, tkg = _tile(SW, 512), _tile(SW, 512)

        def dyraw_epi(acc, ex, outs):
            _, vjp = jax.vjp(jax.nn.gelu, ex[1][...])
            (d,) = vjp(ex[0][...] + acc)
            outs[0][...] = d

        t_ij = pl.BlockSpec((tmg, tng), lambda i, j, k: (i, j))
        (dyraw,) = _mm(
            "s5_dyraw", dlpb, wl["wglu"], _NT, (T // tmg, SW // tng, SW // tkg),
            pl.BlockSpec((tmg, tkg), lambda i, j, k: (i, k)), pl.BlockSpec((tng, tkg), lambda i, j, k: (j, k)), (tmg, tng),
            [dyg1, mx["y_raw"]], [t_ij, t_ij], [SDS((T, SW), f32)], [t_ij], dyraw_epi)

        def dd_fn(dyr, u_t, d_t):
            return d_t * dyr, dyr * u_t

        dud, dd = _colsum_kernel("s5_dd", dd_fn, [dyraw, p, mx["dflat"]], [0, 0, None], [f32], 1, SW)
        small_g["ssm_d"][l] = dd.reshape(G, H)
        dsr, dsi = _bd2("s5_ds", dyraw, 0, mx["cblkT_r"], mx["cblkT_in"])
        dcb_r, dcb_i = _bdT2("s5_dc", mx["sr"], mx["si"], 0, dyraw, dyraw, 0, 8 * P, 8 * H, J)
        small_g["ssm_c_re"][l] = _blockdiag_extract(jnp.transpose(dcb_r, (0, 2, 1)), H, P)
        small_g["ssm_c_im"][l] = -_blockdiag_extract(jnp.transpose(dcb_i, (0, 2, 1)), H, P)
        ar, ai, dlr, dli = _s5_scan_bwd("s5_scan_b", dsr, dsi, mx["sr"], mx["si"], mx["lbr_f"], mx["lbi_f"])

        def du_epi(acc, ex, outs):
            outs[0][...] = (acc + ex[0][...]).astype(outs[0].dtype)

        (du,) = _bd_sum(
            "s5_du", ar, ai, jnp.transpose(mx["bblk_r"], (0, 2, 1)), jnp.transpose(mx["bblk_i"], (0, 2, 1)), [dud],
            lambda tm, nb: [pl.BlockSpec((tm, nb), lambda i, j: (i, j))], [SDS((T, SW), _MXU)], du_epi)
        dbb_r, dbb_i = _bdT2("s5_db", p, p, 0, ar, ai, 0, 8 * H, 8 * P, J)
        dbbr_t = jnp.transpose(_blockdiag_extract(dbb_r, H, P), (1, 0, 2))
        dbbi_t = jnp.transpose(_blockdiag_extract(dbb_i, H, P), (1, 0, 2))
        da_re, da_im, dlog_dt, dbre_t, dbim_t = _zoh_bwd("zoh_b", *mx["zoh_in"], dlr.reshape(G, P), dli.reshape(G, P),
                                                         dbbr_t, dbbi_t)
        small_g["ssm_a_re"][l], small_g["ssm_a_im"][l], small_g["ssm_log_dt"][l] = da_re, da_im, dlog_dt[:, 0]
        small_g["ssm_b_re"][l] = jnp.transpose(dbre_t, (1, 2, 0))
        small_g["ssm_b_im"][l] = jnp.transpose(dbim_t, (1, 2, 0))

        dqkv3, dzb, dbl, dain, dal, ddtb, dnw = _gdn_bwd("gdn_b", mx["qkv"], p, m_z, mx["blt"], mx["aint"], mx["alog"],
                                                         mx["dtb"], mx["nw"], mx["ssave"], dog, NH, HD)
        small_g["gdn_a_log"][l], small_g["gdn_dt_bias"][l], small_g["gdn_norm_w"][l] = dal[:, 0, 0], ddtb[:, 0, 0], dnw[0]
        dqkv_pre, dcw = _conv_bwd("gdn_conv_b", p, m_qkv, wl["cw"], dqkv3)
        if pend:
            pend.wait(dqkv_pre)
            tok = pend.start()

        dpm = jnp.concatenate([du, dqkv_pre, dzb, dgs, dgg], axis=1)
        dpba = dep(jnp.concatenate([jnp.transpose(dbl[:, :, 0]), jnp.transpose(dain[:, :, 0]),
                                    jnp.zeros((T, 128 - 2 * NH), f32)], axis=1), tok).astype(_MXU)
        tnd, tkm = _tile(D, 1024), _tile(NM, 2304)
        t_ba = _mm(
            "mix_dx_ba", dpba, wl["wba"], _NT, (T // tmg, D // tnd, 1),
            pl.BlockSpec((tmg, 128), lambda i, j, k: (i, 0)), pl.BlockSpec((tnd, 128), lambda i, j, k: (j, 0)), (tmg, tnd),
            [], [], [SDS((T, D), f32)], [pl.BlockSpec((tmg, tnd), lambda i, j, k: (i, j))], _store_epi)[0]

        def dx1_epi(acc, ex, outs):
            outs[0][...] = alpha * ex[0][...] + ex[1][...] + acc

        t_d = pl.BlockSpec((tmg, tnd), lambda i, j, k: (i, j))
        (dx1,) = _mm(
            "mix_dx", dpm, wl["wmain"], _NT, (T // tmg, D // tnd, NM // tkm),
            pl.BlockSpec((tmg, tkm), lambda i, j, k: (i, k)), pl.BlockSpec((tnd, tkm), lambda i, j, k: (j, k)), (tmg, tnd),
            [dz2, t_ba], [t_d, t_d], [SDS((T, D), f32)], [t_d], dx1_epi)
        tnm = _tile(NM, 1024)
        tkt = _tile(T, 2048)
        dwmain = _mm(
            "mix_dw", jnp.transpose(mx["x1b"]), dpm, _NN, (D // tkd, NM // tnm, T // tkt),
            pl.BlockSpec((tkd, tkt), lambda i, j, k: (i, k)), pl.BlockSpec((tkt, tnm), lambda i, j, k: (k, j)), (tkd, tnm),
            [], [], [SDS((D, NM), _GDT)], [pl.BlockSpec((tkd, tnm), lambda i, j, k: (i, j))], _store_epi)[0]
        dwba = _mm_tn("mix_dw_ba", mx["x1b"], dpba, _GDT)
        dwin = jnp.concatenate([dwmain[:, :o_b], dwba[:, :2 * NH], dwmain[:, o_b:]], axis=1)
        dwin8 = jnp.transpose(dwin.reshape(D, 8, IN // 8), (1, 0, 2))
        dcw8 = jnp.transpose(dcw.reshape(KC, 8, 3 * GW // 8), (1, 0, 2))

        dx0, dwgu1, dwd1, dg1, db1 = _ffn_bwd("ffn_b", dx1, sv["f1"], wl["wgu1"], wl["wd1"], vec("ln1_g", l), alpha)
        small_g["ln1_g"][l], small_g["ln1_b"][l] = dg1[0], db1[0]
        dy = dx0

        parts = [dwgu1, dwd1.reshape(8, -1, D), dwin8, dcw8, dwglu.reshape(8, SW // 8, SW), dwbs, dwbg,
                 dwo.reshape(8, DS, D), dwgu2, dwd2.reshape(8, -1, D)]
        late = None
        if pend:
            pend.wait(dx0)
            if l > 0:
                for n, (full, recv, last) in zip(_BIG, pend.out):
                    big_out[n] = _adamw_big("adamw_" + n, full, recv, _coord(last), W[n], M[n], V[n], l + 1, big_out[n])
            else:
                late = pend.out
        pend = _AsyncReduceScatter("rsp", parts, _PATHS)

    seg = 8 * 128

    def padded(n):
        return -(-n // seg) * seg

    def pack(arrs):
        flat = jnp.concatenate([jnp.pad(a.reshape(-1), (0, padded(a.size) - a.size)) for a in arrs])
        n = flat.shape[0]
        rows = -(-n // (128 * 512)) * 512
        return jnp.pad(flat, (0, rows * 128 - n)).reshape(rows, 128)

    tok = pend.start()
    gs_full = [jnp.stack(small_g[n]).reshape(W[n].shape) for n in _SMALL]
    gpack = dep(pack(gs_full), tok)
    (gall,) = _all_gather("ag_small", [gpack], ["yx"])
    pend.wait(gall)
    tok = pend.start()
    if late:
        for i, (n, (full, recv, last)) in enumerate(zip(_BIG, late)):
            recv = dep(recv, tok) if i == 0 else recv
            big_out[n] = _adamw_big("adamw_" + n, full, recv, _coord(last), W[n], M[n], V[n], 1, big_out[n])
    sg, sd, sm, sv_ = _adamw_small("adamw_small", gall, dep(pack([W[n] for n in _SMALL]), tok),
                                   pack([M[n] for n in _SMALL]), pack([V[n] for n in _SMALL]))
    pend.wait([sg] + ([big_out[n][0] for n in _BIG] if late else []))
    pend.start()
    pend.wait(sv_)
    for n, (full, recv, last) in zip(_BIG, pend.out):
        big_out[n] = _adamw_big("adamw_" + n, full, recv, _coord(last), W[n], M[n], V[n], 0, big_out[n])

    def unpack(packed):
        out, row = {}, 0
        for n in _SMALL:
            sz = math.prod(W[n].shape)
            rows = padded(sz) // 128
            out[n] = packed[row:row + rows].reshape(-1)[:sz].reshape(W[n].shape)
            row += rows
        return out

    res = [unpack(a) for a in (sg, sd, sm, sv_)]
    for n in _BIG:
        for i in range(4):
            res[i][n] = big_out[n][i]
    outs = [loss, dy[None]]
    for i in range(4):
        outs += [res[i][n] for n in _ORDER]
    return tuple(outs)


def kernel(x, ffn1_w_gu, ffn1_w_down, ln1_g, ln1_b, w_in, conv_w, ssm_a_re, ssm_a_im, ssm_log_dt, ssm_b_re, ssm_b_im, ssm_c_re, ssm_c_im, ssm_d, glu_w, glu_b, gdn_a_log, gdn_dt_bias, gdn_norm_w, w_br_ssm, w_br_gdn, w_out, ln2_g, ln2_b, ffn2_w_gu, ffn2_w_down, ln3_g, ln3_b, loss_target, m_ffn1_w_gu, m_ffn1_w_down, m_ln1_g, m_ln1_b, m_w_in, m_conv_w, m_ssm_a_re, m_ssm_a_im, m_ssm_log_dt, m_ssm_b_re, m_ssm_b_im, m_ssm_c_re, m_ssm_c_im, m_ssm_d, m_glu_w, m_glu_b, m_gdn_a_log, m_gdn_dt_bias, m_gdn_norm_w, m_w_br_ssm, m_w_br_gdn, m_w_out, m_ln2_g, m_ln2_b, m_ffn2_w_gu, m_ffn2_w_down, m_ln3_g, m_ln3_b, v_ffn1_w_gu, v_ffn1_w_down, v_ln1_g, v_ln1_b, v_w_in, v_conv_w, v_ssm_a_re, v_ssm_a_im, v_ssm_log_dt, v_ssm_b_re, v_ssm_b_im, v_ssm_c_re, v_ssm_c_im, v_ssm_d, v_glu_w, v_glu_b, v_gdn_a_log, v_gdn_dt_bias, v_gdn_norm_w, v_w_br_ssm, v_w_br_gdn, v_w_out, v_ln2_g, v_ln2_b, v_ffn2_w_gu, v_ffn2_w_down, v_ln3_g, v_ln3_b):
    given = dict(locals())
    W = {n: given[n] for n in _ORDER}
    M = {n: given["m_" + n] for n in _ORDER}
    V = {n: given["v_" + n] for n in _ORDER}
    return _step(x, loss_target, W, M, V)
```

```python
import functools
import math

import jax
import jax.numpy as jnp
from jax import lax
from jax.experimental import pallas as pl
from jax.experimental.pallas import tpu as pltpu

f32 = jnp.float32
_MXU = jnp.bfloat16
_GDT = jnp.bfloat16
_HP = lax.Precision.HIGHEST
_VMEM_LIMIT = 56 * 1024 * 1024
_MESH_T = pl.DeviceIdType.MESH

LN_EPS = 1e-5
RMS_EPS = 1e-6
L2_EPS = 1e-6
CHUNK = 64
ADAM_LR = 0.001
ADAM_B1 = 0.9
ADAM_B2 = 0.999
ADAM_EPS = 1e-08
ADAM_WD = 0.01
ADAM_STEP = 10

_NN = (((1,), (0,)), ((), ()))
_NT = (((1,), (1,)), ((), ()))
_TN = (((0,), (0,)), ((), ()))

SDS = jax.ShapeDtypeStruct


def _cp(sem):
    return pltpu.CompilerParams(dimension_semantics=sem, vmem_limit_bytes=_VMEM_LIMIT)


def _tile(n, pref):
    if n <= pref:
        return n
    t = (pref // 128) * 128
    while t >= 128:
        if n % t == 0:
            return t
        t -= 128
    return n


def _rtile(n, pref):
    if n <= pref:
        return n
    t = (pref // 16) * 16
    while t >= 16:
        if n % t == 0:
            return t
        t -= 16
    return n


def _mm(name, a, b, dims, grid, a_spec, b_spec, acc_shape, extras, extra_specs, out_shape, out_specs, epilogue):
    nk = grid[2]
    ne = len(extras)
    no = len(out_shape)

    def body(*refs):
        a_ref, b_ref = refs[0], refs[1]
        ex = refs[2:2 + ne]
        outs = refs[2 + ne:2 + ne + no]
        acc = refs[-1]
        k = pl.program_id(2)
        part = lax.dot_general(a_ref[...].astype(_MXU), b_ref[...].astype(_MXU), dims, preferred_element_type=f32)

        @pl.when(k == 0)
        def _():
            acc[...] = part

        @pl.when(k > 0)
        def _():
            acc[...] += part

        @pl.when(k == nk - 1)
        def _():
            epilogue(acc[...], ex, outs)

    return pl.pallas_call(
        body, grid=grid, in_specs=[a_spec, b_spec, *extra_specs], out_specs=list(out_specs), out_shape=list(out_shape),
        scratch_shapes=[pltpu.VMEM(acc_shape, f32)], compiler_params=_cp(("parallel", "parallel", "arbitrary")), name=name,
    )(a, b, *extras)


def _store_epi(acc, ex, outs):
    for o in outs:
        o[...] = acc.astype(o.dtype)


def _ln_epilogue(alpha, c):
    def epi(acc, ex, outs):
        x_ref, g_ref, b_ref = ex
        y_ref, yb_ref, xh_ref, r_ref = outs
        z = alpha * x_ref[...] + c * acc
        mu = jnp.mean(z, axis=-1, keepdims=True)
        zc = z - mu
        var = jnp.mean(zc * zc, axis=-1, keepdims=True)
        r = lax.rsqrt(var + LN_EPS)
        xh = zc * r
        y = xh * g_ref[...] + b_ref[...]
        y_ref[...] = y
        yb_ref[...] = y.astype(yb_ref.dtype)
        xh_ref[...] = xh
        r_ref[...] = r
    return epi


def _mm_ln(name, a, w, x, g, b, alpha, c):
    T, K = a.shape
    D = w.shape[1]
    tm, tk = _rtile(T, 512), _tile(K, 512)
    row = pl.BlockSpec((tm, D), lambda i, j, k: (i, 0))
    vec = pl.BlockSpec((1, D), lambda i, j, k: (0, 0))
    return _mm(
        name, a, w, _NN, (T // tm, 1, K // tk),
        pl.BlockSpec((tm, tk), lambda i, j, k: (i, k)), pl.BlockSpec((tk, D), lambda i, j, k: (k, 0)), (tm, D),
        [x, g, b], [row, vec, vec],
        [SDS((T, D), f32), SDS((T, D), _MXU), SDS((T, D), f32), SDS((T, 1), f32)],
        [row, row, row, pl.BlockSpec((tm, 1), lambda i, j, k: (i, 0))],
        _ln_epilogue(alpha, c),
    )


def _ln_bwd(name, dy, xhat, rstd, g, c):
    T, D = dy.shape
    tm = _rtile(T, 256)

    def body(dy_ref, xh_ref, r_ref, g_ref, dz_ref, df_ref, dg_ref, db_ref):
        i = pl.program_id(0)
        dyv = dy_ref[...]
        xh = xh_ref[...]
        dxh = dyv * g_ref[...]
        m1 = jnp.mean(dxh, axis=-1, keepdims=True)
        m2 = jnp.mean(dxh * xh, axis=-1, keepdims=True)
        dz = r_ref[...] * (dxh - m1 - xh * m2)
        dz_ref[...] = dz
        df_ref[...] = (c * dz).astype(df_ref.dtype)
        pg = jnp.sum(dyv * xh, axis=0, keepdims=True)
        pb = jnp.sum(dyv, axis=0, keepdims=True)

        @pl.when(i == 0)
        def _():
            dg_ref[...] = pg
            db_ref[...] = pb

        @pl.when(i > 0)
        def _():
            dg_ref[...] += pg
            db_ref[...] += pb

    row = pl.BlockSpec((tm, D), lambda i: (i, 0))
    vec = pl.BlockSpec((1, D), lambda i: (0, 0))
    return pl.pallas_call(
        body, grid=(T // tm,), in_specs=[row, row, pl.BlockSpec((tm, 1), lambda i: (i, 0)), vec],
        out_specs=[row, row, vec, vec],
        out_shape=[SDS((T, D), f32), SDS((T, D), _MXU), SDS((1, D), f32), SDS((1, D), f32)],
        compiler_params=_cp(("arbitrary",)), name=name,
    )(dy, xhat, rstd, g)


def _swiglu(g, u):
    return jax.nn.silu(g) * u


def _ffn_up(name, xb, wgu):
    T, D = xb.shape
    FS = wgu.shape[2]
    F = 4 * FS
    tm = _rtile(T, 256)

    def body(x_ref, wg_ref, wu_ref, g_ref, u_ref, h_ref):
        xv = x_ref[...]
        g = jnp.dot(xv, wg_ref[...], preferred_element_type=f32)
        u = jnp.dot(xv, wu_ref[...], preferred_element_type=f32)
        g_ref[...] = g.astype(g_ref.dtype)
        u_ref[...] = u.astype(u_ref.dtype)
        h_ref[...] = _swiglu(g, u).astype(h_ref.dtype)

    out = pl.BlockSpec((tm, FS), lambda j, i: (i, j))
    return pl.pallas_call(
        body, grid=(4, T // tm),
        in_specs=[pl.BlockSpec((tm, D), lambda j, i: (i, 0)),
                  pl.BlockSpec((None, D, FS), lambda j, i: (j, 0, 0)),
                  pl.BlockSpec((None, D, FS), lambda j, i: (j + 4, 0, 0))],
        out_specs=[out, out, out],
        out_shape=[SDS((T, F), _MXU), SDS((T, F), _MXU), SDS((T, F), _MXU)],
        compiler_params=_cp(("parallel", "arbitrary")), name=name,
    )(xb, wgu, wgu)


def _ffn_bwd(pfx, dy, sv, wgu, wd, g_ln, alpha):
    T, D = dy.shape
    FS = wgu.shape[2]
    F = 4 * FS
    dz, dfb, dg, db = _ln_bwd(pfx + "_lnb", dy, sv["xhat"], sv["rstd"], g_ln, 0.5)

    tm, tn, tk = _rtile(T, 1024), _tile(F, 512), _tile(D, 2048)

    def epi(acc, ex, outs):
        g_ref, u_ref = ex
        _, vjp = jax.vjp(_swiglu, g_ref[...].astype(f32), u_ref[...].astype(f32))
        dgate, dup = vjp(acc)
        outs[0][0] = dgate.astype(outs[0].dtype)
        outs[0][1] = dup.astype(outs[0].dtype)

    gu = pl.BlockSpec((tm, tn), lambda i, j, k: (i, j))
    (dgu,) = _mm(
        pfx + "_dh", dfb, wd, _NT, (T // tm, F // tn, D // tk),
        pl.BlockSpec((tm, tk), lambda i, j, k: (i, k)), pl.BlockSpec((tn, tk), lambda i, j, k: (j, k)), (tm, tn),
        [sv["gate"], sv["up"]], [gu, gu],
        [SDS((2, T, F), _MXU)], [pl.BlockSpec((2, tm, tn), lambda i, j, k: (0, i, j))], epi,
    )

    tm2, tk2 = _tile(F, 512), _rtile(T, 512)
    (dwd,) = _mm(
        pfx + "_dwd", sv["h"], dfb, _TN, (F // tm2, 1, T // tk2),
        pl.BlockSpec((tk2, tm2), lambda i, j, k: (k, i)), pl.BlockSpec((tk2, D), lambda i, j, k: (k, 0)), (tm2, D),
        [], [], [SDS((F, D), _GDT)], [pl.BlockSpec((tm2, D), lambda i, j, k: (i, 0))], _store_epi,
    )

    tn3 = _tile(D, 1024)

    def epi3(acc, ex, outs):
        outs[0][...] = alpha * ex[0][...] + acc

    (dx,) = _mm(
        pfx + "_dx", dgu, wgu, _NT, (T // tm, D // tn3, 8),
        pl.BlockSpec((None, tm, FS), lambda i, j, k: (k // 4, i, k % 4)),
        pl.BlockSpec((None, tn3, FS), lambda i, j, k: (k, j, 0)), (tm, tn3),
        [dz], [pl.BlockSpec((tm, tn3), lambda i, j, k: (i, j))],
        [SDS((T, D), f32)], [pl.BlockSpec((tm, tn3), lambda i, j, k: (i, j))], epi3,
    )

    tm4, tk4 = _rtile(D, 512), _tile(T, 2048)
    (dwgu,) = _mm(
        pfx + "_dwgu", jnp.transpose(sv["xb"]), dgu, _NN, (D // tm4, 8, T // tk4),
        pl.BlockSpec((tm4, tk4), lambda i, j, k: (i, k)),
        pl.BlockSpec((None, tk4, FS), lambda i, j, k: (j // 4, k, j % 4)), (tm4, FS),
        [], [], [SDS((8, D, FS), _GDT)], [pl.BlockSpec((None, tm4, FS), lambda i, j, k: (j, i, 0))], _store_epi,
    )
    return dx, dwgu, dwd, dg, db


def _zoh(a_re, a_im, log_dt, b_re_t, b_im_t):
    dt = jnp.exp(log_dt)
    mag = jnp.exp(a_re * dt)
    lr_, li_ = mag * jnp.cos(a_im * dt), mag * jnp.sin(a_im * dt)
    den = a_re * a_re + a_im * a_im
    pr, pi = lr_ - 1.0, li_
    qr, qi = a_re / den, -a_im / den
    zr, zi = pr * qr - pi * qi, pr * qi + pi * qr
    bbr = zr[None] * b_re_t - zi[None] * b_im_t
    bbi = zr[None] * b_im_t + zi[None] * b_re_t
    return lr_, li_, bbr, bbi


def _zoh_fwd(name, a_re, a_im, log_dt, b_re_t, b_im_t):
    G, P = a_re.shape
    H = b_re_t.shape[0]

    def body(ar, ai, ld, br, bi, o1, o2, o3, o4):
        r = _zoh(ar[...], ai[...], ld[...], br[...], bi[...])
        o1[...], o2[...], o3[...], o4[...] = r

    return pl.pallas_call(
        body, out_shape=[SDS((G, P), f32), SDS((G, P), f32), SDS((H, G, P), f32), SDS((H, G, P), f32)], name=name,
    )(a_re, a_im, log_dt, b_re_t, b_im_t)


def _zoh_bwd(name, a_re, a_im, log_dt, b_re_t, b_im_t, dlr, dli, dbbr, dbbi):
    G, P = a_re.shape
    H = b_re_t.shape[0]

    def body(ar, ai, ld, br, bi, g1, g2, g3, g4, o1, o2, o3, o4, o5):
        _, vjp = jax.vjp(_zoh, ar[...], ai[...], ld[...], br[...], bi[...])
        r = vjp((g1[...], g2[...], g3[...], g4[...]))
        o1[...], o2[...], o3[...], o4[...], o5[...] = r

    return pl.pallas_call(
        body, out_shape=[SDS((G, P), f32), SDS((G, P), f32), SDS((G, 1), f32), SDS((H, G, P), f32), SDS((H, G, P), f32)],
        name=name,
    )(a_re, a_im, log_dt, b_re_t, b_im_t, dlr, dli, dbbr, dbbi)


def _blockdiag(m):
    G, A, B = m.shape
    eye = jnp.eye(8, dtype=bool)
    m4 = m.reshape(G // 8, 8, A, B)
    out = jnp.where(eye[None, :, None, :, None], m4[:, :, :, None, :], jnp.zeros((), m.dtype))
    return out.reshape(G // 8, 8 * A, 8 * B)


def _blockdiag_extract(mb, A, B):
    J = mb.shape[0]
    m5 = mb.reshape(J, 8, A, 8, B)
    d = jnp.stack([m5[:, i, :, i, :] for i in range(8)], axis=1)
    return d.reshape(J * 8, A, B)


def _bd2(name, a, a_col0, b1, b2, out_dtype=f32):
    T = a.shape[0]
    J, KA, NB = b1.shape
    tm = _rtile(T, 512)

    def body(a_ref, b1_ref, b2_ref, o1, o2):
        av = a_ref[...].astype(_MXU)
        o1[...] = jnp.dot(av, b1_ref[...].astype(_MXU), preferred_element_type=f32).astype(o1.dtype)
        o2[...] = jnp.dot(av, b2_ref[...].astype(_MXU), preferred_element_type=f32).astype(o2.dtype)

    bs = pl.BlockSpec((None, KA, NB), lambda i, j: (j, 0, 0))
    os_ = pl.BlockSpec((tm, NB), lambda i, j: (i, j))
    return pl.pallas_call(
        body, grid=(T // tm, J), in_specs=[pl.BlockSpec((tm, KA), lambda i, j: (i, j + a_col0)), bs, bs],
        out_specs=[os_, os_], out_shape=[SDS((T, J * NB), out_dtype)] * 2,
        compiler_params=_cp(("parallel", "parallel")), name=name,
    )(a, b1, b2)


def _bd_sum(name, a1, a2, b1, b2, extras, extra_specs_fn, out_shape, epilogue):
    T = a1.shape[0]
    J, KA, NB = b1.shape
    tm = _rtile(T, 512)
    ne = len(extras)

    def body(*refs):
        a1_ref, a2_ref, b1_ref, b2_ref = refs[:4]
        ex = refs[4:4 + ne]
        outs = refs[4 + ne:]
        acc = jnp.dot(a1_ref[...].astype(_MXU), b1_ref[...].astype(_MXU), preferred_element_type=f32)
        acc = acc + jnp.dot(a2_ref[...].astype(_MXU), b2_ref[...].astype(_MXU), preferred_element_type=f32)
        epilogue(acc, ex, outs)

    as_ = pl.BlockSpec((tm, KA), lambda i, j: (i, j))
    bs = pl.BlockSpec((None, KA, NB), lambda i, j: (j, 0, 0))
    os_ = pl.BlockSpec((tm, NB), lambda i, j: (i, j))
    return pl.pallas_call(
        body, grid=(T // tm, J), in_specs=[as_, as_, bs, bs, *extra_specs_fn(tm, NB)],
        out_specs=[os_] * len(out_shape), out_shape=list(out_shape),
        compiler_params=_cp(("parallel", "parallel")), name=name,
    )(a1, a2, b1, b2, *extras)


def _bdT2(name, a1, a2, a_col0, b1, b2, b_col0, KA, NB, J):
    T = a1.shape[0]
    tk = _rtile(T, 512)

    def body(a1_ref, a2_ref, b1_ref, b2_ref, o1, o2):
        k = pl.program_id(1)
        p1 = lax.dot_general(a1_ref[...].astype(_MXU), b1_ref[...].astype(_MXU), _TN, preferred_element_type=f32)
        p2 = lax.dot_general(a2_ref[...].astype(_MXU), b2_ref[...].astype(_MXU), _TN, preferred_element_type=f32)

        @pl.when(k == 0)
        def _():
            o1[...] = p1
            o2[...] = p2

        @pl.when(k > 0)
        def _():
            o1[...] += p1
            o2[...] += p2

    as_ = pl.BlockSpec((tk, KA), lambda j, k: (k, j + a_col0))
    bs = pl.BlockSpec((tk, NB), lambda j, k: (k, j + b_col0))
    os_ = pl.BlockSpec((None, KA, NB), lambda j, k: (j, 0, 0))
    return pl.pallas_call(
        body, grid=(J, T // tk), in_specs=[as_, as_, bs, bs], out_specs=[os_, os_],
        out_shape=[SDS((J, KA, NB), f32)] * 2, compiler_params=_cp(("parallel", "arbitrary")), name=name,
    )(a1, a2, b1, b2)


_RB = 8


def _cmul(ar, ai, br, bi):
    return ar * br - ai * bi, ar * bi + ai * br


def _lam_powers(lr_v, li_v, cb):
    pw = {1: (lr_v, li_v)}
    for k in range(2, _RB + 1):
        pw[k] = _cmul(*pw[k - 1], lr_v, li_v)
    return pw


def _row_powers(pw, row, cb, reverse):
    outr = jnp.zeros((_RB, cb), f32)
    outi = jnp.zeros((_RB, cb), f32)
    for r in range(_RB):
        k = _RB - r if reverse else r + 1
        outr = jnp.where(row == r, pw[k][0], outr)
        outi = jnp.where(row == r, pw[k][1], outi)
    return outr, outi


def _tile_scan(xr, xi, pw, row, reverse):
    for k in (1, 2, 4):
        if reverse:
            keep = row < _RB - k
            shr, shi = pltpu.roll(xr, _RB - k, 0), pltpu.roll(xi, _RB - k, 0)
        else:
            keep = row >= k
            shr, shi = pltpu.roll(xr, k, 0), pltpu.roll(xi, k, 0)
        shr, shi = jnp.where(keep, shr, 0.0), jnp.where(keep, shi, 0.0)
        mr, mi = pw[k]
        xr, xi = xr + (mr * shr - mi * shi), xi + (mr * shi + mi * shr)
    return xr, xi


def _s5_scan(name, bur, bui, lr_, li_):
    T, N = bur.shape
    cb = _tile(N, 512)

    def body(br_ref, bi_ref, lr_ref, li_ref, sr_ref, si_ref):
        pw = _lam_powers(lr_ref[...], li_ref[...], cb)
        row = lax.broadcasted_iota(jnp.int32, (_RB, cb), 0)
        cr, ci = _row_powers(pw, row, cb, False)

        def step(n, carry):
            pr, pi = carry
            t0 = pl.multiple_of(n * _RB, _RB)
            xr, xi = _tile_scan(br_ref[pl.ds(t0, _RB), :], bi_ref[pl.ds(t0, _RB), :], pw, row, False)
            xr, xi = xr + (cr * pr - ci * pi), xi + (cr * pi + ci * pr)
            sr_ref[pl.ds(t0, _RB), :] = xr
            si_ref[pl.ds(t0, _RB), :] = xi
            return xr[_RB - 1:_RB, :], xi[_RB - 1:_RB, :]

        z = jnp.zeros((1, cb), f32)
        lax.fori_loop(0, T // _RB, step, (z, z))

    col = pl.BlockSpec((T, cb), lambda j: (0, j))
    vec = pl.BlockSpec((1, cb), lambda j: (0, j))
    return pl.pallas_call(
        body, grid=(N // cb,), in_specs=[col, col, vec, vec], out_specs=[col, col],
        out_shape=[SDS((T, N), f32)] * 2, compiler_params=_cp(("parallel",)), name=name,
    )(bur, bui, lr_, li_)


def _s5_scan_bwd(name, dsr, dsi, sr, si, lr_, li_):
    T, N = dsr.shape
    cb = _tile(N, 256)

    def body(dr_ref, di_ref, sr_ref, si_ref, lr_ref, li_ref, ar_ref, ai_ref, glr_ref, gli_ref):
        pw = _lam_powers(lr_ref[...], -li_ref[...], cb)
        row = lax.broadcasted_iota(jnp.int32, (_RB, cb), 0)
        cr, ci = _row_powers(pw, row, cb, True)
        NT = T // _RB

        def tile(t0, nxt, prev_last):
            xr, xi = _tile_scan(dr_ref[pl.ds(t0, _RB), :], di_ref[pl.ds(t0, _RB), :], pw, row, True)
            xr, xi = xr + (cr * nxt[0] - ci * nxt[1]), xi + (cr * nxt[1] + ci * nxt[0])
            ar_ref[pl.ds(t0, _RB), :] = xr
            ai_ref[pl.ds(t0, _RB), :] = xi
            pr = jnp.where(row == 0, prev_last[0], pltpu.roll(sr_ref[pl.ds(t0, _RB), :], 1, 0))
            pi = jnp.where(row == 0, prev_last[1], pltpu.roll(si_ref[pl.ds(t0, _RB), :], 1, 0))
            return xr, xi, xr * pr + xi * pi, xi * pr - xr * pi

        def step(n, carry):
            nr, ni, glr, gli = carry
            t0 = pl.multiple_of((NT - 1 - n) * _RB, _RB)
            tp = pl.multiple_of((NT - 2 - n) * _RB, _RB)
            prev_last = (sr_ref[pl.ds(tp, _RB), :][_RB - 1:_RB, :], si_ref[pl.ds(tp, _RB), :][_RB - 1:_RB, :])
            xr, xi, gr, gi = tile(t0, (nr, ni), prev_last)
            return xr[0:1, :], xi[0:1, :], glr + gr, gli + gi

        z1 = jnp.zeros((1, cb), f32)
        z8 = jnp.zeros((_RB, cb), f32)
        nr, ni, glr, gli = lax.fori_loop(0, NT - 1, step, (z1, z1, z8, z8))
        _, _, gr, gi = tile(0, (nr, ni), (z1, z1))
        glr_ref[...] = jnp.sum(glr + gr, axis=0, keepdims=True)
        gli_ref[...] = jnp.sum(gli + gi, axis=0, keepdims=True)

    col = pl.BlockSpec((T, cb), lambda j: (0, j))
    vec = pl.BlockSpec((1, cb), lambda j: (0, j))
    return pl.pallas_call(
        body, grid=(N // cb,), in_specs=[col, col, col, col, vec, vec], out_specs=[col, col, vec, vec],
        out_shape=[SDS((T, N), f32), SDS((T, N), f32), SDS((1, N), f32), SDS((1, N), f32)],
        compiler_params=_cp(("parallel",)), name=name,
    )(dsr, dsi, sr, si, lr_, li_)


def _conv_fwd(name, p, col0, w, GW3):
    T = p.shape[0]
    K = w.shape[0]
    cb = 128
    c0 = col0 // cb

    def body(x_ref, w_ref, o_ref, pad_ref):
        pad_ref[pl.ds(0, 8), :] = jnp.zeros((8, cb), f32)
        pad_ref[pl.ds(8, T), :] = x_ref[...]
        wv = w_ref[...]
        acc = jnp.zeros((T, cb), f32)
        for j in range(K):
            acc = acc + wv[j:j + 1, :] * pad_ref[pl.ds(8 - (K - 1) + j, T), :]
        o_ref[...] = jax.nn.silu(acc)

    return pl.pallas_call(
        body, grid=(GW3 // cb,),
        in_specs=[pl.BlockSpec((T, cb), lambda j: (0, j + c0)), pl.BlockSpec((K, cb), lambda j: (0, j))],
        out_specs=pl.BlockSpec((T, cb), lambda j: (0, j)), out_shape=SDS((T, GW3), f32),
        scratch_shapes=[pltpu.VMEM((T + 8, cb), f32)], compiler_params=_cp(("parallel",)), name=name,
    )(p, w)


def _conv_bwd(name, p, col0, w, dout3):
    T = p.shape[0]
    K = w.shape[0]
    GW = dout3.shape[2]
    GW3 = 3 * GW
    cb = 128
    c0 = col0 // cb
    nb = GW // cb

    def body(x_ref, w_ref, d_ref, dx_ref, dw_ref, pad_ref, dpad_ref):
        pad_ref[pl.ds(0, 8), :] = jnp.zeros((8, cb), f32)
        pad_ref[pl.ds(8, T), :] = x_ref[...]
        wv = w_ref[...]
        pre = jnp.zeros((T, cb), f32)
        for j in range(K):
            pre = pre + wv[j:j + 1, :] * pad_ref[pl.ds(8 - (K - 1) + j, T), :]
        _, vjp = jax.vjp(jax.nn.silu, pre)
        (dpre,) = vjp(d_ref[...])
        dpad_ref[pl.ds(0, T), :] = dpre
        dpad_ref[pl.ds(T, 8), :] = jnp.zeros((8, cb), f32)
        dx = jnp.zeros((T, cb), f32)
        rows = []
        for j in range(K):
            dx = dx + wv[j:j + 1, :] * dpad_ref[pl.ds((K - 1) - j, T), :]
            rows.append(jnp.sum(dpre * pad_ref[pl.ds(8 - (K - 1) + j, T), :], axis=0, keepdims=True))
        dx_ref[...] = dx.astype(dx_ref.dtype)
        for j in range(K):
            dw_ref[pl.ds(j, 1), :] = rows[j]

    return pl.pallas_call(
        body, grid=(GW3 // cb,),
        in_specs=[pl.BlockSpec((T, cb), lambda j: (0, j + c0)), pl.BlockSpec((K, cb), lambda j: (0, j)),
                  pl.BlockSpec((None, T, cb), lambda j: (j // nb, 0, j % nb))],
        out_specs=[pl.BlockSpec((T, cb), lambda j: (0, j)), pl.BlockSpec((K, cb), lambda j: (0, j))],
        out_shape=[SDS((T, GW3), _MXU), SDS((K, GW3), f32)],
        scratch_shapes=[pltpu.VMEM((T + 8, cb), f32), pltpu.VMEM((T + 8, cb), f32)],
        compiler_params=_cp(("parallel",)), name=name,
    )(p, w, dout3)


def _hdot(a, b, dims=_NN):
    return lax.dot_general(a, b, dims, precision=_HP, preferred_element_type=f32)


def _split(a):
    hi = a.astype(jnp.bfloat16)
    lo = (a - hi.astype(f32)).astype(jnp.bfloat16)
    return hi, lo


_BNN = (((2,), (1,)), ((0,), (0,)))
_BNT = (((2,), (2,)), ((0,), (0,)))
_BTN = (((1,), (1,)), ((0,), (0,)))


def _dot3_raw(a, b, dims):
    ah, al = _split(a)
    bh, bl = _split(b)
    d = functools.partial(lax.dot_general, dimension_numbers=dims, preferred_element_type=f32)
    return d(ah, bh) + (d(al, bh) + d(ah, bl))


@jax.custom_vjp
def _dot3(a, b):
    return _dot3_raw(a, b, _BNN)


def _dot3_fwd(a, b):
    return _dot3_raw(a, b, _BNN), (a, b)


def _dot3_bwd(res, g):
    a, b = res
    return _dot3_raw(g, b, _BNT), _dot3_raw(a, g, _BTN)


_dot3.defvjp(_dot3_fwd, _dot3_bwd)


def _ldot(a, b, dims=_BNN):
    return lax.dot_general(a.astype(_MXU), b.astype(_MXU), dims, preferred_element_type=f32)


def _sdot(a, b):
    return _ldot(a, b)


def _gdn_chunk(S, q, k, v, z, bl, ain, alog, dtb, nw):
    H, C, d = q.shape
    ri = lax.broadcasted_iota(jnp.int32, (H, C, C), 1)
    ci = lax.broadcasted_iota(jnp.int32, (H, C, C), 2)
    causal = ri >= ci
    strict = ri > ci
    tri = causal.astype(f32)
    qn = q * lax.rsqrt(jnp.sum(q * q, axis=-1, keepdims=True) + L2_EPS) * (d ** -0.5)
    kn = k * lax.rsqrt(jnp.sum(k * k, axis=-1, keepdims=True) + L2_EPS)
    beta = jax.nn.sigmoid(bl)
    g = -jnp.exp(alog) * jax.nn.softplus(ain + dtb)
    gb = jnp.broadcast_to(g, (H, C, C))
    gc_col = _dot3(tri, gb)
    gc_row = _dot3(jnp.ones((H, C, C), f32), jnp.where(ri <= ci, gb, 0.0))
    diff = jnp.where(causal, gc_col - gc_row, 0.0)
    decay = jnp.where(causal, jnp.exp(diff), 0.0)
    gcum = gc_col[:, :, 0:1]
    glast = gc_col[:, C - 1:C, 0:1]
    egc = jnp.exp(gcum)
    kb = kn * beta
    lower = jnp.where(strict, _ldot(kb, kn, _BNT) * decay, 0.0)
    x = jnp.concatenate([v * beta, kb * egc], axis=-1)
    m = -lower
    for it in range(6):
        x = x + _sdot(m, x)
        if it < 5:
            m = _sdot(m, m)
    u_val, w_key = x[:, :, :d], x[:, :, d:]
    attn = _ldot(qn, kn, _BNT) * decay
    q_dec = qn * egc
    k_dec = kn * jnp.exp(glast - gcum)
    v_new = u_val - _ldot(w_key, S)
    out = _ldot(q_dec, S) + _ldot(attn, v_new)
    s_new = S * jnp.exp(glast) + _ldot(k_dec, v_new, _BTN)
    o = out * lax.rsqrt(jnp.mean(out * out, axis=-1, keepdims=True) + RMS_EPS) * nw
    o = o * jax.nn.silu(z)
    return s_new, o


def _heads_per_step(NH, HD, zcol0):
    for hb in (8, 4, 2):
        if NH % hb == 0 and zcol0 % (hb * HD) == 0:
            return hb
    return 1


def _gdn_fwd(name, qkv, p, zcol0, blt, aint, alog, dtb, nw, NH, HD):
    T = qkv.shape[0]
    N = T // CHUNK
    GW = NH * HD
    HB = _heads_per_step(NH, HD, zcol0)
    W = HB * HD
    zc0 = zcol0 // W
    nb = GW // W

    def body(q_ref, k_ref, v_ref, z_ref, bl_ref, ain_ref, al_ref, dtb_ref, nw_ref, o_ref, ssave_ref, s_scr):
        n = pl.program_id(1)

        @pl.when(n == 0)
        def _():
            s_scr[...] = jnp.zeros_like(s_scr)

        heads = lambda r: jnp.stack([r[:, hh * HD:(hh + 1) * HD] for hh in range(HB)], axis=0)
        s_in = s_scr[...]
        ssave_ref[...] = s_in
        s_new, o = _gdn_chunk(s_in, heads(q_ref), heads(k_ref), heads(v_ref), heads(z_ref), bl_ref[...], ain_ref[...],
                              al_ref[...], dtb_ref[...], nw_ref[...])
        s_scr[...] = s_new
        for hh in range(HB):
            o_ref[:, hh * HD:(hh + 1) * HD] = o[hh].astype(o_ref.dtype)

    ch = lambda off: pl.BlockSpec((CHUNK, W), lambda h, n: (n, h + off))
    sc = pl.BlockSpec((HB, CHUNK, 1), lambda h, n: (h, n, 0))
    hs = pl.BlockSpec((HB, 1, 1), lambda h, n: (h, 0, 0))
    return pl.pallas_call(
        body, grid=(NH // HB, N),
        in_specs=[ch(0), ch(nb), ch(2 * nb), ch(zc0), sc, sc, hs, hs, pl.BlockSpec((1, HD), lambda h, n: (0, 0))],
        out_specs=[pl.BlockSpec((CHUNK, W), lambda h, n: (n, h)),
                   pl.BlockSpec((HB, None, HD, HD), lambda h, n: (h, n, 0, 0))],
        out_shape=[SDS((T, GW), _MXU), SDS((NH, N, HD, HD), f32)],
        scratch_shapes=[pltpu.VMEM((HB, HD, HD), f32)], compiler_params=_cp(("parallel", "arbitrary")), name=name,
    )(qkv, qkv, qkv, p, blt, aint, alog, dtb, nw)


def _gdn_bwd(name, qkv, p, zcol0, blt, aint, alog, dtb, nw, ssave, do, NH, HD):
    T = qkv.shape[0]
    N = T // CHUNK
    GW = NH * HD
    HB = _heads_per_step(NH, HD, zcol0)
    W = HB * HD
    zc0 = zcol0 // W
    nb = GW // W

    def body(q_ref, k_ref, v_ref, z_ref, bl_ref, ain_ref, al_ref, dtb_ref, nw_ref, ss_ref, do_ref,
             dqkv_ref, dz_ref, dbl_ref, dain_ref, dal_ref, ddtb_ref, dnw_ref, ds_scr):
        h = pl.program_id(0)
        n = pl.program_id(1)

        @pl.when(n == 0)
        def _():
            ds_scr[...] = jnp.zeros_like(ds_scr)

        heads = lambda r: jnp.stack([r[:, hh * HD:(hh + 1) * HD] for hh in range(HB)], axis=0)
        _, vjp = jax.vjp(_gdn_chunk, ss_ref[...], heads(q_ref), heads(k_ref), heads(v_ref), heads(z_ref), bl_ref[...],
                         ain_ref[...], al_ref[...], dtb_ref[...], nw_ref[...])
        ds, dq, dk, dv, dz, dbl, dain, dal, ddtb, dnw = vjp((ds_scr[...], heads(do_ref).astype(f32)))
        ds_scr[...] = ds
        for hh in range(HB):
            cs = slice(hh * HD, (hh + 1) * HD)
            dqkv_ref[0, :, cs] = dq[hh]
            dqkv_ref[1, :, cs] = dk[hh]
            dqkv_ref[2, :, cs] = dv[hh]
            dz_ref[:, cs] = dz[hh].astype(dz_ref.dtype)
        dbl_ref[...] = dbl
        dain_ref[...] = dain

        @pl.when(n == 0)
        def _():
            dal_ref[...] = dal
            ddtb_ref[...] = ddtb

        @pl.when(n > 0)
        def _():
            dal_ref[...] += dal
            ddtb_ref[...] += ddtb

        @pl.when((n == 0) & (h == 0))
        def _():
            dnw_ref[...] = dnw

        @pl.when((n > 0) | (h > 0))
        def _():
            dnw_ref[...] += dnw

    R = N - 1
    ch = lambda off: pl.BlockSpec((CHUNK, W), lambda h, n: (R - n, h + off))
    sc = pl.BlockSpec((HB, CHUNK, 1), lambda h, n: (h, R - n, 0))
    hs = pl.BlockSpec((HB, 1, 1), lambda h, n: (h, 0, 0))
    nws = pl.BlockSpec((1, HD), lambda h, n: (0, 0))
    return pl.pallas_call(
        body, grid=(NH // HB, N),
        in_specs=[ch(0), ch(nb), ch(2 * nb), ch(zc0), sc, sc, hs, hs, nws,
                  pl.BlockSpec((HB, None, HD, HD), lambda h, n: (h, R - n, 0, 0)),
                  pl.BlockSpec((CHUNK, W), lambda h, n: (R - n, h))],
        out_specs=[pl.BlockSpec((3, CHUNK, W), lambda h, n: (0, R - n, h)),
                   pl.BlockSpec((CHUNK, W), lambda h, n: (R - n, h)), sc, sc, hs, hs, nws],
        out_shape=[SDS((3, T, GW), f32), SDS((T, GW), _MXU), SDS((NH, T, 1), f32), SDS((NH, T, 1), f32),
                   SDS((NH, 1, 1), f32), SDS((NH, 1, 1), f32), SDS((1, HD), f32)],
        scratch_shapes=[pltpu.VMEM((HB, HD, HD), f32)], compiler_params=_cp(("arbitrary", "arbitrary")), name=name,
    )(qkv, qkv, qkv, p, blt, aint, alog, dtb, nw, ssave, do)


def _loss_head(name, y, tgt):
    T, D = y.shape
    tm = _rtile(T, 256)

    def body(y_ref, t_ref, dy_ref, l_ref):
        i = pl.program_id(0)
        err = y_ref[...] - t_ref[...]
        dy_ref[...] = err * (1.0 / D)
        part = 0.5 * jnp.sum(jnp.sum(err * err, axis=-1, keepdims=True) * (1.0 / D), axis=0, keepdims=True)

        @pl.when(i == 0)
        def _():
            l_ref[...] = part

        @pl.when(i > 0)
        def _():
            l_ref[...] += part

    row = pl.BlockSpec((tm, D), lambda i: (i, 0))
    return pl.pallas_call(
        body, grid=(T // tm,), in_specs=[row, row], out_specs=[row, pl.BlockSpec((1, 1), lambda i: (0, 0))],
        out_shape=[SDS((T, D), f32), SDS((1, 1), f32)], compiler_params=_cp(("arbitrary",)), name=name,
    )(y, tgt)


def _adam_math(w, g, m, v):
    m = ADAM_B1 * m + (1.0 - ADAM_B1) * g
    v = ADAM_B2 * v + (1.0 - ADAM_B2) * jnp.square(g)
    m_hat = m / (1.0 - ADAM_B1 ** ADAM_STEP)
    v_hat = v / (1.0 - ADAM_B2 ** ADAM_STEP)
    delta = -ADAM_LR * (m_hat / (jnp.sqrt(v_hat) + ADAM_EPS) + ADAM_WD * w)
    return delta, m, v


def _add_mine(name, full, recv, me, out_dtype):
    N, _, R, C = full.shape
    tr = _rtile(R, max(16, (1 << 19) // max(C, 1) // 16 * 16))

    def body(me_ref, a_ref, b_ref, o_ref):
        o_ref[...] = (a_ref[...].astype(f32) + b_ref[...].astype(f32)).astype(o_ref.dtype)

    blk = pl.BlockSpec((None, tr, C), lambda n, i, me_ref: (n, i, 0))
    return pl.pallas_call(
        body,
        grid_spec=pltpu.PrefetchScalarGridSpec(
            num_scalar_prefetch=1, grid=(N, R // tr),
            in_specs=[pl.BlockSpec((None, None, tr, C), lambda n, i, me_ref: (n, me_ref[0], i, 0)), blk], out_specs=blk),
        out_shape=SDS((N, R, C), out_dtype), compiler_params=_cp(("parallel", "parallel")), name=name,
    )(me, full, recv)


def _adamw_big(name, full, recv, me, w, m, v, l, accs):
    _, R, C = full.shape
    L = w.shape[0]
    tr = _rtile(R, max(16, (1 << 18) // max(C, 1) // 16 * 16))

    def body(me_ref, ga_ref, gb_ref, w_ref, m_ref, v_ref, a0, a1, a2, a3, g_ref, d_ref, nm_ref, nv_ref):
        g = ga_ref[...].astype(f32) + gb_ref[...].astype(f32)
        d, nm, nv = _adam_math(w_ref[...], g, m_ref[...], v_ref[...])
        g_ref[...] = g
        d_ref[...] = d
        nm_ref[...] = nm
        nv_ref[...] = nv

    blk = pl.BlockSpec((tr, C), lambda i, me_ref: (i, 0))
    lblk = pl.BlockSpec((None, tr, C), lambda i, me_ref: (l, i, 0))
    untouched = pl.BlockSpec(memory_space=pl.ANY)
    return pl.pallas_call(
        body,
        grid_spec=pltpu.PrefetchScalarGridSpec(
            num_scalar_prefetch=1, grid=(R // tr,),
            in_specs=[pl.BlockSpec((None, tr, C), lambda i, me_ref: (me_ref[0], i, 0)), blk, lblk, lblk, lblk] + [untouched] * 4,
            out_specs=[lblk] * 4),
        out_shape=[SDS((L, R, C), f32)] * 4, input_output_aliases={6: 0, 7: 1, 8: 2, 9: 3},
        compiler_params=_cp(("parallel",)), name=name,
    )(me, full, recv, w, m, v, *accs)


def _adamw_small(name, gall, w, m, v):
    _, R, C = gall.shape
    tr = _rtile(R, 512)

    def body(ga_ref, w_ref, m_ref, v_ref, g_ref, d_ref, nm_ref, nv_ref):
        g = ga_ref[0]
        for s in range(1, 8):
            g = g + ga_ref[s]
        d, nm, nv = _adam_math(w_ref[...], g, m_ref[...], v_ref[...])
        g_ref[...] = g
        d_ref[...] = d
        nm_ref[...] = nm
        nv_ref[...] = nv

    blk = pl.BlockSpec((tr, C), lambda i: (i, 0))
    return pl.pallas_call(
        body, grid=(R // tr,), in_specs=[pl.BlockSpec((8, tr, C), lambda i: (0, i, 0)), blk, blk, blk], out_specs=[blk] * 4,
        out_shape=[SDS((R, C), f32)] * 4, compiler_params=_cp(("parallel",)), name=name,
    )(gall, w, m, v)


def _peer(axis):
    x, y, c = lax.axis_index("x"), lax.axis_index("y"), lax.axis_index("c")
    me = {"x": x, "y": y, "c": c}[axis]
    peer = {"x": (1 - x, y, c), "y": (x, 1 - y, c), "c": (x, y, 1 - c)}[axis]
    return me, peer


def _held(ref, done):
    idx = tuple(slice(None) if a in done else lax.axis_index(a) for a in ("x", "y", "c"))
    return ref.at[idx]


def _gather_stage(name, bufs, axes, dones):
    n = len(bufs)
    hbm = pl.BlockSpec(memory_space=pltpu.HBM)

    def body(*refs):
        outs = refs[n:2 * n]
        send_sems, recv_sems = refs[2 * n:]
        cps = []
        for t in range(n):
            _, peer = _peer(axes[t])
            blk = _held(outs[t], dones[t])
            cps.append(pltpu.make_async_remote_copy(src_ref=blk, dst_ref=blk, send_sem=send_sems.at[t],
                                                    recv_sem=recv_sems.at[t], device_id=peer, device_id_type=_MESH_T))
        for cp in cps:
            cp.start()
        for cp in cps:
            cp.wait()

    return pl.pallas_call(
        body, in_specs=[hbm] * n, out_specs=[hbm] * n, out_shape=[SDS(b.shape, b.dtype) for b in bufs],
        input_output_aliases={t: t for t in range(n)},
        scratch_shapes=[pltpu.SemaphoreType.DMA((n,)), pltpu.SemaphoreType.DMA((n,))], name=name,
    )(*bufs)


_HBM = pl.BlockSpec(memory_space=pltpu.HBM)
_SEM = pl.BlockSpec(memory_space=pltpu.SEMAPHORE)
_EFFECT = pltpu.SideEffectType.DATAFLOW_SIDE_EFFECTING


def _split_start(name, arrays, n_copies, make_copies):
    na = len(arrays)

    def body(*refs):
        ins = refs[:na]
        send_sems, recv_sems = refs[na], refs[na + 1]
        token = refs[2 * na + 2]
        for cp in make_copies(ins, send_sems, recv_sems):
            cp.start()
        token[...] = jnp.zeros_like(token)

    res = pl.pallas_call(
        body, name=name,
        out_shape=(pltpu.SemaphoreType.DMA((n_copies,)), pltpu.SemaphoreType.DMA((n_copies,)),
                   *[pltpu.HBM(a.shape, a.dtype) for a in arrays], SDS((8, 128), f32)),
        in_specs=[_HBM] * na, out_specs=(_SEM, _SEM, *[_HBM] * na, pl.BlockSpec(memory_space=pltpu.VMEM)),
        input_output_aliases={i: 2 + i for i in range(na)},
        compiler_params=pltpu.CompilerParams(has_side_effects=_EFFECT),
    )(*[pltpu.with_memory_space_constraint(a, pltpu.HBM) for a in arrays])
    return res[0], res[1], list(res[2:2 + na]), res[2 + na]


def _split_wait(name, arrays, send_sems, recv_sems, after, make_copies):
    na = len(arrays)
    afters = list(after) if isinstance(after, (list, tuple)) else [after]

    def body(*refs):
        ins = refs[:na]
        for cp in make_copies(ins, refs[na], refs[na + 1]):
            cp.wait_send()
            cp.wait_recv()

    res = pl.pallas_call(
        body, name=name, out_shape=tuple(pltpu.HBM(a.shape, a.dtype) for a in arrays),
        in_specs=[_HBM] * na + [_SEM, _SEM] + [pl.BlockSpec(memory_space=pl.ANY)] * len(afters),
        out_specs=tuple([_HBM] * na), input_output_aliases={i: i for i in range(na)},
        compiler_params=pltpu.CompilerParams(has_side_effects=_EFFECT),
    )(*arrays, send_sems, recv_sems, *afters)
    return list(res)


def _gather_copies(axes, dones):
    def make(refs, send_sems, recv_sems):
        cps = []
        for t in range(len(axes)):
            _, peer = _peer(axes[t])
            blk = _held(refs[t], dones[t])
            cps.append(pltpu.make_async_remote_copy(src_ref=blk, dst_ref=blk, send_sem=send_sems.at[t],
                                                    recv_sem=recv_sems.at[t], device_id=peer, device_id_type=_MESH_T))
        return cps
    return make


def _scatter_copies(axes):
    n = len(axes)

    def make(refs, send_sems, recv_sems):
        cps = []
        for t in range(n):
            me, peer = _peer(axes[t])
            cps.append(pltpu.make_async_remote_copy(
                src_ref=refs[t].at[:, 1 - me], dst_ref=refs[n + t], send_sem=send_sems.at[t], recv_sem=recv_sems.at[t],
                device_id=peer, device_id_type=_MESH_T))
        return cps
    return make


class _AsyncGather:
    def __init__(self, pfx, tensors, paths):
        x, y, c = (lax.axis_index(a) for a in ("x", "y", "c"))
        self.pfx, self.shapes = pfx, [tuple(t.shape) for t in tensors]
        self.bufs = [lax.dynamic_update_slice(lax.empty((2, 2, 2) + tuple(t.shape), t.dtype), t[None, None, None],
                                              (x, y, c) + (0,) * t.ndim) for t in tensors]
        self.orders = [tuple(p) + ("c",) for p in paths]
        self.ph = 0

    def _make(self):
        return _gather_copies([o[self.ph] for o in self.orders], [o[:self.ph] for o in self.orders])

    def start(self):
        self.ss, self.rs, self.bufs, tok = _split_start(f"{self.pfx}_start{self.ph}", self.bufs, len(self.bufs), self._make())
        return tok

    def wait(self, after):
        self.bufs = _split_wait(f"{self.pfx}_wait{self.ph}", self.bufs, self.ss, self.rs, after, self._make())
        self.ph += 1

    def result(self):
        return [b.reshape((8,) + s) for b, s in zip(self.bufs, self.shapes)]


class _AsyncReduceScatter:
    def __init__(self, pfx, tensors, paths):
        self.pfx = pfx
        self.rcs = [tuple(t.shape[1:]) for t in tensors]
        self.orders = [("c",) + tuple(p) for p in paths]
        self.left = [["x", "y", "c"] for _ in tensors]
        self.cur = list(tensors)
        self.ph = 0

    def start(self):
        n = len(self.cur)
        views = []
        for i, (t, rc) in enumerate(zip(self.cur, self.rcs)):
            pos = self.left[i].index(self.orders[i][self.ph])
            nb, na = 2 ** pos, 2 ** (len(self.left[i]) - pos - 1)
            views.append(t.reshape((nb, 2, na * rc[0], rc[1])))
        lands = [lax.empty((v.shape[0],) + tuple(v.shape[2:]), v.dtype) for v in views]
        self.make = _scatter_copies([o[self.ph] for o in self.orders])
        self.ss, self.rs, arrs, tok = _split_start(f"{self.pfx}_start{self.ph}", views + lands, n, self.make)
        self.arrs = arrs
        return tok

    def wait(self, after):
        n = len(self.cur)
        arrs = _split_wait(f"{self.pfx}_wait{self.ph}", self.arrs, self.ss, self.rs, after, self.make)
        views, recvs = arrs[:n], arrs[n:]
        ph = self.ph
        if ph == 2:
            self.out = [(v[0], r[0], o[2]) for v, r, o in zip(views, recvs, self.orders)]
        else:
            self.cur = [_add_mine(f"{self.pfx}_add{ph}_{i}", v, r, _coord(o[ph]), v.dtype)
                        for i, (v, r, o) in enumerate(zip(views, recvs, self.orders))]
            for i, o in enumerate(self.orders):
                self.left[i].remove(o[ph])
        self.ph += 1


def _coord(axis):
    return lax.axis_index(axis).astype(jnp.int32).reshape(1)


def _all_gather(pfx, tensors, paths):
    x, y, c = (lax.axis_index(a) for a in ("x", "y", "c"))
    bufs = []
    for t in tensors:
        zero = (0,) * t.ndim
        bufs.append(lax.dynamic_update_slice(lax.empty((2, 2, 2) + tuple(t.shape), t.dtype), t[None, None, None],
                                             (x, y, c) + zero))
    orders = [tuple(p) + ("c",) for p in paths]
    for ph in range(3):
        bufs = _gather_stage(f"{pfx}_{ph}", bufs, [o[ph] for o in orders], [o[:ph] for o in orders])
    return [b.reshape((8,) + tuple(t.shape)) for b, t in zip(bufs, tensors)]


def _mm_nn(name, a, w, out_dtype, tn_pref=1024):
    T, K = a.shape
    N = w.shape[1]
    tm, tn, tk = _rtile(T, 512), _tile(N, tn_pref), _tile(K, 2048)
    return _mm(
        name, a, w, _NN, (T // tm, N // tn, K // tk),
        pl.BlockSpec((tm, tk), lambda i, j, k: (i, k)), pl.BlockSpec((tk, tn), lambda i, j, k: (k, j)), (tm, tn),
        [], [], [SDS((T, N), out_dtype)], [pl.BlockSpec((tm, tn), lambda i, j, k: (i, j))], _store_epi,
    )[0]


def _mm_tn(name, a, b, out_dtype):
    T, M = a.shape
    N = b.shape[1]
    tm, tn, tk = _tile(M, 512), _tile(N, 2048), _rtile(T, 512)
    return _mm(
        name, a, b, _TN, (M // tm, N // tn, T // tk),
        pl.BlockSpec((tk, tm), lambda i, j, k: (k, i)), pl.BlockSpec((tk, tn), lambda i, j, k: (k, j)), (tm, tn),
        [], [], [SDS((M, N), out_dtype)], [pl.BlockSpec((tm, tn), lambda i, j, k: (i, j))], _store_epi,
    )[0]


def _mm_tn_slots(name, a, b, out_dtype):
    T, M = a.shape
    NS = b.shape[1] // 8
    tm, tk = _tile(M, 512), _rtile(T, 512)
    return _mm(
        name, a, b, _TN, (M // tm, 8, T // tk),
        pl.BlockSpec((tk, tm), lambda i, j, k: (k, i)), pl.BlockSpec((tk, NS), lambda i, j, k: (k, j)), (tm, NS),
        [], [], [SDS((8, M, NS), out_dtype)], [pl.BlockSpec((None, tm, NS), lambda i, j, k: (j, i, 0))], _store_epi,
    )[0]


def _mm_nt_slots(name, a, w8, out_dtype):
    T = a.shape[0]
    _, M, NS = w8.shape
    tm, tn = _rtile(T, 512), _tile(M, 1024)
    return _mm(
        name, a, w8, _NT, (T // tm, M // tn, 8),
        pl.BlockSpec((tm, NS), lambda i, j, k: (i, k)), pl.BlockSpec((None, tn, NS), lambda i, j, k: (k, j, 0)), (tm, tn),
        [], [], [SDS((T, M), out_dtype)], [pl.BlockSpec((tm, tn), lambda i, j, k: (i, j))], _store_epi,
    )[0]


def _colsum_kernel(name, fn, ins, in_cols, outs_elem, n_sum, C):
    T = ins[0].shape[0]
    tm = _rtile(T, 256)
    ne = len(outs_elem)

    def body(*refs):
        i = pl.program_id(0)
        iv = [r[...] for r in refs[:len(ins)]]
        res = fn(*iv)
        for o, r in zip(refs[len(ins):len(ins) + ne], res[:ne]):
            o[...] = r.astype(o.dtype)
        sums = [jnp.sum(r, axis=0, keepdims=True) for r in res[ne:]]

        @pl.when(i == 0)
        def _():
            for o, s in zip(refs[len(ins) + ne:], sums):
                o[...] = s

        @pl.when(i > 0)
        def _():
            for o, s in zip(refs[len(ins) + ne:], sums):
                o[...] += s

    in_specs = []
    for arr, off in zip(ins, in_cols):
        if off is None:
            in_specs.append(pl.BlockSpec((1, C), lambda i: (0, 0)))
        else:
            in_specs.append(pl.BlockSpec((tm, C), lambda i, off=off: (i, off)))
    row = pl.BlockSpec((tm, C), lambda i: (i, 0))
    vec = pl.BlockSpec((1, C), lambda i: (0, 0))
    return pl.pallas_call(
        body, grid=(T // tm,), in_specs=in_specs, out_specs=[row] * ne + [vec] * n_sum,
        out_shape=[SDS((T, C), dt) for dt in outs_elem] + [SDS((1, C), f32)] * n_sum,
        compiler_params=_cp(("arbitrary",)), name=name,
    )(*ins)


def _merge(gs, gg, a_s, a_g):
    return jax.nn.sigmoid(gs) * a_s + jax.nn.sigmoid(gg) * a_g


def _glu(yg, lp):
    return yg * jax.nn.sigmoid(lp)


_BIG = ("ffn1_w_gu", "ffn1_w_down", "w_in", "conv_w", "glu_w", "w_br_ssm", "w_br_gdn", "w_out", "ffn2_w_gu", "ffn2_w_down")
_PATHS = ("yx", "yx", "xy", "xy", "yx", "yx", "yx", "yx", "xy", "xy")
_SMALL = ("ln1_g", "ln1_b", "ssm_a_re", "ssm_a_im", "ssm_log_dt", "ssm_b_re", "ssm_b_im", "ssm_c_re", "ssm_c_im", "ssm_d",
          "glu_b", "gdn_a_log", "gdn_dt_bias", "gdn_norm_w", "ln2_g", "ln2_b", "ln3_g", "ln3_b")
_ORDER = ("ffn1_w_gu", "ffn1_w_down", "ln1_g", "ln1_b", "w_in", "conv_w", "ssm_a_re", "ssm_a_im", "ssm_log_dt", "ssm_b_re",
          "ssm_b_im", "ssm_c_re", "ssm_c_im", "ssm_d", "glu_w", "glu_b", "gdn_a_log", "gdn_dt_bias", "gdn_norm_w", "w_br_ssm",
          "w_br_gdn", "w_out", "ln2_g", "ln2_b", "ffn2_w_gu", "ffn2_w_down", "ln3_g", "ln3_b")


def _step(x, tgt, W, M, V):
    T, D = x.shape[1], x.shape[2]
    L = W["ffn1_w_gu"].shape[0]
    G, P = W["ssm_a_re"].shape[1:]
    H = W["ssm_b_re"].shape[3]
    SW = G * H
    NH = W["gdn_a_log"].shape[1]
    HD = W["gdn_norm_w"].shape[1]
    GW = NH * HD
    KC = W["conv_w"].shape[1]
    DS = D // 8
    alpha = (2.0 * L) ** 0.25
    o_b = SW + 4 * GW
    o_gs = o_b + 2 * NH
    IN = o_gs + 2 * D
    NM = IN - 2 * NH
    m_qkv, m_z, m_gs, m_gg = SW, SW + 3 * GW, SW + 4 * GW, SW + 4 * GW + D
    J = G // 8

    x0 = x[0]
    tg = tgt[0]

    def vec(name, l):
        return W[name][l:l + 1]

    saves, weights = [], []
    xc, xcb = x0, x0.astype(_MXU)
    def shards(l):
        return [W["ffn1_w_gu"][l].astype(_MXU), W["ffn1_w_down"][l].astype(_MXU), W["w_in"][l].astype(_MXU), W["conv_w"][l],
                W["glu_w"][l].astype(_MXU), W["w_br_ssm"][l].astype(_MXU), W["w_br_gdn"][l].astype(_MXU),
                W["w_out"][l].astype(_MXU), W["ffn2_w_gu"][l].astype(_MXU), W["ffn2_w_down"][l].astype(_MXU)]

    def dep(a, tok):
        return a if tok is None else a + tok[0:1, 0:1].astype(a.dtype)

    gathered = _all_gather("ag", shards(0), _PATHS)
    ahead = {}
    for l in range(L):
        toks = []
        for k in ([1, 2] if l == 0 else [l + 2]):
            if k < L:
                ahead[k] = _AsyncGather("agp", shards(k), _PATHS)
                toks.append(ahead[k].start())
        nxt = ahead.get(l + 1)
        if nxt and l >= 1:
            nxt.wait(xc)
            toks.append(nxt.start())
        tok = functools.reduce(lambda a, b: a + b, toks) if toks else None
        wgu1, wd1, win8, cw8, wglu, wbs, wbg, wo, wgu2, wd2 = gathered
        wd1 = wd1.reshape(-1, D)
        wd2 = wd2.reshape(-1, D)
        wglu = wglu.reshape(SW, SW)
        wo = wo.reshape(D, D)
        win = jnp.transpose(win8, (1, 0, 2)).reshape(D, IN)
        wmain = jnp.concatenate([win[:, :o_b], win[:, o_gs:]], axis=1)
        wba = jnp.pad(win[:, o_b:o_gs], ((0, 0), (0, 128 - 2 * NH)))
        cw = jnp.transpose(cw8, (1, 0, 2)).reshape(KC, 3 * GW)
        wl = dict(wgu1=wgu1, wd1=wd1, wmain=wmain, wba=wba, cw=cw, wglu=wglu, wbs=wbs, wbg=wbg, wo=wo, wgu2=wgu2, wd2=wd2)
        weights.append(wl)
        sv = {}

        gate, up, hh = _ffn_up("ffn_up", xcb, wgu1)
        x1, x1b, xh1, r1 = _mm_ln("ffn_down_ln", hh, wd1, xc, dep(vec("ln1_g", l), tok), vec("ln1_b", l), alpha, 0.5)
        sv["f1"] = dict(xb=xcb, gate=gate, up=up, h=hh, xhat=xh1, rstd=r1)

        p = _mm_nn("mix_in", x1b, wmain, f32)
        pba = _mm_nn("mix_in_ba", x1b, wba, f32)
        b_re_t = jnp.transpose(W["ssm_b_re"][l], (2, 0, 1))
        b_im_t = jnp.transpose(W["ssm_b_im"][l], (2, 0, 1))
        zoh_in = (W["ssm_a_re"][l], W["ssm_a_im"][l], W["ssm_log_dt"][l][:, None], b_re_t, b_im_t)
        lbr, lbi, bbr_t, bbi_t = _zoh_fwd("zoh", *zoh_in)
        bblk_r = _blockdiag(jnp.transpose(bbr_t, (1, 0, 2)))
        bblk_i = _blockdiag(jnp.transpose(bbi_t, (1, 0, 2)))
        cblkT_r = _blockdiag(W["ssm_c_re"][l])
        cblkT_in = _blockdiag(-W["ssm_c_im"][l])
        lbr_f, lbi_f = lbr.reshape(1, G * P), lbi.reshape(1, G * P)
        bur, bui = _bd2("s5_bu", p, 0, bblk_r, bblk_i)
        if nxt and l == 0:
            nxt.wait(bur)
            tok = nxt.start()
        sr, si = _s5_scan("s5_scan", bur, bui, dep(lbr_f, tok), lbi_f)
        dflat = W["ssm_d"][l].reshape(1, SW)

        def out_epi(acc, ex, outs):
            y_raw = acc + ex[1][...] * ex[0][...]
            yg = jax.nn.gelu(y_raw)
            outs[0][...] = y_raw
            outs[1][...] = yg
            outs[2][...] = yg.astype(outs[2].dtype)

        y_raw, yg, ygb = _bd_sum(
            "s5_out", sr, si, jnp.transpose(cblkT_r, (0, 2, 1)), jnp.transpose(cblkT_in, (0, 2, 1)), [p, dflat],
            lambda tm, nb: [pl.BlockSpec((tm, nb), lambda i, j: (i, j)), pl.BlockSpec((1, nb), lambda i, j: (0, j))],
            [SDS((T, SW), f32), SDS((T, SW), f32), SDS((T, SW), _MXU)], out_epi)

        tmg, tng, tkg = _rtile(T, 512), _tile(SW, 512), _tile(SW, 1024)

        def glu_epi(acc, ex, outs):
            lp = acc + ex[1][...]
            outs[0][...] = lp
            outs[1][...] = _glu(ex[0][...], lp).astype(outs[1].dtype)

        lp, ysb = _mm(
            "s5_glu", ygb, wglu, _NN, (T // tmg, SW // tng, SW // tkg),
            pl.BlockSpec((tmg, tkg), lambda i, j, k: (i, k)), pl.BlockSpec((tkg, tng), lambda i, j, k: (k, j)), (tmg, tng),
            [yg, vec("glu_b", l)], [pl.BlockSpec((tmg, tng), lambda i, j, k: (i, j)), pl.BlockSpec((1, tng), lambda i, j, k: (0, j))],
            [SDS((T, SW), f32), SDS((T, SW), _MXU)], [pl.BlockSpec((tmg, tng), lambda i, j, k: (i, j))] * 2, glu_epi)

        qkv = _conv_fwd("gdn_conv", p, m_qkv, cw, 3 * GW)
        blt = jnp.transpose(pba[:, :NH])[:, :, None]
        aint = jnp.transpose(pba[:, NH:2 * NH])[:, :, None]
        alog = W["gdn_a_log"][l].reshape(NH, 1, 1)
        dtb = W["gdn_dt_bias"][l].reshape(NH, 1, 1)
        nw = vec("gdn_norm_w", l)
        og, ssave = _gdn_fwd("gdn", qkv, p, m_z, blt, aint, alog, dtb, nw, NH, HD)

        a_s = _mm(
            "br_ssm", ysb, wbs, _NN, (T // tmg, 8, SW // tkg),
            pl.BlockSpec((tmg, tkg), lambda i, j, k: (i, k)), pl.BlockSpec((None, tkg, DS), lambda i, j, k: (j, k, 0)), (tmg, DS),
            [], [], [SDS((T, D), f32)], [pl.BlockSpec((tmg, DS), lambda i, j, k: (i, j))], _store_epi)[0]
        tkd = _tile(GW, 1024)
        gsb, ggb = m_gs // DS, m_gg // DS

        def merge_epi(acc, ex, outs):
            outs[0][...] = acc
            outs[1][...] = _merge(ex[1][...], ex[2][...], ex[0][...], acc).astype(outs[1].dtype)

        tile_ij = pl.BlockSpec((tmg, DS), lambda i, j, k: (i, j))
        a_g, merged = _mm(
            "br_gdn_merge", og, wbg, _NN, (T // tmg, 8, GW // tkd),
            pl.BlockSpec((tmg, tkd), lambda i, j, k: (i, k)), pl.BlockSpec((None, tkd, DS), lambda i, j, k: (j, k, 0)), (tmg, DS),
            [a_s, p, p], [tile_ij, pl.BlockSpec((tmg, DS), lambda i, j, k: (i, j + gsb)),
                          pl.BlockSpec((tmg, DS), lambda i, j, k: (i, j + ggb))],
            [SDS((T, D), f32), SDS((T, D), _MXU)], [tile_ij, tile_ij], merge_epi)
        if nxt:
            nxt.wait(merged)
            tok = nxt.start()
        x2, x2b, xh2, r2 = _mm_ln("mix_out_ln", merged, wo, x1, dep(vec("ln2_g", l), tok), vec("ln2_b", l), alpha, 1.0)
        sv["mx"] = dict(x1b=x1b, p=p, zoh_in=zoh_in, lbr_f=lbr_f, lbi_f=lbi_f, bblk_r=bblk_r, bblk_i=bblk_i, cblkT_r=cblkT_r,
                        cblkT_in=cblkT_in, sr=sr, si=si, dflat=dflat, y_raw=y_raw, yg=yg, ygb=ygb, lp=lp, ysb=ysb, qkv=qkv,
                        blt=blt, aint=aint, alog=alog, dtb=dtb, nw=nw, og=og, ssave=ssave, a_s=a_s, a_g=a_g, merged=merged,
                        xhat=xh2, rstd=r2)

        gate2, up2, hh2 = _ffn_up("ffn_up", x2b, wgu2)
        x3, x3b, xh3, r3 = _mm_ln("ffn_down_ln", hh2, wd2, x2, vec("ln3_g", l), vec("ln3_b", l), alpha, 0.5)
        sv["f2"] = dict(xb=x2b, gate=gate2, up=up2, h=hh2, xhat=xh3, rstd=r3)
        saves.append(sv)
        xc, xcb = x3, x3b
        if nxt:
            nxt.wait(x3)
            gathered = nxt.result()

    dy, loss_part = _loss_head("loss_head", xc, tg)
    loss = lax.psum(loss_part[0, 0], ("x", "y", "c"))

    big_out = {n: [lax.empty(W[n].shape, f32) for _ in range(4)] for n in _BIG}
    small_g = {n: [None] * L for n in _SMALL}
    pend = None
    for l in reversed(range(L)):
        sv, wl = saves[l], weights[l]
        mx = sv["mx"]
        p = mx["p"]
        tok = pend.start() if pend else None
        dx2, dwgu2, dwd2, dg3, db3 = _ffn_bwd("ffn_b", dy, sv["f2"], wl["wgu2"], wl["wd2"], dep(vec("ln3_g", l), tok), alpha)
        small_g["ln3_g"][l], small_g["ln3_b"][l] = dg3[0], db3[0]

        if pend:
            pend.wait(dx2)
            tok = pend.start()
        dz2, dmixb, dg2, db2 = _ln_bwd("mix_lnb", dx2, mx["xhat"], mx["rstd"], dep(vec("ln2_g", l), tok), 1.0)
        small_g["ln2_g"][l], small_g["ln2_b"][l] = dg2[0], db2[0]
        tmg, tkd = _rtile(T, 512), _tile(D, 512)
        tnq = 512 if (m_gs % 512 == 0 and D % 512 == 0) else DS
        tkq = _tile(D, 2048)
        gsb, ggb = m_gs // tnq, m_gg // tnq

        def dmerge_epi(acc, ex, outs):
            _, vjp = jax.vjp(_merge, ex[0][...], ex[1][...], ex[2][...], ex[3][...])
            dgs, dgg, das, dag = vjp(acc)
            outs[0][...] = das.astype(outs[0].dtype)
            outs[1][...] = dag.astype(outs[1].dtype)
            outs[2][...] = dgs.astype(outs[2].dtype)
            outs[3][...] = dgg.astype(outs[3].dtype)

        tile_ij = pl.BlockSpec((tmg, tnq), lambda i, j, k: (i, j))
        das, dag, dgs, dgg = _mm(
            "mix_dmerge", dmixb, wl["wo"], _NT, (T // tmg, D // tnq, D // tkq),
            pl.BlockSpec((tmg, tkq), lambda i, j, k: (i, k)), pl.BlockSpec((tnq, tkq), lambda i, j, k: (j, k)), (tmg, tnq),
            [p, p, mx["a_s"], mx["a_g"]],
            [pl.BlockSpec((tmg, tnq), lambda i, j, k: (i, j + gsb)), pl.BlockSpec((tmg, tnq), lambda i, j, k: (i, j + ggb)),
             tile_ij, tile_ij],
            [SDS((T, D), _MXU)] * 4, [tile_ij] * 4, dmerge_epi)
        dwo = _mm_tn("mix_dwo", mx["merged"], dmixb, _GDT)
        dys = _mm_nt_slots("br_ssm_dx", das, wl["wbs"], f32)
        dog = _mm_nt_slots("br_gdn_dx", dag, wl["wbg"], f32)
        dwbs = _mm_tn_slots("br_ssm_dw", mx["ysb"], das, _GDT)
        dwbg = _mm_tn_slots("br_gdn_dw", mx["og"], dag, _GDT)

        def glu_b_fn(dys_t, yg_t, lp_t):
            _, vjp = jax.vjp(_glu, yg_t, lp_t)
            dyg1, dlp = vjp(dys_t)
            return dyg1, dlp, dlp

        dyg1, dlpb, dglub = _colsum_kernel("s5_glu_b", glu_b_fn, [dys, mx["yg"], mx["lp"]], [0, 0, 0], [f32, _MXU], 1, SW)
        small_g["glu_b"][l] = dglub[0]
        dwglu = _mm_tn("s5_dwglu", mx["ygb"], dlpb, _GDT)
        tng, tkg = _tile(SW, 512), _tile(SW, 512)

        def dyraw_epi(acc, ex, outs):
            _, vjp = jax.vjp(jax.nn.gelu, ex[1][...])
            (d,) = vjp(ex[0][...] + acc)
            outs[0][...] = d

        t_ij = pl.BlockSpec((tmg, tng), lambda i, j, k: (i, j))
        (dyraw,) = _mm(
            "s5_dyraw", dlpb, wl["wglu"], _NT, (T // tmg, SW // tng, SW // tkg),
            pl.BlockSpec((tmg, tkg), lambda i, j, k: (i, k)), pl.BlockSpec((tng, tkg), lambda i, j, k: (j, k)), (tmg, tng),
            [dyg1, mx["y_raw"]], [t_ij, t_ij], [SDS((T, SW), f32)], [t_ij], dyraw_epi)

        def dd_fn(dyr, u_t, d_t):
            return d_t * dyr, dyr * u_t

        dud, dd = _colsum_kernel("s5_dd", dd_fn, [dyraw, p, mx["dflat"]], [0, 0, None], [f32], 1, SW)
        small_g["ssm_d"][l] = dd.reshape(G, H)
        dsr, dsi = _bd2("s5_ds", dyraw, 0, mx["cblkT_r"], mx["cblkT_in"])
        dcb_r, dcb_i = _bdT2("s5_dc", mx["sr"], mx["si"], 0, dyraw, dyraw, 0, 8 * P, 8 * H, J)
        small_g["ssm_c_re"][l] = _blockdiag_extract(jnp.transpose(dcb_r, (0, 2, 1)), H, P)
        small_g["ssm_c_im"][l] = -_blockdiag_extract(jnp.transpose(dcb_i, (0, 2, 1)), H, P)
        ar, ai, dlr, dli = _s5_scan_bwd("s5_scan_b", dsr, dsi, mx["sr"], mx["si"], mx["lbr_f"], mx["lbi_f"])

        def du_epi(acc, ex, outs):
            outs[0][...] = (acc + ex[0][...]).astype(outs[0].dtype)

        (du,) = _bd_sum(
            "s5_du", ar, ai, jnp.transpose(mx["bblk_r"], (0, 2, 1)), jnp.transpose(mx["bblk_i"], (0, 2, 1)), [dud],
            lambda tm, nb: [pl.BlockSpec((tm, nb), lambda i, j: (i, j))], [SDS((T, SW), _MXU)], du_epi)
        dbb_r, dbb_i = _bdT2("s5_db", p, p, 0, ar, ai, 0, 8 * H, 8 * P, J)
        dbbr_t = jnp.transpose(_blockdiag_extract(dbb_r, H, P), (1, 0, 2))
        dbbi_t = jnp.transpose(_blockdiag_extract(dbb_i, H, P), (1, 0, 2))
        da_re, da_im, dlog_dt, dbre_t, dbim_t = _zoh_bwd("zoh_b", *mx["zoh_in"], dlr.reshape(G, P), dli.reshape(G, P),
                                                         dbbr_t, dbbi_t)
        small_g["ssm_a_re"][l], small_g["ssm_a_im"][l], small_g["ssm_log_dt"][l] = da_re, da_im, dlog_dt[:, 0]
        small_g["ssm_b_re"][l] = jnp.transpose(dbre_t, (1, 2, 0))
        small_g["ssm_b_im"][l] = jnp.transpose(dbim_t, (1, 2, 0))

        dqkv3, dzb, dbl, dain, dal, ddtb, dnw = _gdn_bwd("gdn_b", mx["qkv"], p, m_z, mx["blt"], mx["aint"], mx["alog"],
                                                         mx["dtb"], mx["nw"], mx["ssave"], dog, NH, HD)
        small_g["gdn_a_log"][l], small_g["gdn_dt_bias"][l], small_g["gdn_norm_w"][l] = dal[:, 0, 0], ddtb[:, 0, 0], dnw[0]
        dqkv_pre, dcw = _conv_bwd("gdn_conv_b", p, m_qkv, wl["cw"], dqkv3)
        if pend:
            pend.wait(dqkv_pre)
            tok = pend.start()

        dpm = jnp.concatenate([du, dqkv_pre, dzb, dgs, dgg], axis=1)
        dpba = dep(jnp.concatenate([jnp.transpose(dbl[:, :, 0]), jnp.transpose(dain[:, :, 0]),
                                    jnp.zeros((T, 128 - 2 * NH), f32)], axis=1), tok).astype(_MXU)
        tnd, tkm = _tile(D, 1024), _tile(NM, 2304)
        t_ba = _mm(
            "mix_dx_ba", dpba, wl["wba"], _NT, (T // tmg, D // tnd, 1),
            pl.BlockSpec((tmg, 128), lambda i, j, k: (i, 0)), pl.BlockSpec((tnd, 128), lambda i, j, k: (j, 0)), (tmg, tnd),
            [], [], [SDS((T, D), f32)], [pl.BlockSpec((tmg, tnd), lambda i, j, k: (i, j))], _store_epi)[0]

        def dx1_epi(acc, ex, outs):
            outs[0][...] = alpha * ex[0][...] + ex[1][...] + acc

        t_d = pl.BlockSpec((tmg, tnd), lambda i, j, k: (i, j))
        (dx1,) = _mm(
            "mix_dx", dpm, wl["wmain"], _NT, (T // tmg, D // tnd, NM // tkm),
            pl.BlockSpec((tmg, tkm), lambda i, j, k: (i, k)), pl.BlockSpec((tnd, tkm), lambda i, j, k: (j, k)), (tmg, tnd),
            [dz2, t_ba], [t_d, t_d], [SDS((T, D), f32)], [t_d], dx1_epi)
        tnm = _tile(NM, 1024)
        tkt = _tile(T, 2048)
        dwmain = _mm(
            "mix_dw", jnp.transpose(mx["x1b"]), dpm, _NN, (D // tkd, NM // tnm, T // tkt),
            pl.BlockSpec((tkd, tkt), lambda i, j, k: (i, k)), pl.BlockSpec((tkt, tnm), lambda i, j, k: (k, j)), (tkd, tnm),
            [], [], [SDS((D, NM), _GDT)], [pl.BlockSpec((tkd, tnm), lambda i, j, k: (i, j))], _store_epi)[0]
        dwba = _mm_tn("mix_dw_ba", mx["x1b"], dpba, _GDT)
        dwin = jnp.concatenate([dwmain[:, :o_b], dwba[:, :2 * NH], dwmain[:, o_b:]], axis=1)
        dwin8 = jnp.transpose(dwin.reshape(D, 8, IN // 8), (1, 0, 2))
        dcw8 = jnp.transpose(dcw.reshape(KC, 8, 3 * GW // 8), (1, 0, 2))

        dx0, dwgu1, dwd1, dg1, db1 = _ffn_bwd("ffn_b", dx1, sv["f1"], wl["wgu1"], wl["wd1"], vec("ln1_g", l), alpha)
        small_g["ln1_g"][l], small_g["ln1_b"][l] = dg1[0], db1[0]
        dy = dx0

        parts = [dwgu1, dwd1.reshape(8, -1, D), dwin8, dcw8, dwglu.reshape(8, SW // 8, SW), dwbs, dwbg,
                 dwo.reshape(8, DS, D), dwgu2, dwd2.reshape(8, -1, D)]
        late = None
        if pend:
            pend.wait(dx0)
            if l > 0:
                for n, (full, recv, last) in zip(_BIG, pend.out):
                    big_out[n] = _adamw_big("adamw_" + n, full, recv, _coord(last), W[n], M[n], V[n], l + 1, big_out[n])
            else:
                late = pend.out
        pend = _AsyncReduceScatter("rsp", parts, _PATHS)

    seg = 8 * 128

    def padded(n):
        return -(-n // seg) * seg

    def pack(arrs):
        flat = jnp.concatenate([jnp.pad(a.reshape(-1), (0, padded(a.size) - a.size)) for a in arrs])
        n = flat.shape[0]
        rows = -(-n // (128 * 512)) * 512
        return jnp.pad(flat, (0, rows * 128 - n)).reshape(rows, 128)

    tok = pend.start()
    gs_full = [jnp.stack(small_g[n]).reshape(W[n].shape) for n in _SMALL]
    gpack = dep(pack(gs_full), tok)
    (gall,) = _all_gather("ag_small", [gpack], ["yx"])
    pend.wait(gall)
    tok = pend.start()
    if late:
        for n, (full, recv, last) in zip(_BIG, late):
            big_out[n] = _adamw_big("adamw_" + n, full, dep(recv, tok), _coord(last), W[n], M[n], V[n], 1, big_out[n])
    pend.wait([gall] + ([big_out[n][0] for n in _BIG] if late else []))
    tok = pend.start()
    sg, sd, sm, sv_ = _adamw_small("adamw_small", gall, dep(pack([W[n] for n in _SMALL]), tok),
                                   pack([M[n] for n in _SMALL]), pack([V[n] for n in _SMALL]))
    pend.wait(sv_)
    for n, (full, recv, last) in zip(_BIG, pend.out):
        big_out[n] = _adamw_big("adamw_" + n, full, recv, _coord(last), W[n], M[n], V[n], 0, big_out[n])

    def unpack(packed):
        out, row = {}, 0
        for n in _SMALL:
            sz = math.prod(W[n].shape)
            rows = padded(sz) // 128
            out[n] = packed[row:row + rows].reshape(-1)[:sz].reshape(W[n].shape)
            row += rows
        return out

    res = [unpack(a) for a in (sg, sd, sm, sv_)]
    for n in _BIG:
        for i in range(4):
            res[i][n] = big_out[n][i]
    outs = [loss, dy[None]]
    for i in range(4):
        outs += [res[i][n] for n in _ORDER]
    return tuple(outs)


def kernel(x, ffn1_w_gu, ffn1_w_down, ln1_g, ln1_b, w_in, conv_w, ssm_a_re, ssm_a_im, ssm_log_dt, ssm_b_re, ssm_b_im, ssm_c_re, ssm_c_im, ssm_d, glu_w, glu_b, gdn_a_log, gdn_dt_bias, gdn_norm_w, w_br_ssm, w_br_gdn, w_out, ln2_g, ln2_b, ffn2_w_gu, ffn2_w_down, ln3_g, ln3_b, loss_target, m_ffn1_w_gu, m_ffn1_w_down, m_ln1_g, m_ln1_b, m_w_in, m_conv_w, m_ssm_a_re, m_ssm_a_im, m_ssm_log_dt, m_ssm_b_re, m_ssm_b_im, m_ssm_c_re, m_ssm_c_im, m_ssm_d, m_glu_w, m_glu_b, m_gdn_a_log, m_gdn_dt_bias, m_gdn_norm_w, m_w_br_ssm, m_w_br_gdn, m_w_out, m_ln2_g, m_ln2_b, m_ffn2_w_gu, m_ffn2_w_down, m_ln3_g, m_ln3_b, v_ffn1_w_gu, v_ffn1_w_down, v_ln1_g, v_ln1_b, v_w_in, v_conv_w, v_ssm_a_re, v_ssm_a_im, v_ssm_log_dt, v_ssm_b_re, v_ssm_b_im, v_ssm_c_re, v_ssm_c_im, v_ssm_d, v_glu_w, v_glu_b, v_gdn_a_log, v_gdn_dt_bias, v_gdn_norm_w, v_w_br_ssm, v_w_br_gdn, v_w_out, v_ln2_g, v_ln2_b, v_ffn2_w_gu, v_ffn2_w_down, v_ln3_g, v_ln3_b):
    given = dict(locals())
    W = {n: given[n] for n in _ORDER}
    M = {n: given["m_" + n] for n in _ORDER}
    V = {n: given["v_" + n] for n in _ORDER}
    return _step(x, loss_target, W, M, V)
```

```python
import functools
import math

import jax
import jax.numpy as jnp
from jax import lax
from jax.experimental import pallas as pl
from jax.experimental.pallas import tpu as pltpu

f32 = jnp.float32
_MXU = jnp.bfloat16
_GDT = jnp.bfloat16
_HP = lax.Precision.HIGHEST
_VMEM_LIMIT = 56 * 1024 * 1024
_MESH_T = pl.DeviceIdType.MESH

LN_EPS = 1e-5
RMS_EPS = 1e-6
L2_EPS = 1e-6
CHUNK = 64
ADAM_LR = 0.001
ADAM_B1 = 0.9
ADAM_B2 = 0.999
ADAM_EPS = 1e-08
ADAM_WD = 0.01
ADAM_STEP = 10

_NN = (((1,), (0,)), ((), ()))
_NT = (((1,), (1,)), ((), ()))
_TN = (((0,), (0,)), ((), ()))

SDS = jax.ShapeDtypeStruct


def _cp(sem):
    return pltpu.CompilerParams(dimension_semantics=sem, vmem_limit_bytes=_VMEM_LIMIT)


def _tile(n, pref):
    if n <= pref:
        return n
    t = (pref // 128) * 128
    while t >= 128:
        if n % t == 0:
            return t
        t -= 128
    return n


def _rtile(n, pref):
    if n <= pref:
        return n
    t = (pref // 16) * 16
    while t >= 16:
        if n % t == 0:
            return t
        t -= 16
    return n


def _mm(name, a, b, dims, grid, a_spec, b_spec, acc_shape, extras, extra_specs, out_shape, out_specs, epilogue):
    nk = grid[2]
    ne = len(extras)
    no = len(out_shape)

    def body(*refs):
        a_ref, b_ref = refs[0], refs[1]
        ex = refs[2:2 + ne]
        outs = refs[2 + ne:2 + ne + no]
        acc = refs[-1]
        k = pl.program_id(2)
        part = lax.dot_general(a_ref[...].astype(_MXU), b_ref[...].astype(_MXU), dims, preferred_element_type=f32)

        @pl.when(k == 0)
        def _():
            acc[...] = part

        @pl.when(k > 0)
        def _():
            acc[...] += part

        @pl.when(k == nk - 1)
        def _():
            epilogue(acc[...], ex, outs)

    return pl.pallas_call(
        body, grid=grid, in_specs=[a_spec, b_spec, *extra_specs], out_specs=list(out_specs), out_shape=list(out_shape),
        scratch_shapes=[pltpu.VMEM(acc_shape, f32)], compiler_params=_cp(("parallel", "parallel", "arbitrary")), name=name,
    )(a, b, *extras)


def _store_epi(acc, ex, outs):
    for o in outs:
        o[...] = acc.astype(o.dtype)


def _ln_epilogue(alpha, c):
    def epi(acc, ex, outs):
        x_ref, g_ref, b_ref = ex
        y_ref, yb_ref, xh_ref, r_ref = outs
        z = alpha * x_ref[...] + c * acc
        mu = jnp.mean(z, axis=-1, keepdims=True)
        zc = z - mu
        var = jnp.mean(zc * zc, axis=-1, keepdims=True)
        r = lax.rsqrt(var + LN_EPS)
        xh = zc * r
        y = xh * g_ref[...] + b_ref[...]
        y_ref[...] = y
        yb_ref[...] = y.astype(yb_ref.dtype)
        xh_ref[...] = xh
        r_ref[...] = r
    return epi


def _mm_ln(name, a, w, x, g, b, alpha, c):
    T, K = a.shape
    D = w.shape[1]
    tm, tk = _rtile(T, 512), _tile(K, 512)
    row = pl.BlockSpec((tm, D), lambda i, j, k: (i, 0))
    vec = pl.BlockSpec((1, D), lambda i, j, k: (0, 0))
    return _mm(
        name, a, w, _NN, (T // tm, 1, K // tk),
        pl.BlockSpec((tm, tk), lambda i, j, k: (i, k)), pl.BlockSpec((tk, D), lambda i, j, k: (k, 0)), (tm, D),
        [x, g, b], [row, vec, vec],
        [SDS((T, D), f32), SDS((T, D), _MXU), SDS((T, D), f32), SDS((T, 1), f32)],
        [row, row, row, pl.BlockSpec((tm, 1), lambda i, j, k: (i, 0))],
        _ln_epilogue(alpha, c),
    )


def _ln_bwd(name, dy, xhat, rstd, g, c):
    T, D = dy.shape
    tm = _rtile(T, 256)

    def body(dy_ref, xh_ref, r_ref, g_ref, dz_ref, df_ref, dg_ref, db_ref):
        i = pl.program_id(0)
        dyv = dy_ref[...]
        xh = xh_ref[...]
        dxh = dyv * g_ref[...]
        m1 = jnp.mean(dxh, axis=-1, keepdims=True)
        m2 = jnp.mean(dxh * xh, axis=-1, keepdims=True)
        dz = r_ref[...] * (dxh - m1 - xh * m2)
        dz_ref[...] = dz
        df_ref[...] = (c * dz).astype(df_ref.dtype)
        pg = jnp.sum(dyv * xh, axis=0, keepdims=True)
        pb = jnp.sum(dyv, axis=0, keepdims=True)

        @pl.when(i == 0)
        def _():
            dg_ref[...] = pg
            db_ref[...] = pb

        @pl.when(i > 0)
        def _():
            dg_ref[...] += pg
            db_ref[...] += pb

    row = pl.BlockSpec((tm, D), lambda i: (i, 0))
    vec = pl.BlockSpec((1, D), lambda i: (0, 0))
    return pl.pallas_call(
        body, grid=(T // tm,), in_specs=[row, row, pl.BlockSpec((tm, 1), lambda i: (i, 0)), vec],
        out_specs=[row, row, vec, vec],
        out_shape=[SDS((T, D), f32), SDS((T, D), _MXU), SDS((1, D), f32), SDS((1, D), f32)],
        compiler_params=_cp(("arbitrary",)), name=name,
    )(dy, xhat, rstd, g)


def _swiglu(g, u):
    return jax.nn.silu(g) * u


def _ffn_up(name, xb, wgu):
    T, D = xb.shape
    FS = wgu.shape[2]
    F = 4 * FS
    tm = _rtile(T, 256)

    def body(x_ref, wg_ref, wu_ref, g_ref, u_ref, h_ref):
        xv = x_ref[...]
        g = jnp.dot(xv, wg_ref[...], preferred_element_type=f32)
        u = jnp.dot(xv, wu_ref[...], preferred_element_type=f32)
        g_ref[...] = g.astype(g_ref.dtype)
        u_ref[...] = u.astype(u_ref.dtype)
        h_ref[...] = _swiglu(g, u).astype(h_ref.dtype)

    out = pl.BlockSpec((tm, FS), lambda j, i: (i, j))
    return pl.pallas_call(
        body, grid=(4, T // tm),
        in_specs=[pl.BlockSpec((tm, D), lambda j, i: (i, 0)),
                  pl.BlockSpec((None, D, FS), lambda j, i: (j, 0, 0)),
                  pl.BlockSpec((None, D, FS), lambda j, i: (j + 4, 0, 0))],
        out_specs=[out, out, out],
        out_shape=[SDS((T, F), _MXU), SDS((T, F), _MXU), SDS((T, F), _MXU)],
        compiler_params=_cp(("parallel", "arbitrary")), name=name,
    )(xb, wgu, wgu)


def _ffn_bwd(pfx, dy, sv, wgu, wd, g_ln, alpha):
    T, D = dy.shape
    FS = wgu.shape[2]
    F = 4 * FS
    dz, dfb, dg, db = _ln_bwd(pfx + "_lnb", dy, sv["xhat"], sv["rstd"], g_ln, 0.5)

    tm, tn, tk = _rtile(T, 1024), _tile(F, 512), _tile(D, 2048)

    def epi(acc, ex, outs):
        g_ref, u_ref = ex
        _, vjp = jax.vjp(_swiglu, g_ref[...].astype(f32), u_ref[...].astype(f32))
        dgate, dup = vjp(acc)
        outs[0][0] = dgate.astype(outs[0].dtype)
        outs[0][1] = dup.astype(outs[0].dtype)

    gu = pl.BlockSpec((tm, tn), lambda i, j, k: (i, j))
    (dgu,) = _mm(
        pfx + "_dh", dfb, wd, _NT, (T // tm, F // tn, D // tk),
        pl.BlockSpec((tm, tk), lambda i, j, k: (i, k)), pl.BlockSpec((tn, tk), lambda i, j, k: (j, k)), (tm, tn),
        [sv["gate"], sv["up"]], [gu, gu],
        [SDS((2, T, F), _MXU)], [pl.BlockSpec((2, tm, tn), lambda i, j, k: (0, i, j))], epi,
    )

    tm2, tk2 = _tile(F, 512), _rtile(T, 512)
    (dwd,) = _mm(
        pfx + "_dwd", sv["h"], dfb, _TN, (F // tm2, 1, T // tk2),
        pl.BlockSpec((tk2, tm2), lambda i, j, k: (k, i)), pl.BlockSpec((tk2, D), lambda i, j, k: (k, 0)), (tm2, D),
        [], [], [SDS((F, D), _GDT)], [pl.BlockSpec((tm2, D), lambda i, j, k: (i, 0))], _store_epi,
    )

    tn3 = _tile(D, 1024)

    def epi3(acc, ex, outs):
        outs[0][...] = alpha * ex[0][...] + acc

    (dx,) = _mm(
        pfx + "_dx", dgu, wgu, _NT, (T // tm, D // tn3, 8),
        pl.BlockSpec((None, tm, FS), lambda i, j, k: (k // 4, i, k % 4)),
        pl.BlockSpec((None, tn3, FS), lambda i, j, k: (k, j, 0)), (tm, tn3),
        [dz], [pl.BlockSpec((tm, tn3), lambda i, j, k: (i, j))],
        [SDS((T, D), f32)], [pl.BlockSpec((tm, tn3), lambda i, j, k: (i, j))], epi3,
    )

    tm4, tk4 = _rtile(D, 512), _tile(T, 2048)
    (dwgu,) = _mm(
        pfx + "_dwgu", jnp.transpose(sv["xb"]), dgu, _NN, (D // tm4, 8, T // tk4),
        pl.BlockSpec((tm4, tk4), lambda i, j, k: (i, k)),
        pl.BlockSpec((None, tk4, FS), lambda i, j, k: (j // 4, k, j % 4)), (tm4, FS),
        [], [], [SDS((8, D, FS), _GDT)], [pl.BlockSpec((None, tm4, FS), lambda i, j, k: (j, i, 0))], _store_epi,
    )
    return dx, dwgu, dwd, dg, db


def _zoh(a_re, a_im, log_dt, b_re_t, b_im_t):
    dt = jnp.exp(log_dt)
    mag = jnp.exp(a_re * dt)
    lr_, li_ = mag * jnp.cos(a_im * dt), mag * jnp.sin(a_im * dt)
    den = a_re * a_re + a_im * a_im
    pr, pi = lr_ - 1.0, li_
    qr, qi = a_re / den, -a_im / den
    zr, zi = pr * qr - pi * qi, pr * qi + pi * qr
    bbr = zr[None] * b_re_t - zi[None] * b_im_t
    bbi = zr[None] * b_im_t + zi[None] * b_re_t
    return lr_, li_, bbr, bbi


def _zoh_fwd(name, a_re, a_im, log_dt, b_re_t, b_im_t):
    G, P = a_re.shape
    H = b_re_t.shape[0]

    def body(ar, ai, ld, br, bi, o1, o2, o3, o4):
        r = _zoh(ar[...], ai[...], ld[...], br[...], bi[...])
        o1[...], o2[...], o3[...], o4[...] = r

    return pl.pallas_call(
        body, out_shape=[SDS((G, P), f32), SDS((G, P), f32), SDS((H, G, P), f32), SDS((H, G, P), f32)], name=name,
    )(a_re, a_im, log_dt, b_re_t, b_im_t)


def _zoh_bwd(name, a_re, a_im, log_dt, b_re_t, b_im_t, dlr, dli, dbbr, dbbi):
    G, P = a_re.shape
    H = b_re_t.shape[0]

    def body(ar, ai, ld, br, bi, g1, g2, g3, g4, o1, o2, o3, o4, o5):
        _, vjp = jax.vjp(_zoh, ar[...], ai[...], ld[...], br[...], bi[...])
        r = vjp((g1[...], g2[...], g3[...], g4[...]))
        o1[...], o2[...], o3[...], o4[...], o5[...] = r

    return pl.pallas_call(
        body, out_shape=[SDS((G, P), f32), SDS((G, P), f32), SDS((G, 1), f32), SDS((H, G, P), f32), SDS((H, G, P), f32)],
        name=name,
    )(a_re, a_im, log_dt, b_re_t, b_im_t, dlr, dli, dbbr, dbbi)


def _blockdiag(m):
    G, A, B = m.shape
    eye = jnp.eye(8, dtype=bool)
    m4 = m.reshape(G // 8, 8, A, B)
    out = jnp.where(eye[None, :, None, :, None], m4[:, :, :, None, :], jnp.zeros((), m.dtype))
    return out.reshape(G // 8, 8 * A, 8 * B)


def _blockdiag_extract(mb, A, B):
    J = mb.shape[0]
    m5 = mb.reshape(J, 8, A, 8, B)
    d = jnp.stack([m5[:, i, :, i, :] for i in range(8)], axis=1)
    return d.reshape(J * 8, A, B)


def _bd2(name, a, a_col0, b1, b2, out_dtype=f32):
    T = a.shape[0]
    J, KA, NB = b1.shape
    tm = _rtile(T, 512)

    def body(a_ref, b1_ref, b2_ref, o1, o2):
        av = a_ref[...].astype(_MXU)
        o1[...] = jnp.dot(av, b1_ref[...].astype(_MXU), preferred_element_type=f32).astype(o1.dtype)
        o2[...] = jnp.dot(av, b2_ref[...].astype(_MXU), preferred_element_type=f32).astype(o2.dtype)

    bs = pl.BlockSpec((None, KA, NB), lambda i, j: (j, 0, 0))
    os_ = pl.BlockSpec((tm, NB), lambda i, j: (i, j))
    return pl.pallas_call(
        body, grid=(T // tm, J), in_specs=[pl.BlockSpec((tm, KA), lambda i, j: (i, j + a_col0)), bs, bs],
        out_specs=[os_, os_], out_shape=[SDS((T, J * NB), out_dtype)] * 2,
        compiler_params=_cp(("parallel", "parallel")), name=name,
    )(a, b1, b2)


def _bd_sum(name, a1, a2, b1, b2, extras, extra_specs_fn, out_shape, epilogue):
    T = a1.shape[0]
    J, KA, NB = b1.shape
    tm = _rtile(T, 512)
    ne = len(extras)

    def body(*refs):
        a1_ref, a2_ref, b1_ref, b2_ref = refs[:4]
        ex = refs[4:4 + ne]
        outs = refs[4 + ne:]
        acc = jnp.dot(a1_ref[...].astype(_MXU), b1_ref[...].astype(_MXU), preferred_element_type=f32)
        acc = acc + jnp.dot(a2_ref[...].astype(_MXU), b2_ref[...].astype(_MXU), preferred_element_type=f32)
        epilogue(acc, ex, outs)

    as_ = pl.BlockSpec((tm, KA), lambda i, j: (i, j))
    bs = pl.BlockSpec((None, KA, NB), lambda i, j: (j, 0, 0))
    os_ = pl.BlockSpec((tm, NB), lambda i, j: (i, j))
    return pl.pallas_call(
        body, grid=(T // tm, J), in_specs=[as_, as_, bs, bs, *extra_specs_fn(tm, NB)],
        out_specs=[os_] * len(out_shape), out_shape=list(out_shape),
        compiler_params=_cp(("parallel", "parallel")), name=name,
    )(a1, a2, b1, b2, *extras)


def _bdT2(name, a1, a2, a_col0, b1, b2, b_col0, KA, NB, J):
    T = a1.shape[0]
    tk = _rtile(T, 512)

    def body(a1_ref, a2_ref, b1_ref, b2_ref, o1, o2):
        k = pl.program_id(1)
        p1 = lax.dot_general(a1_ref[...].astype(_MXU), b1_ref[...].astype(_MXU), _TN, preferred_element_type=f32)
        p2 = lax.dot_general(a2_ref[...].astype(_MXU), b2_ref[...].astype(_MXU), _TN, preferred_element_type=f32)

        @pl.when(k == 0)
        def _():
            o1[...] = p1
            o2[...] = p2

        @pl.when(k > 0)
        def _():
            o1[...] += p1
            o2[...] += p2

    as_ = pl.BlockSpec((tk, KA), lambda j, k: (k, j + a_col0))
    bs = pl.BlockSpec((tk, NB), lambda j, k: (k, j + b_col0))
    os_ = pl.BlockSpec((None, KA, NB), lambda j, k: (j, 0, 0))
    return pl.pallas_call(
        body, grid=(J, T // tk), in_specs=[as_, as_, bs, bs], out_specs=[os_, os_],
        out_shape=[SDS((J, KA, NB), f32)] * 2, compiler_params=_cp(("parallel", "arbitrary")), name=name,
    )(a1, a2, b1, b2)


_RB = 8


def _cmul(ar, ai, br, bi):
    return ar * br - ai * bi, ar * bi + ai * br


def _lam_powers(lr_v, li_v, cb):
    pw = {1: (lr_v, li_v)}
    for k in range(2, _RB + 1):
        pw[k] = _cmul(*pw[k - 1], lr_v, li_v)
    return pw


def _row_powers(pw, row, cb, reverse):
    outr = jnp.zeros((_RB, cb), f32)
    outi = jnp.zeros((_RB, cb), f32)
    for r in range(_RB):
        k = _RB - r if reverse else r + 1
        outr = jnp.where(row == r, pw[k][0], outr)
        outi = jnp.where(row == r, pw[k][1], outi)
    return outr, outi


def _tile_scan(xr, xi, pw, row, reverse):
    for k in (1, 2, 4):
        if reverse:
            keep = row < _RB - k
            shr, shi = pltpu.roll(xr, _RB - k, 0), pltpu.roll(xi, _RB - k, 0)
        else:
            keep = row >= k
            shr, shi = pltpu.roll(xr, k, 0), pltpu.roll(xi, k, 0)
        shr, shi = jnp.where(keep, shr, 0.0), jnp.where(keep, shi, 0.0)
        mr, mi = pw[k]
        xr, xi = xr + (mr * shr - mi * shi), xi + (mr * shi + mi * shr)
    return xr, xi


def _s5_scan(name, bur, bui, lr_, li_):
    T, N = bur.shape
    cb = _tile(N, 512)

    def body(br_ref, bi_ref, lr_ref, li_ref, sr_ref, si_ref):
        pw = _lam_powers(lr_ref[...], li_ref[...], cb)
        row = lax.broadcasted_iota(jnp.int32, (_RB, cb), 0)
        cr, ci = _row_powers(pw, row, cb, False)

        def step(n, carry):
            pr, pi = carry
            t0 = pl.multiple_of(n * _RB, _RB)
            xr, xi = _tile_scan(br_ref[pl.ds(t0, _RB), :], bi_ref[pl.ds(t0, _RB), :], pw, row, False)
            xr, xi = xr + (cr * pr - ci * pi), xi + (cr * pi + ci * pr)
            sr_ref[pl.ds(t0, _RB), :] = xr
            si_ref[pl.ds(t0, _RB), :] = xi
            return xr[_RB - 1:_RB, :], xi[_RB - 1:_RB, :]

        z = jnp.zeros((1, cb), f32)
        lax.fori_loop(0, T // _RB, step, (z, z))

    col = pl.BlockSpec((T, cb), lambda j: (0, j))
    vec = pl.BlockSpec((1, cb), lambda j: (0, j))
    return pl.pallas_call(
        body, grid=(N // cb,), in_specs=[col, col, vec, vec], out_specs=[col, col],
        out_shape=[SDS((T, N), f32)] * 2, compiler_params=_cp(("parallel",)), name=name,
    )(bur, bui, lr_, li_)


def _s5_scan_bwd(name, dsr, dsi, sr, si, lr_, li_):
    T, N = dsr.shape
    cb = _tile(N, 256)

    def body(dr_ref, di_ref, sr_ref, si_ref, lr_ref, li_ref, ar_ref, ai_ref, glr_ref, gli_ref):
        pw = _lam_powers(lr_ref[...], -li_ref[...], cb)
        row = lax.broadcasted_iota(jnp.int32, (_RB, cb), 0)
        cr, ci = _row_powers(pw, row, cb, True)
        NT = T // _RB

        def tile(t0, nxt, prev_last):
            xr, xi = _tile_scan(dr_ref[pl.ds(t0, _RB), :], di_ref[pl.ds(t0, _RB), :], pw, row, True)
            xr, xi = xr + (cr * nxt[0] - ci * nxt[1]), xi + (cr * nxt[1] + ci * nxt[0])
            ar_ref[pl.ds(t0, _RB), :] = xr
            ai_ref[pl.ds(t0, _RB), :] = xi
            pr = jnp.where(row == 0, prev_last[0], pltpu.roll(sr_ref[pl.ds(t0, _RB), :], 1, 0))
            pi = jnp.where(row == 0, prev_last[1], pltpu.roll(si_ref[pl.ds(t0, _RB), :], 1, 0))
            return xr, xi, xr * pr + xi * pi, xi * pr - xr * pi

        def step(n, carry):
            nr, ni, glr, gli = carry
            t0 = pl.multiple_of((NT - 1 - n) * _RB, _RB)
            tp = pl.multiple_of((NT - 2 - n) * _RB, _RB)
            prev_last = (sr_ref[pl.ds(tp, _RB), :][_RB - 1:_RB, :], si_ref[pl.ds(tp, _RB), :][_RB - 1:_RB, :])
            xr, xi, gr, gi = tile(t0, (nr, ni), prev_last)
            return xr[0:1, :], xi[0:1, :], glr + gr, gli + gi

        z1 = jnp.zeros((1, cb), f32)
        z8 = jnp.zeros((_RB, cb), f32)
        nr, ni, glr, gli = lax.fori_loop(0, NT - 1, step, (z1, z1, z8, z8))
        _, _, gr, gi = tile(0, (nr, ni), (z1, z1))
        glr_ref[...] = jnp.sum(glr + gr, axis=0, keepdims=True)
        gli_ref[...] = jnp.sum(gli + gi, axis=0, keepdims=True)

    col = pl.BlockSpec((T, cb), lambda j: (0, j))
    vec = pl.BlockSpec((1, cb), lambda j: (0, j))
    return pl.pallas_call(
        body, grid=(N // cb,), in_specs=[col, col, col, col, vec, vec], out_specs=[col, col, vec, vec],
        out_shape=[SDS((T, N), f32), SDS((T, N), f32), SDS((1, N), f32), SDS((1, N), f32)],
        compiler_params=_cp(("parallel",)), name=name,
    )(dsr, dsi, sr, si, lr_, li_)


def _conv_fwd(name, p, col0, w, GW3):
    T = p.shape[0]
    K = w.shape[0]
    cb = 128
    c0 = col0 // cb

    def body(x_ref, w_ref, o_ref, pad_ref):
        pad_ref[pl.ds(0, 8), :] = jnp.zeros((8, cb), f32)
        pad_ref[pl.ds(8, T), :] = x_ref[...]
        wv = w_ref[...]
        acc = jnp.zeros((T, cb), f32)
        for j in range(K):
            acc = acc + wv[j:j + 1, :] * pad_ref[pl.ds(8 - (K - 1) + j, T), :]
        o_ref[...] = jax.nn.silu(acc)

    return pl.pallas_call(
        body, grid=(GW3 // cb,),
        in_specs=[pl.BlockSpec((T, cb), lambda j: (0, j + c0)), pl.BlockSpec((K, cb), lambda j: (0, j))],
        out_specs=pl.BlockSpec((T, cb), lambda j: (0, j)), out_shape=SDS((T, GW3), f32),
        scratch_shapes=[pltpu.VMEM((T + 8, cb), f32)], compiler_params=_cp(("parallel",)), name=name,
    )(p, w)


def _conv_bwd(name, p, col0, w, dout3):
    T = p.shape[0]
    K = w.shape[0]
    GW = dout3.shape[2]
    GW3 = 3 * GW
    cb = 128
    c0 = col0 // cb
    nb = GW // cb

    def body(x_ref, w_ref, d_ref, dx_ref, dw_ref, pad_ref, dpad_ref):
        pad_ref[pl.ds(0, 8), :] = jnp.zeros((8, cb), f32)
        pad_ref[pl.ds(8, T), :] = x_ref[...]
        wv = w_ref[...]
        pre = jnp.zeros((T, cb), f32)
        for j in range(K):
            pre = pre + wv[j:j + 1, :] * pad_ref[pl.ds(8 - (K - 1) + j, T), :]
        _, vjp = jax.vjp(jax.nn.silu, pre)
        (dpre,) = vjp(d_ref[...])
        dpad_ref[pl.ds(0, T), :] = dpre
        dpad_ref[pl.ds(T, 8), :] = jnp.zeros((8, cb), f32)
        dx = jnp.zeros((T, cb), f32)
        rows = []
        for j in range(K):
            dx = dx + wv[j:j + 1, :] * dpad_ref[pl.ds((K - 1) - j, T), :]
            rows.append(jnp.sum(dpre * pad_ref[pl.ds(8 - (K - 1) + j, T), :], axis=0, keepdims=True))
        dx_ref[...] = dx.astype(dx_ref.dtype)
        for j in range(K):
            dw_ref[pl.ds(j, 1), :] = rows[j]

    return pl.pallas_call(
        body, grid=(GW3 // cb,),
        in_specs=[pl.BlockSpec((T, cb), lambda j: (0, j + c0)), pl.BlockSpec((K, cb), lambda j: (0, j)),
                  pl.BlockSpec((None, T, cb), lambda j: (j // nb, 0, j % nb))],
        out_specs=[pl.BlockSpec((T, cb), lambda j: (0, j)), pl.BlockSpec((K, cb), lambda j: (0, j))],
        out_shape=[SDS((T, GW3), _MXU), SDS((K, GW3), f32)],
        scratch_shapes=[pltpu.VMEM((T + 8, cb), f32), pltpu.VMEM((T + 8, cb), f32)],
        compiler_params=_cp(("parallel",)), name=name,
    )(p, w, dout3)


def _hdot(a, b, dims=_NN):
    return lax.dot_general(a, b, dims, precision=_HP, preferred_element_type=f32)


def _split(a):
    hi = a.astype(jnp.bfloat16)
    lo = (a - hi.astype(f32)).astype(jnp.bfloat16)
    return hi, lo


_BNN = (((2,), (1,)), ((0,), (0,)))
_BNT = (((2,), (2,)), ((0,), (0,)))
_BTN = (((1,), (1,)), ((0,), (0,)))


def _dot3_raw(a, b, dims):
    ah, al = _split(a)
    bh, bl = _split(b)
    d = functools.partial(lax.dot_general, dimension_numbers=dims, preferred_element_type=f32)
    return d(ah, bh) + (d(al, bh) + d(ah, bl))


@jax.custom_vjp
def _dot3(a, b):
    return _dot3_raw(a, b, _BNN)


def _dot3_fwd(a, b):
    return _dot3_raw(a, b, _BNN), (a, b)


def _dot3_bwd(res, g):
    a, b = res
    return _dot3_raw(g, b, _BNT), _dot3_raw(a, g, _BTN)


_dot3.defvjp(_dot3_fwd, _dot3_bwd)


def _ldot(a, b, dims=_BNN):
    return lax.dot_general(a.astype(_MXU), b.astype(_MXU), dims, preferred_element_type=f32)


def _sdot(a, b):
    return _ldot(a, b)


def _gdn_chunk(S, q, k, v, z, bl, ain, alog, dtb, nw):
    H, C, d = q.shape
    ri = lax.broadcasted_iota(jnp.int32, (H, C, C), 1)
    ci = lax.broadcasted_iota(jnp.int32, (H, C, C), 2)
    causal = ri >= ci
    strict = ri > ci
    tri = causal.astype(f32)
    qn = q * lax.rsqrt(jnp.sum(q * q, axis=-1, keepdims=True) + L2_EPS) * (d ** -0.5)
    kn = k * lax.rsqrt(jnp.sum(k * k, axis=-1, keepdims=True) + L2_EPS)
    beta = jax.nn.sigmoid(bl)
    g = -jnp.exp(alog) * jax.nn.softplus(ain + dtb)
    gb = jnp.broadcast_to(g, (H, C, C))
    gc_col = _dot3(tri, gb)
    gc_row = _dot3(jnp.ones((H, C, C), f32), jnp.where(ri <= ci, gb, 0.0))
    diff = jnp.where(causal, gc_col - gc_row, 0.0)
    decay = jnp.where(causal, jnp.exp(diff), 0.0)
    gcum = gc_col[:, :, 0:1]
    glast = gc_col[:, C - 1:C, 0:1]
    egc = jnp.exp(gcum)
    kb = kn * beta
    lower = jnp.where(strict, _ldot(kb, kn, _BNT) * decay, 0.0)
    x = jnp.concatenate([v * beta, kb * egc], axis=-1)
    m = -lower
    for it in range(6):
        x = x + _sdot(m, x)
        if it < 5:
            m = _sdot(m, m)
    u_val, w_key = x[:, :, :d], x[:, :, d:]
    attn = _ldot(qn, kn, _BNT) * decay
    q_dec = qn * egc
    k_dec = kn * jnp.exp(glast - gcum)
    v_new = u_val - _ldot(w_key, S)
    out = _ldot(q_dec, S) + _ldot(attn, v_new)
    s_new = S * jnp.exp(glast) + _ldot(k_dec, v_new, _BTN)
    o = out * lax.rsqrt(jnp.mean(out * out, axis=-1, keepdims=True) + RMS_EPS) * nw
    o = o * jax.nn.silu(z)
    return s_new, o


def _heads_per_step(NH, HD, zcol0):
    for hb in (8, 4, 2):
        if NH % hb == 0 and zcol0 % (hb * HD) == 0:
            return hb
    return 1


def _gdn_fwd(name, qkv, p, zcol0, blt, aint, alog, dtb, nw, NH, HD):
    T = qkv.shape[0]
    N = T // CHUNK
    GW = NH * HD
    HB = _heads_per_step(NH, HD, zcol0)
    W = HB * HD
    zc0 = zcol0 // W
    nb = GW // W

    def body(q_ref, k_ref, v_ref, z_ref, bl_ref, ain_ref, al_ref, dtb_ref, nw_ref, o_ref, ssave_ref, s_scr):
        n = pl.program_id(1)

        @pl.when(n == 0)
        def _():
            s_scr[...] = jnp.zeros_like(s_scr)

        heads = lambda r: jnp.stack([r[:, hh * HD:(hh + 1) * HD] for hh in range(HB)], axis=0)
        s_in = s_scr[...]
        ssave_ref[...] = s_in
        s_new, o = _gdn_chunk(s_in, heads(q_ref), heads(k_ref), heads(v_ref), heads(z_ref), bl_ref[...], ain_ref[...],
                              al_ref[...], dtb_ref[...], nw_ref[...])
        s_scr[...] = s_new
        for hh in range(HB):
            o_ref[:, hh * HD:(hh + 1) * HD] = o[hh].astype(o_ref.dtype)

    ch = lambda off: pl.BlockSpec((CHUNK, W), lambda h, n: (n, h + off))
    sc = pl.BlockSpec((HB, CHUNK, 1), lambda h, n: (h, n, 0))
    hs = pl.BlockSpec((HB, 1, 1), lambda h, n: (h, 0, 0))
    return pl.pallas_call(
        body, grid=(NH // HB, N),
        in_specs=[ch(0), ch(nb), ch(2 * nb), ch(zc0), sc, sc, hs, hs, pl.BlockSpec((1, HD), lambda h, n: (0, 0))],
        out_specs=[pl.BlockSpec((CHUNK, W), lambda h, n: (n, h)),
                   pl.BlockSpec((HB, None, HD, HD), lambda h, n: (h, n, 0, 0))],
        out_shape=[SDS((T, GW), _MXU), SDS((NH, N, HD, HD), f32)],
        scratch_shapes=[pltpu.VMEM((HB, HD, HD), f32)], compiler_params=_cp(("parallel", "arbitrary")), name=name,
    )(qkv, qkv, qkv, p, blt, aint, alog, dtb, nw)


def _gdn_bwd(name, qkv, p, zcol0, blt, aint, alog, dtb, nw, ssave, do, NH, HD):
    T = qkv.shape[0]
    N = T // CHUNK
    GW = NH * HD
    HB = _heads_per_step(NH, HD, zcol0)
    W = HB * HD
    zc0 = zcol0 // W
    nb = GW // W

    def body(q_ref, k_ref, v_ref, z_ref, bl_ref, ain_ref, al_ref, dtb_ref, nw_ref, ss_ref, do_ref,
             dqkv_ref, dz_ref, dbl_ref, dain_ref, dal_ref, ddtb_ref, dnw_ref, ds_scr):
        h = pl.program_id(0)
        n = pl.program_id(1)

        @pl.when(n == 0)
        def _():
            ds_scr[...] = jnp.zeros_like(ds_scr)

        heads = lambda r: jnp.stack([r[:, hh * HD:(hh + 1) * HD] for hh in range(HB)], axis=0)
        _, vjp = jax.vjp(_gdn_chunk, ss_ref[...], heads(q_ref), heads(k_ref), heads(v_ref), heads(z_ref), bl_ref[...],
                         ain_ref[...], al_ref[...], dtb_ref[...], nw_ref[...])
        ds, dq, dk, dv, dz, dbl, dain, dal, ddtb, dnw = vjp((ds_scr[...], heads(do_ref).astype(f32)))
        ds_scr[...] = ds
        for hh in range(HB):
            cs = slice(hh * HD, (hh + 1) * HD)
            dqkv_ref[0, :, cs] = dq[hh]
            dqkv_ref[1, :, cs] = dk[hh]
            dqkv_ref[2, :, cs] = dv[hh]
            dz_ref[:, cs] = dz[hh].astype(dz_ref.dtype)
        dbl_ref[...] = dbl
        dain_ref[...] = dain

        @pl.when(n == 0)
        def _():
            dal_ref[...] = dal
            ddtb_ref[...] = ddtb

        @pl.when(n > 0)
        def _():
            dal_ref[...] += dal
            ddtb_ref[...] += ddtb

        @pl.when((n == 0) & (h == 0))
        def _():
            dnw_ref[...] = dnw

        @pl.when((n > 0) | (h > 0))
        def _():
            dnw_ref[...] += dnw

    R = N - 1
    ch = lambda off: pl.BlockSpec((CHUNK, W), lambda h, n: (R - n, h + off))
    sc = pl.BlockSpec((HB, CHUNK, 1), lambda h, n: (h, R - n, 0))
    hs = pl.BlockSpec((HB, 1, 1), lambda h, n: (h, 0, 0))
    nws = pl.BlockSpec((1, HD), lambda h, n: (0, 0))
    return pl.pallas_call(
        body, grid=(NH // HB, N),
        in_specs=[ch(0), ch(nb), ch(2 * nb), ch(zc0), sc, sc, hs, hs, nws,
                  pl.BlockSpec((HB, None, HD, HD), lambda h, n: (h, R - n, 0, 0)),
                  pl.BlockSpec((CHUNK, W), lambda h, n: (R - n, h))],
        out_specs=[pl.BlockSpec((3, CHUNK, W), lambda h, n: (0, R - n, h)),
                   pl.BlockSpec((CHUNK, W), lambda h, n: (R - n, h)), sc, sc, hs, hs, nws],
        out_shape=[SDS((3, T, GW), f32), SDS((T, GW), _MXU), SDS((NH, T, 1), f32), SDS((NH, T, 1), f32),
                   SDS((NH, 1, 1), f32), SDS((NH, 1, 1), f32), SDS((1, HD), f32)],
        scratch_shapes=[pltpu.VMEM((HB, HD, HD), f32)], compiler_params=_cp(("arbitrary", "arbitrary")), name=name,
    )(qkv, qkv, qkv, p, blt, aint, alog, dtb, nw, ssave, do)


def _loss_head(name, y, tgt):
    T, D = y.shape
    tm = _rtile(T, 256)

    def body(y_ref, t_ref, dy_ref, l_ref):
        i = pl.program_id(0)
        err = y_ref[...] - t_ref[...]
        dy_ref[...] = err * (1.0 / D)
        part = 0.5 * jnp.sum(jnp.sum(err * err, axis=-1, keepdims=True) * (1.0 / D), axis=0, keepdims=True)

        @pl.when(i == 0)
        def _():
            l_ref[...] = part

        @pl.when(i > 0)
        def _():
            l_ref[...] += part

    row = pl.BlockSpec((tm, D), lambda i: (i, 0))
    return pl.pallas_call(
        body, grid=(T // tm,), in_specs=[row, row], out_specs=[row, pl.BlockSpec((1, 1), lambda i: (0, 0))],
        out_shape=[SDS((T, D), f32), SDS((1, 1), f32)], compiler_params=_cp(("arbitrary",)), name=name,
    )(y, tgt)


def _adam_math(w, g, m, v):
    m = ADAM_B1 * m + (1.0 - ADAM_B1) * g
    v = ADAM_B2 * v + (1.0 - ADAM_B2) * jnp.square(g)
    m_hat = m / (1.0 - ADAM_B1 ** ADAM_STEP)
    v_hat = v / (1.0 - ADAM_B2 ** ADAM_STEP)
    delta = -ADAM_LR * (m_hat / (jnp.sqrt(v_hat) + ADAM_EPS) + ADAM_WD * w)
    return delta, m, v


def _add_mine(name, full, recv, me, out_dtype):
    N, _, R, C = full.shape
    tr = _rtile(R, max(16, (1 << 19) // max(C, 1) // 16 * 16))

    def body(me_ref, a_ref, b_ref, o_ref):
        o_ref[...] = (a_ref[...].astype(f32) + b_ref[...].astype(f32)).astype(o_ref.dtype)

    blk = pl.BlockSpec((None, tr, C), lambda n, i, me_ref: (n, i, 0))
    return pl.pallas_call(
        body,
        grid_spec=pltpu.PrefetchScalarGridSpec(
            num_scalar_prefetch=1, grid=(N, R // tr),
            in_specs=[pl.BlockSpec((None, None, tr, C), lambda n, i, me_ref: (n, me_ref[0], i, 0)), blk], out_specs=blk),
        out_shape=SDS((N, R, C), out_dtype), compiler_params=_cp(("parallel", "parallel")), name=name,
    )(me, full, recv)


def _adamw_big(name, full, recv, me, w, m, v, l, accs):
    _, R, C = full.shape
    L = w.shape[0]
    tr = _rtile(R, max(16, (1 << 18) // max(C, 1) // 16 * 16))

    def body(me_ref, ga_ref, gb_ref, w_ref, m_ref, v_ref, a0, a1, a2, a3, g_ref, d_ref, nm_ref, nv_ref):
        g = ga_ref[...].astype(f32) + gb_ref[...].astype(f32)
        d, nm, nv = _adam_math(w_ref[...], g, m_ref[...], v_ref[...])
        g_ref[...] = g
        d_ref[...] = d
        nm_ref[...] = nm
        nv_ref[...] = nv

    blk = pl.BlockSpec((tr, C), lambda i, me_ref: (i, 0))
    lblk = pl.BlockSpec((None, tr, C), lambda i, me_ref: (l, i, 0))
    untouched = pl.BlockSpec(memory_space=pl.ANY)
    return pl.pallas_call(
        body,
        grid_spec=pltpu.PrefetchScalarGridSpec(
            num_scalar_prefetch=1, grid=(R // tr,),
            in_specs=[pl.BlockSpec((None, tr, C), lambda i, me_ref: (me_ref[0], i, 0)), blk, lblk, lblk, lblk] + [untouched] * 4,
            out_specs=[lblk] * 4),
        out_shape=[SDS((L, R, C), f32)] * 4, input_output_aliases={6: 0, 7: 1, 8: 2, 9: 3},
        compiler_params=_cp(("parallel",)), name=name,
    )(me, full, recv, w, m, v, *accs)


def _adamw_small(name, gall, w, m, v):
    _, R, C = gall.shape
    tr = _rtile(R, 512)

    def body(ga_ref, w_ref, m_ref, v_ref, g_ref, d_ref, nm_ref, nv_ref):
        g = ga_ref[0]
        for s in range(1, 8):
            g = g + ga_ref[s]
        d, nm, nv = _adam_math(w_ref[...], g, m_ref[...], v_ref[...])
        g_ref[...] = g
        d_ref[...] = d
        nm_ref[...] = nm
        nv_ref[...] = nv

    blk = pl.BlockSpec((tr, C), lambda i: (i, 0))
    return pl.pallas_call(
        body, grid=(R // tr,), in_specs=[pl.BlockSpec((8, tr, C), lambda i: (0, i, 0)), blk, blk, blk], out_specs=[blk] * 4,
        out_shape=[SDS((R, C), f32)] * 4, compiler_params=_cp(("parallel",)), name=name,
    )(gall, w, m, v)


def _peer(axis):
    x, y, c = lax.axis_index("x"), lax.axis_index("y"), lax.axis_index("c")
    me = {"x": x, "y": y, "c": c}[axis]
    peer = {"x": (1 - x, y, c), "y": (x, 1 - y, c), "c": (x, y, 1 - c)}[axis]
    return me, peer


def _held(ref, done):
    idx = tuple(slice(None) if a in done else lax.axis_index(a) for a in ("x", "y", "c"))
    return ref.at[idx]


def _gather_stage(name, bufs, axes, dones):
    n = len(bufs)
    hbm = pl.BlockSpec(memory_space=pltpu.HBM)

    def body(*refs):
        outs = refs[n:2 * n]
        send_sems, recv_sems = refs[2 * n:]
        cps = []
        for t in range(n):
            _, peer = _peer(axes[t])
            blk = _held(outs[t], dones[t])
            cps.append(pltpu.make_async_remote_copy(src_ref=blk, dst_ref=blk, send_sem=send_sems.at[t],
                                                    recv_sem=recv_sems.at[t], device_id=peer, device_id_type=_MESH_T))
        for cp in cps:
            cp.start()
        for cp in cps:
            cp.wait()

    return pl.pallas_call(
        body, in_specs=[hbm] * n, out_specs=[hbm] * n, out_shape=[SDS(b.shape, b.dtype) for b in bufs],
        input_output_aliases={t: t for t in range(n)},
        scratch_shapes=[pltpu.SemaphoreType.DMA((n,)), pltpu.SemaphoreType.DMA((n,))], name=name,
    )(*bufs)


_HBM = pl.BlockSpec(memory_space=pltpu.HBM)
_SEM = pl.BlockSpec(memory_space=pltpu.SEMAPHORE)
_EFFECT = pltpu.SideEffectType.DATAFLOW_SIDE_EFFECTING


def _split_start(name, arrays, n_copies, make_copies):
    na = len(arrays)

    def body(*refs):
        ins = refs[:na]
        send_sems, recv_sems = refs[na], refs[na + 1]
        token = refs[2 * na + 2]
        for cp in make_copies(ins, send_sems, recv_sems):
            cp.start()
        token[...] = jnp.zeros_like(token)

    res = pl.pallas_call(
        body, name=name,
        out_shape=(pltpu.SemaphoreType.DMA((n_copies,)), pltpu.SemaphoreType.DMA((n_copies,)),
                   *[pltpu.HBM(a.shape, a.dtype) for a in arrays], SDS((8, 128), f32)),
        in_specs=[_HBM] * na, out_specs=(_SEM, _SEM, *[_HBM] * na, pl.BlockSpec(memory_space=pltpu.VMEM)),
        input_output_aliases={i: 2 + i for i in range(na)},
        compiler_params=pltpu.CompilerParams(has_side_effects=_EFFECT),
    )(*[pltpu.with_memory_space_constraint(a, pltpu.HBM) for a in arrays])
    return res[0], res[1], list(res[2:2 + na]), res[2 + na]


def _split_wait(name, arrays, send_sems, recv_sems, after, make_copies):
    na = len(arrays)
    afters = list(after) if isinstance(after, (list, tuple)) else [after]

    def body(*refs):
        ins = refs[:na]
        for cp in make_copies(ins, refs[na], refs[na + 1]):
            cp.wait_send()
            cp.wait_recv()

    res = pl.pallas_call(
        body, name=name, out_shape=tuple(pltpu.HBM(a.shape, a.dtype) for a in arrays),
        in_specs=[_HBM] * na + [_SEM, _SEM] + [pl.BlockSpec(memory_space=pl.ANY)] * len(afters),
        out_specs=tuple([_HBM] * na), input_output_aliases={i: i for i in range(na)},
        compiler_params=pltpu.CompilerParams(has_side_effects=_EFFECT),
    )(*arrays, send_sems, recv_sems, *afters)
    return list(res)


def _gather_copies(axes, dones):
    def make(refs, send_sems, recv_sems):
        cps = []
        for t in range(len(axes)):
            _, peer = _peer(axes[t])
            blk = _held(refs[t], dones[t])
            cps.append(pltpu.make_async_remote_copy(src_ref=blk, dst_ref=blk, send_sem=send_sems.at[t],
                                                    recv_sem=recv_sems.at[t], device_id=peer, device_id_type=_MESH_T))
        return cps
    return make


def _scatter_copies(axes):
    n = len(axes)

    def make(refs, send_sems, recv_sems):
        cps = []
        for t in range(n):
            me, peer = _peer(axes[t])
            cps.append(pltpu.make_async_remote_copy(
                src_ref=refs[t].at[:, 1 - me], dst_ref=refs[n + t], send_sem=send_sems.at[t], recv_sem=recv_sems.at[t],
                device_id=peer, device_id_type=_MESH_T))
        return cps
    return make


class _AsyncGather:
    def __init__(self, pfx, tensors, paths):
        x, y, c = (lax.axis_index(a) for a in ("x", "y", "c"))
        self.pfx, self.shapes = pfx, [tuple(t.shape) for t in tensors]
        self.bufs = [lax.dynamic_update_slice(lax.empty((2, 2, 2) + tuple(t.shape), t.dtype), t[None, None, None],
                                              (x, y, c) + (0,) * t.ndim) for t in tensors]
        self.orders = [tuple(p) + ("c",) for p in paths]
        self.ph = 0

    def _make(self):
        return _gather_copies([o[self.ph] for o in self.orders], [o[:self.ph] for o in self.orders])

    def start(self):
        self.ss, self.rs, self.bufs, tok = _split_start(f"{self.pfx}_start{self.ph}", self.bufs, len(self.bufs), self._make())
        return tok

    def wait(self, after):
        self.bufs = _split_wait(f"{self.pfx}_wait{self.ph}", self.bufs, self.ss, self.rs, after, self._make())
        self.ph += 1

    def result(self):
        return [b.reshape((8,) + s) for b, s in zip(self.bufs, self.shapes)]


class _AsyncReduceScatter:
    def __init__(self, pfx, tensors, paths):
        self.pfx = pfx
        self.rcs = [tuple(t.shape[1:]) for t in tensors]
        self.orders = [("c",) + tuple(p) for p in paths]
        self.left = [["x", "y", "c"] for _ in tensors]
        self.cur = list(tensors)
        self.ph = 0

    def start(self):
        n = len(self.cur)
        views = []
        for i, (t, rc) in enumerate(zip(self.cur, self.rcs)):
            pos = self.left[i].index(self.orders[i][self.ph])
            nb, na = 2 ** pos, 2 ** (len(self.left[i]) - pos - 1)
            views.append(t.reshape((nb, 2, na * rc[0], rc[1])))
        lands = [lax.empty((v.shape[0],) + tuple(v.shape[2:]), v.dtype) for v in views]
        self.make = _scatter_copies([o[self.ph] for o in self.orders])
        self.ss, self.rs, arrs, tok = _split_start(f"{self.pfx}_start{self.ph}", views + lands, n, self.make)
        self.arrs = arrs
        return tok

    def wait(self, after):
        n = len(self.cur)
        arrs = _split_wait(f"{self.pfx}_wait{self.ph}", self.arrs, self.ss, self.rs, after, self.make)
        views, recvs = arrs[:n], arrs[n:]
        ph = self.ph
        if ph == 2:
            self.out = [(v[0], r[0], o[2]) for v, r, o in zip(views, recvs, self.orders)]
        else:
            self.cur = [_add_mine(f"{self.pfx}_add{ph}_{i}", v, r, _coord(o[ph]), v.dtype)
                        for i, (v, r, o) in enumerate(zip(views, recvs, self.orders))]
            for i, o in enumerate(self.orders):
                self.left[i].remove(o[ph])
        self.ph += 1


def _coord(axis):
    return lax.axis_index(axis).astype(jnp.int32).reshape(1)


def _all_gather(pfx, tensors, paths):
    x, y, c = (lax.axis_index(a) for a in ("x", "y", "c"))
    bufs = []
    for t in tensors:
        zero = (0,) * t.ndim
        bufs.append(lax.dynamic_update_slice(lax.empty((2, 2, 2) + tuple(t.shape), t.dtype), t[None, None, None],
                                             (x, y, c) + zero))
    orders = [tuple(p) + ("c",) for p in paths]
    for ph in range(3):
        bufs = _gather_stage(f"{pfx}_{ph}", bufs, [o[ph] for o in orders], [o[:ph] for o in orders])
    return [b.reshape((8,) + tuple(t.shape)) for b, t in zip(bufs, tensors)]


def _mm_nn(name, a, w, out_dtype, tn_pref=1024):
    T, K = a.shape
    N = w.shape[1]
    tm, tn, tk = _rtile(T, 512), _tile(N, tn_pref), _tile(K, 2048)
    return _mm(
        name, a, w, _NN, (T // tm, N // tn, K // tk),
        pl.BlockSpec((tm, tk), lambda i, j, k: (i, k)), pl.BlockSpec((tk, tn), lambda i, j, k: (k, j)), (tm, tn),
        [], [], [SDS((T, N), out_dtype)], [pl.BlockSpec((tm, tn), lambda i, j, k: (i, j))], _store_epi,
    )[0]


def _mm_tn(name, a, b, out_dtype):
    T, M = a.shape
    N = b.shape[1]
    tm, tn, tk = _tile(M, 512), _tile(N, 2048), _rtile(T, 512)
    return _mm(
        name, a, b, _TN, (M // tm, N // tn, T // tk),
        pl.BlockSpec((tk, tm), lambda i, j, k: (k, i)), pl.BlockSpec((tk, tn), lambda i, j, k: (k, j)), (tm, tn),
        [], [], [SDS((M, N), out_dtype)], [pl.BlockSpec((tm, tn), lambda i, j, k: (i, j))], _store_epi,
    )[0]


def _mm_tn_slots(name, a, b, out_dtype):
    T, M = a.shape
    NS = b.shape[1] // 8
    tm, tk = _tile(M, 512), _rtile(T, 512)
    return _mm(
        name, a, b, _TN, (M // tm, 8, T // tk),
        pl.BlockSpec((tk, tm), lambda i, j, k: (k, i)), pl.BlockSpec((tk, NS), lambda i, j, k: (k, j)), (tm, NS),
        [], [], [SDS((8, M, NS), out_dtype)], [pl.BlockSpec((None, tm, NS), lambda i, j, k: (j, i, 0))], _store_epi,
    )[0]


def _mm_nt_slots(name, a, w8, out_dtype):
    T = a.shape[0]
    _, M, NS = w8.shape
    tm, tn = _rtile(T, 512), _tile(M, 1024)
    return _mm(
        name, a, w8, _NT, (T // tm, M // tn, 8),
        pl.BlockSpec((tm, NS), lambda i, j, k: (i, k)), pl.BlockSpec((None, tn, NS), lambda i, j, k: (k, j, 0)), (tm, tn),
        [], [], [SDS((T, M), out_dtype)], [pl.BlockSpec((tm, tn), lambda i, j, k: (i, j))], _store_epi,
    )[0]


def _colsum_kernel(name, fn, ins, in_cols, outs_elem, n_sum, C):
    T = ins[0].shape[0]
    tm = _rtile(T, 256)
    ne = len(outs_elem)

    def body(*refs):
        i = pl.program_id(0)
        iv = [r[...] for r in refs[:len(ins)]]
        res = fn(*iv)
        for o, r in zip(refs[len(ins):len(ins) + ne], res[:ne]):
            o[...] = r.astype(o.dtype)
        sums = [jnp.sum(r, axis=0, keepdims=True) for r in res[ne:]]

        @pl.when(i == 0)
        def _():
            for o, s in zip(refs[len(ins) + ne:], sums):
                o[...] = s

        @pl.when(i > 0)
        def _():
            for o, s in zip(refs[len(ins) + ne:], sums):
                o[...] += s

    in_specs = []
    for arr, off in zip(ins, in_cols):
        if off is None:
            in_specs.append(pl.BlockSpec((1, C), lambda i: (0, 0)))
        else:
            in_specs.append(pl.BlockSpec((tm, C), lambda i, off=off: (i, off)))
    row = pl.BlockSpec((tm, C), lambda i: (i, 0))
    vec = pl.BlockSpec((1, C), lambda i: (0, 0))
    return pl.pallas_call(
        body, grid=(T // tm,), in_specs=in_specs, out_specs=[row] * ne + [vec] * n_sum,
        out_shape=[SDS((T, C), dt) for dt in outs_elem] + [SDS((1, C), f32)] * n_sum,
        compiler_params=_cp(("arbitrary",)), name=name,
    )(*ins)


def _merge(gs, gg, a_s, a_g):
    return jax.nn.sigmoid(gs) * a_s + jax.nn.sigmoid(gg) * a_g


def _glu(yg, lp):
    return yg * jax.nn.sigmoid(lp)


_BIG = ("ffn1_w_gu", "ffn1_w_down", "w_in", "conv_w", "glu_w", "w_br_ssm", "w_br_gdn", "w_out", "ffn2_w_gu", "ffn2_w_down")
_PATHS = ("yx", "yx", "xy", "xy", "yx", "yx", "yx", "yx", "xy", "xy")
_SMALL = ("ln1_g", "ln1_b", "ssm_a_re", "ssm_a_im", "ssm_log_dt", "ssm_b_re", "ssm_b_im", "ssm_c_re", "ssm_c_im", "ssm_d",
          "glu_b", "gdn_a_log", "gdn_dt_bias", "gdn_norm_w", "ln2_g", "ln2_b", "ln3_g", "ln3_b")
_ORDER = ("ffn1_w_gu", "ffn1_w_down", "ln1_g", "ln1_b", "w_in", "conv_w", "ssm_a_re", "ssm_a_im", "ssm_log_dt", "ssm_b_re",
          "ssm_b_im", "ssm_c_re", "ssm_c_im", "ssm_d", "glu_w", "glu_b", "gdn_a_log", "gdn_dt_bias", "gdn_norm_w", "w_br_ssm",
          "w_br_gdn", "w_out", "ln2_g", "ln2_b", "ffn2_w_gu", "ffn2_w_down", "ln3_g", "ln3_b")


def _step(x, tgt, W, M, V):
    T, D = x.shape[1], x.shape[2]
    L = W["ffn1_w_gu"].shape[0]
    G, P = W["ssm_a_re"].shape[1:]
    H = W["ssm_b_re"].shape[3]
    SW = G * H
    NH = W["gdn_a_log"].shape[1]
    HD = W["gdn_norm_w"].shape[1]
    GW = NH * HD
    KC = W["conv_w"].shape[1]
    DS = D // 8
    alpha = (2.0 * L) ** 0.25
    o_b = SW + 4 * GW
    o_gs = o_b + 2 * NH
    IN = o_gs + 2 * D
    NM = IN - 2 * NH
    m_qkv, m_z, m_gs, m_gg = SW, SW + 3 * GW, SW + 4 * GW, SW + 4 * GW + D
    J = G // 8

    x0 = x[0]
    tg = tgt[0]

    def vec(name, l):
        return W[name][l:l + 1]

    saves, weights = [], []
    xc, xcb = x0, x0.astype(_MXU)
    def shards(l):
        return [W["ffn1_w_gu"][l].astype(_MXU), W["ffn1_w_down"][l].astype(_MXU), W["w_in"][l].astype(_MXU), W["conv_w"][l],
                W["glu_w"][l].astype(_MXU), W["w_br_ssm"][l].astype(_MXU), W["w_br_gdn"][l].astype(_MXU),
                W["w_out"][l].astype(_MXU), W["ffn2_w_gu"][l].astype(_MXU), W["ffn2_w_down"][l].astype(_MXU)]

    def dep(a, tok):
        return a if tok is None else a + tok[0:1, 0:1].astype(a.dtype)

    gathered = _all_gather("ag", shards(0), _PATHS)
    ahead = {}
    for l in range(L):
        toks = []
        nxt = ahead.get(l + 1)
        if nxt and l >= 1:
            nxt.wait(xc)
            toks.append(nxt.start())
        for k in ([1, 2] if l == 0 else [l + 2]):
            if k < L:
                ahead[k] = _AsyncGather("agp", shards(k), _PATHS)
                toks.append(ahead[k].start())
        nxt = ahead.get(l + 1)
        tok = functools.reduce(lambda a, b: a + b, toks) if toks else None
        wgu1, wd1, win8, cw8, wglu, wbs, wbg, wo, wgu2, wd2 = gathered
        wd1 = wd1.reshape(-1, D)
        wd2 = wd2.reshape(-1, D)
        wglu = wglu.reshape(SW, SW)
        wo = wo.reshape(D, D)
        win = jnp.transpose(win8, (1, 0, 2)).reshape(D, IN)
        wmain = jnp.concatenate([win[:, :o_b], win[:, o_gs:]], axis=1)
        wba = jnp.pad(win[:, o_b:o_gs], ((0, 0), (0, 128 - 2 * NH)))
        cw = jnp.transpose(cw8, (1, 0, 2)).reshape(KC, 3 * GW)
        wl = dict(wgu1=wgu1, wd1=wd1, wmain=wmain, wba=wba, cw=cw, wglu=wglu, wbs=wbs, wbg=wbg, wo=wo, wgu2=wgu2, wd2=wd2)
        weights.append(wl)
        sv = {}

        gate, up, hh = _ffn_up("ffn_up", xcb, wgu1)
        x1, x1b, xh1, r1 = _mm_ln("ffn_down_ln", hh, wd1, xc, dep(vec("ln1_g", l), tok), vec("ln1_b", l), alpha, 0.5)
        sv["f1"] = dict(xb=xcb, gate=gate, up=up, h=hh, xhat=xh1, rstd=r1)

        p = _mm_nn("mix_in", x1b, wmain, f32)
        pba = _mm_nn("mix_in_ba", x1b, wba, f32)
        b_re_t = jnp.transpose(W["ssm_b_re"][l], (2, 0, 1))
        b_im_t = jnp.transpose(W["ssm_b_im"][l], (2, 0, 1))
        zoh_in = (W["ssm_a_re"][l], W["ssm_a_im"][l], W["ssm_log_dt"][l][:, None], b_re_t, b_im_t)
        lbr, lbi, bbr_t, bbi_t = _zoh_fwd("zoh", *zoh_in)
        bblk_r = _blockdiag(jnp.transpose(bbr_t, (1, 0, 2)))
        bblk_i = _blockdiag(jnp.transpose(bbi_t, (1, 0, 2)))
        cblkT_r = _blockdiag(W["ssm_c_re"][l])
        cblkT_in = _blockdiag(-W["ssm_c_im"][l])
        lbr_f, lbi_f = lbr.reshape(1, G * P), lbi.reshape(1, G * P)
        bur, bui = _bd2("s5_bu", p, 0, bblk_r, bblk_i)
        if nxt and l == 0:
            nxt.wait(bur)
            tok = nxt.start()
        sr, si = _s5_scan("s5_scan", bur, bui, dep(lbr_f, tok), lbi_f)
        dflat = W["ssm_d"][l].reshape(1, SW)

        def out_epi(acc, ex, outs):
            y_raw = acc + ex[1][...] * ex[0][...]
            yg = jax.nn.gelu(y_raw)
            outs[0][...] = y_raw
            outs[1][...] = yg
            outs[2][...] = yg.astype(outs[2].dtype)

        y_raw, yg, ygb = _bd_sum(
            "s5_out", sr, si, jnp.transpose(cblkT_r, (0, 2, 1)), jnp.transpose(cblkT_in, (0, 2, 1)), [p, dflat],
            lambda tm, nb: [pl.BlockSpec((tm, nb), lambda i, j: (i, j)), pl.BlockSpec((1, nb), lambda i, j: (0, j))],
            [SDS((T, SW), f32), SDS((T, SW), f32), SDS((T, SW), _MXU)], out_epi)

        tmg, tng, tkg = _rtile(T, 512), _tile(SW, 512), _tile(SW, 1024)

        def glu_epi(acc, ex, outs):
            lp = acc + ex[1][...]
            outs[0][...] = lp
            outs[1][...] = _glu(ex[0][...], lp).astype(outs[1].dtype)

        lp, ysb = _mm(
            "s5_glu", ygb, wglu, _NN, (T // tmg, SW // tng, SW // tkg),
            pl.BlockSpec((tmg, tkg), lambda i, j, k: (i, k)), pl.BlockSpec((tkg, tng), lambda i, j, k: (k, j)), (tmg, tng),
            [yg, vec("glu_b", l)], [pl.BlockSpec((tmg, tng), lambda i, j, k: (i, j)), pl.BlockSpec((1, tng), lambda i, j, k: (0, j))],
            [SDS((T, SW), f32), SDS((T, SW), _MXU)], [pl.BlockSpec((tmg, tng), lambda i, j, k: (i, j))] * 2, glu_epi)

        qkv = _conv_fwd("gdn_conv", p, m_qkv, cw, 3 * GW)
        blt = jnp.transpose(pba[:, :NH])[:, :, None]
        aint = jnp.transpose(pba[:, NH:2 * NH])[:, :, None]
        alog = W["gdn_a_log"][l].reshape(NH, 1, 1)
        dtb = W["gdn_dt_bias"][l].reshape(NH, 1, 1)
        nw = vec("gdn_norm_w", l)
        og, ssave = _gdn_fwd("gdn", qkv, p, m_z, blt, aint, alog, dtb, nw, NH, HD)

        a_s = _mm(
            "br_ssm", ysb, wbs, _NN, (T // tmg, 8, SW // tkg),
            pl.BlockSpec((tmg, tkg), lambda i, j, k: (i, k)), pl.BlockSpec((None, tkg, DS), lambda i, j, k: (j, k, 0)), (tmg, DS),
            [], [], [SDS((T, D), f32)], [pl.BlockSpec((tmg, DS), lambda i, j, k: (i, j))], _store_epi)[0]
        tkd = _tile(GW, 1024)
        gsb, ggb = m_gs // DS, m_gg // DS

        def merge_epi(acc, ex, outs):
            outs[0][...] = acc
            outs[1][...] = _merge(ex[1][...], ex[2][...], ex[0][...], acc).astype(outs[1].dtype)

        tile_ij = pl.BlockSpec((tmg, DS), lambda i, j, k: (i, j))
        a_g, merged = _mm(
            "br_gdn_merge", og, wbg, _NN, (T // tmg, 8, GW // tkd),
            pl.BlockSpec((tmg, tkd), lambda i, j, k: (i, k)), pl.BlockSpec((None, tkd, DS), lambda i, j, k: (j, k, 0)), (tmg, DS),
            [a_s, p, p], [tile_ij, pl.BlockSpec((tmg, DS), lambda i, j, k: (i, j + gsb)),
                          pl.BlockSpec((tmg, DS), lambda i, j, k: (i, j + ggb))],
            [SDS((T, D), f32), SDS((T, D), _MXU)], [tile_ij, tile_ij], merge_epi)
        if nxt:
            nxt.wait(merged)
            tok = nxt.start()
        x2, x2b, xh2, r2 = _mm_ln("mix_out_ln", merged, wo, x1, dep(vec("ln2_g", l), tok), vec("ln2_b", l), alpha, 1.0)
        sv["mx"] = dict(x1b=x1b, p=p, zoh_in=zoh_in, lbr_f=lbr_f, lbi_f=lbi_f, bblk_r=bblk_r, bblk_i=bblk_i, cblkT_r=cblkT_r,
                        cblkT_in=cblkT_in, sr=sr, si=si, dflat=dflat, y_raw=y_raw, yg=yg, ygb=ygb, lp=lp, ysb=ysb, qkv=qkv,
                        blt=blt, aint=aint, alog=alog, dtb=dtb, nw=nw, og=og, ssave=ssave, a_s=a_s, a_g=a_g, merged=merged,
                        xhat=xh2, rstd=r2)

        gate2, up2, hh2 = _ffn_up("ffn_up", x2b, wgu2)
        x3, x3b, xh3, r3 = _mm_ln("ffn_down_ln", hh2, wd2, x2, vec("ln3_g", l), vec("ln3_b", l), alpha, 0.5)
        sv["f2"] = dict(xb=x2b, gate=gate2, up=up2, h=hh2, xhat=xh3, rstd=r3)
        saves.append(sv)
        xc, xcb = x3, x3b
        if nxt:
            nxt.wait(x3)
            gathered = nxt.result()

    dy, loss_part = _loss_head("loss_head", xc, tg)
    loss = lax.psum(loss_part[0, 0], ("x", "y", "c"))

    big_out = {n: [lax.empty(W[n].shape, f32) for _ in range(4)] for n in _BIG}
    small_g = {n: [None] * L for n in _SMALL}
    pend = None
    for l in reversed(range(L)):
        sv, wl = saves[l], weights[l]
        mx = sv["mx"]
        p = mx["p"]
        tok = pend.start() if pend else None
        dx2, dwgu2, dwd2, dg3, db3 = _ffn_bwd("ffn_b", dy, sv["f2"], wl["wgu2"], wl["wd2"], dep(vec("ln3_g", l), tok), alpha)
        small_g["ln3_g"][l], small_g["ln3_b"][l] = dg3[0], db3[0]

        if pend:
            pend.wait(dx2)
            tok = pend.start()
        dz2, dmixb, dg2, db2 = _ln_bwd("mix_lnb", dx2, mx["xhat"], mx["rstd"], dep(vec("ln2_g", l), tok), 1.0)
        small_g["ln2_g"][l], small_g["ln2_b"][l] = dg2[0], db2[0]
        tmg, tkd = _rtile(T, 512), _tile(D, 512)
        tnq = 512 if (m_gs % 512 == 0 and D % 512 == 0) else DS
        tkq = _tile(D, 2048)
        gsb, ggb = m_gs // tnq, m_gg // tnq

        def dmerge_epi(acc, ex, outs):
            _, vjp = jax.vjp(_merge, ex[0][...], ex[1][...], ex[2][...], ex[3][...])
            dgs, dgg, das, dag = vjp(acc)
            outs[0][...] = das.astype(outs[0].dtype)
            outs[1][...] = dag.astype(outs[1].dtype)
            outs[2][...] = dgs.astype(outs[2].dtype)
            outs[3][...] = dgg.astype(outs[3].dtype)

        tile_ij = pl.BlockSpec((tmg, tnq), lambda i, j, k: (i, j))
        das, dag, dgs, dgg = _mm(
            "mix_dmerge", dmixb, wl["wo"], _NT, (T // tmg, D // tnq, D // tkq),
            pl.BlockSpec((tmg, tkq), lambda i, j, k: (i, k)), pl.BlockSpec((tnq, tkq), lambda i, j, k: (j, k)), (tmg, tnq),
            [p, p, mx["a_s"], mx["a_g"]],
            [pl.BlockSpec((tmg, tnq), lambda i, j, k: (i, j + gsb)), pl.BlockSpec((tmg, tnq), lambda i, j, k: (i, j + ggb)),
             tile_ij, tile_ij],
            [SDS((T, D), _MXU)] * 4, [tile_ij] * 4, dmerge_epi)
        dwo = _mm_tn("mix_dwo", mx["merged"], dmixb, _GDT)
        dys = _mm_nt_slots("br_ssm_dx", das, wl["wbs"], f32)
        dog = _mm_nt_slots("br_gdn_dx", dag, wl["wbg"], f32)
        dwbs = _mm_tn_slots("br_ssm_dw", mx["ysb"], das, _GDT)
        dwbg = _mm_tn_slots("br_gdn_dw", mx["og"], dag, _GDT)

        def glu_b_fn(dys_t, yg_t, lp_t):
            _, vjp = jax.vjp(_glu, yg_t, lp_t)
            dyg1, dlp = vjp(dys_t)
            return dyg1, dlp, dlp

        dyg1, dlpb, dglub = _colsum_kernel("s5_glu_b", glu_b_fn, [dys, mx["yg"], mx["lp"]], [0, 0, 0], [f32, _MXU], 1, SW)
        small_g["glu_b"][l] = dglub[0]
        dwglu = _mm_tn("s5_dwglu", mx["ygb"], dlpb, _GDT)
        tng, tkg = _tile(SW, 512), _tile(SW, 512)

        def dyraw_epi(acc, ex, outs):
            _, vjp = jax.vjp(jax.nn.gelu, ex[1][...])
            (d,) = vjp(ex[0][...] + acc)
            outs[0][...] = d

        t_ij = pl.BlockSpec((tmg, tng), lambda i, j, k: (i, j))
        (dyraw,) = _mm(
            "s5_dyraw", dlpb, wl["wglu"], _NT, (T // tmg, SW // tng, SW // tkg),
            pl.BlockSpec((tmg, tkg), lambda i, j, k: (i, k)), pl.BlockSpec((tng, tkg), lambda i, j, k: (j, k)), (tmg, tng),
            [dyg1, mx["y_raw"]], [t_ij, t_ij], [SDS((T, SW), f32)], [t_ij], dyraw_epi)

        def dd_fn(dyr, u_t, d_t):
            return d_t * dyr, dyr * u_t

        dud, dd = _colsum_kernel("s5_dd", dd_fn, [dyraw, p, mx["dflat"]], [0, 0, None], [f32], 1, SW)
        small_g["ssm_d"][l] = dd.reshape(G, H)
        dsr, dsi = _bd2("s5_ds", dyraw, 0, mx["cblkT_r"], mx["cblkT_in"])
        dcb_r, dcb_i = _bdT2("s5_dc", mx["sr"], mx["si"], 0, dyraw, dyraw, 0, 8 * P, 8 * H, J)
        small_g["ssm_c_re"][l] = _blockdiag_extract(jnp.transpose(dcb_r, (0, 2, 1)), H, P)
        small_g["ssm_c_im"][l] = -_blockdiag_extract(jnp.transpose(dcb_i, (0, 2, 1)), H, P)
        ar, ai, dlr, dli = _s5_scan_bwd("s5_scan_b", dsr, dsi, mx["sr"], mx["si"], mx["lbr_f"], mx["lbi_f"])

        def du_epi(acc, ex, outs):
            outs[0][...] = (acc + ex[0][...]).astype(outs[0].dtype)

        (du,) = _bd_sum(
            "s5_du", ar, ai, jnp.transpose(mx["bblk_r"], (0, 2, 1)), jnp.transpose(mx["bblk_i"], (0, 2, 1)), [dud],
            lambda tm, nb: [pl.BlockSpec((tm, nb), lambda i, j: (i, j))], [SDS((T, SW), _MXU)], du_epi)
        dbb_r, dbb_i = _bdT2("s5_db", p, p, 0, ar, ai, 0, 8 * H, 8 * P, J)
        dbbr_t = jnp.transpose(_blockdiag_extract(dbb_r, H, P), (1, 0, 2))
        dbbi_t = jnp.transpose(_blockdiag_extract(dbb_i, H, P), (1, 0, 2))
        da_re, da_im, dlog_dt, dbre_t, dbim_t = _zoh_bwd("zoh_b", *mx["zoh_in"], dlr.reshape(G, P), dli.reshape(G, P),
                                                         dbbr_t, dbbi_t)
        small_g["ssm_a_re"][l], small_g["ssm_a_im"][l], small_g["ssm_log_dt"][l] = da_re, da_im, dlog_dt[:, 0]
        small_g["ssm_b_re"][l] = jnp.transpose(dbre_t, (1, 2, 0))
        small_g["ssm_b_im"][l] = jnp.transpose(dbim_t, (1, 2, 0))

        dqkv3, dzb, dbl, dain, dal, ddtb, dnw = _gdn_bwd("gdn_b", mx["qkv"], p, m_z, mx["blt"], mx["aint"], mx["alog"],
                                                         mx["dtb"], mx["nw"], mx["ssave"], dog, NH, HD)
        small_g["gdn_a_log"][l], small_g["gdn_dt_bias"][l], small_g["gdn_norm_w"][l] = dal[:, 0, 0], ddtb[:, 0, 0], dnw[0]
        dqkv_pre, dcw = _conv_bwd("gdn_conv_b", p, m_qkv, wl["cw"], dqkv3)
        if pend:
            pend.wait(dqkv_pre)
            tok = pend.start()

        dpm = jnp.concatenate([du, dqkv_pre, dzb, dgs, dgg], axis=1)
        dpba = dep(jnp.concatenate([jnp.transpose(dbl[:, :, 0]), jnp.transpose(dain[:, :, 0]),
                                    jnp.zeros((T, 128 - 2 * NH), f32)], axis=1), tok).astype(_MXU)
        tnd, tkm = _tile(D, 1024), _tile(NM, 2304)
        t_ba = _mm(
            "mix_dx_ba", dpba, wl["wba"], _NT, (T // tmg, D // tnd, 1),
            pl.BlockSpec((tmg, 128), lambda i, j, k: (i, 0)), pl.BlockSpec((tnd, 128), lambda i, j, k: (j, 0)), (tmg, tnd),
            [], [], [SDS((T, D), f32)], [pl.BlockSpec((tmg, tnd), lambda i, j, k: (i, j))], _store_epi)[0]

        def dx1_epi(acc, ex, outs):
            outs[0][...] = alpha * ex[0][...] + ex[1][...] + acc

        t_d = pl.BlockSpec((tmg, tnd), lambda i, j, k: (i, j))
        (dx1,) = _mm(
            "mix_dx", dpm, wl["wmain"], _NT, (T // tmg, D // tnd, NM // tkm),
            pl.BlockSpec((tmg, tkm), lambda i, j, k: (i, k)), pl.BlockSpec((tnd, tkm), lambda i, j, k: (j, k)), (tmg, tnd),
            [dz2, t_ba], [t_d, t_d], [SDS((T, D), f32)], [t_d], dx1_epi)
        tnm = _tile(NM, 1024)
        tkt = _tile(T, 2048)
        dwmain = _mm(
            "mix_dw", jnp.transpose(mx["x1b"]), dpm, _NN, (D // tkd, NM // tnm, T // tkt),
            pl.BlockSpec((tkd, tkt), lambda i, j, k: (i, k)), pl.BlockSpec((tkt, tnm), lambda i, j, k: (k, j)), (tkd, tnm),
            [], [], [SDS((D, NM), _GDT)], [pl.BlockSpec((tkd, tnm), lambda i, j, k: (i, j))], _store_epi)[0]
        dwba = _mm_tn("mix_dw_ba", mx["x1b"], dpba, _GDT)
        dwin = jnp.concatenate([dwmain[:, :o_b], dwba[:, :2 * NH], dwmain[:, o_b:]], axis=1)
        dwin8 = jnp.transpose(dwin.reshape(D, 8, IN // 8), (1, 0, 2))
        dcw8 = jnp.transpose(dcw.reshape(KC, 8, 3 * GW // 8), (1, 0, 2))

        dx0, dwgu1, dwd1, dg1, db1 = _ffn_bwd("ffn_b", dx1, sv["f1"], wl["wgu1"], wl["wd1"], vec("ln1_g", l), alpha)
        small_g["ln1_g"][l], small_g["ln1_b"][l] = dg1[0], db1[0]
        dy = dx0

        parts = [dwgu1, dwd1.reshape(8, -1, D), dwin8, dcw8, dwglu.reshape(8, SW // 8, SW), dwbs, dwbg,
                 dwo.reshape(8, DS, D), dwgu2, dwd2.reshape(8, -1, D)]
        late = None
        if pend:
            pend.wait(dx0)
            if l > 0:
                for n, (full, recv, last) in zip(_BIG, pend.out):
                    big_out[n] = _adamw_big("adamw_" + n, full, recv, _coord(last), W[n], M[n], V[n], l + 1, big_out[n])
            else:
                late = pend.out
        pend = _AsyncReduceScatter("rsp", parts, _PATHS)

    seg = 8 * 128

    def padded(n):
        return -(-n // seg) * seg

    def pack(arrs):
        flat = jnp.concatenate([jnp.pad(a.reshape(-1), (0, padded(a.size) - a.size)) for a in arrs])
        n = flat.shape[0]
        rows = -(-n // (128 * 512)) * 512
        return jnp.pad(flat, (0, rows * 128 - n)).reshape(rows, 128)

    tok = pend.start()
    gs_full = [jnp.stack(small_g[n]).reshape(W[n].shape) for n in _SMALL]
    gpack = dep(pack(gs_full), tok)
    (gall,) = _all_gather("ag_small", [gpack], ["yx"])
    pend.wait(gall)
    tok = pend.start()
    if late:
        for n, (full, recv, last) in zip(_BIG, late):
            big_out[n] = _adamw_big("adamw_" + n, full, dep(recv, tok), _coord(last), W[n], M[n], V[n], 1, big_out[n])
    pend.wait([gall] + ([big_out[n][0] for n in _BIG] if late else []))
    tok = pend.start()
    sg, sd, sm, sv_ = _adamw_small("adamw_small", gall, dep(pack([W[n] for n in _SMALL]), tok),
                                   pack([M[n] for n in _SMALL]), pack([V[n] for n in _SMALL]))
    pend.wait(sv_)
    for n, (full, recv, last) in zip(_BIG, pend.out):
        big_out[n] = _adamw_big("adamw_" + n, full, recv, _coord(last), W[n], M[n], V[n], 0, big_out[n])

    def unpack(packed):
        out, row = {}, 0
        for n in _SMALL:
            sz = math.prod(W[n].shape)
            rows = padded(sz) // 128
            out[n] = packed[row:row + rows].reshape(-1)[:sz].reshape(W[n].shape)
            row += rows
        return out

    res = [unpack(a) for a in (sg, sd, sm, sv_)]
    for n in _BIG:
        for i in range(4):
            res[i][n] = big_out[n][i]
    outs = [loss, dy[None]]
    for i in range(4):
        outs += [res[i][n] for n in _ORDER]
    return tuple(outs)


def kernel(x, ffn1_w_gu, ffn1_w_down, ln1_g, ln1_b, w_in, conv_w, ssm_a_re, ssm_a_im, ssm_log_dt, ssm_b_re, ssm_b_im, ssm_c_re, ssm_c_im, ssm_d, glu_w, glu_b, gdn_a_log, gdn_dt_bias, gdn_norm_w, w_br_ssm, w_br_gdn, w_out, ln2_g, ln2_b, ffn2_w_gu, ffn2_w_down, ln3_g, ln3_b, loss_target, m_ffn1_w_gu, m_ffn1_w_down, m_ln1_g, m_ln1_b, m_w_in, m_conv_w, m_ssm_a_re, m_ssm_a_im, m_ssm_log_dt, m_ssm_b_re, m_ssm_b_im, m_ssm_c_re, m_ssm_c_im, m_ssm_d, m_glu_w, m_glu_b, m_gdn_a_log, m_gdn_dt_bias, m_gdn_norm_w, m_w_br_ssm, m_w_br_gdn, m_w_out, m_ln2_g, m_ln2_b, m_ffn2_w_gu, m_ffn2_w_down, m_ln3_g, m_ln3_b, v_ffn1_w_gu, v_ffn1_w_down, v_ln1_g, v_ln1_b, v_w_in, v_conv_w, v_ssm_a_re, v_ssm_a_im, v_ssm_log_dt, v_ssm_b_re, v_ssm_b_im, v_ssm_c_re, v_ssm_c_im, v_ssm_d, v_glu_w, v_glu_b, v_gdn_a_log, v_gdn_dt_bias, v_gdn_norm_w, v_w_br_ssm, v_w_br_gdn, v_w_out, v_ln2_g, v_ln2_b, v_ffn2_w_gu, v_ffn2_w_down, v_ln3_g, v_ln3_b):
    given = dict(locals())
    W = {n: given[n] for n in _ORDER}
    M = {n: given["m_" + n] for n in _ORDER}
    V = {n: given["v_" + n] for n in _ORDER}
    return _step(x, loss_target, W, M, V)
```

```python
import functools
import math

import jax
import jax.numpy as jnp
from jax import lax
from jax.experimental import pallas as pl
from jax.experimental.pallas import tpu as pltpu

f32 = jnp.float32
_MXU = jnp.bfloat16
_GDT = jnp.bfloat16
_HP = lax.Precision.HIGHEST
_VMEM_LIMIT = 56 * 1024 * 1024
_MESH_T = pl.DeviceIdType.MESH

LN_EPS = 1e-5
RMS_EPS = 1e-6
L2_EPS = 1e-6
CHUNK = 64
ADAM_LR = 0.001
ADAM_B1 = 0.9
ADAM_B2 = 0.999
ADAM_EPS = 1e-08
ADAM_WD = 0.01
ADAM_STEP = 10

_NN = (((1,), (0,)), ((), ()))
_NT = (((1,), (1,)), ((), ()))
_TN = (((0,), (0,)), ((), ()))

SDS = jax.ShapeDtypeStruct


def _cp(sem):
    return pltpu.CompilerParams(dimension_semantics=sem, vmem_limit_bytes=_VMEM_LIMIT)


def _tile(n, pref):
    if n <= pref:
        return n
    t = (pref // 128) * 128
    while t >= 128:
        if n % t == 0:
            return t
        t -= 128
    return n


def _rtile(n, pref):
    if n <= pref:
        return n
    t = (pref // 16) * 16
    while t >= 16:
        if n % t == 0:
            return t
        t -= 16
    return n


def _mm(name, a, b, dims, grid, a_spec, b_spec, acc_shape, extras, extra_specs, out_shape, out_specs, epilogue):
    nk = grid[2]
    ne = len(extras)
    no = len(out_shape)

    def body(*refs):
        a_ref, b_ref = refs[0], refs[1]
        ex = refs[2:2 + ne]
        outs = refs[2 + ne:2 + ne + no]
        acc = refs[-1]
        k = pl.program_id(2)
        part = lax.dot_general(a_ref[...].astype(_MXU), b_ref[...].astype(_MXU), dims, preferred_element_type=f32)

        @pl.when(k == 0)
        def _():
            acc[...] = part

        @pl.when(k > 0)
        def _():
            acc[...] += part

        @pl.when(k == nk - 1)
        def _():
            epilogue(acc[...], ex, outs)

    return pl.pallas_call(
        body, grid=grid, in_specs=[a_spec, b_spec, *extra_specs], out_specs=list(out_specs), out_shape=list(out_shape),
        scratch_shapes=[pltpu.VMEM(acc_shape, f32)], compiler_params=_cp(("parallel", "parallel", "arbitrary")), name=name,
    )(a, b, *extras)


def _store_epi(acc, ex, outs):
    for o in outs:
        o[...] = acc.astype(o.dtype)


def _ln_epilogue(alpha, c):
    def epi(acc, ex, outs):
        x_ref, g_ref, b_ref = ex
        y_ref, yb_ref, xh_ref, r_ref = outs
        z = alpha * x_ref[...] + c * acc
        mu = jnp.mean(z, axis=-1, keepdims=True)
        zc = z - mu
        var = jnp.mean(zc * zc, axis=-1, keepdims=True)
        r = lax.rsqrt(var + LN_EPS)
        xh = zc * r
        y = xh * g_ref[...] + b_ref[...]
        y_ref[...] = y
        yb_ref[...] = y.astype(yb_ref.dtype)
        xh_ref[...] = xh
        r_ref[...] = r
    return epi


def _mm_ln(name, a, w, x, g, b, alpha, c):
    T, K = a.shape
    D = w.shape[1]
    tm, tk = _rtile(T, 512), _tile(K, 512)
    row = pl.BlockSpec((tm, D), lambda i, j, k: (i, 0))
    vec = pl.BlockSpec((1, D), lambda i, j, k: (0, 0))
    return _mm(
        name, a, w, _NN, (T // tm, 1, K // tk),
        pl.BlockSpec((tm, tk), lambda i, j, k: (i, k)), pl.BlockSpec((tk, D), lambda i, j, k: (k, 0)), (tm, D),
        [x, g, b], [row, vec, vec],
        [SDS((T, D), f32), SDS((T, D), _MXU), SDS((T, D), f32), SDS((T, 1), f32)],
        [row, row, row, pl.BlockSpec((tm, 1), lambda i, j, k: (i, 0))],
        _ln_epilogue(alpha, c),
    )


def _ln_bwd(name, dy, xhat, rstd, g, c):
    T, D = dy.shape
    tm = _rtile(T, 256)

    def body(dy_ref, xh_ref, r_ref, g_ref, dz_ref, df_ref, dg_ref, db_ref):
        i = pl.program_id(0)
        dyv = dy_ref[...]
        xh = xh_ref[...]
        dxh = dyv * g_ref[...]
        m1 = jnp.mean(dxh, axis=-1, keepdims=True)
        m2 = jnp.mean(dxh * xh, axis=-1, keepdims=True)
        dz = r_ref[...] * (dxh - m1 - xh * m2)
        dz_ref[...] = dz
        df_ref[...] = (c * dz).astype(df_ref.dtype)
        pg = jnp.sum(dyv * xh, axis=0, keepdims=True)
        pb = jnp.sum(dyv, axis=0, keepdims=True)

        @pl.when(i == 0)
        def _():
            dg_ref[...] = pg
            db_ref[...] = pb

        @pl.when(i > 0)
        def _():
            dg_ref[...] += pg
            db_ref[...] += pb

    row = pl.BlockSpec((tm, D), lambda i: (i, 0))
    vec = pl.BlockSpec((1, D), lambda i: (0, 0))
    return pl.pallas_call(
        body, grid=(T // tm,), in_specs=[row, row, pl.BlockSpec((tm, 1), lambda i: (i, 0)), vec],
        out_specs=[row, row, vec, vec],
        out_shape=[SDS((T, D), f32), SDS((T, D), _MXU), SDS((1, D), f32), SDS((1, D), f32)],
        compiler_params=_cp(("arbitrary",)), name=name,
    )(dy, xhat, rstd, g)


def _swiglu(g, u):
    return jax.nn.silu(g) * u


def _ffn_up(name, xb, wgu):
    T, D = xb.shape
    FS = wgu.shape[2]
    F = 4 * FS
    tm = _rtile(T, 256)

    def body(x_ref, wg_ref, wu_ref, g_ref, u_ref, h_ref):
        xv = x_ref[...]
        g = jnp.dot(xv, wg_ref[...], preferred_element_type=f32)
        u = jnp.dot(xv, wu_ref[...], preferred_element_type=f32)
        g_ref[...] = g.astype(g_ref.dtype)
        u_ref[...] = u.astype(u_ref.dtype)
        h_ref[...] = _swiglu(g, u).astype(h_ref.dtype)

    out = pl.BlockSpec((tm, FS), lambda j, i: (i, j))
    return pl.pallas_call(
        body, grid=(4, T // tm),
        in_specs=[pl.BlockSpec((tm, D), lambda j, i: (i, 0)),
                  pl.BlockSpec((None, D, FS), lambda j, i: (j, 0, 0)),
                  pl.BlockSpec((None, D, FS), lambda j, i: (j + 4, 0, 0))],
        out_specs=[out, out, out],
        out_shape=[SDS((T, F), _MXU), SDS((T, F), _MXU), SDS((T, F), _MXU)],
        compiler_params=_cp(("parallel", "arbitrary")), name=name,
    )(xb, wgu, wgu)


def _ffn_bwd(pfx, dy, sv, wgu, wd, g_ln, alpha):
    T, D = dy.shape
    FS = wgu.shape[2]
    F = 4 * FS
    dz, dfb, dg, db = _ln_bwd(pfx + "_lnb", dy, sv["xhat"], sv["rstd"], g_ln, 0.5)

    tm, tn, tk = _rtile(T, 1024), _tile(F, 512), _tile(D, 2048)

    def epi(acc, ex, outs):
        g_ref, u_ref = ex
        _, vjp = jax.vjp(_swiglu, g_ref[...].astype(f32), u_ref[...].astype(f32))
        dgate, dup = vjp(acc)
        outs[0][0] = dgate.astype(outs[0].dtype)
        outs[0][1] = dup.astype(outs[0].dtype)

    gu = pl.BlockSpec((tm, tn), lambda i, j, k: (i, j))
    (dgu,) = _mm(
        pfx + "_dh", dfb, wd, _NT, (T // tm, F // tn, D // tk),
        pl.BlockSpec((tm, tk), lambda i, j, k: (i, k)), pl.BlockSpec((tn, tk), lambda i, j, k: (j, k)), (tm, tn),
        [sv["gate"], sv["up"]], [gu, gu],
        [SDS((2, T, F), _MXU)], [pl.BlockSpec((2, tm, tn), lambda i, j, k: (0, i, j))], epi,
    )

    tm2, tk2 = _tile(F, 512), _rtile(T, 512)
    (dwd,) = _mm(
        pfx + "_dwd", sv["h"], dfb, _TN, (F // tm2, 1, T // tk2),
        pl.BlockSpec((tk2, tm2), lambda i, j, k: (k, i)), pl.BlockSpec((tk2, D), lambda i, j, k: (k, 0)), (tm2, D),
        [], [], [SDS((F, D), _GDT)], [pl.BlockSpec((tm2, D), lambda i, j, k: (i, 0))], _store_epi,
    )

    tn3 = _tile(D, 1024)

    def epi3(acc, ex, outs):
        outs[0][...] = alpha * ex[0][...] + acc

    (dx,) = _mm(
        pfx + "_dx", dgu, wgu, _NT, (T // tm, D // tn3, 8),
        pl.BlockSpec((None, tm, FS), lambda i, j, k: (k // 4, i, k % 4)),
        pl.BlockSpec((None, tn3, FS), lambda i, j, k: (k, j, 0)), (tm, tn3),
        [dz], [pl.BlockSpec((tm, tn3), lambda i, j, k: (i, j))],
        [SDS((T, D), f32)], [pl.BlockSpec((tm, tn3), lambda i, j, k: (i, j))], epi3,
    )

    tm4, tk4 = _rtile(D, 512), _tile(T, 2048)
    (dwgu,) = _mm(
        pfx + "_dwgu", jnp.transpose(sv["xb"]), dgu, _NN, (D // tm4, 8, T // tk4),
        pl.BlockSpec((tm4, tk4), lambda i, j, k: (i, k)),
        pl.BlockSpec((None, tk4, FS), lambda i, j, k: (j // 4, k, j % 4)), (tm4, FS),
        [], [], [SDS((8, D, FS), _GDT)], [pl.BlockSpec((None, tm4, FS), lambda i, j, k: (j, i, 0))], _store_epi,
    )
    return dx, dwgu, dwd, dg, db


def _zoh(a_re, a_im, log_dt, b_re_t, b_im_t):
    dt = jnp.exp(log_dt)
    mag = jnp.exp(a_re * dt)
    lr_, li_ = mag * jnp.cos(a_im * dt), mag * jnp.sin(a_im * dt)
    den = a_re * a_re + a_im * a_im
    pr, pi = lr_ - 1.0, li_
    qr, qi = a_re / den, -a_im / den
    zr, zi = pr * qr - pi * qi, pr * qi + pi * qr
    bbr = zr[None] * b_re_t - zi[None] * b_im_t
    bbi = zr[None] * b_im_t + zi[None] * b_re_t
    return lr_, li_, bbr, bbi


def _zoh_fwd(name, a_re, a_im, log_dt, b_re_t, b_im_t):
    G, P = a_re.shape
    H = b_re_t.shape[0]

    def body(ar, ai, ld, br, bi, o1, o2, o3, o4):
        r = _zoh(ar[...], ai[...], ld[...], br[...], bi[...])
        o1[...], o2[...], o3[...], o4[...] = r

    return pl.pallas_call(
        body, out_shape=[SDS((G, P), f32), SDS((G, P), f32), SDS((H, G, P), f32), SDS((H, G, P), f32)], name=name,
    )(a_re, a_im, log_dt, b_re_t, b_im_t)


def _zoh_bwd(name, a_re, a_im, log_dt, b_re_t, b_im_t, dlr, dli, dbbr, dbbi):
    G, P = a_re.shape
    H = b_re_t.shape[0]

    def body(ar, ai, ld, br, bi, g1, g2, g3, g4, o1, o2, o3, o4, o5):
        _, vjp = jax.vjp(_zoh, ar[...], ai[...], ld[...], br[...], bi[...])
        r = vjp((g1[...], g2[...], g3[...], g4[...]))
        o1[...], o2[...], o3[...], o4[...], o5[...] = r

    return pl.pallas_call(
        body, out_shape=[SDS((G, P), f32), SDS((G, P), f32), SDS((G, 1), f32), SDS((H, G, P), f32), SDS((H, G, P), f32)],
        name=name,
    )(a_re, a_im, log_dt, b_re_t, b_im_t, dlr, dli, dbbr, dbbi)


def _blockdiag(m):
    G, A, B = m.shape
    eye = jnp.eye(8, dtype=bool)
    m4 = m.reshape(G // 8, 8, A, B)
    out = jnp.where(eye[None, :, None, :, None], m4[:, :, :, None, :], jnp.zeros((), m.dtype))
    return out.reshape(G // 8, 8 * A, 8 * B)


def _blockdiag_extract(mb, A, B):
    J = mb.shape[0]
    m5 = mb.reshape(J, 8, A, 8, B)
    d = jnp.stack([m5[:, i, :, i, :] for i in range(8)], axis=1)
    return d.reshape(J * 8, A, B)


def _bd2(name, a, a_col0, b1, b2, out_dtype=f32):
    T = a.shape[0]
    J, KA, NB = b1.shape
    tm = _rtile(T, 512)

    def body(a_ref, b1_ref, b2_ref, o1, o2):
        av = a_ref[...].astype(_MXU)
        o1[...] = jnp.dot(av, b1_ref[...].astype(_MXU), preferred_element_type=f32).astype(o1.dtype)
        o2[...] = jnp.dot(av, b2_ref[...].astype(_MXU), preferred_element_type=f32).astype(o2.dtype)

    bs = pl.BlockSpec((None, KA, NB), lambda i, j: (j, 0, 0))
    os_ = pl.BlockSpec((tm, NB), lambda i, j: (i, j))
    return pl.pallas_call(
        body, grid=(T // tm, J), in_specs=[pl.BlockSpec((tm, KA), lambda i, j: (i, j + a_col0)), bs, bs],
        out_specs=[os_, os_], out_shape=[SDS((T, J * NB), out_dtype)] * 2,
        compiler_params=_cp(("parallel", "parallel")), name=name,
    )(a, b1, b2)


def _bd_sum(name, a1, a2, b1, b2, extras, extra_specs_fn, out_shape, epilogue):
    T = a1.shape[0]
    J, KA, NB = b1.shape
    tm = _rtile(T, 512)
    ne = len(extras)

    def body(*refs):
        a1_ref, a2_ref, b1_ref, b2_ref = refs[:4]
        ex = refs[4:4 + ne]
        outs = refs[4 + ne:]
        acc = jnp.dot(a1_ref[...].astype(_MXU), b1_ref[...].astype(_MXU), preferred_element_type=f32)
        acc = acc + jnp.dot(a2_ref[...].astype(_MXU), b2_ref[...].astype(_MXU), preferred_element_type=f32)
        epilogue(acc, ex, outs)

    as_ = pl.BlockSpec((tm, KA), lambda i, j: (i, j))
    bs = pl.BlockSpec((None, KA, NB), lambda i, j: (j, 0, 0))
    os_ = pl.BlockSpec((tm, NB), lambda i, j: (i, j))
    return pl.pallas_call(
        body, grid=(T // tm, J), in_specs=[as_, as_, bs, bs, *extra_specs_fn(tm, NB)],
        out_specs=[os_] * len(out_shape), out_shape=list(out_shape),
        compiler_params=_cp(("parallel", "parallel")), name=name,
    )(a1, a2, b1, b2, *extras)


def _bdT2(name, a1, a2, a_col0, b1, b2, b_col0, KA, NB, J):
    T = a1.shape[0]
    tk = _rtile(T, 512)

    def body(a1_ref, a2_ref, b1_ref, b2_ref, o1, o2):
        k = pl.program_id(1)
        p1 = lax.dot_general(a1_ref[...].astype(_MXU), b1_ref[...].astype(_MXU), _TN, preferred_element_type=f32)
        p2 = lax.dot_general(a2_ref[...].astype(_MXU), b2_ref[...].astype(_MXU), _TN, preferred_element_type=f32)

        @pl.when(k == 0)
        def _():
            o1[...] = p1
            o2[...] = p2

        @pl.when(k > 0)
        def _():
            o1[...] += p1
            o2[...] += p2

    as_ = pl.BlockSpec((tk, KA), lambda j, k: (k, j + a_col0))
    bs = pl.BlockSpec((tk, NB), lambda j, k: (k, j + b_col0))
    os_ = pl.BlockSpec((None, KA, NB), lambda j, k: (j, 0, 0))
    return pl.pallas_call(
        body, grid=(J, T // tk), in_specs=[as_, as_, bs, bs], out_specs=[os_, os_],
        out_shape=[SDS((J, KA, NB), f32)] * 2, compiler_params=_cp(("parallel", "arbitrary")), name=name,
    )(a1, a2, b1, b2)


_RB = 8


def _cmul(ar, ai, br, bi):
    return ar * br - ai * bi, ar * bi + ai * br


def _lam_powers(lr_v, li_v, cb):
    pw = {1: (lr_v, li_v)}
    for k in range(2, _RB + 1):
        pw[k] = _cmul(*pw[k - 1], lr_v, li_v)
    return pw


def _row_powers(pw, row, cb, reverse):
    outr = jnp.zeros((_RB, cb), f32)
    outi = jnp.zeros((_RB, cb), f32)
    for r in range(_RB):
        k = _RB - r if reverse else r + 1
        outr = jnp.where(row == r, pw[k][0], outr)
        outi = jnp.where(row == r, pw[k][1], outi)
    return outr, outi


def _tile_scan(xr, xi, pw, row, reverse):
    for k in (1, 2, 4):
        if reverse:
            keep = row < _RB - k
            shr, shi = pltpu.roll(xr, _RB - k, 0), pltpu.roll(xi, _RB - k, 0)
        else:
            keep = row >= k
            shr, shi = pltpu.roll(xr, k, 0), pltpu.roll(xi, k, 0)
        shr, shi = jnp.where(keep, shr, 0.0), jnp.where(keep, shi, 0.0)
        mr, mi = pw[k]
        xr, xi = xr + (mr * shr - mi * shi), xi + (mr * shi + mi * shr)
    return xr, xi


def _s5_scan(name, bur, bui, lr_, li_):
    T, N = bur.shape
    cb = _tile(N, 512)

    def body(br_ref, bi_ref, lr_ref, li_ref, sr_ref, si_ref):
        pw = _lam_powers(lr_ref[...], li_ref[...], cb)
        row = lax.broadcasted_iota(jnp.int32, (_RB, cb), 0)
        cr, ci = _row_powers(pw, row, cb, False)

        def step(n, carry):
            pr, pi = carry
            t0 = pl.multiple_of(n * _RB, _RB)
            xr, xi = _tile_scan(br_ref[pl.ds(t0, _RB), :], bi_ref[pl.ds(t0, _RB), :], pw, row, False)
            xr, xi = xr + (cr * pr - ci * pi), xi + (cr * pi + ci * pr)
            sr_ref[pl.ds(t0, _RB), :] = xr
            si_ref[pl.ds(t0, _RB), :] = xi
            return xr[_RB - 1:_RB, :], xi[_RB - 1:_RB, :]

        z = jnp.zeros((1, cb), f32)
        lax.fori_loop(0, T // _RB, step, (z, z))

    col = pl.BlockSpec((T, cb), lambda j: (0, j))
    vec = pl.BlockSpec((1, cb), lambda j: (0, j))
    return pl.pallas_call(
        body, grid=(N // cb,), in_specs=[col, col, vec, vec], out_specs=[col, col],
        out_shape=[SDS((T, N), f32)] * 2, compiler_params=_cp(("parallel",)), name=name,
    )(bur, bui, lr_, li_)


def _s5_scan_bwd(name, dsr, dsi, sr, si, lr_, li_):
    T, N = dsr.shape
    cb = _tile(N, 256)

    def body(dr_ref, di_ref, sr_ref, si_ref, lr_ref, li_ref, ar_ref, ai_ref, glr_ref, gli_ref):
        pw = _lam_powers(lr_ref[...], -li_ref[...], cb)
        row = lax.broadcasted_iota(jnp.int32, (_RB, cb), 0)
        cr, ci = _row_powers(pw, row, cb, True)
        NT = T // _RB

        def tile(t0, nxt, prev_last):
            xr, xi = _tile_scan(dr_ref[pl.ds(t0, _RB), :], di_ref[pl.ds(t0, _RB), :], pw, row, True)
            xr, xi = xr + (cr * nxt[0] - ci * nxt[1]), xi + (cr * nxt[1] + ci * nxt[0])
            ar_ref[pl.ds(t0, _RB), :] = xr
            ai_ref[pl.ds(t0, _RB), :] = xi
            pr = jnp.where(row == 0, prev_last[0], pltpu.roll(sr_ref[pl.ds(t0, _RB), :], 1, 0))
            pi = jnp.where(row == 0, prev_last[1], pltpu.roll(si_ref[pl.ds(t0, _RB), :], 1, 0))
            return xr, xi, xr * pr + xi * pi, xi * pr - xr * pi

        def step(n, carry):
            nr, ni, glr, gli = carry
            t0 = pl.multiple_of((NT - 1 - n) * _RB, _RB)
            tp = pl.multiple_of((NT - 2 - n) * _RB, _RB)
            prev_last = (sr_ref[pl.ds(tp, _RB), :][_RB - 1:_RB, :], si_ref[pl.ds(tp, _RB), :][_RB - 1:_RB, :])
            xr, xi, gr, gi = tile(t0, (nr, ni), prev_last)
            return xr[0:1, :], xi[0:1, :], glr + gr, gli + gi

        z1 = jnp.zeros((1, cb), f32)
        z8 = jnp.zeros((_RB, cb), f32)
        nr, ni, glr, gli = lax.fori_loop(0, NT - 1, step, (z1, z1, z8, z8))
        _, _, gr, gi = tile(0, (nr, ni), (z1, z1))
        glr_ref[...] = jnp.sum(glr + gr, axis=0, keepdims=True)
        gli_ref[...] = jnp.sum(gli + gi, axis=0, keepdims=True)

    col = pl.BlockSpec((T, cb), lambda j: (0, j))
    vec = pl.BlockSpec((1, cb), lambda j: (0, j))
    return pl.pallas_call(
        body, grid=(N // cb,), in_specs=[col, col, col, col, vec, vec], out_specs=[col, col, vec, vec],
        out_shape=[SDS((T, N), f32), SDS((T, N), f32), SDS((1, N), f32), SDS((1, N), f32)],
        compiler_params=_cp(("parallel",)), name=name,
    )(dsr, dsi, sr, si, lr_, li_)


def _conv_fwd(name, p, col0, w, GW3):
    T = p.shape[0]
    K = w.shape[0]
    cb = 128
    c0 = col0 // cb

    def body(x_ref, w_ref, o_ref, pad_ref):
        pad_ref[pl.ds(0, 8), :] = jnp.zeros((8, cb), f32)
        pad_ref[pl.ds(8, T), :] = x_ref[...]
        wv = w_ref[...]
        acc = jnp.zeros((T, cb), f32)
        for j in range(K):
            acc = acc + wv[j:j + 1, :] * pad_ref[pl.ds(8 - (K - 1) + j, T), :]
        o_ref[...] = jax.nn.silu(acc)

    return pl.pallas_call(
        body, grid=(GW3 // cb,),
        in_specs=[pl.BlockSpec((T, cb), lambda j: (0, j + c0)), pl.BlockSpec((K, cb), lambda j: (0, j))],
        out_specs=pl.BlockSpec((T, cb), lambda j: (0, j)), out_shape=SDS((T, GW3), f32),
        scratch_shapes=[pltpu.VMEM((T + 8, cb), f32)], compiler_params=_cp(("parallel",)), name=name,
    )(p, w)


def _conv_bwd(name, p, col0, w, dout3):
    T = p.shape[0]
    K = w.shape[0]
    GW = dout3.shape[2]
    GW3 = 3 * GW
    cb = 128
    c0 = col0 // cb
    nb = GW // cb

    def body(x_ref, w_ref, d_ref, dx_ref, dw_ref, pad_ref, dpad_ref):
        pad_ref[pl.ds(0, 8), :] = jnp.zeros((8, cb), f32)
        pad_ref[pl.ds(8, T), :] = x_ref[...]
        wv = w_ref[...]
        pre = jnp.zeros((T, cb), f32)
        for j in range(K):
            pre = pre + wv[j:j + 1, :] * pad_ref[pl.ds(8 - (K - 1) + j, T), :]
        _, vjp = jax.vjp(jax.nn.silu, pre)
        (dpre,) = vjp(d_ref[...])
        dpad_ref[pl.ds(0, T), :] = dpre
        dpad_ref[pl.ds(T, 8), :] = jnp.zeros((8, cb), f32)
        dx = jnp.zeros((T, cb), f32)
        rows = []
        for j in range(K):
            dx = dx + wv[j:j + 1, :] * dpad_ref[pl.ds((K - 1) - j, T), :]
            rows.append(jnp.sum(dpre * pad_ref[pl.ds(8 - (K - 1) + j, T), :], axis=0, keepdims=True))
        dx_ref[...] = dx.astype(dx_ref.dtype)
        for j in range(K):
            dw_ref[pl.ds(j, 1), :] = rows[j]

    return pl.pallas_call(
        body, grid=(GW3 // cb,),
        in_specs=[pl.BlockSpec((T, cb), lambda j: (0, j + c0)), pl.BlockSpec((K, cb), lambda j: (0, j)),
                  pl.BlockSpec((None, T, cb), lambda j: (j // nb, 0, j % nb))],
        out_specs=[pl.BlockSpec((T, cb), lambda j: (0, j)), pl.BlockSpec((K, cb), lambda j: (0, j))],
        out_shape=[SDS((T, GW3), _MXU), SDS((K, GW3), f32)],
        scratch_shapes=[pltpu.VMEM((T + 8, cb), f32), pltpu.VMEM((T + 8, cb), f32)],
        compiler_params=_cp(("parallel",)), name=name,
    )(p, w, dout3)


def _hdot(a, b, dims=_NN):
    return lax.dot_general(a, b, dims, precision=_HP, preferred_element_type=f32)


def _split(a):
    hi = a.astype(jnp.bfloat16)
    lo = (a - hi.astype(f32)).astype(jnp.bfloat16)
    return hi, lo


_BNN = (((2,), (1,)), ((0,), (0,)))
_BNT = (((2,), (2,)), ((0,), (0,)))
_BTN = (((1,), (1,)), ((0,), (0,)))


def _dot3_raw(a, b, dims):
    ah, al = _split(a)
    bh, bl = _split(b)
    d = functools.partial(lax.dot_general, dimension_numbers=dims, preferred_element_type=f32)
    return d(ah, bh) + (d(al, bh) + d(ah, bl))


@jax.custom_vjp
def _dot3(a, b):
    return _dot3_raw(a, b, _BNN)


def _dot3_fwd(a, b):
    return _dot3_raw(a, b, _BNN), (a, b)


def _dot3_bwd(res, g):
    a, b = res
    return _dot3_raw(g, b, _BNT), _dot3_raw(a, g, _BTN)


_dot3.defvjp(_dot3_fwd, _dot3_bwd)


def _ldot(a, b, dims=_BNN):
    return lax.dot_general(a.astype(_MXU), b.astype(_MXU), dims, preferred_element_type=f32)


def _sdot(a, b):
    return _ldot(a, b)


def _gdn_chunk(S, q, k, v, z, bl, ain, alog, dtb, nw):
    H, C, d = q.shape
    ri = lax.broadcasted_iota(jnp.int32, (H, C, C), 1)
    ci = lax.broadcasted_iota(jnp.int32, (H, C, C), 2)
    causal = ri >= ci
    strict = ri > ci
    tri = causal.astype(f32)
    qn = q * lax.rsqrt(jnp.sum(q * q, axis=-1, keepdims=True) + L2_EPS) * (d ** -0.5)
    kn = k * lax.rsqrt(jnp.sum(k * k, axis=-1, keepdims=True) + L2_EPS)
    beta = jax.nn.sigmoid(bl)
    g = -jnp.exp(alog) * jax.nn.softplus(ain + dtb)
    gb = jnp.broadcast_to(g, (H, C, C))
    gc_col = _dot3(tri, gb)
    gc_row = _dot3(jnp.ones((H, C, C), f32), jnp.where(ri <= ci, gb, 0.0))
    diff = jnp.where(causal, gc_col - gc_row, 0.0)
    decay = jnp.where(causal, jnp.exp(diff), 0.0)
    gcum = gc_col[:, :, 0:1]
    glast = gc_col[:, C - 1:C, 0:1]
    egc = jnp.exp(gcum)
    kb = kn * beta
    lower = jnp.where(strict, _ldot(kb, kn, _BNT) * decay, 0.0)
    x = jnp.concatenate([v * beta, kb * egc], axis=-1)
    m = -lower
    for it in range(6):
        x = x + _sdot(m, x)
        if it < 5:
            m = _sdot(m, m)
    u_val, w_key = x[:, :, :d], x[:, :, d:]
    attn = _ldot(qn, kn, _BNT) * decay
    q_dec = qn * egc
    k_dec = kn * jnp.exp(glast - gcum)
    v_new = u_val - _ldot(w_key, S)
    out = _ldot(q_dec, S) + _ldot(attn, v_new)
    s_new = S * jnp.exp(glast) + _ldot(k_dec, v_new, _BTN)
    o = out * lax.rsqrt(jnp.mean(out * out, axis=-1, keepdims=True) + RMS_EPS) * nw
    o = o * jax.nn.silu(z)
    return s_new, o


def _heads_per_step(NH, HD, zcol0):
    for hb in (8, 4, 2):
        if NH % hb == 0 and zcol0 % (hb * HD) == 0:
            return hb
    return 1


def _gdn_fwd(name, qkv, p, zcol0, blt, aint, alog, dtb, nw, NH, HD):
    T = qkv.shape[0]
    N = T // CHUNK
    GW = NH * HD
    HB = _heads_per_step(NH, HD, zcol0)
    W = HB * HD
    zc0 = zcol0 // W
    nb = GW // W

    def body(q_ref, k_ref, v_ref, z_ref, bl_ref, ain_ref, al_ref, dtb_ref, nw_ref, o_ref, ssave_ref, s_scr):
        n = pl.program_id(1)

        @pl.when(n == 0)
        def _():
            s_scr[...] = jnp.zeros_like(s_scr)

        heads = lambda r: jnp.stack([r[:, hh * HD:(hh + 1) * HD] for hh in range(HB)], axis=0)
        s_in = s_scr[...]
        ssave_ref[...] = s_in
        s_new, o = _gdn_chunk(s_in, heads(q_ref), heads(k_ref), heads(v_ref), heads(z_ref), bl_ref[...], ain_ref[...],
                              al_ref[...], dtb_ref[...], nw_ref[...])
        s_scr[...] = s_new
        for hh in range(HB):
            o_ref[:, hh * HD:(hh + 1) * HD] = o[hh].astype(o_ref.dtype)

    ch = lambda off: pl.BlockSpec((CHUNK, W), lambda h, n: (n, h + off))
    sc = pl.BlockSpec((HB, CHUNK, 1), lambda h, n: (h, n, 0))
    hs = pl.BlockSpec((HB, 1, 1), lambda h, n: (h, 0, 0))
    return pl.pallas_call(
        body, grid=(NH // HB, N),
        in_specs=[ch(0), ch(nb), ch(2 * nb), ch(zc0), sc, sc, hs, hs, pl.BlockSpec((1, HD), lambda h, n: (0, 0))],
        out_specs=[pl.BlockSpec((CHUNK, W), lambda h, n: (n, h)),
                   pl.BlockSpec((HB, None, HD, HD), lambda h, n: (h, n, 0, 0))],
        out_shape=[SDS((T, GW), _MXU), SDS((NH, N, HD, HD), f32)],
        scratch_shapes=[pltpu.VMEM((HB, HD, HD), f32)], compiler_params=_cp(("parallel", "arbitrary")), name=name,
    )(qkv, qkv, qkv, p, blt, aint, alog, dtb, nw)


def _gdn_bwd(name, qkv, p, zcol0, blt, aint, alog, dtb, nw, ssave, do, NH, HD):
    T = qkv.shape[0]
    N = T // CHUNK
    GW = NH * HD
    HB = _heads_per_step(NH, HD, zcol0)
    W = HB * HD
    zc0 = zcol0 // W
    nb = GW // W

    def body(q_ref, k_ref, v_ref, z_ref, bl_ref, ain_ref, al_ref, dtb_ref, nw_ref, ss_ref, do_ref,
             dqkv_ref, dz_ref, dbl_ref, dain_ref, dal_ref, ddtb_ref, dnw_ref, ds_scr):
        h = pl.program_id(0)
        n = pl.program_id(1)

        @pl.when(n == 0)
        def _():
            ds_scr[...] = jnp.zeros_like(ds_scr)

        heads = lambda r: jnp.stack([r[:, hh * HD:(hh + 1) * HD] for hh in range(HB)], axis=0)
        _, vjp = jax.vjp(_gdn_chunk, ss_ref[...], heads(q_ref), heads(k_ref), heads(v_ref), heads(z_ref), bl_ref[...],
                         ain_ref[...], al_ref[...], dtb_ref[...], nw_ref[...])
        ds, dq, dk, dv, dz, dbl, dain, dal, ddtb, dnw = vjp((ds_scr[...], heads(do_ref).astype(f32)))
        ds_scr[...] = ds
        for hh in range(HB):
            cs = slice(hh * HD, (hh + 1) * HD)
            dqkv_ref[0, :, cs] = dq[hh]
            dqkv_ref[1, :, cs] = dk[hh]
            dqkv_ref[2, :, cs] = dv[hh]
            dz_ref[:, cs] = dz[hh].astype(dz_ref.dtype)
        dbl_ref[...] = dbl
        dain_ref[...] = dain

        @pl.when(n == 0)
        def _():
            dal_ref[...] = dal
            ddtb_ref[...] = ddtb

        @pl.when(n > 0)
        def _():
            dal_ref[...] += dal
            ddtb_ref[...] += ddtb

        @pl.when((n == 0) & (h == 0))
        def _():
            dnw_ref[...] = dnw

        @pl.when((n > 0) | (h > 0))
        def _():
            dnw_ref[...] += dnw

    R = N - 1
    ch = lambda off: pl.BlockSpec((CHUNK, W), lambda h, n: (R - n, h + off))
    sc = pl.BlockSpec((HB, CHUNK, 1), lambda h, n: (h, R - n, 0))
    hs = pl.BlockSpec((HB, 1, 1), lambda h, n: (h, 0, 0))
    nws = pl.BlockSpec((1, HD), lambda h, n: (0, 0))
    return pl.pallas_call(
        body, grid=(NH // HB, N),
        in_specs=[ch(0), ch(nb), ch(2 * nb), ch(zc0), sc, sc, hs, hs, nws,
                  pl.BlockSpec((HB, None, HD, HD), lambda h, n: (h, R - n, 0, 0)),
                  pl.BlockSpec((CHUNK, W), lambda h, n: (R - n, h))],
        out_specs=[pl.BlockSpec((3, CHUNK, W), lambda h, n: (0, R - n, h)),
                   pl.BlockSpec((CHUNK, W), lambda h, n: (R - n, h)), sc, sc, hs, hs, nws],
        out_shape=[SDS((3, T, GW), f32), SDS((T, GW), _MXU), SDS((NH, T, 1), f32), SDS((NH, T, 1), f32),
                   SDS((NH, 1, 1), f32), SDS((NH, 1, 1), f32), SDS((1, HD), f32)],
        scratch_shapes=[pltpu.VMEM((HB, HD, HD), f32)], compiler_params=_cp(("arbitrary", "arbitrary")), name=name,
    )(qkv, qkv, qkv, p, blt, aint, alog, dtb, nw, ssave, do)


def _loss_head(name, y, tgt):
    T, D = y.shape
    tm = _rtile(T, 256)

    def body(y_ref, t_ref, dy_ref, l_ref):
        i = pl.program_id(0)
        err = y_ref[...] - t_ref[...]
        dy_ref[...] = err * (1.0 / D)
        part = 0.5 * jnp.sum(jnp.sum(err * err, axis=-1, keepdims=True) * (1.0 / D), axis=0, keepdims=True)

        @pl.when(i == 0)
        def _():
            l_ref[...] = part

        @pl.when(i > 0)
        def _():
            l_ref[...] += part

    row = pl.BlockSpec((tm, D), lambda i: (i, 0))
    return pl.pallas_call(
        body, grid=(T // tm,), in_specs=[row, row], out_specs=[row, pl.BlockSpec((1, 1), lambda i: (0, 0))],
        out_shape=[SDS((T, D), f32), SDS((1, 1), f32)], compiler_params=_cp(("arbitrary",)), name=name,
    )(y, tgt)


def _adam_math(w, g, m, v):
    m = ADAM_B1 * m + (1.0 - ADAM_B1) * g
    v = ADAM_B2 * v + (1.0 - ADAM_B2) * jnp.square(g)
    m_hat = m / (1.0 - ADAM_B1 ** ADAM_STEP)
    v_hat = v / (1.0 - ADAM_B2 ** ADAM_STEP)
    delta = -ADAM_LR * (m_hat / (jnp.sqrt(v_hat) + ADAM_EPS) + ADAM_WD * w)
    return delta, m, v


def _add_mine(name, full, recv, me, out_dtype):
    N, _, R, C = full.shape
    tr = _rtile(R, max(16, (1 << 19) // max(C, 1) // 16 * 16))

    def body(me_ref, a_ref, b_ref, o_ref):
        o_ref[...] = (a_ref[...].astype(f32) + b_ref[...].astype(f32)).astype(o_ref.dtype)

    blk = pl.BlockSpec((None, tr, C), lambda n, i, me_ref: (n, i, 0))
    return pl.pallas_call(
        body,
        grid_spec=pltpu.PrefetchScalarGridSpec(
            num_scalar_prefetch=1, grid=(N, R // tr),
            in_specs=[pl.BlockSpec((None, None, tr, C), lambda n, i, me_ref: (n, me_ref[0], i, 0)), blk], out_specs=blk),
        out_shape=SDS((N, R, C), out_dtype), compiler_params=_cp(("parallel", "parallel")), name=name,
    )(me, full, recv)


def _adamw_big(name, full, recv, me, w, m, v, l, accs):
    _, R, C = full.shape
    L = w.shape[0]
    tr = _rtile(R, max(16, (1 << 18) // max(C, 1) // 16 * 16))

    def body(me_ref, ga_ref, gb_ref, w_ref, m_ref, v_ref, a0, a1, a2, a3, g_ref, d_ref, nm_ref, nv_ref):
        g = ga_ref[...].astype(f32) + gb_ref[...].astype(f32)
        d, nm, nv = _adam_math(w_ref[...], g, m_ref[...], v_ref[...])
        g_ref[...] = g
        d_ref[...] = d
        nm_ref[...] = nm
        nv_ref[...] = nv

    blk = pl.BlockSpec((tr, C), lambda i, me_ref: (i, 0))
    lblk = pl.BlockSpec((None, tr, C), lambda i, me_ref: (l, i, 0))
    untouched = pl.BlockSpec(memory_space=pl.ANY)
    return pl.pallas_call(
        body,
        grid_spec=pltpu.PrefetchScalarGridSpec(
            num_scalar_prefetch=1, grid=(R // tr,),
            in_specs=[pl.BlockSpec((None, tr, C), lambda i, me_ref: (me_ref[0], i, 0)), blk, lblk, lblk, lblk] + [untouched] * 4,
            out_specs=[lblk] * 4),
        out_shape=[SDS((L, R, C), f32)] * 4, input_output_aliases={6: 0, 7: 1, 8: 2, 9: 3},
        compiler_params=_cp(("parallel",)), name=name,
    )(me, full, recv, w, m, v, *accs)


def _adamw_small(name, gall, w, m, v):
    _, R, C = gall.shape
    tr = _rtile(R, 512)

    def body(ga_ref, w_ref, m_ref, v_ref, g_ref, d_ref, nm_ref, nv_ref):
        g = ga_ref[0]
        for s in range(1, 8):
            g = g + ga_ref[s]
        d, nm, nv = _adam_math(w_ref[...], g, m_ref[...], v_ref[...])
        g_ref[...] = g
        d_ref[...] = d
        nm_ref[...] = nm
        nv_ref[...] = nv

    blk = pl.BlockSpec((tr, C), lambda i: (i, 0))
    return pl.pallas_call(
        body, grid=(R // tr,), in_specs=[pl.BlockSpec((8, tr, C), lambda i: (0, i, 0)), blk, blk, blk], out_specs=[blk] * 4,
        out_shape=[SDS((R, C), f32)] * 4, compiler_params=_cp(("parallel",)), name=name,
    )(gall, w, m, v)


def _peer(axis):
    x, y, c = lax.axis_index("x"), lax.axis_index("y"), lax.axis_index("c")
    me = {"x": x, "y": y, "c": c}[axis]
    peer = {"x": (1 - x, y, c), "y": (x, 1 - y, c), "c": (x, y, 1 - c)}[axis]
    return me, peer


def _held(ref, done):
    idx = tuple(slice(None) if a in done else lax.axis_index(a) for a in ("x", "y", "c"))
    return ref.at[idx]


def _gather_stage(name, bufs, axes, dones):
    n = len(bufs)
    hbm = pl.BlockSpec(memory_space=pltpu.HBM)

    def body(*refs):
        outs = refs[n:2 * n]
        send_sems, recv_sems = refs[2 * n:]
        cps = []
        for t in range(n):
            _, peer = _peer(axes[t])
            blk = _held(outs[t], dones[t])
            cps.append(pltpu.make_async_remote_copy(src_ref=blk, dst_ref=blk, send_sem=send_sems.at[t],
                                                    recv_sem=recv_sems.at[t], device_id=peer, device_id_type=_MESH_T))
        for cp in cps:
            cp.start()
        for cp in cps:
            cp.wait()

    return pl.pallas_call(
        body, in_specs=[hbm] * n, out_specs=[hbm] * n, out_shape=[SDS(b.shape, b.dtype) for b in bufs],
        input_output_aliases={t: t for t in range(n)},
        scratch_shapes=[pltpu.SemaphoreType.DMA((n,)), pltpu.SemaphoreType.DMA((n,))], name=name,
    )(*bufs)


_HBM = pl.BlockSpec(memory_space=pltpu.HBM)
_SEM = pl.BlockSpec(memory_space=pltpu.SEMAPHORE)
_EFFECT = pltpu.SideEffectType.DATAFLOW_SIDE_EFFECTING


def _split_start(name, arrays, n_copies, make_copies):
    na = len(arrays)

    def body(*refs):
        ins = refs[:na]
        send_sems, recv_sems = refs[na], refs[na + 1]
        token = refs[2 * na + 2]
        for cp in make_copies(ins, send_sems, recv_sems):
            cp.start()
        token[...] = jnp.zeros_like(token)

    res = pl.pallas_call(
        body, name=name,
        out_shape=(pltpu.SemaphoreType.DMA((n_copies,)), pltpu.SemaphoreType.DMA((n_copies,)),
                   *[pltpu.HBM(a.shape, a.dtype) for a in arrays], SDS((8, 128), f32)),
        in_specs=[_HBM] * na, out_specs=(_SEM, _SEM, *[_HBM] * na, pl.BlockSpec(memory_space=pltpu.VMEM)),
        input_output_aliases={i: 2 + i for i in range(na)},
        compiler_params=pltpu.CompilerParams(has_side_effects=_EFFECT),
    )(*[pltpu.with_memory_space_constraint(a, pltpu.HBM) for a in arrays])
    return res[0], res[1], list(res[2:2 + na]), res[2 + na]


def _split_wait(name, arrays, send_sems, recv_sems, after, make_copies):
    na = len(arrays)
    afters = list(after) if isinstance(after, (list, tuple)) else [after]

    def body(*refs):
        ins = refs[:na]
        for cp in make_copies(ins, refs[na], refs[na + 1]):
            cp.wait_send()
            cp.wait_recv()

    res = pl.pallas_call(
        body, name=name, out_shape=tuple(pltpu.HBM(a.shape, a.dtype) for a in arrays),
        in_specs=[_HBM] * na + [_SEM, _SEM] + [pl.BlockSpec(memory_space=pl.ANY)] * len(afters),
        out_specs=tuple([_HBM] * na), input_output_aliases={i: i for i in range(na)},
        compiler_params=pltpu.CompilerParams(has_side_effects=_EFFECT),
    )(*arrays, send_sems, recv_sems, *afters)
    return list(res)


def _gather_copies(axes, dones):
    def make(refs, send_sems, recv_sems):
        cps = []
        for t in range(len(axes)):
            _, peer = _peer(axes[t])
            blk = _held(refs[t], dones[t])
            cps.append(pltpu.make_async_remote_copy(src_ref=blk, dst_ref=blk, send_sem=send_sems.at[t],
                                                    recv_sem=recv_sems.at[t], device_id=peer, device_id_type=_MESH_T))
        return cps
    return make


def _scatter_copies(axes):
    n = len(axes)

    def make(refs, send_sems, recv_sems):
        cps = []
        for t in range(n):
            me, peer = _peer(axes[t])
            cps.append(pltpu.make_async_remote_copy(
                src_ref=refs[t].at[:, 1 - me], dst_ref=refs[n + t], send_sem=send_sems.at[t], recv_sem=recv_sems.at[t],
                device_id=peer, device_id_type=_MESH_T))
        return cps
    return make


class _AsyncGather:
    def __init__(self, pfx, tensors, paths):
        x, y, c = (lax.axis_index(a) for a in ("x", "y", "c"))
        self.pfx, self.shapes = pfx, [tuple(t.shape) for t in tensors]
        self.bufs = [lax.dynamic_update_slice(lax.empty((2, 2, 2) + tuple(t.shape), t.dtype), t[None, None, None],
                                              (x, y, c) + (0,) * t.ndim) for t in tensors]
        self.orders = [tuple(p) + ("c",) for p in paths]
        self.ph = 0

    def _make(self):
        return _gather_copies([o[self.ph] for o in self.orders], [o[:self.ph] for o in self.orders])

    def start(self):
        self.ss, self.rs, self.bufs, tok = _split_start(f"{self.pfx}_start{self.ph}", self.bufs, len(self.bufs), self._make())
        return tok

    def wait(self, after):
        self.bufs = _split_wait(f"{self.pfx}_wait{self.ph}", self.bufs, self.ss, self.rs, after, self._make())
        self.ph += 1

    def result(self):
        return [b.reshape((8,) + s) for b, s in zip(self.bufs, self.shapes)]


class _AsyncReduceScatter:
    def __init__(self, pfx, tensors, paths):
        self.pfx = pfx
        self.rcs = [tuple(t.shape[1:]) for t in tensors]
        self.orders = [("c",) + tuple(p) for p in paths]
        self.left = [["x", "y", "c"] for _ in tensors]
        self.cur = list(tensors)
        self.ph = 0

    def start(self):
        n = len(self.cur)
        views = []
        for i, (t, rc) in enumerate(zip(self.cur, self.rcs)):
            pos = self.left[i].index(self.orders[i][self.ph])
            nb, na = 2 ** pos, 2 ** (len(self.left[i]) - pos - 1)
            views.append(t.reshape((nb, 2, na * rc[0], rc[1])))
        lands = [lax.empty((v.shape[0],) + tuple(v.shape[2:]), v.dtype) for v in views]
        self.make = _scatter_copies([o[self.ph] for o in self.orders])
        self.ss, self.rs, arrs, tok = _split_start(f"{self.pfx}_start{self.ph}", views + lands, n, self.make)
        self.arrs = arrs
        return tok

    def wait(self, after):
        n = len(self.cur)
        arrs = _split_wait(f"{self.pfx}_wait{self.ph}", self.arrs, self.ss, self.rs, after, self.make)
        views, recvs = arrs[:n], arrs[n:]
        ph = self.ph
        if ph == 2:
            self.out = [(v[0], r[0], o[2]) for v, r, o in zip(views, recvs, self.orders)]
        else:
            self.cur = [_add_mine(f"{self.pfx}_add{ph}_{i}", v, r, _coord(o[ph]), v.dtype)
                        for i, (v, r, o) in enumerate(zip(views, recvs, self.orders))]
            for i, o in enumerate(self.orders):
                self.left[i].remove(o[ph])
        self.ph += 1


def _coord(axis):
    return lax.axis_index(axis).astype(jnp.int32).reshape(1)


def _all_gather(pfx, tensors, paths):
    x, y, c = (lax.axis_index(a) for a in ("x", "y", "c"))
    bufs = []
    for t in tensors:
        zero = (0,) * t.ndim
        bufs.append(lax.dynamic_update_slice(lax.empty((2, 2, 2) + tuple(t.shape), t.dtype), t[None, None, None],
                                             (x, y, c) + zero))
    orders = [tuple(p) + ("c",) for p in paths]
    for ph in range(3):
        bufs = _gather_stage(f"{pfx}_{ph}", bufs, [o[ph] for o in orders], [o[:ph] for o in orders])
    return [b.reshape((8,) + tuple(t.shape)) for b, t in zip(bufs, tensors)]


def _mm_nn(name, a, w, out_dtype, tn_pref=1024):
    T, K = a.shape
    N = w.shape[1]
    tm, tn, tk = _rtile(T, 512), _tile(N, tn_pref), _tile(K, 2048)
    return _mm(
        name, a, w, _NN, (T // tm, N // tn, K // tk),
        pl.BlockSpec((tm, tk), lambda i, j, k: (i, k)), pl.BlockSpec((tk, tn), lambda i, j, k: (k, j)), (tm, tn),
        [], [], [SDS((T, N), out_dtype)], [pl.BlockSpec((tm, tn), lambda i, j, k: (i, j))], _store_epi,
    )[0]


def _mm_tn(name, a, b, out_dtype):
    T, M = a.shape
    N = b.shape[1]
    tm, tn, tk = _tile(M, 512), _tile(N, 2048), _rtile(T, 512)
    return _mm(
        name, a, b, _TN, (M // tm, N // tn, T // tk),
        pl.BlockSpec((tk, tm), lambda i, j, k: (k, i)), pl.BlockSpec((tk, tn), lambda i, j, k: (k, j)), (tm, tn),
        [], [], [SDS((M, N), out_dtype)], [pl.BlockSpec((tm, tn), lambda i, j, k: (i, j))], _store_epi,
    )[0]


def _mm_tn_slots(name, a, b, out_dtype):
    T, M = a.shape
    NS = b.shape[1] // 8
    tm, tk = _tile(M, 512), _rtile(T, 512)
    return _mm(
        name, a, b, _TN, (M // tm, 8, T // tk),
        pl.BlockSpec((tk, tm), lambda i, j, k: (k, i)), pl.BlockSpec((tk, NS), lambda i, j, k: (k, j)), (tm, NS),
        [], [], [SDS((8, M, NS), out_dtype)], [pl.BlockSpec((None, tm, NS), lambda i, j, k: (j, i, 0))], _store_epi,
    )[0]


def _mm_nt_slots(name, a, w8, out_dtype):
    T = a.shape[0]
    _, M, NS = w8.shape
    tm, tn = _rtile(T, 512), _tile(M, 1024)
    return _mm(
        name, a, w8, _NT, (T // tm, M // tn, 8),
        pl.BlockSpec((tm, NS), lambda i, j, k: (i, k)), pl.BlockSpec((None, tn, NS), lambda i, j, k: (k, j, 0)), (tm, tn),
        [], [], [SDS((T, M), out_dtype)], [pl.BlockSpec((tm, tn), lambda i, j, k: (i, j))], _store_epi,
    )[0]


def _colsum_kernel(name, fn, ins, in_cols, outs_elem, n_sum, C):
    T = ins[0].shape[0]
    tm = _rtile(T, 256)
    ne = len(outs_elem)

    def body(*refs):
        i = pl.program_id(0)
        iv = [r[...] for r in refs[:len(ins)]]
        res = fn(*iv)
        for o, r in zip(refs[len(ins):len(ins) + ne], res[:ne]):
            o[...] = r.astype(o.dtype)
        sums = [jnp.sum(r, axis=0, keepdims=True) for r in res[ne:]]

        @pl.when(i == 0)
        def _():
            for o, s in zip(refs[len(ins) + ne:], sums):
                o[...] = s

        @pl.when(i > 0)
        def _():
            for o, s in zip(refs[len(ins) + ne:], sums):
                o[...] += s

    in_specs = []
    for arr, off in zip(ins, in_cols):
        if off is None:
            in_specs.append(pl.BlockSpec((1, C), lambda i: (0, 0)))
        else:
            in_specs.append(pl.BlockSpec((tm, C), lambda i, off=off: (i, off)))
    row = pl.BlockSpec((tm, C), lambda i: (i, 0))
    vec = pl.BlockSpec((1, C), lambda i: (0, 0))
    return pl.pallas_call(
        body, grid=(T // tm,), in_specs=in_specs, out_specs=[row] * ne + [vec] * n_sum,
        out_shape=[SDS((T, C), dt) for dt in outs_elem] + [SDS((1, C), f32)] * n_sum,
        compiler_params=_cp(("arbitrary",)), name=name,
    )(*ins)


def _merge(gs, gg, a_s, a_g):
    return jax.nn.sigmoid(gs) * a_s + jax.nn.sigmoid(gg) * a_g


def _glu(yg, lp):
    return yg * jax.nn.sigmoid(lp)


_BIG = ("ffn1_w_gu", "ffn1_w_down", "w_in", "conv_w", "glu_w", "w_br_ssm", "w_br_gdn", "w_out", "ffn2_w_gu", "ffn2_w_down")
_PATHS = ("yx", "yx", "xy", "xy", "yx", "yx", "yx", "yx", "xy", "xy")
_SMALL = ("ln1_g", "ln1_b", "ssm_a_re", "ssm_a_im", "ssm_log_dt", "ssm_b_re", "ssm_b_im", "ssm_c_re", "ssm_c_im", "ssm_d",
          "glu_b", "gdn_a_log", "gdn_dt_bias", "gdn_norm_w", "ln2_g", "ln2_b", "ln3_g", "ln3_b")
_ORDER = ("ffn1_w_gu", "ffn1_w_down", "ln1_g", "ln1_b", "w_in", "conv_w", "ssm_a_re", "ssm_a_im", "ssm_log_dt", "ssm_b_re",
          "ssm_b_im", "ssm_c_re", "ssm_c_im", "ssm_d", "glu_w", "glu_b", "gdn_a_log", "gdn_dt_bias", "gdn_norm_w", "w_br_ssm",
          "w_br_gdn", "w_out", "ln2_g", "ln2_b", "ffn2_w_gu", "ffn2_w_down", "ln3_g", "ln3_b")


def _step(x, tgt, W, M, V):
    T, D = x.shape[1], x.shape[2]
    L = W["ffn1_w_gu"].shape[0]
    G, P = W["ssm_a_re"].shape[1:]
    H = W["ssm_b_re"].shape[3]
    SW = G * H
    NH = W["gdn_a_log"].shape[1]
    HD = W["gdn_norm_w"].shape[1]
    GW = NH * HD
    KC = W["conv_w"].shape[1]
    DS = D // 8
    alpha = (2.0 * L) ** 0.25
    o_b = SW + 4 * GW
    o_gs = o_b + 2 * NH
    IN = o_gs + 2 * D
    NM = IN - 2 * NH
    m_qkv, m_z, m_gs, m_gg = SW, SW + 3 * GW, SW + 4 * GW, SW + 4 * GW + D
    J = G // 8

    x0 = x[0]
    tg = tgt[0]

    def vec(name, l):
        return W[name][l:l + 1]

    saves, weights = [], []
    xc, xcb = x0, x0.astype(_MXU)
    def shards(l):
        return [W["ffn1_w_gu"][l].astype(_MXU), W["ffn1_w_down"][l].astype(_MXU), W["w_in"][l].astype(_MXU), W["conv_w"][l],
                W["glu_w"][l].astype(_MXU), W["w_br_ssm"][l].astype(_MXU), W["w_br_gdn"][l].astype(_MXU),
                W["w_out"][l].astype(_MXU), W["ffn2_w_gu"][l].astype(_MXU), W["ffn2_w_down"][l].astype(_MXU)]

    def dep(a, tok):
        return a if tok is None else a + tok[0:1, 0:1].astype(a.dtype)

    sh0 = shards(0)
    gathered = _all_gather("ag", sh0[:2], _PATHS[:2])
    rest0 = _AsyncGather("agq", sh0[2:], _PATHS[2:])
    rest0.start()
    ahead = {}
    for l in range(L):
        toks = []
        nxt = ahead.get(l + 1)
        if nxt and l >= 1:
            nxt.wait(xc)
            toks.append(nxt.start())
        for k in ([1, 2] if l == 0 else [l + 2]):
            if k < L:
                ahead[k] = _AsyncGather("agp", shards(k), _PATHS)
                toks.append(ahead[k].start())
        nxt = ahead.get(l + 1)
        tok = functools.reduce(lambda a, b: a + b, toks) if toks else None
        wgu1, wd1 = gathered[0], gathered[1].reshape(-1, D)
        sv = {}

        gate, up, hh = _ffn_up("ffn_up", xcb, wgu1)
        if l == 0:
            rest0.wait(hh)
            rest0.start()
        x1, x1b, xh1, r1 = _mm_ln("ffn_down_ln", hh, wd1, xc, dep(vec("ln1_g", l), tok), vec("ln1_b", l), alpha, 0.5)
        sv["f1"] = dict(xb=xcb, gate=gate, up=up, h=hh, xhat=xh1, rstd=r1)
        if l == 0:
            rest0.wait(x1)
            rest0.start()
            rest0.wait(x1b)
            gathered = gathered + rest0.result()
        _, _, win8, cw8, wglu, wbs, wbg, wo, wgu2, wd2 = gathered
        wd2 = wd2.reshape(-1, D)
        wglu = wglu.reshape(SW, SW)
        wo = wo.reshape(D, D)
        win = jnp.transpose(win8, (1, 0, 2)).reshape(D, IN)
        wmain = jnp.concatenate([win[:, :o_b], win[:, o_gs:]], axis=1)
        wba = jnp.pad(win[:, o_b:o_gs], ((0, 0), (0, 128 - 2 * NH)))
        cw = jnp.transpose(cw8, (1, 0, 2)).reshape(KC, 3 * GW)
        wl = dict(wgu1=wgu1, wd1=wd1, wmain=wmain, wba=wba, cw=cw, wglu=wglu, wbs=wbs, wbg=wbg, wo=wo, wgu2=wgu2, wd2=wd2)
        weights.append(wl)

        p = _mm_nn("mix_in", x1b, wmain, f32)
        pba = _mm_nn("mix_in_ba", x1b, wba, f32)
        b_re_t = jnp.transpose(W["ssm_b_re"][l], (2, 0, 1))
        b_im_t = jnp.transpose(W["ssm_b_im"][l], (2, 0, 1))
        zoh_in = (W["ssm_a_re"][l], W["ssm_a_im"][l], W["ssm_log_dt"][l][:, None], b_re_t, b_im_t)
        lbr, lbi, bbr_t, bbi_t = _zoh_fwd("zoh", *zoh_in)
        bblk_r = _blockdiag(jnp.transpose(bbr_t, (1, 0, 2)))
        bblk_i = _blockdiag(jnp.transpose(bbi_t, (1, 0, 2)))
        cblkT_r = _blockdiag(W["ssm_c_re"][l])
        cblkT_in = _blockdiag(-W["ssm_c_im"][l])
        lbr_f, lbi_f = lbr.reshape(1, G * P), lbi.reshape(1, G * P)
        bur, bui = _bd2("s5_bu", p, 0, bblk_r, bblk_i)
        if nxt and l == 0:
            nxt.wait(bur)
            tok = nxt.start()
        sr, si = _s5_scan("s5_scan", bur, bui, dep(lbr_f, tok), lbi_f)
        dflat = W["ssm_d"][l].reshape(1, SW)

        def out_epi(acc, ex, outs):
            y_raw = acc + ex[1][...] * ex[0][...]
            yg = jax.nn.gelu(y_raw)
            outs[0][...] = y_raw
            outs[1][...] = yg
            outs[2][...] = yg.astype(outs[2].dtype)

        y_raw, yg, ygb = _bd_sum(
            "s5_out", sr, si, jnp.transpose(cblkT_r, (0, 2, 1)), jnp.transpose(cblkT_in, (0, 2, 1)), [p, dflat],
            lambda tm, nb: [pl.BlockSpec((tm, nb), lambda i, j: (i, j)), pl.BlockSpec((1, nb), lambda i, j: (0, j))],
            [SDS((T, SW), f32), SDS((T, SW), f32), SDS((T, SW), _MXU)], out_epi)

        tmg, tng, tkg = _rtile(T, 512), _tile(SW, 512), _tile(SW, 1024)

        def glu_epi(acc, ex, outs):
            lp = acc + ex[1][...]
            outs[0][...] = lp
            outs[1][...] = _glu(ex[0][...], lp).astype(outs[1].dtype)

        lp, ysb = _mm(
            "s5_glu", ygb, wglu, _NN, (T // tmg, SW // tng, SW // tkg),
            pl.BlockSpec((tmg, tkg), lambda i, j, k: (i, k)), pl.BlockSpec((tkg, tng), lambda i, j, k: (k, j)), (tmg, tng),
            [yg, vec("glu_b", l)], [pl.BlockSpec((tmg, tng), lambda i, j, k: (i, j)), pl.BlockSpec((1, tng), lambda i, j, k: (0, j))],
            [SDS((T, SW), f32), SDS((T, SW), _MXU)], [pl.BlockSpec((tmg, tng), lambda i, j, k: (i, j))] * 2, glu_epi)

        qkv = _conv_fwd("gdn_conv", p, m_qkv, cw, 3 * GW)
        blt = jnp.transpose(pba[:, :NH])[:, :, None]
        aint = jnp.transpose(pba[:, NH:2 * NH])[:, :, None]
        alog = W["gdn_a_log"][l].reshape(NH, 1, 1)
        dtb = W["gdn_dt_bias"][l].reshape(NH, 1, 1)
        nw = vec("gdn_norm_w", l)
        og, ssave = _gdn_fwd("gdn", qkv, p, m_z, blt, aint, alog, dtb, nw, NH, HD)

        a_s = _mm(
            "br_ssm", ysb, wbs, _NN, (T // tmg, 8, SW // tkg),
            pl.BlockSpec((tmg, tkg), lambda i, j, k: (i, k)), pl.BlockSpec((None, tkg, DS), lambda i, j, k: (j, k, 0)), (tmg, DS),
            [], [], [SDS((T, D), f32)], [pl.BlockSpec((tmg, DS), lambda i, j, k: (i, j))], _store_epi)[0]
        tkd = _tile(GW, 1024)
        gsb, ggb = m_gs // DS, m_gg // DS

        def merge_epi(acc, ex, outs):
            outs[0][...] = acc
            outs[1][...] = _merge(ex[1][...], ex[2][...], ex[0][...], acc).astype(outs[1].dtype)

        tile_ij = pl.BlockSpec((tmg, DS), lambda i, j, k: (i, j))
        a_g, merged = _mm(
            "br_gdn_merge", og, wbg, _NN, (T // tmg, 8, GW // tkd),
            pl.BlockSpec((tmg, tkd), lambda i, j, k: (i, k)), pl.BlockSpec((None, tkd, DS), lambda i, j, k: (j, k, 0)), (tmg, DS),
            [a_s, p, p], [tile_ij, pl.BlockSpec((tmg, DS), lambda i, j, k: (i, j + gsb)),
                          pl.BlockSpec((tmg, DS), lambda i, j, k: (i, j + ggb))],
            [SDS((T, D), f32), SDS((T, D), _MXU)], [tile_ij, tile_ij], merge_epi)
        if nxt:
            nxt.wait(merged)
            tok = nxt.start()
        x2, x2b, xh2, r2 = _mm_ln("mix_out_ln", merged, wo, x1, dep(vec("ln2_g", l), tok), vec("ln2_b", l), alpha, 1.0)
        sv["mx"] = dict(x1b=x1b, p=p, zoh_in=zoh_in, lbr_f=lbr_f, lbi_f=lbi_f, bblk_r=bblk_r, bblk_i=bblk_i, cblkT_r=cblkT_r,
                        cblkT_in=cblkT_in, sr=sr, si=si, dflat=dflat, y_raw=y_raw, yg=yg, ygb=ygb, lp=lp, ysb=ysb, qkv=qkv,
                        blt=blt, aint=aint, alog=alog, dtb=dtb, nw=nw, og=og, ssave=ssave, a_s=a_s, a_g=a_g, merged=merged,
                        xhat=xh2, rstd=r2)

        gate2, up2, hh2 = _ffn_up("ffn_up", x2b, wgu2)
        x3, x3b, xh3, r3 = _mm_ln("ffn_down_ln", hh2, wd2, x2, vec("ln3_g", l), vec("ln3_b", l), alpha, 0.5)
        sv["f2"] = dict(xb=x2b, gate=gate2, up=up2, h=hh2, xhat=xh3, rstd=r3)
        saves.append(sv)
        xc, xcb = x3, x3b
        if nxt:
            nxt.wait(x3)
            gathered = nxt.result()

    dy, loss_part = _loss_head("loss_head", xc, tg)
    loss = lax.psum(loss_part[0, 0], ("x", "y", "c"))

    big_out = {n: [lax.empty(W[n].shape, f32) for _ in range(4)] for n in _BIG}
    small_g = {n: [None] * L for n in _SMALL}
    pend = None
    for l in reversed(range(L)):
        sv, wl = saves[l], weights[l]
        mx = sv["mx"]
        p = mx["p"]
        tok = pend.start() if pend else None
        dx2, dwgu2, dwd2, dg3, db3 = _ffn_bwd("ffn_b", dy, sv["f2"], wl["wgu2"], wl["wd2"], dep(vec("ln3_g", l), tok), alpha)
        small_g["ln3_g"][l], small_g["ln3_b"][l] = dg3[0], db3[0]

        if pend:
            pend.wait(dx2)
            tok = pend.start()
        dz2, dmixb, dg2, db2 = _ln_bwd("mix_lnb", dx2, mx["xhat"], mx["rstd"], dep(vec("ln2_g", l), tok), 1.0)
        small_g["ln2_g"][l], small_g["ln2_b"][l] = dg2[0], db2[0]
        tmg, tkd = _rtile(T, 512), _tile(D, 512)
        tnq = 512 if (m_gs % 512 == 0 and D % 512 == 0) else DS
        tkq = _tile(D, 2048)
        gsb, ggb = m_gs // tnq, m_gg // tnq

        def dmerge_epi(acc, ex, outs):
            _, vjp = jax.vjp(_merge, ex[0][...], ex[1][...], ex[2][...], ex[3][...])
            dgs, dgg, das, dag = vjp(acc)
            outs[0][...] = das.astype(outs[0].dtype)
            outs[1][...] = dag.astype(outs[1].dtype)
            outs[2][...] = dgs.astype(outs[2].dtype)
            outs[3][...] = dgg.astype(outs[3].dtype)

        tile_ij = pl.BlockSpec((tmg, tnq), lambda i, j, k: (i, j))
        das, dag, dgs, dgg = _mm(
            "mix_dmerge", dmixb, wl["wo"], _NT, (T // tmg, D // tnq, D // tkq),
            pl.BlockSpec((tmg, tkq), lambda i, j, k: (i, k)), pl.BlockSpec((tnq, tkq), lambda i, j, k: (j, k)), (tmg, tnq),
            [p, p, mx["a_s"], mx["a_g"]],
            [pl.BlockSpec((tmg, tnq), lambda i, j, k: (i, j + gsb)), pl.BlockSpec((tmg, tnq), lambda i, j, k: (i, j + ggb)),
             tile_ij, tile_ij],
            [SDS((T, D), _MXU)] * 4, [tile_ij] * 4, dmerge_epi)
        dwo = _mm_tn("mix_dwo", mx["merged"], dmixb, _GDT)
        dys = _mm_nt_slots("br_ssm_dx", das, wl["wbs"], f32)
        dog = _mm_nt_slots("br_gdn_dx", dag, wl["wbg"], f32)
        dwbs = _mm_tn_slots("br_ssm_dw", mx["ysb"], das, _GDT)
        dwbg = _mm_tn_slots("br_gdn_dw", mx["og"], dag, _GDT)

        def glu_b_fn(dys_t, yg_t, lp_t):
            _, vjp = jax.vjp(_glu, yg_t, lp_t)
            dyg1, dlp = vjp(dys_t)
            return dyg1, dlp, dlp

        dyg1, dlpb, dglub = _colsum_kernel("s5_glu_b", glu_b_fn, [dys, mx["yg"], mx["lp"]], [0, 0, 0], [f32, _MXU], 1, SW)
        small_g["glu_b"][l] = dglub[0]
        dwglu = _mm_tn("s5_dwglu", mx["ygb"], dlpb, _GDT)
        tng, tkg = _tile(SW, 512), _tile(SW, 512)

        def dyraw_epi(acc, ex, outs):
            _, vjp = jax.vjp(jax.nn.gelu, ex[1][...])
            (d,) = vjp(ex[0][...] + acc)
            outs[0][...] = d

        t_ij = pl.BlockSpec((tmg, tng), lambda i, j, k: (i, j))
        (dyraw,) = _mm(
            "s5_dyraw", dlpb, wl["wglu"], _NT, (T // tmg, SW // tng, SW // tkg),
            pl.BlockSpec((tmg, tkg), lambda i, j, k: (i, k)), pl.BlockSpec((tng, tkg), lambda i, j, k: (j, k)), (tmg, tng),
            [dyg1, mx["y_raw"]], [t_ij, t_ij], [SDS((T, SW), f32)], [t_ij], dyraw_epi)

        def dd_fn(dyr, u_t, d_t):
            return d_t * dyr, dyr * u_t

        dud, dd = _colsum_kernel("s5_dd", dd_fn, [dyraw, p, mx["dflat"]], [0, 0, None], [f32], 1, SW)
        small_g["ssm_d"][l] = dd.reshape(G, H)
        dsr, dsi = _bd2("s5_ds", dyraw, 0, mx["cblkT_r"], mx["cblkT_in"])
        dcb_r, dcb_i = _bdT2("s5_dc", mx["sr"], mx["si"], 0, dyraw, dyraw, 0, 8 * P, 8 * H, J)
        small_g["ssm_c_re"][l] = _blockdiag_extract(jnp.transpose(dcb_r, (0, 2, 1)), H, P)
        small_g["ssm_c_im"][l] = -_blockdiag_extract(jnp.transpose(dcb_i, (0, 2, 1)), H, P)
        ar, ai, dlr, dli = _s5_scan_bwd("s5_scan_b", dsr, dsi, mx["sr"], mx["si"], mx["lbr_f"], mx["lbi_f"])

        def du_epi(acc, ex, outs):
            outs[0][...] = (acc + ex[0][...]).astype(outs[0].dtype)

        (du,) = _bd_sum(
            "s5_du", ar, ai, jnp.transpose(mx["bblk_r"], (0, 2, 1)), jnp.transpose(mx["bblk_i"], (0, 2, 1)), [dud],
            lambda tm, nb: [pl.BlockSpec((tm, nb), lambda i, j: (i, j))], [SDS((T, SW), _MXU)], du_epi)
        dbb_r, dbb_i = _bdT2("s5_db", p, p, 0, ar, ai, 0, 8 * H, 8 * P, J)
        dbbr_t = jnp.transpose(_blockdiag_extract(dbb_r, H, P), (1, 0, 2))
        dbbi_t = jnp.transpose(_blockdiag_extract(dbb_i, H, P), (1, 0, 2))
        da_re, da_im, dlog_dt, dbre_t, dbim_t = _zoh_bwd("zoh_b", *mx["zoh_in"], dlr.reshape(G, P), dli.reshape(G, P),
                                                         dbbr_t, dbbi_t)
        small_g["ssm_a_re"][l], small_g["ssm_a_im"][l], small_g["ssm_log_dt"][l] = da_re, da_im, dlog_dt[:, 0]
        small_g["ssm_b_re"][l] = jnp.transpose(dbre_t, (1, 2, 0))
        small_g["ssm_b_im"][l] = jnp.transpose(dbim_t, (1, 2, 0))

        dqkv3, dzb, dbl, dain, dal, ddtb, dnw = _gdn_bwd("gdn_b", mx["qkv"], p, m_z, mx["blt"], mx["aint"], mx["alog"],
                                                         mx["dtb"], mx["nw"], mx["ssave"], dog, NH, HD)
        small_g["gdn_a_log"][l], small_g["gdn_dt_bias"][l], small_g["gdn_norm_w"][l] = dal[:, 0, 0], ddtb[:, 0, 0], dnw[0]
        dqkv_pre, dcw = _conv_bwd("gdn_conv_b", p, m_qkv, wl["cw"], dqkv3)
        if pend:
            pend.wait(dqkv_pre)
            tok = pend.start()

        dpm = jnp.concatenate([du, dqkv_pre, dzb, dgs, dgg], axis=1)
        dpba = dep(jnp.concatenate([jnp.transpose(dbl[:, :, 0]), jnp.transpose(dain[:, :, 0]),
                                    jnp.zeros((T, 128 - 2 * NH), f32)], axis=1), tok).astype(_MXU)
        tnd, tkm = _tile(D, 1024), _tile(NM, 2304)
        t_ba = _mm(
            "mix_dx_ba", dpba, wl["wba"], _NT, (T // tmg, D // tnd, 1),
            pl.BlockSpec((tmg, 128), lambda i, j, k: (i, 0)), pl.BlockSpec((tnd, 128), lambda i, j, k: (j, 0)), (tmg, tnd),
            [], [], [SDS((T, D), f32)], [pl.BlockSpec((tmg, tnd), lambda i, j, k: (i, j))], _store_epi)[0]

        def dx1_epi(acc, ex, outs):
            outs[0][...] = alpha * ex[0][...] + ex[1][...] + acc

        t_d = pl.BlockSpec((tmg, tnd), lambda i, j, k: (i, j))
        (dx1,) = _mm(
            "mix_dx", dpm, wl["wmain"], _NT, (T // tmg, D // tnd, NM // tkm),
            pl.BlockSpec((tmg, tkm), lambda i, j, k: (i, k)), pl.BlockSpec((tnd, tkm), lambda i, j, k: (j, k)), (tmg, tnd),
            [dz2, t_ba], [t_d, t_d], [SDS((T, D), f32)], [t_d], dx1_epi)
        tnm = _tile(NM, 1024)
        tkt = _tile(T, 2048)
        dwmain = _mm(
            "mix_dw", jnp.transpose(mx["x1b"]), dpm, _NN, (D // tkd, NM // tnm, T // tkt),
            pl.BlockSpec((tkd, tkt), lambda i, j, k: (i, k)), pl.BlockSpec((tkt, tnm), lambda i, j, k: (k, j)), (tkd, tnm),
            [], [], [SDS((D, NM), _GDT)], [pl.BlockSpec((tkd, tnm), lambda i, j, k: (i, j))], _store_epi)[0]
        dwba = _mm_tn("mix_dw_ba", mx["x1b"], dpba, _GDT)
        dwin = jnp.concatenate([dwmain[:, :o_b], dwba[:, :2 * NH], dwmain[:, o_b:]], axis=1)
        dwin8 = jnp.transpose(dwin.reshape(D, 8, IN // 8), (1, 0, 2))
        dcw8 = jnp.transpose(dcw.reshape(KC, 8, 3 * GW // 8), (1, 0, 2))

        dx0, dwgu1, dwd1, dg1, db1 = _ffn_bwd("ffn_b", dx1, sv["f1"], wl["wgu1"], wl["wd1"], vec("ln1_g", l), alpha)
        small_g["ln1_g"][l], small_g["ln1_b"][l] = dg1[0], db1[0]
        dy = dx0

        parts = [dwgu1, dwd1.reshape(8, -1, D), dwin8, dcw8, dwglu.reshape(8, SW // 8, SW), dwbs, dwbg,
                 dwo.reshape(8, DS, D), dwgu2, dwd2.reshape(8, -1, D)]
        late = None
        if pend:
            pend.wait(dx0)
            if l > 0:
                for n, (full, recv, last) in zip(_BIG, pend.out):
                    big_out[n] = _adamw_big("adamw_" + n, full, recv, _coord(last), W[n], M[n], V[n], l + 1, big_out[n])
            else:
                late = pend.out
        pend = _AsyncReduceScatter("rsp", parts, _PATHS)

    seg = 8 * 128

    def padded(n):
        return -(-n // seg) * seg

    def pack(arrs):
        flat = jnp.concatenate([jnp.pad(a.reshape(-1), (0, padded(a.size) - a.size)) for a in arrs])
        n = flat.shape[0]
        rows = -(-n // (128 * 512)) * 512
        return jnp.pad(flat, (0, rows * 128 - n)).reshape(rows, 128)

    tok = pend.start()
    gs_full = [jnp.stack(small_g[n]).reshape(W[n].shape) for n in _SMALL]
    gpack = dep(pack(gs_full), tok)
    (gall,) = _all_gather("ag_small", [gpack], ["yx"])
    pend.wait(gall)
    tok = pend.start()
    if late:
        for n, (full, recv, last) in zip(_BIG, late):
            big_out[n] = _adamw_big("adamw_" + n, full, dep(recv, tok), _coord(last), W[n], M[n], V[n], 1, big_out[n])
    pend.wait([gall] + ([big_out[n][0] for n in _BIG] if late else []))
    tok = pend.start()
    sg, sd, sm, sv_ = _adamw_small("adamw_small", gall, dep(pack([W[n] for n in _SMALL]), tok),
                                   pack([M[n] for n in _SMALL]), pack([V[n] for n in _SMALL]))
    pend.wait(sv_)
    for n, (full, recv, last) in zip(_BIG, pend.out):
        big_out[n] = _adamw_big("adamw_" + n, full, recv, _coord(last), W[n], M[n], V[n], 0, big_out[n])

    def unpack(packed):
        out, row = {}, 0
        for n in _SMALL:
            sz = math.prod(W[n].shape)
            rows = padded(sz) // 128
            out[n] = packed[row:row + rows].reshape(-1)[:sz].reshape(W[n].shape)
            row += rows
        return out

    res = [unpack(a) for a in (sg, sd, sm, sv_)]
    for n in _BIG:
        for i in range(4):
            res[i][n] = big_out[n][i]
    outs = [loss, dy[None]]
    for i in range(4):
        outs += [res[i][n] for n in _ORDER]
    return tuple(outs)


def kernel(x, ffn1_w_gu, ffn1_w_down, ln1_g, ln1_b, w_in, conv_w, ssm_a_re, ssm_a_im, ssm_log_dt, ssm_b_re, ssm_b_im, ssm_c_re, ssm_c_im, ssm_d, glu_w, glu_b, gdn_a_log, gdn_dt_bias, gdn_norm_w, w_br_ssm, w_br_gdn, w_out, ln2_g, ln2_b, ffn2_w_gu, ffn2_w_down, ln3_g, ln3_b, loss_target, m_ffn1_w_gu, m_ffn1_w_down, m_ln1_g, m_ln1_b, m_w_in, m_conv_w, m_ssm_a_re, m_ssm_a_im, m_ssm_log_dt, m_ssm_b_re, m_ssm_b_im, m_ssm_c_re, m_ssm_c_im, m_ssm_d, m_glu_w, m_glu_b, m_gdn_a_log, m_gdn_dt_bias, m_gdn_norm_w, m_w_br_ssm, m_w_br_gdn, m_w_out, m_ln2_g, m_ln2_b, m_ffn2_w_gu, m_ffn2_w_down, m_ln3_g, m_ln3_b, v_ffn1_w_gu, v_ffn1_w_down, v_ln1_g, v_ln1_b, v_w_in, v_conv_w, v_ssm_a_re, v_ssm_a_im, v_ssm_log_dt, v_ssm_b_re, v_ssm_b_im, v_ssm_c_re, v_ssm_c_im, v_ssm_d, v_glu_w, v_glu_b, v_gdn_a_log, v_gdn_dt_bias, v_gdn_norm_w, v_w_br_ssm, v_w_br_gdn, v_w_out, v_ln2_g, v_ln2_b, v_ffn2_w_gu, v_ffn2_w_down, v_ln3_g, v_ln3_b):
    given = dict(locals())
    W = {n: given[n] for n in _ORDER}
    M = {n: given["m_" + n] for n in _ORDER}
    V = {n: given["v_" + n] for n in _ORDER}
    return _step(x, loss_target, W, M, V)
```
